```python
import jax, jax.numpy as jnp
from jax import lax
import numpy as np

D_MODEL = 2048
BATCH = 8
SEQ = 4096
DEPTH = 1

CONV_WIDTH = D_MODEL // 2
CONV_GROUPS = 8
CONV_GROUP_DIM = CONV_WIDTH // CONV_GROUPS
CONV_KERNEL = 31
HGRN_WIDTH = D_MODEL - CONV_WIDTH
HGRN_HEADS = 8
HGRN_EXPAND = HGRN_WIDTH // HGRN_HEADS
HGRN_HEAD_DIM = HGRN_WIDTH // HGRN_HEADS
HGRN_CHUNK = 64
HGRN_SUB = 8
IN_PROJ_DIM = 2 * CONV_WIDTH + 4 * HGRN_WIDTH
D_FF = 5632
FFN_KERNEL = 3
LN_EPS = 1e-5
RMS_EPS = 1e-6
ALPHA = (2.0 * DEPTH) ** 0.25
BETA = (8.0 * DEPTH) ** -0.25

kernel_name = "hymba_conformer_hgrn2_deepnorm"


def layer_norm(x, g, b):
    xf = x.astype(jnp.float32)
    mu = jnp.mean(xf, axis=-1, keepdims=True)
    var = jnp.mean(jnp.square(xf - mu), axis=-1, keepdims=True)
    y = (xf - mu) * lax.rsqrt(var + LN_EPS)
    return (y * g.astype(jnp.float32) + b.astype(jnp.float32)).astype(x.dtype)


def causal_dwconv(x, w, b):
    k = w.shape[0]
    c = x.shape[-1]
    y = lax.conv_general_dilated(
        x, w[:, None, :].astype(x.dtype), window_strides=(1,), padding=[(k - 1, 0)],
        dimension_numbers=("NWC", "WIO", "NWC"), feature_group_count=c)
    return y + b.astype(x.dtype)


def hgrn2_recurrence(q, k, v, log_f):
    B, T, H, N = q.shape
    Dv = v.shape[-1]
    C, L = HGRN_CHUNK, HGRN_SUB
    nc, ns = T // C, C // L

    def blocks(t):
        return t.astype(jnp.float32).reshape(B, nc, ns, L, H, -1).transpose(0, 4, 1, 2, 3, 5)

    qb, kb, vb = blocks(q), blocks(k), blocks(v)
    gb = blocks(log_f)
    cum = jnp.cumsum(gb.reshape(B, H, nc, C, N), axis=3)
    bb = cum.reshape(B, H, nc, ns, L, N)
    bend = bb[..., -1, :]

    idx = jnp.arange(ns)
    blk_lower = idx[:, None] > idx[None, :]
    e_off = bb[:, :, :, :, None, :, :] - bend[:, :, :, None, :, None, :]
    q_off = qb[:, :, :, :, None] * jnp.exp(jnp.where(blk_lower[:, :, None, None], e_off, -jnp.inf))
    k_off = kb * jnp.exp(bend[..., None, :] - bb)
    a_off = jnp.einsum("bhcijtn,bhcjsn->bhcijts", q_off, k_off)
    o_off = jnp.einsum("bhcijts,bhcjsd->bhcitd", a_off, vb)

    pos = jnp.arange(L)
    causal = pos[:, None] >= pos[None, :]
    e_d = bb[..., :, None, :] - bb[..., None, :, :]
    decay = jnp.exp(jnp.where(causal[:, :, None], e_d, -jnp.inf))
    a_d = jnp.einsum("bhcitn,bhcitsn,bhcisn->bhcits", qb, decay, kb)
    o_d = jnp.einsum("bhcits,bhcisd->bhcitd", a_d, vb)
    o_intra = (o_off + o_d).reshape(B, H, nc, C, Dv)

    qc = qb.reshape(B, H, nc, C, N)
    kc = kb.reshape(B, H, nc, C, N)
    vc = vb.reshape(B, H, nc, C, Dv)
    blast = cum[..., -1, :]
    q_in = qc * jnp.exp(cum)
    k_up = kc * jnp.exp(blast[..., None, :] - cum)

    def step(S, xs):
        q_i, k_i, v_i, bl_i = xs
        o_i = jnp.einsum("bhtn,bhnd->bhtd", q_i, S)
        S = S * jnp.exp(bl_i)[..., None] + jnp.einsum("bhtn,bhtd->bhnd", k_i, v_i)
        return S, o_i

    xs = (jnp.moveaxis(q_in, 2, 0), jnp.moveaxis(k_up, 2, 0),
          jnp.moveaxis(vc, 2, 0), jnp.moveaxis(blast, 2, 0))
    S0 = jnp.zeros((B, H, N, Dv), jnp.float32)
    _, o_inter = lax.scan(step, S0, xs)
    o = o_intra + jnp.moveaxis(o_inter, 0, 2)
    return o.reshape(B, H, T, Dv).transpose(0, 2, 1, 3)


def hybrid_mixer(x, w_in, conv_w, conv_b, conv_norm_g, conv_norm_b, lb_logits, hgrn_norm_g, w_out, layer):
    B, T, _ = x.shape
    h = x @ w_in
    c1, c2 = CONV_WIDTH, 2 * CONV_WIDTH
    a, gate = h[..., :c1], h[..., c1:c2]
    q, f, i, og = (h[..., c2 + n * HGRN_WIDTH: c2 + (n + 1) * HGRN_WIDTH] for n in range(4))

    u = a * jax.nn.sigmoid(gate)
    u = causal_dwconv(u, conv_w, conv_b)
    u = layer_norm(u.reshape(B, T, CONV_GROUPS, CONV_GROUP_DIM),
                   conv_norm_g.reshape(CONV_GROUPS, CONV_GROUP_DIM),
                   conv_norm_b.reshape(CONV_GROUPS, CONV_GROUP_DIM)).reshape(B, T, CONV_WIDTH)
    u = jax.nn.silu(u)

    lb_table = jnp.cumsum(jax.nn.softmax(lb_logits.astype(jnp.float32), axis=0), axis=0)
    lb = lb_table[layer]
    fg = lb + (1.0 - lb) * jax.nn.sigmoid(f.astype(jnp.float32))
    log_f = jnp.log(fg)
    kk = 1.0 - fg
    qh = jax.nn.silu(q.astype(jnp.float32))
    shp = (B, T, HGRN_HEADS, HGRN_EXPAND)
    o = hgrn2_recurrence(qh.reshape(shp), kk.reshape(shp),
                         i.reshape(B, T, HGRN_HEADS, HGRN_HEAD_DIM), log_f.reshape(shp))
    o = o * lax.rsqrt(jnp.mean(jnp.square(o), axis=-1, keepdims=True) + RMS_EPS)
    o = o.reshape(B, T, HGRN_WIDTH) * hgrn_norm_g.astype(jnp.float32)
    o = (o * jax.nn.silu(og.astype(jnp.float32))).astype(x.dtype)

    return jnp.concatenate([u, o], axis=-1) @ w_out


def conv_ffn(x, w_up, conv_w, conv_b, w_down):
    h = x @ w_up
    g, v = h[..., :D_FF], h[..., D_FF:]
    g = causal_dwconv(g, conv_w, conv_b)
    return (jax.nn.silu(g) * v) @ w_down


def _fwd_setup_inputs(seed: int = 0) -> dict:
    key = jax.random.key(seed)
    ks = jax.random.split(key, 20)
    f32 = jnp.float32
    nrm = lambda k, s: jax.random.normal(k, s, f32)
    return {
        "x": nrm(ks[0], (BATCH, SEQ, D_MODEL)),
        "emb_ln_g": 1.0 + 0.02 * nrm(ks[1], (D_MODEL,)),
        "emb_ln_b": 0.02 * nrm(ks[2], (D_MODEL,)),
        "w_in": nrm(ks[3], (DEPTH, D_MODEL, IN_PROJ_DIM)) * D_MODEL ** -0.5,
        "conv_w": nrm(ks[4], (DEPTH, CONV_KERNEL, CONV_WIDTH)) * CONV_KERNEL ** -0.5,
        "conv_b": 0.01 * nrm(ks[5], (DEPTH, CONV_WIDTH)),
        "conv_norm_g": 1.0 + 0.02 * nrm(ks[6], (DEPTH, CONV_WIDTH)),
        "conv_norm_b": 0.02 * nrm(ks[7], (DEPTH, CONV_WIDTH)),
        "lb_logits": 0.5 * nrm(ks[8], (DEPTH + 1, HGRN_WIDTH)),
        "hgrn_norm_g": 1.0 + 0.02 * nrm(ks[9], (DEPTH, HGRN_WIDTH)),
        "w_out": nrm(ks[10], (DEPTH, D_MODEL, D_MODEL)) * (D_MODEL ** -0.5) * BETA,
        "ln1_g": 1.0 + 0.02 * nrm(ks[11], (DEPTH, D_MODEL)),
        "ln1_b": 0.02 * nrm(ks[12], (DEPTH, D_MODEL)),
        "w_ffn_up": nrm(ks[13], (DEPTH, D_MODEL, 2 * D_FF)) * D_MODEL ** -0.5,
        "ffn_conv_w": nrm(ks[14], (DEPTH, FFN_KERNEL, D_FF)) * FFN_KERNEL ** -0.5,
        "ffn_conv_b": 0.01 * nrm(ks[15], (DEPTH, D_FF)),
        "w_ffn_down": nrm(ks[16], (DEPTH, D_FF, D_MODEL)) * (D_FF ** -0.5) * BETA,
        "ln2_g": 1.0 + 0.02 * nrm(ks[17], (DEPTH, D_MODEL)),
        "ln2_b": 0.02 * nrm(ks[18], (DEPTH, D_MODEL)),
    }


def _fwd_reference(x, emb_ln_g, emb_ln_b, w_in, conv_w, conv_b, conv_norm_g, conv_norm_b, lb_logits,
              hgrn_norm_g, w_out, ln1_g, ln1_b, w_ffn_up, ffn_conv_w, ffn_conv_b, w_ffn_down,
              ln2_g, ln2_b):
    h = layer_norm(x, emb_ln_g, emb_ln_b)
    for l in range(DEPTH):
        mix = hybrid_mixer(h, w_in[l], conv_w[l], conv_b[l], conv_norm_g[l], conv_norm_b[l],
                           lb_logits, hgrn_norm_g[l], w_out[l], l)
        h = layer_norm(ALPHA * h + mix, ln1_g[l], ln1_b[l])
        ffn = conv_ffn(h, w_ffn_up[l], ffn_conv_w[l], ffn_conv_b[l], w_ffn_down[l])
        h = layer_norm(ALPHA * h + ffn, ln2_g[l], ln2_b[l])
    return h


import jax as _jax
import jax.numpy as _jnp

TWIN_FORMAT = 'train_step'
FWD_PARAMS = ['x', 'emb_ln_g', 'emb_ln_b', 'w_in', 'conv_w', 'conv_b', 'conv_norm_g', 'conv_norm_b', 'lb_logits', 'hgrn_norm_g', 'w_out', 'ln1_g', 'ln1_b', 'w_ffn_up', 'ffn_conv_w', 'ffn_conv_b', 'w_ffn_down', 'ln2_g', 'ln2_b']
TWIN_WEIGHTS = ['emb_ln_g', 'emb_ln_b', 'w_in', 'conv_w', 'conv_b', 'conv_norm_g', 'conv_norm_b', 'lb_logits', 'hgrn_norm_g', 'w_out', 'ln1_g', 'ln1_b', 'w_ffn_up', 'ffn_conv_w', 'ffn_conv_b', 'w_ffn_down', 'ln2_g', 'ln2_b']
TWIN_DIFF_INPUT = 'x'
TWIN_INPUTS = ['x', 'emb_ln_g', 'emb_ln_b', 'w_in', 'conv_w', 'conv_b', 'conv_norm_g', 'conv_norm_b', 'lb_logits', 'hgrn_norm_g', 'w_out', 'ln1_g', 'ln1_b', 'w_ffn_up', 'ffn_conv_w', 'ffn_conv_b', 'w_ffn_down', 'ln2_g', 'ln2_b', 'loss_target', 'm_emb_ln_g', 'm_emb_ln_b', 'm_w_in', 'm_conv_w', 'm_conv_b', 'm_conv_norm_g', 'm_conv_norm_b', 'm_lb_logits', 'm_hgrn_norm_g', 'm_w_out', 'm_ln1_g', 'm_ln1_b', 'm_w_ffn_up', 'm_ffn_conv_w', 'm_ffn_conv_b', 'm_w_ffn_down', 'm_ln2_g', 'm_ln2_b', 'v_emb_ln_g', 'v_emb_ln_b', 'v_w_in', 'v_conv_w', 'v_conv_b', 'v_conv_norm_g', 'v_conv_norm_b', 'v_lb_logits', 'v_hgrn_norm_g', 'v_w_out', 'v_ln1_g', 'v_ln1_b', 'v_w_ffn_up', 'v_ffn_conv_w', 'v_ffn_conv_b', 'v_w_ffn_down', 'v_ln2_g', 'v_ln2_b']
TWIN_OUTPUTS = ['loss', 'grad_x', 'grad_emb_ln_g', 'grad_emb_ln_b', 'grad_w_in', 'grad_conv_w', 'grad_conv_b', 'grad_conv_norm_g', 'grad_conv_norm_b', 'grad_lb_logits', 'grad_hgrn_norm_g', 'grad_w_out', 'grad_ln1_g', 'grad_ln1_b', 'grad_w_ffn_up', 'grad_ffn_conv_w', 'grad_ffn_conv_b', 'grad_w_ffn_down', 'grad_ln2_g', 'grad_ln2_b', 'delta_emb_ln_g', 'delta_emb_ln_b', 'delta_w_in', 'delta_conv_w', 'delta_conv_b', 'delta_conv_norm_g', 'delta_conv_norm_b', 'delta_lb_logits', 'delta_hgrn_norm_g', 'delta_w_out', 'delta_ln1_g', 'delta_ln1_b', 'delta_w_ffn_up', 'delta_ffn_conv_w', 'delta_ffn_conv_b', 'delta_w_ffn_down', 'delta_ln2_g', 'delta_ln2_b', 'new_m_emb_ln_g', 'new_m_emb_ln_b', 'new_m_w_in', 'new_m_conv_w', 'new_m_conv_b', 'new_m_conv_norm_g', 'new_m_conv_norm_b', 'new_m_lb_logits', 'new_m_hgrn_norm_g', 'new_m_w_out', 'new_m_ln1_g', 'new_m_ln1_b', 'new_m_w_ffn_up', 'new_m_ffn_conv_w', 'new_m_ffn_conv_b', 'new_m_w_ffn_down', 'new_m_ln2_g', 'new_m_ln2_b', 'new_v_emb_ln_g', 'new_v_emb_ln_b', 'new_v_w_in', 'new_v_conv_w', 'new_v_conv_b', 'new_v_conv_norm_g', 'new_v_conv_norm_b', 'new_v_lb_logits', 'new_v_hgrn_norm_g', 'new_v_w_out', 'new_v_ln1_g', 'new_v_ln1_b', 'new_v_w_ffn_up', 'new_v_ffn_conv_w', 'new_v_ffn_conv_b', 'new_v_w_ffn_down', 'new_v_ln2_g', 'new_v_ln2_b']
TWIN_LEAF_KINDS = {'loss': 'loss', 'grad_x': 'grad_x', 'grad_emb_ln_g': 'grad_w', 'grad_emb_ln_b': 'grad_w', 'grad_w_in': 'grad_w', 'grad_conv_w': 'grad_w', 'grad_conv_b': 'grad_w', 'grad_conv_norm_g': 'grad_w', 'grad_conv_norm_b': 'grad_w', 'grad_lb_logits': 'grad_w', 'grad_hgrn_norm_g': 'grad_w', 'grad_w_out': 'grad_w', 'grad_ln1_g': 'grad_w', 'grad_ln1_b': 'grad_w', 'grad_w_ffn_up': 'grad_w', 'grad_ffn_conv_w': 'grad_w', 'grad_ffn_conv_b': 'grad_w', 'grad_w_ffn_down': 'grad_w', 'grad_ln2_g': 'grad_w', 'grad_ln2_b': 'grad_w', 'delta_emb_ln_g': 'delta_w', 'delta_emb_ln_b': 'delta_w', 'delta_w_in': 'delta_w', 'delta_conv_w': 'delta_w', 'delta_conv_b': 'delta_w', 'delta_conv_norm_g': 'delta_w', 'delta_conv_norm_b': 'delta_w', 'delta_lb_logits': 'delta_w', 'delta_hgrn_norm_g': 'delta_w', 'delta_w_out': 'delta_w', 'delta_ln1_g': 'delta_w', 'delta_ln1_b': 'delta_w', 'delta_w_ffn_up': 'delta_w', 'delta_ffn_conv_w': 'delta_w', 'delta_ffn_conv_b': 'delta_w', 'delta_w_ffn_down': 'delta_w', 'delta_ln2_g': 'delta_w', 'delta_ln2_b': 'delta_w', 'new_m_emb_ln_g': 'new_m', 'new_m_emb_ln_b': 'new_m', 'new_m_w_in': 'new_m', 'new_m_conv_w': 'new_m', 'new_m_conv_b': 'new_m', 'new_m_conv_norm_g': 'new_m', 'new_m_conv_norm_b': 'new_m', 'new_m_lb_logits': 'new_m', 'new_m_hgrn_norm_g': 'new_m', 'new_m_w_out': 'new_m', 'new_m_ln1_g': 'new_m', 'new_m_ln1_b': 'new_m', 'new_m_w_ffn_up': 'new_m', 'new_m_ffn_conv_w': 'new_m', 'new_m_ffn_conv_b': 'new_m', 'new_m_w_ffn_down': 'new_m', 'new_m_ln2_g': 'new_m', 'new_m_ln2_b': 'new_m', 'new_v_emb_ln_g': 'new_v', 'new_v_emb_ln_b': 'new_v', 'new_v_w_in': 'new_v', 'new_v_conv_w': 'new_v', 'new_v_conv_b': 'new_v', 'new_v_conv_norm_g': 'new_v', 'new_v_conv_norm_b': 'new_v', 'new_v_lb_logits': 'new_v', 'new_v_hgrn_norm_g': 'new_v', 'new_v_w_out': 'new_v', 'new_v_ln1_g': 'new_v', 'new_v_ln1_b': 'new_v', 'new_v_w_ffn_up': 'new_v', 'new_v_ffn_conv_w': 'new_v', 'new_v_ffn_conv_b': 'new_v', 'new_v_w_ffn_down': 'new_v', 'new_v_ln2_g': 'new_v', 'new_v_ln2_b': 'new_v'}


def _forward(args):
    return _fwd_reference(*[args[k] for k in FWD_PARAMS])


def _output_shape():
    def fwd():
        inp = _fwd_setup_inputs(0)
        return _fwd_reference(*[inp[k] for k in FWD_PARAMS])
    out = _jax.eval_shape(fwd)
    return out.shape, out.dtype

N_MICROBATCH = 1
ADAM_LR = 0.001
ADAM_B1 = 0.9
ADAM_B2 = 0.999
ADAM_EPS = 1e-08
ADAM_WD = 0.01
ADAM_STEP = 10
PER_EXAMPLE_BATCH_AXIS = {'x': 0, 'loss_target': 0}
SHARED_INPUTS = []
_WEIGHT_DTYPES = {'emb_ln_g': _jnp.float32, 'emb_ln_b': _jnp.float32, 'w_in': _jnp.float32, 'conv_w': _jnp.float32, 'conv_b': _jnp.float32, 'conv_norm_g': _jnp.float32, 'conv_norm_b': _jnp.float32, 'lb_logits': _jnp.float32, 'hgrn_norm_g': _jnp.float32, 'w_out': _jnp.float32, 'ln1_g': _jnp.float32, 'ln1_b': _jnp.float32, 'w_ffn_up': _jnp.float32, 'ffn_conv_w': _jnp.float32, 'ffn_conv_b': _jnp.float32, 'w_ffn_down': _jnp.float32, 'ln2_g': _jnp.float32, 'ln2_b': _jnp.float32}
MOMENT_SCALE = {'emb_ln_g': 5.026938e-01, 'emb_ln_b': 2.759378e-01, 'w_in': 1.977406e-02, 'conv_w': 2.821207e-02, 'conv_b': 1.164590e-01, 'conv_norm_g': 5.284407e-02, 'conv_norm_b': 6.930514e-02, 'lb_logits': 2.332952e-03, 'hgrn_norm_g': 2.747335e-02, 'w_out': 5.158434e-02, 'ln1_g': 5.488700e-01, 'ln1_b': 2.843881e-01, 'w_ffn_up': 1.592210e-02, 'ffn_conv_w': 1.627515e-02, 'ffn_conv_b': 1.575284e-02, 'w_ffn_down': 4.380216e-02, 'ln2_g': 1.600782e+01, 'ln2_b': 1.167475e+00}


def _to_microbatches(a, axis):
    t = _jnp.moveaxis(a, axis, 0)
    t = t.reshape((N_MICROBATCH, t.shape[0] // N_MICROBATCH) + t.shape[1:])
    return _jnp.moveaxis(t, 1, axis + 1)


def setup_inputs(seed: int = 0) -> dict:
    inp = _fwd_setup_inputs(seed)
    key = _jax.random.fold_in(_jax.random.key(seed), 7919)
    shape, _ = _output_shape()
    out = dict(inp)
    out["loss_target"] = _jax.random.normal(_jax.random.fold_in(key, 0), shape, _jnp.float32)
    for i, name in enumerate(TWIN_WEIGHTS):
        w = inp[name].astype(_jnp.float32)
        if MOMENT_SCALE is None:
            s = _jnp.sqrt(_jnp.mean(_jnp.square(w)) + 1e-30)
        else:
            s = MOMENT_SCALE[name]
        km, kv = _jax.random.split(_jax.random.fold_in(key, i + 1))
        out[name] = w
        out["m_" + name] = s * _jax.random.normal(km, w.shape, _jnp.float32)
        out["v_" + name] = (s * s) * _jax.random.uniform(kv, w.shape, _jnp.float32, 0.5, 1.5)
    if N_MICROBATCH > 1:
        for name, axis in PER_EXAMPLE_BATCH_AXIS.items():
            out[name] = _to_microbatches(out[name], axis)
    return {'x': out['x'], 'emb_ln_g': out['emb_ln_g'], 'emb_ln_b': out['emb_ln_b'], 'w_in': out['w_in'], 'conv_w': out['conv_w'], 'conv_b': out['conv_b'], 'conv_norm_g': out['conv_norm_g'], 'conv_norm_b': out['conv_norm_b'], 'lb_logits': out['lb_logits'], 'hgrn_norm_g': out['hgrn_norm_g'], 'w_out': out['w_out'], 'ln1_g': out['ln1_g'], 'ln1_b': out['ln1_b'], 'w_ffn_up': out['w_ffn_up'], 'ffn_conv_w': out['ffn_conv_w'], 'ffn_conv_b': out['ffn_conv_b'], 'w_ffn_down': out['w_ffn_down'], 'ln2_g': out['ln2_g'], 'ln2_b': out['ln2_b'], 'loss_target': out['loss_target'], 'm_emb_ln_g': out['m_emb_ln_g'], 'm_emb_ln_b': out['m_emb_ln_b'], 'm_w_in': out['m_w_in'], 'm_conv_w': out['m_conv_w'], 'm_conv_b': out['m_conv_b'], 'm_conv_norm_g': out['m_conv_norm_g'], 'm_conv_norm_b': out['m_conv_norm_b'], 'm_lb_logits': out['m_lb_logits'], 'm_hgrn_norm_g': out['m_hgrn_norm_g'], 'm_w_out': out['m_w_out'], 'm_ln1_g': out['m_ln1_g'], 'm_ln1_b': out['m_ln1_b'], 'm_w_ffn_up': out['m_w_ffn_up'], 'm_ffn_conv_w': out['m_ffn_conv_w'], 'm_ffn_conv_b': out['m_ffn_conv_b'], 'm_w_ffn_down': out['m_w_ffn_down'], 'm_ln2_g': out['m_ln2_g'], 'm_ln2_b': out['m_ln2_b'], 'v_emb_ln_g': out['v_emb_ln_g'], 'v_emb_ln_b': out['v_emb_ln_b'], 'v_w_in': out['v_w_in'], 'v_conv_w': out['v_conv_w'], 'v_conv_b': out['v_conv_b'], 'v_conv_norm_g': out['v_conv_norm_g'], 'v_conv_norm_b': out['v_conv_norm_b'], 'v_lb_logits': out['v_lb_logits'], 'v_hgrn_norm_g': out['v_hgrn_norm_g'], 'v_w_out': out['v_w_out'], 'v_ln1_g': out['v_ln1_g'], 'v_ln1_b': out['v_ln1_b'], 'v_w_ffn_up': out['v_w_ffn_up'], 'v_ffn_conv_w': out['v_ffn_conv_w'], 'v_ffn_conv_b': out['v_ffn_conv_b'], 'v_w_ffn_down': out['v_w_ffn_down'], 'v_ln2_g': out['v_ln2_g'], 'v_ln2_b': out['v_ln2_b']}


def _loss(weights, diff, rest, loss_target):
    with _jax.named_scope("forward"):
        args = {**rest, TWIN_DIFF_INPUT: diff, **{k: w.astype(_WEIGHT_DTYPES[k]) for k, w in weights.items()}}
        y = _forward(args)
    with _jax.named_scope("loss_head"):
        err = _jnp.square(y.astype(_jnp.float32) - loss_target)
        return 0.5 * _jnp.sum(_jnp.mean(err, axis=-1)) if err.ndim else 0.5 * err


def _adamw(w, g, m, v):
    m = ADAM_B1 * m + (1.0 - ADAM_B1) * g
    v = ADAM_B2 * v + (1.0 - ADAM_B2) * _jnp.square(g)
    m_hat = m / (1.0 - ADAM_B1 ** ADAM_STEP)
    v_hat = v / (1.0 - ADAM_B2 ** ADAM_STEP)
    delta = -ADAM_LR * (m_hat / (_jnp.sqrt(v_hat) + ADAM_EPS) + ADAM_WD * w)
    return delta, m, v


def reference(x, emb_ln_g, emb_ln_b, w_in, conv_w, conv_b, conv_norm_g, conv_norm_b, lb_logits, hgrn_norm_g, w_out, ln1_g, ln1_b, w_ffn_up, ffn_conv_w, ffn_conv_b, w_ffn_down, ln2_g, ln2_b, loss_target, m_emb_ln_g, m_emb_ln_b, m_w_in, m_conv_w, m_conv_b, m_conv_norm_g, m_conv_norm_b, m_lb_logits, m_hgrn_norm_g, m_w_out, m_ln1_g, m_ln1_b, m_w_ffn_up, m_ffn_conv_w, m_ffn_conv_b, m_w_ffn_down, m_ln2_g, m_ln2_b, v_emb_ln_g, v_emb_ln_b, v_w_in, v_conv_w, v_conv_b, v_conv_norm_g, v_conv_norm_b, v_lb_logits, v_hgrn_norm_g, v_w_out, v_ln1_g, v_ln1_b, v_w_ffn_up, v_ffn_conv_w, v_ffn_conv_b, v_w_ffn_down, v_ln2_g, v_ln2_b):
    given = dict(x=x, emb_ln_g=emb_ln_g, emb_ln_b=emb_ln_b, w_in=w_in, conv_w=conv_w, conv_b=conv_b, conv_norm_g=conv_norm_g, conv_norm_b=conv_norm_b, lb_logits=lb_logits, hgrn_norm_g=hgrn_norm_g, w_out=w_out, ln1_g=ln1_g, ln1_b=ln1_b, w_ffn_up=w_ffn_up, ffn_conv_w=ffn_conv_w, ffn_conv_b=ffn_conv_b, w_ffn_down=w_ffn_down, ln2_g=ln2_g, ln2_b=ln2_b, loss_target=loss_target, m_emb_ln_g=m_emb_ln_g, m_emb_ln_b=m_emb_ln_b, m_w_in=m_w_in, m_conv_w=m_conv_w, m_conv_b=m_conv_b, m_conv_norm_g=m_conv_norm_g, m_conv_norm_b=m_conv_norm_b, m_lb_logits=m_lb_logits, m_hgrn_norm_g=m_hgrn_norm_g, m_w_out=m_w_out, m_ln1_g=m_ln1_g, m_ln1_b=m_ln1_b, m_w_ffn_up=m_w_ffn_up, m_ffn_conv_w=m_ffn_conv_w, m_ffn_conv_b=m_ffn_conv_b, m_w_ffn_down=m_w_ffn_down, m_ln2_g=m_ln2_g, m_ln2_b=m_ln2_b, v_emb_ln_g=v_emb_ln_g, v_emb_ln_b=v_emb_ln_b, v_w_in=v_w_in, v_conv_w=v_conv_w, v_conv_b=v_conv_b, v_conv_norm_g=v_conv_norm_g, v_conv_norm_b=v_conv_norm_b, v_lb_logits=v_lb_logits, v_hgrn_norm_g=v_hgrn_norm_g, v_w_out=v_w_out, v_ln1_g=v_ln1_g, v_ln1_b=v_ln1_b, v_w_ffn_up=v_w_ffn_up, v_ffn_conv_w=v_ffn_conv_w, v_ffn_conv_b=v_ffn_conv_b, v_w_ffn_down=v_w_ffn_down, v_ln2_g=v_ln2_g, v_ln2_b=v_ln2_b)
    weights = {n: given[n] for n in TWIN_WEIGHTS}
    shared = {n: given[n] for n in SHARED_INPUTS}
    per_example = {n: given[n] for n in ['x']}
    grad_fn = _jax.value_and_grad(_loss, argnums=(0, 1))

    def one_microbatch(ex, loss_target):
        ex = dict(ex)
        diff = ex.pop(TWIN_DIFF_INPUT)
        return grad_fn(weights, diff, {**shared, **ex}, loss_target)

    if N_MICROBATCH == 1:
        loss, (grad_w, grad_x) = one_microbatch(per_example, given["loss_target"])
    else:
        def body(carry, xs):
            loss_sum, grad_sum = carry
            l_k, (gw_k, gx_k) = one_microbatch(xs[0], xs[1])
            with _jax.named_scope("update"):
                return (loss_sum + l_k, _jax.tree.map(_jnp.add, grad_sum, gw_k)), gx_k

        init = (_jnp.zeros((), _jnp.float32), _jax.tree.map(_jnp.zeros_like, weights))
        (loss, grad_w), grad_x = _jax.lax.scan(body, init, (per_example, given["loss_target"]))
    with _jax.named_scope("update"):
        delta_w, new_m, new_v = {}, {}, {}
        for n in TWIN_WEIGHTS:
            delta_w[n], new_m[n], new_v[n] = _adamw(weights[n], grad_w[n], given["m_" + n], given["v_" + n])
    return (loss, grad_x, *[grad_w[n] for n in TWIN_WEIGHTS], *[delta_w[n] for n in TWIN_WEIGHTS],
            *[new_m[n] for n in TWIN_WEIGHTS], *[new_v[n] for n in TWIN_WEIGHTS])
```

```python
import functools

import jax
import jax.numpy as jnp
from jax import lax
from jax.experimental import pallas as pl
from jax.experimental.pallas import tpu as pltpu

F32 = jnp.float32
BF16 = jnp.bfloat16

N_DEV = 8
D_MODEL = 2048
CONV_WIDTH = 1024
CONV_KERNEL = 31
HGRN_WIDTH = 1024
GROUP = 128
N_GROUPS = 8
IN_PROJ = 2 * CONV_WIDTH + 4 * HGRN_WIDTH
D_FF = 5632
FFN_KERNEL = 3
CHUNK = 64
SUB = 8
LN_EPS = 1e-5
RMS_EPS = 1e-6
ALPHA = 2.0 ** 0.25
ADAM_LR, ADAM_B1, ADAM_B2, ADAM_EPS, ADAM_WD, ADAM_STEP = 0.001, 0.9, 0.999, 1e-08, 0.01, 10

VMEM_LIMIT = 56 * 1024 * 1024
MESH = pl.DeviceIdType.MESH


def _cparams(sem=None):
    return pltpu.CompilerParams(dimension_semantics=sem, vmem_limit_bytes=VMEM_LIMIT)


def _sigmoid(x):
    return 1.0 / (1.0 + jnp.exp(-x))


def _matmul(name, a, b, out_shape, out_dtype, grid, a_spec, b_spec, o_spec, nt):
    nk = grid[2]
    dims = (((1,), (1,)), ((), ())) if nt else (((1,), (0,)), ((), ()))

    def body(a_ref, b_ref, o_ref, acc_ref):
        k = pl.program_id(2)

        @pl.when(k == 0)
        def _():
            acc_ref[...] = jnp.zeros_like(acc_ref)

        av = a_ref[0] if len(a_ref.shape) == 3 else a_ref[...]
        bv = b_ref[0] if len(b_ref.shape) == 3 else b_ref[...]
        acc_ref[...] += lax.dot_general(av, bv, dims, preferred_element_type=F32)

        @pl.when(k == nk - 1)
        def _():
            res = acc_ref[...].astype(out_dtype)
            if len(o_ref.shape) == 3:
                o_ref[0] = res
            else:
                o_ref[...] = res

    acc_shape = o_spec.block_shape[-2:]
    assert all(g >= 1 for g in grid), (name, grid)
    return pl.pallas_call(
        body, name=name, grid=grid, in_specs=[a_spec, b_spec], out_specs=o_spec,
        out_shape=jax.ShapeDtypeStruct(out_shape, out_dtype),
        scratch_shapes=[pltpu.VMEM(acc_shape, F32)],
        compiler_params=_cparams(("parallel", "parallel", "arbitrary")),
    )(a, b)


def _mm_nn_cols(name, a, w, out_dtype, tm=1024, tk=512):
    m, k = a.shape
    tm = min(tm, m)
    ns = w.shape[2]
    return _matmul(
        name, a, w, (m, N_DEV * ns), out_dtype, (m // tm, N_DEV, k // tk),
        pl.BlockSpec((tm, tk), lambda i, j, kk: (i, kk)),
        pl.BlockSpec((1, tk, ns), lambda i, j, kk: (j, kk, 0)),
        pl.BlockSpec((tm, ns), lambda i, j, kk: (i, j)), nt=False)


def _mm_nn(name, a, w, out_dtype, tm=1024, tn=1024, tk=512):
    m, k = a.shape
    tm = min(tm, m)
    n = w.shape[1]
    return _matmul(
        name, a, w, (m, n), out_dtype, (m // tm, n // tn, k // tk),
        pl.BlockSpec((tm, tk), lambda i, j, kk: (i, kk)),
        pl.BlockSpec((tk, tn), lambda i, j, kk: (kk, j)),
        pl.BlockSpec((tm, tn), lambda i, j, kk: (i, j)), nt=False)


def _mm_nt_cols(name, a, w, out_dtype, tm=1024, tn=1024):
    m = a.shape[0]
    tm = min(tm, m)
    n, ns = w.shape[1], w.shape[2]
    return _matmul(
        name, a, w, (m, n), out_dtype, (m // tm, n // tn, N_DEV),
        pl.BlockSpec((tm, ns), lambda i, j, kk: (i, kk)),
        pl.BlockSpec((1, tn, ns), lambda i, j, kk: (kk, j, 0)),
        pl.BlockSpec((tm, tn), lambda i, j, kk: (i, j)), nt=True)


def _mm_nt(name, a, w, out_dtype, tm=1024, tn=1024, tk=512):
    m, k = a.shape
    tm = min(tm, m)
    n = w.shape[0]
    return _matmul(
        name, a, w, (m, n), out_dtype, (m // tm, n // tn, k // tk),
        pl.BlockSpec((tm, tk), lambda i, j, kk: (i, kk)),
        pl.BlockSpec((tn, tk), lambda i, j, kk: (j, kk)),
        pl.BlockSpec((tm, tn), lambda i, j, kk: (i, j)), nt=True)


def _mm_grad_cols(name, at, b, ns, tm=1024, tk=512):
    k1, t = at.shape
    tk = min(tk, t)
    return _matmul(
        name, at, b, (N_DEV, k1, ns), BF16, (k1 // tm, N_DEV, t // tk),
        pl.BlockSpec((tm, tk), lambda i, j, kk: (i, kk)),
        pl.BlockSpec((tk, ns), lambda i, j, kk: (kk, j)),
        pl.BlockSpec((1, tm, ns), lambda i, j, kk: (j, i, 0)), nt=False)


LN_ROWS = 256


def _ln_stats(r):
    mu = jnp.mean(r, axis=-1, keepdims=True)
    xc = r - mu
    var = jnp.mean(xc * xc, axis=-1, keepdims=True)
    rstd = lax.rsqrt(var + LN_EPS)
    return xc * rstd, rstd


def _row_spec(d):
    return pl.BlockSpec((LN_ROWS, d), lambda i: (i, 0))


def _vec_spec(d):
    return pl.BlockSpec((1, d), lambda i: (0, 0))


def _ln_fwd(name, a, m, g, b, alpha):
    t, d = a.shape
    has_m = m is not None

    def body(*refs):
        if has_m:
            a_ref, m_ref, g_ref, b_ref, r_ref, y_ref, yb_ref = refs
            r = alpha * a_ref[...] + m_ref[...]
            r_ref[...] = r
        else:
            a_ref, g_ref, b_ref, y_ref, yb_ref = refs
            r = a_ref[...]
        xhat, _ = _ln_stats(r)
        y = xhat * g_ref[...] + b_ref[...]
        y_ref[...] = y
        yb_ref[...] = y.astype(BF16)

    ins = [a] + ([m] if has_m else []) + [g, b]
    in_specs = [_row_spec(d)] * (2 if has_m else 1) + [_vec_spec(d)] * 2
    outs = ([jax.ShapeDtypeStruct((t, d), F32)] if has_m else []) + [
        jax.ShapeDtypeStruct((t, d), F32), jax.ShapeDtypeStruct((t, d), BF16)]
    res = pl.pallas_call(
        body, name=name, grid=(t // LN_ROWS,), in_specs=in_specs,
        out_specs=[_row_spec(d)] * len(outs), out_shape=outs,
        compiler_params=_cparams(("parallel",)),
    )(*ins)
    return res if has_m else (None, *res)


def _ln_bwd_math(r, dy, g):
    xhat, rstd = _ln_stats(r)
    dxhat = dy * g
    m1 = jnp.mean(dxhat, axis=-1, keepdims=True)
    m2 = jnp.mean(dxhat * xhat, axis=-1, keepdims=True)
    dr = rstd * (dxhat - m1 - xhat * m2)
    return dr, jnp.sum(dy * xhat, axis=0, keepdims=True), jnp.sum(dy, axis=0, keepdims=True)


def _ln2_loss_bwd(name, h1, ffn, g, b, tgt):
    t, d = h1.shape

    def body(h1_ref, f_ref, g_ref, b_ref, t_ref, dr_ref, drb_ref, dg_ref, db_ref, loss_ref):
        @pl.when(pl.program_id(0) == 0)
        def _():
            dg_ref[...] = jnp.zeros_like(dg_ref)
            db_ref[...] = jnp.zeros_like(db_ref)
            loss_ref[...] = jnp.zeros_like(loss_ref)

        r = ALPHA * h1_ref[...] + f_ref[...]
        xhat, _ = _ln_stats(r)
        e = xhat * g_ref[...] + b_ref[...] - t_ref[...]
        loss_ref[...] += 0.5 / d * jnp.sum(e * e)
        dr, dg, db = _ln_bwd_math(r, e * (1.0 / d), g_ref[...])
        dr_ref[...] = dr
        drb_ref[...] = dr.astype(BF16)
        dg_ref[...] += dg
        db_ref[...] += db

    return pl.pallas_call(
        body, name=name, grid=(t // LN_ROWS,),
        in_specs=[_row_spec(d), _row_spec(d), _vec_spec(d), _vec_spec(d), _row_spec(d)],
        out_specs=[_row_spec(d), _row_spec(d), _vec_spec(d), _vec_spec(d), _vec_spec(128)],
        out_shape=[jax.ShapeDtypeStruct((t, d), F32), jax.ShapeDtypeStruct((t, d), BF16),
                   jax.ShapeDtypeStruct((1, d), F32), jax.ShapeDtypeStruct((1, d), F32),
                   jax.ShapeDtypeStruct((1, 128), F32)],
        compiler_params=_cparams(("arbitrary",)),
    )(h1, ffn, g, b, tgt)


def _ln_bwd(name, r, dya, dyb, g, alpha, want_bf16):
    t, d = r.shape

    def body(r_ref, dya_ref, dyb_ref, g_ref, *outs):
        dr_ref = outs[0]
        dg_ref, db_ref = outs[-2:]

        @pl.when(pl.program_id(0) == 0)
        def _():
            dg_ref[...] = jnp.zeros_like(dg_ref)
            db_ref[...] = jnp.zeros_like(db_ref)

        dy = alpha * dya_ref[...] + dyb_ref[...]
        dr, dg, db = _ln_bwd_math(r_ref[...], dy, g_ref[...])
        dr_ref[...] = dr
        if want_bf16:
            outs[1][...] = dr.astype(BF16)
        dg_ref[...] += dg
        db_ref[...] += db

    big = [jax.ShapeDtypeStruct((t, d), F32)] + ([jax.ShapeDtypeStruct((t, d), BF16)] if want_bf16 else [])
    return pl.pallas_call(
        body, name=name, grid=(t // LN_ROWS,),
        in_specs=[_row_spec(d)] * 3 + [_vec_spec(d)],
        out_specs=[_row_spec(d)] * len(big) + [_vec_spec(d)] * 2,
        out_shape=big + [jax.ShapeDtypeStruct((1, d), F32)] * 2,
        compiler_params=_cparams(("arbitrary",)),
    )(r, dya, dyb, g)


CONV_ROWS = 64


def _for_shifted(win, tm, shifts, fn):
    n = win.shape[0]
    for r in range(8):
        group = [s for s in shifts if s % 8 == r]
        if not group:
            continue
        rolled = win if r == 0 else pltpu.roll(win, n - r, axis=0)
        for s in group:
            fn(s, rolled[8 * (s // 8): 8 * (s // 8) + tm])


def _col_spec(t, cb, off=0):
    return pl.BlockSpec((t, cb), lambda j: (0, j + off))


def _ffn_act_fwd(name, hf, w, b, cb=256):
    t = hf.shape[0]
    f = hf.shape[1] // 2
    nb = f // cb
    tm = CONV_ROWS

    def body(g_ref, v_ref, w_ref, b_ref, act_ref, pad_ref):
        pad_ref[pl.ds(0, 8), :] = jnp.zeros((8, cb), F32)
        pad_ref[pl.ds(8, t), :] = g_ref[...]
        wv = [w_ref[pl.ds(k, 1), :] for k in range(FFN_KERNEL)]
        bias = b_ref[...]

        def tile(i, carry):
            r0 = pl.multiple_of(i * tm, tm)
            win = pad_ref[pl.ds(r0, tm + 8), :]
            acc = [jnp.broadcast_to(bias, (tm, cb))]

            def tap(s, rows):
                acc[0] = acc[0] + wv[s - 6] * rows

            _for_shifted(win, tm, (6, 7, 8), tap)
            gc = acc[0]
            act_ref[pl.ds(r0, tm), :] = (gc * _sigmoid(gc) * v_ref[pl.ds(r0, tm), :]).astype(BF16)
            return carry

        lax.fori_loop(0, t // tm, tile, 0)

    return pl.pallas_call(
        body, name=name, grid=(nb,),
        in_specs=[_col_spec(t, cb), _col_spec(t, cb, nb),
                  pl.BlockSpec((FFN_KERNEL, cb), lambda j: (0, j)), pl.BlockSpec((1, cb), lambda j: (0, j))],
        out_specs=_col_spec(t, cb), out_shape=jax.ShapeDtypeStruct((t, f), BF16),
        scratch_shapes=[pltpu.VMEM((t + 8, cb), F32)],
        compiler_params=_cparams(("parallel",)),
    )(hf, hf, w, b)


def _ffn_act_bwd(name, dact, hf, w, b, cb=128):
    t = hf.shape[0]
    f = hf.shape[1] // 2
    nb = f // cb
    tm = CONV_ROWS

    def body(da_ref, g_ref, v_ref, w_ref, b_ref, dhf_ref, dw_ref, db_ref, pad_ref, dgc_ref):
        pad_ref[pl.ds(0, 8), :] = jnp.zeros((8, cb), F32)
        pad_ref[pl.ds(8, t), :] = g_ref[...]
        dgc_ref[pl.ds(t, 8), :] = jnp.zeros((8, cb), F32)
        wv = [w_ref[pl.ds(k, 1), :] for k in range(FFN_KERNEL)]
        bias = b_ref[...]

        def tile_a(i, carry):
            r0 = pl.multiple_of(i * tm, tm)
            win = pad_ref[pl.ds(r0, tm + 8), :]
            taps = {}
            _for_shifted(win, tm, (6, 7, 8), lambda s, rows: taps.__setitem__(s, rows))
            gc = bias + wv[0] * taps[6] + wv[1] * taps[7] + wv[2] * taps[8]
            sg = _sigmoid(gc)
            da = da_ref[pl.ds(r0, tm), :]
            dhf_ref[1, pl.ds(r0, tm), :] = (da * gc * sg).astype(BF16)
            dgc = da * v_ref[pl.ds(r0, tm), :] * sg * (1.0 + gc * (1.0 - sg))
            dgc_ref[pl.ds(r0, tm), :] = dgc
            sums = [jnp.sum(dgc * taps[6 + k], axis=0, keepdims=True) for k in range(3)]
            sums.append(jnp.sum(dgc, axis=0, keepdims=True))
            return tuple(c + s for c, s in zip(carry, sums))

        zero = jnp.zeros((1, cb), F32)
        dw0, dw1, dw2, dbias = lax.fori_loop(0, t // tm, tile_a, (zero, zero, zero, zero))
        row = lax.broadcasted_iota(jnp.int32, (8, cb), 0)
        dw_ref[...] = jnp.where(row == 0, dw0, jnp.where(row == 1, dw1, jnp.where(row == 2, dw2, 0.0)))
        db_ref[...] = dbias

        def tile_b(i, carry):
            r0 = pl.multiple_of(i * tm, tm)
            win = dgc_ref[pl.ds(r0, tm + 8), :]
            acc = [jnp.zeros((tm, cb), F32)]

            def tap(s, rows):
                acc[0] = acc[0] + wv[2 - s] * rows

            _for_shifted(win, tm, (0, 1, 2), tap)
            dhf_ref[0, pl.ds(r0, tm), :] = acc[0].astype(BF16)
            return carry

        lax.fori_loop(0, t // tm, tile_b, 0)

    return pl.pallas_call(
        body, name=name, grid=(nb,),
        in_specs=[_col_spec(t, cb), _col_spec(t, cb), _col_spec(t, cb, nb),
                  pl.BlockSpec((FFN_KERNEL, cb), lambda j: (0, j)), pl.BlockSpec((1, cb), lambda j: (0, j))],
        out_specs=[pl.BlockSpec((2, t, cb), lambda j: (0, 0, j)),
                   pl.BlockSpec((8, cb), lambda j: (0, j)), pl.BlockSpec((1, cb), lambda j: (0, j))],
        out_shape=[jax.ShapeDtypeStruct((2, t, f), BF16), jax.ShapeDtypeStruct((8, f), F32),
                   jax.ShapeDtypeStruct((1, f), F32)],
        scratch_shapes=[pltpu.VMEM((t + 8, cb), F32), pltpu.VMEM((t + 8, cb), F32)],
        compiler_params=_cparams(("parallel",)),
    )(dact, hf, hf, w, b)


def _silu_grad(z, sg):
    return sg * (1.0 + z * (1.0 - sg))


def _conv_fwd(name, hin, w, b, ng, nb_):
    t = hin.shape[0]
    c = GROUP
    tm = CONV_ROWS
    pad = 32
    shifts = tuple(2 + k for k in range(CONV_KERNEL))

    def body(a_ref, gt_ref, w_ref, b_ref, ng_ref, nb_ref, u1_ref, u3_ref, pad_ref):
        pad_ref[pl.ds(0, pad), :] = jnp.zeros((pad, c), F32)
        pad_ref[pl.ds(pad, t), :] = a_ref[...] * _sigmoid(gt_ref[...])
        bias, gam, bet = b_ref[...], ng_ref[...], nb_ref[...]

        def tile(i, carry):
            r0 = pl.multiple_of(i * tm, tm)
            win = pad_ref[pl.ds(r0, tm + pad), :]
            acc = [jnp.broadcast_to(bias, (tm, c))]

            def tap(s, rows):
                acc[0] = acc[0] + w_ref[pl.ds(s - 2, 1), :] * rows

            _for_shifted(win, tm, shifts, tap)
            u1 = acc[0]
            u1_ref[pl.ds(r0, tm), :] = u1
            xhat, _ = _ln_stats(u1)
            u2 = xhat * gam + bet
            u3_ref[pl.ds(r0, tm), :] = (u2 * _sigmoid(u2)).astype(BF16)
            return carry

        lax.fori_loop(0, t // tm, tile, 0)

    vec = pl.BlockSpec((1, c), lambda j: (0, j))
    return pl.pallas_call(
        body, name=name, grid=(N_GROUPS,),
        in_specs=[_col_spec(t, c), _col_spec(t, c, N_GROUPS),
                  pl.BlockSpec((CONV_KERNEL, c), lambda j: (0, j)), vec, vec, vec],
        out_specs=[_col_spec(t, c), _col_spec(t, c)],
        out_shape=[jax.ShapeDtypeStruct((t, CONV_WIDTH), F32), jax.ShapeDtypeStruct((t, CONV_WIDTH), BF16)],
        scratch_shapes=[pltpu.VMEM((t + pad, c), F32)],
        compiler_params=_cparams(("parallel",)),
    )(hin, hin, w, b, ng, nb_)


def _conv_bwd(name, dcat, u1, hin, w, ng, nb_):
    t = hin.shape[0]
    c = GROUP
    tm = CONV_ROWS
    pad = 32
    nk = CONV_KERNEL

    def body(du3_ref, u1_ref, a_ref, gt_ref, w_ref, ng_ref, nb_ref,
             da_ref, dgt_ref, dw_ref, db_ref, dng_ref, dnb_ref, u0_ref, du1_ref, dwp_ref):
        u0_ref[pl.ds(0, pad), :] = jnp.zeros((pad, c), F32)
        u0_ref[pl.ds(pad, t), :] = a_ref[...] * _sigmoid(gt_ref[...])
        du1_ref[pl.ds(t, pad), :] = jnp.zeros((pad, c), F32)
        dwp_ref[...] = jnp.zeros_like(dwp_ref)
        gam, bet = ng_ref[...], nb_ref[...]

        def tile_a(i, carry):
            r0 = pl.multiple_of(i * tm, tm)
            u1 = u1_ref[pl.ds(r0, tm), :]
            xhat, rstd = _ln_stats(u1)
            u2 = xhat * gam + bet
            sg = _sigmoid(u2)
            du2 = du3_ref[pl.ds(r0, tm), :] * _silu_grad(u2, sg)
            dxhat = du2 * gam
            m1 = jnp.mean(dxhat, axis=-1, keepdims=True)
            m2 = jnp.mean(dxhat * xhat, axis=-1, keepdims=True)
            du1 = rstd * (dxhat - m1 - xhat * m2)
            du1_ref[pl.ds(r0, tm), :] = du1
            sums = (jnp.sum(du1, axis=0, keepdims=True), jnp.sum(du2 * xhat, axis=0, keepdims=True),
                    jnp.sum(du2, axis=0, keepdims=True))
            return tuple(x + s for x, s in zip(carry, sums))

        zero = jnp.zeros((1, c), F32)
        dbias, dgam, dbet = lax.fori_loop(0, t // tm, tile_a, (zero, zero, zero))
        db_ref[...] = dbias
        dng_ref[...] = dgam
        dnb_ref[...] = dbet

        def tile_b(i, carry):
            r0 = pl.multiple_of(i * tm, tm)
            du1 = du1_ref[pl.ds(r0, tm), :]
            acc = [jnp.zeros((tm, c), F32)]

            def tap_dx(s, rows):
                acc[0] = acc[0] + w_ref[pl.ds(nk - 1 - s, 1), :] * rows

            _for_shifted(du1_ref[pl.ds(r0, tm + pad), :], tm, tuple(range(nk)), tap_dx)

            def tap_dw(s, rows):
                part = (du1 * rows).reshape(tm // 8, 8, c).sum(axis=0)
                dwp_ref[s - 2] = dwp_ref[s - 2] + part

            _for_shifted(u0_ref[pl.ds(r0, tm + pad), :], tm, tuple(2 + k for k in range(nk)), tap_dw)
            du0 = acc[0]
            a = a_ref[pl.ds(r0, tm), :]
            sg = _sigmoid(gt_ref[pl.ds(r0, tm), :])
            da_ref[pl.ds(r0, tm), :] = (du0 * sg).astype(BF16)
            dgt_ref[pl.ds(r0, tm), :] = (du0 * a * sg * (1.0 - sg)).astype(BF16)
            return carry

        lax.fori_loop(0, t // tm, tile_b, 0)
        dw_ref[...] = jnp.sum(dwp_ref[...], axis=1)

    vec = pl.BlockSpec((1, c), lambda j: (0, j))
    vshape = jax.ShapeDtypeStruct((1, CONV_WIDTH), F32)
    return pl.pallas_call(
        body, name=name, grid=(N_GROUPS,),
        in_specs=[_col_spec(t, c), _col_spec(t, c), _col_spec(t, c), _col_spec(t, c, N_GROUPS),
                  pl.BlockSpec((nk, c), lambda j: (0, j)), vec, vec],
        out_specs=[_col_spec(t, c), _col_spec(t, c), pl.BlockSpec((32, c), lambda j: (0, j)), vec, vec, vec],
        out_shape=[jax.ShapeDtypeStruct((t, CONV_WIDTH), BF16), jax.ShapeDtypeStruct((t, CONV_WIDTH), BF16),
                   jax.ShapeDtypeStruct((32, CONV_WIDTH), F32), vshape, vshape, vshape],
        scratch_shapes=[pltpu.VMEM((t + pad, c), F32), pltpu.VMEM((t + pad, c), F32),
                        pltpu.VMEM((32, 8, c), F32)],
        compiler_params=_cparams(("parallel",)),
    )(dcat, u1, hin, hin, w, ng, nb_)


LEVELS = (64, 32, 16)
NT_DIMS = (((1,), (1,)), ((), ()))
NN_DIMS = (((1,), (0,)), ((), ()))
TN_DIMS = (((0,), (0,)), ((), ()))


def _bdot(a, b, dims):
    return lax.dot_general(a.astype(BF16), b.astype(BF16), dims, preferred_element_type=F32)


def _hdot(a, b):
    return jnp.dot(a, b, precision=lax.Precision.HIGHEST, preferred_element_type=F32)


def _chunk_consts():
    rid = lax.broadcasted_iota(jnp.int32, (CHUNK, GROUP), 0)
    ti = lax.broadcasted_iota(jnp.int32, (CHUNK, CHUNK), 0)
    si = lax.broadcasted_iota(jnp.int32, (CHUNK, CHUNK), 1)
    tri = (si <= ti).astype(F32)
    second = [(rid & (b // 2)) != 0 for b in LEVELS]
    same = [None] + [(ti // b) == (si // b) for b in LEVELS[1:]]
    sub = lax.broadcasted_iota(jnp.int32, (SUB, GROUP), 0)
    return rid, tri, second, same, sub


def _level_refs(cum_ref, rid):
    row = lambda i: cum_ref[pl.ds(i, 1), :]
    l1 = jnp.broadcast_to(row(31), (CHUNK, GROUP))
    l2 = jnp.where(rid < 32, row(15), row(47))
    l3 = jnp.where(rid < 16, row(7), jnp.where(rid < 32, row(23), jnp.where(rid < 48, row(39), row(55))))
    return l1, l2, l3


def _level_factors(cum, brefs, second):
    out = []
    for bref, sec in zip(brefs, second):
        eq = jnp.where(sec, jnp.exp(jnp.minimum(cum - bref, 0.0)), 0.0)
        ek = jnp.where(sec, 0.0, jnp.exp(jnp.minimum(bref - cum, 0.0)))
        out.append((eq, ek))
    return out


def _gates(q, f, lb):
    sq = _sigmoid(q)
    sf = _sigmoid(f)
    fg = lb + (1.0 - lb) * sf
    return q * sq, sq, sf, fg


def _hgrn_specs(t, nc):
    c = GROUP
    col = lambda off: pl.BlockSpec((t, c), lambda h: (0, h + off))
    hin_specs = [col(16), col(24), col(32), col(40)]
    vec = pl.BlockSpec((1, c), lambda h: (0, h))
    lbs = pl.BlockSpec((2, c), lambda h: (0, h))
    st = pl.BlockSpec((1, nc, c, c), lambda h: (h, 0, 0, 0))
    return col, hin_specs, vec, lbs, st


def _hgrn_fwd(name, hin, lb_logits, hg):
    t = hin.shape[0]
    nc = t // CHUNK
    c = GROUP
    col, hin_specs, vec, lbs, st = _hgrn_specs(t, nc)

    def body(q_ref, f_ref, v_ref, og_ref, lb_ref, hg_ref, o_ref, ob_ref, st_ref,
             s_ref, cum_ref, kk_ref, vc_ref):
        rid, tri, second, same, sub = _chunk_consts()
        lb = _sigmoid(lb_ref[pl.ds(0, 1), :] - lb_ref[pl.ds(1, 1), :])
        gain = hg_ref[...]
        s_ref[...] = jnp.zeros_like(s_ref)

        def chunk(ci, carry):
            r0 = pl.multiple_of(ci * CHUNK, CHUNK)
            rows = pl.ds(r0, CHUNK)
            qh, _, _, fg = _gates(q_ref[rows, :], f_ref[rows, :], lb)
            v = v_ref[rows, :]
            kk = 1.0 - fg
            cum = _hdot(tri, jnp.log(fg))
            cum_ref[...] = cum
            kk_ref[...] = kk
            vc_ref[...] = v
            sprev = s_ref[...]
            st_ref[0, ci] = sprev
            blast = cum_ref[pl.ds(CHUNK - 1, 1), :]
            o = _bdot(qh * jnp.exp(cum), sprev, NT_DIMS)
            s_ref[...] = sprev * jnp.exp(blast) + _bdot(v, kk * jnp.exp(blast - cum), TN_DIMS)
            a = None
            for (eq, ek), msk in zip(_level_factors(cum, _level_refs(cum_ref, rid), second), same):
                al = _bdot(qh * eq, kk * ek, NT_DIMS)
                al = al if msk is None else jnp.where(msk, al, 0.0)
                a = al if a is None else a + al
            o = o + _bdot(a, v, NN_DIMS)
            diag = []
            for sb in range(CHUNK // SUB):
                lo = sb * SUB
                qb = qh[lo:lo + SUB]
                cb = cum[lo:lo + SUB]
                od = jnp.zeros((SUB, c), F32)
                for s in range(SUB):
                    e = jnp.where(sub >= s, jnp.exp(jnp.minimum(cb - cum_ref[pl.ds(lo + s, 1), :], 0.0)), 0.0)
                    acol = jnp.sum(qb * e * kk_ref[pl.ds(lo + s, 1), :], axis=-1, keepdims=True)
                    od = od + acol * vc_ref[pl.ds(lo + s, 1), :]
                diag.append(od)
            o = o + jnp.concatenate(diag, axis=0)
            o_ref[rows, :] = o
            y = o * lax.rsqrt(jnp.mean(o * o, axis=-1, keepdims=True) + RMS_EPS) * gain
            og = og_ref[rows, :]
            ob_ref[rows, :] = (y * og * _sigmoid(og)).astype(BF16)
            return carry

        lax.fori_loop(0, nc, chunk, 0)

    return pl.pallas_call(
        body, name=name, grid=(N_GROUPS,),
        in_specs=hin_specs + [lbs, vec],
        out_specs=[col(0), col(0), st],
        out_shape=[jax.ShapeDtypeStruct((t, HGRN_WIDTH), F32), jax.ShapeDtypeStruct((t, HGRN_WIDTH), BF16),
                   jax.ShapeDtypeStruct((N_GROUPS, nc, c, c), F32)],
        scratch_shapes=[pltpu.VMEM((c, c), F32), pltpu.VMEM((CHUNK, c), F32), pltpu.VMEM((CHUNK, c), F32),
                        pltpu.VMEM((CHUNK, c), F32)],
        compiler_params=_cparams(("parallel",)),
    )(hin, hin, hin, hin, lb_logits, hg)


def _hgrn_bwd(name, dcat, hin, o_raw, states, lb_logits, hg):
    t = hin.shape[0]
    nc = t // CHUNK
    c = GROUP
    col, hin_specs, vec, lbs, st = _hgrn_specs(t, nc)

    def body(do_ref, q_ref, f_ref, v_ref, og_ref, o_ref, st_ref, lb_ref, hg_ref,
             dq_ref, df_ref, dv_ref, dog_ref, dhg_ref, dlb_ref,
             ds_ref, cum_ref, kk_ref, vc_ref):
        rid, tri, second, same, sub = _chunk_consts()
        trit = tri.T
        lb = _sigmoid(lb_ref[pl.ds(0, 1), :] - lb_ref[pl.ds(1, 1), :])
        gain = hg_ref[...]
        ds_ref[...] = jnp.zeros_like(ds_ref)

        def chunk(i, carry):
            dhg, dlb = carry
            ci = nc - 1 - i
            r0 = pl.multiple_of(ci * CHUNK, CHUNK)
            rows = pl.ds(r0, CHUNK)
            q = q_ref[rows, :]
            qh, sq, sf, fg = _gates(q, f_ref[rows, :], lb)
            v = v_ref[rows, :]
            kk = 1.0 - fg
            cum = _hdot(tri, jnp.log(fg))
            cum_ref[...] = cum
            kk_ref[...] = kk
            vc_ref[...] = v
            o = o_ref[rows, :]
            og = og_ref[rows, :]
            sg = _sigmoid(og)
            rinv = lax.rsqrt(jnp.mean(o * o, axis=-1, keepdims=True) + RMS_EPS)
            yn = o * rinv
            dof = do_ref[rows, :]
            dog_ref[rows, :] = (dof * yn * gain * _silu_grad(og, sg)).astype(BF16)
            dz = dof * og * sg
            dhg = dhg + jnp.sum(dz * yn, axis=0, keepdims=True)
            dy = dz * gain
            do = rinv * (dy - yn * jnp.mean(dy * yn, axis=-1, keepdims=True))
            sprev = st_ref[0, ci]
            dsn = ds_ref[...]
            blast = cum_ref[pl.ds(CHUNK - 1, 1), :]
            eq0 = jnp.exp(cum)
            ek0 = jnp.exp(blast - cum)
            dqh = _bdot(do, sprev, NN_DIMS) * eq0
            dkk = _bdot(v, dsn, NN_DIMS) * ek0
            dlast = (jnp.sum(kk * dkk, axis=0, keepdims=True)
                     + jnp.exp(blast) * jnp.sum(dsn * sprev, axis=0, keepdims=True))
            dv = _bdot(kk * ek0, dsn, NT_DIMS)
            ds_ref[...] = dsn * jnp.exp(blast) + _bdot(do, qh * eq0, TN_DIMS)
            dg = qh * dqh - kk * dkk
            da = _bdot(do, v, NT_DIMS)
            a = None
            for (eq, ek), msk in zip(_level_factors(cum, _level_refs(cum_ref, rid), second), same):
                ql, kl = (qh * eq).astype(BF16), (kk * ek).astype(BF16)
                al = _bdot(ql, kl, NT_DIMS)
                dal = da
                if msk is not None:
                    al = jnp.where(msk, al, 0.0)
                    dal = jnp.where(msk, da, 0.0)
                a = al if a is None else a + al
                dql = _bdot(dal, kl, NN_DIMS)
                dkl = _bdot(dal, ql, TN_DIMS)
                dqh = dqh + dql * eq
                dkk = dkk + dkl * ek
                dg = dg + (ql.astype(F32) * dql - kl.astype(F32) * dkl)
            dv = dv + _bdot(a, do, TN_DIMS)
            dq_d, dk_d, dv_d = [], [], []
            for sb in range(CHUNK // SUB):
                lo = sb * SUB
                qb = qh[lo:lo + SUB]
                cb = cum[lo:lo + SUB]
                dob = do[lo:lo + SUB]
                dqb = jnp.zeros((SUB, c), F32)
                dkb = jnp.zeros((SUB, c), F32)
                dvb = jnp.zeros((SUB, c), F32)
                for s in range(SUB):
                    e = jnp.where(sub >= s, jnp.exp(jnp.minimum(cb - cum_ref[pl.ds(lo + s, 1), :], 0.0)), 0.0)
                    ks = kk_ref[pl.ds(lo + s, 1), :]
                    qe = qb * e
                    dacol = jnp.sum(dob * vc_ref[pl.ds(lo + s, 1), :], axis=-1, keepdims=True)
                    acol = jnp.sum(qe * ks, axis=-1, keepdims=True)
                    dqb = dqb + dacol * (ks * e)
                    dkb = jnp.where(sub == s, jnp.sum(dacol * qe, axis=0, keepdims=True), dkb)
                    dvb = jnp.where(sub == s, jnp.sum(acol * dob, axis=0, keepdims=True), dvb)
                dq_d.append(dqb)
                dk_d.append(dkb)
                dv_d.append(dvb)
            dq_d = jnp.concatenate(dq_d, axis=0)
            dk_d = jnp.concatenate(dk_d, axis=0)
            dqh = dqh + dq_d
            dkk = dkk + dk_d
            dg = dg + (qh * dq_d - kk * dk_d)
            dv = dv + jnp.concatenate(dv_d, axis=0)
            dlf = _hdot(trit, dg) + dlast
            dfg = dlf / fg - dkk
            df_ref[rows, :] = (dfg * (1.0 - lb) * sf * (1.0 - sf)).astype(BF16)
            dlb = dlb + jnp.sum(dfg * (1.0 - sf), axis=0, keepdims=True)
            dq_ref[rows, :] = (dqh * _silu_grad(q, sq)).astype(BF16)
            dv_ref[rows, :] = dv.astype(BF16)
            return dhg, dlb

        zero = jnp.zeros((1, c), F32)
        dhg, dlb = lax.fori_loop(0, nc, chunk, (zero, zero))
        dhg_ref[...] = dhg
        dl0 = dlb * lb * (1.0 - lb)
        dlb_ref[...] = jnp.where(lax.broadcasted_iota(jnp.int32, (2, c), 0) == 0, dl0, -dl0)

    big = jax.ShapeDtypeStruct((t, HGRN_WIDTH), BF16)
    return pl.pallas_call(
        body, name=name, grid=(N_GROUPS,),
        in_specs=[col(8)] + hin_specs + [col(0), st, lbs, vec],
        out_specs=[col(0)] * 4 + [vec, lbs],
        out_shape=[big] * 4 + [jax.ShapeDtypeStruct((1, HGRN_WIDTH), F32), jax.ShapeDtypeStruct((2, HGRN_WIDTH), F32)],
        scratch_shapes=[pltpu.VMEM((c, c), F32)] + [pltpu.VMEM((CHUNK, c), F32)] * 3,
        compiler_params=_cparams(("parallel",)),
    )(dcat, hin, hin, hin, hin, o_raw, states, lb_logits, hg)


ANY = pl.BlockSpec(memory_space=pl.ANY)


def _my_place():
    return lax.axis_index("x"), lax.axis_index("y"), lax.axis_index("c")


def _all_gather(name, shard):
    def body(x_ref, out_ref, send_sems, recv_sems, local_sem):
        x, y, c = _my_place()
        me, sibling = (x, y, c), (x, y, 1 - c)
        chips = [(1 - x, y), (x, 1 - y), (1 - x, 1 - y)]

        def slot(px, py, pc):
            return out_ref.at[4 * px + 2 * py + pc]

        def copy(k, block, to, src=None):
            return pltpu.make_async_remote_copy(
                src_ref=slot(*block) if src is None else src, dst_ref=slot(*block),
                send_sem=send_sems.at[k], recv_sem=recv_sems.at[k], device_id=to, device_id_type=MESH)

        mine = pltpu.make_async_copy(x_ref, slot(*me), local_sem)
        mine.start()
        first = [copy(0, me, sibling, src=x_ref)]
        first += [copy(1 + j, me, (*chip, c), src=x_ref) for j, chip in enumerate(chips)]
        for cp in first:
            cp.start()
        passed = [copy(4 + j, (*chip, c), sibling) for j, chip in enumerate(chips)]
        for j, chip in enumerate(chips):
            copy(1 + j, (*chip, c), me).wait_recv()
            passed[j].start()
        copy(0, sibling, me).wait_recv()
        for j, chip in enumerate(chips):
            copy(4 + j, (*chip, 1 - c), me).wait_recv()
        for cp in first + passed:
            cp.wait_send()
        mine.wait()

    return pl.pallas_call(
        body, name=name, out_shape=jax.ShapeDtypeStruct((N_DEV,) + shard.shape, shard.dtype),
        in_specs=[ANY], out_specs=ANY,
        scratch_shapes=[pltpu.SemaphoreType.DMA((7,)), pltpu.SemaphoreType.DMA((7,)), pltpu.SemaphoreType.DMA],
    )(shard)


def _all_to_all(name, parts):
    def body(x_ref, out_ref, send_sems, recv_sems, local_sem):
        x, y, c = _my_place()
        me = 4 * x + 2 * y + c
        mine = pltpu.make_async_copy(x_ref.at[me], out_ref.at[me], local_sem)
        mine.start()

        def peer(k):
            px = 1 - x if k & 4 else x
            py = 1 - y if k & 2 else y
            pc = 1 - c if k & 1 else c
            return (px, py, pc), 4 * px + 2 * py + pc

        sends = []
        for k in range(1, N_DEV):
            to, idx = peer(k)
            cp = pltpu.make_async_remote_copy(
                src_ref=x_ref.at[idx], dst_ref=out_ref.at[me], send_sem=send_sems.at[k - 1],
                recv_sem=recv_sems.at[k - 1], device_id=to, device_id_type=MESH)
            cp.start()
            sends.append(cp)
        for k in range(1, N_DEV):
            to, idx = peer(k)
            pltpu.make_async_remote_copy(
                src_ref=x_ref.at[idx], dst_ref=out_ref.at[idx], send_sem=send_sems.at[k - 1],
                recv_sem=recv_sems.at[k - 1], device_id=to, device_id_type=MESH).wait_recv()
        for cp in sends:
            cp.wait_send()
        mine.wait()

    return pl.pallas_call(
        body, name=name, out_shape=jax.ShapeDtypeStruct(parts.shape, parts.dtype),
        in_specs=[ANY], out_specs=ANY,
        scratch_shapes=[pltpu.SemaphoreType.DMA((7,)), pltpu.SemaphoreType.DMA((7,)), pltpu.SemaphoreType.DMA],
    )(parts)


def _adamw_math(w, g, m, v):
    m = ADAM_B1 * m + (1.0 - ADAM_B1) * g
    v = ADAM_B2 * v + (1.0 - ADAM_B2) * (g * g)
    m_hat = m / (1.0 - ADAM_B1 ** ADAM_STEP)
    v_hat = v / (1.0 - ADAM_B2 ** ADAM_STEP)
    delta = -ADAM_LR * (m_hat / (jnp.sqrt(v_hat) + ADAM_EPS) + ADAM_WD * w)
    return delta, m, v


def _adamw_sum(name, recv, w, m, v, tr):
    r, c = w.shape

    def body(recv_ref, w_ref, m_ref, v_ref, g_ref, d_ref, mo_ref, vo_ref):
        g = recv_ref[0].astype(F32)
        for j in range(1, N_DEV):
            g = g + recv_ref[j].astype(F32)
        g_ref[...] = g
        d_ref[...], mo_ref[...], vo_ref[...] = _adamw_math(w_ref[...], g, m_ref[...], v_ref[...])

    tile = pl.BlockSpec((tr, c), lambda i: (i, 0))
    out = jax.ShapeDtypeStruct((r, c), F32)
    return pl.pallas_call(
        body, name=name, grid=(r // tr,),
        in_specs=[pl.BlockSpec((N_DEV, tr, c), lambda i: (0, i, 0)), tile, tile, tile],
        out_specs=[tile] * 4, out_shape=[out] * 4,
        compiler_params=_cparams(("parallel",)),
    )(recv, w, m, v)


def _sum_parts(name, parts):
    _, r, c = parts.shape

    def body(p_ref, o_ref):
        acc = p_ref[0]
        for j in range(1, N_DEV):
            acc = acc + p_ref[j]
        o_ref[...] = acc

    return pl.pallas_call(body, name=name, out_shape=jax.ShapeDtypeStruct((r, c), F32),
                          compiler_params=_cparams())(parts)


def _adamw_small(name, w, g, m, v):
    def body(w_ref, g_ref, m_ref, v_ref, d_ref, mo_ref, vo_ref):
        d_ref[...], mo_ref[...], vo_ref[...] = _adamw_math(w_ref[...], g_ref[...], m_ref[...], v_ref[...])

    out = jax.ShapeDtypeStruct(w.shape, F32)
    return pl.pallas_call(body, name=name, out_shape=[out] * 3, compiler_params=_cparams())(w, g, m, v)


def _pack(pieces, rows):
    flat = jnp.concatenate([p.reshape(-1).astype(F32) for p in pieces])
    return jnp.pad(flat, (0, rows * 128 - flat.shape[0])).reshape(rows, 128)


def _unpack(packed, shapes):
    flat = packed.reshape(-1)
    out, off = [], 0
    for s in shapes:
        n = 1
        for d in s:
            n *= d
        out.append(flat[off:off + n].reshape(s))
        off += n
    return out


def kernel(x, emb_ln_g, emb_ln_b, w_in, conv_w, conv_b, conv_norm_g, conv_norm_b, lb_logits, hgrn_norm_g, w_out, ln1_g, ln1_b, w_ffn_up, ffn_conv_w, ffn_conv_b, w_ffn_down, ln2_g, ln2_b, loss_target, m_emb_ln_g, m_emb_ln_b, m_w_in, m_conv_w, m_conv_b, m_conv_norm_g, m_conv_norm_b, m_lb_logits, m_hgrn_norm_g, m_w_out, m_ln1_g, m_ln1_b, m_w_ffn_up, m_ffn_conv_w, m_ffn_conv_b, m_w_ffn_down, m_ln2_g, m_ln2_b, v_emb_ln_g, v_emb_ln_b, v_w_in, v_conv_w, v_conv_b, v_conv_norm_g, v_conv_norm_b, v_lb_logits, v_hgrn_norm_g, v_w_out, v_ln1_g, v_ln1_b, v_w_ffn_up, v_ffn_conv_w, v_ffn_conv_b, v_w_ffn_down, v_ln2_g, v_ln2_b):
    t = x.shape[1]
    me = 4 * lax.axis_index("x") + 2 * lax.axis_index("y") + lax.axis_index("c")
    x2, tgt = x[0], loss_target[0]
    ns_in, ns_up = w_in.shape[2], w_ffn_up.shape[2]
    rs_out, rs_down = w_out.shape[1], w_ffn_down.shape[1]
    cs, fs = conv_w.shape[2], ffn_conv_w.shape[2]
    tk = min(512, t)

    win_g = _all_gather("ag_w_in", w_in[0].astype(BF16))
    wout_g = _all_gather("ag_w_out", w_out[0].astype(BF16)).reshape(D_MODEL, D_MODEL)
    wup_g = _all_gather("ag_w_up", w_ffn_up[0].astype(BF16))
    wdown_g = _all_gather("ag_w_down", w_ffn_down[0].astype(BF16)).reshape(D_FF, D_MODEL)
    n_cw, n_fw = CONV_KERNEL * cs, FFN_KERNEL * fs
    taps_g = _all_gather("ag_taps", _pack([conv_w[0], ffn_conv_w[0]], 48)).reshape(N_DEV, -1)
    cw_full = taps_g[:, :n_cw].reshape(N_DEV, CONV_KERNEL, cs).transpose(1, 0, 2).reshape(CONV_KERNEL, CONV_WIDTH)
    fw_full = taps_g[:, n_cw:n_cw + n_fw].reshape(N_DEV, FFN_KERNEL, fs).transpose(1, 0, 2).reshape(FFN_KERNEL, D_FF)

    row = lambda a: a.reshape(1, -1)

    _, h0, h0b = _ln_fwd("ln_in", x2, None, row(emb_ln_g), row(emb_ln_b), 1.0)
    hin = _mm_nn_cols("mm_in", h0b, win_g, F32)
    u1, u3b = _conv_fwd("conv_fwd", hin, cw_full, conv_b, conv_norm_g, conv_norm_b)
    o_raw, ob, states = _hgrn_fwd("hgrn_fwd", hin, lb_logits, hgrn_norm_g)
    catb = jnp.concatenate([u3b, ob], axis=1)
    mix = _mm_nn("mm_out", catb, wout_g, F32)
    r1, h1, h1b = _ln_fwd("ln1", h0, mix, ln1_g, ln1_b, ALPHA)
    hf = _mm_nn_cols("mm_up", h1b, wup_g, F32)
    actb = _ffn_act_fwd("ffn_act", hf, fw_full, ffn_conv_b)
    ffn = _mm_nn("mm_down", actb, wdown_g, F32)
    dr2, dr2b, g_ln2g, g_ln2b, loss = _ln2_loss_bwd("ln2_loss", h1, ffn, ln2_g, ln2_b, tgt)

    dact = _mm_nt("mm_dact", dr2b, wdown_g, F32, tn=1408)
    gw_down = _mm_nn("mm_dw_down", actb.T, dr2b, BF16, tm=rs_down, tn=D_MODEL, tk=tk)
    dhf, g_fw, g_fb = _ffn_act_bwd("ffn_act_bwd", dact, hf, fw_full, ffn_conv_b)
    tm = min(1024, t)
    dh1 = _matmul(
        "mm_dh1", dhf, wup_g, (t, D_MODEL), F32, (t // tm, D_MODEL // 1024, N_DEV),
        pl.BlockSpec((1, tm, ns_up), lambda i, j, kk: (kk // 4, i, kk % 4)),
        pl.BlockSpec((1, 1024, ns_up), lambda i, j, kk: (kk, j, 0)),
        pl.BlockSpec((tm, 1024), lambda i, j, kk: (i, j)), nt=True)
    gw_up = _matmul(
        "mm_dw_up", h1b.T, dhf, (N_DEV, D_MODEL, ns_up), BF16, (D_MODEL // 1024, N_DEV, t // tk),
        pl.BlockSpec((1024, tk), lambda i, j, kk: (i, kk)),
        pl.BlockSpec((1, tk, ns_up), lambda i, j, kk: (j // 4, kk, j % 4)),
        pl.BlockSpec((1, 1024, ns_up), lambda i, j, kk: (j, i, 0)), nt=False)
    dr1, dr1b, g_ln1g, g_ln1b = _ln_bwd("ln1_bwd", r1, dr2, dh1, ln1_g, ALPHA, True)
    dcat = _mm_nt("mm_dcat", dr1b, wout_g, F32)
    gw_out = _mm_nn("mm_dw_out", catb.T, dr1b, BF16, tm=1024, tn=D_MODEL, tk=tk)
    da, dgate, g_cw, g_cb, g_cng, g_cnb = _conv_bwd("conv_bwd", dcat, u1, hin, cw_full, conv_norm_g, conv_norm_b)
    dq, df, di, dog, g_hg, g_lb = _hgrn_bwd("hgrn_bwd", dcat, hin, o_raw, states, lb_logits, hgrn_norm_g)
    dhin = jnp.concatenate([da, dgate, dq, df, di, dog], axis=1)
    dh0 = _mm_nt_cols("mm_dh0", dhin, win_g, F32)
    gw_in = _mm_grad_cols("mm_dw_in", h0b.T, dhin, ns_in, tk=tk)
    grad_x, g_eg, g_eb = _ln_bwd("ln_in_bwd", x2, dr1, dh0, row(emb_ln_g), ALPHA, False)

    def big(name, parts, w, m, v, tr):
        recv = _all_to_all("a2a_" + name, parts)
        return [o[None] for o in _adamw_sum("adamw_" + name, recv, w[0], m[0], v[0], tr)]

    u_in = big("w_in", gw_in, w_in, m_w_in, v_w_in, 128)
    u_out = big("w_out", gw_out.reshape(N_DEV, rs_out, D_MODEL), w_out, m_w_out, v_w_out, 64)
    u_up = big("w_up", gw_up, w_ffn_up, m_w_ffn_up, v_w_ffn_up, 64)
    u_down = big("w_down", gw_down.reshape(N_DEV, rs_down, D_MODEL), w_ffn_down, m_w_ffn_down, v_w_ffn_down, 64)

    small_shapes = [(D_MODEL,), (D_MODEL,), (CONV_KERNEL, CONV_WIDTH), (1, CONV_WIDTH), (1, CONV_WIDTH),
                    (1, CONV_WIDTH), (2, HGRN_WIDTH), (1, HGRN_WIDTH), (1, D_MODEL), (1, D_MODEL),
                    (FFN_KERNEL, D_FF), (1, D_FF), (1, D_MODEL), (1, D_MODEL), (128,)]
    rows_small = 569
    packed = _pack([g_eg, g_eb, g_cw[:CONV_KERNEL], g_cb, g_cng, g_cnb, g_lb, g_hg, g_ln1g, g_ln1b,
                    g_fw[:FFN_KERNEL], g_fb, g_ln2g, g_ln2b, loss], rows_small)
    summed = _sum_parts("sum_small", _all_gather("ag_small", packed))
    (s_eg, s_eb, s_cw, s_cb, s_cng, s_cnb, s_lb, s_hg, s_l1g, s_l1b, s_fw, s_fb, s_l2g, s_l2b,
     s_loss) = _unpack(summed, small_shapes)
    s_cw = lax.dynamic_slice_in_dim(s_cw, me * cs, cs, axis=1)[None]
    s_fw = lax.dynamic_slice_in_dim(s_fw, me * fs, fs, axis=1)[None]
    g_small = [s_eg, s_eb, s_cw, s_cb, s_cng, s_cnb, s_lb, s_hg, s_l1g, s_l1b, s_fw, s_fb, s_l2g, s_l2b]
    w_small = [emb_ln_g, emb_ln_b, conv_w, conv_b, conv_norm_g, conv_norm_b, lb_logits, hgrn_norm_g,
               ln1_g, ln1_b, ffn_conv_w, ffn_conv_b, ln2_g, ln2_b]
    m_small = [m_emb_ln_g, m_emb_ln_b, m_conv_w, m_conv_b, m_conv_norm_g, m_conv_norm_b, m_lb_logits,
               m_hgrn_norm_g, m_ln1_g, m_ln1_b, m_ffn_conv_w, m_ffn_conv_b, m_ln2_g, m_ln2_b]
    v_small = [v_emb_ln_g, v_emb_ln_b, v_conv_w, v_conv_b, v_conv_norm_g, v_conv_norm_b, v_lb_logits,
               v_hgrn_norm_g, v_ln1_g, v_ln1_b, v_ffn_conv_w, v_ffn_conv_b, v_ln2_g, v_ln2_b]
    rows_own = 236
    shapes_own = [w.shape for w in w_small]
    upd = _adamw_small("adamw_small", _pack(w_small, rows_own), _pack(g_small, rows_own),
                       _pack(m_small, rows_own), _pack(v_small, rows_own))
    d_small, nm_small, nv_small = (_unpack(u, shapes_own) for u in upd)
    g_small = [g.reshape(s) for g, s in zip(g_small, shapes_own)]

    def ordered(small, i_in, i_out, i_up, i_down):
        (eg, eb, cw, cb, cng, cnb, lb, hg, l1g, l1b, fw, fb, l2g, l2b) = small
        return [eg, eb, i_in, cw, cb, cng, cnb, lb, hg, i_out, l1g, l1b, i_up, fw, fb, i_down, l2g, l2b]

    outs = [s_loss[0], grad_x[None]]
    for k, small in enumerate([g_small, d_small, nm_small, nv_small]):
        outs += ordered(small, u_in[k], u_out[k], u_up[k], u_down[k])
    return tuple(outs)
```

```python
import functools

import jax
import jax.numpy as jnp
from jax import lax
from jax.experimental import pallas as pl
from jax.experimental.pallas import tpu as pltpu

F32 = jnp.float32
BF16 = jnp.bfloat16

N_DEV = 8
D_MODEL = 2048
CONV_WIDTH = 1024
CONV_KERNEL = 31
HGRN_WIDTH = 1024
GROUP = 128
N_GROUPS = 8
IN_PROJ = 2 * CONV_WIDTH + 4 * HGRN_WIDTH
D_FF = 5632
FFN_KERNEL = 3
CHUNK = 64
SUB = 8
LN_EPS = 1e-5
RMS_EPS = 1e-6
ALPHA = 2.0 ** 0.25
ADAM_LR, ADAM_B1, ADAM_B2, ADAM_EPS, ADAM_WD, ADAM_STEP = 0.001, 0.9, 0.999, 1e-08, 0.01, 10

VMEM_LIMIT = 56 * 1024 * 1024
MESH = pl.DeviceIdType.MESH


def _cparams(sem=None):
    return pltpu.CompilerParams(dimension_semantics=sem, vmem_limit_bytes=VMEM_LIMIT)


def _sigmoid(x):
    return 1.0 / (1.0 + jnp.exp(-x))


def _matmul(name, a, b, out_shape, out_dtype, grid, a_spec, b_spec, o_spec, nt):
    nk = grid[2]
    dims = (((1,), (1,)), ((), ())) if nt else (((1,), (0,)), ((), ()))

    def body(a_ref, b_ref, o_ref, acc_ref):
        k = pl.program_id(2)

        @pl.when(k == 0)
        def _():
            acc_ref[...] = jnp.zeros_like(acc_ref)

        av = a_ref[0] if len(a_ref.shape) == 3 else a_ref[...]
        bv = b_ref[0] if len(b_ref.shape) == 3 else b_ref[...]
        acc_ref[...] += lax.dot_general(av, bv, dims, preferred_element_type=F32)

        @pl.when(k == nk - 1)
        def _():
            res = acc_ref[...].astype(out_dtype)
            if len(o_ref.shape) == 3:
                o_ref[0] = res
            else:
                o_ref[...] = res

    acc_shape = o_spec.block_shape[-2:]
    assert all(g >= 1 for g in grid), (name, grid)
    return pl.pallas_call(
        body, name=name, grid=grid, in_specs=[a_spec, b_spec], out_specs=o_spec,
        out_shape=jax.ShapeDtypeStruct(out_shape, out_dtype),
        scratch_shapes=[pltpu.VMEM(acc_shape, F32)],
        compiler_params=_cparams(("parallel", "parallel", "arbitrary")),
    )(a, b)


def _mm_nn_cols(name, a, w, out_dtype, tm=1024, tk=512):
    m, k = a.shape
    tm = min(tm, m)
    ns = w.shape[2]
    return _matmul(
        name, a, w, (m, N_DEV * ns), out_dtype, (m // tm, N_DEV, k // tk),
        pl.BlockSpec((tm, tk), lambda i, j, kk: (i, kk)),
        pl.BlockSpec((1, tk, ns), lambda i, j, kk: (j, kk, 0)),
        pl.BlockSpec((tm, ns), lambda i, j, kk: (i, j)), nt=False)


def _mm_nn(name, a, w, out_dtype, tm=1024, tn=1024, tk=512):
    m, k = a.shape
    tm = min(tm, m)
    n = w.shape[1]
    return _matmul(
        name, a, w, (m, n), out_dtype, (m // tm, n // tn, k // tk),
        pl.BlockSpec((tm, tk), lambda i, j, kk: (i, kk)),
        pl.BlockSpec((tk, tn), lambda i, j, kk: (kk, j)),
        pl.BlockSpec((tm, tn), lambda i, j, kk: (i, j)), nt=False)


def _mm_nt_cols(name, a, w, out_dtype, tm=1024, tn=1024):
    m = a.shape[0]
    tm = min(tm, m)
    n, ns = w.shape[1], w.shape[2]
    return _matmul(
        name, a, w, (m, n), out_dtype, (m // tm, n // tn, N_DEV),
        pl.BlockSpec((tm, ns), lambda i, j, kk: (i, kk)),
        pl.BlockSpec((1, tn, ns), lambda i, j, kk: (kk, j, 0)),
        pl.BlockSpec((tm, tn), lambda i, j, kk: (i, j)), nt=True)


def _mm_nt(name, a, w, out_dtype, tm=1024, tn=1024, tk=512):
    m, k = a.shape
    tm = min(tm, m)
    n = w.shape[0]
    return _matmul(
        name, a, w, (m, n), out_dtype, (m // tm, n // tn, k // tk),
        pl.BlockSpec((tm, tk), lambda i, j, kk: (i, kk)),
        pl.BlockSpec((tn, tk), lambda i, j, kk: (j, kk)),
        pl.BlockSpec((tm, tn), lambda i, j, kk: (i, j)), nt=True)


def _mm_grad_cols(name, at, b, ns, tm=1024, tk=512):
    k1, t = at.shape
    tk = min(tk, t)
    return _matmul(
        name, at, b, (N_DEV, k1, ns), BF16, (k1 // tm, N_DEV, t // tk),
        pl.BlockSpec((tm, tk), lambda i, j, kk: (i, kk)),
        pl.BlockSpec((tk, ns), lambda i, j, kk: (kk, j)),
        pl.BlockSpec((1, tm, ns), lambda i, j, kk: (j, i, 0)), nt=False)


LN_ROWS = 256


def _ln_stats(r):
    mu = jnp.mean(r, axis=-1, keepdims=True)
    xc = r - mu
    var = jnp.mean(xc * xc, axis=-1, keepdims=True)
    rstd = lax.rsqrt(var + LN_EPS)
    return xc * rstd, rstd


def _row_spec(d):
    return pl.BlockSpec((LN_ROWS, d), lambda i: (i, 0))


def _vec_spec(d):
    return pl.BlockSpec((1, d), lambda i: (0, 0))


def _ln_fwd(name, a, m, g, b, alpha):
    t, d = a.shape
    has_m = m is not None

    def body(*refs):
        if has_m:
            a_ref, m_ref, g_ref, b_ref, r_ref, y_ref, yb_ref = refs
            r = alpha * a_ref[...] + m_ref[...]
            r_ref[...] = r
        else:
            a_ref, g_ref, b_ref, y_ref, yb_ref = refs
            r = a_ref[...]
        xhat, _ = _ln_stats(r)
        y = xhat * g_ref[...] + b_ref[...]
        y_ref[...] = y
        yb_ref[...] = y.astype(BF16)

    ins = [a] + ([m] if has_m else []) + [g, b]
    in_specs = [_row_spec(d)] * (2 if has_m else 1) + [_vec_spec(d)] * 2
    outs = ([jax.ShapeDtypeStruct((t, d), F32)] if has_m else []) + [
        jax.ShapeDtypeStruct((t, d), F32), jax.ShapeDtypeStruct((t, d), BF16)]
    res = pl.pallas_call(
        body, name=name, grid=(t // LN_ROWS,), in_specs=in_specs,
        out_specs=[_row_spec(d)] * len(outs), out_shape=outs,
        compiler_params=_cparams(("parallel",)),
    )(*ins)
    return res if has_m else (None, *res)


def _ln_bwd_math(r, dy, g):
    xhat, rstd = _ln_stats(r)
    dxhat = dy * g
    m1 = jnp.mean(dxhat, axis=-1, keepdims=True)
    m2 = jnp.mean(dxhat * xhat, axis=-1, keepdims=True)
    dr = rstd * (dxhat - m1 - xhat * m2)
    return dr, jnp.sum(dy * xhat, axis=0, keepdims=True), jnp.sum(dy, axis=0, keepdims=True)


def _ln2_loss_bwd(name, h1, ffn, g, b, tgt):
    t, d = h1.shape

    def body(h1_ref, f_ref, g_ref, b_ref, t_ref, dr_ref, drb_ref, dg_ref, db_ref, loss_ref):
        @pl.when(pl.program_id(0) == 0)
        def _():
            dg_ref[...] = jnp.zeros_like(dg_ref)
            db_ref[...] = jnp.zeros_like(db_ref)
            loss_ref[...] = jnp.zeros_like(loss_ref)

        r = ALPHA * h1_ref[...] + f_ref[...]
        xhat, _ = _ln_stats(r)
        e = xhat * g_ref[...] + b_ref[...] - t_ref[...]
        loss_ref[...] += 0.5 / d * jnp.sum(e * e)
        dr, dg, db = _ln_bwd_math(r, e * (1.0 / d), g_ref[...])
        dr_ref[...] = dr
        drb_ref[...] = dr.astype(BF16)
        dg_ref[...] += dg
        db_ref[...] += db

    return pl.pallas_call(
        body, name=name, grid=(t // LN_ROWS,),
        in_specs=[_row_spec(d), _row_spec(d), _vec_spec(d), _vec_spec(d), _row_spec(d)],
        out_specs=[_row_spec(d), _row_spec(d), _vec_spec(d), _vec_spec(d), _vec_spec(128)],
        out_shape=[jax.ShapeDtypeStruct((t, d), F32), jax.ShapeDtypeStruct((t, d), BF16),
                   jax.ShapeDtypeStruct((1, d), F32), jax.ShapeDtypeStruct((1, d), F32),
                   jax.ShapeDtypeStruct((1, 128), F32)],
        compiler_params=_cparams(("arbitrary",)),
    )(h1, ffn, g, b, tgt)


def _ln_bwd(name, r, dya, dyb, g, alpha, want_bf16):
    t, d = r.shape

    def body(r_ref, dya_ref, dyb_ref, g_ref, *outs):
        dr_ref = outs[0]
        dg_ref, db_ref = outs[-2:]

        @pl.when(pl.program_id(0) == 0)
        def _():
            dg_ref[...] = jnp.zeros_like(dg_ref)
            db_ref[...] = jnp.zeros_like(db_ref)

        dy = alpha * dya_ref[...] + dyb_ref[...]
        dr, dg, db = _ln_bwd_math(r_ref[...], dy, g_ref[...])
        dr_ref[...] = dr
        if want_bf16:
            outs[1][...] = dr.astype(BF16)
        dg_ref[...] += dg
        db_ref[...] += db

    big = [jax.ShapeDtypeStruct((t, d), F32)] + ([jax.ShapeDtypeStruct((t, d), BF16)] if want_bf16 else [])
    return pl.pallas_call(
        body, name=name, grid=(t // LN_ROWS,),
        in_specs=[_row_spec(d)] * 3 + [_vec_spec(d)],
        out_specs=[_row_spec(d)] * len(big) + [_vec_spec(d)] * 2,
        out_shape=big + [jax.ShapeDtypeStruct((1, d), F32)] * 2,
        compiler_params=_cparams(("arbitrary",)),
    )(r, dya, dyb, g)


CONV_ROWS = 64


def _for_shifted(win, tm, shifts, fn):
    n = win.shape[0]
    for r in range(8):
        group = [s for s in shifts if s % 8 == r]
        if not group:
            continue
        rolled = win if r == 0 else pltpu.roll(win, n - r, axis=0)
        for s in group:
            fn(s, rolled[8 * (s // 8): 8 * (s // 8) + tm])


def _col_spec(t, cb, off=0):
    return pl.BlockSpec((t, cb), lambda j: (0, j + off))


def _ffn_act_fwd(name, hf, w, b, cb=256):
    t = hf.shape[0]
    f = hf.shape[1] // 2
    nb = f // cb
    tm = CONV_ROWS

    def body(g_ref, v_ref, w_ref, b_ref, act_ref, pad_ref):
        pad_ref[pl.ds(0, 8), :] = jnp.zeros((8, cb), F32)
        pad_ref[pl.ds(8, t), :] = g_ref[...]
        wv = [w_ref[pl.ds(k, 1), :] for k in range(FFN_KERNEL)]
        bias = b_ref[...]

        def tile(i, carry):
            r0 = pl.multiple_of(i * tm, tm)
            win = pad_ref[pl.ds(r0, tm + 8), :]
            acc = [jnp.broadcast_to(bias, (tm, cb))]

            def tap(s, rows):
                acc[0] = acc[0] + wv[s - 6] * rows

            _for_shifted(win, tm, (6, 7, 8), tap)
            gc = acc[0]
            act_ref[pl.ds(r0, tm), :] = (gc * _sigmoid(gc) * v_ref[pl.ds(r0, tm), :]).astype(BF16)
            return carry

        lax.fori_loop(0, t // tm, tile, 0)

    return pl.pallas_call(
        body, name=name, grid=(nb,),
        in_specs=[_col_spec(t, cb), _col_spec(t, cb, nb),
                  pl.BlockSpec((FFN_KERNEL, cb), lambda j: (0, j)), pl.BlockSpec((1, cb), lambda j: (0, j))],
        out_specs=_col_spec(t, cb), out_shape=jax.ShapeDtypeStruct((t, f), BF16),
        scratch_shapes=[pltpu.VMEM((t + 8, cb), F32)],
        compiler_params=_cparams(("parallel",)),
    )(hf, hf, w, b)


def _ffn_act_bwd(name, dact, hf, w, b, cb=128):
    t = hf.shape[0]
    f = hf.shape[1] // 2
    nb = f // cb
    tm = CONV_ROWS

    def body(da_ref, g_ref, v_ref, w_ref, b_ref, dhf_ref, dw_ref, db_ref, pad_ref, dgc_ref):
        pad_ref[pl.ds(0, 8), :] = jnp.zeros((8, cb), F32)
        pad_ref[pl.ds(8, t), :] = g_ref[...]
        dgc_ref[pl.ds(t, 8), :] = jnp.zeros((8, cb), F32)
        wv = [w_ref[pl.ds(k, 1), :] for k in range(FFN_KERNEL)]
        bias = b_ref[...]

        def tile_a(i, carry):
            r0 = pl.multiple_of(i * tm, tm)
            win = pad_ref[pl.ds(r0, tm + 8), :]
            taps = {}
            _for_shifted(win, tm, (6, 7, 8), lambda s, rows: taps.__setitem__(s, rows))
            gc = bias + wv[0] * taps[6] + wv[1] * taps[7] + wv[2] * taps[8]
            sg = _sigmoid(gc)
            da = da_ref[pl.ds(r0, tm), :]
            dhf_ref[1, pl.ds(r0, tm), :] = (da * gc * sg).astype(BF16)
            dgc = da * v_ref[pl.ds(r0, tm), :] * sg * (1.0 + gc * (1.0 - sg))
            dgc_ref[pl.ds(r0, tm), :] = dgc
            sums = [jnp.sum(dgc * taps[6 + k], axis=0, keepdims=True) for k in range(3)]
            sums.append(jnp.sum(dgc, axis=0, keepdims=True))
            return tuple(c + s for c, s in zip(carry, sums))

        zero = jnp.zeros((1, cb), F32)
        dw0, dw1, dw2, dbias = lax.fori_loop(0, t // tm, tile_a, (zero, zero, zero, zero))
        row = lax.broadcasted_iota(jnp.int32, (8, cb), 0)
        dw_ref[...] = jnp.where(row == 0, dw0, jnp.where(row == 1, dw1, jnp.where(row == 2, dw2, 0.0)))
        db_ref[...] = dbias

        def tile_b(i, carry):
            r0 = pl.multiple_of(i * tm, tm)
            win = dgc_ref[pl.ds(r0, tm + 8), :]
            acc = [jnp.zeros((tm, cb), F32)]

            def tap(s, rows):
                acc[0] = acc[0] + wv[2 - s] * rows

            _for_shifted(win, tm, (0, 1, 2), tap)
            dhf_ref[0, pl.ds(r0, tm), :] = acc[0].astype(BF16)
            return carry

        lax.fori_loop(0, t // tm, tile_b, 0)

    return pl.pallas_call(
        body, name=name, grid=(nb,),
        in_specs=[_col_spec(t, cb), _col_spec(t, cb), _col_spec(t, cb, nb),
                  pl.BlockSpec((FFN_KERNEL, cb), lambda j: (0, j)), pl.BlockSpec((1, cb), lambda j: (0, j))],
        out_specs=[pl.BlockSpec((2, t, cb), lambda j: (0, 0, j)),
                   pl.BlockSpec((8, cb), lambda j: (0, j)), pl.BlockSpec((1, cb), lambda j: (0, j))],
        out_shape=[jax.ShapeDtypeStruct((2, t, f), BF16), jax.ShapeDtypeStruct((8, f), F32),
                   jax.ShapeDtypeStruct((1, f), F32)],
        scratch_shapes=[pltpu.VMEM((t + 8, cb), F32), pltpu.VMEM((t + 8, cb), F32)],
        compiler_params=_cparams(("parallel",)),
    )(dact, hf, hf, w, b)


def _silu_grad(z, sg):
    return sg * (1.0 + z * (1.0 - sg))


def _conv_fwd(name, hin, w, b, ng, nb_):
    t = hin.shape[0]
    c = GROUP
    tm = CONV_ROWS
    pad = 32
    shifts = tuple(2 + k for k in range(CONV_KERNEL))

    def body(a_ref, gt_ref, w_ref, b_ref, ng_ref, nb_ref, u1_ref, u3_ref, pad_ref):
        pad_ref[pl.ds(0, pad), :] = jnp.zeros((pad, c), F32)
        pad_ref[pl.ds(pad, t), :] = a_ref[...] * _sigmoid(gt_ref[...])
        bias, gam, bet = b_ref[...], ng_ref[...], nb_ref[...]

        def tile(i, carry):
            r0 = pl.multiple_of(i * tm, tm)
            win = pad_ref[pl.ds(r0, tm + pad), :]
            acc = [jnp.broadcast_to(bias, (tm, c))]

            def tap(s, rows):
                acc[0] = acc[0] + w_ref[pl.ds(s - 2, 1), :] * rows

            _for_shifted(win, tm, shifts, tap)
            u1 = acc[0]
            u1_ref[pl.ds(r0, tm), :] = u1
            xhat, _ = _ln_stats(u1)
            u2 = xhat * gam + bet
            u3_ref[pl.ds(r0, tm), :] = (u2 * _sigmoid(u2)).astype(BF16)
            return carry

        lax.fori_loop(0, t // tm, tile, 0)

    vec = pl.BlockSpec((1, c), lambda j: (0, j))
    return pl.pallas_call(
        body, name=name, grid=(N_GROUPS,),
        in_specs=[_col_spec(t, c), _col_spec(t, c, N_GROUPS),
                  pl.BlockSpec((CONV_KERNEL, c), lambda j: (0, j)), vec, vec, vec],
        out_specs=[_col_spec(t, c), _col_spec(t, c)],
        out_shape=[jax.ShapeDtypeStruct((t, CONV_WIDTH), F32), jax.ShapeDtypeStruct((t, CONV_WIDTH), BF16)],
        scratch_shapes=[pltpu.VMEM((t + pad, c), F32)],
        compiler_params=_cparams(("parallel",)),
    )(hin, hin, w, b, ng, nb_)


def _conv_bwd(name, dcat, u1, hin, w, ng, nb_):
    t = hin.shape[0]
    c = GROUP
    tm = CONV_ROWS
    pad = 32
    nk = CONV_KERNEL

    def body(du3_ref, u1_ref, a_ref, gt_ref, w_ref, ng_ref, nb_ref,
             da_ref, dgt_ref, dw_ref, db_ref, dng_ref, dnb_ref, u0_ref, du1_ref, dwp_ref):
        u0_ref[pl.ds(0, pad), :] = jnp.zeros((pad, c), F32)
        u0_ref[pl.ds(pad, t), :] = a_ref[...] * _sigmoid(gt_ref[...])
        du1_ref[pl.ds(t, pad), :] = jnp.zeros((pad, c), F32)
        dwp_ref[...] = jnp.zeros_like(dwp_ref)
        gam, bet = ng_ref[...], nb_ref[...]

        def tile_a(i, carry):
            r0 = pl.multiple_of(i * tm, tm)
            u1 = u1_ref[pl.ds(r0, tm), :]
            xhat, rstd = _ln_stats(u1)
            u2 = xhat * gam + bet
            sg = _sigmoid(u2)
            du2 = du3_ref[pl.ds(r0, tm), :] * _silu_grad(u2, sg)
            dxhat = du2 * gam
            m1 = jnp.mean(dxhat, axis=-1, keepdims=True)
            m2 = jnp.mean(dxhat * xhat, axis=-1, keepdims=True)
            du1 = rstd * (dxhat - m1 - xhat * m2)
            du1_ref[pl.ds(r0, tm), :] = du1
            sums = (jnp.sum(du1, axis=0, keepdims=True), jnp.sum(du2 * xhat, axis=0, keepdims=True),
                    jnp.sum(du2, axis=0, keepdims=True))
            return tuple(x + s for x, s in zip(carry, sums))

        zero = jnp.zeros((1, c), F32)
        dbias, dgam, dbet = lax.fori_loop(0, t // tm, tile_a, (zero, zero, zero))
        db_ref[...] = dbias
        dng_ref[...] = dgam
        dnb_ref[...] = dbet

        def tile_b(i, carry):
            r0 = pl.multiple_of(i * tm, tm)
            du1 = du1_ref[pl.ds(r0, tm), :]
            acc = [jnp.zeros((tm, c), F32)]

            def tap_dx(s, rows):
                acc[0] = acc[0] + w_ref[pl.ds(nk - 1 - s, 1), :] * rows

            _for_shifted(du1_ref[pl.ds(r0, tm + pad), :], tm, tuple(range(nk)), tap_dx)

            def tap_dw(s, rows):
                part = (du1 * rows).reshape(tm // 8, 8, c).sum(axis=0)
                dwp_ref[s - 2] = dwp_ref[s - 2] + part

            _for_shifted(u0_ref[pl.ds(r0, tm + pad), :], tm, tuple(2 + k for k in range(nk)), tap_dw)
            du0 = acc[0]
            a = a_ref[pl.ds(r0, tm), :]
            sg = _sigmoid(gt_ref[pl.ds(r0, tm), :])
            da_ref[pl.ds(r0, tm), :] = (du0 * sg).astype(BF16)
            dgt_ref[pl.ds(r0, tm), :] = (du0 * a * sg * (1.0 - sg)).astype(BF16)
            return carry

        lax.fori_loop(0, t // tm, tile_b, 0)
        dw_ref[...] = jnp.sum(dwp_ref[...], axis=1)

    vec = pl.BlockSpec((1, c), lambda j: (0, j))
    vshape = jax.ShapeDtypeStruct((1, CONV_WIDTH), F32)
    return pl.pallas_call(
        body, name=name, grid=(N_GROUPS,),
        in_specs=[_col_spec(t, c), _col_spec(t, c), _col_spec(t, c), _col_spec(t, c, N_GROUPS),
                  pl.BlockSpec((nk, c), lambda j: (0, j)), vec, vec],
        out_specs=[_col_spec(t, c), _col_spec(t, c), pl.BlockSpec((32, c), lambda j: (0, j)), vec, vec, vec],
        out_shape=[jax.ShapeDtypeStruct((t, CONV_WIDTH), BF16), jax.ShapeDtypeStruct((t, CONV_WIDTH), BF16),
                   jax.ShapeDtypeStruct((32, CONV_WIDTH), F32), vshape, vshape, vshape],
        scratch_shapes=[pltpu.VMEM((t + pad, c), F32), pltpu.VMEM((t + pad, c), F32),
                        pltpu.VMEM((32, 8, c), F32)],
        compiler_params=_cparams(("parallel",)),
    )(dcat, u1, hin, hin, w, ng, nb_)


LEVELS = (64, 32, 16)
NT_DIMS = (((1,), (1,)), ((), ()))
NN_DIMS = (((1,), (0,)), ((), ()))
TN_DIMS = (((0,), (0,)), ((), ()))


def _bdot(a, b, dims):
    return lax.dot_general(a.astype(BF16), b.astype(BF16), dims, preferred_element_type=F32)


def _hdot(a, b):
    return jnp.dot(a, b, precision=lax.Precision.HIGHEST, preferred_element_type=F32)


def _chunk_consts():
    rid = lax.broadcasted_iota(jnp.int32, (CHUNK, GROUP), 0)
    ti = lax.broadcasted_iota(jnp.int32, (CHUNK, CHUNK), 0)
    si = lax.broadcasted_iota(jnp.int32, (CHUNK, CHUNK), 1)
    tri = (si <= ti).astype(F32)
    second = [(rid & (b // 2)) != 0 for b in LEVELS]
    same = [None] + [(ti // b) == (si // b) for b in LEVELS[1:]]
    sub = lax.broadcasted_iota(jnp.int32, (SUB, GROUP), 0)
    return rid, tri, second, same, sub


def _level_refs(cum_ref, rid):
    row = lambda i: cum_ref[pl.ds(i, 1), :]
    l1 = jnp.broadcast_to(row(31), (CHUNK, GROUP))
    l2 = jnp.where(rid < 32, row(15), row(47))
    l3 = jnp.where(rid < 16, row(7), jnp.where(rid < 32, row(23), jnp.where(rid < 48, row(39), row(55))))
    return l1, l2, l3


def _level_factors(cum, brefs, second):
    out = []
    for bref, sec in zip(brefs, second):
        eq = jnp.where(sec, jnp.exp(jnp.minimum(cum - bref, 0.0)), 0.0)
        ek = jnp.where(sec, 0.0, jnp.exp(jnp.minimum(bref - cum, 0.0)))
        out.append((eq, ek))
    return out


def _gates(q, f, lb):
    sq = _sigmoid(q)
    sf = _sigmoid(f)
    fg = lb + (1.0 - lb) * sf
    return q * sq, sq, sf, fg


def _hgrn_specs(t, nc):
    c = GROUP
    col = lambda off: pl.BlockSpec((t, c), lambda h: (0, h + off))
    hin_specs = [col(16), col(24), col(32), col(40)]
    vec = pl.BlockSpec((1, c), lambda h: (0, h))
    lbs = pl.BlockSpec((2, c), lambda h: (0, h))
    st = pl.BlockSpec((1, nc, c, c), lambda h: (h, 0, 0, 0))
    return col, hin_specs, vec, lbs, st


def _hgrn_fwd(name, hin, lb_logits, hg):
    t = hin.shape[0]
    nc = t // CHUNK
    c = GROUP
    col, hin_specs, vec, lbs, st = _hgrn_specs(t, nc)

    def body(q_ref, f_ref, v_ref, og_ref, lb_ref, hg_ref, o_ref, ob_ref, st_ref,
             s_ref, cum_ref, kk_ref, vc_ref):
        rid, tri, second, same, sub = _chunk_consts()
        lb = _sigmoid(lb_ref[pl.ds(0, 1), :] - lb_ref[pl.ds(1, 1), :])
        gain = hg_ref[...]
        s_ref[...] = jnp.zeros_like(s_ref)

        def chunk(ci, carry):
            r0 = pl.multiple_of(ci * CHUNK, CHUNK)
            rows = pl.ds(r0, CHUNK)
            qh, _, _, fg = _gates(q_ref[rows, :], f_ref[rows, :], lb)
            v = v_ref[rows, :]
            kk = 1.0 - fg
            cum = _hdot(tri, jnp.log(fg))
            cum_ref[...] = cum
            kk_ref[...] = kk
            vc_ref[...] = v
            sprev = s_ref[...]
            st_ref[0, ci] = sprev
            blast = cum_ref[pl.ds(CHUNK - 1, 1), :]
            o = _bdot(qh * jnp.exp(cum), sprev, NT_DIMS)
            s_ref[...] = sprev * jnp.exp(blast) + _bdot(v, kk * jnp.exp(blast - cum), TN_DIMS)
            a = None
            for (eq, ek), msk in zip(_level_factors(cum, _level_refs(cum_ref, rid), second), same):
                al = _bdot(qh * eq, kk * ek, NT_DIMS)
                al = al if msk is None else jnp.where(msk, al, 0.0)
                a = al if a is None else a + al
            o = o + _bdot(a, v, NN_DIMS)
            diag = []
            for sb in range(CHUNK // SUB):
                lo = sb * SUB
                qb = qh[lo:lo + SUB]
                cb = cum[lo:lo + SUB]
                od = jnp.zeros((SUB, c), F32)
                for s in range(SUB):
                    e = jnp.where(sub >= s, jnp.exp(jnp.minimum(cb - cum_ref[pl.ds(lo + s, 1), :], 0.0)), 0.0)
                    acol = jnp.sum(qb * e * kk_ref[pl.ds(lo + s, 1), :], axis=-1, keepdims=True)
                    od = od + acol * vc_ref[pl.ds(lo + s, 1), :]
                diag.append(od)
            o = o + jnp.concatenate(diag, axis=0)
            o_ref[rows, :] = o
            y = o * lax.rsqrt(jnp.mean(o * o, axis=-1, keepdims=True) + RMS_EPS) * gain
            og = og_ref[rows, :]
            ob_ref[rows, :] = (y * og * _sigmoid(og)).astype(BF16)
            return carry

        lax.fori_loop(0, nc, chunk, 0)

    return pl.pallas_call(
        body, name=name, grid=(N_GROUPS,),
        in_specs=hin_specs + [lbs, vec],
        out_specs=[col(0), col(0), st],
        out_shape=[jax.ShapeDtypeStruct((t, HGRN_WIDTH), F32), jax.ShapeDtypeStruct((t, HGRN_WIDTH), BF16),
                   jax.ShapeDtypeStruct((N_GROUPS, nc, c, c), F32)],
        scratch_shapes=[pltpu.VMEM((c, c), F32), pltpu.VMEM((CHUNK, c), F32), pltpu.VMEM((CHUNK, c), F32),
                        pltpu.VMEM((CHUNK, c), F32)],
        compiler_params=_cparams(("parallel",)),
    )(hin, hin, hin, hin, lb_logits, hg)


def _hgrn_bwd(name, dcat, hin, o_raw, states, lb_logits, hg):
    t = hin.shape[0]
    nc = t // CHUNK
    c = GROUP
    col, hin_specs, vec, lbs, st = _hgrn_specs(t, nc)

    def body(do_ref, q_ref, f_ref, v_ref, og_ref, o_ref, st_ref, lb_ref, hg_ref,
             dq_ref, df_ref, dv_ref, dog_ref, dhg_ref, dlb_ref,
             ds_ref, cum_ref, kk_ref, vc_ref):
        rid, tri, second, same, sub = _chunk_consts()
        trit = tri.T
        lb = _sigmoid(lb_ref[pl.ds(0, 1), :] - lb_ref[pl.ds(1, 1), :])
        gain = hg_ref[...]
        ds_ref[...] = jnp.zeros_like(ds_ref)

        def chunk(i, carry):
            dhg, dlb = carry
            ci = nc - 1 - i
            r0 = pl.multiple_of(ci * CHUNK, CHUNK)
            rows = pl.ds(r0, CHUNK)
            q = q_ref[rows, :]
            qh, sq, sf, fg = _gates(q, f_ref[rows, :], lb)
            v = v_ref[rows, :]
            kk = 1.0 - fg
            cum = _hdot(tri, jnp.log(fg))
            cum_ref[...] = cum
            kk_ref[...] = kk
            vc_ref[...] = v
            o = o_ref[rows, :]
            og = og_ref[rows, :]
            sg = _sigmoid(og)
            rinv = lax.rsqrt(jnp.mean(o * o, axis=-1, keepdims=True) + RMS_EPS)
            yn = o * rinv
            dof = do_ref[rows, :]
            dog_ref[rows, :] = (dof * yn * gain * _silu_grad(og, sg)).astype(BF16)
            dz = dof * og * sg
            dhg = dhg + jnp.sum(dz * yn, axis=0, keepdims=True)
            dy = dz * gain
            do = rinv * (dy - yn * jnp.mean(dy * yn, axis=-1, keepdims=True))
            sprev = st_ref[0, ci]
            dsn = ds_ref[...]
            blast = cum_ref[pl.ds(CHUNK - 1, 1), :]
            eq0 = jnp.exp(cum)
            ek0 = jnp.exp(blast - cum)
            dqh = _bdot(do, sprev, NN_DIMS) * eq0
            dkk = _bdot(v, dsn, NN_DIMS) * ek0
            dlast = (jnp.sum(kk * dkk, axis=0, keepdims=True)
                     + jnp.exp(blast) * jnp.sum(dsn * sprev, axis=0, keepdims=True))
            dv = _bdot(kk * ek0, dsn, NT_DIMS)
            ds_ref[...] = dsn * jnp.exp(blast) + _bdot(do, qh * eq0, TN_DIMS)
            dg = qh * dqh - kk * dkk
            da = _bdot(do, v, NT_DIMS)
            a = None
            for (eq, ek), msk in zip(_level_factors(cum, _level_refs(cum_ref, rid), second), same):
                ql, kl = (qh * eq).astype(BF16), (kk * ek).astype(BF16)
                al = _bdot(ql, kl, NT_DIMS)
                dal = da
                if msk is not None:
                    al = jnp.where(msk, al, 0.0)
                    dal = jnp.where(msk, da, 0.0)
                a = al if a is None else a + al
                dql = _bdot(dal, kl, NN_DIMS)
                dkl = _bdot(dal, ql, TN_DIMS)
                dqh = dqh + dql * eq
                dkk = dkk + dkl * ek
                dg = dg + (ql.astype(F32) * dql - kl.astype(F32) * dkl)
            dv = dv + _bdot(a, do, TN_DIMS)
            dq_d, dk_d, dv_d = [], [], []
            for sb in range(CHUNK // SUB):
                lo = sb * SUB
                qb = qh[lo:lo + SUB]
                cb = cum[lo:lo + SUB]
                dob = do[lo:lo + SUB]
                dqb = jnp.zeros((SUB, c), F32)
                dkb = jnp.zeros((SUB, c), F32)
                dvb = jnp.zeros((SUB, c), F32)
                for s in range(SUB):
                    e = jnp.where(sub >= s, jnp.exp(jnp.minimum(cb - cum_ref[pl.ds(lo + s, 1), :], 0.0)), 0.0)
                    ks = kk_ref[pl.ds(lo + s, 1), :]
                    qe = qb * e
                    dacol = jnp.sum(dob * vc_ref[pl.ds(lo + s, 1), :], axis=-1, keepdims=True)
                    acol = jnp.sum(qe * ks, axis=-1, keepdims=True)
                    dqb = dqb + dacol * (ks * e)
                    dkb = jnp.where(sub == s, jnp.sum(dacol * qe, axis=0, keepdims=True), dkb)
                    dvb = jnp.where(sub == s, jnp.sum(acol * dob, axis=0, keepdims=True), dvb)
                dq_d.append(dqb)
                dk_d.append(dkb)
                dv_d.append(dvb)
            dq_d = jnp.concatenate(dq_d, axis=0)
            dk_d = jnp.concatenate(dk_d, axis=0)
            dqh = dqh + dq_d
            dkk = dkk + dk_d
            dg = dg + (qh * dq_d - kk * dk_d)
            dv = dv + jnp.concatenate(dv_d, axis=0)
            dlf = _hdot(trit, dg) + dlast
            dfg = dlf / fg - dkk
            df_ref[rows, :] = (dfg * (1.0 - lb) * sf * (1.0 - sf)).astype(BF16)
            dlb = dlb + jnp.sum(dfg * (1.0 - sf), axis=0, keepdims=True)
            dq_ref[rows, :] = (dqh * _silu_grad(q, sq)).astype(BF16)
            dv_ref[rows, :] = dv.astype(BF16)
            return dhg, dlb

        zero = jnp.zeros((1, c), F32)
        dhg, dlb = lax.fori_loop(0, nc, chunk, (zero, zero))
        dhg_ref[...] = dhg
        dl0 = dlb * lb * (1.0 - lb)
        dlb_ref[...] = jnp.where(lax.broadcasted_iota(jnp.int32, (2, c), 0) == 0, dl0, -dl0)

    big = jax.ShapeDtypeStruct((t, HGRN_WIDTH), BF16)
    return pl.pallas_call(
        body, name=name, grid=(N_GROUPS,),
        in_specs=[col(8)] + hin_specs + [col(0), st, lbs, vec],
        out_specs=[col(0)] * 4 + [vec, lbs],
        out_shape=[big] * 4 + [jax.ShapeDtypeStruct((1, HGRN_WIDTH), F32), jax.ShapeDtypeStruct((2, HGRN_WIDTH), F32)],
        scratch_shapes=[pltpu.VMEM((c, c), F32)] + [pltpu.VMEM((CHUNK, c), F32)] * 3,
        compiler_params=_cparams(("parallel",)),
    )(dcat, hin, hin, hin, hin, o_raw, states, lb_logits, hg)


ANY = pl.BlockSpec(memory_space=pl.ANY)


def _my_place():
    return lax.axis_index("x"), lax.axis_index("y"), lax.axis_index("c")


def _all_gather(name, shard):
    def body(x_ref, out_ref, send_sems, recv_sems, local_sem):
        x, y, c = _my_place()
        me, sibling = (x, y, c), (x, y, 1 - c)
        chips = [(1 - x, y), (x, 1 - y), (1 - x, 1 - y)]

        def slot(px, py, pc):
            return out_ref.at[4 * px + 2 * py + pc]

        def copy(k, block, to, src=None):
            return pltpu.make_async_remote_copy(
                src_ref=slot(*block) if src is None else src, dst_ref=slot(*block),
                send_sem=send_sems.at[k], recv_sem=recv_sems.at[k], device_id=to, device_id_type=MESH)

        mine = pltpu.make_async_copy(x_ref, slot(*me), local_sem)
        mine.start()
        first = [copy(0, me, sibling, src=x_ref)]
        first += [copy(1 + j, me, (*chip, c), src=x_ref) for j, chip in enumerate(chips)]
        for cp in first:
            cp.start()
        passed = [copy(4 + j, (*chip, c), sibling) for j, chip in enumerate(chips)]
        for j, chip in enumerate(chips):
            copy(1 + j, (*chip, c), me).wait_recv()
            passed[j].start()
        copy(0, sibling, me).wait_recv()
        for j, chip in enumerate(chips):
            copy(4 + j, (*chip, 1 - c), me).wait_recv()
        for cp in first + passed:
            cp.wait_send()
        mine.wait()

    return pl.pallas_call(
        body, name=name, out_shape=jax.ShapeDtypeStruct((N_DEV,) + shard.shape, shard.dtype),
        in_specs=[ANY], out_specs=ANY,
        scratch_shapes=[pltpu.SemaphoreType.DMA((7,)), pltpu.SemaphoreType.DMA((7,)), pltpu.SemaphoreType.DMA],
    )(shard)


HBM = pl.BlockSpec(memory_space=pltpu.HBM)
SEM = pl.BlockSpec(memory_space=pltpu.SEMAPHORE)
EFFECT = pltpu.SideEffectType.DATAFLOW_SIDE_EFFECTING


def _peer(k):
    x, y, c = _my_place()
    px = 1 - x if k & 4 else x
    py = 1 - y if k & 2 else y
    pc = 1 - c if k & 1 else c
    return (px, py, pc), 4 * px + 2 * py + pc


def _exchange_copy(k, src_ref, land_ref, send_sems, recv_sems, scatter, landing):
    x, y, c = _my_place()
    me = 4 * x + 2 * y + c
    to, idx = _peer(k)
    return pltpu.make_async_remote_copy(
        src_ref=src_ref.at[idx] if scatter else src_ref,
        dst_ref=land_ref.at[idx] if landing else land_ref.at[me],
        send_sem=send_sems.at[k - 1], recv_sem=recv_sems.at[k - 1], device_id=to, device_id_type=MESH)


def _exchange_start(name, src, land, scatter):
    def body(src_ref, land_ref, send_sems, recv_sems, src_thru, land_thru, token):
        for k in range(1, N_DEV):
            _exchange_copy(k, src_ref, land_ref, send_sems, recv_sems, scatter, landing=False).start()
        token[...] = jnp.zeros_like(token)

    send_sems, recv_sems, src_thru, land_thru, token = pl.pallas_call(
        body, name=name,
        out_shape=(pltpu.SemaphoreType.DMA((N_DEV - 1,)), pltpu.SemaphoreType.DMA((N_DEV - 1,)),
                   pltpu.HBM(src.shape, src.dtype), pltpu.HBM(land.shape, land.dtype),
                   jax.ShapeDtypeStruct((8, 128), F32)),
        in_specs=(HBM, HBM), out_specs=(SEM, SEM, HBM, HBM, pl.BlockSpec(memory_space=pltpu.VMEM)),
        input_output_aliases={0: 2, 1: 3},
        compiler_params=pltpu.CompilerParams(has_side_effects=EFFECT),
    )(pltpu.with_memory_space_constraint(src, pltpu.HBM), pltpu.with_memory_space_constraint(land, pltpu.HBM))
    return (send_sems, recv_sems, src_thru, land_thru, scatter), token


def _exchange_wait(name, handle, after):
    send_sems, recv_sems, src_thru, land_thru, scatter = handle

    def body(src_ref, land_ref, send_sems, recv_sems, after_ref, src_dead, got_ref):
        for k in range(1, N_DEV):
            cp = _exchange_copy(k, src_ref, land_ref, send_sems, recv_sems, scatter, landing=True)
            cp.wait_send()
            cp.wait_recv()

    return pl.pallas_call(
        body, name=name,
        out_shape=(pltpu.HBM(src_thru.shape, src_thru.dtype), pltpu.HBM(land_thru.shape, land_thru.dtype)),
        in_specs=(HBM, HBM, SEM, SEM, ANY), out_specs=(HBM, HBM), input_output_aliases={0: 0, 1: 1},
        compiler_params=pltpu.CompilerParams(has_side_effects=EFFECT),
    )(src_thru, land_thru, send_sems, recv_sems, after)[1]


def _own_slot(own, me):
    land = lax.empty((N_DEV,) + own.shape, own.dtype)
    return lax.dynamic_update_slice_in_dim(land, own[None], me, axis=0)


def _adamw_math(w, g, m, v):
    m = ADAM_B1 * m + (1.0 - ADAM_B1) * g
    v = ADAM_B2 * v + (1.0 - ADAM_B2) * (g * g)
    m_hat = m / (1.0 - ADAM_B1 ** ADAM_STEP)
    v_hat = v / (1.0 - ADAM_B2 ** ADAM_STEP)
    delta = -ADAM_LR * (m_hat / (jnp.sqrt(v_hat) + ADAM_EPS) + ADAM_WD * w)
    return delta, m, v


def _adamw_sum(name, recv, w, m, v, tr):
    r, c = w.shape

    def body(recv_ref, w_ref, m_ref, v_ref, g_ref, d_ref, mo_ref, vo_ref):
        g = recv_ref[0].astype(F32)
        for j in range(1, N_DEV):
            g = g + recv_ref[j].astype(F32)
        g_ref[...] = g
        d_ref[...], mo_ref[...], vo_ref[...] = _adamw_math(w_ref[...], g, m_ref[...], v_ref[...])

    tile = pl.BlockSpec((tr, c), lambda i: (i, 0))
    out = jax.ShapeDtypeStruct((r, c), F32)
    return pl.pallas_call(
        body, name=name, grid=(r // tr,),
        in_specs=[pl.BlockSpec((N_DEV, tr, c), lambda i: (0, i, 0)), tile, tile, tile],
        out_specs=[tile] * 4, out_shape=[out] * 4,
        compiler_params=_cparams(("parallel",)),
    )(recv, w, m, v)


def _sum_parts(name, parts):
    _, r, c = parts.shape

    def body(p_ref, o_ref):
        acc = p_ref[0]
        for j in range(1, N_DEV):
            acc = acc + p_ref[j]
        o_ref[...] = acc

    return pl.pallas_call(body, name=name, out_shape=jax.ShapeDtypeStruct((r, c), F32),
                          compiler_params=_cparams())(parts)


def _adamw_small(name, w, g, m, v):
    def body(w_ref, g_ref, m_ref, v_ref, d_ref, mo_ref, vo_ref):
        d_ref[...], mo_ref[...], vo_ref[...] = _adamw_math(w_ref[...], g_ref[...], m_ref[...], v_ref[...])

    out = jax.ShapeDtypeStruct(w.shape, F32)
    return pl.pallas_call(body, name=name, out_shape=[out] * 3, compiler_params=_cparams())(w, g, m, v)


def _pack(pieces, rows):
    flat = jnp.concatenate([p.reshape(-1).astype(F32) for p in pieces])
    return jnp.pad(flat, (0, rows * 128 - flat.shape[0])).reshape(rows, 128)


def _unpack(packed, shapes):
    flat = packed.reshape(-1)
    out, off = [], 0
    for s in shapes:
        n = 1
        for d in s:
            n *= d
        out.append(flat[off:off + n].reshape(s))
        off += n
    return out


def kernel(x, emb_ln_g, emb_ln_b, w_in, conv_w, conv_b, conv_norm_g, conv_norm_b, lb_logits, hgrn_norm_g, w_out, ln1_g, ln1_b, w_ffn_up, ffn_conv_w, ffn_conv_b, w_ffn_down, ln2_g, ln2_b, loss_target, m_emb_ln_g, m_emb_ln_b, m_w_in, m_conv_w, m_conv_b, m_conv_norm_g, m_conv_norm_b, m_lb_logits, m_hgrn_norm_g, m_w_out, m_ln1_g, m_ln1_b, m_w_ffn_up, m_ffn_conv_w, m_ffn_conv_b, m_w_ffn_down, m_ln2_g, m_ln2_b, v_emb_ln_g, v_emb_ln_b, v_w_in, v_conv_w, v_conv_b, v_conv_norm_g, v_conv_norm_b, v_lb_logits, v_hgrn_norm_g, v_w_out, v_ln1_g, v_ln1_b, v_w_ffn_up, v_ffn_conv_w, v_ffn_conv_b, v_w_ffn_down, v_ln2_g, v_ln2_b):
    t = x.shape[1]
    me = 4 * lax.axis_index("x") + 2 * lax.axis_index("y") + lax.axis_index("c")
    x2, tgt = x[0], loss_target[0]
    ns_in, ns_up = w_in.shape[2], w_ffn_up.shape[2]
    rs_out, rs_down = w_out.shape[1], w_ffn_down.shape[1]
    cs, fs = conv_w.shape[2], ffn_conv_w.shape[2]
    tk = min(512, t)

    def gather_start(name, w, prev):
        shard = (w[0] + prev).astype(BF16)
        return _exchange_start(name, shard, _own_slot(shard, me), scatter=False)

    h_in, tok = gather_start("ag_w_in_start", w_in, 0.0)
    h_out, tok = gather_start("ag_w_out_start", w_out, tok[0, 0])
    h_up, tok = gather_start("ag_w_up_start", w_ffn_up, tok[0, 0])
    h_down, tok = gather_start("ag_w_down_start", w_ffn_down, tok[0, 0])
    n_cw, n_fw = CONV_KERNEL * cs, FFN_KERNEL * fs
    taps_g = _all_gather("ag_taps", _pack([conv_w[0], ffn_conv_w[0]], 48)).reshape(N_DEV, -1)
    cw_full = taps_g[:, :n_cw].reshape(N_DEV, CONV_KERNEL, cs).transpose(1, 0, 2).reshape(CONV_KERNEL, CONV_WIDTH)
    fw_full = taps_g[:, n_cw:n_cw + n_fw].reshape(N_DEV, FFN_KERNEL, fs).transpose(1, 0, 2).reshape(FFN_KERNEL, D_FF)

    row = lambda a: a.reshape(1, -1)

    _, h0, h0b = _ln_fwd("ln_in", x2, None, row(emb_ln_g) + tok[0, 0], row(emb_ln_b), 1.0)
    win_g = _exchange_wait("ag_w_in_wait", h_in, h0b)
    hin = _mm_nn_cols("mm_in", h0b, win_g, F32)
    u1, u3b = _conv_fwd("conv_fwd", hin, cw_full, conv_b, conv_norm_g, conv_norm_b)
    o_raw, ob, states = _hgrn_fwd("hgrn_fwd", hin, lb_logits, hgrn_norm_g)
    catb = jnp.concatenate([u3b, ob], axis=1)
    wout_g = _exchange_wait("ag_w_out_wait", h_out, catb).reshape(D_MODEL, D_MODEL)
    mix = _mm_nn("mm_out", catb, wout_g, F32)
    r1, h1, h1b = _ln_fwd("ln1", h0, mix, ln1_g, ln1_b, ALPHA)
    wup_g = _exchange_wait("ag_w_up_wait", h_up, h1b)
    hf = _mm_nn_cols("mm_up", h1b, wup_g, F32)
    actb = _ffn_act_fwd("ffn_act", hf, fw_full, ffn_conv_b)
    wdown_g = _exchange_wait("ag_w_down_wait", h_down, actb).reshape(D_FF, D_MODEL)
    ffn = _mm_nn("mm_down", actb, wdown_g, F32)
    dr2, dr2b, g_ln2g, g_ln2b, loss = _ln2_loss_bwd("ln2_loss", h1, ffn, ln2_g, ln2_b, tgt)

    def scatter_start(name, parts):
        own = lax.dynamic_index_in_dim(parts, me, axis=0, keepdims=False)
        return _exchange_start(name, parts, _own_slot(own, me), scatter=True)

    dact = _mm_nt("mm_dact", dr2b, wdown_g, F32, tn=1408)
    gw_down = _mm_nn("mm_dw_down", actb.T, dr2b, BF16, tm=rs_down, tn=D_MODEL, tk=tk)
    s_down, tok = scatter_start("a2a_w_down_start", gw_down.reshape(N_DEV, rs_down, D_MODEL))
    dhf, g_fw, g_fb = _ffn_act_bwd("ffn_act_bwd", dact, hf, fw_full, ffn_conv_b + tok[0, 0])
    tm = min(1024, t)
    gw_up = _matmul(
        "mm_dw_up", h1b.T, dhf, (N_DEV, D_MODEL, ns_up), BF16, (D_MODEL // 1024, N_DEV, t // tk),
        pl.BlockSpec((1024, tk), lambda i, j, kk: (i, kk)),
        pl.BlockSpec((1, tk, ns_up), lambda i, j, kk: (j // 4, kk, j % 4)),
        pl.BlockSpec((1, 1024, ns_up), lambda i, j, kk: (j, i, 0)), nt=False)
    s_up, tok = scatter_start("a2a_w_up_start", gw_up)
    dh1 = _matmul(
        "mm_dh1", dhf, wup_g, (t, D_MODEL), F32, (t // tm, D_MODEL // 1024, N_DEV),
        pl.BlockSpec((1, tm, ns_up), lambda i, j, kk: (kk // 4, i, kk % 4)),
        pl.BlockSpec((1, 1024, ns_up), lambda i, j, kk: (kk, j, 0)),
        pl.BlockSpec((tm, 1024), lambda i, j, kk: (i, j)), nt=True)
    dr1, dr1b, g_ln1g, g_ln1b = _ln_bwd("ln1_bwd", r1, dr2, dh1, ln1_g + tok[0, 0], ALPHA, True)
    gw_out = _mm_nn("mm_dw_out", catb.T, dr1b, BF16, tm=1024, tn=D_MODEL, tk=tk)
    s_out, tok = scatter_start("a2a_w_out_start", gw_out.reshape(N_DEV, rs_out, D_MODEL))
    dcat = _mm_nt("mm_dcat", dr1b, wout_g, F32)
    da, dgate, g_cw, g_cb, g_cng, g_cnb = _conv_bwd("conv_bwd", dcat, u1, hin, cw_full, conv_norm_g + tok[0, 0],
                                                    conv_norm_b)
    dq, df, di, dog, g_hg, g_lb = _hgrn_bwd("hgrn_bwd", dcat, hin, o_raw, states, lb_logits, hgrn_norm_g)
    dhin = jnp.concatenate([da, dgate, dq, df, di, dog], axis=1)
    gw_in = _mm_grad_cols("mm_dw_in", h0b.T, dhin, ns_in, tk=tk)
    s_in, tok = scatter_start("a2a_w_in_start", gw_in)
    dh0 = _mm_nt_cols("mm_dh0", dhin, win_g, F32)
    grad_x, g_eg, g_eb = _ln_bwd("ln_in_bwd", x2, dr1, dh0, row(emb_ln_g) + tok[0, 0], ALPHA, False)

    small_shapes = [(D_MODEL,), (D_MODEL,), (CONV_KERNEL, CONV_WIDTH), (1, CONV_WIDTH), (1, CONV_WIDTH),
                    (1, CONV_WIDTH), (2, HGRN_WIDTH), (1, HGRN_WIDTH), (1, D_MODEL), (1, D_MODEL),
                    (FFN_KERNEL, D_FF), (1, D_FF), (1, D_MODEL), (1, D_MODEL), (128,)]
    rows_small = 569
    packed = _pack([g_eg, g_eb, g_cw[:CONV_KERNEL], g_cb, g_cng, g_cnb, g_lb, g_hg, g_ln1g, g_ln1b,
                    g_fw[:FFN_KERNEL], g_fb, g_ln2g, g_ln2b, loss], rows_small)
    summed = _sum_parts("sum_small", _all_gather("ag_small", packed))
    (s_eg, s_eb, s_cw, s_cb, s_cng, s_cnb, s_lb, s_hg, s_l1g, s_l1b, s_fw, s_fb, s_l2g, s_l2b,
     s_loss) = _unpack(summed, small_shapes)
    s_cw = lax.dynamic_slice_in_dim(s_cw, me * cs, cs, axis=1)[None]
    s_fw = lax.dynamic_slice_in_dim(s_fw, me * fs, fs, axis=1)[None]
    g_small = [s_eg, s_eb, s_cw, s_cb, s_cng, s_cnb, s_lb, s_hg, s_l1g, s_l1b, s_fw, s_fb, s_l2g, s_l2b]
    w_small = [emb_ln_g, emb_ln_b, conv_w, conv_b, conv_norm_g, conv_norm_b, lb_logits, hgrn_norm_g,
               ln1_g, ln1_b, ffn_conv_w, ffn_conv_b, ln2_g, ln2_b]
    m_small = [m_emb_ln_g, m_emb_ln_b, m_conv_w, m_conv_b, m_conv_norm_g, m_conv_norm_b, m_lb_logits,
               m_hgrn_norm_g, m_ln1_g, m_ln1_b, m_ffn_conv_w, m_ffn_conv_b, m_ln2_g, m_ln2_b]
    v_small = [v_emb_ln_g, v_emb_ln_b, v_conv_w, v_conv_b, v_conv_norm_g, v_conv_norm_b, v_lb_logits,
               v_hgrn_norm_g, v_ln1_g, v_ln1_b, v_ffn_conv_w, v_ffn_conv_b, v_ln2_g, v_ln2_b]
    rows_own = 236
    shapes_own = [w.shape for w in w_small]
    upd = _adamw_small("adamw_small", _pack(w_small, rows_own), _pack(g_small, rows_own),
                       _pack(m_small, rows_own), _pack(v_small, rows_own))
    d_small, nm_small, nv_small = (_unpack(u, shapes_own) for u in upd)
    g_small = [g.reshape(s) for g, s in zip(g_small, shapes_own)]

    def big(name, handle, after, w, m, v, tr):
        recv = _exchange_wait("a2a_" + name + "_wait", handle, after)
        return [o[None] for o in _adamw_sum("adamw_" + name, recv, w[0], m[0], v[0], tr)]

    u_down = big("w_down", s_down, upd[0], w_ffn_down, m_w_ffn_down, v_w_ffn_down, 64)
    u_up = big("w_up", s_up, u_down[1], w_ffn_up, m_w_ffn_up, v_w_ffn_up, 64)
    u_out = big("w_out", s_out, u_up[1], w_out, m_w_out, v_w_out, 64)
    u_in = big("w_in", s_in, u_out[1], w_in, m_w_in, v_w_in, 128)

    def ordered(small, i_in, i_out, i_up, i_down):
        (eg, eb, cw, cb, cng, cnb, lb, hg, l1g, l1b, fw, fb, l2g, l2b) = small
        return [eg, eb, i_in, cw, cb, cng, cnb, lb, hg, i_out, l1g, l1b, i_up, fw, fb, i_down, l2g, l2b]

    outs = [s_loss[0], grad_x[None]]
    for k, small in enumerate([g_small, d_small, nm_small, nv_small]):
        outs += ordered(small, u_in[k], u_out[k], u_up[k], u_down[k])
    return tuple(outs)
```

```python
import functools

import jax
import jax.numpy as jnp
from jax import lax
from jax.experimental import pallas as pl
from jax.experimental.pallas import tpu as pltpu

F32 = jnp.float32
BF16 = jnp.bfloat16

N_DEV = 8
D_MODEL = 2048
CONV_WIDTH = 1024
CONV_KERNEL = 31
HGRN_WIDTH = 1024
GROUP = 128
N_GROUPS = 8
IN_PROJ = 2 * CONV_WIDTH + 4 * HGRN_WIDTH
D_FF = 5632
FFN_KERNEL = 3
CHUNK = 64
SUB = 8
LN_EPS = 1e-5
RMS_EPS = 1e-6
ALPHA = 2.0 ** 0.25
ADAM_LR, ADAM_B1, ADAM_B2, ADAM_EPS, ADAM_WD, ADAM_STEP = 0.001, 0.9, 0.999, 1e-08, 0.01, 10

VMEM_LIMIT = 56 * 1024 * 1024
MESH = pl.DeviceIdType.MESH


def _cparams(sem=None):
    return pltpu.CompilerParams(dimension_semantics=sem, vmem_limit_bytes=VMEM_LIMIT)


def _sigmoid(x):
    return 1.0 / (1.0 + jnp.exp(-x))


def _matmul(name, a, b, out_shape, out_dtype, grid, a_spec, b_spec, o_spec, nt):
    nk = grid[2]
    dims = (((1,), (1,)), ((), ())) if nt else (((1,), (0,)), ((), ()))

    def body(a_ref, b_ref, o_ref, acc_ref):
        k = pl.program_id(2)

        @pl.when(k == 0)
        def _():
            acc_ref[...] = jnp.zeros_like(acc_ref)

        av = a_ref[0] if len(a_ref.shape) == 3 else a_ref[...]
        bv = b_ref[0] if len(b_ref.shape) == 3 else b_ref[...]
        acc_ref[...] += lax.dot_general(av, bv, dims, preferred_element_type=F32)

        @pl.when(k == nk - 1)
        def _():
            res = acc_ref[...].astype(out_dtype)
            if len(o_ref.shape) == 3:
                o_ref[0] = res
            else:
                o_ref[...] = res

    acc_shape = o_spec.block_shape[-2:]
    assert all(g >= 1 for g in grid), (name, grid)
    return pl.pallas_call(
        body, name=name, grid=grid, in_specs=[a_spec, b_spec], out_specs=o_spec,
        out_shape=jax.ShapeDtypeStruct(out_shape, out_dtype),
        scratch_shapes=[pltpu.VMEM(acc_shape, F32)],
        compiler_params=_cparams(("parallel", "parallel", "arbitrary")),
    )(a, b)


def _mm_nn_cols(name, a, w, out_dtype, tm=1024, tk=512):
    m, k = a.shape
    tm = min(tm, m)
    ns = w.shape[2]
    return _matmul(
        name, a, w, (m, N_DEV * ns), out_dtype, (m // tm, N_DEV, k // tk),
        pl.BlockSpec((tm, tk), lambda i, j, kk: (i, kk)),
        pl.BlockSpec((1, tk, ns), lambda i, j, kk: (j, kk, 0)),
        pl.BlockSpec((tm, ns), lambda i, j, kk: (i, j)), nt=False)


def _mm_nn(name, a, w, out_dtype, tm=1024, tn=1024, tk=512):
    m, k = a.shape
    tm = min(tm, m)
    n = w.shape[1]
    return _matmul(
        name, a, w, (m, n), out_dtype, (m // tm, n // tn, k // tk),
        pl.BlockSpec((tm, tk), lambda i, j, kk: (i, kk)),
        pl.BlockSpec((tk, tn), lambda i, j, kk: (kk, j)),
        pl.BlockSpec((tm, tn), lambda i, j, kk: (i, j)), nt=False)


def _mm_nt_cols(name, a, w, out_dtype, tm=1024, tn=1024):
    m = a.shape[0]
    tm = min(tm, m)
    n, ns = w.shape[1], w.shape[2]
    return _matmul(
        name, a, w, (m, n), out_dtype, (m // tm, n // tn, N_DEV),
        pl.BlockSpec((tm, ns), lambda i, j, kk: (i, kk)),
        pl.BlockSpec((1, tn, ns), lambda i, j, kk: (kk, j, 0)),
        pl.BlockSpec((tm, tn), lambda i, j, kk: (i, j)), nt=True)


def _mm_nt(name, a, w, out_dtype, tm=1024, tn=1024, tk=512):
    m, k = a.shape
    tm = min(tm, m)
    n = w.shape[0]
    return _matmul(
        name, a, w, (m, n), out_dtype, (m // tm, n // tn, k // tk),
        pl.BlockSpec((tm, tk), lambda i, j, kk: (i, kk)),
        pl.BlockSpec((tn, tk), lambda i, j, kk: (j, kk)),
        pl.BlockSpec((tm, tn), lambda i, j, kk: (i, j)), nt=True)


def _mm_grad_cols(name, at, b, ns, row0, rows, tm=1024, tk=512):
    t = at.shape[1]
    tk = min(tk, t)
    off = row0 // tm
    return _matmul(
        name, at, b, (N_DEV, rows, ns), BF16, (rows // tm, N_DEV, t // tk),
        pl.BlockSpec((tm, tk), lambda i, j, kk: (i + off, kk)),
        pl.BlockSpec((tk, ns), lambda i, j, kk: (kk, j)),
        pl.BlockSpec((1, tm, ns), lambda i, j, kk: (j, i, 0)), nt=False)


LN_ROWS = 256


def _ln_stats(r):
    mu = jnp.mean(r, axis=-1, keepdims=True)
    xc = r - mu
    var = jnp.mean(xc * xc, axis=-1, keepdims=True)
    rstd = lax.rsqrt(var + LN_EPS)
    return xc * rstd, rstd


def _row_spec(d):
    return pl.BlockSpec((LN_ROWS, d), lambda i: (i, 0))


def _vec_spec(d):
    return pl.BlockSpec((1, d), lambda i: (0, 0))


def _ln_fwd(name, a, m, g, b, alpha):
    t, d = a.shape
    has_m = m is not None

    def body(*refs):
        if has_m:
            a_ref, m_ref, g_ref, b_ref, r_ref, y_ref, yb_ref = refs
            r = alpha * a_ref[...] + m_ref[...]
            r_ref[...] = r
        else:
            a_ref, g_ref, b_ref, y_ref, yb_ref = refs
            r = a_ref[...]
        xhat, _ = _ln_stats(r)
        y = xhat * g_ref[...] + b_ref[...]
        y_ref[...] = y
        yb_ref[...] = y.astype(BF16)

    ins = [a] + ([m] if has_m else []) + [g, b]
    in_specs = [_row_spec(d)] * (2 if has_m else 1) + [_vec_spec(d)] * 2
    outs = ([jax.ShapeDtypeStruct((t, d), F32)] if has_m else []) + [
        jax.ShapeDtypeStruct((t, d), F32), jax.ShapeDtypeStruct((t, d), BF16)]
    res = pl.pallas_call(
        body, name=name, grid=(t // LN_ROWS,), in_specs=in_specs,
        out_specs=[_row_spec(d)] * len(outs), out_shape=outs,
        compiler_params=_cparams(("parallel",)),
    )(*ins)
    return res if has_m else (None, *res)


def _ln_bwd_math(r, dy, g):
    xhat, rstd = _ln_stats(r)
    dxhat = dy * g
    m1 = jnp.mean(dxhat, axis=-1, keepdims=True)
    m2 = jnp.mean(dxhat * xhat, axis=-1, keepdims=True)
    dr = rstd * (dxhat - m1 - xhat * m2)
    return dr, jnp.sum(dy * xhat, axis=0, keepdims=True), jnp.sum(dy, axis=0, keepdims=True)


def _ln2_loss_bwd(name, h1, ffn, g, b, tgt):
    t, d = h1.shape

    def body(h1_ref, f_ref, g_ref, b_ref, t_ref, dr_ref, drb_ref, dg_ref, db_ref, loss_ref):
        @pl.when(pl.program_id(0) == 0)
        def _():
            dg_ref[...] = jnp.zeros_like(dg_ref)
            db_ref[...] = jnp.zeros_like(db_ref)
            loss_ref[...] = jnp.zeros_like(loss_ref)

        r = ALPHA * h1_ref[...] + f_ref[...]
        xhat, _ = _ln_stats(r)
        e = xhat * g_ref[...] + b_ref[...] - t_ref[...]
        loss_ref[...] += 0.5 / d * jnp.sum(e * e)
        dr, dg, db = _ln_bwd_math(r, e * (1.0 / d), g_ref[...])
        dr_ref[...] = dr
        drb_ref[...] = dr.astype(BF16)
        dg_ref[...] += dg
        db_ref[...] += db

    return pl.pallas_call(
        body, name=name, grid=(t // LN_ROWS,),
        in_specs=[_row_spec(d), _row_spec(d), _vec_spec(d), _vec_spec(d), _row_spec(d)],
        out_specs=[_row_spec(d), _row_spec(d), _vec_spec(d), _vec_spec(d), _vec_spec(128)],
        out_shape=[jax.ShapeDtypeStruct((t, d), F32), jax.ShapeDtypeStruct((t, d), BF16),
                   jax.ShapeDtypeStruct((1, d), F32), jax.ShapeDtypeStruct((1, d), F32),
                   jax.ShapeDtypeStruct((1, 128), F32)],
        compiler_params=_cparams(("arbitrary",)),
    )(h1, ffn, g, b, tgt)


def _ln_bwd(name, r, dya, dyb, g, alpha, want_bf16):
    t, d = r.shape

    def body(r_ref, dya_ref, dyb_ref, g_ref, *outs):
        dr_ref = outs[0]
        dg_ref, db_ref = outs[-2:]

        @pl.when(pl.program_id(0) == 0)
        def _():
            dg_ref[...] = jnp.zeros_like(dg_ref)
            db_ref[...] = jnp.zeros_like(db_ref)

        dy = alpha * dya_ref[...] + dyb_ref[...]
        dr, dg, db = _ln_bwd_math(r_ref[...], dy, g_ref[...])
        dr_ref[...] = dr
        if want_bf16:
            outs[1][...] = dr.astype(BF16)
        dg_ref[...] += dg
        db_ref[...] += db

    big = [jax.ShapeDtypeStruct((t, d), F32)] + ([jax.ShapeDtypeStruct((t, d), BF16)] if want_bf16 else [])
    return pl.pallas_call(
        body, name=name, grid=(t // LN_ROWS,),
        in_specs=[_row_spec(d)] * 3 + [_vec_spec(d)],
        out_specs=[_row_spec(d)] * len(big) + [_vec_spec(d)] * 2,
        out_shape=big + [jax.ShapeDtypeStruct((1, d), F32)] * 2,
        compiler_params=_cparams(("arbitrary",)),
    )(r, dya, dyb, g)


CONV_ROWS = 64


def _for_shifted(win, tm, shifts, fn):
    n = win.shape[0]
    for r in range(8):
        group = [s for s in shifts if s % 8 == r]
        if not group:
            continue
        rolled = win if r == 0 else pltpu.roll(win, n - r, axis=0)
        for s in group:
            fn(s, rolled[8 * (s // 8): 8 * (s // 8) + tm])


def _col_spec(t, cb, off=0):
    return pl.BlockSpec((t, cb), lambda j: (0, j + off))


def _ffn_act_fwd(name, hf, w, b, cb=256):
    t = hf.shape[0]
    f = hf.shape[1] // 2
    nb = f // cb
    tm = CONV_ROWS

    def body(g_ref, v_ref, w_ref, b_ref, act_ref, pad_ref):
        pad_ref[pl.ds(0, 8), :] = jnp.zeros((8, cb), F32)
        pad_ref[pl.ds(8, t), :] = g_ref[...]
        wv = [w_ref[pl.ds(k, 1), :] for k in range(FFN_KERNEL)]
        bias = b_ref[...]

        def tile(i, carry):
            r0 = pl.multiple_of(i * tm, tm)
            win = pad_ref[pl.ds(r0, tm + 8), :]
            acc = [jnp.broadcast_to(bias, (tm, cb))]

            def tap(s, rows):
                acc[0] = acc[0] + wv[s - 6] * rows

            _for_shifted(win, tm, (6, 7, 8), tap)
            gc = acc[0]
            act_ref[pl.ds(r0, tm), :] = (gc * _sigmoid(gc) * v_ref[pl.ds(r0, tm), :]).astype(BF16)
            return carry

        lax.fori_loop(0, t // tm, tile, 0)

    return pl.pallas_call(
        body, name=name, grid=(nb,),
        in_specs=[_col_spec(t, cb), _col_spec(t, cb, nb),
                  pl.BlockSpec((FFN_KERNEL, cb), lambda j: (0, j)), pl.BlockSpec((1, cb), lambda j: (0, j))],
        out_specs=_col_spec(t, cb), out_shape=jax.ShapeDtypeStruct((t, f), BF16),
        scratch_shapes=[pltpu.VMEM((t + 8, cb), F32)],
        compiler_params=_cparams(("parallel",)),
    )(hf, hf, w, b)


def _ffn_act_bwd(name, dact, hf, w, b, cb=128):
    t = hf.shape[0]
    f = hf.shape[1] // 2
    nb = f // cb
    tm = CONV_ROWS

    def body(da_ref, g_ref, v_ref, w_ref, b_ref, dhf_ref, dw_ref, db_ref, pad_ref, dgc_ref):
        pad_ref[pl.ds(0, 8), :] = jnp.zeros((8, cb), F32)
        pad_ref[pl.ds(8, t), :] = g_ref[...]
        dgc_ref[pl.ds(t, 8), :] = jnp.zeros((8, cb), F32)
        wv = [w_ref[pl.ds(k, 1), :] for k in range(FFN_KERNEL)]
        bias = b_ref[...]

        def tile_a(i, carry):
            r0 = pl.multiple_of(i * tm, tm)
            win = pad_ref[pl.ds(r0, tm + 8), :]
            taps = {}
            _for_shifted(win, tm, (6, 7, 8), lambda s, rows: taps.__setitem__(s, rows))
            gc = bias + wv[0] * taps[6] + wv[1] * taps[7] + wv[2] * taps[8]
            sg = _sigmoid(gc)
            da = da_ref[pl.ds(r0, tm), :]
            dhf_ref[1, pl.ds(r0, tm), :] = (da * gc * sg).astype(BF16)
            dgc = da * v_ref[pl.ds(r0, tm), :] * sg * (1.0 + gc * (1.0 - sg))
            dgc_ref[pl.ds(r0, tm), :] = dgc
            sums = [jnp.sum(dgc * taps[6 + k], axis=0, keepdims=True) for k in range(3)]
            sums.append(jnp.sum(dgc, axis=0, keepdims=True))
            return tuple(c + s for c, s in zip(carry, sums))

        zero = jnp.zeros((1, cb), F32)
        dw0, dw1, dw2, dbias = lax.fori_loop(0, t // tm, tile_a, (zero, zero, zero, zero))
        row = lax.broadcasted_iota(jnp.int32, (8, cb), 0)
        dw_ref[...] = jnp.where(row == 0, dw0, jnp.where(row == 1, dw1, jnp.where(row == 2, dw2, 0.0)))
        db_ref[...] = dbias

        def tile_b(i, carry):
            r0 = pl.multiple_of(i * tm, tm)
            win = dgc_ref[pl.ds(r0, tm + 8), :]
            acc = [jnp.zeros((tm, cb), F32)]

            def tap(s, rows):
                acc[0] = acc[0] + wv[2 - s] * rows

            _for_shifted(win, tm, (0, 1, 2), tap)
            dhf_ref[0, pl.ds(r0, tm), :] = acc[0].astype(BF16)
            return carry

        lax.fori_loop(0, t // tm, tile_b, 0)

    return pl.pallas_call(
        body, name=name, grid=(nb,),
        in_specs=[_col_spec(t, cb), _col_spec(t, cb), _col_spec(t, cb, nb),
                  pl.BlockSpec((FFN_KERNEL, cb), lambda j: (0, j)), pl.BlockSpec((1, cb), lambda j: (0, j))],
        out_specs=[pl.BlockSpec((2, t, cb), lambda j: (0, 0, j)),
                   pl.BlockSpec((8, cb), lambda j: (0, j)), pl.BlockSpec((1, cb), lambda j: (0, j))],
        out_shape=[jax.ShapeDtypeStruct((2, t, f), BF16), jax.ShapeDtypeStruct((8, f), F32),
                   jax.ShapeDtypeStruct((1, f), F32)],
        scratch_shapes=[pltpu.VMEM((t + 8, cb), F32), pltpu.VMEM((t + 8, cb), F32)],
        compiler_params=_cparams(("parallel",)),
    )(dact, hf, hf, w, b)


def _silu_grad(z, sg):
    return sg * (1.0 + z * (1.0 - sg))


def _conv_fwd(name, hin, w, b, ng, nb_):
    t = hin.shape[0]
    c = GROUP
    tm = CONV_ROWS
    pad = 32
    shifts = tuple(2 + k for k in range(CONV_KERNEL))

    def body(a_ref, gt_ref, w_ref, b_ref, ng_ref, nb_ref, u1_ref, u3_ref, pad_ref):
        pad_ref[pl.ds(0, pad), :] = jnp.zeros((pad, c), F32)
        pad_ref[pl.ds(pad, t), :] = a_ref[...] * _sigmoid(gt_ref[...])
        bias, gam, bet = b_ref[...], ng_ref[...], nb_ref[...]

        def tile(i, carry):
            r0 = pl.multiple_of(i * tm, tm)
            win = pad_ref[pl.ds(r0, tm + pad), :]
            acc = [jnp.broadcast_to(bias, (tm, c))]

            def tap(s, rows):
                acc[0] = acc[0] + w_ref[pl.ds(s - 2, 1), :] * rows

            _for_shifted(win, tm, shifts, tap)
            u1 = acc[0]
            u1_ref[pl.ds(r0, tm), :] = u1
            xhat, _ = _ln_stats(u1)
            u2 = xhat * gam + bet
            u3_ref[pl.ds(r0, tm), :] = (u2 * _sigmoid(u2)).astype(BF16)
            return carry

        lax.fori_loop(0, t // tm, tile, 0)

    vec = pl.BlockSpec((1, c), lambda j: (0, j))
    return pl.pallas_call(
        body, name=name, grid=(N_GROUPS,),
        in_specs=[_col_spec(t, c), _col_spec(t, c, N_GROUPS),
                  pl.BlockSpec((CONV_KERNEL, c), lambda j: (0, j)), vec, vec, vec],
        out_specs=[_col_spec(t, c), _col_spec(t, c)],
        out_shape=[jax.ShapeDtypeStruct((t, CONV_WIDTH), F32), jax.ShapeDtypeStruct((t, CONV_WIDTH), BF16)],
        scratch_shapes=[pltpu.VMEM((t + pad, c), F32)],
        compiler_params=_cparams(("parallel",)),
    )(hin, hin, w, b, ng, nb_)


def _conv_bwd(name, dcat, u1, hin, w, ng, nb_):
    t = hin.shape[0]
    c = GROUP
    tm = CONV_ROWS
    pad = 32
    nk = CONV_KERNEL

    def body(du3_ref, u1_ref, a_ref, gt_ref, w_ref, ng_ref, nb_ref,
             da_ref, dgt_ref, dw_ref, db_ref, dng_ref, dnb_ref, u0_ref, du1_ref, dwp_ref):
        u0_ref[pl.ds(0, pad), :] = jnp.zeros((pad, c), F32)
        u0_ref[pl.ds(pad, t), :] = a_ref[...] * _sigmoid(gt_ref[...])
        du1_ref[pl.ds(t, pad), :] = jnp.zeros((pad, c), F32)
        dwp_ref[...] = jnp.zeros_like(dwp_ref)
        gam, bet = ng_ref[...], nb_ref[...]

        def tile_a(i, carry):
            r0 = pl.multiple_of(i * tm, tm)
            u1 = u1_ref[pl.ds(r0, tm), :]
            xhat, rstd = _ln_stats(u1)
            u2 = xhat * gam + bet
            sg = _sigmoid(u2)
            du2 = du3_ref[pl.ds(r0, tm), :] * _silu_grad(u2, sg)
            dxhat = du2 * gam
            m1 = jnp.mean(dxhat, axis=-1, keepdims=True)
            m2 = jnp.mean(dxhat * xhat, axis=-1, keepdims=True)
            du1 = rstd * (dxhat - m1 - xhat * m2)
            du1_ref[pl.ds(r0, tm), :] = du1
            sums = (jnp.sum(du1, axis=0, keepdims=True), jnp.sum(du2 * xhat, axis=0, keepdims=True),
                    jnp.sum(du2, axis=0, keepdims=True))
            return tuple(x + s for x, s in zip(carry, sums))

        zero = jnp.zeros((1, c), F32)
        dbias, dgam, dbet = lax.fori_loop(0, t // tm, tile_a, (zero, zero, zero))
        db_ref[...] = dbias
        dng_ref[...] = dgam
        dnb_ref[...] = dbet

        def tile_b(i, carry):
            r0 = pl.multiple_of(i * tm, tm)
            du1 = du1_ref[pl.ds(r0, tm), :]
            acc = [jnp.zeros((tm, c), F32)]

            def tap_dx(s, rows):
                acc[0] = acc[0] + w_ref[pl.ds(nk - 1 - s, 1), :] * rows

            _for_shifted(du1_ref[pl.ds(r0, tm + pad), :], tm, tuple(range(nk)), tap_dx)

            def tap_dw(s, rows):
                part = (du1 * rows).reshape(tm // 8, 8, c).sum(axis=0)
                dwp_ref[s - 2] = dwp_ref[s - 2] + part

            _for_shifted(u0_ref[pl.ds(r0, tm + pad), :], tm, tuple(2 + k for k in range(nk)), tap_dw)
            du0 = acc[0]
            a = a_ref[pl.ds(r0, tm), :]
            sg = _sigmoid(gt_ref[pl.ds(r0, tm), :])
            da_ref[pl.ds(r0, tm), :] = (du0 * sg).astype(BF16)
            dgt_ref[pl.ds(r0, tm), :] = (du0 * a * sg * (1.0 - sg)).astype(BF16)
            return carry

        lax.fori_loop(0, t // tm, tile_b, 0)
        dw_ref[...] = jnp.sum(dwp_ref[...], axis=1)

    vec = pl.BlockSpec((1, c), lambda j: (0, j))
    vshape = jax.ShapeDtypeStruct((1, CONV_WIDTH), F32)
    return pl.pallas_call(
        body, name=name, grid=(N_GROUPS,),
        in_specs=[_col_spec(t, c), _col_spec(t, c), _col_spec(t, c), _col_spec(t, c, N_GROUPS),
                  pl.BlockSpec((nk, c), lambda j: (0, j)), vec, vec],
        out_specs=[_col_spec(t, c), _col_spec(t, c), pl.BlockSpec((32, c), lambda j: (0, j)), vec, vec, vec],
        out_shape=[jax.ShapeDtypeStruct((t, CONV_WIDTH), BF16), jax.ShapeDtypeStruct((t, CONV_WIDTH), BF16),
                   jax.ShapeDtypeStruct((32, CONV_WIDTH), F32), vshape, vshape, vshape],
        scratch_shapes=[pltpu.VMEM((t + pad, c), F32), pltpu.VMEM((t + pad, c), F32),
                        pltpu.VMEM((32, 8, c), F32)],
        compiler_params=_cparams(("parallel",)),
    )(dcat, u1, hin, hin, w, ng, nb_)


LEVELS = (64, 32, 16)
NT_DIMS = (((1,), (1,)), ((), ()))
NN_DIMS = (((1,), (0,)), ((), ()))
TN_DIMS = (((0,), (0,)), ((), ()))


def _bdot(a, b, dims):
    return lax.dot_general(a.astype(BF16), b.astype(BF16), dims, preferred_element_type=F32)


def _hdot(a, b):
    return jnp.dot(a, b, precision=lax.Precision.HIGHEST, preferred_element_type=F32)


def _chunk_consts():
    rid = lax.broadcasted_iota(jnp.int32, (CHUNK, GROUP), 0)
    ti = lax.broadcasted_iota(jnp.int32, (CHUNK, CHUNK), 0)
    si = lax.broadcasted_iota(jnp.int32, (CHUNK, CHUNK), 1)
    tri = (si <= ti).astype(F32)
    second = [(rid & (b // 2)) != 0 for b in LEVELS]
    same = [None] + [(ti // b) == (si // b) for b in LEVELS[1:]]
    sub = lax.broadcasted_iota(jnp.int32, (SUB, GROUP), 0)
    return rid, tri, second, same, sub


def _level_refs(cum_ref, rid):
    row = lambda i: cum_ref[pl.ds(i, 1), :]
    l1 = jnp.broadcast_to(row(31), (CHUNK, GROUP))
    l2 = jnp.where(rid < 32, row(15), row(47))
    l3 = jnp.where(rid < 16, row(7), jnp.where(rid < 32, row(23), jnp.where(rid < 48, row(39), row(55))))
    return l1, l2, l3


def _level_factors(cum, brefs, second):
    out = []
    for bref, sec in zip(brefs, second):
        eq = jnp.where(sec, jnp.exp(jnp.minimum(cum - bref, 0.0)), 0.0)
        ek = jnp.where(sec, 0.0, jnp.exp(jnp.minimum(bref - cum, 0.0)))
        out.append((eq, ek))
    return out


def _gates(q, f, lb):
    sq = _sigmoid(q)
    sf = _sigmoid(f)
    fg = lb + (1.0 - lb) * sf
    return q * sq, sq, sf, fg


def _hgrn_specs(t, nc):
    c = GROUP
    col = lambda off: pl.BlockSpec((t, c), lambda h: (0, h + off))
    hin_specs = [col(16), col(24), col(32), col(40)]
    vec = pl.BlockSpec((1, c), lambda h: (0, h))
    lbs = pl.BlockSpec((2, c), lambda h: (0, h))
    st = pl.BlockSpec((1, nc, c, c), lambda h: (h, 0, 0, 0))
    return col, hin_specs, vec, lbs, st


def _hgrn_fwd(name, hin, lb_logits, hg):
    t = hin.shape[0]
    nc = t // CHUNK
    c = GROUP
    col, hin_specs, vec, lbs, st = _hgrn_specs(t, nc)

    def body(q_ref, f_ref, v_ref, og_ref, lb_ref, hg_ref, o_ref, ob_ref, st_ref,
             s_ref, cum_ref, kk_ref, vc_ref):
        rid, tri, second, same, sub = _chunk_consts()
        lb = _sigmoid(lb_ref[pl.ds(0, 1), :] - lb_ref[pl.ds(1, 1), :])
        gain = hg_ref[...]
        s_ref[...] = jnp.zeros_like(s_ref)

        def chunk(ci, carry):
            r0 = pl.multiple_of(ci * CHUNK, CHUNK)
            rows = pl.ds(r0, CHUNK)
            qh, _, _, fg = _gates(q_ref[rows, :], f_ref[rows, :], lb)
            v = v_ref[rows, :]
            kk = 1.0 - fg
            cum = _hdot(tri, jnp.log(fg))
            cum_ref[...] = cum
            kk_ref[...] = kk
            vc_ref[...] = v
            sprev = s_ref[...]
            st_ref[0, ci] = sprev
            blast = cum_ref[pl.ds(CHUNK - 1, 1), :]
            o = _bdot(qh * jnp.exp(cum), sprev, NT_DIMS)
            s_ref[...] = sprev * jnp.exp(blast) + _bdot(v, kk * jnp.exp(blast - cum), TN_DIMS)
            a = None
            for (eq, ek), msk in zip(_level_factors(cum, _level_refs(cum_ref, rid), second), same):
                al = _bdot(qh * eq, kk * ek, NT_DIMS)
                al = al if msk is None else jnp.where(msk, al, 0.0)
                a = al if a is None else a + al
            o = o + _bdot(a, v, NN_DIMS)
            diag = []
            for sb in range(CHUNK // SUB):
                lo = sb * SUB
                qb = qh[lo:lo + SUB]
                cb = cum[lo:lo + SUB]
                od = jnp.zeros((SUB, c), F32)
                for s in range(SUB):
                    e = jnp.where(sub >= s, jnp.exp(jnp.minimum(cb - cum_ref[pl.ds(lo + s, 1), :], 0.0)), 0.0)
                    acol = jnp.sum(qb * e * kk_ref[pl.ds(lo + s, 1), :], axis=-1, keepdims=True)
                    od = od + acol * vc_ref[pl.ds(lo + s, 1), :]
                diag.append(od)
            o = o + jnp.concatenate(diag, axis=0)
            o_ref[rows, :] = o
            y = o * lax.rsqrt(jnp.mean(o * o, axis=-1, keepdims=True) + RMS_EPS) * gain
            og = og_ref[rows, :]
            ob_ref[rows, :] = (y * og * _sigmoid(og)).astype(BF16)
            return carry

        lax.fori_loop(0, nc, chunk, 0)

    return pl.pallas_call(
        body, name=name, grid=(N_GROUPS,),
        in_specs=hin_specs + [lbs, vec],
        out_specs=[col(0), col(0), st],
        out_shape=[jax.ShapeDtypeStruct((t, HGRN_WIDTH), F32), jax.ShapeDtypeStruct((t, HGRN_WIDTH), BF16),
                   jax.ShapeDtypeStruct((N_GROUPS, nc, c, c), F32)],
        scratch_shapes=[pltpu.VMEM((c, c), F32), pltpu.VMEM((CHUNK, c), F32), pltpu.VMEM((CHUNK, c), F32),
                        pltpu.VMEM((CHUNK, c), F32)],
        compiler_params=_cparams(("parallel",)),
    )(hin, hin, hin, hin, lb_logits, hg)


def _hgrn_bwd(name, dcat, hin, o_raw, states, lb_logits, hg):
    t = hin.shape[0]
    nc = t // CHUNK
    c = GROUP
    col, hin_specs, vec, lbs, st = _hgrn_specs(t, nc)

    def body(do_ref, q_ref, f_ref, v_ref, og_ref, o_ref, st_ref, lb_ref, hg_ref,
             dq_ref, df_ref, dv_ref, dog_ref, dhg_ref, dlb_ref,
             ds_ref, cum_ref, kk_ref, vc_ref):
        rid, tri, second, same, sub = _chunk_consts()
        trit = tri.T
        lb = _sigmoid(lb_ref[pl.ds(0, 1), :] - lb_ref[pl.ds(1, 1), :])
        gain = hg_ref[...]
        ds_ref[...] = jnp.zeros_like(ds_ref)

        def chunk(i, carry):
            dhg, dlb = carry
            ci = nc - 1 - i
            r0 = pl.multiple_of(ci * CHUNK, CHUNK)
            rows = pl.ds(r0, CHUNK)
            q = q_ref[rows, :]
            qh, sq, sf, fg = _gates(q, f_ref[rows, :], lb)
            v = v_ref[rows, :]
            kk = 1.0 - fg
            cum = _hdot(tri, jnp.log(fg))
            cum_ref[...] = cum
            kk_ref[...] = kk
            vc_ref[...] = v
            o = o_ref[rows, :]
            og = og_ref[rows, :]
            sg = _sigmoid(og)
            rinv = lax.rsqrt(jnp.mean(o * o, axis=-1, keepdims=True) + RMS_EPS)
            yn = o * rinv
            dof = do_ref[rows, :]
            dog_ref[rows, :] = (dof * yn * gain * _silu_grad(og, sg)).astype(BF16)
            dz = dof * og * sg
            dhg = dhg + jnp.sum(dz * yn, axis=0, keepdims=True)
            dy = dz * gain
            do = rinv * (dy - yn * jnp.mean(dy * yn, axis=-1, keepdims=True))
            sprev = st_ref[0, ci]
            dsn = ds_ref[...]
            blast = cum_ref[pl.ds(CHUNK - 1, 1), :]
            eq0 = jnp.exp(cum)
            ek0 = jnp.exp(blast - cum)
            dqh = _bdot(do, sprev, NN_DIMS) * eq0
            dkk = _bdot(v, dsn, NN_DIMS) * ek0
            dlast = (jnp.sum(kk * dkk, axis=0, keepdims=True)
                     + jnp.exp(blast) * jnp.sum(dsn * sprev, axis=0, keepdims=True))
            dv = _bdot(kk * ek0, dsn, NT_DIMS)
            ds_ref[...] = dsn * jnp.exp(blast) + _bdot(do, qh * eq0, TN_DIMS)
            dg = qh * dqh - kk * dkk
            da = _bdot(do, v, NT_DIMS)
            a = None
            for (eq, ek), msk in zip(_level_factors(cum, _level_refs(cum_ref, rid), second), same):
                ql, kl = (qh * eq).astype(BF16), (kk * ek).astype(BF16)
                al = _bdot(ql, kl, NT_DIMS)
                dal = da
                if msk is not None:
                    al = jnp.where(msk, al, 0.0)
                    dal = jnp.where(msk, da, 0.0)
                a = al if a is None else a + al
                dql = _bdot(dal, kl, NN_DIMS)
                dkl = _bdot(dal, ql, TN_DIMS)
                dqh = dqh + dql * eq
                dkk = dkk + dkl * ek
                dg = dg + (ql.astype(F32) * dql - kl.astype(F32) * dkl)
            dv = dv + _bdot(a, do, TN_DIMS)
            dq_d, dk_d, dv_d = [], [], []
            for sb in range(CHUNK // SUB):
                lo = sb * SUB
                qb = qh[lo:lo + SUB]
                cb = cum[lo:lo + SUB]
                dob = do[lo:lo + SUB]
                dqb = jnp.zeros((SUB, c), F32)
                dkb = jnp.zeros((SUB, c), F32)
                dvb = jnp.zeros((SUB, c), F32)
                for s in range(SUB):
                    e = jnp.where(sub >= s, jnp.exp(jnp.minimum(cb - cum_ref[pl.ds(lo + s, 1), :], 0.0)), 0.0)
                    ks = kk_ref[pl.ds(lo + s, 1), :]
                    qe = qb * e
                    dacol = jnp.sum(dob * vc_ref[pl.ds(lo + s, 1), :], axis=-1, keepdims=True)
                    acol = jnp.sum(qe * ks, axis=-1, keepdims=True)
                    dqb = dqb + dacol * (ks * e)
                    dkb = jnp.where(sub == s, jnp.sum(dacol * qe, axis=0, keepdims=True), dkb)
                    dvb = jnp.where(sub == s, jnp.sum(acol * dob, axis=0, keepdims=True), dvb)
                dq_d.append(dqb)
                dk_d.append(dkb)
                dv_d.append(dvb)
            dq_d = jnp.concatenate(dq_d, axis=0)
            dk_d = jnp.concatenate(dk_d, axis=0)
            dqh = dqh + dq_d
            dkk = dkk + dk_d
            dg = dg + (qh * dq_d - kk * dk_d)
            dv = dv + jnp.concatenate(dv_d, axis=0)
            dlf = _hdot(trit, dg) + dlast
            dfg = dlf / fg - dkk
            df_ref[rows, :] = (dfg * (1.0 - lb) * sf * (1.0 - sf)).astype(BF16)
            dlb = dlb + jnp.sum(dfg * (1.0 - sf), axis=0, keepdims=True)
            dq_ref[rows, :] = (dqh * _silu_grad(q, sq)).astype(BF16)
            dv_ref[rows, :] = dv.astype(BF16)
            return dhg, dlb

        zero = jnp.zeros((1, c), F32)
        dhg, dlb = lax.fori_loop(0, nc, chunk, (zero, zero))
        dhg_ref[...] = dhg
        dl0 = dlb * lb * (1.0 - lb)
        dlb_ref[...] = jnp.where(lax.broadcasted_iota(jnp.int32, (2, c), 0) == 0, dl0, -dl0)

    big = jax.ShapeDtypeStruct((t, HGRN_WIDTH), BF16)
    return pl.pallas_call(
        body, name=name, grid=(N_GROUPS,),
        in_specs=[col(8)] + hin_specs + [col(0), st, lbs, vec],
        out_specs=[col(0)] * 4 + [vec, lbs],
        out_shape=[big] * 4 + [jax.ShapeDtypeStruct((1, HGRN_WIDTH), F32), jax.ShapeDtypeStruct((2, HGRN_WIDTH), F32)],
        scratch_shapes=[pltpu.VMEM((c, c), F32)] + [pltpu.VMEM((CHUNK, c), F32)] * 3,
        compiler_params=_cparams(("parallel",)),
    )(dcat, hin, hin, hin, hin, o_raw, states, lb_logits, hg)


ANY = pl.BlockSpec(memory_space=pl.ANY)


def _my_place():
    return lax.axis_index("x"), lax.axis_index("y"), lax.axis_index("c")


HBM = pl.BlockSpec(memory_space=pltpu.HBM)
SEM = pl.BlockSpec(memory_space=pltpu.SEMAPHORE)
EFFECT = pltpu.SideEffectType.DATAFLOW_SIDE_EFFECTING


def _peer(k):
    x, y, c = _my_place()
    px = 1 - x if k & 4 else x
    py = 1 - y if k & 2 else y
    pc = 1 - c if k & 1 else c
    return (px, py, pc), 4 * px + 2 * py + pc


def _exchange_copy(k, src_ref, land_ref, send_sems, recv_sems, scatter, landing):
    x, y, c = _my_place()
    me = 4 * x + 2 * y + c
    to, idx = _peer(k)
    return pltpu.make_async_remote_copy(
        src_ref=src_ref.at[idx] if scatter else src_ref,
        dst_ref=land_ref.at[idx] if landing else land_ref.at[me],
        send_sem=send_sems.at[k - 1], recv_sem=recv_sems.at[k - 1], device_id=to, device_id_type=MESH)


def _exchange_start(name, src, land, scatter):
    def body(src_ref, land_ref, send_sems, recv_sems, src_thru, land_thru, token):
        for k in range(1, N_DEV):
            _exchange_copy(k, src_ref, land_ref, send_sems, recv_sems, scatter, landing=False).start()
        token[...] = jnp.zeros_like(token)

    send_sems, recv_sems, src_thru, land_thru, token = pl.pallas_call(
        body, name=name,
        out_shape=(pltpu.SemaphoreType.DMA((N_DEV - 1,)), pltpu.SemaphoreType.DMA((N_DEV - 1,)),
                   pltpu.HBM(src.shape, src.dtype), pltpu.HBM(land.shape, land.dtype),
                   jax.ShapeDtypeStruct((8, 128), F32)),
        in_specs=(HBM, HBM), out_specs=(SEM, SEM, HBM, HBM, pl.BlockSpec(memory_space=pltpu.VMEM)),
        input_output_aliases={0: 2, 1: 3},
        compiler_params=pltpu.CompilerParams(has_side_effects=EFFECT),
    )(pltpu.with_memory_space_constraint(src, pltpu.HBM), pltpu.with_memory_space_constraint(land, pltpu.HBM))
    return (send_sems, recv_sems, src_thru, land_thru, scatter), token


def _exchange_wait(name, handle, after):
    send_sems, recv_sems, src_thru, land_thru, scatter = handle

    def body(src_ref, land_ref, send_sems, recv_sems, after_ref, src_dead, got_ref):
        for k in range(1, N_DEV):
            cp = _exchange_copy(k, src_ref, land_ref, send_sems, recv_sems, scatter, landing=True)
            cp.wait_send()
            cp.wait_recv()

    return pl.pallas_call(
        body, name=name,
        out_shape=(pltpu.HBM(src_thru.shape, src_thru.dtype), pltpu.HBM(land_thru.shape, land_thru.dtype)),
        in_specs=(HBM, HBM, SEM, SEM, ANY), out_specs=(HBM, HBM), input_output_aliases={0: 0, 1: 1},
        compiler_params=pltpu.CompilerParams(has_side_effects=EFFECT),
    )(src_thru, land_thru, send_sems, recv_sems, after)[1]


def _after(x, token):
    return lax.optimization_barrier((x, token))[0]


def _own_slot(own, me):
    land = lax.empty((N_DEV,) + own.shape, own.dtype)
    return lax.dynamic_update_slice_in_dim(land, own[None], me, axis=0)


def _adamw_math(w, g, m, v):
    m = ADAM_B1 * m + (1.0 - ADAM_B1) * g
    v = ADAM_B2 * v + (1.0 - ADAM_B2) * (g * g)
    m_hat = m / (1.0 - ADAM_B1 ** ADAM_STEP)
    v_hat = v / (1.0 - ADAM_B2 ** ADAM_STEP)
    delta = -ADAM_LR * (m_hat / (jnp.sqrt(v_hat) + ADAM_EPS) + ADAM_WD * w)
    return delta, m, v


def _adamw_sum(name, recv, w, m, v, tr, row0=0, partial=None):
    r, c = w.shape
    rr = recv.shape[1]
    off = row0 // tr

    def body(recv_ref, w_ref, m_ref, v_ref, *refs):
        g_ref, d_ref, mo_ref, vo_ref = refs[-4:]
        g = recv_ref[0].astype(F32)
        for j in range(1, N_DEV):
            g = g + recv_ref[j].astype(F32)
        g_ref[...] = g
        d_ref[...], mo_ref[...], vo_ref[...] = _adamw_math(w_ref[...], g, m_ref[...], v_ref[...])

    tile = pl.BlockSpec((tr, c), lambda i: (i + off, 0))
    out = jax.ShapeDtypeStruct((r, c), F32)
    prev = list(partial) if partial is not None else []
    return pl.pallas_call(
        body, name=name, grid=(rr // tr,),
        in_specs=[pl.BlockSpec((N_DEV, tr, c), lambda i: (0, i, 0)), tile, tile, tile] + [ANY] * len(prev),
        out_specs=[tile] * 4, out_shape=[out] * 4,
        input_output_aliases={4 + i: i for i in range(len(prev))},
        compiler_params=_cparams(("parallel",)),
    )(recv, w, m, v, *prev)


def _sum_parts(name, parts):
    _, r, c = parts.shape

    def body(p_ref, o_ref):
        acc = p_ref[0]
        for j in range(1, N_DEV):
            acc = acc + p_ref[j]
        o_ref[...] = acc

    return pl.pallas_call(body, name=name, out_shape=jax.ShapeDtypeStruct((r, c), F32),
                          compiler_params=_cparams())(parts)


def _adamw_small(name, w, g, m, v):
    def body(w_ref, g_ref, m_ref, v_ref, d_ref, mo_ref, vo_ref):
        d_ref[...], mo_ref[...], vo_ref[...] = _adamw_math(w_ref[...], g_ref[...], m_ref[...], v_ref[...])

    out = jax.ShapeDtypeStruct(w.shape, F32)
    return pl.pallas_call(body, name=name, out_shape=[out] * 3, compiler_params=_cparams())(w, g, m, v)


def _pack(pieces, rows):
    flat = jnp.concatenate([p.reshape(-1).astype(F32) for p in pieces])
    return jnp.pad(flat, (0, rows * 128 - flat.shape[0])).reshape(rows, 128)


def _unpack(packed, shapes):
    flat = packed.reshape(-1)
    out, off = [], 0
    for s in shapes:
        n = 1
        for d in s:
            n *= d
        out.append(flat[off:off + n].reshape(s))
        off += n
    return out


def kernel(x, emb_ln_g, emb_ln_b, w_in, conv_w, conv_b, conv_norm_g, conv_norm_b, lb_logits, hgrn_norm_g, w_out, ln1_g, ln1_b, w_ffn_up, ffn_conv_w, ffn_conv_b, w_ffn_down, ln2_g, ln2_b, loss_target, m_emb_ln_g, m_emb_ln_b, m_w_in, m_conv_w, m_conv_b, m_conv_norm_g, m_conv_norm_b, m_lb_logits, m_hgrn_norm_g, m_w_out, m_ln1_g, m_ln1_b, m_w_ffn_up, m_ffn_conv_w, m_ffn_conv_b, m_w_ffn_down, m_ln2_g, m_ln2_b, v_emb_ln_g, v_emb_ln_b, v_w_in, v_conv_w, v_conv_b, v_conv_norm_g, v_conv_norm_b, v_lb_logits, v_hgrn_norm_g, v_w_out, v_ln1_g, v_ln1_b, v_w_ffn_up, v_ffn_conv_w, v_ffn_conv_b, v_w_ffn_down, v_ln2_g, v_ln2_b):
    t = x.shape[1]
    me = 4 * lax.axis_index("x") + 2 * lax.axis_index("y") + lax.axis_index("c")
    x2, tgt = x[0], loss_target[0]
    ns_in, ns_up = w_in.shape[2], w_ffn_up.shape[2]
    rs_out, rs_down = w_out.shape[1], w_ffn_down.shape[1]
    cs, fs = conv_w.shape[2], ffn_conv_w.shape[2]
    tk = min(512, t)

    def gather_start(name, w, prev):
        shard = (w[0] + prev).astype(BF16)
        return _exchange_start(name, shard, _own_slot(shard, me), scatter=False)

    h_in, tok = gather_start("ag_w_in_start", w_in, 0.0)
    taps = _pack([conv_w[0], ffn_conv_w[0]], 48) + tok[0, 0]
    h_taps, tok = _exchange_start("ag_taps_start", taps, _own_slot(taps, me), scatter=False)
    h_out, tok = gather_start("ag_w_out_start", w_out, tok[0, 0])
    h_up, tok = gather_start("ag_w_up_start", w_ffn_up, tok[0, 0])
    h_down, tok = gather_start("ag_w_down_start", w_ffn_down, tok[0, 0])

    row = lambda a: a.reshape(1, -1)

    _, h0, h0b = _ln_fwd("ln_in", x2, None, row(emb_ln_g) + tok[0, 0], row(emb_ln_b), 1.0)
    win_g = _exchange_wait("ag_w_in_wait", h_in, h0b)
    hin = _mm_nn_cols("mm_in", h0b, win_g, F32)
    n_cw, n_fw = CONV_KERNEL * cs, FFN_KERNEL * fs
    taps_g = _exchange_wait("ag_taps_wait", h_taps, hin).reshape(N_DEV, -1)
    cw_full = taps_g[:, :n_cw].reshape(N_DEV, CONV_KERNEL, cs).transpose(1, 0, 2).reshape(CONV_KERNEL, CONV_WIDTH)
    fw_full = taps_g[:, n_cw:n_cw + n_fw].reshape(N_DEV, FFN_KERNEL, fs).transpose(1, 0, 2).reshape(FFN_KERNEL, D_FF)

    u1, u3b = _conv_fwd("conv_fwd", hin, cw_full, conv_b, conv_norm_g, conv_norm_b)
    o_raw, ob, states = _hgrn_fwd("hgrn_fwd", hin, lb_logits, hgrn_norm_g)
    catb = jnp.concatenate([u3b, ob], axis=1)
    wout_g = _exchange_wait("ag_w_out_wait", h_out, catb).reshape(D_MODEL, D_MODEL)
    mix = _mm_nn("mm_out", catb, wout_g, F32)
    r1, h1, h1b = _ln_fwd("ln1", h0, mix, ln1_g, ln1_b, ALPHA)
    wup_g = _exchange_wait("ag_w_up_wait", h_up, h1b)
    hf = _mm_nn_cols("mm_up", h1b, wup_g, F32)
    actb = _ffn_act_fwd("ffn_act", hf, fw_full, ffn_conv_b)
    wdown_g = _exchange_wait("ag_w_down_wait", h_down, actb).reshape(D_FF, D_MODEL)
    ffn = _mm_nn("mm_down", actb, wdown_g, F32)
    dr2, dr2b, g_ln2g, g_ln2b, loss = _ln2_loss_bwd("ln2_loss", h1, ffn, ln2_g, ln2_b, tgt)

    def scatter_start(name, parts):
        own = lax.dynamic_index_in_dim(parts, me, axis=0, keepdims=False)
        return _exchange_start(name, parts, _own_slot(own, me), scatter=True)

    dact = _mm_nt("mm_dact", dr2b, wdown_g, F32, tn=1408)
    gw_down = _mm_nn("mm_dw_down", actb.T, dr2b, BF16, tm=rs_down, tn=D_MODEL, tk=tk)
    s_down, tok = scatter_start("a2a_w_down_start", gw_down.reshape(N_DEV, rs_down, D_MODEL))
    dhf, g_fw, g_fb = _ffn_act_bwd("ffn_act_bwd", dact, hf, fw_full, ffn_conv_b + tok[0, 0])
    tm = min(1024, t)
    gw_up = _matmul(
        "mm_dw_up", h1b.T, dhf, (N_DEV, D_MODEL, ns_up), BF16, (D_MODEL // 1024, N_DEV, t // tk),
        pl.BlockSpec((1024, tk), lambda i, j, kk: (i, kk)),
        pl.BlockSpec((1, tk, ns_up), lambda i, j, kk: (j // 4, kk, j % 4)),
        pl.BlockSpec((1, 1024, ns_up), lambda i, j, kk: (j, i, 0)), nt=False)
    s_up, tok = scatter_start("a2a_w_up_start", gw_up)
    dh1 = _matmul(
        "mm_dh1", _after(dhf, tok), wup_g, (t, D_MODEL), F32, (t // tm, D_MODEL // 1024, N_DEV),
        pl.BlockSpec((1, tm, ns_up), lambda i, j, kk: (kk // 4, i, kk % 4)),
        pl.BlockSpec((1, 1024, ns_up), lambda i, j, kk: (kk, j, 0)),
        pl.BlockSpec((tm, 1024), lambda i, j, kk: (i, j)), nt=True)
    dr1, dr1b, g_ln1g, g_ln1b = _ln_bwd("ln1_bwd", r1, dr2, dh1, ln1_g + tok[0, 0], ALPHA, True)
    gw_out = _mm_nn("mm_dw_out", catb.T, dr1b, BF16, tm=1024, tn=D_MODEL, tk=tk)
    s_out, tok = scatter_start("a2a_w_out_start", gw_out.reshape(N_DEV, rs_out, D_MODEL))
    dcat = _mm_nt("mm_dcat", _after(dr1b, tok), wout_g, F32)
    da, dgate, g_cw, g_cb, g_cng, g_cnb = _conv_bwd("conv_bwd", dcat, u1, hin, cw_full, conv_norm_g + tok[0, 0],
                                                    conv_norm_b)
    dq, df, di, dog, g_hg, g_lb = _hgrn_bwd("hgrn_bwd", dcat, hin, o_raw, states, lb_logits, hgrn_norm_g)
    dhin = jnp.concatenate([da, dgate, dq, df, di, dog], axis=1)
    dh0 = _mm_nt_cols("mm_dh0", dhin, win_g, F32)
    grad_x, g_eg, g_eb = _ln_bwd("ln_in_bwd", x2, dr1, dh0, row(emb_ln_g), ALPHA, False)

    small_shapes = [(D_MODEL,), (D_MODEL,), (CONV_KERNEL, CONV_WIDTH), (1, CONV_WIDTH), (1, CONV_WIDTH),
                    (1, CONV_WIDTH), (2, HGRN_WIDTH), (1, HGRN_WIDTH), (1, D_MODEL), (1, D_MODEL),
                    (FFN_KERNEL, D_FF), (1, D_FF), (1, D_MODEL), (1, D_MODEL), (128,)]
    rows_small = 569
    packed = _pack([g_eg, g_eb, g_cw[:CONV_KERNEL], g_cb, g_cng, g_cnb, g_lb, g_hg, g_ln1g, g_ln1b,
                    g_fw[:FFN_KERNEL], g_fb, g_ln2g, g_ln2b, loss], rows_small)
    h_small, tok = _exchange_start("ag_small_start", packed, _own_slot(packed, me), scatter=False)
    half = D_MODEL // 2
    h0bt = h0b.T
    gw_in_a = _mm_grad_cols("mm_dw_in_a", h0bt, _after(dhin, tok), ns_in, 0, half, tk=tk)
    s_in_a, tok = scatter_start("a2a_w_in_a_start", gw_in_a)
    gw_in_b = _mm_grad_cols("mm_dw_in_b", h0bt, _after(dhin, tok), ns_in, half, half, tk=tk)
    s_in_b, tok = scatter_start("a2a_w_in_b_start", gw_in_b)
    summed = _sum_parts("sum_small", _exchange_wait("ag_small_wait", h_small, tok))
    (s_eg, s_eb, s_cw, s_cb, s_cng, s_cnb, s_lb, s_hg, s_l1g, s_l1b, s_fw, s_fb, s_l2g, s_l2b,
     s_loss) = _unpack(summed, small_shapes)
    s_cw = lax.dynamic_slice_in_dim(s_cw, me * cs, cs, axis=1)[None]
    s_fw = lax.dynamic_slice_in_dim(s_fw, me * fs, fs, axis=1)[None]
    g_small = [s_eg, s_eb, s_cw, s_cb, s_cng, s_cnb, s_lb, s_hg, s_l1g, s_l1b, s_fw, s_fb, s_l2g, s_l2b]
    w_small = [emb_ln_g, emb_ln_b, conv_w, conv_b, conv_norm_g, conv_norm_b, lb_logits, hgrn_norm_g,
               ln1_g, ln1_b, ffn_conv_w, ffn_conv_b, ln2_g, ln2_b]
    m_small = [m_emb_ln_g, m_emb_ln_b, m_conv_w, m_conv_b, m_conv_norm_g, m_conv_norm_b, m_lb_logits,
               m_hgrn_norm_g, m_ln1_g, m_ln1_b, m_ffn_conv_w, m_ffn_conv_b, m_ln2_g, m_ln2_b]
    v_small = [v_emb_ln_g, v_emb_ln_b, v_conv_w, v_conv_b, v_conv_norm_g, v_conv_norm_b, v_lb_logits,
               v_hgrn_norm_g, v_ln1_g, v_ln1_b, v_ffn_conv_w, v_ffn_conv_b, v_ln2_g, v_ln2_b]
    rows_own = 236
    shapes_own = [w.shape for w in w_small]
    upd = _adamw_small("adamw_small", _pack(w_small, rows_own), _pack(g_small, rows_own),
                       _pack(m_small, rows_own), _pack(v_small, rows_own))
    d_small, nm_small, nv_small = (_unpack(u, shapes_own) for u in upd)
    g_small = [g.reshape(s) for g, s in zip(g_small, shapes_own)]

    def big(name, handle, after, w, m, v, tr):
        recv = _exchange_wait("a2a_" + name + "_wait", handle, after)
        return [o[None] for o in _adamw_sum("adamw_" + name, recv, w[0], m[0], v[0], tr)]

    u_down = big("w_down", s_down, upd[0], w_ffn_down, m_w_ffn_down, v_w_ffn_down, 64)
    u_up = big("w_up", s_up, u_down[1], w_ffn_up, m_w_ffn_up, v_w_ffn_up, 64)
    u_out = big("w_out", s_out, u_up[1], w_out, m_w_out, v_w_out, 64)
    recv_a = _exchange_wait("a2a_w_in_a_wait", s_in_a, u_out[1])
    part = _adamw_sum("adamw_w_in_a", recv_a, w_in[0], m_w_in[0], v_w_in[0], 128)
    recv_b = _exchange_wait("a2a_w_in_b_wait", s_in_b, part[1])
    u_in = [o[None] for o in _adamw_sum("adamw_w_in_b", recv_b, w_in[0], m_w_in[0], v_w_in[0], 128,
                                        row0=half, partial=part)]

    def ordered(small, i_in, i_out, i_up, i_down):
        (eg, eb, cw, cb, cng, cnb, lb, hg, l1g, l1b, fw, fb, l2g, l2b) = small
        return [eg, eb, i_in, cw, cb, cng, cnb, lb, hg, i_out, l1g, l1b, i_up, fw, fb, i_down, l2g, l2b]

    outs = [s_loss[0], grad_x[None]]
    for k, small in enumerate([g_small, d_small, nm_small, nv_small]):
        outs += ordered(small, u_in[k], u_out[k], u_up[k], u_down[k])
    return tuple(outs)
```

```python
import functools

import jax
import jax.numpy as jnp
from jax import lax
from jax.experimental import pallas as pl
from jax.experimental.pallas import tpu as pltpu

F32 = jnp.float32
BF16 = jnp.bfloat16

N_DEV = 8
D_MODEL = 2048
CONV_WIDTH = 1024
CONV_KERNEL = 31
HGRN_WIDTH = 1024
GROUP = 128
N_GROUPS = 8
IN_PROJ = 2 * CONV_WIDTH + 4 * HGRN_WIDTH
D_FF = 5632
FFN_KERNEL = 3
CHUNK = 64
SUB = 8
LN_EPS = 1e-5
RMS_EPS = 1e-6
ALPHA = 2.0 ** 0.25
ADAM_LR, ADAM_B1, ADAM_B2, ADAM_EPS, ADAM_WD, ADAM_STEP = 0.001, 0.9, 0.999, 1e-08, 0.01, 10

VMEM_LIMIT = 56 * 1024 * 1024
MESH = pl.DeviceIdType.MESH


def _cparams(sem=None):
    return pltpu.CompilerParams(dimension_semantics=sem, vmem_limit_bytes=VMEM_LIMIT)


def _sigmoid(x):
    return 1.0 / (1.0 + jnp.exp(-x))


def _matmul(name, a, b, out_shape, out_dtype, grid, a_spec, b_spec, o_spec, nt):
    nk = grid[2]
    dims = (((1,), (1,)), ((), ())) if nt else (((1,), (0,)), ((), ()))

    def body(a_ref, b_ref, o_ref, *scratch):
        av = a_ref[0] if len(a_ref.shape) == 3 else a_ref[...]
        bv = b_ref[0] if len(b_ref.shape) == 3 else b_ref[...]
        part = lax.dot_general(av, bv, dims, preferred_element_type=F32)

        def write(res):
            if len(o_ref.shape) == 3:
                o_ref[0] = res.astype(out_dtype)
            else:
                o_ref[...] = res.astype(out_dtype)

        if nk == 1:
            write(part)
            return
        acc_ref, = scratch
        k = pl.program_id(2)

        @pl.when(k == 0)
        def _():
            acc_ref[...] = part

        @pl.when(jnp.logical_and(k > 0, k < nk - 1))
        def _():
            acc_ref[...] += part

        @pl.when(k == nk - 1)
        def _():
            write(acc_ref[...] + part)

    acc_shape = o_spec.block_shape[-2:]
    assert all(g >= 1 for g in grid), (name, grid)
    return pl.pallas_call(
        body, name=name, grid=grid, in_specs=[a_spec, b_spec], out_specs=o_spec,
        out_shape=jax.ShapeDtypeStruct(out_shape, out_dtype),
        scratch_shapes=[pltpu.VMEM(acc_shape, F32)] if nk > 1 else [],
        compiler_params=_cparams(("parallel", "parallel", "arbitrary")),
    )(a, b)


def _mm_nn(name, a, w, out_dtype, tm, tn, tk):
    m, k = a.shape
    tm, tk = min(tm, m), min(tk, k)
    n = w.shape[1]
    return _matmul(
        name, a, w, (m, n), out_dtype, (m // tm, n // tn, k // tk),
        pl.BlockSpec((tm, tk), lambda i, j, kk: (i, kk)),
        pl.BlockSpec((tk, tn), lambda i, j, kk: (kk, j)),
        pl.BlockSpec((tm, tn), lambda i, j, kk: (i, j)), nt=False)


def _mm_nt(name, a, w, out_dtype, tm, tn, tk):
    m, k = a.shape
    tm = min(tm, m)
    n = w.shape[0]
    return _matmul(
        name, a, w, (m, n), out_dtype, (m // tm, n // tn, k // tk),
        pl.BlockSpec((tm, tk), lambda i, j, kk: (i, kk)),
        pl.BlockSpec((tn, tk), lambda i, j, kk: (j, kk)),
        pl.BlockSpec((tm, tn), lambda i, j, kk: (i, j)), nt=True)


def _mm_grad_cols(name, at, b, ns, row0, rows, tm=1024, tk=4096):
    t = at.shape[1]
    tk = min(tk, t)
    off = row0 // tm
    return _matmul(
        name, at, b, (N_DEV, rows, ns), BF16, (rows // tm, N_DEV, t // tk),
        pl.BlockSpec((tm, tk), lambda i, j, kk: (i + off, kk)),
        pl.BlockSpec((tk, ns), lambda i, j, kk: (kk, j)),
        pl.BlockSpec((1, tm, ns), lambda i, j, kk: (j, i, 0)), nt=False)


LN_ROWS = 256


def _ln_stats(r):
    mu = jnp.mean(r, axis=-1, keepdims=True)
    xc = r - mu
    var = jnp.mean(xc * xc, axis=-1, keepdims=True)
    rstd = lax.rsqrt(var + LN_EPS)
    return xc * rstd, rstd


def _row_spec(d):
    return pl.BlockSpec((LN_ROWS, d), lambda i: (i, 0))


def _vec_spec(d):
    return pl.BlockSpec((1, d), lambda i: (0, 0))


def _ln_fwd(name, a, m, g, b, alpha):
    t, d = a.shape
    has_m = m is not None

    def body(*refs):
        if has_m:
            a_ref, m_ref, g_ref, b_ref, r_ref, y_ref, yb_ref = refs
            r = alpha * a_ref[...] + m_ref[...]
            r_ref[...] = r
        else:
            a_ref, g_ref, b_ref, y_ref, yb_ref = refs
            r = a_ref[...]
        xhat, _ = _ln_stats(r)
        y = xhat * g_ref[...] + b_ref[...]
        y_ref[...] = y
        yb_ref[...] = y.astype(BF16)

    ins = [a] + ([m] if has_m else []) + [g, b]
    in_specs = [_row_spec(d)] * (2 if has_m else 1) + [_vec_spec(d)] * 2
    outs = ([jax.ShapeDtypeStruct((t, d), F32)] if has_m else []) + [
        jax.ShapeDtypeStruct((t, d), F32), jax.ShapeDtypeStruct((t, d), BF16)]
    res = pl.pallas_call(
        body, name=name, grid=(t // LN_ROWS,), in_specs=in_specs,
        out_specs=[_row_spec(d)] * len(outs), out_shape=outs,
        compiler_params=_cparams(("parallel",)),
    )(*ins)
    return res if has_m else (None, *res)


def _ln_bwd_math(r, dy, g):
    xhat, rstd = _ln_stats(r)
    dxhat = dy * g
    m1 = jnp.mean(dxhat, axis=-1, keepdims=True)
    m2 = jnp.mean(dxhat * xhat, axis=-1, keepdims=True)
    dr = rstd * (dxhat - m1 - xhat * m2)
    return dr, jnp.sum(dy * xhat, axis=0, keepdims=True), jnp.sum(dy, axis=0, keepdims=True)


def _ln2_loss_bwd(name, h1, ffn, g, b, tgt):
    t, d = h1.shape

    def body(h1_ref, f_ref, g_ref, b_ref, t_ref, dr_ref, drb_ref, dg_ref, db_ref, loss_ref):
        @pl.when(pl.program_id(0) == 0)
        def _():
            dg_ref[...] = jnp.zeros_like(dg_ref)
            db_ref[...] = jnp.zeros_like(db_ref)
            loss_ref[...] = jnp.zeros_like(loss_ref)

        r = ALPHA * h1_ref[...] + f_ref[...]
        xhat, _ = _ln_stats(r)
        e = xhat * g_ref[...] + b_ref[...] - t_ref[...]
        loss_ref[...] += 0.5 / d * jnp.sum(e * e)
        dr, dg, db = _ln_bwd_math(r, e * (1.0 / d), g_ref[...])
        dr_ref[...] = dr
        drb_ref[...] = dr.astype(BF16)
        dg_ref[...] += dg
        db_ref[...] += db

    return pl.pallas_call(
        body, name=name, grid=(t // LN_ROWS,),
        in_specs=[_row_spec(d), _row_spec(d), _vec_spec(d), _vec_spec(d), _row_spec(d)],
        out_specs=[_row_spec(d), _row_spec(d), _vec_spec(d), _vec_spec(d), _vec_spec(128)],
        out_shape=[jax.ShapeDtypeStruct((t, d), F32), jax.ShapeDtypeStruct((t, d), BF16),
                   jax.ShapeDtypeStruct((1, d), F32), jax.ShapeDtypeStruct((1, d), F32),
                   jax.ShapeDtypeStruct((1, 128), F32)],
        compiler_params=_cparams(("arbitrary",)),
    )(h1, ffn, g, b, tgt)


def _ln_bwd(name, r, dya, dyb, g, alpha, want_bf16):
    t, d = r.shape

    def body(r_ref, dya_ref, dyb_ref, g_ref, *outs):
        dr_ref = outs[0]
        dg_ref, db_ref = outs[-2:]

        @pl.when(pl.program_id(0) == 0)
        def _():
            dg_ref[...] = jnp.zeros_like(dg_ref)
            db_ref[...] = jnp.zeros_like(db_ref)

        dy = alpha * dya_ref[...] + dyb_ref[...]
        dr, dg, db = _ln_bwd_math(r_ref[...], dy, g_ref[...])
        dr_ref[...] = dr
        if want_bf16:
            outs[1][...] = dr.astype(BF16)
        dg_ref[...] += dg
        db_ref[...] += db

    big = [jax.ShapeDtypeStruct((t, d), F32)] + ([jax.ShapeDtypeStruct((t, d), BF16)] if want_bf16 else [])
    return pl.pallas_call(
        body, name=name, grid=(t // LN_ROWS,),
        in_specs=[_row_spec(d)] * 3 + [_vec_spec(d)],
        out_specs=[_row_spec(d)] * len(big) + [_vec_spec(d)] * 2,
        out_shape=big + [jax.ShapeDtypeStruct((1, d), F32)] * 2,
        compiler_params=_cparams(("arbitrary",)),
    )(r, dya, dyb, g)


CONV_ROWS = 64


def _for_shifted(win, tm, shifts, fn):
    n = win.shape[0]
    for r in range(8):
        group = [s for s in shifts if s % 8 == r]
        if not group:
            continue
        rolled = win if r == 0 else pltpu.roll(win, n - r, axis=0)
        for s in group:
            fn(s, rolled[8 * (s // 8): 8 * (s // 8) + tm])


def _col_spec(t, cb, off=0):
    return pl.BlockSpec((t, cb), lambda j: (0, j + off))


def _ffn_act_fwd(name, hf, w, b, cb=256):
    t = hf.shape[0]
    f = hf.shape[1] // 2
    nb = f // cb
    tm = CONV_ROWS

    def body(g_ref, v_ref, w_ref, b_ref, act_ref, pad_ref):
        pad_ref[pl.ds(0, 8), :] = jnp.zeros((8, cb), F32)
        pad_ref[pl.ds(8, t), :] = g_ref[...]
        wv = [w_ref[pl.ds(k, 1), :] for k in range(FFN_KERNEL)]
        bias = b_ref[...]

        def tile(i, carry):
            r0 = pl.multiple_of(i * tm, tm)
            win = pad_ref[pl.ds(r0, tm + 8), :]
            acc = [jnp.broadcast_to(bias, (tm, cb))]

            def tap(s, rows):
                acc[0] = acc[0] + wv[s - 6] * rows

            _for_shifted(win, tm, (6, 7, 8), tap)
            gc = acc[0]
            act_ref[pl.ds(r0, tm), :] = (gc * _sigmoid(gc) * v_ref[pl.ds(r0, tm), :]).astype(BF16)
            return carry

        lax.fori_loop(0, t // tm, tile, 0)

    return pl.pallas_call(
        body, name=name, grid=(nb,),
        in_specs=[_col_spec(t, cb), _col_spec(t, cb, nb),
                  pl.BlockSpec((FFN_KERNEL, cb), lambda j: (0, j)), pl.BlockSpec((1, cb), lambda j: (0, j))],
        out_specs=_col_spec(t, cb), out_shape=jax.ShapeDtypeStruct((t, f), BF16),
        scratch_shapes=[pltpu.VMEM((t + 8, cb), F32)],
        compiler_params=_cparams(("parallel",)),
    )(hf, hf, w, b)


def _ffn_act_bwd(name, dact, hf, w, b, cb=128):
    t = hf.shape[0]
    f = hf.shape[1] // 2
    nb = f // cb
    tm = CONV_ROWS

    def body(da_ref, g_ref, v_ref, w_ref, b_ref, dhf_ref, dw_ref, db_ref, pad_ref, dgc_ref):
        pad_ref[pl.ds(0, 8), :] = jnp.zeros((8, cb), F32)
        pad_ref[pl.ds(8, t), :] = g_ref[...]
        dgc_ref[pl.ds(t, 8), :] = jnp.zeros((8, cb), F32)
        wv = [w_ref[pl.ds(k, 1), :] for k in range(FFN_KERNEL)]
        bias = b_ref[...]

        def tile_a(i, carry):
            r0 = pl.multiple_of(i * tm, tm)
            win = pad_ref[pl.ds(r0, tm + 8), :]
            taps = {}
            _for_shifted(win, tm, (6, 7, 8), lambda s, rows: taps.__setitem__(s, rows))
            gc = bias + wv[0] * taps[6] + wv[1] * taps[7] + wv[2] * taps[8]
            sg = _sigmoid(gc)
            da = da_ref[pl.ds(r0, tm), :]
            dhf_ref[1, pl.ds(r0, tm), :] = (da * gc * sg).astype(BF16)
            dgc = da * v_ref[pl.ds(r0, tm), :] * sg * (1.0 + gc * (1.0 - sg))
            dgc_ref[pl.ds(r0, tm), :] = dgc
            sums = [jnp.sum(dgc * taps[6 + k], axis=0, keepdims=True) for k in range(3)]
            sums.append(jnp.sum(dgc, axis=0, keepdims=True))
            return tuple(c + s for c, s in zip(carry, sums))

        zero = jnp.zeros((1, cb), F32)
        dw0, dw1, dw2, dbias = lax.fori_loop(0, t // tm, tile_a, (zero, zero, zero, zero))
        row = lax.broadcasted_iota(jnp.int32, (8, cb), 0)
        dw_ref[...] = jnp.where(row == 0, dw0, jnp.where(row == 1, dw1, jnp.where(row == 2, dw2, 0.0)))
        db_ref[...] = dbias

        def tile_b(i, carry):
            r0 = pl.multiple_of(i * tm, tm)
            win = dgc_ref[pl.ds(r0, tm + 8), :]
            acc = [jnp.zeros((tm, cb), F32)]

            def tap(s, rows):
                acc[0] = acc[0] + wv[2 - s] * rows

            _for_shifted(win, tm, (0, 1, 2), tap)
            dhf_ref[0, pl.ds(r0, tm), :] = acc[0].astype(BF16)
            return carry

        lax.fori_loop(0, t // tm, tile_b, 0)

    return pl.pallas_call(
        body, name=name, grid=(nb,),
        in_specs=[_col_spec(t, cb), _col_spec(t, cb), _col_spec(t, cb, nb),
                  pl.BlockSpec((FFN_KERNEL, cb), lambda j: (0, j)), pl.BlockSpec((1, cb), lambda j: (0, j))],
        out_specs=[pl.BlockSpec((2, t, cb), lambda j: (0, 0, j)),
                   pl.BlockSpec((8, cb), lambda j: (0, j)), pl.BlockSpec((1, cb), lambda j: (0, j))],
        out_shape=[jax.ShapeDtypeStruct((2, t, f), BF16), jax.ShapeDtypeStruct((8, f), F32),
                   jax.ShapeDtypeStruct((1, f), F32)],
        scratch_shapes=[pltpu.VMEM((t + 8, cb), F32), pltpu.VMEM((t + 8, cb), F32)],
        compiler_params=_cparams(("parallel",)),
    )(dact, hf, hf, w, b)


def _silu_grad(z, sg):
    return sg * (1.0 + z * (1.0 - sg))


def _conv_fwd(name, hin, w, b, ng, nb_):
    t = hin.shape[0]
    c = GROUP
    tm = CONV_ROWS
    pad = 32
    shifts = tuple(2 + k for k in range(CONV_KERNEL))

    def body(a_ref, gt_ref, w_ref, b_ref, ng_ref, nb_ref, u1_ref, u3_ref, pad_ref):
        pad_ref[pl.ds(0, pad), :] = jnp.zeros((pad, c), F32)
        pad_ref[pl.ds(pad, t), :] = a_ref[...] * _sigmoid(gt_ref[...])
        bias, gam, bet = b_ref[...], ng_ref[...], nb_ref[...]

        def tile(i, carry):
            r0 = pl.multiple_of(i * tm, tm)
            win = pad_ref[pl.ds(r0, tm + pad), :]
            acc = [jnp.broadcast_to(bias, (tm, c))]

            def tap(s, rows):
                acc[0] = acc[0] + w_ref[pl.ds(s - 2, 1), :] * rows

            _for_shifted(win, tm, shifts, tap)
            u1 = acc[0]
            u1_ref[pl.ds(r0, tm), :] = u1
            xhat, _ = _ln_stats(u1)
            u2 = xhat * gam + bet
            u3_ref[pl.ds(r0, tm), :] = (u2 * _sigmoid(u2)).astype(BF16)
            return carry

        lax.fori_loop(0, t // tm, tile, 0)

    vec = pl.BlockSpec((1, c), lambda j: (0, j))
    return pl.pallas_call(
        body, name=name, grid=(N_GROUPS,),
        in_specs=[_col_spec(t, c), _col_spec(t, c, N_GROUPS),
                  pl.BlockSpec((CONV_KERNEL, c), lambda j: (0, j)), vec, vec, vec],
        out_specs=[_col_spec(t, c), _col_spec(t, c)],
        out_shape=[jax.ShapeDtypeStruct((t, CONV_WIDTH), F32), jax.ShapeDtypeStruct((t, CONV_WIDTH), BF16)],
        scratch_shapes=[pltpu.VMEM((t + pad, c), F32)],
        compiler_params=_cparams(("parallel",)),
    )(hin, hin, w, b, ng, nb_)


def _conv_bwd(name, dcat, u1, hin, w, ng, nb_):
    t = hin.shape[0]
    c = GROUP
    tm = CONV_ROWS
    pad = 32
    nk = CONV_KERNEL

    def body(du3_ref, u1_ref, a_ref, gt_ref, w_ref, ng_ref, nb_ref,
             da_ref, dgt_ref, dw_ref, db_ref, dng_ref, dnb_ref, u0_ref, du1_ref, dwp_ref):
        u0_ref[pl.ds(0, pad), :] = jnp.zeros((pad, c), F32)
        u0_ref[pl.ds(pad, t), :] = a_ref[...] * _sigmoid(gt_ref[...])
        du1_ref[pl.ds(t, pad), :] = jnp.zeros((pad, c), F32)
        dwp_ref[...] = jnp.zeros_like(dwp_ref)
        gam, bet = ng_ref[...], nb_ref[...]

        def tile_a(i, carry):
            r0 = pl.multiple_of(i * tm, tm)
            u1 = u1_ref[pl.ds(r0, tm), :]
            xhat, rstd = _ln_stats(u1)
            u2 = xhat * gam + bet
            sg = _sigmoid(u2)
            du2 = du3_ref[pl.ds(r0, tm), :] * _silu_grad(u2, sg)
            dxhat = du2 * gam
            m1 = jnp.mean(dxhat, axis=-1, keepdims=True)
            m2 = jnp.mean(dxhat * xhat, axis=-1, keepdims=True)
            du1 = rstd * (dxhat - m1 - xhat * m2)
            du1_ref[pl.ds(r0, tm), :] = du1
            sums = (jnp.sum(du1, axis=0, keepdims=True), jnp.sum(du2 * xhat, axis=0, keepdims=True),
                    jnp.sum(du2, axis=0, keepdims=True))
            return tuple(x + s for x, s in zip(carry, sums))

        zero = jnp.zeros((1, c), F32)
        dbias, dgam, dbet = lax.fori_loop(0, t // tm, tile_a, (zero, zero, zero))
        db_ref[...] = dbias
        dng_ref[...] = dgam
        dnb_ref[...] = dbet

        def tile_b(i, carry):
            r0 = pl.multiple_of(i * tm, tm)
            du1 = du1_ref[pl.ds(r0, tm), :]
            acc = [jnp.zeros((tm, c), F32)]

            def tap_dx(s, rows):
                acc[0] = acc[0] + w_ref[pl.ds(nk - 1 - s, 1), :] * rows

            _for_shifted(du1_ref[pl.ds(r0, tm + pad), :], tm, tuple(range(nk)), tap_dx)

            def tap_dw(s, rows):
                part = (du1 * rows).reshape(tm // 8, 8, c).sum(axis=0)
                dwp_ref[s - 2] = dwp_ref[s - 2] + part

            _for_shifted(u0_ref[pl.ds(r0, tm + pad), :], tm, tuple(2 + k for k in range(nk)), tap_dw)
            du0 = acc[0]
            a = a_ref[pl.ds(r0, tm), :]
            sg = _sigmoid(gt_ref[pl.ds(r0, tm), :])
            da_ref[pl.ds(r0, tm), :] = (du0 * sg).astype(BF16)
            dgt_ref[pl.ds(r0, tm), :] = (du0 * a * sg * (1.0 - sg)).astype(BF16)
            return carry

        lax.fori_loop(0, t // tm, tile_b, 0)
        dw_ref[...] = jnp.sum(dwp_ref[...], axis=1)

    vec = pl.BlockSpec((1, c), lambda j: (0, j))
    vshape = jax.ShapeDtypeStruct((1, CONV_WIDTH), F32)
    return pl.pallas_call(
        body, name=name, grid=(N_GROUPS,),
        in_specs=[_col_spec(t, c), _col_spec(t, c), _col_spec(t, c), _col_spec(t, c, N_GROUPS),
                  pl.BlockSpec((nk, c), lambda j: (0, j)), vec, vec],
        out_specs=[_col_spec(t, c), _col_spec(t, c), pl.BlockSpec((32, c), lambda j: (0, j)), vec, vec, vec],
        out_shape=[jax.ShapeDtypeStruct((t, CONV_WIDTH), BF16), jax.ShapeDtypeStruct((t, CONV_WIDTH), BF16),
                   jax.ShapeDtypeStruct((32, CONV_WIDTH), F32), vshape, vshape, vshape],
        scratch_shapes=[pltpu.VMEM((t + pad, c), F32), pltpu.VMEM((t + pad, c), F32),
                        pltpu.VMEM((32, 8, c), F32)],
        compiler_params=_cparams(("parallel",)),
    )(dcat, u1, hin, hin, w, ng, nb_)


LEVELS = (64, 32, 16)
NT_DIMS = (((1,), (1,)), ((), ()))
NN_DIMS = (((1,), (0,)), ((), ()))
TN_DIMS = (((0,), (0,)), ((), ()))


def _bdot(a, b, dims):
    return lax.dot_general(a.astype(BF16), b.astype(BF16), dims, preferred_element_type=F32)


def _hdot(a, b):
    return jnp.dot(a, b, precision=lax.Precision.HIGHEST, preferred_element_type=F32)


def _chunk_consts():
    rid = lax.broadcasted_iota(jnp.int32, (CHUNK, GROUP), 0)
    ti = lax.broadcasted_iota(jnp.int32, (CHUNK, CHUNK), 0)
    si = lax.broadcasted_iota(jnp.int32, (CHUNK, CHUNK), 1)
    tri = (si <= ti).astype(F32)
    second = [(rid & (b // 2)) != 0 for b in LEVELS]
    same = [None] + [(ti // b) == (si // b) for b in LEVELS[1:]]
    sub = lax.broadcasted_iota(jnp.int32, (SUB, GROUP), 0)
    return rid, tri, second, same, sub


def _level_refs(cum_ref, rid):
    row = lambda i: cum_ref[pl.ds(i, 1), :]
    l1 = jnp.broadcast_to(row(31), (CHUNK, GROUP))
    l2 = jnp.where(rid < 32, row(15), row(47))
    l3 = jnp.where(rid < 16, row(7), jnp.where(rid < 32, row(23), jnp.where(rid < 48, row(39), row(55))))
    return l1, l2, l3


def _level_factors(cum, brefs, second):
    out = []
    for bref, sec in zip(brefs, second):
        eq = jnp.where(sec, jnp.exp(jnp.minimum(cum - bref, 0.0)), 0.0)
        ek = jnp.where(sec, 0.0, jnp.exp(jnp.minimum(bref - cum, 0.0)))
        out.append((eq, ek))
    return out


def _gates(q, f, lb):
    sq = _sigmoid(q)
    sf = _sigmoid(f)
    fg = lb + (1.0 - lb) * sf
    return q * sq, sq, sf, fg


def _hgrn_specs(t, nc):
    c = GROUP
    col = lambda off: pl.BlockSpec((t, c), lambda h: (0, h + off))
    hin_specs = [col(16), col(24), col(32), col(40)]
    vec = pl.BlockSpec((1, c), lambda h: (0, h))
    lbs = pl.BlockSpec((2, c), lambda h: (0, h))
    st = pl.BlockSpec((1, nc, c, c), lambda h: (h, 0, 0, 0))
    return col, hin_specs, vec, lbs, st


def _hgrn_fwd(name, hin, lb_logits, hg):
    t = hin.shape[0]
    nc = t // CHUNK
    c = GROUP
    col, hin_specs, vec, lbs, st = _hgrn_specs(t, nc)

    def body(q_ref, f_ref, v_ref, og_ref, lb_ref, hg_ref, o_ref, ob_ref, st_ref,
             s_ref, cum_ref, kk_ref, vc_ref):
        rid, tri, second, same, sub = _chunk_consts()
        lb = _sigmoid(lb_ref[pl.ds(0, 1), :] - lb_ref[pl.ds(1, 1), :])
        gain = hg_ref[...]
        s_ref[...] = jnp.zeros_like(s_ref)

        def chunk(ci, carry):
            r0 = pl.multiple_of(ci * CHUNK, CHUNK)
            rows = pl.ds(r0, CHUNK)
            qh, _, _, fg = _gates(q_ref[rows, :], f_ref[rows, :], lb)
            v = v_ref[rows, :]
            kk = 1.0 - fg
            cum = _hdot(tri, jnp.log(fg))
            cum_ref[...] = cum
            kk_ref[...] = kk
            vc_ref[...] = v
            sprev = s_ref[...]
            st_ref[0, ci] = sprev
            blast = cum_ref[pl.ds(CHUNK - 1, 1), :]
            o = _bdot(qh * jnp.exp(cum), sprev, NT_DIMS)
            s_ref[...] = sprev * jnp.exp(blast) + _bdot(v, kk * jnp.exp(blast - cum), TN_DIMS)
            a = None
            for (eq, ek), msk in zip(_level_factors(cum, _level_refs(cum_ref, rid), second), same):
                al = _bdot(qh * eq, kk * ek, NT_DIMS)
                al = al if msk is None else jnp.where(msk, al, 0.0)
                a = al if a is None else a + al
            o = o + _bdot(a, v, NN_DIMS)
            diag = []
            for sb in range(CHUNK // SUB):
                lo = sb * SUB
                qb = qh[lo:lo + SUB]
                cb = cum[lo:lo + SUB]
                od = jnp.zeros((SUB, c), F32)
                for s in range(SUB):
                    e = jnp.where(sub >= s, jnp.exp(jnp.minimum(cb - cum_ref[pl.ds(lo + s, 1), :], 0.0)), 0.0)
                    acol = jnp.sum(qb * e * kk_ref[pl.ds(lo + s, 1), :], axis=-1, keepdims=True)
                    od = od + acol * vc_ref[pl.ds(lo + s, 1), :]
                diag.append(od)
            o = o + jnp.concatenate(diag, axis=0)
            o_ref[rows, :] = o
            y = o * lax.rsqrt(jnp.mean(o * o, axis=-1, keepdims=True) + RMS_EPS) * gain
            og = og_ref[rows, :]
            ob_ref[rows, :] = (y * og * _sigmoid(og)).astype(BF16)
            return carry

        lax.fori_loop(0, nc, chunk, 0)

    return pl.pallas_call(
        body, name=name, grid=(N_GROUPS,),
        in_specs=hin_specs + [lbs, vec],
        out_specs=[col(0), col(0), st],
        out_shape=[jax.ShapeDtypeStruct((t, HGRN_WIDTH), F32), jax.ShapeDtypeStruct((t, HGRN_WIDTH), BF16),
                   jax.ShapeDtypeStruct((N_GROUPS, nc, c, c), F32)],
        scratch_shapes=[pltpu.VMEM((c, c), F32), pltpu.VMEM((CHUNK, c), F32), pltpu.VMEM((CHUNK, c), F32),
                        pltpu.VMEM((CHUNK, c), F32)],
        compiler_params=_cparams(("parallel",)),
    )(hin, hin, hin, hin, lb_logits, hg)


def _hgrn_bwd(name, dcat, hin, o_raw, states, lb_logits, hg):
    t = hin.shape[0]
    nc = t // CHUNK
    c = GROUP
    col, hin_specs, vec, lbs, st = _hgrn_specs(t, nc)

    def body(do_ref, q_ref, f_ref, v_ref, og_ref, o_ref, st_ref, lb_ref, hg_ref,
             dq_ref, df_ref, dv_ref, dog_ref, dhg_ref, dlb_ref,
             ds_ref, cum_ref, kk_ref, vc_ref):
        rid, tri, second, same, sub = _chunk_consts()
        trit = tri.T
        lb = _sigmoid(lb_ref[pl.ds(0, 1), :] - lb_ref[pl.ds(1, 1), :])
        gain = hg_ref[...]
        ds_ref[...] = jnp.zeros_like(ds_ref)

        def chunk(i, carry):
            dhg, dlb = carry
            ci = nc - 1 - i
            r0 = pl.multiple_of(ci * CHUNK, CHUNK)
            rows = pl.ds(r0, CHUNK)
            q = q_ref[rows, :]
            qh, sq, sf, fg = _gates(q, f_ref[rows, :], lb)
            v = v_ref[rows, :]
            kk = 1.0 - fg
            cum = _hdot(tri, jnp.log(fg))
            cum_ref[...] = cum
            kk_ref[...] = kk
            vc_ref[...] = v
            o = o_ref[rows, :]
            og = og_ref[rows, :]
            sg = _sigmoid(og)
            rinv = lax.rsqrt(jnp.mean(o * o, axis=-1, keepdims=True) + RMS_EPS)
            yn = o * rinv
            dof = do_ref[rows, :]
            dog_ref[rows, :] = (dof * yn * gain * _silu_grad(og, sg)).astype(BF16)
            dz = dof * og * sg
            dhg = dhg + jnp.sum(dz * yn, axis=0, keepdims=True)
            dy = dz * gain
            do = rinv * (dy - yn * jnp.mean(dy * yn, axis=-1, keepdims=True))
            sprev = st_ref[0, ci]
            dsn = ds_ref[...]
            blast = cum_ref[pl.ds(CHUNK - 1, 1), :]
            eq0 = jnp.exp(cum)
            ek0 = jnp.exp(blast - cum)
            dqh = _bdot(do, sprev, NN_DIMS) * eq0
            dkk = _bdot(v, dsn, NN_DIMS) * ek0
            dlast = (jnp.sum(kk * dkk, axis=0, keepdims=True)
                     + jnp.exp(blast) * jnp.sum(dsn * sprev, axis=0, keepdims=True))
            dv = _bdot(kk * ek0, dsn, NT_DIMS)
            ds_ref[...] = dsn * jnp.exp(blast) + _bdot(do, qh * eq0, TN_DIMS)
            dg = qh * dqh - kk * dkk
            da = _bdot(do, v, NT_DIMS)
            a = None
            for (eq, ek), msk in zip(_level_factors(cum, _level_refs(cum_ref, rid), second), same):
                ql, kl = (qh * eq).astype(BF16), (kk * ek).astype(BF16)
                al = _bdot(ql, kl, NT_DIMS)
                dal = da
                if msk is not None:
                    al = jnp.where(msk, al, 0.0)
                    dal = jnp.where(msk, da, 0.0)
                a = al if a is None else a + al
                dql = _bdot(dal, kl, NN_DIMS)
                dkl = _bdot(dal, ql, TN_DIMS)
                dqh = dqh + dql * eq
                dkk = dkk + dkl * ek
                dg = dg + (ql.astype(F32) * dql - kl.astype(F32) * dkl)
            dv = dv + _bdot(a, do, TN_DIMS)
            dq_d, dk_d, dv_d = [], [], []
            for sb in range(CHUNK // SUB):
                lo = sb * SUB
                qb = qh[lo:lo + SUB]
                cb = cum[lo:lo + SUB]
                dob = do[lo:lo + SUB]
                dqb = jnp.zeros((SUB, c), F32)
                dkb = jnp.zeros((SUB, c), F32)
                dvb = jnp.zeros((SUB, c), F32)
                for s in range(SUB):
                    e = jnp.where(sub >= s, jnp.exp(jnp.minimum(cb - cum_ref[pl.ds(lo + s, 1), :], 0.0)), 0.0)
                    ks = kk_ref[pl.ds(lo + s, 1), :]
                    qe = qb * e
                    dacol = jnp.sum(dob * vc_ref[pl.ds(lo + s, 1), :], axis=-1, keepdims=True)
                    acol = jnp.sum(qe * ks, axis=-1, keepdims=True)
                    dqb = dqb + dacol * (ks * e)
                    dkb = jnp.where(sub == s, jnp.sum(dacol * qe, axis=0, keepdims=True), dkb)
                    dvb = jnp.where(sub == s, jnp.sum(acol * dob, axis=0, keepdims=True), dvb)
                dq_d.append(dqb)
                dk_d.append(dkb)
                dv_d.append(dvb)
            dq_d = jnp.concatenate(dq_d, axis=0)
            dk_d = jnp.concatenate(dk_d, axis=0)
            dqh = dqh + dq_d
            dkk = dkk + dk_d
            dg = dg + (qh * dq_d - kk * dk_d)
            dv = dv + jnp.concatenate(dv_d, axis=0)
            dlf = _hdot(trit, dg) + dlast
            dfg = dlf / fg - dkk
            df_ref[rows, :] = (dfg * (1.0 - lb) * sf * (1.0 - sf)).astype(BF16)
            dlb = dlb + jnp.sum(dfg * (1.0 - sf), axis=0, keepdims=True)
            dq_ref[rows, :] = (dqh * _silu_grad(q, sq)).astype(BF16)
            dv_ref[rows, :] = dv.astype(BF16)
            return dhg, dlb

        zero = jnp.zeros((1, c), F32)
        dhg, dlb = lax.fori_loop(0, nc, chunk, (zero, zero))
        dhg_ref[...] = dhg
        dl0 = dlb * lb * (1.0 - lb)
        dlb_ref[...] = jnp.where(lax.broadcasted_iota(jnp.int32, (2, c), 0) == 0, dl0, -dl0)

    big = jax.ShapeDtypeStruct((t, HGRN_WIDTH), BF16)
    return pl.pallas_call(
        body, name=name, grid=(N_GROUPS,),
        in_specs=[col(8)] + hin_specs + [col(0), st, lbs, vec],
        out_specs=[col(0)] * 4 + [vec, lbs],
        out_shape=[big] * 4 + [jax.ShapeDtypeStruct((1, HGRN_WIDTH), F32), jax.ShapeDtypeStruct((2, HGRN_WIDTH), F32)],
        scratch_shapes=[pltpu.VMEM((c, c), F32)] + [pltpu.VMEM((CHUNK, c), F32)] * 3,
        compiler_params=_cparams(("parallel",)),
    )(dcat, hin, hin, hin, hin, o_raw, states, lb_logits, hg)


ANY = pl.BlockSpec(memory_space=pl.ANY)


def _my_place():
    return lax.axis_index("x"), lax.axis_index("y"), lax.axis_index("c")


HBM = pl.BlockSpec(memory_space=pltpu.HBM)
SEM = pl.BlockSpec(memory_space=pltpu.SEMAPHORE)
EFFECT = pltpu.SideEffectType.DATAFLOW_SIDE_EFFECTING


def _peer(k):
    x, y, c = _my_place()
    px = 1 - x if k & 4 else x
    py = 1 - y if k & 2 else y
    pc = 1 - c if k & 1 else c
    return (px, py, pc), 4 * px + 2 * py + pc


def _exchange_copy(k, src_ref, land_ref, send_sems, recv_sems, scatter, landing):
    x, y, c = _my_place()
    me = 4 * x + 2 * y + c
    to, idx = _peer(k)
    return pltpu.make_async_remote_copy(
        src_ref=src_ref.at[idx] if scatter else src_ref,
        dst_ref=land_ref.at[idx] if landing else land_ref.at[me],
        send_sem=send_sems.at[k - 1], recv_sem=recv_sems.at[k - 1], device_id=to, device_id_type=MESH)


def _exchange_start(name, src, land, scatter):
    def body(src_ref, land_ref, send_sems, recv_sems, src_thru, land_thru, token):
        for k in range(1, N_DEV):
            _exchange_copy(k, src_ref, land_ref, send_sems, recv_sems, scatter, landing=False).start()
        token[...] = jnp.zeros_like(token)

    send_sems, recv_sems, src_thru, land_thru, token = pl.pallas_call(
        body, name=name,
        out_shape=(pltpu.SemaphoreType.DMA((N_DEV - 1,)), pltpu.SemaphoreType.DMA((N_DEV - 1,)),
                   pltpu.HBM(src.shape, src.dtype), pltpu.HBM(land.shape, land.dtype),
                   jax.ShapeDtypeStruct((8, 128), F32)),
        in_specs=(HBM, HBM), out_specs=(SEM, SEM, HBM, HBM, pl.BlockSpec(memory_space=pltpu.VMEM)),
        input_output_aliases={0: 2, 1: 3},
        compiler_params=pltpu.CompilerParams(has_side_effects=EFFECT),
    )(pltpu.with_memory_space_constraint(src, pltpu.HBM), pltpu.with_memory_space_constraint(land, pltpu.HBM))
    return (send_sems, recv_sems, src_thru, land_thru, scatter), token


def _exchange_wait(name, handle, after):
    send_sems, recv_sems, src_thru, land_thru, scatter = handle

    def body(src_ref, land_ref, send_sems, recv_sems, after_ref, src_dead, got_ref):
        for k in range(1, N_DEV):
            cp = _exchange_copy(k, src_ref, land_ref, send_sems, recv_sems, scatter, landing=True)
            cp.wait_send()
            cp.wait_recv()

    return pl.pallas_call(
        body, name=name,
        out_shape=(pltpu.HBM(src_thru.shape, src_thru.dtype), pltpu.HBM(land_thru.shape, land_thru.dtype)),
        in_specs=(HBM, HBM, SEM, SEM, ANY), out_specs=(HBM, HBM), input_output_aliases={0: 0, 1: 1},
        compiler_params=pltpu.CompilerParams(has_side_effects=EFFECT),
    )(src_thru, land_thru, send_sems, recv_sems, after)[1]


def _own_slot(own, me):
    land = lax.empty((N_DEV,) + own.shape, own.dtype)
    return lax.dynamic_update_slice_in_dim(land, own[None], me, axis=0)


def _adamw_math(w, g, m, v):
    m = ADAM_B1 * m + (1.0 - ADAM_B1) * g
    v = ADAM_B2 * v + (1.0 - ADAM_B2) * (g * g)
    m_hat = m / (1.0 - ADAM_B1 ** ADAM_STEP)
    v_hat = v / (1.0 - ADAM_B2 ** ADAM_STEP)
    delta = -ADAM_LR * (m_hat / (jnp.sqrt(v_hat) + ADAM_EPS) + ADAM_WD * w)
    return delta, m, v


def _adamw_sum(name, recv, w, m, v, tr, row0=0, partial=None):
    r, c = w.shape
    rr = recv.shape[1]
    off = row0 // tr

    def body(recv_ref, w_ref, m_ref, v_ref, *refs):
        g_ref, d_ref, mo_ref, vo_ref = refs[-4:]
        g = recv_ref[0].astype(F32)
        for j in range(1, N_DEV):
            g = g + recv_ref[j].astype(F32)
        g_ref[...] = g
        d_ref[...], mo_ref[...], vo_ref[...] = _adamw_math(w_ref[...], g, m_ref[...], v_ref[...])

    tile = pl.BlockSpec((tr, c), lambda i: (i + off, 0))
    out = jax.ShapeDtypeStruct((r, c), F32)
    prev = list(partial) if partial is not None else []
    return pl.pallas_call(
        body, name=name, grid=(rr // tr,),
        in_specs=[pl.BlockSpec((N_DEV, tr, c), lambda i: (0, i, 0)), tile, tile, tile] + [ANY] * len(prev),
        out_specs=[tile] * 4, out_shape=[out] * 4,
        input_output_aliases={4 + i: i for i in range(len(prev))},
        compiler_params=_cparams(("parallel",)),
    )(recv, w, m, v, *prev)


def _sum_parts(name, parts):
    _, r, c = parts.shape

    def body(p_ref, o_ref):
        acc = p_ref[0]
        for j in range(1, N_DEV):
            acc = acc + p_ref[j]
        o_ref[...] = acc

    return pl.pallas_call(body, name=name, out_shape=jax.ShapeDtypeStruct((r, c), F32),
                          compiler_params=_cparams())(parts)


def _adamw_small(name, w, g, m, v):
    def body(w_ref, g_ref, m_ref, v_ref, d_ref, mo_ref, vo_ref):
        d_ref[...], mo_ref[...], vo_ref[...] = _adamw_math(w_ref[...], g_ref[...], m_ref[...], v_ref[...])

    out = jax.ShapeDtypeStruct(w.shape, F32)
    return pl.pallas_call(body, name=name, out_shape=[out] * 3, compiler_params=_cparams())(w, g, m, v)


def _pack(pieces, rows):
    flat = jnp.concatenate([p.reshape(-1).astype(F32) for p in pieces])
    return jnp.pad(flat, (0, rows * 128 - flat.shape[0])).reshape(rows, 128)


def _unpack(packed, shapes):
    flat = packed.reshape(-1)
    out, off = [], 0
    for s in shapes:
        n = 1
        for d in s:
            n *= d
        out.append(flat[off:off + n].reshape(s))
        off += n
    return out


def kernel(x, emb_ln_g, emb_ln_b, w_in, conv_w, conv_b, conv_norm_g, conv_norm_b, lb_logits, hgrn_norm_g, w_out, ln1_g, ln1_b, w_ffn_up, ffn_conv_w, ffn_conv_b, w_ffn_down, ln2_g, ln2_b, loss_target, m_emb_ln_g, m_emb_ln_b, m_w_in, m_conv_w, m_conv_b, m_conv_norm_g, m_conv_norm_b, m_lb_logits, m_hgrn_norm_g, m_w_out, m_ln1_g, m_ln1_b, m_w_ffn_up, m_ffn_conv_w, m_ffn_conv_b, m_w_ffn_down, m_ln2_g, m_ln2_b, v_emb_ln_g, v_emb_ln_b, v_w_in, v_conv_w, v_conv_b, v_conv_norm_g, v_conv_norm_b, v_lb_logits, v_hgrn_norm_g, v_w_out, v_ln1_g, v_ln1_b, v_w_ffn_up, v_ffn_conv_w, v_ffn_conv_b, v_w_ffn_down, v_ln2_g, v_ln2_b):
    t = x.shape[1]
    me = 4 * lax.axis_index("x") + 2 * lax.axis_index("y") + lax.axis_index("c")
    x2, tgt = x[0], loss_target[0]
    ns_in, ns_up = w_in.shape[2], w_ffn_up.shape[2]
    rs_out, rs_down = w_out.shape[1], w_ffn_down.shape[1]
    cs, fs = conv_w.shape[2], ffn_conv_w.shape[2]

    def gather_start(name, w, prev):
        shard = (w[0] + prev).astype(BF16)
        return _exchange_start(name, shard, _own_slot(shard, me), scatter=False)

    h_in, tok = gather_start("ag_w_in_start", w_in, 0.0)
    taps = _pack([conv_w[0], ffn_conv_w[0]], 48) + tok[0, 0]
    h_taps, tok = _exchange_start("ag_taps_start", taps, _own_slot(taps, me), scatter=False)
    h_out, tok = gather_start("ag_w_out_start", w_out, tok[0, 0])
    h_up, tok = gather_start("ag_w_up_start", w_ffn_up, tok[0, 0])
    h_down, tok = gather_start("ag_w_down_start", w_ffn_down, tok[0, 0])

    row = lambda a: a.reshape(1, -1)

    _, h0, h0b = _ln_fwd("ln_in", x2, None, row(emb_ln_g) + tok[0, 0], row(emb_ln_b), 1.0)
    win_g = _exchange_wait("ag_w_in_wait", h_in, h0b)
    win_n = win_g.transpose(1, 0, 2).reshape(D_MODEL, IN_PROJ)
    hin = _mm_nn("mm_in", h0b, win_n, F32, tm=1024, tn=ns_in, tk=D_MODEL)
    n_cw, n_fw = CONV_KERNEL * cs, FFN_KERNEL * fs
    taps_g = _exchange_wait("ag_taps_wait", h_taps, hin).reshape(N_DEV, -1)
    cw_full = taps_g[:, :n_cw].reshape(N_DEV, CONV_KERNEL, cs).transpose(1, 0, 2).reshape(CONV_KERNEL, CONV_WIDTH)
    fw_full = taps_g[:, n_cw:n_cw + n_fw].reshape(N_DEV, FFN_KERNEL, fs).transpose(1, 0, 2).reshape(FFN_KERNEL, D_FF)

    u1, u3b = _conv_fwd("conv_fwd", hin, cw_full, conv_b, conv_norm_g, conv_norm_b)
    o_raw, ob, states = _hgrn_fwd("hgrn_fwd", hin, lb_logits, hgrn_norm_g)
    catb = jnp.concatenate([u3b, ob], axis=1)
    wout_g = _exchange_wait("ag_w_out_wait", h_out, catb).reshape(D_MODEL, D_MODEL)
    mix = _mm_nn("mm_out", catb, wout_g, F32, tm=1024, tn=1024, tk=D_MODEL)
    r1, h1, h1b = _ln_fwd("ln1", h0, mix, ln1_g, ln1_b, ALPHA)
    wup_g = _exchange_wait("ag_w_up_wait", h_up, h1b)
    wup_n = wup_g.transpose(1, 0, 2).reshape(D_MODEL, 2 * D_FF)
    hf = _mm_nn("mm_up", h1b, wup_n, F32, tm=1024, tn=ns_up, tk=D_MODEL)
    actb = _ffn_act_fwd("ffn_act", hf, fw_full, ffn_conv_b)
    wdown_g = _exchange_wait("ag_w_down_wait", h_down, actb).reshape(D_FF, D_MODEL)
    ffn = _mm_nn("mm_down", actb, wdown_g, F32, tm=512, tn=1024, tk=D_FF)
    dr2, dr2b, g_ln2g, g_ln2b, loss = _ln2_loss_bwd("ln2_loss", h1, ffn, ln2_g, ln2_b, tgt)

    def scatter_start(name, parts):
        own = lax.dynamic_index_in_dim(parts, me, axis=0, keepdims=False)
        return _exchange_start(name, parts, _own_slot(own, me), scatter=True)

    dact = _mm_nt("mm_dact", dr2b, wdown_g, F32, tm=1024, tn=1408, tk=D_MODEL)
    gw_down = _mm_nn("mm_dw_down", actb.T, dr2b, BF16, tm=rs_down, tn=1024, tk=t)
    s_down, tok = scatter_start("a2a_w_down_start", gw_down.reshape(N_DEV, rs_down, D_MODEL))
    dhf, g_fw, g_fb = _ffn_act_bwd("ffn_act_bwd", dact, hf, fw_full, ffn_conv_b + tok[0, 0])
    tm = min(1024, t)
    gw_up = _matmul(
        "mm_dw_up", h1b.T, dhf, (N_DEV, D_MODEL, ns_up), BF16, (D_MODEL // 1024, N_DEV, 1),
        pl.BlockSpec((1024, t), lambda i, j, kk: (i, 0)),
        pl.BlockSpec((1, t, ns_up), lambda i, j, kk: (j // 4, 0, j % 4)),
        pl.BlockSpec((1, 1024, ns_up), lambda i, j, kk: (j, i, 0)), nt=False)
    s_up, tok = scatter_start("a2a_w_up_start", gw_up)
    tkf = D_FF // 2
    dh1 = _matmul(
        "mm_dh1", dhf, wup_n, (t, D_MODEL), F32, (t // tm, D_MODEL // 1024, 4),
        pl.BlockSpec((1, tm, tkf), lambda i, j, kk: (kk // 2, i, kk % 2)),
        pl.BlockSpec((1024, tkf), lambda i, j, kk: (j, kk)),
        pl.BlockSpec((tm, 1024), lambda i, j, kk: (i, j)), nt=True)
    dr1, dr1b, g_ln1g, g_ln1b = _ln_bwd("ln1_bwd", r1, dr2, dh1, ln1_g + tok[0, 0], ALPHA, True)
    gw_out = _mm_nn("mm_dw_out", catb.T, dr1b, BF16, tm=1024, tn=1024, tk=t)
    s_out, tok = scatter_start("a2a_w_out_start", gw_out.reshape(N_DEV, rs_out, D_MODEL))
    dcat = _mm_nt("mm_dcat", dr1b, wout_g, F32, tm=1024, tn=1024, tk=D_MODEL)
    da, dgate, g_cw, g_cb, g_cng, g_cnb = _conv_bwd("conv_bwd", dcat, u1, hin, cw_full, conv_norm_g + tok[0, 0],
                                                    conv_norm_b)
    dq, df, di, dog, g_hg, g_lb = _hgrn_bwd("hgrn_bwd", dcat, hin, o_raw, states, lb_logits, hgrn_norm_g)
    dhin = jnp.concatenate([da, dgate, dq, df, di, dog], axis=1)
    dh0 = _mm_nt("mm_dh0", dhin, win_n, F32, tm=1024, tn=1024, tk=IN_PROJ // 2)
    grad_x, g_eg, g_eb = _ln_bwd("ln_in_bwd", x2, dr1, dh0, row(emb_ln_g), ALPHA, False)

    small_shapes = [(D_MODEL,), (D_MODEL,), (CONV_KERNEL, CONV_WIDTH), (1, CONV_WIDTH), (1, CONV_WIDTH),
                    (1, CONV_WIDTH), (2, HGRN_WIDTH), (1, HGRN_WIDTH), (1, D_MODEL), (1, D_MODEL),
                    (FFN_KERNEL, D_FF), (1, D_FF), (1, D_MODEL), (1, D_MODEL), (128,)]
    rows_small = 569
    packed = _pack([g_eg, g_eb, g_cw[:CONV_KERNEL], g_cb, g_cng, g_cnb, g_lb, g_hg, g_ln1g, g_ln1b,
                    g_fw[:FFN_KERNEL], g_fb, g_ln2g, g_ln2b, loss], rows_small)
    h_small, tok = _exchange_start("ag_small_start", packed, _own_slot(packed, me), scatter=False)
    half = D_MODEL // 2
    h0bt = h0b.T
    gw_in_a = _mm_grad_cols("mm_dw_in_a", h0bt, dhin, ns_in, 0, half)
    s_in_a, tok = scatter_start("a2a_w_in_a_start", gw_in_a)
    gw_in_b = _mm_grad_cols("mm_dw_in_b", h0bt, dhin, ns_in, half, half)
    s_in_b, tok = scatter_start("a2a_w_in_b_start", gw_in_b)
    summed = _sum_parts("sum_small", _exchange_wait("ag_small_wait", h_small, tok))
    (s_eg, s_eb, s_cw, s_cb, s_cng, s_cnb, s_lb, s_hg, s_l1g, s_l1b, s_fw, s_fb, s_l2g, s_l2b,
     s_loss) = _unpack(summed, small_shapes)
    s_cw = lax.dynamic_slice_in_dim(s_cw, me * cs, cs, axis=1)[None]
    s_fw = lax.dynamic_slice_in_dim(s_fw, me * fs, fs, axis=1)[None]
    g_small = [s_eg, s_eb, s_cw, s_cb, s_cng, s_cnb, s_lb, s_hg, s_l1g, s_l1b, s_fw, s_fb, s_l2g, s_l2b]
    w_small = [emb_ln_g, emb_ln_b, conv_w, conv_b, conv_norm_g, conv_norm_b, lb_logits, hgrn_norm_g,
               ln1_g, ln1_b, ffn_conv_w, ffn_conv_b, ln2_g, ln2_b]
    m_small = [m_emb_ln_g, m_emb_ln_b, m_conv_w, m_conv_b, m_conv_norm_g, m_conv_norm_b, m_lb_logits,
               m_hgrn_norm_g, m_ln1_g, m_ln1_b, m_ffn_conv_w, m_ffn_conv_b, m_ln2_g, m_ln2_b]
    v_small = [v_emb_ln_g, v_emb_ln_b, v_conv_w, v_conv_b, v_conv_norm_g, v_conv_norm_b, v_lb_logits,
               v_hgrn_norm_g, v_ln1_g, v_ln1_b, v_ffn_conv_w, v_ffn_conv_b, v_ln2_g, v_ln2_b]
    rows_own = 236
    shapes_own = [w.shape for w in w_small]
    upd = _adamw_small("adamw_small", _pack(w_small, rows_own), _pack(g_small, rows_own),
                       _pack(m_small, rows_own), _pack(v_small, rows_own))
    d_small, nm_small, nv_small = (_unpack(u, shapes_own) for u in upd)
    g_small = [g.reshape(s) for g, s in zip(g_small, shapes_own)]

    def big(name, handle, after, w, m, v, tr):
        recv = _exchange_wait("a2a_" + name + "_wait", handle, after)
        return [o[None] for o in _adamw_sum("adamw_" + name, recv, w[0], m[0], v[0], tr)]

    u_down = big("w_down", s_down, upd[0], w_ffn_down, m_w_ffn_down, v_w_ffn_down, 64)
    u_up = big("w_up", s_up, u_down[1], w_ffn_up, m_w_ffn_up, v_w_ffn_up, 64)
    u_out = big("w_out", s_out, u_up[1], w_out, m_w_out, v_w_out, 64)
    recv_a = _exchange_wait("a2a_w_in_a_wait", s_in_a, u_out[1])
    part = _adamw_sum("adamw_w_in_a", recv_a, w_in[0], m_w_in[0], v_w_in[0], 128)
    recv_b = _exchange_wait("a2a_w_in_b_wait", s_in_b, part[1])
    u_in = [o[None] for o in _adamw_sum("adamw_w_in_b", recv_b, w_in[0], m_w_in[0], v_w_in[0], 128,
                                        row0=half, partial=part)]

    def ordered(small, i_in, i_out, i_up, i_down):
        (eg, eb, cw, cb, cng, cnb, lb, hg, l1g, l1b, fw, fb, l2g, l2b) = small
        return [eg, eb, i_in, cw, cb, cng, cnb, lb, hg, i_out, l1g, l1b, i_up, fw, fb, i_down, l2g, l2b]

    outs = [s_loss[0], grad_x[None]]
    for k, small in enumerate([g_small, d_small, nm_small, nv_small]):
        outs += ordered(small, u_in[k], u_out[k], u_up[k], u_down[k])
    return tuple(outs)
```

```python
import functools

import jax
import jax.numpy as jnp
from jax import lax
from jax.experimental import pallas as pl
from jax.experimental.pallas import tpu as pltpu

F32 = jnp.float32
BF16 = jnp.bfloat16

N_DEV = 8
D_MODEL = 2048
CONV_WIDTH = 1024
CONV_KERNEL = 31
HGRN_WIDTH = 1024
GROUP = 128
N_GROUPS = 8
IN_PROJ = 2 * CONV_WIDTH + 4 * HGRN_WIDTH
D_FF = 5632
FFN_KERNEL = 3
CHUNK = 64
SUB = 8
LN_EPS = 1e-5
RMS_EPS = 1e-6
ALPHA = 2.0 ** 0.25
ADAM_LR, ADAM_B1, ADAM_B2, ADAM_EPS, ADAM_WD, ADAM_STEP = 0.001, 0.9, 0.999, 1e-08, 0.01, 10

VMEM_LIMIT = 56 * 1024 * 1024
MESH = pl.DeviceIdType.MESH


def _cparams(sem=None):
    return pltpu.CompilerParams(dimension_semantics=sem, vmem_limit_bytes=VMEM_LIMIT)


def _sigmoid(x):
    return 1.0 / (1.0 + jnp.exp(-x))


def _matmul(name, a, b, out_shape, out_dtype, grid, a_spec, b_spec, o_spec, nt, after=None):
    nk = grid[2]
    dims = (((1,), (1,)), ((), ())) if nt else (((1,), (0,)), ((), ()))
    extra = [] if after is None else [after]

    def body(a_ref, b_ref, *rest):
        o_ref, *scratch = rest[len(extra):]
        av = a_ref[0] if len(a_ref.shape) == 3 else a_ref[...]
        bv = b_ref[0] if len(b_ref.shape) == 3 else b_ref[...]
        part = lax.dot_general(av, bv, dims, preferred_element_type=F32)

        def write(res):
            if len(o_ref.shape) == 3:
                o_ref[0] = res.astype(out_dtype)
            else:
                o_ref[...] = res.astype(out_dtype)

        if nk == 1:
            write(part)
            return
        acc_ref, = scratch
        k = pl.program_id(2)

        @pl.when(k == 0)
        def _():
            acc_ref[...] = part

        @pl.when(jnp.logical_and(k > 0, k < nk - 1))
        def _():
            acc_ref[...] += part

        @pl.when(k == nk - 1)
        def _():
            write(acc_ref[...] + part)

    acc_shape = o_spec.block_shape[-2:]
    assert all(g >= 1 for g in grid), (name, grid)
    return pl.pallas_call(
        body, name=name, grid=grid, in_specs=[a_spec, b_spec] + [pl.BlockSpec(memory_space=pl.ANY)] * len(extra),
        out_specs=o_spec, out_shape=jax.ShapeDtypeStruct(out_shape, out_dtype),
        scratch_shapes=[pltpu.VMEM(acc_shape, F32)] if nk > 1 else [],
        compiler_params=_cparams(("parallel", "parallel", "arbitrary")),
    )(a, b, *extra)


def _mm_nn(name, a, w, out_dtype, tm, tn, tk, after=None):
    m, k = a.shape
    tm, tk = min(tm, m), min(tk, k)
    n = w.shape[1]
    return _matmul(
        name, a, w, (m, n), out_dtype, (m // tm, n // tn, k // tk),
        pl.BlockSpec((tm, tk), lambda i, j, kk: (i, kk)),
        pl.BlockSpec((tk, tn), lambda i, j, kk: (kk, j)),
        pl.BlockSpec((tm, tn), lambda i, j, kk: (i, j)), nt=False, after=after)


def _mm_nt(name, a, w, out_dtype, tm, tn, tk, after=None):
    m, k = a.shape
    tm = min(tm, m)
    n = w.shape[0]
    return _matmul(
        name, a, w, (m, n), out_dtype, (m // tm, n // tn, k // tk),
        pl.BlockSpec((tm, tk), lambda i, j, kk: (i, kk)),
        pl.BlockSpec((tn, tk), lambda i, j, kk: (j, kk)),
        pl.BlockSpec((tm, tn), lambda i, j, kk: (i, j)), nt=True, after=after)


def _mm_grad_cols(name, at, b, ns, row0, rows, after, tm=1024, tk=4096):
    t = at.shape[1]
    tk = min(tk, t)
    off = row0 // tm
    return _matmul(
        name, at, b, (N_DEV, rows, ns), BF16, (rows // tm, N_DEV, t // tk),
        pl.BlockSpec((tm, tk), lambda i, j, kk: (i + off, kk)),
        pl.BlockSpec((tk, ns), lambda i, j, kk: (kk, j)),
        pl.BlockSpec((1, tm, ns), lambda i, j, kk: (j, i, 0)), nt=False, after=after)


LN_ROWS = 256


def _ln_stats(r):
    mu = jnp.mean(r, axis=-1, keepdims=True)
    xc = r - mu
    var = jnp.mean(xc * xc, axis=-1, keepdims=True)
    rstd = lax.rsqrt(var + LN_EPS)
    return xc * rstd, rstd


def _row_spec(d):
    return pl.BlockSpec((LN_ROWS, d), lambda i: (i, 0))


def _vec_spec(d):
    return pl.BlockSpec((1, d), lambda i: (0, 0))


def _ln_fwd(name, a, m, g, b, alpha):
    t, d = a.shape
    has_m = m is not None

    def body(*refs):
        if has_m:
            a_ref, m_ref, g_ref, b_ref, r_ref, y_ref, yb_ref = refs
            r = alpha * a_ref[...] + m_ref[...]
            r_ref[...] = r
        else:
            a_ref, g_ref, b_ref, y_ref, yb_ref = refs
            r = a_ref[...]
        xhat, _ = _ln_stats(r)
        y = xhat * g_ref[...] + b_ref[...]
        y_ref[...] = y
        yb_ref[...] = y.astype(BF16)

    ins = [a] + ([m] if has_m else []) + [g, b]
    in_specs = [_row_spec(d)] * (2 if has_m else 1) + [_vec_spec(d)] * 2
    outs = ([jax.ShapeDtypeStruct((t, d), F32)] if has_m else []) + [
        jax.ShapeDtypeStruct((t, d), F32), jax.ShapeDtypeStruct((t, d), BF16)]
    res = pl.pallas_call(
        body, name=name, grid=(t // LN_ROWS,), in_specs=in_specs,
        out_specs=[_row_spec(d)] * len(outs), out_shape=outs,
        compiler_params=_cparams(("parallel",)),
    )(*ins)
    return res if has_m else (None, *res)


def _ln_bwd_math(r, dy, g):
    xhat, rstd = _ln_stats(r)
    dxhat = dy * g
    m1 = jnp.mean(dxhat, axis=-1, keepdims=True)
    m2 = jnp.mean(dxhat * xhat, axis=-1, keepdims=True)
    dr = rstd * (dxhat - m1 - xhat * m2)
    return dr, jnp.sum(dy * xhat, axis=0, keepdims=True), jnp.sum(dy, axis=0, keepdims=True)


def _ln2_loss_bwd(name, h1, ffn, g, b, tgt):
    t, d = h1.shape

    def body(h1_ref, f_ref, g_ref, b_ref, t_ref, dr_ref, drb_ref, dg_ref, db_ref, loss_ref):
        @pl.when(pl.program_id(0) == 0)
        def _():
            dg_ref[...] = jnp.zeros_like(dg_ref)
            db_ref[...] = jnp.zeros_like(db_ref)
            loss_ref[...] = jnp.zeros_like(loss_ref)

        r = ALPHA * h1_ref[...] + f_ref[...]
        xhat, _ = _ln_stats(r)
        e = xhat * g_ref[...] + b_ref[...] - t_ref[...]
        loss_ref[...] += 0.5 / d * jnp.sum(e * e)
        dr, dg, db = _ln_bwd_math(r, e * (1.0 / d), g_ref[...])
        dr_ref[...] = dr
        drb_ref[...] = dr.astype(BF16)
        dg_ref[...] += dg
        db_ref[...] += db

    return pl.pallas_call(
        body, name=name, grid=(t // LN_ROWS,),
        in_specs=[_row_spec(d), _row_spec(d), _vec_spec(d), _vec_spec(d), _row_spec(d)],
        out_specs=[_row_spec(d), _row_spec(d), _vec_spec(d), _vec_spec(d), _vec_spec(128)],
        out_shape=[jax.ShapeDtypeStruct((t, d), F32), jax.ShapeDtypeStruct((t, d), BF16),
                   jax.ShapeDtypeStruct((1, d), F32), jax.ShapeDtypeStruct((1, d), F32),
                   jax.ShapeDtypeStruct((1, 128), F32)],
        compiler_params=_cparams(("arbitrary",)),
    )(h1, ffn, g, b, tgt)


def _ln_bwd(name, r, dya, dyb, g, alpha, want_bf16):
    t, d = r.shape

    def body(r_ref, dya_ref, dyb_ref, g_ref, *outs):
        dr_ref = outs[0]
        dg_ref, db_ref = outs[-2:]

        @pl.when(pl.program_id(0) == 0)
        def _():
            dg_ref[...] = jnp.zeros_like(dg_ref)
            db_ref[...] = jnp.zeros_like(db_ref)

        dy = alpha * dya_ref[...] + dyb_ref[...]
        dr, dg, db = _ln_bwd_math(r_ref[...], dy, g_ref[...])
        dr_ref[...] = dr
        if want_bf16:
            outs[1][...] = dr.astype(BF16)
        dg_ref[...] += dg
        db_ref[...] += db

    big = [jax.ShapeDtypeStruct((t, d), F32)] + ([jax.ShapeDtypeStruct((t, d), BF16)] if want_bf16 else [])
    return pl.pallas_call(
        body, name=name, grid=(t // LN_ROWS,),
        in_specs=[_row_spec(d)] * 3 + [_vec_spec(d)],
        out_specs=[_row_spec(d)] * len(big) + [_vec_spec(d)] * 2,
        out_shape=big + [jax.ShapeDtypeStruct((1, d), F32)] * 2,
        compiler_params=_cparams(("arbitrary",)),
    )(r, dya, dyb, g)


CONV_ROWS = 64
CONV_UNROLL = 2


def _for_shifted(ref, r0, tm, shifts, fn):
    for s in shifts:
        fn(s, ref[pl.ds(r0 + s, tm), :])


def _col_spec(t, cb, off=0):
    return pl.BlockSpec((t, cb), lambda j: (0, j + off))


def _ffn_act_fwd(name, hf, w, b, cb=128):
    t = hf.shape[0]
    f = hf.shape[1] // 2
    nb = f // cb
    tm = CONV_ROWS

    def body(g_ref, v_ref, w_ref, b_ref, act_ref, pad_ref):
        pad_ref[pl.ds(0, 8), :] = jnp.zeros((8, cb), F32)
        pad_ref[pl.ds(8, t), :] = g_ref[...].astype(F32)
        wv = [w_ref[pl.ds(k, 1), :] for k in range(FFN_KERNEL)]
        bias = b_ref[...]

        def tile(i, carry):
            r0 = pl.multiple_of(i * tm, tm)
            acc = [jnp.broadcast_to(bias, (tm, cb))]

            def tap(s, rows):
                acc[0] = acc[0] + wv[s - 6] * rows

            _for_shifted(pad_ref, r0, tm, (6, 7, 8), tap)
            gc = acc[0]
            act_ref[pl.ds(r0, tm), :] = (gc * _sigmoid(gc) * v_ref[pl.ds(r0, tm), :].astype(F32)).astype(BF16)
            return carry

        lax.fori_loop(0, t // tm, tile, 0)

    return pl.pallas_call(
        body, name=name, grid=(nb,),
        in_specs=[_col_spec(t, cb), _col_spec(t, cb, nb),
                  pl.BlockSpec((FFN_KERNEL, cb), lambda j: (0, j)), pl.BlockSpec((1, cb), lambda j: (0, j))],
        out_specs=_col_spec(t, cb), out_shape=jax.ShapeDtypeStruct((t, f), BF16),
        scratch_shapes=[pltpu.VMEM((t + 8, cb), F32)],
        compiler_params=_cparams(("parallel",)),
    )(hf, hf, w, b)


def _ffn_act_bwd(name, dact, hf, w, b, cb=128):
    t = hf.shape[0]
    f = hf.shape[1] // 2
    nb = f // cb
    tm = CONV_ROWS

    def body(da_ref, g_ref, v_ref, w_ref, b_ref, dhf_ref, dw_ref, db_ref, pad_ref, dgc_ref):
        pad_ref[pl.ds(0, 8), :] = jnp.zeros((8, cb), F32)
        pad_ref[pl.ds(8, t), :] = g_ref[...].astype(F32)
        dgc_ref[pl.ds(t, 8), :] = jnp.zeros((8, cb), F32)
        wv = [w_ref[pl.ds(k, 1), :] for k in range(FFN_KERNEL)]
        bias = b_ref[...]

        def tile_a(i, carry):
            r0 = pl.multiple_of(i * tm, tm)
            taps = {}
            _for_shifted(pad_ref, r0, tm, (6, 7, 8), lambda s, rows: taps.__setitem__(s, rows))
            gc = bias + wv[0] * taps[6] + wv[1] * taps[7] + wv[2] * taps[8]
            sg = _sigmoid(gc)
            da = da_ref[pl.ds(r0, tm), :].astype(F32)
            dhf_ref[1, pl.ds(r0, tm), :] = (da * gc * sg).astype(BF16)
            dgc = da * v_ref[pl.ds(r0, tm), :].astype(F32) * sg * (1.0 + gc * (1.0 - sg))
            dgc_ref[pl.ds(r0, tm), :] = dgc
            sums = [jnp.sum(dgc * taps[6 + k], axis=0, keepdims=True) for k in range(3)]
            sums.append(jnp.sum(dgc, axis=0, keepdims=True))
            return tuple(c + s for c, s in zip(carry, sums))

        zero = jnp.zeros((1, cb), F32)
        dw0, dw1, dw2, dbias = lax.fori_loop(0, t // tm, tile_a, (zero, zero, zero, zero))
        row = lax.broadcasted_iota(jnp.int32, (8, cb), 0)
        dw_ref[...] = jnp.where(row == 0, dw0, jnp.where(row == 1, dw1, jnp.where(row == 2, dw2, 0.0)))
        db_ref[...] = dbias

        def tile_b(i, carry):
            r0 = pl.multiple_of(i * tm, tm)
            acc = [jnp.zeros((tm, cb), F32)]

            def tap(s, rows):
                acc[0] = acc[0] + wv[2 - s] * rows

            _for_shifted(dgc_ref, r0, tm, (0, 1, 2), tap)
            dhf_ref[0, pl.ds(r0, tm), :] = acc[0].astype(BF16)
            return carry

        lax.fori_loop(0, t // tm, tile_b, 0)

    return pl.pallas_call(
        body, name=name, grid=(nb,),
        in_specs=[_col_spec(t, cb), _col_spec(t, cb), _col_spec(t, cb, nb),
                  pl.BlockSpec((FFN_KERNEL, cb), lambda j: (0, j)), pl.BlockSpec((1, cb), lambda j: (0, j))],
        out_specs=[pl.BlockSpec((2, t, cb), lambda j: (0, 0, j)),
                   pl.BlockSpec((8, cb), lambda j: (0, j)), pl.BlockSpec((1, cb), lambda j: (0, j))],
        out_shape=[jax.ShapeDtypeStruct((2, t, f), BF16), jax.ShapeDtypeStruct((8, f), F32),
                   jax.ShapeDtypeStruct((1, f), F32)],
        scratch_shapes=[pltpu.VMEM((t + 8, cb), F32), pltpu.VMEM((t + 8, cb), F32)],
        compiler_params=_cparams(("parallel",)),
    )(dact, hf, hf, w, b)


def _silu_grad(z, sg):
    return sg * (1.0 + z * (1.0 - sg))


def _conv_fwd(name, hin, w, b, ng, nb_):
    t = hin.shape[0]
    c = GROUP
    tm = CONV_ROWS
    pad = 32
    shifts = tuple(2 + k for k in range(CONV_KERNEL))

    def body(a_ref, gt_ref, w_ref, b_ref, ng_ref, nb_ref, u1_ref, u3_ref, pad_ref):
        pad_ref[pl.ds(0, pad), :] = jnp.zeros((pad, c), F32)
        pad_ref[pl.ds(pad, t), :] = a_ref[...] * _sigmoid(gt_ref[...])
        bias, gam, bet = b_ref[...], ng_ref[...], nb_ref[...]

        def tile(i, carry):
            r0 = pl.multiple_of(i * tm, tm)
            acc = [jnp.broadcast_to(bias, (tm, c))]

            def tap(s, rows):
                acc[0] = acc[0] + w_ref[pl.ds(s - 2, 1), :] * rows

            _for_shifted(pad_ref, r0, tm, shifts, tap)
            u1 = acc[0]
            u1_ref[pl.ds(r0, tm), :] = u1
            xhat, _ = _ln_stats(u1)
            u2 = xhat * gam + bet
            u3_ref[pl.ds(r0, tm), :] = (u2 * _sigmoid(u2)).astype(BF16)
            return carry

        lax.fori_loop(0, t // tm, tile, 0, unroll=CONV_UNROLL)

    vec = pl.BlockSpec((1, c), lambda j: (0, j))
    return pl.pallas_call(
        body, name=name, grid=(N_GROUPS,),
        in_specs=[_col_spec(t, c), _col_spec(t, c, N_GROUPS),
                  pl.BlockSpec((CONV_KERNEL, c), lambda j: (0, j)), vec, vec, vec],
        out_specs=[_col_spec(t, c), _col_spec(t, c)],
        out_shape=[jax.ShapeDtypeStruct((t, CONV_WIDTH), F32), jax.ShapeDtypeStruct((t, CONV_WIDTH), BF16)],
        scratch_shapes=[pltpu.VMEM((t + pad, c), F32)],
        compiler_params=_cparams(("parallel",)),
    )(hin, hin, w, b, ng, nb_)


def _conv_bwd(name, dcat, u1, hin, w, ng, nb_):
    t = hin.shape[0]
    c = GROUP
    tm = CONV_ROWS
    pad = 32
    nk = CONV_KERNEL

    def body(du3_ref, u1_ref, a_ref, gt_ref, w_ref, ng_ref, nb_ref,
             da_ref, dgt_ref, dw_ref, db_ref, dng_ref, dnb_ref, u0_ref, du1_ref, dwp_ref):
        u0_ref[pl.ds(0, pad), :] = jnp.zeros((pad, c), F32)
        u0_ref[pl.ds(pad, t), :] = a_ref[...] * _sigmoid(gt_ref[...])
        du1_ref[pl.ds(t, pad), :] = jnp.zeros((pad, c), F32)
        dwp_ref[...] = jnp.zeros_like(dwp_ref)
        gam, bet = ng_ref[...], nb_ref[...]

        def tile_a(i, carry):
            r0 = pl.multiple_of(i * tm, tm)
            u1 = u1_ref[pl.ds(r0, tm), :]
            xhat, rstd = _ln_stats(u1)
            u2 = xhat * gam + bet
            sg = _sigmoid(u2)
            du2 = du3_ref[pl.ds(r0, tm), :] * _silu_grad(u2, sg)
            dxhat = du2 * gam
            m1 = jnp.mean(dxhat, axis=-1, keepdims=True)
            m2 = jnp.mean(dxhat * xhat, axis=-1, keepdims=True)
            du1 = rstd * (dxhat - m1 - xhat * m2)
            du1_ref[pl.ds(r0, tm), :] = du1
            sums = (jnp.sum(du1, axis=0, keepdims=True), jnp.sum(du2 * xhat, axis=0, keepdims=True),
                    jnp.sum(du2, axis=0, keepdims=True))
            return tuple(x + s for x, s in zip(carry, sums))

        zero = jnp.zeros((1, c), F32)
        def tiles_a(i, carry):
            for u in range(CONV_UNROLL):
                carry = tile_a(i * CONV_UNROLL + u, carry)
            return carry

        dbias, dgam, dbet = lax.fori_loop(0, t // (tm * CONV_UNROLL), tiles_a, (zero, zero, zero))
        db_ref[...] = dbias
        dng_ref[...] = dgam
        dnb_ref[...] = dbet

        def tile_b(i, carry):
            r0 = pl.multiple_of(i * tm, tm)
            du1 = du1_ref[pl.ds(r0, tm), :]
            acc = [jnp.zeros((tm, c), F32)]

            def tap_dx(s, rows):
                acc[0] = acc[0] + w_ref[pl.ds(nk - 1 - s, 1), :] * rows

            _for_shifted(du1_ref, r0, tm, tuple(range(nk)), tap_dx)

            def tap_dw(s, rows):
                part = (du1 * rows).reshape(tm // 8, 8, c).sum(axis=0)
                dwp_ref[s - 2] = dwp_ref[s - 2] + part

            _for_shifted(u0_ref, r0, tm, tuple(2 + k for k in range(nk)), tap_dw)
            du0 = acc[0]
            a = a_ref[pl.ds(r0, tm), :]
            sg = _sigmoid(gt_ref[pl.ds(r0, tm), :])
            da_ref[pl.ds(r0, tm), :] = (du0 * sg).astype(BF16)
            dgt_ref[pl.ds(r0, tm), :] = (du0 * a * sg * (1.0 - sg)).astype(BF16)
            return carry

        lax.fori_loop(0, t // tm, tile_b, 0)
        dw_ref[...] = jnp.sum(dwp_ref[...], axis=1)

    vec = pl.BlockSpec((1, c), lambda j: (0, j))
    vshape = jax.ShapeDtypeStruct((1, CONV_WIDTH), F32)
    return pl.pallas_call(
        body, name=name, grid=(N_GROUPS,),
        in_specs=[_col_spec(t, c), _col_spec(t, c), _col_spec(t, c), _col_spec(t, c, N_GROUPS),
                  pl.BlockSpec((nk, c), lambda j: (0, j)), vec, vec],
        out_specs=[_col_spec(t, c), _col_spec(t, c), pl.BlockSpec((32, c), lambda j: (0, j)), vec, vec, vec],
        out_shape=[jax.ShapeDtypeStruct((t, CONV_WIDTH), BF16), jax.ShapeDtypeStruct((t, CONV_WIDTH), BF16),
                   jax.ShapeDtypeStruct((32, CONV_WIDTH), F32), vshape, vshape, vshape],
        scratch_shapes=[pltpu.VMEM((t + pad, c), F32), pltpu.VMEM((t + pad, c), F32),
                        pltpu.VMEM((32, 8, c), F32)],
        compiler_params=_cparams(("parallel",)),
    )(dcat, u1, hin, hin, w, ng, nb_)


LEVELS = (64, 32, 16)
NT_DIMS = (((1,), (1,)), ((), ()))
NN_DIMS = (((1,), (0,)), ((), ()))
TN_DIMS = (((0,), (0,)), ((), ()))


def _bdot(a, b, dims):
    return lax.dot_general(a.astype(BF16), b.astype(BF16), dims, preferred_element_type=F32)


def _hdot(a, b):
    return jnp.dot(a, b, precision=lax.Precision.HIGHEST, preferred_element_type=F32)


def _chunk_consts():
    rid = lax.broadcasted_iota(jnp.int32, (CHUNK, GROUP), 0)
    ti = lax.broadcasted_iota(jnp.int32, (CHUNK, CHUNK), 0)
    si = lax.broadcasted_iota(jnp.int32, (CHUNK, CHUNK), 1)
    tri = (si <= ti).astype(F32)
    second = [(rid & (b // 2)) != 0 for b in LEVELS]
    same = [None] + [(ti // b) == (si // b) for b in LEVELS[1:]]
    sub = lax.broadcasted_iota(jnp.int32, (SUB, GROUP), 0)
    return rid, tri, second, same, sub


def _level_refs(cum_ref, rid):
    row = lambda i: cum_ref[pl.ds(i, 1), :]
    l1 = jnp.broadcast_to(row(31), (CHUNK, GROUP))
    l2 = jnp.where(rid < 32, row(15), row(47))
    l3 = jnp.where(rid < 16, row(7), jnp.where(rid < 32, row(23), jnp.where(rid < 48, row(39), row(55))))
    return l1, l2, l3


def _level_factors(cum, brefs, second):
    out = []
    for bref, sec in zip(brefs, second):
        eq = jnp.where(sec, jnp.exp(jnp.minimum(cum - bref, 0.0)), 0.0)
        ek = jnp.where(sec, 0.0, jnp.exp(jnp.minimum(bref - cum, 0.0)))
        out.append((eq, ek))
    return out


def _gates(q, f, lb):
    sq = _sigmoid(q)
    sf = _sigmoid(f)
    fg = lb + (1.0 - lb) * sf
    return q * sq, sq, sf, fg


def _hgrn_specs(t, nc):
    c = GROUP
    col = lambda off: pl.BlockSpec((t, c), lambda h: (0, h + off))
    hin_specs = [col(16), col(24), col(32), col(40)]
    vec = pl.BlockSpec((1, c), lambda h: (0, h))
    lbs = pl.BlockSpec((2, c), lambda h: (0, h))
    st = pl.BlockSpec((1, nc, c, c), lambda h: (h, 0, 0, 0))
    return col, hin_specs, vec, lbs, st


def _hgrn_fwd(name, hin, lb_logits, hg):
    t = hin.shape[0]
    nc = t // CHUNK
    c = GROUP
    col, hin_specs, vec, lbs, st = _hgrn_specs(t, nc)

    def body(q_ref, f_ref, v_ref, og_ref, lb_ref, hg_ref, o_ref, ob_ref, st_ref,
             s_ref, cum_ref, kk_ref, vc_ref):
        rid, tri, second, same, sub = _chunk_consts()
        lb = _sigmoid(lb_ref[pl.ds(0, 1), :] - lb_ref[pl.ds(1, 1), :])
        gain = hg_ref[...]
        s_ref[...] = jnp.zeros_like(s_ref)

        def chunk(ci, carry):
            r0 = pl.multiple_of(ci * CHUNK, CHUNK)
            rows = pl.ds(r0, CHUNK)
            qh, _, _, fg = _gates(q_ref[rows, :], f_ref[rows, :], lb)
            v = v_ref[rows, :]
            kk = 1.0 - fg
            cum = _hdot(tri, jnp.log(fg))
            cum_ref[...] = cum
            kk_ref[...] = kk
            vc_ref[...] = v
            sprev = s_ref[...]
            st_ref[0, ci] = sprev
            blast = cum_ref[pl.ds(CHUNK - 1, 1), :]
            o = _bdot(qh * jnp.exp(cum), sprev, NT_DIMS)
            s_ref[...] = sprev * jnp.exp(blast) + _bdot(v, kk * jnp.exp(blast - cum), TN_DIMS)
            a = None
            for (eq, ek), msk in zip(_level_factors(cum, _level_refs(cum_ref, rid), second), same):
                al = _bdot(qh * eq, kk * ek, NT_DIMS)
                al = al if msk is None else jnp.where(msk, al, 0.0)
                a = al if a is None else a + al
            o = o + _bdot(a, v, NN_DIMS)
            diag = []
            for sb in range(CHUNK // SUB):
                lo = sb * SUB
                qb = qh[lo:lo + SUB]
                cb = cum[lo:lo + SUB]
                od = jnp.zeros((SUB, c), F32)
                for s in range(SUB):
                    e = jnp.where(sub >= s, jnp.exp(jnp.minimum(cb - cum_ref[pl.ds(lo + s, 1), :], 0.0)), 0.0)
                    acol = jnp.sum(qb * e * kk_ref[pl.ds(lo + s, 1), :], axis=-1, keepdims=True)
                    od = od + acol * vc_ref[pl.ds(lo + s, 1), :]
                diag.append(od)
            o = o + jnp.concatenate(diag, axis=0)
            o_ref[rows, :] = o
            y = o * lax.rsqrt(jnp.mean(o * o, axis=-1, keepdims=True) + RMS_EPS) * gain
            og = og_ref[rows, :]
            ob_ref[rows, :] = (y * og * _sigmoid(og)).astype(BF16)
            return carry

        lax.fori_loop(0, nc, chunk, 0)

    return pl.pallas_call(
        body, name=name, grid=(N_GROUPS,),
        in_specs=hin_specs + [lbs, vec],
        out_specs=[col(0), col(0), st],
        out_shape=[jax.ShapeDtypeStruct((t, HGRN_WIDTH), F32), jax.ShapeDtypeStruct((t, HGRN_WIDTH), BF16),
                   jax.ShapeDtypeStruct((N_GROUPS, nc, c, c), F32)],
        scratch_shapes=[pltpu.VMEM((c, c), F32), pltpu.VMEM((CHUNK, c), F32), pltpu.VMEM((CHUNK, c), F32),
                        pltpu.VMEM((CHUNK, c), F32)],
        compiler_params=_cparams(("parallel",)),
    )(hin, hin, hin, hin, lb_logits, hg)


def _hgrn_bwd(name, dcat, hin, o_raw, states, lb_logits, hg):
    t = hin.shape[0]
    nc = t // CHUNK
    c = GROUP
    col, hin_specs, vec, lbs, st = _hgrn_specs(t, nc)

    def body(do_ref, q_ref, f_ref, v_ref, og_ref, o_ref, st_ref, lb_ref, hg_ref,
             dq_ref, df_ref, dv_ref, dog_ref, dhg_ref, dlb_ref,
             ds_ref, cum_ref, kk_ref, vc_ref):
        rid, tri, second, same, sub = _chunk_consts()
        trit = tri.T
        lb = _sigmoid(lb_ref[pl.ds(0, 1), :] - lb_ref[pl.ds(1, 1), :])
        gain = hg_ref[...]
        ds_ref[...] = jnp.zeros_like(ds_ref)

        def chunk(i, carry):
            dhg, dlb = carry
            ci = nc - 1 - i
            r0 = pl.multiple_of(ci * CHUNK, CHUNK)
            rows = pl.ds(r0, CHUNK)
            q = q_ref[rows, :]
            qh, sq, sf, fg = _gates(q, f_ref[rows, :], lb)
            v = v_ref[rows, :]
            kk = 1.0 - fg
            cum = _hdot(tri, jnp.log(fg))
            cum_ref[...] = cum
            kk_ref[...] = kk
            vc_ref[...] = v
            o = o_ref[rows, :]
            og = og_ref[rows, :]
            sg = _sigmoid(og)
            rinv = lax.rsqrt(jnp.mean(o * o, axis=-1, keepdims=True) + RMS_EPS)
            yn = o * rinv
            dof = do_ref[rows, :]
            dog_ref[rows, :] = (dof * yn * gain * _silu_grad(og, sg)).astype(BF16)
            dz = dof * og * sg
            dhg = dhg + jnp.sum(dz * yn, axis=0, keepdims=True)
            dy = dz * gain
            do = rinv * (dy - yn * jnp.mean(dy * yn, axis=-1, keepdims=True))
            sprev = st_ref[0, ci]
            dsn = ds_ref[...]
            blast = cum_ref[pl.ds(CHUNK - 1, 1), :]
            eq0 = jnp.exp(cum)
            ek0 = jnp.exp(blast - cum)
            dqh = _bdot(do, sprev, NN_DIMS) * eq0
            dkk = _bdot(v, dsn, NN_DIMS) * ek0
            dlast = (jnp.sum(kk * dkk, axis=0, keepdims=True)
                     + jnp.exp(blast) * jnp.sum(dsn * sprev, axis=0, keepdims=True))
            dv = _bdot(kk * ek0, dsn, NT_DIMS)
            ds_ref[...] = dsn * jnp.exp(blast) + _bdot(do, qh * eq0, TN_DIMS)
            dg = qh * dqh - kk * dkk
            da = _bdot(do, v, NT_DIMS)
            a = None
            for (eq, ek), msk in zip(_level_factors(cum, _level_refs(cum_ref, rid), second), same):
                ql, kl = (qh * eq).astype(BF16), (kk * ek).astype(BF16)
                al = _bdot(ql, kl, NT_DIMS)
                dal = da
                if msk is not None:
                    al = jnp.where(msk, al, 0.0)
                    dal = jnp.where(msk, da, 0.0)
                a = al if a is None else a + al
                dql = _bdot(dal, kl, NN_DIMS)
                dkl = _bdot(dal, ql, TN_DIMS)
                dqh = dqh + dql * eq
                dkk = dkk + dkl * ek
                dg = dg + (ql.astype(F32) * dql - kl.astype(F32) * dkl)
            dv = dv + _bdot(a, do, TN_DIMS)
            dq_d, dk_d, dv_d = [], [], []
            for sb in range(CHUNK // SUB):
                lo = sb * SUB
                qb = qh[lo:lo + SUB]
                cb = cum[lo:lo + SUB]
                dob = do[lo:lo + SUB]
                dqb = jnp.zeros((SUB, c), F32)
                dkb = jnp.zeros((SUB, c), F32)
                dvb = jnp.zeros((SUB, c), F32)
                for s in range(SUB):
                    e = jnp.where(sub >= s, jnp.exp(jnp.minimum(cb - cum_ref[pl.ds(lo + s, 1), :], 0.0)), 0.0)
                    ks = kk_ref[pl.ds(lo + s, 1), :]
                    qe = qb * e
                    dacol = jnp.sum(dob * vc_ref[pl.ds(lo + s, 1), :], axis=-1, keepdims=True)
                    acol = jnp.sum(qe * ks, axis=-1, keepdims=True)
                    dqb = dqb + dacol * (ks * e)
                    dkb = jnp.where(sub == s, jnp.sum(dacol * qe, axis=0, keepdims=True), dkb)
                    dvb = jnp.where(sub == s, jnp.sum(acol * dob, axis=0, keepdims=True), dvb)
                dq_d.append(dqb)
                dk_d.append(dkb)
                dv_d.append(dvb)
            dq_d = jnp.concatenate(dq_d, axis=0)
            dk_d = jnp.concatenate(dk_d, axis=0)
            dqh = dqh + dq_d
            dkk = dkk + dk_d
            dg = dg + (qh * dq_d - kk * dk_d)
            dv = dv + jnp.concatenate(dv_d, axis=0)
            dlf = _hdot(trit, dg) + dlast
            dfg = dlf / fg - dkk
            df_ref[rows, :] = (dfg * (1.0 - lb) * sf * (1.0 - sf)).astype(BF16)
            dlb = dlb + jnp.sum(dfg * (1.0 - sf), axis=0, keepdims=True)
            dq_ref[rows, :] = (dqh * _silu_grad(q, sq)).astype(BF16)
            dv_ref[rows, :] = dv.astype(BF16)
            return dhg, dlb

        zero = jnp.zeros((1, c), F32)
        dhg, dlb = lax.fori_loop(0, nc, chunk, (zero, zero))
        dhg_ref[...] = dhg
        dl0 = dlb * lb * (1.0 - lb)
        dlb_ref[...] = jnp.where(lax.broadcasted_iota(jnp.int32, (2, c), 0) == 0, dl0, -dl0)

    big = jax.ShapeDtypeStruct((t, HGRN_WIDTH), BF16)
    return pl.pallas_call(
        body, name=name, grid=(N_GROUPS,),
        in_specs=[col(8)] + hin_specs + [col(0), st, lbs, vec],
        out_specs=[col(0)] * 4 + [vec, lbs],
        out_shape=[big] * 4 + [jax.ShapeDtypeStruct((1, HGRN_WIDTH), F32), jax.ShapeDtypeStruct((2, HGRN_WIDTH), F32)],
        scratch_shapes=[pltpu.VMEM((c, c), F32)] + [pltpu.VMEM((CHUNK, c), F32)] * 3,
        compiler_params=_cparams(("parallel",)),
    )(dcat, hin, hin, hin, hin, o_raw, states, lb_logits, hg)


ANY = pl.BlockSpec(memory_space=pl.ANY)


def _my_place():
    return lax.axis_index("x"), lax.axis_index("y"), lax.axis_index("c")


HBM = pl.BlockSpec(memory_space=pltpu.HBM)
SEM = pl.BlockSpec(memory_space=pltpu.SEMAPHORE)
EFFECT = pltpu.SideEffectType.DATAFLOW_SIDE_EFFECTING


def _peer(k):
    x, y, c = _my_place()
    px = 1 - x if k & 4 else x
    py = 1 - y if k & 2 else y
    pc = 1 - c if k & 1 else c
    return (px, py, pc), 4 * px + 2 * py + pc


def _exchange_copy(k, src_ref, land_ref, send_sems, recv_sems, scatter, landing):
    x, y, c = _my_place()
    me = 4 * x + 2 * y + c
    to, idx = _peer(k)
    return pltpu.make_async_remote_copy(
        src_ref=src_ref.at[idx] if scatter else src_ref,
        dst_ref=land_ref.at[idx] if landing else land_ref.at[me],
        send_sem=send_sems.at[k - 1], recv_sem=recv_sems.at[k - 1], device_id=to, device_id_type=MESH)


def _exchange_start(name, src, land, scatter):
    def body(src_ref, land_ref, send_sems, recv_sems, src_thru, land_thru, token):
        for k in range(1, N_DEV):
            _exchange_copy(k, src_ref, land_ref, send_sems, recv_sems, scatter, landing=False).start()
        token[...] = jnp.zeros_like(token)

    send_sems, recv_sems, src_thru, land_thru, token = pl.pallas_call(
        body, name=name,
        out_shape=(pltpu.SemaphoreType.DMA((N_DEV - 1,)), pltpu.SemaphoreType.DMA((N_DEV - 1,)),
                   pltpu.HBM(src.shape, src.dtype), pltpu.HBM(land.shape, land.dtype),
                   jax.ShapeDtypeStruct((8, 128), F32)),
        in_specs=(HBM, HBM), out_specs=(SEM, SEM, HBM, HBM, pl.BlockSpec(memory_space=pltpu.VMEM)),
        input_output_aliases={0: 2, 1: 3},
        compiler_params=pltpu.CompilerParams(has_side_effects=EFFECT),
    )(pltpu.with_memory_space_constraint(src, pltpu.HBM), pltpu.with_memory_space_constraint(land, pltpu.HBM))
    return (send_sems, recv_sems, src_thru, land_thru, scatter), token


def _exchange_wait(name, handle, after):
    send_sems, recv_sems, src_thru, land_thru, scatter = handle

    def body(src_ref, land_ref, send_sems, recv_sems, after_ref, src_dead, got_ref):
        for k in range(1, N_DEV):
            cp = _exchange_copy(k, src_ref, land_ref, send_sems, recv_sems, scatter, landing=True)
            cp.wait_send()
            cp.wait_recv()

    return pl.pallas_call(
        body, name=name,
        out_shape=(pltpu.HBM(src_thru.shape, src_thru.dtype), pltpu.HBM(land_thru.shape, land_thru.dtype)),
        in_specs=(HBM, HBM, SEM, SEM, ANY), out_specs=(HBM, HBM), input_output_aliases={0: 0, 1: 1},
        compiler_params=pltpu.CompilerParams(has_side_effects=EFFECT),
    )(src_thru, land_thru, send_sems, recv_sems, after)[1]


def _own_slot(own, me):
    land = lax.empty((N_DEV,) + own.shape, own.dtype)
    return lax.dynamic_update_slice_in_dim(land, own[None], me, axis=0)


def _adamw_math(w, g, m, v):
    m = ADAM_B1 * m + (1.0 - ADAM_B1) * g
    v = ADAM_B2 * v + (1.0 - ADAM_B2) * (g * g)
    m_hat = m / (1.0 - ADAM_B1 ** ADAM_STEP)
    v_hat = v / (1.0 - ADAM_B2 ** ADAM_STEP)
    delta = -ADAM_LR * (m_hat / (jnp.sqrt(v_hat) + ADAM_EPS) + ADAM_WD * w)
    return delta, m, v


def _adamw_sum(name, recv, w, m, v, tr, row0=0, partial=None):
    r, c = w.shape
    rr = recv.shape[1]
    off = row0 // tr

    def body(recv_ref, w_ref, m_ref, v_ref, *refs):
        g_ref, d_ref, mo_ref, vo_ref = refs[-4:]
        g = recv_ref[0].astype(F32)
        for j in range(1, N_DEV):
            g = g + recv_ref[j].astype(F32)
        g_ref[...] = g
        d_ref[...], mo_ref[...], vo_ref[...] = _adamw_math(w_ref[...], g, m_ref[...], v_ref[...])

    tile = pl.BlockSpec((tr, c), lambda i: (i + off, 0))
    out = jax.ShapeDtypeStruct((r, c), F32)
    prev = list(partial) if partial is not None else []
    return pl.pallas_call(
        body, name=name, grid=(rr // tr,),
        in_specs=[pl.BlockSpec((N_DEV, tr, c), lambda i: (0, i, 0)), tile, tile, tile] + [ANY] * len(prev),
        out_specs=[tile] * 4, out_shape=[out] * 4,
        input_output_aliases={4 + i: i for i in range(len(prev))},
        compiler_params=_cparams(("parallel",)),
    )(recv, w, m, v, *prev)


def _sum_parts(name, parts):
    _, r, c = parts.shape

    def body(p_ref, o_ref):
        acc = p_ref[0]
        for j in range(1, N_DEV):
            acc = acc + p_ref[j]
        o_ref[...] = acc

    return pl.pallas_call(body, name=name, out_shape=jax.ShapeDtypeStruct((r, c), F32),
                          compiler_params=_cparams())(parts)


def _adamw_small(name, w, g, m, v):
    def body(w_ref, g_ref, m_ref, v_ref, d_ref, mo_ref, vo_ref):
        d_ref[...], mo_ref[...], vo_ref[...] = _adamw_math(w_ref[...], g_ref[...], m_ref[...], v_ref[...])

    out = jax.ShapeDtypeStruct(w.shape, F32)
    return pl.pallas_call(body, name=name, out_shape=[out] * 3, compiler_params=_cparams())(w, g, m, v)


def _pack(pieces, rows):
    flat = jnp.concatenate([p.reshape(-1).astype(F32) for p in pieces])
    return jnp.pad(flat, (0, rows * 128 - flat.shape[0])).reshape(rows, 128)


def _unpack(packed, shapes):
    flat = packed.reshape(-1)
    out, off = [], 0
    for s in shapes:
        n = 1
        for d in s:
            n *= d
        out.append(flat[off:off + n].reshape(s))
        off += n
    return out


def kernel(x, emb_ln_g, emb_ln_b, w_in, conv_w, conv_b, conv_norm_g, conv_norm_b, lb_logits, hgrn_norm_g, w_out, ln1_g, ln1_b, w_ffn_up, ffn_conv_w, ffn_conv_b, w_ffn_down, ln2_g, ln2_b, loss_target, m_emb_ln_g, m_emb_ln_b, m_w_in, m_conv_w, m_conv_b, m_conv_norm_g, m_conv_norm_b, m_lb_logits, m_hgrn_norm_g, m_w_out, m_ln1_g, m_ln1_b, m_w_ffn_up, m_ffn_conv_w, m_ffn_conv_b, m_w_ffn_down, m_ln2_g, m_ln2_b, v_emb_ln_g, v_emb_ln_b, v_w_in, v_conv_w, v_conv_b, v_conv_norm_g, v_conv_norm_b, v_lb_logits, v_hgrn_norm_g, v_w_out, v_ln1_g, v_ln1_b, v_w_ffn_up, v_ffn_conv_w, v_ffn_conv_b, v_w_ffn_down, v_ln2_g, v_ln2_b):
    t = x.shape[1]
    me = 4 * lax.axis_index("x") + 2 * lax.axis_index("y") + lax.axis_index("c")
    x2, tgt = x[0], loss_target[0]
    ns_in, ns_up = w_in.shape[2], w_ffn_up.shape[2]
    rs_out, rs_down = w_out.shape[1], w_ffn_down.shape[1]
    cs, fs = conv_w.shape[2], ffn_conv_w.shape[2]

    def gather_start(name, w, prev):
        shard = (w[0] + prev).astype(BF16)
        return _exchange_start(name, shard, _own_slot(shard, me), scatter=False)

    h_in, tok = gather_start("ag_w_in_start", w_in, 0.0)
    taps = _pack([conv_w[0], ffn_conv_w[0]], 48) + tok[0, 0]
    h_taps, tok = _exchange_start("ag_taps_start", taps, _own_slot(taps, me), scatter=False)
    h_out, tok = gather_start("ag_w_out_start", w_out, tok[0, 0])
    h_up, tok = gather_start("ag_w_up_start", w_ffn_up, tok[0, 0])
    h_down, tok = gather_start("ag_w_down_start", w_ffn_down, tok[0, 0])

    row = lambda a: a.reshape(1, -1)

    _, h0, h0b = _ln_fwd("ln_in", x2, None, row(emb_ln_g) + tok[0, 0], row(emb_ln_b), 1.0)
    win_g = _exchange_wait("ag_w_in_wait", h_in, h0b)
    win_n = win_g.transpose(1, 0, 2).reshape(D_MODEL, IN_PROJ)
    hin = _mm_nn("mm_in", h0b, win_n, F32, tm=1024, tn=ns_in, tk=D_MODEL)
    n_cw, n_fw = CONV_KERNEL * cs, FFN_KERNEL * fs
    taps_g = _exchange_wait("ag_taps_wait", h_taps, hin).reshape(N_DEV, -1)
    cw_full = taps_g[:, :n_cw].reshape(N_DEV, CONV_KERNEL, cs).transpose(1, 0, 2).reshape(CONV_KERNEL, CONV_WIDTH)
    fw_full = taps_g[:, n_cw:n_cw + n_fw].reshape(N_DEV, FFN_KERNEL, fs).transpose(1, 0, 2).reshape(FFN_KERNEL, D_FF)

    u1, u3b = _conv_fwd("conv_fwd", hin, cw_full, conv_b, conv_norm_g, conv_norm_b)
    o_raw, ob, states = _hgrn_fwd("hgrn_fwd", hin, lb_logits, hgrn_norm_g)
    catb = jnp.concatenate([u3b, ob], axis=1)
    wout_g = _exchange_wait("ag_w_out_wait", h_out, catb).reshape(D_MODEL, D_MODEL)
    mix = _mm_nn("mm_out", catb, wout_g, F32, tm=1024, tn=1024, tk=D_MODEL)
    r1, h1, h1b = _ln_fwd("ln1", h0, mix, ln1_g, ln1_b, ALPHA)
    wup_g = _exchange_wait("ag_w_up_wait", h_up, h1b)
    wup_n = wup_g.transpose(1, 0, 2).reshape(D_MODEL, 2 * D_FF)
    hf = _mm_nn("mm_up", h1b, wup_n, BF16, tm=1024, tn=ns_up, tk=D_MODEL)
    actb = _ffn_act_fwd("ffn_act", hf, fw_full, ffn_conv_b)
    wdown_g = _exchange_wait("ag_w_down_wait", h_down, actb).reshape(D_FF, D_MODEL)
    ffn = _mm_nn("mm_down", actb, wdown_g, F32, tm=512, tn=1024, tk=D_FF)
    dr2, dr2b, g_ln2g, g_ln2b, loss = _ln2_loss_bwd("ln2_loss", h1, ffn, ln2_g, ln2_b, tgt)

    def scatter_start(name, parts):
        own = lax.dynamic_index_in_dim(parts, me, axis=0, keepdims=False)
        return _exchange_start(name, parts, _own_slot(own, me), scatter=True)

    dact = _mm_nt("mm_dact", dr2b, wdown_g, BF16, tm=1024, tn=1408, tk=D_MODEL)
    gw_down = _mm_nn("mm_dw_down", actb.T, dr2b, BF16, tm=rs_down, tn=1024, tk=t)
    s_down, tok = scatter_start("a2a_w_down_start", gw_down.reshape(N_DEV, rs_down, D_MODEL))
    dhf, g_fw, g_fb = _ffn_act_bwd("ffn_act_bwd", dact, hf, fw_full, ffn_conv_b + tok[0, 0])
    tm = min(1024, t)
    gw_up = _matmul(
        "mm_dw_up", h1b.T, dhf, (N_DEV, D_MODEL, ns_up), BF16, (D_MODEL // 1024, N_DEV, 1),
        pl.BlockSpec((1024, t), lambda i, j, kk: (i, 0)),
        pl.BlockSpec((1, t, ns_up), lambda i, j, kk: (j // 4, 0, j % 4)),
        pl.BlockSpec((1, 1024, ns_up), lambda i, j, kk: (j, i, 0)), nt=False)
    s_up, tok = scatter_start("a2a_w_up_start", gw_up)
    tkf = D_FF // 2
    dh1 = _matmul(
        "mm_dh1", dhf, wup_n, (t, D_MODEL), F32, (t // tm, D_MODEL // 1024, 4),
        pl.BlockSpec((1, tm, tkf), lambda i, j, kk: (kk // 2, i, kk % 2)),
        pl.BlockSpec((1024, tkf), lambda i, j, kk: (j, kk)),
        pl.BlockSpec((tm, 1024), lambda i, j, kk: (i, j)), nt=True, after=tok)
    dr1, dr1b, g_ln1g, g_ln1b = _ln_bwd("ln1_bwd", r1, dr2, dh1, ln1_g + tok[0, 0], ALPHA, True)
    gw_out = _mm_nn("mm_dw_out", catb.T, dr1b, BF16, tm=1024, tn=1024, tk=t)
    s_out, tok = scatter_start("a2a_w_out_start", gw_out.reshape(N_DEV, rs_out, D_MODEL))
    dcat = _mm_nt("mm_dcat", dr1b, wout_g, F32, tm=1024, tn=1024, tk=D_MODEL, after=tok)
    da, dgate, g_cw, g_cb, g_cng, g_cnb = _conv_bwd("conv_bwd", dcat, u1, hin, cw_full, conv_norm_g + tok[0, 0],
                                                    conv_norm_b)
    dq, df, di, dog, g_hg, g_lb = _hgrn_bwd("hgrn_bwd", dcat, hin, o_raw, states, lb_logits, hgrn_norm_g)
    dhin = jnp.concatenate([da, dgate, dq, df, di, dog], axis=1)
    dh0 = _mm_nt("mm_dh0", dhin, win_n, F32, tm=1024, tn=1024, tk=IN_PROJ // 2)
    grad_x, g_eg, g_eb = _ln_bwd("ln_in_bwd", x2, dr1, dh0, row(emb_ln_g), ALPHA, False)

    small_shapes = [(D_MODEL,), (D_MODEL,), (CONV_KERNEL, CONV_WIDTH), (1, CONV_WIDTH), (1, CONV_WIDTH),
                    (1, CONV_WIDTH), (2, HGRN_WIDTH), (1, HGRN_WIDTH), (1, D_MODEL), (1, D_MODEL),
                    (FFN_KERNEL, D_FF), (1, D_FF), (1, D_MODEL), (1, D_MODEL), (128,)]
    rows_small = 569
    packed = _pack([g_eg, g_eb, g_cw[:CONV_KERNEL], g_cb, g_cng, g_cnb, g_lb, g_hg, g_ln1g, g_ln1b,
                    g_fw[:FFN_KERNEL], g_fb, g_ln2g, g_ln2b, loss], rows_small)
    h_small, tok = _exchange_start("ag_small_start", packed, _own_slot(packed, me), scatter=False)
    half = D_MODEL // 2
    h0bt = h0b.T
    gw_in_a = _mm_grad_cols("mm_dw_in_a", h0bt, dhin, ns_in, 0, half, after=tok)
    s_in_a, tok = scatter_start("a2a_w_in_a_start", gw_in_a)
    gw_in_b = _mm_grad_cols("mm_dw_in_b", h0bt, dhin, ns_in, half, half, after=tok)
    s_in_b, tok = scatter_start("a2a_w_in_b_start", gw_in_b)
    summed = _sum_parts("sum_small", _exchange_wait("ag_small_wait", h_small, tok))
    (s_eg, s_eb, s_cw, s_cb, s_cng, s_cnb, s_lb, s_hg, s_l1g, s_l1b, s_fw, s_fb, s_l2g, s_l2b,
     s_loss) = _unpack(summed, small_shapes)
    s_cw = lax.dynamic_slice_in_dim(s_cw, me * cs, cs, axis=1)[None]
    s_fw = lax.dynamic_slice_in_dim(s_fw, me * fs, fs, axis=1)[None]
    g_small = [s_eg, s_eb, s_cw, s_cb, s_cng, s_cnb, s_lb, s_hg, s_l1g, s_l1b, s_fw, s_fb, s_l2g, s_l2b]
    w_small = [emb_ln_g, emb_ln_b, conv_w, conv_b, conv_norm_g, conv_norm_b, lb_logits, hgrn_norm_g,
               ln1_g, ln1_b, ffn_conv_w, ffn_conv_b, ln2_g, ln2_b]
    m_small = [m_emb_ln_g, m_emb_ln_b, m_conv_w, m_conv_b, m_conv_norm_g, m_conv_norm_b, m_lb_logits,
               m_hgrn_norm_g, m_ln1_g, m_ln1_b, m_ffn_conv_w, m_ffn_conv_b, m_ln2_g, m_ln2_b]
    v_small = [v_emb_ln_g, v_emb_ln_b, v_conv_w, v_conv_b, v_conv_norm_g, v_conv_norm_b, v_lb_logits,
               v_hgrn_norm_g, v_ln1_g, v_ln1_b, v_ffn_conv_w, v_ffn_conv_b, v_ln2_g, v_ln2_b]
    rows_own = 236
    shapes_own = [w.shape for w in w_small]
    upd = _adamw_small("adamw_small", _pack(w_small, rows_own), _pack(g_small, rows_own),
                       _pack(m_small, rows_own), _pack(v_small, rows_own))
    d_small, nm_small, nv_small = (_unpack(u, shapes_own) for u in upd)
    g_small = [g.reshape(s) for g, s in zip(g_small, shapes_own)]

    def big(name, handle, after, w, m, v, tr):
        recv = _exchange_wait("a2a_" + name + "_wait", handle, after)
        return [o[None] for o in _adamw_sum("adamw_" + name, recv, w[0], m[0], v[0], tr)]

    u_down = big("w_down", s_down, upd[0], w_ffn_down, m_w_ffn_down, v_w_ffn_down, 64)
    u_up = big("w_up", s_up, u_down[1], w_ffn_up, m_w_ffn_up, v_w_ffn_up, 64)
    u_out = big("w_out", s_out, u_up[1], w_out, m_w_out, v_w_out, 64)
    recv_a = _exchange_wait("a2a_w_in_a_wait", s_in_a, u_out[1])
    part = _adamw_sum("adamw_w_in_a", recv_a, w_in[0], m_w_in[0], v_w_in[0], 128)
    recv_b = _exchange_wait("a2a_w_in_b_wait", s_in_b, part[1])
    u_in = [o[None] for o in _adamw_sum("adamw_w_in_b", recv_b, w_in[0], m_w_in[0], v_w_in[0], 128,
                                        row0=half, partial=part)]

    def ordered(small, i_in, i_out, i_up, i_down):
        (eg, eb, cw, cb, cng, cnb, lb, hg, l1g, l1b, fw, fb, l2g, l2b) = small
        return [eg, eb, i_in, cw, cb, cng, cnb, lb, hg, i_out, l1g, l1b, i_up, fw, fb, i_down, l2g, l2b]

    outs = [s_loss[0], grad_x[None]]
    for k, small in enumerate([g_small, d_small, nm_small, nv_small]):
        outs += ordered(small, u_in[k], u_out[k], u_up[k], u_down[k])
    return tuple(outs)
```

```python
import functools

import jax
import jax.numpy as jnp
from jax import lax
from jax.experimental import pallas as pl
from jax.experimental.pallas import tpu as pltpu

F32 = jnp.float32
BF16 = jnp.bfloat16

N_DEV = 8
D_MODEL = 2048
CONV_WIDTH = 1024
CONV_KERNEL = 31
HGRN_WIDTH = 1024
GROUP = 128
N_GROUPS = 8
IN_PROJ = 2 * CONV_WIDTH + 4 * HGRN_WIDTH
D_FF = 5632
FFN_KERNEL = 3
CHUNK = 64
SUB = 8
LN_EPS = 1e-5
RMS_EPS = 1e-6
ALPHA = 2.0 ** 0.25
ADAM_LR, ADAM_B1, ADAM_B2, ADAM_EPS, ADAM_WD, ADAM_STEP = 0.001, 0.9, 0.999, 1e-08, 0.01, 10

VMEM_LIMIT = 56 * 1024 * 1024
MESH = pl.DeviceIdType.MESH


def _cparams(sem=None):
    return pltpu.CompilerParams(dimension_semantics=sem, vmem_limit_bytes=VMEM_LIMIT)


def _sigmoid(x):
    return 1.0 / (1.0 + jnp.exp(-x))


def _matmul(name, a, b, out_shape, out_dtype, grid, a_spec, b_spec, o_spec, nt, after=None):
    nk = grid[2]
    dims = (((1,), (1,)), ((), ())) if nt else (((1,), (0,)), ((), ()))
    extra = [] if after is None else [after]

    def body(a_ref, b_ref, *rest):
        o_ref, *scratch = rest[len(extra):]
        av = a_ref[0] if len(a_ref.shape) == 3 else a_ref[...]
        bv = b_ref[0] if len(b_ref.shape) == 3 else b_ref[...]
        part = lax.dot_general(av, bv, dims, preferred_element_type=F32)

        def write(res):
            if len(o_ref.shape) == 3:
                o_ref[0] = res.astype(out_dtype)
            else:
                o_ref[...] = res.astype(out_dtype)

        if nk == 1:
            write(part)
            return
        acc_ref, = scratch
        k = pl.program_id(2)

        @pl.when(k == 0)
        def _():
            acc_ref[...] = part

        @pl.when(jnp.logical_and(k > 0, k < nk - 1))
        def _():
            acc_ref[...] += part

        @pl.when(k == nk - 1)
        def _():
            write(acc_ref[...] + part)

    acc_shape = o_spec.block_shape[-2:]
    assert all(g >= 1 for g in grid), (name, grid)
    return pl.pallas_call(
        body, name=name, grid=grid, in_specs=[a_spec, b_spec] + [pl.BlockSpec(memory_space=pl.ANY)] * len(extra),
        out_specs=o_spec, out_shape=jax.ShapeDtypeStruct(out_shape, out_dtype),
        scratch_shapes=[pltpu.VMEM(acc_shape, F32)] if nk > 1 else [],
        compiler_params=_cparams(("parallel", "parallel", "arbitrary")),
    )(a, b, *extra)


def _mm_nn(name, a, w, out_dtype, tm, tn, tk, after=None):
    m, k = a.shape
    tm, tk = min(tm, m), min(tk, k)
    n = w.shape[1]
    return _matmul(
        name, a, w, (m, n), out_dtype, (m // tm, n // tn, k // tk),
        pl.BlockSpec((tm, tk), lambda i, j, kk: (i, kk)),
        pl.BlockSpec((tk, tn), lambda i, j, kk: (kk, j)),
        pl.BlockSpec((tm, tn), lambda i, j, kk: (i, j)), nt=False, after=after)


def _mm_nt(name, a, w, out_dtype, tm, tn, tk, after=None):
    m, k = a.shape
    tm = min(tm, m)
    n = w.shape[0]
    return _matmul(
        name, a, w, (m, n), out_dtype, (m // tm, n // tn, k // tk),
        pl.BlockSpec((tm, tk), lambda i, j, kk: (i, kk)),
        pl.BlockSpec((tn, tk), lambda i, j, kk: (j, kk)),
        pl.BlockSpec((tm, tn), lambda i, j, kk: (i, j)), nt=True, after=after)


def _mm_grad_cols(name, at, b, ns, row0, rows, after, tm=1024, tk=4096):
    t = at.shape[1]
    tk = min(tk, t)
    off = row0 // tm
    return _matmul(
        name, at, b, (N_DEV, rows, ns), BF16, (rows // tm, N_DEV, t // tk),
        pl.BlockSpec((tm, tk), lambda i, j, kk: (i + off, kk)),
        pl.BlockSpec((tk, ns), lambda i, j, kk: (kk, j)),
        pl.BlockSpec((1, tm, ns), lambda i, j, kk: (j, i, 0)), nt=False, after=after)


LN_ROWS = 256


def _ln_stats(r):
    mu = jnp.mean(r, axis=-1, keepdims=True)
    xc = r - mu
    var = jnp.mean(xc * xc, axis=-1, keepdims=True)
    rstd = lax.rsqrt(var + LN_EPS)
    return xc * rstd, rstd


def _row_spec(d):
    return pl.BlockSpec((LN_ROWS, d), lambda i: (i, 0))


def _vec_spec(d):
    return pl.BlockSpec((1, d), lambda i: (0, 0))


def _ln_fwd(name, a, m, g, b, alpha):
    t, d = a.shape
    has_m = m is not None

    def body(*refs):
        if has_m:
            a_ref, m_ref, g_ref, b_ref, r_ref, y_ref, yb_ref = refs
            r = alpha * a_ref[...] + m_ref[...]
            r_ref[...] = r
        else:
            a_ref, g_ref, b_ref, y_ref, yb_ref = refs
            r = a_ref[...]
        xhat, _ = _ln_stats(r)
        y = xhat * g_ref[...] + b_ref[...]
        y_ref[...] = y
        yb_ref[...] = y.astype(BF16)

    ins = [a] + ([m] if has_m else []) + [g, b]
    in_specs = [_row_spec(d)] * (2 if has_m else 1) + [_vec_spec(d)] * 2
    outs = ([jax.ShapeDtypeStruct((t, d), F32)] if has_m else []) + [
        jax.ShapeDtypeStruct((t, d), F32), jax.ShapeDtypeStruct((t, d), BF16)]
    res = pl.pallas_call(
        body, name=name, grid=(t // LN_ROWS,), in_specs=in_specs,
        out_specs=[_row_spec(d)] * len(outs), out_shape=outs,
        compiler_params=_cparams(("parallel",)),
    )(*ins)
    return res if has_m else (None, *res)


def _ln_bwd_math(r, dy, g):
    xhat, rstd = _ln_stats(r)
    dxhat = dy * g
    m1 = jnp.mean(dxhat, axis=-1, keepdims=True)
    m2 = jnp.mean(dxhat * xhat, axis=-1, keepdims=True)
    dr = rstd * (dxhat - m1 - xhat * m2)
    return dr, jnp.sum(dy * xhat, axis=0, keepdims=True), jnp.sum(dy, axis=0, keepdims=True)


def _ln2_loss_bwd(name, h1, ffn, g, b, tgt):
    t, d = h1.shape

    def body(h1_ref, f_ref, g_ref, b_ref, t_ref, dr_ref, drb_ref, dg_ref, db_ref, loss_ref):
        @pl.when(pl.program_id(0) == 0)
        def _():
            dg_ref[...] = jnp.zeros_like(dg_ref)
            db_ref[...] = jnp.zeros_like(db_ref)
            loss_ref[...] = jnp.zeros_like(loss_ref)

        r = ALPHA * h1_ref[...] + f_ref[...]
        xhat, _ = _ln_stats(r)
        e = xhat * g_ref[...] + b_ref[...] - t_ref[...]
        loss_ref[...] += 0.5 / d * jnp.sum(e * e)
        dr, dg, db = _ln_bwd_math(r, e * (1.0 / d), g_ref[...])
        dr_ref[...] = dr
        drb_ref[...] = dr.astype(BF16)
        dg_ref[...] += dg
        db_ref[...] += db

    return pl.pallas_call(
        body, name=name, grid=(t // LN_ROWS,),
        in_specs=[_row_spec(d), _row_spec(d), _vec_spec(d), _vec_spec(d), _row_spec(d)],
        out_specs=[_row_spec(d), _row_spec(d), _vec_spec(d), _vec_spec(d), _vec_spec(128)],
        out_shape=[jax.ShapeDtypeStruct((t, d), F32), jax.ShapeDtypeStruct((t, d), BF16),
                   jax.ShapeDtypeStruct((1, d), F32), jax.ShapeDtypeStruct((1, d), F32),
                   jax.ShapeDtypeStruct((1, 128), F32)],
        compiler_params=_cparams(("arbitrary",)),
    )(h1, ffn, g, b, tgt)


def _ln_bwd(name, r, dya, dyb, g, alpha, want_bf16):
    t, d = r.shape

    def body(r_ref, dya_ref, dyb_ref, g_ref, *outs):
        dr_ref = outs[0]
        dg_ref, db_ref = outs[-2:]

        @pl.when(pl.program_id(0) == 0)
        def _():
            dg_ref[...] = jnp.zeros_like(dg_ref)
            db_ref[...] = jnp.zeros_like(db_ref)

        dy = alpha * dya_ref[...] + dyb_ref[...]
        dr, dg, db = _ln_bwd_math(r_ref[...], dy, g_ref[...])
        dr_ref[...] = dr
        if want_bf16:
            outs[1][...] = dr.astype(BF16)
        dg_ref[...] += dg
        db_ref[...] += db

    big = [jax.ShapeDtypeStruct((t, d), F32)] + ([jax.ShapeDtypeStruct((t, d), BF16)] if want_bf16 else [])
    return pl.pallas_call(
        body, name=name, grid=(t // LN_ROWS,),
        in_specs=[_row_spec(d)] * 3 + [_vec_spec(d)],
        out_specs=[_row_spec(d)] * len(big) + [_vec_spec(d)] * 2,
        out_shape=big + [jax.ShapeDtypeStruct((1, d), F32)] * 2,
        compiler_params=_cparams(("arbitrary",)),
    )(r, dya, dyb, g)


CONV_ROWS = 64
CONV_UNROLL = 2


def _for_shifted(ref, r0, tm, shifts, fn):
    for s in shifts:
        fn(s, ref[pl.ds(r0 + s, tm), :])


def _col_spec(t, cb, off=0):
    return pl.BlockSpec((t, cb), lambda j: (0, j + off))


def _ffn_act_fwd(name, hf, w, b, cb=128):
    t = hf.shape[0]
    f = hf.shape[1] // 2
    nb = f // cb
    tm = CONV_ROWS

    def body(g_ref, v_ref, w_ref, b_ref, act_ref, pad_ref):
        pad_ref[pl.ds(0, 8), :] = jnp.zeros((8, cb), F32)
        pad_ref[pl.ds(8, t), :] = g_ref[...].astype(F32)
        wv = [w_ref[pl.ds(k, 1), :] for k in range(FFN_KERNEL)]
        bias = b_ref[...]

        def tile(i, carry):
            r0 = pl.multiple_of(i * tm, tm)
            acc = [jnp.broadcast_to(bias, (tm, cb))]

            def tap(s, rows):
                acc[0] = acc[0] + wv[s - 6] * rows

            _for_shifted(pad_ref, r0, tm, (6, 7, 8), tap)
            gc = acc[0]
            act_ref[pl.ds(r0, tm), :] = (gc * _sigmoid(gc) * v_ref[pl.ds(r0, tm), :].astype(F32)).astype(BF16)
            return carry

        lax.fori_loop(0, t // tm, tile, 0)

    return pl.pallas_call(
        body, name=name, grid=(nb,),
        in_specs=[_col_spec(t, cb), _col_spec(t, cb, nb),
                  pl.BlockSpec((FFN_KERNEL, cb), lambda j: (0, j)), pl.BlockSpec((1, cb), lambda j: (0, j))],
        out_specs=_col_spec(t, cb), out_shape=jax.ShapeDtypeStruct((t, f), BF16),
        scratch_shapes=[pltpu.VMEM((t + 8, cb), F32)],
        compiler_params=_cparams(("parallel",)),
    )(hf, hf, w, b)


def _ffn_act_bwd(name, dact, hf, w, b, cb=128):
    t = hf.shape[0]
    f = hf.shape[1] // 2
    nb = f // cb
    tm = CONV_ROWS

    def body(da_ref, g_ref, v_ref, w_ref, b_ref, dhf_ref, dw_ref, db_ref, pad_ref, dgc_ref):
        pad_ref[pl.ds(0, 8), :] = jnp.zeros((8, cb), F32)
        pad_ref[pl.ds(8, t), :] = g_ref[...].astype(F32)
        dgc_ref[pl.ds(t, 8), :] = jnp.zeros((8, cb), F32)
        wv = [w_ref[pl.ds(k, 1), :] for k in range(FFN_KERNEL)]
        bias = b_ref[...]

        def tile_a(i, carry):
            r0 = pl.multiple_of(i * tm, tm)
            taps = {}
            _for_shifted(pad_ref, r0, tm, (6, 7, 8), lambda s, rows: taps.__setitem__(s, rows))
            gc = bias + wv[0] * taps[6] + wv[1] * taps[7] + wv[2] * taps[8]
            sg = _sigmoid(gc)
            da = da_ref[pl.ds(r0, tm), :].astype(F32)
            dhf_ref[1, pl.ds(r0, tm), :] = (da * gc * sg).astype(BF16)
            dgc = da * v_ref[pl.ds(r0, tm), :].astype(F32) * sg * (1.0 + gc * (1.0 - sg))
            dgc_ref[pl.ds(r0, tm), :] = dgc
            sums = [jnp.sum(dgc * taps[6 + k], axis=0, keepdims=True) for k in range(3)]
            sums.append(jnp.sum(dgc, axis=0, keepdims=True))
            return tuple(c + s for c, s in zip(carry, sums))

        zero = jnp.zeros((1, cb), F32)
        dw0, dw1, dw2, dbias = lax.fori_loop(0, t // tm, tile_a, (zero, zero, zero, zero))
        row = lax.broadcasted_iota(jnp.int32, (8, cb), 0)
        dw_ref[...] = jnp.where(row == 0, dw0, jnp.where(row == 1, dw1, jnp.where(row == 2, dw2, 0.0)))
        db_ref[...] = dbias

        def tile_b(i, carry):
            r0 = pl.multiple_of(i * tm, tm)
            acc = [jnp.zeros((tm, cb), F32)]

            def tap(s, rows):
                acc[0] = acc[0] + wv[2 - s] * rows

            _for_shifted(dgc_ref, r0, tm, (0, 1, 2), tap)
            dhf_ref[0, pl.ds(r0, tm), :] = acc[0].astype(BF16)
            return carry

        lax.fori_loop(0, t // tm, tile_b, 0)

    return pl.pallas_call(
        body, name=name, grid=(nb,),
        in_specs=[_col_spec(t, cb), _col_spec(t, cb), _col_spec(t, cb, nb),
                  pl.BlockSpec((FFN_KERNEL, cb), lambda j: (0, j)), pl.BlockSpec((1, cb), lambda j: (0, j))],
        out_specs=[pl.BlockSpec((2, t, cb), lambda j: (0, 0, j)),
                   pl.BlockSpec((8, cb), lambda j: (0, j)), pl.BlockSpec((1, cb), lambda j: (0, j))],
        out_shape=[jax.ShapeDtypeStruct((2, t, f), BF16), jax.ShapeDtypeStruct((8, f), F32),
                   jax.ShapeDtypeStruct((1, f), F32)],
        scratch_shapes=[pltpu.VMEM((t + 8, cb), F32), pltpu.VMEM((t + 8, cb), F32)],
        compiler_params=_cparams(("parallel",)),
    )(dact, hf, hf, w, b)


def _silu_grad(z, sg):
    return sg * (1.0 + z * (1.0 - sg))


def _conv_fwd(name, hin, w, b, ng, nb_):
    t = hin.shape[0]
    c = GROUP
    tm = CONV_ROWS
    pad = 32
    shifts = tuple(2 + k for k in range(CONV_KERNEL))

    def body(a_ref, gt_ref, w_ref, b_ref, ng_ref, nb_ref, u1_ref, u3_ref, pad_ref):
        pad_ref[pl.ds(0, pad), :] = jnp.zeros((pad, c), F32)
        pad_ref[pl.ds(pad, t), :] = a_ref[...] * _sigmoid(gt_ref[...])
        bias, gam, bet = b_ref[...], ng_ref[...], nb_ref[...]

        def tile(i, carry):
            r0 = pl.multiple_of(i * tm, tm)
            acc = [jnp.broadcast_to(bias, (tm, c))]

            def tap(s, rows):
                acc[0] = acc[0] + w_ref[pl.ds(s - 2, 1), :] * rows

            _for_shifted(pad_ref, r0, tm, shifts, tap)
            u1 = acc[0]
            u1_ref[pl.ds(r0, tm), :] = u1
            xhat, _ = _ln_stats(u1)
            u2 = xhat * gam + bet
            u3_ref[pl.ds(r0, tm), :] = (u2 * _sigmoid(u2)).astype(BF16)
            return carry

        lax.fori_loop(0, t // tm, tile, 0, unroll=CONV_UNROLL)

    vec = pl.BlockSpec((1, c), lambda j: (0, j))
    return pl.pallas_call(
        body, name=name, grid=(N_GROUPS,),
        in_specs=[_col_spec(t, c), _col_spec(t, c, N_GROUPS),
                  pl.BlockSpec((CONV_KERNEL, c), lambda j: (0, j)), vec, vec, vec],
        out_specs=[_col_spec(t, c), _col_spec(t, c)],
        out_shape=[jax.ShapeDtypeStruct((t, CONV_WIDTH), F32), jax.ShapeDtypeStruct((t, CONV_WIDTH), BF16)],
        scratch_shapes=[pltpu.VMEM((t + pad, c), F32)],
        compiler_params=_cparams(("parallel",)),
    )(hin, hin, w, b, ng, nb_)


def _conv_bwd(name, dcat, u1, hin, w, ng, nb_):
    t = hin.shape[0]
    c = GROUP
    tm = CONV_ROWS
    pad = 32
    nk = CONV_KERNEL

    def body(du3_ref, u1_ref, a_ref, gt_ref, w_ref, ng_ref, nb_ref,
             da_ref, dgt_ref, dw_ref, db_ref, dng_ref, dnb_ref, u0_ref, du1_ref, dwp_ref):
        u0_ref[pl.ds(0, pad), :] = jnp.zeros((pad, c), F32)
        u0_ref[pl.ds(pad, t), :] = a_ref[...] * _sigmoid(gt_ref[...])
        du1_ref[pl.ds(t, pad), :] = jnp.zeros((pad, c), F32)
        dwp_ref[...] = jnp.zeros_like(dwp_ref)
        gam, bet = ng_ref[...], nb_ref[...]

        def tile_a(i, carry):
            r0 = pl.multiple_of(i * tm, tm)
            u1 = u1_ref[pl.ds(r0, tm), :]
            xhat, rstd = _ln_stats(u1)
            u2 = xhat * gam + bet
            sg = _sigmoid(u2)
            du2 = du3_ref[pl.ds(r0, tm), :] * _silu_grad(u2, sg)
            dxhat = du2 * gam
            m1 = jnp.mean(dxhat, axis=-1, keepdims=True)
            m2 = jnp.mean(dxhat * xhat, axis=-1, keepdims=True)
            du1 = rstd * (dxhat - m1 - xhat * m2)
            du1_ref[pl.ds(r0, tm), :] = du1
            sums = (jnp.sum(du1, axis=0, keepdims=True), jnp.sum(du2 * xhat, axis=0, keepdims=True),
                    jnp.sum(du2, axis=0, keepdims=True))
            return tuple(x + s for x, s in zip(carry, sums))

        zero = jnp.zeros((1, c), F32)
        def tiles_a(i, carry):
            for u in range(CONV_UNROLL):
                carry = tile_a(i * CONV_UNROLL + u, carry)
            return carry

        dbias, dgam, dbet = lax.fori_loop(0, t // (tm * CONV_UNROLL), tiles_a, (zero, zero, zero))
        db_ref[...] = dbias
        dng_ref[...] = dgam
        dnb_ref[...] = dbet

        def tile_b(i, carry):
            r0 = pl.multiple_of(i * tm, tm)
            du1 = du1_ref[pl.ds(r0, tm), :]
            acc = [jnp.zeros((tm, c), F32)]

            def tap_dx(s, rows):
                acc[0] = acc[0] + w_ref[pl.ds(nk - 1 - s, 1), :] * rows

            _for_shifted(du1_ref, r0, tm, tuple(range(nk)), tap_dx)

            def tap_dw(s, rows):
                part = (du1 * rows).reshape(tm // 8, 8, c).sum(axis=0)
                dwp_ref[s - 2] = dwp_ref[s - 2] + part

            _for_shifted(u0_ref, r0, tm, tuple(2 + k for k in range(nk)), tap_dw)
            du0 = acc[0]
            a = a_ref[pl.ds(r0, tm), :]
            sg = _sigmoid(gt_ref[pl.ds(r0, tm), :])
            da_ref[pl.ds(r0, tm), :] = (du0 * sg).astype(BF16)
            dgt_ref[pl.ds(r0, tm), :] = (du0 * a * sg * (1.0 - sg)).astype(BF16)
            return carry

        lax.fori_loop(0, t // tm, tile_b, 0)
        dw_ref[...] = jnp.sum(dwp_ref[...], axis=1)

    vec = pl.BlockSpec((1, c), lambda j: (0, j))
    vshape = jax.ShapeDtypeStruct((1, CONV_WIDTH), F32)
    return pl.pallas_call(
        body, name=name, grid=(N_GROUPS,),
        in_specs=[_col_spec(t, c), _col_spec(t, c), _col_spec(t, c), _col_spec(t, c, N_GROUPS),
                  pl.BlockSpec((nk, c), lambda j: (0, j)), vec, vec],
        out_specs=[_col_spec(t, c), _col_spec(t, c), pl.BlockSpec((32, c), lambda j: (0, j)), vec, vec, vec],
        out_shape=[jax.ShapeDtypeStruct((t, CONV_WIDTH), BF16), jax.ShapeDtypeStruct((t, CONV_WIDTH), BF16),
                   jax.ShapeDtypeStruct((32, CONV_WIDTH), F32), vshape, vshape, vshape],
        scratch_shapes=[pltpu.VMEM((t + pad, c), F32), pltpu.VMEM((t + pad, c), F32),
                        pltpu.VMEM((32, 8, c), F32)],
        compiler_params=_cparams(("parallel",)),
    )(dcat, u1, hin, hin, w, ng, nb_)


LEVELS = (64, 32, 16)
HGRN_UNROLL = 2
HGRN_UNROLL_FWD = 4
NT_DIMS = (((1,), (1,)), ((), ()))
NN_DIMS = (((1,), (0,)), ((), ()))
TN_DIMS = (((0,), (0,)), ((), ()))


def _bdot(a, b, dims):
    return lax.dot_general(a.astype(BF16), b.astype(BF16), dims, preferred_element_type=F32)


def _hdot(a, b):
    return jnp.dot(a, b, precision=lax.Precision.HIGHEST, preferred_element_type=F32)


def _chunk_consts():
    rid = lax.broadcasted_iota(jnp.int32, (CHUNK, GROUP), 0)
    ti = lax.broadcasted_iota(jnp.int32, (CHUNK, CHUNK), 0)
    si = lax.broadcasted_iota(jnp.int32, (CHUNK, CHUNK), 1)
    tri = (si <= ti).astype(F32)
    second = [(rid & (b // 2)) != 0 for b in LEVELS]
    same = [None] + [(ti // b) == (si // b) for b in LEVELS[1:]]
    sub = lax.broadcasted_iota(jnp.int32, (SUB, GROUP), 0)
    return rid, tri, second, same, sub


def _level_refs(cum_ref, rid, base):
    row = lambda i: cum_ref[pl.ds(base + i, 1), :]
    l1 = jnp.broadcast_to(row(31), (CHUNK, GROUP))
    l2 = jnp.where(rid < 32, row(15), row(47))
    l3 = jnp.where(rid < 16, row(7), jnp.where(rid < 32, row(23), jnp.where(rid < 48, row(39), row(55))))
    return l1, l2, l3


def _level_factors(cum, brefs, second):
    out = []
    for bref, sec in zip(brefs, second):
        eq = jnp.where(sec, jnp.exp(jnp.minimum(cum - bref, 0.0)), 0.0)
        ek = jnp.where(sec, 0.0, jnp.exp(jnp.minimum(bref - cum, 0.0)))
        out.append((eq, ek))
    return out


def _gates(q, f, lb):
    sq = _sigmoid(q)
    sf = _sigmoid(f)
    fg = lb + (1.0 - lb) * sf
    return q * sq, sq, sf, fg


def _hgrn_specs(t, nc):
    c = GROUP
    col = lambda off: pl.BlockSpec((t, c), lambda h: (0, h + off))
    hin_specs = [col(16), col(24), col(32), col(40)]
    vec = pl.BlockSpec((1, c), lambda h: (0, h))
    lbs = pl.BlockSpec((2, c), lambda h: (0, h))
    st = pl.BlockSpec((1, nc, c, c), lambda h: (h, 0, 0, 0))
    return col, hin_specs, vec, lbs, st


def _hgrn_fwd(name, hin, lb_logits, hg):
    t = hin.shape[0]
    nc = t // CHUNK
    c = GROUP
    col, hin_specs, vec, lbs, st = _hgrn_specs(t, nc)

    def body(q_ref, f_ref, v_ref, og_ref, lb_ref, hg_ref, o_ref, ob_ref, st_ref,
             s_ref, cum_ref, kk_ref, vc_ref):
        rid, tri, second, same, sub = _chunk_consts()
        lb = _sigmoid(lb_ref[pl.ds(0, 1), :] - lb_ref[pl.ds(1, 1), :])
        gain = hg_ref[...]
        s_ref[...] = jnp.zeros_like(s_ref)

        def chunk(ci, u):
            base = u * CHUNK
            r0 = pl.multiple_of(ci * CHUNK, CHUNK)
            rows = pl.ds(r0, CHUNK)
            qh, _, _, fg = _gates(q_ref[rows, :], f_ref[rows, :], lb)
            v = v_ref[rows, :]
            kk = 1.0 - fg
            cum = _hdot(tri, jnp.log(fg))
            cum_ref[pl.ds(base, CHUNK), :] = cum
            kk_ref[pl.ds(base, CHUNK), :] = kk
            vc_ref[pl.ds(base, CHUNK), :] = v
            sprev = s_ref[...]
            st_ref[0, ci] = sprev
            blast = cum_ref[pl.ds(base + CHUNK - 1, 1), :]
            o = _bdot(qh * jnp.exp(cum), sprev, NT_DIMS)
            s_ref[...] = sprev * jnp.exp(blast) + _bdot(v, kk * jnp.exp(blast - cum), TN_DIMS)
            a = None
            for (eq, ek), msk in zip(_level_factors(cum, _level_refs(cum_ref, rid, base), second), same):
                al = _bdot(qh * eq, kk * ek, NT_DIMS)
                al = al if msk is None else jnp.where(msk, al, 0.0)
                a = al if a is None else a + al
            o = o + _bdot(a, v, NN_DIMS)
            diag = []
            for sb in range(CHUNK // SUB):
                lo = sb * SUB
                qb = qh[lo:lo + SUB]
                cb = cum[lo:lo + SUB]
                od = jnp.zeros((SUB, c), F32)
                for s in range(SUB):
                    e = jnp.where(sub >= s, jnp.exp(jnp.minimum(cb - cum_ref[pl.ds(base + lo + s, 1), :], 0.0)), 0.0)
                    acol = jnp.sum(qb * e * kk_ref[pl.ds(base + lo + s, 1), :], axis=-1, keepdims=True)
                    od = od + acol * vc_ref[pl.ds(base + lo + s, 1), :]
                diag.append(od)
            o = o + jnp.concatenate(diag, axis=0)
            o_ref[rows, :] = o
            y = o * lax.rsqrt(jnp.mean(o * o, axis=-1, keepdims=True) + RMS_EPS) * gain
            og = og_ref[rows, :]
            ob_ref[rows, :] = (y * og * _sigmoid(og)).astype(BF16)

        def chunks(i, carry):
            for u in range(HGRN_UNROLL_FWD):
                chunk(i * HGRN_UNROLL_FWD + u, u)
            return carry

        lax.fori_loop(0, nc // HGRN_UNROLL_FWD, chunks, 0)

    return pl.pallas_call(
        body, name=name, grid=(N_GROUPS,),
        in_specs=hin_specs + [lbs, vec],
        out_specs=[col(0), col(0), st],
        out_shape=[jax.ShapeDtypeStruct((t, HGRN_WIDTH), F32), jax.ShapeDtypeStruct((t, HGRN_WIDTH), BF16),
                   jax.ShapeDtypeStruct((N_GROUPS, nc, c, c), F32)],
        scratch_shapes=[pltpu.VMEM((c, c), F32)] + [pltpu.VMEM((HGRN_UNROLL_FWD * CHUNK, c), F32)] * 3,
        compiler_params=_cparams(("parallel",)),
    )(hin, hin, hin, hin, lb_logits, hg)


def _hgrn_bwd(name, dcat, hin, o_raw, states, lb_logits, hg):
    t = hin.shape[0]
    nc = t // CHUNK
    c = GROUP
    col, hin_specs, vec, lbs, st = _hgrn_specs(t, nc)

    def body(do_ref, q_ref, f_ref, v_ref, og_ref, o_ref, st_ref, lb_ref, hg_ref,
             dq_ref, df_ref, dv_ref, dog_ref, dhg_ref, dlb_ref,
             ds_ref, cum_ref, kk_ref, vc_ref):
        rid, tri, second, same, sub = _chunk_consts()
        trit = tri.T
        lb = _sigmoid(lb_ref[pl.ds(0, 1), :] - lb_ref[pl.ds(1, 1), :])
        gain = hg_ref[...]
        ds_ref[...] = jnp.zeros_like(ds_ref)

        def chunk(i, carry, u):
            base = u * CHUNK
            dhg, dlb = carry
            ci = nc - 1 - i
            r0 = pl.multiple_of(ci * CHUNK, CHUNK)
            rows = pl.ds(r0, CHUNK)
            q = q_ref[rows, :]
            qh, sq, sf, fg = _gates(q, f_ref[rows, :], lb)
            v = v_ref[rows, :]
            kk = 1.0 - fg
            cum = _hdot(tri, jnp.log(fg))
            cum_ref[pl.ds(base, CHUNK), :] = cum
            kk_ref[pl.ds(base, CHUNK), :] = kk
            vc_ref[pl.ds(base, CHUNK), :] = v
            o = o_ref[rows, :]
            og = og_ref[rows, :]
            sg = _sigmoid(og)
            rinv = lax.rsqrt(jnp.mean(o * o, axis=-1, keepdims=True) + RMS_EPS)
            yn = o * rinv
            dof = do_ref[rows, :]
            dog_ref[rows, :] = (dof * yn * gain * _silu_grad(og, sg)).astype(BF16)
            dz = dof * og * sg
            dhg = dhg + jnp.sum(dz * yn, axis=0, keepdims=True)
            dy = dz * gain
            do = rinv * (dy - yn * jnp.mean(dy * yn, axis=-1, keepdims=True))
            sprev = st_ref[0, ci]
            dsn = ds_ref[...]
            blast = cum_ref[pl.ds(base + CHUNK - 1, 1), :]
            eq0 = jnp.exp(cum)
            ek0 = jnp.exp(blast - cum)
            dqh = _bdot(do, sprev, NN_DIMS) * eq0
            dkk = _bdot(v, dsn, NN_DIMS) * ek0
            dlast = (jnp.sum(kk * dkk, axis=0, keepdims=True)
                     + jnp.exp(blast) * jnp.sum(dsn * sprev, axis=0, keepdims=True))
            dv = _bdot(kk * ek0, dsn, NT_DIMS)
            ds_ref[...] = dsn * jnp.exp(blast) + _bdot(do, qh * eq0, TN_DIMS)
            dg = qh * dqh - kk * dkk
            da = _bdot(do, v, NT_DIMS)
            a = None
            for (eq, ek), msk in zip(_level_factors(cum, _level_refs(cum_ref, rid, base), second), same):
                ql, kl = (qh * eq).astype(BF16), (kk * ek).astype(BF16)
                al = _bdot(ql, kl, NT_DIMS)
                dal = da
                if msk is not None:
                    al = jnp.where(msk, al, 0.0)
                    dal = jnp.where(msk, da, 0.0)
                a = al if a is None else a + al
                dql = _bdot(dal, kl, NN_DIMS)
                dkl = _bdot(dal, ql, TN_DIMS)
                dqh = dqh + dql * eq
                dkk = dkk + dkl * ek
                dg = dg + (ql.astype(F32) * dql - kl.astype(F32) * dkl)
            dv = dv + _bdot(a, do, TN_DIMS)
            dq_d, dk_d, dv_d = [], [], []
            for sb in range(CHUNK // SUB):
                lo = sb * SUB
                qb = qh[lo:lo + SUB]
                cb = cum[lo:lo + SUB]
                dob = do[lo:lo + SUB]
                dqb = jnp.zeros((SUB, c), F32)
                dkb = jnp.zeros((SUB, c), F32)
                dvb = jnp.zeros((SUB, c), F32)
                for s in range(SUB):
                    e = jnp.where(sub >= s, jnp.exp(jnp.minimum(cb - cum_ref[pl.ds(base + lo + s, 1), :], 0.0)), 0.0)
                    ks = kk_ref[pl.ds(base + lo + s, 1), :]
                    qe = qb * e
                    dacol = jnp.sum(dob * vc_ref[pl.ds(base + lo + s, 1), :], axis=-1, keepdims=True)
                    acol = jnp.sum(qe * ks, axis=-1, keepdims=True)
                    dqb = dqb + dacol * (ks * e)
                    dkb = jnp.where(sub == s, jnp.sum(dacol * qe, axis=0, keepdims=True), dkb)
                    dvb = jnp.where(sub == s, jnp.sum(acol * dob, axis=0, keepdims=True), dvb)
                dq_d.append(dqb)
                dk_d.append(dkb)
                dv_d.append(dvb)
            dq_d = jnp.concatenate(dq_d, axis=0)
            dk_d = jnp.concatenate(dk_d, axis=0)
            dqh = dqh + dq_d
            dkk = dkk + dk_d
            dg = dg + (qh * dq_d - kk * dk_d)
            dv = dv + jnp.concatenate(dv_d, axis=0)
            dlf = _hdot(trit, dg) + dlast
            dfg = dlf / fg - dkk
            df_ref[rows, :] = (dfg * (1.0 - lb) * sf * (1.0 - sf)).astype(BF16)
            dlb = dlb + jnp.sum(dfg * (1.0 - sf), axis=0, keepdims=True)
            dq_ref[rows, :] = (dqh * _silu_grad(q, sq)).astype(BF16)
            dv_ref[rows, :] = dv.astype(BF16)
            return dhg, dlb

        def chunks(i, carry):
            for u in range(HGRN_UNROLL):
                carry = chunk(i * HGRN_UNROLL + u, carry, u)
            return carry

        zero = jnp.zeros((1, c), F32)
        dhg, dlb = lax.fori_loop(0, nc // HGRN_UNROLL, chunks, (zero, zero))
        dhg_ref[...] = dhg
        dl0 = dlb * lb * (1.0 - lb)
        dlb_ref[...] = jnp.where(lax.broadcasted_iota(jnp.int32, (2, c), 0) == 0, dl0, -dl0)

    big = jax.ShapeDtypeStruct((t, HGRN_WIDTH), BF16)
    return pl.pallas_call(
        body, name=name, grid=(N_GROUPS,),
        in_specs=[col(8)] + hin_specs + [col(0), st, lbs, vec],
        out_specs=[col(0)] * 4 + [vec, lbs],
        out_shape=[big] * 4 + [jax.ShapeDtypeStruct((1, HGRN_WIDTH), F32), jax.ShapeDtypeStruct((2, HGRN_WIDTH), F32)],
        scratch_shapes=[pltpu.VMEM((c, c), F32)] + [pltpu.VMEM((HGRN_UNROLL * CHUNK, c), F32)] * 3,
        compiler_params=_cparams(("parallel",)),
    )(dcat, hin, hin, hin, hin, o_raw, states, lb_logits, hg)


ANY = pl.BlockSpec(memory_space=pl.ANY)


def _my_place():
    return lax.axis_index("x"), lax.axis_index("y"), lax.axis_index("c")


HBM = pl.BlockSpec(memory_space=pltpu.HBM)
SEM = pl.BlockSpec(memory_space=pltpu.SEMAPHORE)
EFFECT = pltpu.SideEffectType.DATAFLOW_SIDE_EFFECTING


def _peer(k):
    x, y, c = _my_place()
    px = 1 - x if k & 4 else x
    py = 1 - y if k & 2 else y
    pc = 1 - c if k & 1 else c
    return (px, py, pc), 4 * px + 2 * py + pc


def _exchange_copy(k, src_ref, land_ref, send_sems, recv_sems, scatter, landing):
    x, y, c = _my_place()
    me = 4 * x + 2 * y + c
    to, idx = _peer(k)
    return pltpu.make_async_remote_copy(
        src_ref=src_ref.at[idx] if scatter else src_ref,
        dst_ref=land_ref.at[idx] if landing else land_ref.at[me],
        send_sem=send_sems.at[k - 1], recv_sem=recv_sems.at[k - 1], device_id=to, device_id_type=MESH)


def _exchange_start(name, src, land, scatter):
    def body(src_ref, land_ref, send_sems, recv_sems, src_thru, land_thru, token):
        for k in range(1, N_DEV):
            _exchange_copy(k, src_ref, land_ref, send_sems, recv_sems, scatter, landing=False).start()
        token[...] = jnp.zeros_like(token)

    send_sems, recv_sems, src_thru, land_thru, token = pl.pallas_call(
        body, name=name,
        out_shape=(pltpu.SemaphoreType.DMA((N_DEV - 1,)), pltpu.SemaphoreType.DMA((N_DEV - 1,)),
                   pltpu.HBM(src.shape, src.dtype), pltpu.HBM(land.shape, land.dtype),
                   jax.ShapeDtypeStruct((8, 128), F32)),
        in_specs=(HBM, HBM), out_specs=(SEM, SEM, HBM, HBM, pl.BlockSpec(memory_space=pltpu.VMEM)),
        input_output_aliases={0: 2, 1: 3},
        compiler_params=pltpu.CompilerParams(has_side_effects=EFFECT),
    )(pltpu.with_memory_space_constraint(src, pltpu.HBM), pltpu.with_memory_space_constraint(land, pltpu.HBM))
    return (send_sems, recv_sems, src_thru, land_thru, scatter), token


def _exchange_wait(name, handle, after):
    send_sems, recv_sems, src_thru, land_thru, scatter = handle

    def body(src_ref, land_ref, send_sems, recv_sems, after_ref, src_dead, got_ref):
        for k in range(1, N_DEV):
            cp = _exchange_copy(k, src_ref, land_ref, send_sems, recv_sems, scatter, landing=True)
            cp.wait_send()
            cp.wait_recv()

    return pl.pallas_call(
        body, name=name,
        out_shape=(pltpu.HBM(src_thru.shape, src_thru.dtype), pltpu.HBM(land_thru.shape, land_thru.dtype)),
        in_specs=(HBM, HBM, SEM, SEM, ANY), out_specs=(HBM, HBM), input_output_aliases={0: 0, 1: 1},
        compiler_params=pltpu.CompilerParams(has_side_effects=EFFECT),
    )(src_thru, land_thru, send_sems, recv_sems, after)[1]


def _own_slot(own, me):
    land = lax.empty((N_DEV,) + own.shape, own.dtype)
    return lax.dynamic_update_slice_in_dim(land, own[None], me, axis=0)


def _adamw_math(w, g, m, v):
    m = ADAM_B1 * m + (1.0 - ADAM_B1) * g
    v = ADAM_B2 * v + (1.0 - ADAM_B2) * (g * g)
    m_hat = m / (1.0 - ADAM_B1 ** ADAM_STEP)
    v_hat = v / (1.0 - ADAM_B2 ** ADAM_STEP)
    delta = -ADAM_LR * (m_hat / (jnp.sqrt(v_hat) + ADAM_EPS) + ADAM_WD * w)
    return delta, m, v


def _adamw_sum(name, recv, w, m, v, tr, row0=0, partial=None):
    r, c = w.shape
    rr = recv.shape[1]
    off = row0 // tr

    def body(recv_ref, w_ref, m_ref, v_ref, *refs):
        g_ref, d_ref, mo_ref, vo_ref = refs[-4:]
        g = recv_ref[0].astype(F32)
        for j in range(1, N_DEV):
            g = g + recv_ref[j].astype(F32)
        g_ref[...] = g
        d_ref[...], mo_ref[...], vo_ref[...] = _adamw_math(w_ref[...], g, m_ref[...], v_ref[...])

    tile = pl.BlockSpec((tr, c), lambda i: (i + off, 0))
    out = jax.ShapeDtypeStruct((r, c), F32)
    prev = list(partial) if partial is not None else []
    return pl.pallas_call(
        body, name=name, grid=(rr // tr,),
        in_specs=[pl.BlockSpec((N_DEV, tr, c), lambda i: (0, i, 0)), tile, tile, tile] + [ANY] * len(prev),
        out_specs=[tile] * 4, out_shape=[out] * 4,
        input_output_aliases={4 + i: i for i in range(len(prev))},
        compiler_params=_cparams(("parallel",)),
    )(recv, w, m, v, *prev)


def _sum_parts(name, parts):
    _, r, c = parts.shape

    def body(p_ref, o_ref):
        acc = p_ref[0]
        for j in range(1, N_DEV):
            acc = acc + p_ref[j]
        o_ref[...] = acc

    return pl.pallas_call(body, name=name, out_shape=jax.ShapeDtypeStruct((r, c), F32),
                          compiler_params=_cparams())(parts)


def _adamw_small(name, w, g, m, v):
    def body(w_ref, g_ref, m_ref, v_ref, d_ref, mo_ref, vo_ref):
        d_ref[...], mo_ref[...], vo_ref[...] = _adamw_math(w_ref[...], g_ref[...], m_ref[...], v_ref[...])

    out = jax.ShapeDtypeStruct(w.shape, F32)
    return pl.pallas_call(body, name=name, out_shape=[out] * 3, compiler_params=_cparams())(w, g, m, v)


def _pack(pieces, rows):
    flat = jnp.concatenate([p.reshape(-1).astype(F32) for p in pieces])
    return jnp.pad(flat, (0, rows * 128 - flat.shape[0])).reshape(rows, 128)


def _unpack(packed, shapes):
    flat = packed.reshape(-1)
    out, off = [], 0
    for s in shapes:
        n = 1
        for d in s:
            n *= d
        out.append(flat[off:off + n].reshape(s))
        off += n
    return out


def kernel(x, emb_ln_g, emb_ln_b, w_in, conv_w, conv_b, conv_norm_g, conv_norm_b, lb_logits, hgrn_norm_g, w_out, ln1_g, ln1_b, w_ffn_up, ffn_conv_w, ffn_conv_b, w_ffn_down, ln2_g, ln2_b, loss_target, m_emb_ln_g, m_emb_ln_b, m_w_in, m_conv_w, m_conv_b, m_conv_norm_g, m_conv_norm_b, m_lb_logits, m_hgrn_norm_g, m_w_out, m_ln1_g, m_ln1_b, m_w_ffn_up, m_ffn_conv_w, m_ffn_conv_b, m_w_ffn_down, m_ln2_g, m_ln2_b, v_emb_ln_g, v_emb_ln_b, v_w_in, v_conv_w, v_conv_b, v_conv_norm_g, v_conv_norm_b, v_lb_logits, v_hgrn_norm_g, v_w_out, v_ln1_g, v_ln1_b, v_w_ffn_up, v_ffn_conv_w, v_ffn_conv_b, v_w_ffn_down, v_ln2_g, v_ln2_b):
    t = x.shape[1]
    me = 4 * lax.axis_index("x") + 2 * lax.axis_index("y") + lax.axis_index("c")
    x2, tgt = x[0], loss_target[0]
    ns_in, ns_up = w_in.shape[2], w_ffn_up.shape[2]
    rs_out, rs_down = w_out.shape[1], w_ffn_down.shape[1]
    cs, fs = conv_w.shape[2], ffn_conv_w.shape[2]

    def gather_start(name, w, prev):
        shard = (w[0] + prev).astype(BF16)
        return _exchange_start(name, shard, _own_slot(shard, me), scatter=False)

    h_in, tok = gather_start("ag_w_in_start", w_in, 0.0)
    taps = _pack([conv_w[0], ffn_conv_w[0]], 48) + tok[0, 0]
    h_taps, tok = _exchange_start("ag_taps_start", taps, _own_slot(taps, me), scatter=False)
    h_out, tok = gather_start("ag_w_out_start", w_out, tok[0, 0])
    h_up, tok = gather_start("ag_w_up_start", w_ffn_up, tok[0, 0])
    h_down, tok = gather_start("ag_w_down_start", w_ffn_down, tok[0, 0])

    row = lambda a: a.reshape(1, -1)

    _, h0, h0b = _ln_fwd("ln_in", x2, None, row(emb_ln_g) + tok[0, 0], row(emb_ln_b), 1.0)
    win_g = _exchange_wait("ag_w_in_wait", h_in, h0b)
    win_n = win_g.transpose(1, 0, 2).reshape(D_MODEL, IN_PROJ)
    hin = _mm_nn("mm_in", h0b, win_n, F32, tm=1024, tn=ns_in, tk=D_MODEL)
    n_cw, n_fw = CONV_KERNEL * cs, FFN_KERNEL * fs
    taps_g = _exchange_wait("ag_taps_wait", h_taps, hin).reshape(N_DEV, -1)
    cw_full = taps_g[:, :n_cw].reshape(N_DEV, CONV_KERNEL, cs).transpose(1, 0, 2).reshape(CONV_KERNEL, CONV_WIDTH)
    fw_full = taps_g[:, n_cw:n_cw + n_fw].reshape(N_DEV, FFN_KERNEL, fs).transpose(1, 0, 2).reshape(FFN_KERNEL, D_FF)

    u1, u3b = _conv_fwd("conv_fwd", hin, cw_full, conv_b, conv_norm_g, conv_norm_b)
    o_raw, ob, states = _hgrn_fwd("hgrn_fwd", hin, lb_logits, hgrn_norm_g)
    catb = jnp.concatenate([u3b, ob], axis=1)
    wout_g = _exchange_wait("ag_w_out_wait", h_out, catb).reshape(D_MODEL, D_MODEL)
    mix = _mm_nn("mm_out", catb, wout_g, F32, tm=1024, tn=1024, tk=D_MODEL)
    r1, h1, h1b = _ln_fwd("ln1", h0, mix, ln1_g, ln1_b, ALPHA)
    wup_g = _exchange_wait("ag_w_up_wait", h_up, h1b)
    wup_n = wup_g.transpose(1, 0, 2).reshape(D_MODEL, 2 * D_FF)
    hf = _mm_nn("mm_up", h1b, wup_n, BF16, tm=1024, tn=ns_up, tk=D_MODEL)
    actb = _ffn_act_fwd("ffn_act", hf, fw_full, ffn_conv_b)
    wdown_g = _exchange_wait("ag_w_down_wait", h_down, actb).reshape(D_FF, D_MODEL)
    ffn = _mm_nn("mm_down", actb, wdown_g, F32, tm=512, tn=1024, tk=D_FF)
    dr2, dr2b, g_ln2g, g_ln2b, loss = _ln2_loss_bwd("ln2_loss", h1, ffn, ln2_g, ln2_b, tgt)

    def scatter_start(name, parts):
        own = lax.dynamic_index_in_dim(parts, me, axis=0, keepdims=False)
        return _exchange_start(name, parts, _own_slot(own, me), scatter=True)

    dact = _mm_nt("mm_dact", dr2b, wdown_g, BF16, tm=1024, tn=1408, tk=D_MODEL)
    gw_down = _mm_nn("mm_dw_down", actb.T, dr2b, BF16, tm=rs_down, tn=1024, tk=t)
    s_down, tok = scatter_start("a2a_w_down_start", gw_down.reshape(N_DEV, rs_down, D_MODEL))
    dhf, g_fw, g_fb = _ffn_act_bwd("ffn_act_bwd", dact, hf, fw_full, ffn_conv_b + tok[0, 0])
    tm = min(1024, t)
    gw_up = _matmul(
        "mm_dw_up", h1b.T, dhf, (N_DEV, D_MODEL, ns_up), BF16, (D_MODEL // 1024, N_DEV, 1),
        pl.BlockSpec((1024, t), lambda i, j, kk: (i, 0)),
        pl.BlockSpec((1, t, ns_up), lambda i, j, kk: (j // 4, 0, j % 4)),
        pl.BlockSpec((1, 1024, ns_up), lambda i, j, kk: (j, i, 0)), nt=False)
    s_up, tok = scatter_start("a2a_w_up_start", gw_up)
    tkf = D_FF // 2
    dh1 = _matmul(
        "mm_dh1", dhf, wup_n, (t, D_MODEL), F32, (t // tm, D_MODEL // 1024, 4),
        pl.BlockSpec((1, tm, tkf), lambda i, j, kk: (kk // 2, i, kk % 2)),
        pl.BlockSpec((1024, tkf), lambda i, j, kk: (j, kk)),
        pl.BlockSpec((tm, 1024), lambda i, j, kk: (i, j)), nt=True, after=tok)
    dr1, dr1b, g_ln1g, g_ln1b = _ln_bwd("ln1_bwd", r1, dr2, dh1, ln1_g + tok[0, 0], ALPHA, True)
    gw_out = _mm_nn("mm_dw_out", catb.T, dr1b, BF16, tm=1024, tn=1024, tk=t)
    s_out, tok = scatter_start("a2a_w_out_start", gw_out.reshape(N_DEV, rs_out, D_MODEL))
    dcat = _mm_nt("mm_dcat", dr1b, wout_g, F32, tm=1024, tn=1024, tk=D_MODEL, after=tok)
    da, dgate, g_cw, g_cb, g_cng, g_cnb = _conv_bwd("conv_bwd", dcat, u1, hin, cw_full, conv_norm_g + tok[0, 0],
                                                    conv_norm_b)
    dq, df, di, dog, g_hg, g_lb = _hgrn_bwd("hgrn_bwd", dcat, hin, o_raw, states, lb_logits, hgrn_norm_g)
    dhin = jnp.concatenate([da, dgate, dq, df, di, dog], axis=1)
    dh0 = _mm_nt("mm_dh0", dhin, win_n, F32, tm=1024, tn=1024, tk=IN_PROJ // 2)
    grad_x, g_eg, g_eb = _ln_bwd("ln_in_bwd", x2, dr1, dh0, row(emb_ln_g), ALPHA, False)

    small_shapes = [(D_MODEL,), (D_MODEL,), (CONV_KERNEL, CONV_WIDTH), (1, CONV_WIDTH), (1, CONV_WIDTH),
                    (1, CONV_WIDTH), (2, HGRN_WIDTH), (1, HGRN_WIDTH), (1, D_MODEL), (1, D_MODEL),
                    (FFN_KERNEL, D_FF), (1, D_FF), (1, D_MODEL), (1, D_MODEL), (128,)]
    rows_small = 569
    packed = _pack([g_eg, g_eb, g_cw[:CONV_KERNEL], g_cb, g_cng, g_cnb, g_lb, g_hg, g_ln1g, g_ln1b,
                    g_fw[:FFN_KERNEL], g_fb, g_ln2g, g_ln2b, loss], rows_small)
    h_small, tok = _exchange_start("ag_small_start", packed, _own_slot(packed, me), scatter=False)
    half = D_MODEL // 2
    h0bt = h0b.T
    gw_in_a = _mm_grad_cols("mm_dw_in_a", h0bt, dhin, ns_in, 0, half, after=tok)
    s_in_a, tok = scatter_start("a2a_w_in_a_start", gw_in_a)
    gw_in_b = _mm_grad_cols("mm_dw_in_b", h0bt, dhin, ns_in, half, half, after=tok)
    s_in_b, tok = scatter_start("a2a_w_in_b_start", gw_in_b)
    summed = _sum_parts("sum_small", _exchange_wait("ag_small_wait", h_small, tok))
    (s_eg, s_eb, s_cw, s_cb, s_cng, s_cnb, s_lb, s_hg, s_l1g, s_l1b, s_fw, s_fb, s_l2g, s_l2b,
     s_loss) = _unpack(summed, small_shapes)
    s_cw = lax.dynamic_slice_in_dim(s_cw, me * cs, cs, axis=1)[None]
    s_fw = lax.dynamic_slice_in_dim(s_fw, me * fs, fs, axis=1)[None]
    g_small = [s_eg, s_eb, s_cw, s_cb, s_cng, s_cnb, s_lb, s_hg, s_l1g, s_l1b, s_fw, s_fb, s_l2g, s_l2b]
    w_small = [emb_ln_g, emb_ln_b, conv_w, conv_b, conv_norm_g, conv_norm_b, lb_logits, hgrn_norm_g,
               ln1_g, ln1_b, ffn_conv_w, ffn_conv_b, ln2_g, ln2_b]
    m_small = [m_emb_ln_g, m_emb_ln_b, m_conv_w, m_conv_b, m_conv_norm_g, m_conv_norm_b, m_lb_logits,
               m_hgrn_norm_g, m_ln1_g, m_ln1_b, m_ffn_conv_w, m_ffn_conv_b, m_ln2_g, m_ln2_b]
    v_small = [v_emb_ln_g, v_emb_ln_b, v_conv_w, v_conv_b, v_conv_norm_g, v_conv_norm_b, v_lb_logits,
               v_hgrn_norm_g, v_ln1_g, v_ln1_b, v_ffn_conv_w, v_ffn_conv_b, v_ln2_g, v_ln2_b]
    rows_own = 236
    shapes_own = [w.shape for w in w_small]
    upd = _adamw_small("adamw_small", _pack(w_small, rows_own), _pack(g_small, rows_own),
                       _pack(m_small, rows_own), _pack(v_small, rows_own))
    d_small, nm_small, nv_small = (_unpack(u, shapes_own) for u in upd)
    g_small = [g.reshape(s) for g, s in zip(g_small, shapes_own)]

    def big(name, handle, after, w, m, v, tr):
        recv = _exchange_wait("a2a_" + name + "_wait", handle, after)
        return [o[None] for o in _adamw_sum("adamw_" + name, recv, w[0], m[0], v[0], tr)]

    u_down = big("w_down", s_down, upd[0], w_ffn_down, m_w_ffn_down, v_w_ffn_down, 64)
    u_up = big("w_up", s_up, u_down[1], w_ffn_up, m_w_ffn_up, v_w_ffn_up, 64)
    u_out = big("w_out", s_out, u_up[1], w_out, m_w_out, v_w_out, 64)
    recv_a = _exchange_wait("a2a_w_in_a_wait", s_in_a, u_out[1])
    part = _adamw_sum("adamw_w_in_a", recv_a, w_in[0], m_w_in[0], v_w_in[0], 128)
    recv_b = _exchange_wait("a2a_w_in_b_wait", s_in_b, part[1])
    u_in = [o[None] for o in _adamw_sum("adamw_w_in_b", recv_b, w_in[0], m_w_in[0], v_w_in[0], 128,
                                        row0=half, partial=part)]

    def ordered(small, i_in, i_out, i_up, i_down):
        (eg, eb, cw, cb, cng, cnb, lb, hg, l1g, l1b, fw, fb, l2g, l2b) = small
        return [eg, eb, i_in, cw, cb, cng, cnb, lb, hg, i_out, l1g, l1b, i_up, fw, fb, i_down, l2g, l2b]

    outs = [s_loss[0], grad_x[None]]
    for k, small in enumerate([g_small, d_small, nm_small, nv_small]):
        outs += ordered(small, u_in[k], u_out[k], u_up[k], u_down[k])
    return tuple(outs)
```

```python
import functools

import jax
import jax.numpy as jnp
from jax import lax
from jax.experimental import pallas as pl
from jax.experimental.pallas import tpu as pltpu

F32 = jnp.float32
BF16 = jnp.bfloat16

N_DEV = 8
D_MODEL = 2048
CONV_WIDTH = 1024
CONV_KERNEL = 31
HGRN_WIDTH = 1024
GROUP = 128
N_GROUPS = 8
IN_PROJ = 2 * CONV_WIDTH + 4 * HGRN_WIDTH
D_FF = 5632
FFN_KERNEL = 3
CHUNK = 64
SUB = 8
LN_EPS = 1e-5
RMS_EPS = 1e-6
ALPHA = 2.0 ** 0.25
ADAM_LR, ADAM_B1, ADAM_B2, ADAM_EPS, ADAM_WD, ADAM_STEP = 0.001, 0.9, 0.999, 1e-08, 0.01, 10

VMEM_LIMIT = 56 * 1024 * 1024
MESH = pl.DeviceIdType.MESH


def _cparams(sem=None):
    return pltpu.CompilerParams(dimension_semantics=sem, vmem_limit_bytes=VMEM_LIMIT)


def _sigmoid(x):
    return 1.0 / (1.0 + jnp.exp(-x))


def _matmul(name, a, b, out_shape, out_dtype, grid, a_spec, b_spec, o_spec, nt, after=None):
    nk = grid[2]
    dims = (((1,), (1,)), ((), ())) if nt else (((1,), (0,)), ((), ()))
    extra = [] if after is None else [after]

    def body(a_ref, b_ref, *rest):
        o_ref, *scratch = rest[len(extra):]
        av = a_ref[0] if len(a_ref.shape) == 3 else a_ref[...]
        bv = b_ref[0] if len(b_ref.shape) == 3 else b_ref[...]
        part = lax.dot_general(av, bv, dims, preferred_element_type=F32)

        def write(res):
            if len(o_ref.shape) == 3:
                o_ref[0] = res.astype(out_dtype)
            else:
                o_ref[...] = res.astype(out_dtype)

        if nk == 1:
            write(part)
            return
        acc_ref, = scratch
        k = pl.program_id(2)

        @pl.when(k == 0)
        def _():
            acc_ref[...] = part

        @pl.when(jnp.logical_and(k > 0, k < nk - 1))
        def _():
            acc_ref[...] += part

        @pl.when(k == nk - 1)
        def _():
            write(acc_ref[...] + part)

    acc_shape = o_spec.block_shape[-2:]
    assert all(g >= 1 for g in grid), (name, grid)
    return pl.pallas_call(
        body, name=name, grid=grid, in_specs=[a_spec, b_spec] + [pl.BlockSpec(memory_space=pl.ANY)] * len(extra),
        out_specs=o_spec, out_shape=jax.ShapeDtypeStruct(out_shape, out_dtype),
        scratch_shapes=[pltpu.VMEM(acc_shape, F32)] if nk > 1 else [],
        compiler_params=_cparams(("parallel", "parallel", "arbitrary")),
    )(a, b, *extra)


def _mm_nn(name, a, w, out_dtype, tm, tn, tk, after=None):
    m, k = a.shape
    tm, tk = min(tm, m), min(tk, k)
    n = w.shape[1]
    return _matmul(
        name, a, w, (m, n), out_dtype, (m // tm, n // tn, k // tk),
        pl.BlockSpec((tm, tk), lambda i, j, kk: (i, kk)),
        pl.BlockSpec((tk, tn), lambda i, j, kk: (kk, j)),
        pl.BlockSpec((tm, tn), lambda i, j, kk: (i, j)), nt=False, after=after)


def _mm_nt(name, a, w, out_dtype, tm, tn, tk, after=None):
    m, k = a.shape
    tm = min(tm, m)
    n = w.shape[0]
    return _matmul(
        name, a, w, (m, n), out_dtype, (m // tm, n // tn, k // tk),
        pl.BlockSpec((tm, tk), lambda i, j, kk: (i, kk)),
        pl.BlockSpec((tn, tk), lambda i, j, kk: (j, kk)),
        pl.BlockSpec((tm, tn), lambda i, j, kk: (i, j)), nt=True, after=after)


def _mm_grad_cols(name, at, b, ns, row0, rows, after, tm=1024, tk=4096):
    t = at.shape[1]
    tk = min(tk, t)
    off = row0 // tm
    return _matmul(
        name, at, b, (N_DEV, rows, ns), BF16, (rows // tm, N_DEV, t // tk),
        pl.BlockSpec((tm, tk), lambda i, j, kk: (i + off, kk)),
        pl.BlockSpec((tk, ns), lambda i, j, kk: (kk, j)),
        pl.BlockSpec((1, tm, ns), lambda i, j, kk: (j, i, 0)), nt=False, after=after)


LN_ROWS = 256


def _ln_stats(r):
    mu = jnp.mean(r, axis=-1, keepdims=True)
    xc = r - mu
    var = jnp.mean(xc * xc, axis=-1, keepdims=True)
    rstd = lax.rsqrt(var + LN_EPS)
    return xc * rstd, rstd


def _row_spec(d):
    return pl.BlockSpec((LN_ROWS, d), lambda i: (i, 0))


def _vec_spec(d):
    return pl.BlockSpec((1, d), lambda i: (0, 0))


def _ln_fwd(name, a, m, g, b, alpha):
    t, d = a.shape
    has_m = m is not None

    def body(*refs):
        if has_m:
            a_ref, m_ref, g_ref, b_ref, r_ref, y_ref, yb_ref, yt_ref = refs
            r = alpha * a_ref[...] + m_ref[...]
            r_ref[...] = r
        else:
            a_ref, g_ref, b_ref, y_ref, yb_ref, yt_ref = refs
            r = a_ref[...]
        xhat, _ = _ln_stats(r)
        y = xhat * g_ref[...] + b_ref[...]
        y_ref[...] = y
        yb_ref[...] = y.astype(BF16)
        yt_ref[...] = y.T.astype(BF16)

    ins = [a] + ([m] if has_m else []) + [g, b]
    in_specs = [_row_spec(d)] * (2 if has_m else 1) + [_vec_spec(d)] * 2
    outs = ([jax.ShapeDtypeStruct((t, d), F32)] if has_m else []) + [
        jax.ShapeDtypeStruct((t, d), F32), jax.ShapeDtypeStruct((t, d), BF16), jax.ShapeDtypeStruct((d, t), BF16)]
    res = pl.pallas_call(
        body, name=name, grid=(t // LN_ROWS,), in_specs=in_specs,
        out_specs=[_row_spec(d)] * (len(outs) - 1) + [pl.BlockSpec((d, LN_ROWS), lambda i: (0, i))], out_shape=outs,
        compiler_params=_cparams(("parallel",)),
    )(*ins)
    return res if has_m else (None, *res)


def _ln_bwd_math(r, dy, g):
    xhat, rstd = _ln_stats(r)
    dxhat = dy * g
    m1 = jnp.mean(dxhat, axis=-1, keepdims=True)
    m2 = jnp.mean(dxhat * xhat, axis=-1, keepdims=True)
    dr = rstd * (dxhat - m1 - xhat * m2)
    return dr, jnp.sum(dy * xhat, axis=0, keepdims=True), jnp.sum(dy, axis=0, keepdims=True)


def _ln2_loss_bwd(name, h1, ffn, g, b, tgt):
    t, d = h1.shape

    def body(h1_ref, f_ref, g_ref, b_ref, t_ref, dr_ref, drb_ref, dg_ref, db_ref, loss_ref):
        @pl.when(pl.program_id(0) == 0)
        def _():
            dg_ref[...] = jnp.zeros_like(dg_ref)
            db_ref[...] = jnp.zeros_like(db_ref)
            loss_ref[...] = jnp.zeros_like(loss_ref)

        r = ALPHA * h1_ref[...] + f_ref[...]
        xhat, _ = _ln_stats(r)
        e = xhat * g_ref[...] + b_ref[...] - t_ref[...]
        loss_ref[...] += 0.5 / d * jnp.sum(e * e)
        dr, dg, db = _ln_bwd_math(r, e * (1.0 / d), g_ref[...])
        dr_ref[...] = dr
        drb_ref[...] = dr.astype(BF16)
        dg_ref[...] += dg
        db_ref[...] += db

    return pl.pallas_call(
        body, name=name, grid=(t // LN_ROWS,),
        in_specs=[_row_spec(d), _row_spec(d), _vec_spec(d), _vec_spec(d), _row_spec(d)],
        out_specs=[_row_spec(d), _row_spec(d), _vec_spec(d), _vec_spec(d), _vec_spec(128)],
        out_shape=[jax.ShapeDtypeStruct((t, d), F32), jax.ShapeDtypeStruct((t, d), BF16),
                   jax.ShapeDtypeStruct((1, d), F32), jax.ShapeDtypeStruct((1, d), F32),
                   jax.ShapeDtypeStruct((1, 128), F32)],
        compiler_params=_cparams(("arbitrary",)),
    )(h1, ffn, g, b, tgt)


def _ln_bwd(name, r, dya, dyb, g, alpha, want_bf16):
    t, d = r.shape

    def body(r_ref, dya_ref, dyb_ref, g_ref, *outs):
        dr_ref = outs[0]
        dg_ref, db_ref = outs[-2:]

        @pl.when(pl.program_id(0) == 0)
        def _():
            dg_ref[...] = jnp.zeros_like(dg_ref)
            db_ref[...] = jnp.zeros_like(db_ref)

        dy = alpha * dya_ref[...] + dyb_ref[...]
        dr, dg, db = _ln_bwd_math(r_ref[...], dy, g_ref[...])
        dr_ref[...] = dr
        if want_bf16:
            outs[1][...] = dr.astype(BF16)
        dg_ref[...] += dg
        db_ref[...] += db

    big = [jax.ShapeDtypeStruct((t, d), F32)] + ([jax.ShapeDtypeStruct((t, d), BF16)] if want_bf16 else [])
    return pl.pallas_call(
        body, name=name, grid=(t // LN_ROWS,),
        in_specs=[_row_spec(d)] * 3 + [_vec_spec(d)],
        out_specs=[_row_spec(d)] * len(big) + [_vec_spec(d)] * 2,
        out_shape=big + [jax.ShapeDtypeStruct((1, d), F32)] * 2,
        compiler_params=_cparams(("arbitrary",)),
    )(r, dya, dyb, g)


CONV_ROWS = 64
CONV_UNROLL = 4
FFN_UNROLL = 2


def _unrolled(n, unroll, fn, init):
    def body(i, carry):
        for u in range(unroll):
            carry = fn(i * unroll + u, carry)
        return carry

    return lax.fori_loop(0, n // unroll, body, init)


def _for_shifted(ref, r0, tm, shifts, fn):
    for s in shifts:
        fn(s, ref[pl.ds(r0 + s, tm), :])


def _col_spec(t, cb, off=0):
    return pl.BlockSpec((t, cb), lambda j: (0, j + off))


def _ffn_act_fwd(name, hf, w, b, cb=128):
    t = hf.shape[0]
    f = hf.shape[1] // 2
    nb = f // cb
    tm = 128

    def body(g_ref, v_ref, w_ref, b_ref, act_ref, actt_ref, pad_ref):
        pad_ref[pl.ds(0, 8), :] = jnp.zeros((8, cb), F32)
        pad_ref[pl.ds(8, t), :] = g_ref[...].astype(F32)
        wv = [w_ref[pl.ds(k, 1), :] for k in range(FFN_KERNEL)]
        bias = b_ref[...]

        def tile(i, carry):
            r0 = pl.multiple_of(i * tm, tm)
            acc = [jnp.broadcast_to(bias, (tm, cb))]

            def tap(s, rows):
                acc[0] = acc[0] + wv[s - 6] * rows

            _for_shifted(pad_ref, r0, tm, (6, 7, 8), tap)
            gc = acc[0]
            act = gc * _sigmoid(gc) * v_ref[pl.ds(r0, tm), :].astype(F32)
            act_ref[pl.ds(r0, tm), :] = act.astype(BF16)
            actt_ref[:, pl.ds(r0, tm)] = act.T.astype(BF16)
            return carry

        lax.fori_loop(0, t // tm, tile, 0)

    return pl.pallas_call(
        body, name=name, grid=(nb,),
        in_specs=[_col_spec(t, cb), _col_spec(t, cb, nb),
                  pl.BlockSpec((FFN_KERNEL, cb), lambda j: (0, j)), pl.BlockSpec((1, cb), lambda j: (0, j))],
        out_specs=[_col_spec(t, cb), pl.BlockSpec((cb, t), lambda j: (j, 0))],
        out_shape=[jax.ShapeDtypeStruct((t, f), BF16), jax.ShapeDtypeStruct((f, t), BF16)],
        scratch_shapes=[pltpu.VMEM((t + 8, cb), F32)],
        compiler_params=_cparams(("parallel",)),
    )(hf, hf, w, b)


def _ffn_act_bwd(name, dact, hf, w, b, cb=128):
    t = hf.shape[0]
    f = hf.shape[1] // 2
    nb = f // cb
    tm = CONV_ROWS

    def body(da_ref, g_ref, v_ref, w_ref, b_ref, dhf_ref, dw_ref, db_ref, pad_ref, dgc_ref):
        pad_ref[pl.ds(0, 8), :] = jnp.zeros((8, cb), F32)
        pad_ref[pl.ds(8, t), :] = g_ref[...].astype(F32)
        dgc_ref[pl.ds(t, 8), :] = jnp.zeros((8, cb), F32)
        wv = [w_ref[pl.ds(k, 1), :] for k in range(FFN_KERNEL)]
        bias = b_ref[...]

        def tile_a(i, carry):
            r0 = pl.multiple_of(i * tm, tm)
            taps = {}
            _for_shifted(pad_ref, r0, tm, (6, 7, 8), lambda s, rows: taps.__setitem__(s, rows))
            gc = bias + wv[0] * taps[6] + wv[1] * taps[7] + wv[2] * taps[8]
            sg = _sigmoid(gc)
            da = da_ref[pl.ds(r0, tm), :].astype(F32)
            dhf_ref[1, pl.ds(r0, tm), :] = (da * gc * sg).astype(BF16)
            dgc = da * v_ref[pl.ds(r0, tm), :].astype(F32) * sg * (1.0 + gc * (1.0 - sg))
            dgc_ref[pl.ds(r0, tm), :] = dgc
            sums = [jnp.sum(dgc * taps[6 + k], axis=0, keepdims=True) for k in range(3)]
            sums.append(jnp.sum(dgc, axis=0, keepdims=True))
            return tuple(c + s for c, s in zip(carry, sums))

        zero = jnp.zeros((1, cb), F32)
        dw0, dw1, dw2, dbias = _unrolled(t // tm, FFN_UNROLL, tile_a, (zero, zero, zero, zero))
        row = lax.broadcasted_iota(jnp.int32, (8, cb), 0)
        dw_ref[...] = jnp.where(row == 0, dw0, jnp.where(row == 1, dw1, jnp.where(row == 2, dw2, 0.0)))
        db_ref[...] = dbias

        def tile_b(i, carry):
            r0 = pl.multiple_of(i * tm, tm)
            acc = [jnp.zeros((tm, cb), F32)]

            def tap(s, rows):
                acc[0] = acc[0] + wv[2 - s] * rows

            _for_shifted(dgc_ref, r0, tm, (0, 1, 2), tap)
            dhf_ref[0, pl.ds(r0, tm), :] = acc[0].astype(BF16)
            return carry

        lax.fori_loop(0, t // tm, tile_b, 0)

    return pl.pallas_call(
        body, name=name, grid=(nb,),
        in_specs=[_col_spec(t, cb), _col_spec(t, cb), _col_spec(t, cb, nb),
                  pl.BlockSpec((FFN_KERNEL, cb), lambda j: (0, j)), pl.BlockSpec((1, cb), lambda j: (0, j))],
        out_specs=[pl.BlockSpec((2, t, cb), lambda j: (0, 0, j)),
                   pl.BlockSpec((8, cb), lambda j: (0, j)), pl.BlockSpec((1, cb), lambda j: (0, j))],
        out_shape=[jax.ShapeDtypeStruct((2, t, f), BF16), jax.ShapeDtypeStruct((8, f), F32),
                   jax.ShapeDtypeStruct((1, f), F32)],
        scratch_shapes=[pltpu.VMEM((t + 8, cb), F32), pltpu.VMEM((t + 8, cb), F32)],
        compiler_params=_cparams(("parallel",)),
    )(dact, hf, hf, w, b)


def _silu_grad(z, sg):
    return sg * (1.0 + z * (1.0 - sg))


def _conv_fwd(name, hin, w, b, ng, nb_):
    t = hin.shape[0]
    c = GROUP
    tm = CONV_ROWS
    pad = 32
    shifts = tuple(2 + k for k in range(CONV_KERNEL))

    def body(a_ref, gt_ref, w_ref, b_ref, ng_ref, nb_ref, u1_ref, u3_ref, pad_ref):
        pad_ref[pl.ds(0, pad), :] = jnp.zeros((pad, c), F32)
        pad_ref[pl.ds(pad, t), :] = a_ref[...] * _sigmoid(gt_ref[...])
        bias, gam, bet = b_ref[...], ng_ref[...], nb_ref[...]

        def tile(i, carry):
            r0 = pl.multiple_of(i * tm, tm)
            acc = [jnp.broadcast_to(bias, (tm, c))]

            def tap(s, rows):
                acc[0] = acc[0] + w_ref[pl.ds(s - 2, 1), :] * rows

            _for_shifted(pad_ref, r0, tm, shifts, tap)
            u1 = acc[0]
            u1_ref[pl.ds(r0, tm), :] = u1
            xhat, _ = _ln_stats(u1)
            u2 = xhat * gam + bet
            u3_ref[pl.ds(r0, tm), :] = (u2 * _sigmoid(u2)).astype(BF16)
            return carry

        _unrolled(t // tm, CONV_UNROLL, tile, 0)

    vec = pl.BlockSpec((1, c), lambda j: (0, j))
    return pl.pallas_call(
        body, name=name, grid=(N_GROUPS,),
        in_specs=[_col_spec(t, c), _col_spec(t, c, N_GROUPS),
                  pl.BlockSpec((CONV_KERNEL, c), lambda j: (0, j)), vec, vec, vec],
        out_specs=[_col_spec(t, c), _col_spec(t, c)],
        out_shape=[jax.ShapeDtypeStruct((t, CONV_WIDTH), F32), jax.ShapeDtypeStruct((t, CONV_WIDTH), BF16)],
        scratch_shapes=[pltpu.VMEM((t + pad, c), F32)],
        compiler_params=_cparams(("parallel",)),
    )(hin, hin, w, b, ng, nb_)


def _conv_bwd(name, dcat, u1, hin, w, ng, nb_):
    t = hin.shape[0]
    c = GROUP
    tm = CONV_ROWS
    pad = 32
    nk = CONV_KERNEL

    def body(du3_ref, u1_ref, a_ref, gt_ref, w_ref, ng_ref, nb_ref,
             da_ref, dgt_ref, dw_ref, db_ref, dng_ref, dnb_ref, u0_ref, du1_ref, dwp_ref):
        u0_ref[pl.ds(0, pad), :] = jnp.zeros((pad, c), F32)
        u0_ref[pl.ds(pad, t), :] = a_ref[...] * _sigmoid(gt_ref[...])
        du1_ref[pl.ds(t, pad), :] = jnp.zeros((pad, c), F32)
        dwp_ref[...] = jnp.zeros_like(dwp_ref)
        gam, bet = ng_ref[...], nb_ref[...]

        def tile_a(i, carry):
            r0 = pl.multiple_of(i * tm, tm)
            u1 = u1_ref[pl.ds(r0, tm), :]
            xhat, rstd = _ln_stats(u1)
            u2 = xhat * gam + bet
            sg = _sigmoid(u2)
            du2 = du3_ref[pl.ds(r0, tm), :] * _silu_grad(u2, sg)
            dxhat = du2 * gam
            m1 = jnp.mean(dxhat, axis=-1, keepdims=True)
            m2 = jnp.mean(dxhat * xhat, axis=-1, keepdims=True)
            du1 = rstd * (dxhat - m1 - xhat * m2)
            du1_ref[pl.ds(r0, tm), :] = du1
            sums = (jnp.sum(du1, axis=0, keepdims=True), jnp.sum(du2 * xhat, axis=0, keepdims=True),
                    jnp.sum(du2, axis=0, keepdims=True))
            return tuple(x + s for x, s in zip(carry, sums))

        zero = jnp.zeros((1, c), F32)
        dbias, dgam, dbet = _unrolled(t // tm, CONV_UNROLL, tile_a, (zero, zero, zero))
        db_ref[...] = dbias
        dng_ref[...] = dgam
        dnb_ref[...] = dbet

        def tile_b(i, carry):
            r0 = pl.multiple_of(i * tm, tm)
            du1 = du1_ref[pl.ds(r0, tm), :]
            acc = [jnp.zeros((tm, c), F32)]

            def tap_dx(s, rows):
                acc[0] = acc[0] + w_ref[pl.ds(nk - 1 - s, 1), :] * rows

            _for_shifted(du1_ref, r0, tm, tuple(range(nk)), tap_dx)

            def tap_dw(s, rows):
                part = (du1 * rows).reshape(tm // 8, 8, c).sum(axis=0)
                dwp_ref[s - 2] = dwp_ref[s - 2] + part

            _for_shifted(u0_ref, r0, tm, tuple(2 + k for k in range(nk)), tap_dw)
            du0 = acc[0]
            a = a_ref[pl.ds(r0, tm), :]
            sg = _sigmoid(gt_ref[pl.ds(r0, tm), :])
            da_ref[pl.ds(r0, tm), :] = (du0 * sg).astype(BF16)
            dgt_ref[pl.ds(r0, tm), :] = (du0 * a * sg * (1.0 - sg)).astype(BF16)
            return carry

        lax.fori_loop(0, t // tm, tile_b, 0)
        dw_ref[...] = jnp.sum(dwp_ref[...], axis=1)

    vec = pl.BlockSpec((1, c), lambda j: (0, j))
    vshape = jax.ShapeDtypeStruct((1, CONV_WIDTH), F32)
    return pl.pallas_call(
        body, name=name, grid=(N_GROUPS,),
        in_specs=[_col_spec(t, c), _col_spec(t, c), _col_spec(t, c), _col_spec(t, c, N_GROUPS),
                  pl.BlockSpec((nk, c), lambda j: (0, j)), vec, vec],
        out_specs=[_col_spec(t, c), _col_spec(t, c), pl.BlockSpec((32, c), lambda j: (0, j)), vec, vec, vec],
        out_shape=[jax.ShapeDtypeStruct((t, CONV_WIDTH), BF16), jax.ShapeDtypeStruct((t, CONV_WIDTH), BF16),
                   jax.ShapeDtypeStruct((32, CONV_WIDTH), F32), vshape, vshape, vshape],
        scratch_shapes=[pltpu.VMEM((t + pad, c), F32), pltpu.VMEM((t + pad, c), F32),
                        pltpu.VMEM((32, 8, c), F32)],
        compiler_params=_cparams(("parallel",)),
    )(dcat, u1, hin, hin, w, ng, nb_)


LEVELS = (64, 32, 16)
HGRN_UNROLL = 2
HGRN_UNROLL_FWD = 4
NT_DIMS = (((1,), (1,)), ((), ()))
NN_DIMS = (((1,), (0,)), ((), ()))
TN_DIMS = (((0,), (0,)), ((), ()))


def _bdot(a, b, dims):
    return lax.dot_general(a.astype(BF16), b.astype(BF16), dims, preferred_element_type=F32)


def _hdot(a, b):
    return jnp.dot(a, b, precision=lax.Precision.HIGHEST, preferred_element_type=F32)


def _chunk_consts():
    rid = lax.broadcasted_iota(jnp.int32, (CHUNK, GROUP), 0)
    ti = lax.broadcasted_iota(jnp.int32, (CHUNK, CHUNK), 0)
    si = lax.broadcasted_iota(jnp.int32, (CHUNK, CHUNK), 1)
    tri = (si <= ti).astype(F32)
    second = [(rid & (b // 2)) != 0 for b in LEVELS]
    same = [None] + [(ti // b) == (si // b) for b in LEVELS[1:]]
    sub = lax.broadcasted_iota(jnp.int32, (SUB, GROUP), 0)
    return rid, tri, second, same, sub


def _level_refs(cum_ref, rid, base):
    row = lambda i: cum_ref[pl.ds(base + i, 1), :]
    l1 = jnp.broadcast_to(row(31), (CHUNK, GROUP))
    l2 = jnp.where(rid < 32, row(15), row(47))
    l3 = jnp.where(rid < 16, row(7), jnp.where(rid < 32, row(23), jnp.where(rid < 48, row(39), row(55))))
    return l1, l2, l3


def _level_factors(cum, brefs, second):
    out = []
    for bref, sec in zip(brefs, second):
        eq = jnp.where(sec, jnp.exp(jnp.minimum(cum - bref, 0.0)), 0.0)
        ek = jnp.where(sec, 0.0, jnp.exp(jnp.minimum(bref - cum, 0.0)))
        out.append((eq, ek))
    return out


def _gates(q, f, lb):
    sq = _sigmoid(q)
    sf = _sigmoid(f)
    fg = lb + (1.0 - lb) * sf
    return q * sq, sq, sf, fg


def _hgrn_specs(t, nc):
    c = GROUP
    col = lambda off: pl.BlockSpec((t, c), lambda h: (0, h + off))
    hin_specs = [col(16), col(24), col(32), col(40)]
    vec = pl.BlockSpec((1, c), lambda h: (0, h))
    lbs = pl.BlockSpec((2, c), lambda h: (0, h))
    st = pl.BlockSpec((1, nc, c, c), lambda h: (h, 0, 0, 0))
    return col, hin_specs, vec, lbs, st


def _hgrn_fwd(name, hin, lb_logits, hg):
    t = hin.shape[0]
    nc = t // CHUNK
    c = GROUP
    col, hin_specs, vec, lbs, st = _hgrn_specs(t, nc)

    def body(q_ref, f_ref, v_ref, og_ref, lb_ref, hg_ref, o_ref, ob_ref, st_ref,
             s_ref, cum_ref, kk_ref, vc_ref):
        rid, tri, second, same, sub = _chunk_consts()
        lb = _sigmoid(lb_ref[pl.ds(0, 1), :] - lb_ref[pl.ds(1, 1), :])
        gain = hg_ref[...]
        s_ref[...] = jnp.zeros_like(s_ref)

        def chunk(ci, u):
            base = u * CHUNK
            r0 = pl.multiple_of(ci * CHUNK, CHUNK)
            rows = pl.ds(r0, CHUNK)
            qh, _, _, fg = _gates(q_ref[rows, :], f_ref[rows, :], lb)
            v = v_ref[rows, :]
            kk = 1.0 - fg
            cum = _hdot(tri, jnp.log(fg))
            cum_ref[pl.ds(base, CHUNK), :] = cum
            kk_ref[pl.ds(base, CHUNK), :] = kk
            vc_ref[pl.ds(base, CHUNK), :] = v
            sprev = s_ref[...]
            st_ref[0, ci] = sprev
            blast = cum_ref[pl.ds(base + CHUNK - 1, 1), :]
            o = _bdot(qh * jnp.exp(cum), sprev, NT_DIMS)
            s_ref[...] = sprev * jnp.exp(blast) + _bdot(v, kk * jnp.exp(blast - cum), TN_DIMS)
            a = None
            for (eq, ek), msk in zip(_level_factors(cum, _level_refs(cum_ref, rid, base), second), same):
                al = _bdot(qh * eq, kk * ek, NT_DIMS)
                al = al if msk is None else jnp.where(msk, al, 0.0)
                a = al if a is None else a + al
            o = o + _bdot(a, v, NN_DIMS)
            diag = []
            for sb in range(CHUNK // SUB):
                lo = sb * SUB
                qb = qh[lo:lo + SUB]
                cb = cum[lo:lo + SUB]
                od = jnp.zeros((SUB, c), F32)
                for s in range(SUB):
                    e = jnp.where(sub >= s, jnp.exp(jnp.minimum(cb - cum_ref[pl.ds(base + lo + s, 1), :], 0.0)), 0.0)
                    acol = jnp.sum(qb * e * kk_ref[pl.ds(base + lo + s, 1), :], axis=-1, keepdims=True)
                    od = od + acol * vc_ref[pl.ds(base + lo + s, 1), :]
                diag.append(od)
            o = o + jnp.concatenate(diag, axis=0)
            o_ref[rows, :] = o
            y = o * lax.rsqrt(jnp.mean(o * o, axis=-1, keepdims=True) + RMS_EPS) * gain
            og = og_ref[rows, :]
            ob_ref[rows, :] = (y * og * _sigmoid(og)).astype(BF16)

        def chunks(i, carry):
            for u in range(HGRN_UNROLL_FWD):
                chunk(i * HGRN_UNROLL_FWD + u, u)
            return carry

        lax.fori_loop(0, nc // HGRN_UNROLL_FWD, chunks, 0)

    return pl.pallas_call(
        body, name=name, grid=(N_GROUPS,),
        in_specs=hin_specs + [lbs, vec],
        out_specs=[col(0), col(0), st],
        out_shape=[jax.ShapeDtypeStruct((t, HGRN_WIDTH), F32), jax.ShapeDtypeStruct((t, HGRN_WIDTH), BF16),
                   jax.ShapeDtypeStruct((N_GROUPS, nc, c, c), F32)],
        scratch_shapes=[pltpu.VMEM((c, c), F32)] + [pltpu.VMEM((HGRN_UNROLL_FWD * CHUNK, c), F32)] * 3,
        compiler_params=_cparams(("parallel",)),
    )(hin, hin, hin, hin, lb_logits, hg)


def _hgrn_bwd(name, dcat, hin, o_raw, states, lb_logits, hg):
    t = hin.shape[0]
    nc = t // CHUNK
    c = GROUP
    col, hin_specs, vec, lbs, st = _hgrn_specs(t, nc)

    def body(do_ref, q_ref, f_ref, v_ref, og_ref, o_ref, st_ref, lb_ref, hg_ref,
             dq_ref, df_ref, dv_ref, dog_ref, dhg_ref, dlb_ref,
             ds_ref, cum_ref, kk_ref, vc_ref):
        rid, tri, second, same, sub = _chunk_consts()
        trit = tri.T
        lb = _sigmoid(lb_ref[pl.ds(0, 1), :] - lb_ref[pl.ds(1, 1), :])
        gain = hg_ref[...]
        ds_ref[...] = jnp.zeros_like(ds_ref)

        def chunk(i, carry, u):
            base = u * CHUNK
            dhg, dlb = carry
            ci = nc - 1 - i
            r0 = pl.multiple_of(ci * CHUNK, CHUNK)
            rows = pl.ds(r0, CHUNK)
            q = q_ref[rows, :]
            qh, sq, sf, fg = _gates(q, f_ref[rows, :], lb)
            v = v_ref[rows, :]
            kk = 1.0 - fg
            cum = _hdot(tri, jnp.log(fg))
            cum_ref[pl.ds(base, CHUNK), :] = cum
            kk_ref[pl.ds(base, CHUNK), :] = kk
            vc_ref[pl.ds(base, CHUNK), :] = v
            o = o_ref[rows, :]
            og = og_ref[rows, :]
            sg = _sigmoid(og)
            rinv = lax.rsqrt(jnp.mean(o * o, axis=-1, keepdims=True) + RMS_EPS)
            yn = o * rinv
            dof = do_ref[rows, :]
            dog_ref[rows, :] = (dof * yn * gain * _silu_grad(og, sg)).astype(BF16)
            dz = dof * og * sg
            dhg = dhg + jnp.sum(dz * yn, axis=0, keepdims=True)
            dy = dz * gain
            do = rinv * (dy - yn * jnp.mean(dy * yn, axis=-1, keepdims=True))
            sprev = st_ref[0, ci]
            dsn = ds_ref[...]
            blast = cum_ref[pl.ds(base + CHUNK - 1, 1), :]
            eq0 = jnp.exp(cum)
            ek0 = jnp.exp(blast - cum)
            dqh = _bdot(do, sprev, NN_DIMS) * eq0
            dkk = _bdot(v, dsn, NN_DIMS) * ek0
            dlast = (jnp.sum(kk * dkk, axis=0, keepdims=True)
                     + jnp.exp(blast) * jnp.sum(dsn * sprev, axis=0, keepdims=True))
            dv = _bdot(kk * ek0, dsn, NT_DIMS)
            ds_ref[...] = dsn * jnp.exp(blast) + _bdot(do, qh * eq0, TN_DIMS)
            dg = qh * dqh - kk * dkk
            da = _bdot(do, v, NT_DIMS)
            a = None
            for (eq, ek), msk in zip(_level_factors(cum, _level_refs(cum_ref, rid, base), second), same):
                ql, kl = (qh * eq).astype(BF16), (kk * ek).astype(BF16)
                al = _bdot(ql, kl, NT_DIMS)
                dal = da
                if msk is not None:
                    al = jnp.where(msk, al, 0.0)
                    dal = jnp.where(msk, da, 0.0)
                a = al if a is None else a + al
                dql = _bdot(dal, kl, NN_DIMS)
                dkl = _bdot(dal, ql, TN_DIMS)
                dqh = dqh + dql * eq
                dkk = dkk + dkl * ek
                dg = dg + (ql.astype(F32) * dql - kl.astype(F32) * dkl)
            dv = dv + _bdot(a, do, TN_DIMS)
            dq_d, dk_d, dv_d = [], [], []
            for sb in range(CHUNK // SUB):
                lo = sb * SUB
                qb = qh[lo:lo + SUB]
                cb = cum[lo:lo + SUB]
                dob = do[lo:lo + SUB]
                dqb = jnp.zeros((SUB, c), F32)
                dkb = jnp.zeros((SUB, c), F32)
                dvb = jnp.zeros((SUB, c), F32)
                for s in range(SUB):
                    e = jnp.where(sub >= s, jnp.exp(jnp.minimum(cb - cum_ref[pl.ds(base + lo + s, 1), :], 0.0)), 0.0)
                    ks = kk_ref[pl.ds(base + lo + s, 1), :]
                    qe = qb * e
                    dacol = jnp.sum(dob * vc_ref[pl.ds(base + lo + s, 1), :], axis=-1, keepdims=True)
                    acol = jnp.sum(qe * ks, axis=-1, keepdims=True)
                    dqb = dqb + dacol * (ks * e)
                    dkb = jnp.where(sub == s, jnp.sum(dacol * qe, axis=0, keepdims=True), dkb)
                    dvb = jnp.where(sub == s, jnp.sum(acol * dob, axis=0, keepdims=True), dvb)
                dq_d.append(dqb)
                dk_d.append(dkb)
                dv_d.append(dvb)
            dq_d = jnp.concatenate(dq_d, axis=0)
            dk_d = jnp.concatenate(dk_d, axis=0)
            dqh = dqh + dq_d
            dkk = dkk + dk_d
            dg = dg + (qh * dq_d - kk * dk_d)
            dv = dv + jnp.concatenate(dv_d, axis=0)
            dlf = _hdot(trit, dg) + dlast
            dfg = dlf / fg - dkk
            df_ref[rows, :] = (dfg * (1.0 - lb) * sf * (1.0 - sf)).astype(BF16)
            dlb = dlb + jnp.sum(dfg * (1.0 - sf), axis=0, keepdims=True)
            dq_ref[rows, :] = (dqh * _silu_grad(q, sq)).astype(BF16)
            dv_ref[rows, :] = dv.astype(BF16)
            return dhg, dlb

        def chunks(i, carry):
            for u in range(HGRN_UNROLL):
                carry = chunk(i * HGRN_UNROLL + u, carry, u)
            return carry

        zero = jnp.zeros((1, c), F32)
        dhg, dlb = lax.fori_loop(0, nc // HGRN_UNROLL, chunks, (zero, zero))
        dhg_ref[...] = dhg
        dl0 = dlb * lb * (1.0 - lb)
        dlb_ref[...] = jnp.where(lax.broadcasted_iota(jnp.int32, (2, c), 0) == 0, dl0, -dl0)

    big = jax.ShapeDtypeStruct((t, HGRN_WIDTH), BF16)
    return pl.pallas_call(
        body, name=name, grid=(N_GROUPS,),
        in_specs=[col(8)] + hin_specs + [col(0), st, lbs, vec],
        out_specs=[col(0)] * 4 + [vec, lbs],
        out_shape=[big] * 4 + [jax.ShapeDtypeStruct((1, HGRN_WIDTH), F32), jax.ShapeDtypeStruct((2, HGRN_WIDTH), F32)],
        scratch_shapes=[pltpu.VMEM((c, c), F32)] + [pltpu.VMEM((HGRN_UNROLL * CHUNK, c), F32)] * 3,
        compiler_params=_cparams(("parallel",)),
    )(dcat, hin, hin, hin, hin, o_raw, states, lb_logits, hg)


ANY = pl.BlockSpec(memory_space=pl.ANY)


def _my_place():
    return lax.axis_index("x"), lax.axis_index("y"), lax.axis_index("c")


HBM = pl.BlockSpec(memory_space=pltpu.HBM)
SEM = pl.BlockSpec(memory_space=pltpu.SEMAPHORE)
EFFECT = pltpu.SideEffectType.DATAFLOW_SIDE_EFFECTING


def _peer(k):
    x, y, c = _my_place()
    px = 1 - x if k & 4 else x
    py = 1 - y if k & 2 else y
    pc = 1 - c if k & 1 else c
    return (px, py, pc), 4 * px + 2 * py + pc


def _exchange_copy(k, src_ref, land_ref, send_sems, recv_sems, scatter, landing):
    x, y, c = _my_place()
    me = 4 * x + 2 * y + c
    to, idx = _peer(k)
    return pltpu.make_async_remote_copy(
        src_ref=src_ref.at[idx] if scatter else src_ref,
        dst_ref=land_ref.at[idx] if landing else land_ref.at[me],
        send_sem=send_sems.at[k - 1], recv_sem=recv_sems.at[k - 1], device_id=to, device_id_type=MESH)


def _exchange_start(name, src, land, scatter):
    def body(src_ref, land_ref, send_sems, recv_sems, src_thru, land_thru, token):
        for k in range(1, N_DEV):
            _exchange_copy(k, src_ref, land_ref, send_sems, recv_sems, scatter, landing=False).start()
        token[...] = jnp.zeros_like(token)

    send_sems, recv_sems, src_thru, land_thru, token = pl.pallas_call(
        body, name=name,
        out_shape=(pltpu.SemaphoreType.DMA((N_DEV - 1,)), pltpu.SemaphoreType.DMA((N_DEV - 1,)),
                   pltpu.HBM(src.shape, src.dtype), pltpu.HBM(land.shape, land.dtype),
                   jax.ShapeDtypeStruct((8, 128), F32)),
        in_specs=(HBM, HBM), out_specs=(SEM, SEM, HBM, HBM, pl.BlockSpec(memory_space=pltpu.VMEM)),
        input_output_aliases={0: 2, 1: 3},
        compiler_params=pltpu.CompilerParams(has_side_effects=EFFECT),
    )(pltpu.with_memory_space_constraint(src, pltpu.HBM), pltpu.with_memory_space_constraint(land, pltpu.HBM))
    return (send_sems, recv_sems, src_thru, land_thru, scatter), token


def _exchange_wait(name, handle, after):
    send_sems, recv_sems, src_thru, land_thru, scatter = handle

    def body(src_ref, land_ref, send_sems, recv_sems, after_ref, src_dead, got_ref):
        for k in range(1, N_DEV):
            cp = _exchange_copy(k, src_ref, land_ref, send_sems, recv_sems, scatter, landing=True)
            cp.wait_send()
            cp.wait_recv()

    return pl.pallas_call(
        body, name=name,
        out_shape=(pltpu.HBM(src_thru.shape, src_thru.dtype), pltpu.HBM(land_thru.shape, land_thru.dtype)),
        in_specs=(HBM, HBM, SEM, SEM, ANY), out_specs=(HBM, HBM), input_output_aliases={0: 0, 1: 1},
        compiler_params=pltpu.CompilerParams(has_side_effects=EFFECT),
    )(src_thru, land_thru, send_sems, recv_sems, after)[1]


def _own_slot(own, me):
    land = lax.empty((N_DEV,) + own.shape, own.dtype)
    return lax.dynamic_update_slice_in_dim(land, own[None], me, axis=0)


def _adamw_math(w, g, m, v):
    m = ADAM_B1 * m + (1.0 - ADAM_B1) * g
    v = ADAM_B2 * v + (1.0 - ADAM_B2) * (g * g)
    m_hat = m / (1.0 - ADAM_B1 ** ADAM_STEP)
    v_hat = v / (1.0 - ADAM_B2 ** ADAM_STEP)
    delta = -ADAM_LR * (m_hat / (jnp.sqrt(v_hat) + ADAM_EPS) + ADAM_WD * w)
    return delta, m, v


def _adamw_sum(name, recv, w, m, v, tr, row0=0, partial=None):
    r, c = w.shape
    rr = recv.shape[1]
    off = row0 // tr

    def body(recv_ref, w_ref, m_ref, v_ref, *refs):
        g_ref, d_ref, mo_ref, vo_ref = refs[-4:]
        g = recv_ref[0].astype(F32)
        for j in range(1, N_DEV):
            g = g + recv_ref[j].astype(F32)
        g_ref[...] = g
        d_ref[...], mo_ref[...], vo_ref[...] = _adamw_math(w_ref[...], g, m_ref[...], v_ref[...])

    tile = pl.BlockSpec((tr, c), lambda i: (i + off, 0))
    out = jax.ShapeDtypeStruct((r, c), F32)
    prev = list(partial) if partial is not None else []
    return pl.pallas_call(
        body, name=name, grid=(rr // tr,),
        in_specs=[pl.BlockSpec((N_DEV, tr, c), lambda i: (0, i, 0)), tile, tile, tile] + [ANY] * len(prev),
        out_specs=[tile] * 4, out_shape=[out] * 4,
        input_output_aliases={4 + i: i for i in range(len(prev))},
        compiler_params=_cparams(("parallel",)),
    )(recv, w, m, v, *prev)


def _sum_parts(name, parts):
    _, r, c = parts.shape

    def body(p_ref, o_ref):
        acc = p_ref[0]
        for j in range(1, N_DEV):
            acc = acc + p_ref[j]
        o_ref[...] = acc

    return pl.pallas_call(body, name=name, out_shape=jax.ShapeDtypeStruct((r, c), F32),
                          compiler_params=_cparams())(parts)


def _adamw_small(name, w, g, m, v):
    def body(w_ref, g_ref, m_ref, v_ref, d_ref, mo_ref, vo_ref):
        d_ref[...], mo_ref[...], vo_ref[...] = _adamw_math(w_ref[...], g_ref[...], m_ref[...], v_ref[...])

    out = jax.ShapeDtypeStruct(w.shape, F32)
    return pl.pallas_call(body, name=name, out_shape=[out] * 3, compiler_params=_cparams())(w, g, m, v)


def _pack(pieces, rows):
    flat = jnp.concatenate([p.reshape(-1).astype(F32) for p in pieces])
    return jnp.pad(flat, (0, rows * 128 - flat.shape[0])).reshape(rows, 128)


def _unpack(packed, shapes):
    flat = packed.reshape(-1)
    out, off = [], 0
    for s in shapes:
        n = 1
        for d in s:
            n *= d
        out.append(flat[off:off + n].reshape(s))
        off += n
    return out


def kernel(x, emb_ln_g, emb_ln_b, w_in, conv_w, conv_b, conv_norm_g, conv_norm_b, lb_logits, hgrn_norm_g, w_out, ln1_g, ln1_b, w_ffn_up, ffn_conv_w, ffn_conv_b, w_ffn_down, ln2_g, ln2_b, loss_target, m_emb_ln_g, m_emb_ln_b, m_w_in, m_conv_w, m_conv_b, m_conv_norm_g, m_conv_norm_b, m_lb_logits, m_hgrn_norm_g, m_w_out, m_ln1_g, m_ln1_b, m_w_ffn_up, m_ffn_conv_w, m_ffn_conv_b, m_w_ffn_down, m_ln2_g, m_ln2_b, v_emb_ln_g, v_emb_ln_b, v_w_in, v_conv_w, v_conv_b, v_conv_norm_g, v_conv_norm_b, v_lb_logits, v_hgrn_norm_g, v_w_out, v_ln1_g, v_ln1_b, v_w_ffn_up, v_ffn_conv_w, v_ffn_conv_b, v_w_ffn_down, v_ln2_g, v_ln2_b):
    t = x.shape[1]
    me = 4 * lax.axis_index("x") + 2 * lax.axis_index("y") + lax.axis_index("c")
    x2, tgt = x[0], loss_target[0]
    ns_in, ns_up = w_in.shape[2], w_ffn_up.shape[2]
    rs_out, rs_down = w_out.shape[1], w_ffn_down.shape[1]
    cs, fs = conv_w.shape[2], ffn_conv_w.shape[2]

    def gather_start(name, w, prev):
        shard = (w[0] + prev).astype(BF16)
        return _exchange_start(name, shard, _own_slot(shard, me), scatter=False)

    h_in, tok = gather_start("ag_w_in_start", w_in, 0.0)
    taps = _pack([conv_w[0], ffn_conv_w[0]], 48) + tok[0, 0]
    h_taps, tok = _exchange_start("ag_taps_start", taps, _own_slot(taps, me), scatter=False)
    h_out, tok = gather_start("ag_w_out_start", w_out, tok[0, 0])
    h_up, tok = gather_start("ag_w_up_start", w_ffn_up, tok[0, 0])
    h_down, tok = gather_start("ag_w_down_start", w_ffn_down, tok[0, 0])

    row = lambda a: a.reshape(1, -1)

    _, h0, h0b, h0bt = _ln_fwd("ln_in", x2, None, row(emb_ln_g) + tok[0, 0], row(emb_ln_b), 1.0)
    win_g = _exchange_wait("ag_w_in_wait", h_in, h0b)
    win_n = win_g.transpose(1, 0, 2).reshape(D_MODEL, IN_PROJ)
    hin = _mm_nn("mm_in", h0b, win_n, F32, tm=1024, tn=ns_in, tk=D_MODEL)
    n_cw, n_fw = CONV_KERNEL * cs, FFN_KERNEL * fs
    taps_g = _exchange_wait("ag_taps_wait", h_taps, hin).reshape(N_DEV, -1)
    cw_full = taps_g[:, :n_cw].reshape(N_DEV, CONV_KERNEL, cs).transpose(1, 0, 2).reshape(CONV_KERNEL, CONV_WIDTH)
    fw_full = taps_g[:, n_cw:n_cw + n_fw].reshape(N_DEV, FFN_KERNEL, fs).transpose(1, 0, 2).reshape(FFN_KERNEL, D_FF)

    u1, u3b = _conv_fwd("conv_fwd", hin, cw_full, conv_b, conv_norm_g, conv_norm_b)
    o_raw, ob, states = _hgrn_fwd("hgrn_fwd", hin, lb_logits, hgrn_norm_g)
    catb = jnp.concatenate([u3b, ob], axis=1)
    wout_g = _exchange_wait("ag_w_out_wait", h_out, catb).reshape(D_MODEL, D_MODEL)
    mix = _mm_nn("mm_out", catb, wout_g, F32, tm=1024, tn=1024, tk=D_MODEL)
    r1, h1, h1b, h1bt = _ln_fwd("ln1", h0, mix, ln1_g, ln1_b, ALPHA)
    wup_g = _exchange_wait("ag_w_up_wait", h_up, h1b)
    wup_n = wup_g.transpose(1, 0, 2).reshape(D_MODEL, 2 * D_FF)
    hf = _mm_nn("mm_up", h1b, wup_n, BF16, tm=1024, tn=ns_up, tk=D_MODEL)
    actb, actbt = _ffn_act_fwd("ffn_act", hf, fw_full, ffn_conv_b)
    wdown_g = _exchange_wait("ag_w_down_wait", h_down, actb).reshape(D_FF, D_MODEL)
    ffn = _mm_nn("mm_down", actb, wdown_g, F32, tm=512, tn=1024, tk=D_FF)
    dr2, dr2b, g_ln2g, g_ln2b, loss = _ln2_loss_bwd("ln2_loss", h1, ffn, ln2_g, ln2_b, tgt)

    def scatter_start(name, parts):
        own = lax.dynamic_index_in_dim(parts, me, axis=0, keepdims=False)
        return _exchange_start(name, parts, _own_slot(own, me), scatter=True)

    dact = _mm_nt("mm_dact", dr2b, wdown_g, BF16, tm=1024, tn=1408, tk=D_MODEL)
    gw_down = _mm_nn("mm_dw_down", actbt, dr2b, BF16, tm=rs_down, tn=1024, tk=t)
    s_down, tok = scatter_start("a2a_w_down_start", gw_down.reshape(N_DEV, rs_down, D_MODEL))
    dhf, g_fw, g_fb = _ffn_act_bwd("ffn_act_bwd", dact, hf, fw_full, ffn_conv_b + tok[0, 0])
    tm = min(1024, t)
    gw_up = _matmul(
        "mm_dw_up", h1bt, dhf, (N_DEV, D_MODEL, ns_up), BF16, (D_MODEL // 1024, N_DEV, 1),
        pl.BlockSpec((1024, t), lambda i, j, kk: (i, 0)),
        pl.BlockSpec((1, t, ns_up), lambda i, j, kk: (j // 4, 0, j % 4)),
        pl.BlockSpec((1, 1024, ns_up), lambda i, j, kk: (j, i, 0)), nt=False)
    s_up, tok = scatter_start("a2a_w_up_start", gw_up)
    dh1 = _matmul(
        "mm_dh1", dhf, wup_n, (t, D_MODEL), F32, (t // tm, D_MODEL // 512, 2),
        pl.BlockSpec((1, tm, D_FF), lambda i, j, kk: (kk, i, 0)),
        pl.BlockSpec((512, D_FF), lambda i, j, kk: (j, kk)),
        pl.BlockSpec((tm, 512), lambda i, j, kk: (i, j)), nt=True, after=tok)
    dr1, dr1b, g_ln1g, g_ln1b = _ln_bwd("ln1_bwd", r1, dr2, dh1, ln1_g + tok[0, 0], ALPHA, True)
    gw_out = _mm_nn("mm_dw_out", catb.T, dr1b, BF16, tm=1024, tn=1024, tk=t)
    s_out, tok = scatter_start("a2a_w_out_start", gw_out.reshape(N_DEV, rs_out, D_MODEL))
    dcat = _mm_nt("mm_dcat", dr1b, wout_g, F32, tm=1024, tn=1024, tk=D_MODEL, after=tok)
    da, dgate, g_cw, g_cb, g_cng, g_cnb = _conv_bwd("conv_bwd", dcat, u1, hin, cw_full, conv_norm_g + tok[0, 0],
                                                    conv_norm_b)
    dq, df, di, dog, g_hg, g_lb = _hgrn_bwd("hgrn_bwd", dcat, hin, o_raw, states, lb_logits, hgrn_norm_g)
    dhin = jnp.concatenate([da, dgate, dq, df, di, dog], axis=1)
    dh0 = _mm_nt("mm_dh0", dhin, win_n, F32, tm=1024, tn=1024, tk=IN_PROJ // 2)
    grad_x, g_eg, g_eb = _ln_bwd("ln_in_bwd", x2, dr1, dh0, row(emb_ln_g), ALPHA, False)

    small_shapes = [(D_MODEL,), (D_MODEL,), (CONV_KERNEL, CONV_WIDTH), (1, CONV_WIDTH), (1, CONV_WIDTH),
                    (1, CONV_WIDTH), (2, HGRN_WIDTH), (1, HGRN_WIDTH), (1, D_MODEL), (1, D_MODEL),
                    (FFN_KERNEL, D_FF), (1, D_FF), (1, D_MODEL), (1, D_MODEL), (128,)]
    rows_small = 569
    packed = _pack([g_eg, g_eb, g_cw[:CONV_KERNEL], g_cb, g_cng, g_cnb, g_lb, g_hg, g_ln1g, g_ln1b,
                    g_fw[:FFN_KERNEL], g_fb, g_ln2g, g_ln2b, loss], rows_small)
    h_small, tok = _exchange_start("ag_small_start", packed, _own_slot(packed, me), scatter=False)
    half = D_MODEL // 2
    gw_in_a = _mm_grad_cols("mm_dw_in_a", h0bt, dhin, ns_in, 0, half, after=tok)
    s_in_a, tok = scatter_start("a2a_w_in_a_start", gw_in_a)
    gw_in_b = _mm_grad_cols("mm_dw_in_b", h0bt, dhin, ns_in, half, half, after=tok)
    s_in_b, tok = scatter_start("a2a_w_in_b_start", gw_in_b)
    summed = _sum_parts("sum_small", _exchange_wait("ag_small_wait", h_small, tok))
    (s_eg, s_eb, s_cw, s_cb, s_cng, s_cnb, s_lb, s_hg, s_l1g, s_l1b, s_fw, s_fb, s_l2g, s_l2b,
     s_loss) = _unpack(summed, small_shapes)
    s_cw = lax.dynamic_slice_in_dim(s_cw, me * cs, cs, axis=1)[None]
    s_fw = lax.dynamic_slice_in_dim(s_fw, me * fs, fs, axis=1)[None]
    g_small = [s_eg, s_eb, s_cw, s_cb, s_cng, s_cnb, s_lb, s_hg, s_l1g, s_l1b, s_fw, s_fb, s_l2g, s_l2b]
    w_small = [emb_ln_g, emb_ln_b, conv_w, conv_b, conv_norm_g, conv_norm_b, lb_logits, hgrn_norm_g,
               ln1_g, ln1_b, ffn_conv_w, ffn_conv_b, ln2_g, ln2_b]
    m_small = [m_emb_ln_g, m_emb_ln_b, m_conv_w, m_conv_b, m_conv_norm_g, m_conv_norm_b, m_lb_logits,
               m_hgrn_norm_g, m_ln1_g, m_ln1_b, m_ffn_conv_w, m_ffn_conv_b, m_ln2_g, m_ln2_b]
    v_small = [v_emb_ln_g, v_emb_ln_b, v_conv_w, v_conv_b, v_conv_norm_g, v_conv_norm_b, v_lb_logits,
               v_hgrn_norm_g, v_ln1_g, v_ln1_b, v_ffn_conv_w, v_ffn_conv_b, v_ln2_g, v_ln2_b]
    rows_own = 236
    shapes_own = [w.shape for w in w_small]
    upd = _adamw_small("adamw_small", _pack(w_small, rows_own), _pack(g_small, rows_own),
                       _pack(m_small, rows_own), _pack(v_small, rows_own))
    d_small, nm_small, nv_small = (_unpack(u, shapes_own) for u in upd)
    g_small = [g.reshape(s) for g, s in zip(g_small, shapes_own)]

    def big(name, handle, after, w, m, v, tr):
        recv = _exchange_wait("a2a_" + name + "_wait", handle, after)
        return [o[None] for o in _adamw_sum("adamw_" + name, recv, w[0], m[0], v[0], tr)]

    u_down = big("w_down", s_down, upd[0], w_ffn_down, m_w_ffn_down, v_w_ffn_down, 64)
    u_up = big("w_up", s_up, u_down[1], w_ffn_up, m_w_ffn_up, v_w_ffn_up, 64)
    u_out = big("w_out", s_out, u_up[1], w_out, m_w_out, v_w_out, 64)
    recv_a = _exchange_wait("a2a_w_in_a_wait", s_in_a, u_out[1])
    part = _adamw_sum("adamw_w_in_a", recv_a, w_in[0], m_w_in[0], v_w_in[0], 128)
    recv_b = _exchange_wait("a2a_w_in_b_wait", s_in_b, part[1])
    u_in = [o[None] for o in _adamw_sum("adamw_w_in_b", recv_b, w_in[0], m_w_in[0], v_w_in[0], 128,
                                        row0=half, partial=part)]

    def ordered(small, i_in, i_out, i_up, i_down):
        (eg, eb, cw, cb, cng, cnb, lb, hg, l1g, l1b, fw, fb, l2g, l2b) = small
        return [eg, eb, i_in, cw, cb, cng, cnb, lb, hg, i_out, l1g, l1b, i_up, fw, fb, i_down, l2g, l2b]

    outs = [s_loss[0], grad_x[None]]
    for k, small in enumerate([g_small, d_small, nm_small, nv_small]):
        outs += ordered(small, u_in[k], u_out[k], u_up[k], u_down[k])
    return tuple(outs)
```

```python
import functools

import jax
import jax.numpy as jnp
from jax import lax
from jax.experimental import pallas as pl
from jax.experimental.pallas import tpu as pltpu

F32 = jnp.float32
BF16 = jnp.bfloat16

N_DEV = 8
D_MODEL = 2048
CONV_WIDTH = 1024
CONV_KERNEL = 31
HGRN_WIDTH = 1024
GROUP = 128
N_GROUPS = 8
IN_PROJ = 2 * CONV_WIDTH + 4 * HGRN_WIDTH
D_FF = 5632
FFN_KERNEL = 3
CHUNK = 64
SUB = 8
LN_EPS = 1e-5
RMS_EPS = 1e-6
ALPHA = 2.0 ** 0.25
ADAM_LR, ADAM_B1, ADAM_B2, ADAM_EPS, ADAM_WD, ADAM_STEP = 0.001, 0.9, 0.999, 1e-08, 0.01, 10

VMEM_LIMIT = 56 * 1024 * 1024
MESH = pl.DeviceIdType.MESH


def _cparams(sem=None):
    return pltpu.CompilerParams(dimension_semantics=sem, vmem_limit_bytes=VMEM_LIMIT)


def _sigmoid(x):
    return 1.0 / (1.0 + jnp.exp(-x))


def _matmul(name, a, b, out_shape, out_dtype, grid, a_spec, b_spec, o_spec, nt, after=None):
    nk = grid[2]
    dims = (((1,), (1,)), ((), ())) if nt else (((1,), (0,)), ((), ()))
    extra = [] if after is None else [after]

    def body(a_ref, b_ref, *rest):
        o_ref, *scratch = rest[len(extra):]
        av = a_ref[0] if len(a_ref.shape) == 3 else a_ref[...]
        bv = b_ref[0] if len(b_ref.shape) == 3 else b_ref[...]
        part = lax.dot_general(av, bv, dims, preferred_element_type=F32)

        def write(res):
            if len(o_ref.shape) == 3:
                o_ref[0] = res.astype(out_dtype)
            else:
                o_ref[...] = res.astype(out_dtype)

        if nk == 1:
            write(part)
            return
        acc_ref, = scratch
        k = pl.program_id(2)

        @pl.when(k == 0)
        def _():
            acc_ref[...] = part

        @pl.when(jnp.logical_and(k > 0, k < nk - 1))
        def _():
            acc_ref[...] += part

        @pl.when(k == nk - 1)
        def _():
            write(acc_ref[...] + part)

    acc_shape = o_spec.block_shape[-2:]
    assert all(g >= 1 for g in grid), (name, grid)
    return pl.pallas_call(
        body, name=name, grid=grid, in_specs=[a_spec, b_spec] + [pl.BlockSpec(memory_space=pl.ANY)] * len(extra),
        out_specs=o_spec, out_shape=jax.ShapeDtypeStruct(out_shape, out_dtype),
        scratch_shapes=[pltpu.VMEM(acc_shape, F32)] if nk > 1 else [],
        compiler_params=_cparams(("parallel", "parallel", "arbitrary")),
    )(a, b, *extra)


def _mm_nn(name, a, w, out_dtype, tm, tn, tk, after=None):
    m, k = a.shape
    tm, tk = min(tm, m), min(tk, k)
    n = w.shape[1]
    return _matmul(
        name, a, w, (m, n), out_dtype, (m // tm, n // tn, k // tk),
        pl.BlockSpec((tm, tk), lambda i, j, kk: (i, kk)),
        pl.BlockSpec((tk, tn), lambda i, j, kk: (kk, j)),
        pl.BlockSpec((tm, tn), lambda i, j, kk: (i, j)), nt=False, after=after)


def _mm_nt(name, a, w, out_dtype, tm, tn, tk, after=None):
    m, k = a.shape
    tm = min(tm, m)
    n = w.shape[0]
    return _matmul(
        name, a, w, (m, n), out_dtype, (m // tm, n // tn, k // tk),
        pl.BlockSpec((tm, tk), lambda i, j, kk: (i, kk)),
        pl.BlockSpec((tn, tk), lambda i, j, kk: (j, kk)),
        pl.BlockSpec((tm, tn), lambda i, j, kk: (i, j)), nt=True, after=after)


def _mm_grad_cols(name, at, b, ns, row0, rows, after, tm=1024, tk=4096):
    t = at.shape[1]
    tk = min(tk, t)
    off = row0 // tm
    return _matmul(
        name, at, b, (N_DEV, rows, ns), BF16, (rows // tm, N_DEV, t // tk),
        pl.BlockSpec((tm, tk), lambda i, j, kk: (i + off, kk)),
        pl.BlockSpec((tk, ns), lambda i, j, kk: (kk, j)),
        pl.BlockSpec((1, tm, ns), lambda i, j, kk: (j, i, 0)), nt=False, after=after)


LN_ROWS = 256


def _ln_stats(r):
    mu = jnp.mean(r, axis=-1, keepdims=True)
    xc = r - mu
    var = jnp.mean(xc * xc, axis=-1, keepdims=True)
    rstd = lax.rsqrt(var + LN_EPS)
    return xc * rstd, rstd


def _row_spec(d):
    return pl.BlockSpec((LN_ROWS, d), lambda i: (i, 0))


def _vec_spec(d):
    return pl.BlockSpec((1, d), lambda i: (0, 0))


def _ln_fwd(name, a, m, g, b, alpha):
    t, d = a.shape
    has_m = m is not None

    def body(*refs):
        if has_m:
            a_ref, m_ref, g_ref, b_ref, r_ref, y_ref, yb_ref, yt_ref = refs
            r = alpha * a_ref[...] + m_ref[...]
            r_ref[...] = r
        else:
            a_ref, g_ref, b_ref, y_ref, yb_ref, yt_ref = refs
            r = a_ref[...]
        xhat, _ = _ln_stats(r)
        y = xhat * g_ref[...] + b_ref[...]
        y_ref[...] = y
        yb_ref[...] = y.astype(BF16)
        yt_ref[...] = y.T.astype(BF16)

    ins = [a] + ([m] if has_m else []) + [g, b]
    in_specs = [_row_spec(d)] * (2 if has_m else 1) + [_vec_spec(d)] * 2
    outs = ([jax.ShapeDtypeStruct((t, d), F32)] if has_m else []) + [
        jax.ShapeDtypeStruct((t, d), F32), jax.ShapeDtypeStruct((t, d), BF16), jax.ShapeDtypeStruct((d, t), BF16)]
    res = pl.pallas_call(
        body, name=name, grid=(t // LN_ROWS,), in_specs=in_specs,
        out_specs=[_row_spec(d)] * (len(outs) - 1) + [pl.BlockSpec((d, LN_ROWS), lambda i: (0, i))], out_shape=outs,
        compiler_params=_cparams(("parallel",)),
    )(*ins)
    return res if has_m else (None, *res)


def _ln_bwd_math(r, dy, g):
    xhat, rstd = _ln_stats(r)
    dxhat = dy * g
    m1 = jnp.mean(dxhat, axis=-1, keepdims=True)
    m2 = jnp.mean(dxhat * xhat, axis=-1, keepdims=True)
    dr = rstd * (dxhat - m1 - xhat * m2)
    return dr, jnp.sum(dy * xhat, axis=0, keepdims=True), jnp.sum(dy, axis=0, keepdims=True)


def _ln2_loss_bwd(name, h1, ffn, g, b, tgt):
    t, d = h1.shape

    def body(h1_ref, f_ref, g_ref, b_ref, t_ref, dr_ref, drb_ref, dg_ref, db_ref, loss_ref):
        @pl.when(pl.program_id(0) == 0)
        def _():
            dg_ref[...] = jnp.zeros_like(dg_ref)
            db_ref[...] = jnp.zeros_like(db_ref)
            loss_ref[...] = jnp.zeros_like(loss_ref)

        r = ALPHA * h1_ref[...] + f_ref[...]
        xhat, _ = _ln_stats(r)
        e = xhat * g_ref[...] + b_ref[...] - t_ref[...]
        loss_ref[...] += 0.5 / d * jnp.sum(e * e)
        dr, dg, db = _ln_bwd_math(r, e * (1.0 / d), g_ref[...])
        dr_ref[...] = dr
        drb_ref[...] = dr.astype(BF16)
        dg_ref[...] += dg
        db_ref[...] += db

    return pl.pallas_call(
        body, name=name, grid=(t // LN_ROWS,),
        in_specs=[_row_spec(d), _row_spec(d), _vec_spec(d), _vec_spec(d), _row_spec(d)],
        out_specs=[_row_spec(d), _row_spec(d), _vec_spec(d), _vec_spec(d), _vec_spec(128)],
        out_shape=[jax.ShapeDtypeStruct((t, d), F32), jax.ShapeDtypeStruct((t, d), BF16),
                   jax.ShapeDtypeStruct((1, d), F32), jax.ShapeDtypeStruct((1, d), F32),
                   jax.ShapeDtypeStruct((1, 128), F32)],
        compiler_params=_cparams(("arbitrary",)),
    )(h1, ffn, g, b, tgt)


def _ln_bwd(name, r, dya, dyb, g, alpha, want_bf16):
    t, d = r.shape

    def body(r_ref, dya_ref, dyb_ref, g_ref, *outs):
        dr_ref = outs[0]
        dg_ref, db_ref = outs[-2:]

        @pl.when(pl.program_id(0) == 0)
        def _():
            dg_ref[...] = jnp.zeros_like(dg_ref)
            db_ref[...] = jnp.zeros_like(db_ref)

        dy = alpha * dya_ref[...] + dyb_ref[...]
        dr, dg, db = _ln_bwd_math(r_ref[...], dy, g_ref[...])
        dr_ref[...] = dr
        if want_bf16:
            outs[1][...] = dr.astype(BF16)
        dg_ref[...] += dg
        db_ref[...] += db

    big = [jax.ShapeDtypeStruct((t, d), F32)] + ([jax.ShapeDtypeStruct((t, d), BF16)] if want_bf16 else [])
    return pl.pallas_call(
        body, name=name, grid=(t // LN_ROWS,),
        in_specs=[_row_spec(d)] * 3 + [_vec_spec(d)],
        out_specs=[_row_spec(d)] * len(big) + [_vec_spec(d)] * 2,
        out_shape=big + [jax.ShapeDtypeStruct((1, d), F32)] * 2,
        compiler_params=_cparams(("arbitrary",)),
    )(r, dya, dyb, g)


CONV_ROWS = 64
CONV_UNROLL = 4
FFN_UNROLL = 2


def _unrolled(n, unroll, fn, init):
    def body(i, carry):
        for u in range(unroll):
            carry = fn(i * unroll + u, carry)
        return carry

    return lax.fori_loop(0, n // unroll, body, init)


def _for_shifted(ref, r0, tm, shifts, fn):
    for s in shifts:
        fn(s, ref[pl.ds(r0 + s, tm), :])


def _col_spec(t, cb, off=0):
    return pl.BlockSpec((t, cb), lambda j: (0, j + off))


def _ffn_act_fwd(name, hf, w, b, cb=128):
    t = hf.shape[0]
    f = hf.shape[1] // 2
    nb = f // cb
    tm = CONV_ROWS

    def body(g_ref, v_ref, w_ref, b_ref, act_ref, pad_ref):
        pad_ref[pl.ds(0, 8), :] = jnp.zeros((8, cb), F32)
        pad_ref[pl.ds(8, t), :] = g_ref[...].astype(F32)
        wv = [w_ref[pl.ds(k, 1), :] for k in range(FFN_KERNEL)]
        bias = b_ref[...]

        def tile(i, carry):
            r0 = pl.multiple_of(i * tm, tm)
            acc = [jnp.broadcast_to(bias, (tm, cb))]

            def tap(s, rows):
                acc[0] = acc[0] + wv[s - 6] * rows

            _for_shifted(pad_ref, r0, tm, (6, 7, 8), tap)
            gc = acc[0]
            act_ref[pl.ds(r0, tm), :] = (gc * _sigmoid(gc) * v_ref[pl.ds(r0, tm), :].astype(F32)).astype(BF16)
            return carry

        _unrolled(t // tm, FFN_UNROLL, tile, 0)

    return pl.pallas_call(
        body, name=name, grid=(nb,),
        in_specs=[_col_spec(t, cb), _col_spec(t, cb, nb),
                  pl.BlockSpec((FFN_KERNEL, cb), lambda j: (0, j)), pl.BlockSpec((1, cb), lambda j: (0, j))],
        out_specs=_col_spec(t, cb), out_shape=jax.ShapeDtypeStruct((t, f), BF16),
        scratch_shapes=[pltpu.VMEM((t + 8, cb), F32)],
        compiler_params=_cparams(("parallel",)),
    )(hf, hf, w, b)


def _ffn_act_bwd(name, dact, hf, w, b, cb=128):
    t = hf.shape[0]
    f = hf.shape[1] // 2
    nb = f // cb
    tm = CONV_ROWS

    def body(da_ref, g_ref, v_ref, w_ref, b_ref, dhf_ref, dw_ref, db_ref, pad_ref, dgc_ref):
        pad_ref[pl.ds(0, 8), :] = jnp.zeros((8, cb), F32)
        pad_ref[pl.ds(8, t), :] = g_ref[...].astype(F32)
        dgc_ref[pl.ds(t, 8), :] = jnp.zeros((8, cb), F32)
        wv = [w_ref[pl.ds(k, 1), :] for k in range(FFN_KERNEL)]
        bias = b_ref[...]

        def tile_a(i, carry):
            r0 = pl.multiple_of(i * tm, tm)
            taps = {}
            _for_shifted(pad_ref, r0, tm, (6, 7, 8), lambda s, rows: taps.__setitem__(s, rows))
            gc = bias + wv[0] * taps[6] + wv[1] * taps[7] + wv[2] * taps[8]
            sg = _sigmoid(gc)
            da = da_ref[pl.ds(r0, tm), :].astype(F32)
            dhf_ref[1, pl.ds(r0, tm), :] = (da * gc * sg).astype(BF16)
            dgc = da * v_ref[pl.ds(r0, tm), :].astype(F32) * sg * (1.0 + gc * (1.0 - sg))
            dgc_ref[pl.ds(r0, tm), :] = dgc
            sums = [jnp.sum(dgc * taps[6 + k], axis=0, keepdims=True) for k in range(3)]
            sums.append(jnp.sum(dgc, axis=0, keepdims=True))
            return tuple(c + s for c, s in zip(carry, sums))

        zero = jnp.zeros((1, cb), F32)
        dw0, dw1, dw2, dbias = _unrolled(t // tm, FFN_UNROLL, tile_a, (zero, zero, zero, zero))
        row = lax.broadcasted_iota(jnp.int32, (8, cb), 0)
        dw_ref[...] = jnp.where(row == 0, dw0, jnp.where(row == 1, dw1, jnp.where(row == 2, dw2, 0.0)))
        db_ref[...] = dbias

        def tile_b(i, carry):
            r0 = pl.multiple_of(i * tm, tm)
            acc = [jnp.zeros((tm, cb), F32)]

            def tap(s, rows):
                acc[0] = acc[0] + wv[2 - s] * rows

            _for_shifted(dgc_ref, r0, tm, (0, 1, 2), tap)
            dhf_ref[0, pl.ds(r0, tm), :] = acc[0].astype(BF16)
            return carry

        lax.fori_loop(0, t // tm, tile_b, 0)

    return pl.pallas_call(
        body, name=name, grid=(nb,),
        in_specs=[_col_spec(t, cb), _col_spec(t, cb), _col_spec(t, cb, nb),
                  pl.BlockSpec((FFN_KERNEL, cb), lambda j: (0, j)), pl.BlockSpec((1, cb), lambda j: (0, j))],
        out_specs=[pl.BlockSpec((2, t, cb), lambda j: (0, 0, j)),
                   pl.BlockSpec((8, cb), lambda j: (0, j)), pl.BlockSpec((1, cb), lambda j: (0, j))],
        out_shape=[jax.ShapeDtypeStruct((2, t, f), BF16), jax.ShapeDtypeStruct((8, f), F32),
                   jax.ShapeDtypeStruct((1, f), F32)],
        scratch_shapes=[pltpu.VMEM((t + 8, cb), F32), pltpu.VMEM((t + 8, cb), F32)],
        compiler_params=_cparams(("parallel",)),
    )(dact, hf, hf, w, b)


def _silu_grad(z, sg):
    return sg * (1.0 + z * (1.0 - sg))


def _conv_fwd(name, hin, w, b, ng, nb_, cat):
    t = hin.shape[0]
    c = GROUP
    tm = CONV_ROWS
    pad = 32
    shifts = tuple(2 + k for k in range(CONV_KERNEL))

    def body(a_ref, gt_ref, w_ref, b_ref, ng_ref, nb_ref, cat_ref, u1_ref, u3_ref, pad_ref):
        pad_ref[pl.ds(0, pad), :] = jnp.zeros((pad, c), F32)
        pad_ref[pl.ds(pad, t), :] = a_ref[...] * _sigmoid(gt_ref[...])
        bias, gam, bet = b_ref[...], ng_ref[...], nb_ref[...]

        def tile(i, carry):
            r0 = pl.multiple_of(i * tm, tm)
            acc = [jnp.broadcast_to(bias, (tm, c))]

            def tap(s, rows):
                acc[0] = acc[0] + w_ref[pl.ds(s - 2, 1), :] * rows

            _for_shifted(pad_ref, r0, tm, shifts, tap)
            u1 = acc[0]
            u1_ref[pl.ds(r0, tm), :] = u1
            xhat, _ = _ln_stats(u1)
            u2 = xhat * gam + bet
            u3_ref[pl.ds(r0, tm), :] = (u2 * _sigmoid(u2)).astype(BF16)
            return carry

        _unrolled(t // tm, CONV_UNROLL, tile, 0)

    vec = pl.BlockSpec((1, c), lambda j: (0, j))
    return pl.pallas_call(
        body, name=name, grid=(N_GROUPS,),
        in_specs=[_col_spec(t, c), _col_spec(t, c, N_GROUPS),
                  pl.BlockSpec((CONV_KERNEL, c), lambda j: (0, j)), vec, vec, vec, ANY],
        out_specs=[_col_spec(t, c), _col_spec(t, c)],
        out_shape=[jax.ShapeDtypeStruct((t, CONV_WIDTH), F32), jax.ShapeDtypeStruct(cat.shape, BF16)],
        input_output_aliases={6: 1},
        scratch_shapes=[pltpu.VMEM((t + pad, c), F32)],
        compiler_params=_cparams(("parallel",)),
    )(hin, hin, w, b, ng, nb_, cat)


def _conv_bwd(name, dcat, u1, hin, w, ng, nb_):
    t = hin.shape[0]
    c = GROUP
    tm = CONV_ROWS
    pad = 32
    nk = CONV_KERNEL

    def body(du3_ref, u1_ref, a_ref, gt_ref, w_ref, ng_ref, nb_ref,
             da_ref, dgt_ref, dw_ref, db_ref, dng_ref, dnb_ref, u0_ref, du1_ref, dwp_ref):
        u0_ref[pl.ds(0, pad), :] = jnp.zeros((pad, c), F32)
        u0_ref[pl.ds(pad, t), :] = a_ref[...] * _sigmoid(gt_ref[...])
        du1_ref[pl.ds(t, pad), :] = jnp.zeros((pad, c), F32)
        dwp_ref[...] = jnp.zeros_like(dwp_ref)
        gam, bet = ng_ref[...], nb_ref[...]

        def tile_a(i, carry):
            r0 = pl.multiple_of(i * tm, tm)
            u1 = u1_ref[pl.ds(r0, tm), :]
            xhat, rstd = _ln_stats(u1)
            u2 = xhat * gam + bet
            sg = _sigmoid(u2)
            du2 = du3_ref[pl.ds(r0, tm), :] * _silu_grad(u2, sg)
            dxhat = du2 * gam
            m1 = jnp.mean(dxhat, axis=-1, keepdims=True)
            m2 = jnp.mean(dxhat * xhat, axis=-1, keepdims=True)
            du1 = rstd * (dxhat - m1 - xhat * m2)
            du1_ref[pl.ds(r0, tm), :] = du1
            sums = (jnp.sum(du1, axis=0, keepdims=True), jnp.sum(du2 * xhat, axis=0, keepdims=True),
                    jnp.sum(du2, axis=0, keepdims=True))
            return tuple(x + s for x, s in zip(carry, sums))

        zero = jnp.zeros((1, c), F32)
        dbias, dgam, dbet = _unrolled(t // tm, CONV_UNROLL, tile_a, (zero, zero, zero))
        db_ref[...] = dbias
        dng_ref[...] = dgam
        dnb_ref[...] = dbet

        def tile_b(i, carry):
            r0 = pl.multiple_of(i * tm, tm)
            du1 = du1_ref[pl.ds(r0, tm), :]
            acc = [jnp.zeros((tm, c), F32)]

            def tap_dx(s, rows):
                acc[0] = acc[0] + w_ref[pl.ds(nk - 1 - s, 1), :] * rows

            _for_shifted(du1_ref, r0, tm, tuple(range(nk)), tap_dx)

            def tap_dw(s, rows):
                part = (du1 * rows).reshape(tm // 8, 8, c).sum(axis=0)
                dwp_ref[s - 2] = dwp_ref[s - 2] + part

            _for_shifted(u0_ref, r0, tm, tuple(2 + k for k in range(nk)), tap_dw)
            du0 = acc[0]
            a = a_ref[pl.ds(r0, tm), :]
            sg = _sigmoid(gt_ref[pl.ds(r0, tm), :])
            da_ref[pl.ds(r0, tm), :] = (du0 * sg).astype(BF16)
            dgt_ref[pl.ds(r0, tm), :] = (du0 * a * sg * (1.0 - sg)).astype(BF16)
            return carry

        lax.fori_loop(0, t // tm, tile_b, 0)
        dw_ref[...] = jnp.sum(dwp_ref[...], axis=1)

    vec = pl.BlockSpec((1, c), lambda j: (0, j))
    vshape = jax.ShapeDtypeStruct((1, CONV_WIDTH), F32)
    return pl.pallas_call(
        body, name=name, grid=(N_GROUPS,),
        in_specs=[_col_spec(t, c), _col_spec(t, c), _col_spec(t, c), _col_spec(t, c, N_GROUPS),
                  pl.BlockSpec((nk, c), lambda j: (0, j)), vec, vec],
        out_specs=[_col_spec(t, c), _col_spec(t, c), pl.BlockSpec((32, c), lambda j: (0, j)), vec, vec, vec],
        out_shape=[jax.ShapeDtypeStruct((t, CONV_WIDTH), BF16), jax.ShapeDtypeStruct((t, CONV_WIDTH), BF16),
                   jax.ShapeDtypeStruct((32, CONV_WIDTH), F32), vshape, vshape, vshape],
        scratch_shapes=[pltpu.VMEM((t + pad, c), F32), pltpu.VMEM((t + pad, c), F32),
                        pltpu.VMEM((32, 8, c), F32)],
        compiler_params=_cparams(("parallel",)),
    )(dcat, u1, hin, hin, w, ng, nb_)


LEVELS = (64, 32, 16)
HGRN_UNROLL = 2
HGRN_UNROLL_FWD = 4
NT_DIMS = (((1,), (1,)), ((), ()))
NN_DIMS = (((1,), (0,)), ((), ()))
TN_DIMS = (((0,), (0,)), ((), ()))


def _bdot(a, b, dims):
    return lax.dot_general(a.astype(BF16), b.astype(BF16), dims, preferred_element_type=F32)


def _hdot(a, b):
    return jnp.dot(a, b, precision=lax.Precision.HIGHEST, preferred_element_type=F32)


def _chunk_consts():
    rid = lax.broadcasted_iota(jnp.int32, (CHUNK, GROUP), 0)
    ti = lax.broadcasted_iota(jnp.int32, (CHUNK, CHUNK), 0)
    si = lax.broadcasted_iota(jnp.int32, (CHUNK, CHUNK), 1)
    tri = (si <= ti).astype(F32)
    second = [(rid & (b // 2)) != 0 for b in LEVELS]
    same = [None] + [(ti // b) == (si // b) for b in LEVELS[1:]]
    sub = lax.broadcasted_iota(jnp.int32, (SUB, GROUP), 0)
    return rid, tri, second, same, sub


def _level_refs(cum_ref, rid, base):
    row = lambda i: cum_ref[pl.ds(base + i, 1), :]
    l1 = jnp.broadcast_to(row(31), (CHUNK, GROUP))
    l2 = jnp.where(rid < 32, row(15), row(47))
    l3 = jnp.where(rid < 16, row(7), jnp.where(rid < 32, row(23), jnp.where(rid < 48, row(39), row(55))))
    return l1, l2, l3


def _level_factors(cum, brefs, second):
    out = []
    for bref, sec in zip(brefs, second):
        eq = jnp.where(sec, jnp.exp(jnp.minimum(cum - bref, 0.0)), 0.0)
        ek = jnp.where(sec, 0.0, jnp.exp(jnp.minimum(bref - cum, 0.0)))
        out.append((eq, ek))
    return out


def _gates(q, f, lb):
    sq = _sigmoid(q)
    sf = _sigmoid(f)
    fg = lb + (1.0 - lb) * sf
    return q * sq, sq, sf, fg


def _hgrn_specs(t, nc):
    c = GROUP
    col = lambda off: pl.BlockSpec((t, c), lambda h: (0, h + off))
    hin_specs = [col(16), col(24), col(32), col(40)]
    vec = pl.BlockSpec((1, c), lambda h: (0, h))
    lbs = pl.BlockSpec((2, c), lambda h: (0, h))
    st = pl.BlockSpec((1, nc, c, c), lambda h: (h, 0, 0, 0))
    return col, hin_specs, vec, lbs, st


def _hgrn_fwd(name, hin, lb_logits, hg):
    t = hin.shape[0]
    nc = t // CHUNK
    c = GROUP
    col, hin_specs, vec, lbs, st = _hgrn_specs(t, nc)

    def body(q_ref, f_ref, v_ref, og_ref, lb_ref, hg_ref, o_ref, ob_ref, st_ref,
             s_ref, cum_ref, kk_ref, vc_ref):
        rid, tri, second, same, sub = _chunk_consts()
        lb = _sigmoid(lb_ref[pl.ds(0, 1), :] - lb_ref[pl.ds(1, 1), :])
        gain = hg_ref[...]
        s_ref[...] = jnp.zeros_like(s_ref)

        def chunk(ci, u):
            base = u * CHUNK
            r0 = pl.multiple_of(ci * CHUNK, CHUNK)
            rows = pl.ds(r0, CHUNK)
            qh, _, _, fg = _gates(q_ref[rows, :], f_ref[rows, :], lb)
            v = v_ref[rows, :]
            kk = 1.0 - fg
            cum = _hdot(tri, jnp.log(fg))
            cum_ref[pl.ds(base, CHUNK), :] = cum
            kk_ref[pl.ds(base, CHUNK), :] = kk
            vc_ref[pl.ds(base, CHUNK), :] = v
            sprev = s_ref[...]
            st_ref[0, ci] = sprev
            blast = cum_ref[pl.ds(base + CHUNK - 1, 1), :]
            o = _bdot(qh * jnp.exp(cum), sprev, NT_DIMS)
            s_ref[...] = sprev * jnp.exp(blast) + _bdot(v, kk * jnp.exp(blast - cum), TN_DIMS)
            a = None
            for (eq, ek), msk in zip(_level_factors(cum, _level_refs(cum_ref, rid, base), second), same):
                al = _bdot(qh * eq, kk * ek, NT_DIMS)
                al = al if msk is None else jnp.where(msk, al, 0.0)
                a = al if a is None else a + al
            o = o + _bdot(a, v, NN_DIMS)
            diag = []
            for sb in range(CHUNK // SUB):
                lo = sb * SUB
                qb = qh[lo:lo + SUB]
                cb = cum[lo:lo + SUB]
                od = jnp.zeros((SUB, c), F32)
                for s in range(SUB):
                    e = jnp.where(sub >= s, jnp.exp(jnp.minimum(cb - cum_ref[pl.ds(base + lo + s, 1), :], 0.0)), 0.0)
                    acol = jnp.sum(qb * e * kk_ref[pl.ds(base + lo + s, 1), :], axis=-1, keepdims=True)
                    od = od + acol * vc_ref[pl.ds(base + lo + s, 1), :]
                diag.append(od)
            o = o + jnp.concatenate(diag, axis=0)
            o_ref[rows, :] = o
            y = o * lax.rsqrt(jnp.mean(o * o, axis=-1, keepdims=True) + RMS_EPS) * gain
            og = og_ref[rows, :]
            ob_ref[rows, :] = (y * og * _sigmoid(og)).astype(BF16)

        def chunks(i, carry):
            for u in range(HGRN_UNROLL_FWD):
                chunk(i * HGRN_UNROLL_FWD + u, u)
            return carry

        lax.fori_loop(0, nc // HGRN_UNROLL_FWD, chunks, 0)

    return pl.pallas_call(
        body, name=name, grid=(N_GROUPS,),
        in_specs=hin_specs + [lbs, vec],
        out_specs=[col(0), col(N_GROUPS), st],
        out_shape=[jax.ShapeDtypeStruct((t, HGRN_WIDTH), F32), jax.ShapeDtypeStruct((t, CONV_WIDTH + HGRN_WIDTH), BF16),
                   jax.ShapeDtypeStruct((N_GROUPS, nc, c, c), F32)],
        scratch_shapes=[pltpu.VMEM((c, c), F32)] + [pltpu.VMEM((HGRN_UNROLL_FWD * CHUNK, c), F32)] * 3,
        compiler_params=_cparams(("parallel",)),
    )(hin, hin, hin, hin, lb_logits, hg)


def _hgrn_bwd(name, dcat, hin, o_raw, states, lb_logits, hg):
    t = hin.shape[0]
    nc = t // CHUNK
    c = GROUP
    col, hin_specs, vec, lbs, st = _hgrn_specs(t, nc)

    def body(do_ref, q_ref, f_ref, v_ref, og_ref, o_ref, st_ref, lb_ref, hg_ref,
             dq_ref, df_ref, dv_ref, dog_ref, dhg_ref, dlb_ref,
             ds_ref, cum_ref, kk_ref, vc_ref):
        rid, tri, second, same, sub = _chunk_consts()
        trit = tri.T
        lb = _sigmoid(lb_ref[pl.ds(0, 1), :] - lb_ref[pl.ds(1, 1), :])
        gain = hg_ref[...]
        ds_ref[...] = jnp.zeros_like(ds_ref)

        def chunk(i, carry, u):
            base = u * CHUNK
            dhg, dlb = carry
            ci = nc - 1 - i
            r0 = pl.multiple_of(ci * CHUNK, CHUNK)
            rows = pl.ds(r0, CHUNK)
            q = q_ref[rows, :]
            qh, sq, sf, fg = _gates(q, f_ref[rows, :], lb)
            v = v_ref[rows, :]
            kk = 1.0 - fg
            cum = _hdot(tri, jnp.log(fg))
            cum_ref[pl.ds(base, CHUNK), :] = cum
            kk_ref[pl.ds(base, CHUNK), :] = kk
            vc_ref[pl.ds(base, CHUNK), :] = v
            o = o_ref[rows, :]
            og = og_ref[rows, :]
            sg = _sigmoid(og)
            rinv = lax.rsqrt(jnp.mean(o * o, axis=-1, keepdims=True) + RMS_EPS)
            yn = o * rinv
            dof = do_ref[rows, :]
            dog_ref[rows, :] = (dof * yn * gain * _silu_grad(og, sg)).astype(BF16)
            dz = dof * og * sg
            dhg = dhg + jnp.sum(dz * yn, axis=0, keepdims=True)
            dy = dz * gain
            do = rinv * (dy - yn * jnp.mean(dy * yn, axis=-1, keepdims=True))
            sprev = st_ref[0, ci]
            dsn = ds_ref[...]
            blast = cum_ref[pl.ds(base + CHUNK - 1, 1), :]
            eq0 = jnp.exp(cum)
            ek0 = jnp.exp(blast - cum)
            dqh = _bdot(do, sprev, NN_DIMS) * eq0
            dkk = _bdot(v, dsn, NN_DIMS) * ek0
            dlast = (jnp.sum(kk * dkk, axis=0, keepdims=True)
                     + jnp.exp(blast) * jnp.sum(dsn * sprev, axis=0, keepdims=True))
            dv = _bdot(kk * ek0, dsn, NT_DIMS)
            ds_ref[...] = dsn * jnp.exp(blast) + _bdot(do, qh * eq0, TN_DIMS)
            dg = qh * dqh - kk * dkk
            da = _bdot(do, v, NT_DIMS)
            a = None
            for (eq, ek), msk in zip(_level_factors(cum, _level_refs(cum_ref, rid, base), second), same):
                ql, kl = (qh * eq).astype(BF16), (kk * ek).astype(BF16)
                al = _bdot(ql, kl, NT_DIMS)
                dal = da
                if msk is not None:
                    al = jnp.where(msk, al, 0.0)
                    dal = jnp.where(msk, da, 0.0)
                a = al if a is None else a + al
                dql = _bdot(dal, kl, NN_DIMS)
                dkl = _bdot(dal, ql, TN_DIMS)
                dqh = dqh + dql * eq
                dkk = dkk + dkl * ek
                dg = dg + (ql.astype(F32) * dql - kl.astype(F32) * dkl)
            dv = dv + _bdot(a, do, TN_DIMS)
            dq_d, dk_d, dv_d = [], [], []
            for sb in range(CHUNK // SUB):
                lo = sb * SUB
                qb = qh[lo:lo + SUB]
                cb = cum[lo:lo + SUB]
                dob = do[lo:lo + SUB]
                dqb = jnp.zeros((SUB, c), F32)
                dkb = jnp.zeros((SUB, c), F32)
                dvb = jnp.zeros((SUB, c), F32)
                for s in range(SUB):
                    e = jnp.where(sub >= s, jnp.exp(jnp.minimum(cb - cum_ref[pl.ds(base + lo + s, 1), :], 0.0)), 0.0)
                    ks = kk_ref[pl.ds(base + lo + s, 1), :]
                    qe = qb * e
                    dacol = jnp.sum(dob * vc_ref[pl.ds(base + lo + s, 1), :], axis=-1, keepdims=True)
                    acol = jnp.sum(qe * ks, axis=-1, keepdims=True)
                    dqb = dqb + dacol * (ks * e)
                    dkb = jnp.where(sub == s, jnp.sum(dacol * qe, axis=0, keepdims=True), dkb)
                    dvb = jnp.where(sub == s, jnp.sum(acol * dob, axis=0, keepdims=True), dvb)
                dq_d.append(dqb)
                dk_d.append(dkb)
                dv_d.append(dvb)
            dq_d = jnp.concatenate(dq_d, axis=0)
            dk_d = jnp.concatenate(dk_d, axis=0)
            dqh = dqh + dq_d
            dkk = dkk + dk_d
            dg = dg + (qh * dq_d - kk * dk_d)
            dv = dv + jnp.concatenate(dv_d, axis=0)
            dlf = _hdot(trit, dg) + dlast
            dfg = dlf / fg - dkk
            df_ref[rows, :] = (dfg * (1.0 - lb) * sf * (1.0 - sf)).astype(BF16)
            dlb = dlb + jnp.sum(dfg * (1.0 - sf), axis=0, keepdims=True)
            dq_ref[rows, :] = (dqh * _silu_grad(q, sq)).astype(BF16)
            dv_ref[rows, :] = dv.astype(BF16)
            return dhg, dlb

        def chunks(i, carry):
            for u in range(HGRN_UNROLL):
                carry = chunk(i * HGRN_UNROLL + u, carry, u)
            return carry

        zero = jnp.zeros((1, c), F32)
        dhg, dlb = lax.fori_loop(0, nc // HGRN_UNROLL, chunks, (zero, zero))
        dhg_ref[...] = dhg
        dl0 = dlb * lb * (1.0 - lb)
        dlb_ref[...] = jnp.where(lax.broadcasted_iota(jnp.int32, (2, c), 0) == 0, dl0, -dl0)

    big = jax.ShapeDtypeStruct((t, HGRN_WIDTH), BF16)
    return pl.pallas_call(
        body, name=name, grid=(N_GROUPS,),
        in_specs=[col(8)] + hin_specs + [col(0), st, lbs, vec],
        out_specs=[col(0)] * 4 + [vec, lbs],
        out_shape=[big] * 4 + [jax.ShapeDtypeStruct((1, HGRN_WIDTH), F32), jax.ShapeDtypeStruct((2, HGRN_WIDTH), F32)],
        scratch_shapes=[pltpu.VMEM((c, c), F32)] + [pltpu.VMEM((HGRN_UNROLL * CHUNK, c), F32)] * 3,
        compiler_params=_cparams(("parallel",)),
    )(dcat, hin, hin, hin, hin, o_raw, states, lb_logits, hg)


ANY = pl.BlockSpec(memory_space=pl.ANY)


def _my_place():
    return lax.axis_index("x"), lax.axis_index("y"), lax.axis_index("c")


HBM = pl.BlockSpec(memory_space=pltpu.HBM)
SEM = pl.BlockSpec(memory_space=pltpu.SEMAPHORE)
EFFECT = pltpu.SideEffectType.DATAFLOW_SIDE_EFFECTING


def _peer(k):
    x, y, c = _my_place()
    px = 1 - x if k & 4 else x
    py = 1 - y if k & 2 else y
    pc = 1 - c if k & 1 else c
    return (px, py, pc), 4 * px + 2 * py + pc


def _exchange_copy(k, src_ref, land_ref, send_sems, recv_sems, scatter, landing):
    x, y, c = _my_place()
    me = 4 * x + 2 * y + c
    to, idx = _peer(k)
    return pltpu.make_async_remote_copy(
        src_ref=src_ref.at[idx] if scatter else src_ref,
        dst_ref=land_ref.at[idx] if landing else land_ref.at[me],
        send_sem=send_sems.at[k - 1], recv_sem=recv_sems.at[k - 1], device_id=to, device_id_type=MESH)


ALL_PEERS = tuple(range(1, N_DEV))
NEAR_PEERS = (1, 2, 4, 6)
SAME_CORE_PEERS = (2, 4, 6)


def _exchange_start(name, src, land, scatter, ks=ALL_PEERS):
    def body(src_ref, land_ref, send_sems, recv_sems, src_thru, land_thru, token):
        for k in ks:
            _exchange_copy(k, src_ref, land_ref, send_sems, recv_sems, scatter, landing=False).start()
        token[...] = jnp.zeros_like(token)

    send_sems, recv_sems, src_thru, land_thru, token = pl.pallas_call(
        body, name=name,
        out_shape=(pltpu.SemaphoreType.DMA((N_DEV - 1,)), pltpu.SemaphoreType.DMA((N_DEV - 1,)),
                   pltpu.HBM(src.shape, src.dtype), pltpu.HBM(land.shape, land.dtype),
                   jax.ShapeDtypeStruct((8, 128), F32)),
        in_specs=(HBM, HBM), out_specs=(SEM, SEM, HBM, HBM, pl.BlockSpec(memory_space=pltpu.VMEM)),
        input_output_aliases={0: 2, 1: 3},
        compiler_params=pltpu.CompilerParams(has_side_effects=EFFECT),
    )(pltpu.with_memory_space_constraint(src, pltpu.HBM), pltpu.with_memory_space_constraint(land, pltpu.HBM))
    return (send_sems, recv_sems, src_thru, land_thru, scatter, ks), token


def _exchange_wait(name, handle, after):
    send_sems, recv_sems, src_thru, land_thru, scatter, ks = handle

    def body(src_ref, land_ref, send_sems, recv_sems, after_ref, src_dead, got_ref):
        for k in ks:
            cp = _exchange_copy(k, src_ref, land_ref, send_sems, recv_sems, scatter, landing=True)
            cp.wait_send()
            cp.wait_recv()

    return pl.pallas_call(
        body, name=name,
        out_shape=(pltpu.HBM(src_thru.shape, src_thru.dtype), pltpu.HBM(land_thru.shape, land_thru.dtype)),
        in_specs=(HBM, HBM, SEM, SEM, ANY), out_specs=(HBM, HBM), input_output_aliases={0: 0, 1: 1},
        compiler_params=pltpu.CompilerParams(has_side_effects=EFFECT),
    )(src_thru, land_thru, send_sems, recv_sems, after)[1]


def _relay_copy(j, land_ref, send_sems, recv_sems, landing):
    x, y, c = _my_place()
    k = SAME_CORE_PEERS[j]
    _, sent = _peer(k)
    _, got = _peer(k + 1)
    return pltpu.make_async_remote_copy(
        src_ref=land_ref.at[sent], dst_ref=land_ref.at[got] if landing else land_ref.at[sent],
        send_sem=send_sems.at[j], recv_sem=recv_sems.at[j], device_id=(x, y, 1 - c), device_id_type=MESH)


def _relay_start(name, land):
    n = len(SAME_CORE_PEERS)

    def body(land_ref, send_sems, recv_sems, land_thru, token):
        for j in range(n):
            _relay_copy(j, land_ref, send_sems, recv_sems, landing=False).start()
        token[...] = jnp.zeros_like(token)

    send_sems, recv_sems, land_thru, token = pl.pallas_call(
        body, name=name,
        out_shape=(pltpu.SemaphoreType.DMA((n,)), pltpu.SemaphoreType.DMA((n,)),
                   pltpu.HBM(land.shape, land.dtype), jax.ShapeDtypeStruct((8, 128), F32)),
        in_specs=(HBM,), out_specs=(SEM, SEM, HBM, pl.BlockSpec(memory_space=pltpu.VMEM)),
        input_output_aliases={0: 2},
        compiler_params=pltpu.CompilerParams(has_side_effects=EFFECT),
    )(pltpu.with_memory_space_constraint(land, pltpu.HBM))
    return (send_sems, recv_sems, land_thru), token


def _relay_wait(name, handle, after):
    send_sems, recv_sems, land_thru = handle

    def body(land_ref, send_sems, recv_sems, after_ref, got_ref):
        for j in range(len(SAME_CORE_PEERS)):
            cp = _relay_copy(j, land_ref, send_sems, recv_sems, landing=True)
            cp.wait_send()
            cp.wait_recv()

    return pl.pallas_call(
        body, name=name, out_shape=pltpu.HBM(land_thru.shape, land_thru.dtype),
        in_specs=(HBM, SEM, SEM, ANY), out_specs=HBM, input_output_aliases={0: 0},
        compiler_params=pltpu.CompilerParams(has_side_effects=EFFECT),
    )(land_thru, send_sems, recv_sems, after)


def _own_slot(own, me):
    land = lax.empty((N_DEV,) + own.shape, own.dtype)
    return lax.dynamic_update_slice_in_dim(land, own[None], me, axis=0)


def _adamw_math(w, g, m, v):
    m = ADAM_B1 * m + (1.0 - ADAM_B1) * g
    v = ADAM_B2 * v + (1.0 - ADAM_B2) * (g * g)
    m_hat = m / (1.0 - ADAM_B1 ** ADAM_STEP)
    v_hat = v / (1.0 - ADAM_B2 ** ADAM_STEP)
    delta = -ADAM_LR * (m_hat / (jnp.sqrt(v_hat) + ADAM_EPS) + ADAM_WD * w)
    return delta, m, v


def _adamw_sum(name, recv, w, m, v, tr, row0=0, partial=None):
    r, c = w.shape
    rr = recv.shape[1]
    off = row0 // tr

    def body(recv_ref, w_ref, m_ref, v_ref, *refs):
        g_ref, d_ref, mo_ref, vo_ref = refs[-4:]
        g = recv_ref[0].astype(F32)
        for j in range(1, N_DEV):
            g = g + recv_ref[j].astype(F32)
        g_ref[...] = g
        d_ref[...], mo_ref[...], vo_ref[...] = _adamw_math(w_ref[...], g, m_ref[...], v_ref[...])

    tile = pl.BlockSpec((tr, c), lambda i: (i + off, 0))
    out = jax.ShapeDtypeStruct((r, c), F32)
    prev = list(partial) if partial is not None else []
    return pl.pallas_call(
        body, name=name, grid=(rr // tr,),
        in_specs=[pl.BlockSpec((N_DEV, tr, c), lambda i: (0, i, 0)), tile, tile, tile] + [ANY] * len(prev),
        out_specs=[tile] * 4, out_shape=[out] * 4,
        input_output_aliases={4 + i: i for i in range(len(prev))},
        compiler_params=_cparams(("parallel",)),
    )(recv, w, m, v, *prev)


def _sum_parts(name, parts):
    _, r, c = parts.shape

    def body(p_ref, o_ref):
        acc = p_ref[0]
        for j in range(1, N_DEV):
            acc = acc + p_ref[j]
        o_ref[...] = acc

    return pl.pallas_call(body, name=name, out_shape=jax.ShapeDtypeStruct((r, c), F32),
                          compiler_params=_cparams())(parts)


def _adamw_small(name, w, g, m, v):
    def body(w_ref, g_ref, m_ref, v_ref, d_ref, mo_ref, vo_ref):
        d_ref[...], mo_ref[...], vo_ref[...] = _adamw_math(w_ref[...], g_ref[...], m_ref[...], v_ref[...])

    out = jax.ShapeDtypeStruct(w.shape, F32)
    return pl.pallas_call(body, name=name, out_shape=[out] * 3, compiler_params=_cparams())(w, g, m, v)


def _pack(pieces, rows):
    flat = jnp.concatenate([p.reshape(-1).astype(F32) for p in pieces])
    return jnp.pad(flat, (0, rows * 128 - flat.shape[0])).reshape(rows, 128)


def _unpack(packed, shapes):
    flat = packed.reshape(-1)
    out, off = [], 0
    for s in shapes:
        n = 1
        for d in s:
            n *= d
        out.append(flat[off:off + n].reshape(s))
        off += n
    return out


def kernel(x, emb_ln_g, emb_ln_b, w_in, conv_w, conv_b, conv_norm_g, conv_norm_b, lb_logits, hgrn_norm_g, w_out, ln1_g, ln1_b, w_ffn_up, ffn_conv_w, ffn_conv_b, w_ffn_down, ln2_g, ln2_b, loss_target, m_emb_ln_g, m_emb_ln_b, m_w_in, m_conv_w, m_conv_b, m_conv_norm_g, m_conv_norm_b, m_lb_logits, m_hgrn_norm_g, m_w_out, m_ln1_g, m_ln1_b, m_w_ffn_up, m_ffn_conv_w, m_ffn_conv_b, m_w_ffn_down, m_ln2_g, m_ln2_b, v_emb_ln_g, v_emb_ln_b, v_w_in, v_conv_w, v_conv_b, v_conv_norm_g, v_conv_norm_b, v_lb_logits, v_hgrn_norm_g, v_w_out, v_ln1_g, v_ln1_b, v_w_ffn_up, v_ffn_conv_w, v_ffn_conv_b, v_w_ffn_down, v_ln2_g, v_ln2_b):
    t = x.shape[1]
    me = 4 * lax.axis_index("x") + 2 * lax.axis_index("y") + lax.axis_index("c")
    x2, tgt = x[0], loss_target[0]
    ns_in, ns_up = w_in.shape[2], w_ffn_up.shape[2]
    rs_out, rs_down = w_out.shape[1], w_ffn_down.shape[1]
    cs, fs = conv_w.shape[2], ffn_conv_w.shape[2]

    def gather_start(name, w, prev, ks=ALL_PEERS):
        shard = (w[0] + prev).astype(BF16)
        return _exchange_start(name, shard, _own_slot(shard, me), scatter=False, ks=ks)

    h_in, tok = gather_start("ag_w_in_start", w_in, 0.0, NEAR_PEERS)
    taps = _pack([conv_w[0], ffn_conv_w[0]], 48) + tok[0, 0]
    h_taps, tok = _exchange_start("ag_taps_start", taps, _own_slot(taps, me), scatter=False)
    h_out, tok = gather_start("ag_w_out_start", w_out, tok[0, 0])
    h_up, tok = gather_start("ag_w_up_start", w_ffn_up, tok[0, 0])
    h_down, tok = gather_start("ag_w_down_start", w_ffn_down, tok[0, 0])

    row = lambda a: a.reshape(1, -1)

    _, h0, h0b, h0bt = _ln_fwd("ln_in", x2, None, row(emb_ln_g) + tok[0, 0], row(emb_ln_b), 1.0)
    h_relay, tok_relay = _relay_start("ag_w_in_relay_start", _exchange_wait("ag_w_in_wait", h_in, h0b))
    win_g = _relay_wait("ag_w_in_relay_wait", h_relay, tok_relay)
    win_n = win_g.transpose(1, 0, 2).reshape(D_MODEL, IN_PROJ)
    hin = _mm_nn("mm_in", h0b, win_n, F32, tm=1024, tn=ns_in, tk=D_MODEL)
    n_cw, n_fw = CONV_KERNEL * cs, FFN_KERNEL * fs
    taps_g = _exchange_wait("ag_taps_wait", h_taps, hin).reshape(N_DEV, -1)
    cw_full = taps_g[:, :n_cw].reshape(N_DEV, CONV_KERNEL, cs).transpose(1, 0, 2).reshape(CONV_KERNEL, CONV_WIDTH)
    fw_full = taps_g[:, n_cw:n_cw + n_fw].reshape(N_DEV, FFN_KERNEL, fs).transpose(1, 0, 2).reshape(FFN_KERNEL, D_FF)

    o_raw, cat_right, states = _hgrn_fwd("hgrn_fwd", hin, lb_logits, hgrn_norm_g)
    u1, catb = _conv_fwd("conv_fwd", hin, cw_full, conv_b, conv_norm_g, conv_norm_b, cat_right)
    wout_g = _exchange_wait("ag_w_out_wait", h_out, catb).reshape(D_MODEL, D_MODEL)
    mix = _mm_nn("mm_out", catb, wout_g, F32, tm=1024, tn=1024, tk=D_MODEL)
    r1, h1, h1b, h1bt = _ln_fwd("ln1", h0, mix, ln1_g, ln1_b, ALPHA)
    wup_g = _exchange_wait("ag_w_up_wait", h_up, h1b)
    wup_n = wup_g.transpose(1, 0, 2).reshape(D_MODEL, 2 * D_FF)
    hf = _mm_nn("mm_up", h1b, wup_n, BF16, tm=1024, tn=1024, tk=D_MODEL)
    actb = _ffn_act_fwd("ffn_act", hf, fw_full, ffn_conv_b)
    wdown_g = _exchange_wait("ag_w_down_wait", h_down, actb).reshape(D_FF, D_MODEL)
    ffn = _mm_nn("mm_down", actb, wdown_g, F32, tm=512, tn=1024, tk=D_FF)
    dr2, dr2b, g_ln2g, g_ln2b, loss = _ln2_loss_bwd("ln2_loss", h1, ffn, ln2_g, ln2_b, tgt)

    def scatter_start(name, parts):
        own = lax.dynamic_index_in_dim(parts, me, axis=0, keepdims=False)
        return _exchange_start(name, parts, _own_slot(own, me), scatter=True)

    dact = _mm_nt("mm_dact", dr2b, wdown_g, BF16, tm=1024, tn=D_FF // 2, tk=D_MODEL)
    gw_down = _mm_nn("mm_dw_down", actb.T, dr2b, BF16, tm=rs_down, tn=1024, tk=t)
    s_down, tok = scatter_start("a2a_w_down_start", gw_down.reshape(N_DEV, rs_down, D_MODEL))
    dhf, g_fw, g_fb = _ffn_act_bwd("ffn_act_bwd", dact, hf, fw_full, ffn_conv_b + tok[0, 0])
    tm = min(1024, t)
    gw_up = _matmul(
        "mm_dw_up", h1bt, dhf, (N_DEV, D_MODEL, ns_up), BF16, (D_MODEL // 1024, N_DEV, 1),
        pl.BlockSpec((1024, t), lambda i, j, kk: (i, 0)),
        pl.BlockSpec((1, t, ns_up), lambda i, j, kk: (j // 4, 0, j % 4)),
        pl.BlockSpec((1, 1024, ns_up), lambda i, j, kk: (j, i, 0)), nt=False)
    s_up, tok = scatter_start("a2a_w_up_start", gw_up)
    tkf = D_FF // 2
    dh1 = _matmul(
        "mm_dh1", dhf, wup_n, (t, D_MODEL), F32, (t // tm, D_MODEL // 1024, 4),
        pl.BlockSpec((1, tm, tkf), lambda i, j, kk: (kk // 2, i, kk % 2)),
        pl.BlockSpec((1024, tkf), lambda i, j, kk: (j, kk)),
        pl.BlockSpec((tm, 1024), lambda i, j, kk: (i, j)), nt=True, after=tok)
    dr1, dr1b, g_ln1g, g_ln1b = _ln_bwd("ln1_bwd", r1, dr2, dh1, ln1_g + tok[0, 0], ALPHA, True)
    gw_out = _mm_nn("mm_dw_out", catb.T, dr1b, BF16, tm=1024, tn=1024, tk=t)
    s_out, tok = scatter_start("a2a_w_out_start", gw_out.reshape(N_DEV, rs_out, D_MODEL))
    dcat = _mm_nt("mm_dcat", dr1b, wout_g, F32, tm=1024, tn=1024, tk=D_MODEL, after=tok)
    da, dgate, g_cw, g_cb, g_cng, g_cnb = _conv_bwd("conv_bwd", dcat, u1, hin, cw_full, conv_norm_g + tok[0, 0],
                                                    conv_norm_b)
    dq, df, di, dog, g_hg, g_lb = _hgrn_bwd("hgrn_bwd", dcat, hin, o_raw, states, lb_logits, hgrn_norm_g)
    dhin = jnp.concatenate([da, dgate, dq, df, di, dog], axis=1)
    dh0 = _mm_nt("mm_dh0", dhin, win_n, F32, tm=1024, tn=1024, tk=IN_PROJ // 2)
    grad_x, g_eg, g_eb = _ln_bwd("ln_in_bwd", x2, dr1, dh0, row(emb_ln_g), ALPHA, False)

    small_shapes = [(D_MODEL,), (D_MODEL,), (CONV_KERNEL, CONV_WIDTH), (1, CONV_WIDTH), (1, CONV_WIDTH),
                    (1, CONV_WIDTH), (2, HGRN_WIDTH), (1, HGRN_WIDTH), (1, D_MODEL), (1, D_MODEL),
                    (FFN_KERNEL, D_FF), (1, D_FF), (1, D_MODEL), (1, D_MODEL), (128,)]
    rows_small = 569
    packed = _pack([g_eg, g_eb, g_cw[:CONV_KERNEL], g_cb, g_cng, g_cnb, g_lb, g_hg, g_ln1g, g_ln1b,
                    g_fw[:FFN_KERNEL], g_fb, g_ln2g, g_ln2b, loss], rows_small)
    h_small, tok = _exchange_start("ag_small_start", packed, _own_slot(packed, me), scatter=False)
    half = D_MODEL // 2
    gw_in_a = _mm_grad_cols("mm_dw_in_a", h0bt, dhin, ns_in, 0, half, after=tok)
    s_in_a, tok = scatter_start("a2a_w_in_a_start", gw_in_a)
    gw_in_b = _mm_grad_cols("mm_dw_in_b", h0bt, dhin, ns_in, half, half, after=tok)
    s_in_b, tok = scatter_start("a2a_w_in_b_start", gw_in_b)
    summed = _sum_parts("sum_small", _exchange_wait("ag_small_wait", h_small, tok))
    (s_eg, s_eb, s_cw, s_cb, s_cng, s_cnb, s_lb, s_hg, s_l1g, s_l1b, s_fw, s_fb, s_l2g, s_l2b,
     s_loss) = _unpack(summed, small_shapes)
    s_cw = lax.dynamic_slice_in_dim(s_cw, me * cs, cs, axis=1)[None]
    s_fw = lax.dynamic_slice_in_dim(s_fw, me * fs, fs, axis=1)[None]
    g_small = [s_eg, s_eb, s_cw, s_cb, s_cng, s_cnb, s_lb, s_hg, s_l1g, s_l1b, s_fw, s_fb, s_l2g, s_l2b]
    w_small = [emb_ln_g, emb_ln_b, conv_w, conv_b, conv_norm_g, conv_norm_b, lb_logits, hgrn_norm_g,
               ln1_g, ln1_b, ffn_conv_w, ffn_conv_b, ln2_g, ln2_b]
    m_small = [m_emb_ln_g, m_emb_ln_b, m_conv_w, m_conv_b, m_conv_norm_g, m_conv_norm_b, m_lb_logits,
               m_hgrn_norm_g, m_ln1_g, m_ln1_b, m_ffn_conv_w, m_ffn_conv_b, m_ln2_g, m_ln2_b]
    v_small = [v_emb_ln_g, v_emb_ln_b, v_conv_w, v_conv_b, v_conv_norm_g, v_conv_norm_b, v_lb_logits,
               v_hgrn_norm_g, v_ln1_g, v_ln1_b, v_ffn_conv_w, v_ffn_conv_b, v_ln2_g, v_ln2_b]
    rows_own = 236
    shapes_own = [w.shape for w in w_small]
    upd = _adamw_small("adamw_small", _pack(w_small, rows_own), _pack(g_small, rows_own),
                       _pack(m_small, rows_own), _pack(v_small, rows_own))
    d_small, nm_small, nv_small = (_unpack(u, shapes_own) for u in upd)
    g_small = [g.reshape(s) for g, s in zip(g_small, shapes_own)]

    def big(name, handle, after, w, m, v, tr):
        recv = _exchange_wait("a2a_" + name + "_wait", handle, after)
        return [o[None] for o in _adamw_sum("adamw_" + name, recv, w[0], m[0], v[0], tr)]

    u_down = big("w_down", s_down, upd[0], w_ffn_down, m_w_ffn_down, v_w_ffn_down, 64)
    u_up = big("w_up", s_up, u_down[1], w_ffn_up, m_w_ffn_up, v_w_ffn_up, 64)
    u_out = big("w_out", s_out, u_up[1], w_out, m_w_out, v_w_out, 64)
    recv_a = _exchange_wait("a2a_w_in_a_wait", s_in_a, u_out[1])
    part = _adamw_sum("adamw_w_in_a", recv_a, w_in[0], m_w_in[0], v_w_in[0], 128)
    recv_b = _exchange_wait("a2a_w_in_b_wait", s_in_b, part[1])
    u_in = [o[None] for o in _adamw_sum("adamw_w_in_b", recv_b, w_in[0], m_w_in[0], v_w_in[0], 128,
                                        row0=half, partial=part)]

    def ordered(small, i_in, i_out, i_up, i_down):
        (eg, eb, cw, cb, cng, cnb, lb, hg, l1g, l1b, fw, fb, l2g, l2b) = small
        return [eg, eb, i_in, cw, cb, cng, cnb, lb, hg, i_out, l1g, l1b, i_up, fw, fb, i_down, l2g, l2b]

    outs = [s_loss[0], grad_x[None]]
    for k, small in enumerate([g_small, d_small, nm_small, nv_small]):
        outs += ordered(small, u_in[k], u_out[k], u_up[k], u_down[k])
    return tuple(outs)
```

```python
import functools

import jax
import jax.numpy as jnp
from jax import lax
from jax.experimental import pallas as pl
from jax.experimental.pallas import tpu as pltpu

F32 = jnp.float32
BF16 = jnp.bfloat16

N_DEV = 8
D_MODEL = 2048
CONV_WIDTH = 1024
CONV_KERNEL = 31
HGRN_WIDTH = 1024
GROUP = 128
N_GROUPS = 8
IN_PROJ = 2 * CONV_WIDTH + 4 * HGRN_WIDTH
D_FF = 5632
FFN_KERNEL = 3
CHUNK = 64
SUB = 8
LN_EPS = 1e-5
RMS_EPS = 1e-6
ALPHA = 2.0 ** 0.25
ADAM_LR, ADAM_B1, ADAM_B2, ADAM_EPS, ADAM_WD, ADAM_STEP = 0.001, 0.9, 0.999, 1e-08, 0.01, 10

VMEM_LIMIT = 56 * 1024 * 1024
MESH = pl.DeviceIdType.MESH


def _cparams(sem=None):
    return pltpu.CompilerParams(dimension_semantics=sem, vmem_limit_bytes=VMEM_LIMIT)


def _sigmoid(x):
    return 1.0 / (1.0 + jnp.exp(-x))


def _matmul(name, a, b, out_shape, out_dtype, grid, a_spec, b_spec, o_spec, nt, after=None):
    nk = grid[2]
    dims = (((1,), (1,)), ((), ())) if nt else (((1,), (0,)), ((), ()))
    extra = [] if after is None else [after]

    def body(a_ref, b_ref, *rest):
        o_ref, *scratch = rest[len(extra):]
        av = a_ref[0] if len(a_ref.shape) == 3 else a_ref[...]
        bv = b_ref[0] if len(b_ref.shape) == 3 else b_ref[...]
        part = lax.dot_general(av, bv, dims, preferred_element_type=F32)

        def write(res):
            if len(o_ref.shape) == 3:
                o_ref[0] = res.astype(out_dtype)
            else:
                o_ref[...] = res.astype(out_dtype)

        if nk == 1:
            write(part)
            return
        acc_ref, = scratch
        k = pl.program_id(2)

        @pl.when(k == 0)
        def _():
            acc_ref[...] = part

        @pl.when(jnp.logical_and(k > 0, k < nk - 1))
        def _():
            acc_ref[...] += part

        @pl.when(k == nk - 1)
        def _():
            write(acc_ref[...] + part)

    acc_shape = o_spec.block_shape[-2:]
    assert all(g >= 1 for g in grid), (name, grid)
    return pl.pallas_call(
        body, name=name, grid=grid, in_specs=[a_spec, b_spec] + [pl.BlockSpec(memory_space=pl.ANY)] * len(extra),
        out_specs=o_spec, out_shape=jax.ShapeDtypeStruct(out_shape, out_dtype),
        scratch_shapes=[pltpu.VMEM(acc_shape, F32)] if nk > 1 else [],
        compiler_params=_cparams(("parallel", "parallel", "arbitrary")),
    )(a, b, *extra)


def _mm_nn(name, a, w, out_dtype, tm, tn, tk, after=None):
    m, k = a.shape
    tm, tk = min(tm, m), min(tk, k)
    n = w.shape[1]
    return _matmul(
        name, a, w, (m, n), out_dtype, (m // tm, n // tn, k // tk),
        pl.BlockSpec((tm, tk), lambda i, j, kk: (i, kk)),
        pl.BlockSpec((tk, tn), lambda i, j, kk: (kk, j)),
        pl.BlockSpec((tm, tn), lambda i, j, kk: (i, j)), nt=False, after=after)


def _mm_nt(name, a, w, out_dtype, tm, tn, tk, after=None):
    m, k = a.shape
    tm = min(tm, m)
    n = w.shape[0]
    return _matmul(
        name, a, w, (m, n), out_dtype, (m // tm, n // tn, k // tk),
        pl.BlockSpec((tm, tk), lambda i, j, kk: (i, kk)),
        pl.BlockSpec((tn, tk), lambda i, j, kk: (j, kk)),
        pl.BlockSpec((tm, tn), lambda i, j, kk: (i, j)), nt=True, after=after)


def _mm_grad_cols(name, at, b, ns, row0, rows, after, tm=1024, tk=4096):
    t = at.shape[1]
    tk = min(tk, t)
    off = row0 // tm
    return _matmul(
        name, at, b, (N_DEV, rows, ns), BF16, (rows // tm, N_DEV, t // tk),
        pl.BlockSpec((tm, tk), lambda i, j, kk: (i + off, kk)),
        pl.BlockSpec((tk, ns), lambda i, j, kk: (kk, j)),
        pl.BlockSpec((1, tm, ns), lambda i, j, kk: (j, i, 0)), nt=False, after=after)


LN_ROWS = 256


def _ln_stats(r):
    mu = jnp.mean(r, axis=-1, keepdims=True)
    xc = r - mu
    var = jnp.mean(xc * xc, axis=-1, keepdims=True)
    rstd = lax.rsqrt(var + LN_EPS)
    return xc * rstd, rstd


def _row_spec(d):
    return pl.BlockSpec((LN_ROWS, d), lambda i: (i, 0))


def _vec_spec(d):
    return pl.BlockSpec((1, d), lambda i: (0, 0))


def _ln_fwd(name, a, m, g, b, alpha):
    t, d = a.shape
    has_m = m is not None

    def body(*refs):
        if has_m:
            a_ref, m_ref, g_ref, b_ref, r_ref, y_ref, yb_ref, yt_ref = refs
            r = alpha * a_ref[...] + m_ref[...]
            r_ref[...] = r
        else:
            a_ref, g_ref, b_ref, y_ref, yb_ref, yt_ref = refs
            r = a_ref[...]
        xhat, _ = _ln_stats(r)
        y = xhat * g_ref[...] + b_ref[...]
        y_ref[...] = y
        yb_ref[...] = y.astype(BF16)
        yt_ref[...] = y.T.astype(BF16)

    ins = [a] + ([m] if has_m else []) + [g, b]
    in_specs = [_row_spec(d)] * (2 if has_m else 1) + [_vec_spec(d)] * 2
    outs = ([jax.ShapeDtypeStruct((t, d), F32)] if has_m else []) + [
        jax.ShapeDtypeStruct((t, d), F32), jax.ShapeDtypeStruct((t, d), BF16), jax.ShapeDtypeStruct((d, t), BF16)]
    res = pl.pallas_call(
        body, name=name, grid=(t // LN_ROWS,), in_specs=in_specs,
        out_specs=[_row_spec(d)] * (len(outs) - 1) + [pl.BlockSpec((d, LN_ROWS), lambda i: (0, i))], out_shape=outs,
        compiler_params=_cparams(("parallel",)),
    )(*ins)
    return res if has_m else (None, *res)


def _ln_bwd_math(r, dy, g):
    xhat, rstd = _ln_stats(r)
    dxhat = dy * g
    m1 = jnp.mean(dxhat, axis=-1, keepdims=True)
    m2 = jnp.mean(dxhat * xhat, axis=-1, keepdims=True)
    dr = rstd * (dxhat - m1 - xhat * m2)
    return dr, jnp.sum(dy * xhat, axis=0, keepdims=True), jnp.sum(dy, axis=0, keepdims=True)


def _ln2_loss_bwd(name, h1, ffn, g, b, tgt):
    t, d = h1.shape

    def body(h1_ref, f_ref, g_ref, b_ref, t_ref, dr_ref, drb_ref, dg_ref, db_ref, loss_ref):
        @pl.when(pl.program_id(0) == 0)
        def _():
            dg_ref[...] = jnp.zeros_like(dg_ref)
            db_ref[...] = jnp.zeros_like(db_ref)
            loss_ref[...] = jnp.zeros_like(loss_ref)

        r = ALPHA * h1_ref[...] + f_ref[...]
        xhat, _ = _ln_stats(r)
        e = xhat * g_ref[...] + b_ref[...] - t_ref[...]
        loss_ref[...] += 0.5 / d * jnp.sum(e * e)
        dr, dg, db = _ln_bwd_math(r, e * (1.0 / d), g_ref[...])
        dr_ref[...] = dr
        drb_ref[...] = dr.astype(BF16)
        dg_ref[...] += dg
        db_ref[...] += db

    return pl.pallas_call(
        body, name=name, grid=(t // LN_ROWS,),
        in_specs=[_row_spec(d), _row_spec(d), _vec_spec(d), _vec_spec(d), _row_spec(d)],
        out_specs=[_row_spec(d), _row_spec(d), _vec_spec(d), _vec_spec(d), _vec_spec(128)],
        out_shape=[jax.ShapeDtypeStruct((t, d), F32), jax.ShapeDtypeStruct((t, d), BF16),
                   jax.ShapeDtypeStruct((1, d), F32), jax.ShapeDtypeStruct((1, d), F32),
                   jax.ShapeDtypeStruct((1, 128), F32)],
        compiler_params=_cparams(("arbitrary",)),
    )(h1, ffn, g, b, tgt)


def _ln_bwd(name, r, dya, dyb, g, alpha, want_bf16):
    t, d = r.shape

    def body(r_ref, dya_ref, dyb_ref, g_ref, *outs):
        dr_ref = outs[0]
        dg_ref, db_ref = outs[-2:]

        @pl.when(pl.program_id(0) == 0)
        def _():
            dg_ref[...] = jnp.zeros_like(dg_ref)
            db_ref[...] = jnp.zeros_like(db_ref)

        dy = alpha * dya_ref[...] + dyb_ref[...]
        dr, dg, db = _ln_bwd_math(r_ref[...], dy, g_ref[...])
        dr_ref[...] = dr
        if want_bf16:
            outs[1][...] = dr.astype(BF16)
        dg_ref[...] += dg
        db_ref[...] += db

    big = [jax.ShapeDtypeStruct((t, d), F32)] + ([jax.ShapeDtypeStruct((t, d), BF16)] if want_bf16 else [])
    return pl.pallas_call(
        body, name=name, grid=(t // LN_ROWS,),
        in_specs=[_row_spec(d)] * 3 + [_vec_spec(d)],
        out_specs=[_row_spec(d)] * len(big) + [_vec_spec(d)] * 2,
        out_shape=big + [jax.ShapeDtypeStruct((1, d), F32)] * 2,
        compiler_params=_cparams(("arbitrary",)),
    )(r, dya, dyb, g)


CONV_ROWS = 64
CONV_UNROLL = 4
FFN_UNROLL = 2


def _unrolled(n, unroll, fn, init):
    def body(i, carry):
        for u in range(unroll):
            carry = fn(i * unroll + u, carry)
        return carry

    return lax.fori_loop(0, n // unroll, body, init)


def _for_shifted(ref, r0, tm, shifts, fn):
    for s in shifts:
        fn(s, ref[pl.ds(r0 + s, tm), :])


def _col_spec(t, cb, off=0):
    return pl.BlockSpec((t, cb), lambda j: (0, j + off))


def _ffn_act_fwd(name, hf, w, b, cb=128):
    t = hf.shape[0]
    f = hf.shape[1] // 2
    nb = f // cb
    tm = CONV_ROWS

    def body(g_ref, v_ref, w_ref, b_ref, act_ref, pad_ref):
        pad_ref[pl.ds(0, 8), :] = jnp.zeros((8, cb), F32)
        pad_ref[pl.ds(8, t), :] = g_ref[...].astype(F32)
        wv = [w_ref[pl.ds(k, 1), :] for k in range(FFN_KERNEL)]
        bias = b_ref[...]

        def tile(i, carry):
            r0 = pl.multiple_of(i * tm, tm)
            acc = [jnp.broadcast_to(bias, (tm, cb))]

            def tap(s, rows):
                acc[0] = acc[0] + wv[s - 6] * rows

            _for_shifted(pad_ref, r0, tm, (6, 7, 8), tap)
            gc = acc[0]
            act_ref[pl.ds(r0, tm), :] = (gc * _sigmoid(gc) * v_ref[pl.ds(r0, tm), :].astype(F32)).astype(BF16)
            return carry

        _unrolled(t // tm, FFN_UNROLL, tile, 0)

    return pl.pallas_call(
        body, name=name, grid=(nb,),
        in_specs=[_col_spec(t, cb), _col_spec(t, cb, nb),
                  pl.BlockSpec((FFN_KERNEL, cb), lambda j: (0, j)), pl.BlockSpec((1, cb), lambda j: (0, j))],
        out_specs=_col_spec(t, cb), out_shape=jax.ShapeDtypeStruct((t, f), BF16),
        scratch_shapes=[pltpu.VMEM((t + 8, cb), F32)],
        compiler_params=_cparams(("parallel",)),
    )(hf, hf, w, b)


def _ffn_act_bwd(name, dact, hf, w, b, cb=128):
    t = hf.shape[0]
    f = hf.shape[1] // 2
    nb = f // cb
    tm = CONV_ROWS

    def body(da_ref, g_ref, v_ref, w_ref, b_ref, dhf_ref, dw_ref, db_ref, pad_ref, dgc_ref):
        pad_ref[pl.ds(0, 8), :] = jnp.zeros((8, cb), F32)
        pad_ref[pl.ds(8, t), :] = g_ref[...].astype(F32)
        dgc_ref[pl.ds(t, 8), :] = jnp.zeros((8, cb), F32)
        wv = [w_ref[pl.ds(k, 1), :] for k in range(FFN_KERNEL)]
        bias = b_ref[...]

        def tile_a(i, carry):
            r0 = pl.multiple_of(i * tm, tm)
            taps = {}
            _for_shifted(pad_ref, r0, tm, (6, 7, 8), lambda s, rows: taps.__setitem__(s, rows))
            gc = bias + wv[0] * taps[6] + wv[1] * taps[7] + wv[2] * taps[8]
            sg = _sigmoid(gc)
            da = da_ref[pl.ds(r0, tm), :].astype(F32)
            dhf_ref[1, pl.ds(r0, tm), :] = (da * gc * sg).astype(BF16)
            dgc = da * v_ref[pl.ds(r0, tm), :].astype(F32) * sg * (1.0 + gc * (1.0 - sg))
            dgc_ref[pl.ds(r0, tm), :] = dgc
            sums = [jnp.sum(dgc * taps[6 + k], axis=0, keepdims=True) for k in range(3)]
            sums.append(jnp.sum(dgc, axis=0, keepdims=True))
            return tuple(c + s for c, s in zip(carry, sums))

        zero = jnp.zeros((1, cb), F32)
        dw0, dw1, dw2, dbias = _unrolled(t // tm, FFN_UNROLL, tile_a, (zero, zero, zero, zero))
        row = lax.broadcasted_iota(jnp.int32, (8, cb), 0)
        dw_ref[...] = jnp.where(row == 0, dw0, jnp.where(row == 1, dw1, jnp.where(row == 2, dw2, 0.0)))
        db_ref[...] = dbias

        def tile_b(i, carry):
            r0 = pl.multiple_of(i * tm, tm)
            acc = [jnp.zeros((tm, cb), F32)]

            def tap(s, rows):
                acc[0] = acc[0] + wv[2 - s] * rows

            _for_shifted(dgc_ref, r0, tm, (0, 1, 2), tap)
            dhf_ref[0, pl.ds(r0, tm), :] = acc[0].astype(BF16)
            return carry

        lax.fori_loop(0, t // tm, tile_b, 0)

    return pl.pallas_call(
        body, name=name, grid=(nb,),
        in_specs=[_col_spec(t, cb), _col_spec(t, cb), _col_spec(t, cb, nb),
                  pl.BlockSpec((FFN_KERNEL, cb), lambda j: (0, j)), pl.BlockSpec((1, cb), lambda j: (0, j))],
        out_specs=[pl.BlockSpec((2, t, cb), lambda j: (0, 0, j)),
                   pl.BlockSpec((8, cb), lambda j: (0, j)), pl.BlockSpec((1, cb), lambda j: (0, j))],
        out_shape=[jax.ShapeDtypeStruct((2, t, f), BF16), jax.ShapeDtypeStruct((8, f), F32),
                   jax.ShapeDtypeStruct((1, f), F32)],
        scratch_shapes=[pltpu.VMEM((t + 8, cb), F32), pltpu.VMEM((t + 8, cb), F32)],
        compiler_params=_cparams(("parallel",)),
    )(dact, hf, hf, w, b)


def _silu_grad(z, sg):
    return sg * (1.0 + z * (1.0 - sg))


def _conv_fwd(name, hin, w, b, ng, nb_, cat):
    t = hin.shape[0]
    c = GROUP
    tm = CONV_ROWS
    pad = 32
    shifts = tuple(2 + k for k in range(CONV_KERNEL))

    def body(a_ref, gt_ref, w_ref, b_ref, ng_ref, nb_ref, cat_ref, u1_ref, u3_ref, pad_ref):
        pad_ref[pl.ds(0, pad), :] = jnp.zeros((pad, c), F32)
        pad_ref[pl.ds(pad, t), :] = a_ref[...] * _sigmoid(gt_ref[...])
        bias, gam, bet = b_ref[...], ng_ref[...], nb_ref[...]

        def tile(i, carry):
            r0 = pl.multiple_of(i * tm, tm)
            acc = [jnp.broadcast_to(bias, (tm, c))]

            def tap(s, rows):
                acc[0] = acc[0] + w_ref[pl.ds(s - 2, 1), :] * rows

            _for_shifted(pad_ref, r0, tm, shifts, tap)
            u1 = acc[0]
            u1_ref[pl.ds(r0, tm), :] = u1
            xhat, _ = _ln_stats(u1)
            u2 = xhat * gam + bet
            u3_ref[pl.ds(r0, tm), :] = (u2 * _sigmoid(u2)).astype(BF16)
            return carry

        _unrolled(t // tm, CONV_UNROLL, tile, 0)

    vec = pl.BlockSpec((1, c), lambda j: (0, j))
    return pl.pallas_call(
        body, name=name, grid=(N_GROUPS,),
        in_specs=[_col_spec(t, c), _col_spec(t, c, N_GROUPS),
                  pl.BlockSpec((CONV_KERNEL, c), lambda j: (0, j)), vec, vec, vec, ANY],
        out_specs=[_col_spec(t, c), _col_spec(t, c)],
        out_shape=[jax.ShapeDtypeStruct((t, CONV_WIDTH), F32), jax.ShapeDtypeStruct(cat.shape, BF16)],
        input_output_aliases={6: 1},
        scratch_shapes=[pltpu.VMEM((t + pad, c), F32)],
        compiler_params=_cparams(("parallel",)),
    )(hin, hin, w, b, ng, nb_, cat)


def _conv_bwd(name, dcat, u1, hin, w, ng, nb_):
    t = hin.shape[0]
    c = GROUP
    tm = CONV_ROWS
    pad = 32
    nk = CONV_KERNEL

    def body(du3_ref, u1_ref, a_ref, gt_ref, w_ref, ng_ref, nb_ref,
             da_ref, dgt_ref, dw_ref, db_ref, dng_ref, dnb_ref, u0_ref, du1_ref, dwp_ref):
        u0_ref[pl.ds(0, pad), :] = jnp.zeros((pad, c), F32)
        u0_ref[pl.ds(pad, t), :] = a_ref[...] * _sigmoid(gt_ref[...])
        du1_ref[pl.ds(t, pad), :] = jnp.zeros((pad, c), F32)
        dwp_ref[...] = jnp.zeros_like(dwp_ref)
        gam, bet = ng_ref[...], nb_ref[...]

        def tile_a(i, carry):
            r0 = pl.multiple_of(i * tm, tm)
            u1 = u1_ref[pl.ds(r0, tm), :]
            xhat, rstd = _ln_stats(u1)
            u2 = xhat * gam + bet
            sg = _sigmoid(u2)
            du2 = du3_ref[pl.ds(r0, tm), :] * _silu_grad(u2, sg)
            dxhat = du2 * gam
            m1 = jnp.mean(dxhat, axis=-1, keepdims=True)
            m2 = jnp.mean(dxhat * xhat, axis=-1, keepdims=True)
            du1 = rstd * (dxhat - m1 - xhat * m2)
            du1_ref[pl.ds(r0, tm), :] = du1
            sums = (jnp.sum(du1, axis=0, keepdims=True), jnp.sum(du2 * xhat, axis=0, keepdims=True),
                    jnp.sum(du2, axis=0, keepdims=True))
            return tuple(x + s for x, s in zip(carry, sums))

        zero = jnp.zeros((1, c), F32)
        dbias, dgam, dbet = _unrolled(t // tm, CONV_UNROLL, tile_a, (zero, zero, zero))
        db_ref[...] = dbias
        dng_ref[...] = dgam
        dnb_ref[...] = dbet

        def tile_b(i, carry):
            r0 = pl.multiple_of(i * tm, tm)
            du1 = du1_ref[pl.ds(r0, tm), :]
            acc = [jnp.zeros((tm, c), F32)]

            def tap_dx(s, rows):
                acc[0] = acc[0] + w_ref[pl.ds(nk - 1 - s, 1), :] * rows

            _for_shifted(du1_ref, r0, tm, tuple(range(nk)), tap_dx)

            def tap_dw(s, rows):
                part = (du1 * rows).reshape(tm // 8, 8, c).sum(axis=0)
                dwp_ref[s - 2] = dwp_ref[s - 2] + part

            _for_shifted(u0_ref, r0, tm, tuple(2 + k for k in range(nk)), tap_dw)
            du0 = acc[0]
            a = a_ref[pl.ds(r0, tm), :]
            sg = _sigmoid(gt_ref[pl.ds(r0, tm), :])
            da_ref[pl.ds(r0, tm), :] = (du0 * sg).astype(BF16)
            dgt_ref[pl.ds(r0, tm), :] = (du0 * a * sg * (1.0 - sg)).astype(BF16)
            return carry

        lax.fori_loop(0, t // tm, tile_b, 0)
        dw_ref[...] = jnp.sum(dwp_ref[...], axis=1)

    vec = pl.BlockSpec((1, c), lambda j: (0, j))
    vshape = jax.ShapeDtypeStruct((1, CONV_WIDTH), F32)
    return pl.pallas_call(
        body, name=name, grid=(N_GROUPS,),
        in_specs=[_col_spec(t, c), _col_spec(t, c), _col_spec(t, c), _col_spec(t, c, N_GROUPS),
                  pl.BlockSpec((nk, c), lambda j: (0, j)), vec, vec],
        out_specs=[_col_spec(t, c), _col_spec(t, c), pl.BlockSpec((32, c), lambda j: (0, j)), vec, vec, vec],
        out_shape=[jax.ShapeDtypeStruct((t, CONV_WIDTH), BF16), jax.ShapeDtypeStruct((t, CONV_WIDTH), BF16),
                   jax.ShapeDtypeStruct((32, CONV_WIDTH), F32), vshape, vshape, vshape],
        scratch_shapes=[pltpu.VMEM((t + pad, c), F32), pltpu.VMEM((t + pad, c), F32),
                        pltpu.VMEM((32, 8, c), F32)],
        compiler_params=_cparams(("parallel",)),
    )(dcat, u1, hin, hin, w, ng, nb_)


LEVELS = (64, 32, 16)
HGRN_UNROLL = 2
HGRN_UNROLL_FWD = 4
NT_DIMS = (((1,), (1,)), ((), ()))
NN_DIMS = (((1,), (0,)), ((), ()))
TN_DIMS = (((0,), (0,)), ((), ()))


def _bdot(a, b, dims):
    return lax.dot_general(a.astype(BF16), b.astype(BF16), dims, preferred_element_type=F32)


def _hdot(a, b):
    return jnp.dot(a, b, precision=lax.Precision.HIGHEST, preferred_element_type=F32)


def _chunk_consts():
    rid = lax.broadcasted_iota(jnp.int32, (CHUNK, GROUP), 0)
    ti = lax.broadcasted_iota(jnp.int32, (CHUNK, CHUNK), 0)
    si = lax.broadcasted_iota(jnp.int32, (CHUNK, CHUNK), 1)
    tri = (si <= ti).astype(F32)
    second = [(rid & (b // 2)) != 0 for b in LEVELS]
    same = [None] + [(ti // b) == (si // b) for b in LEVELS[1:]]
    sub = lax.broadcasted_iota(jnp.int32, (SUB, GROUP), 0)
    return rid, tri, second, same, sub


def _level_refs(cum_ref, rid, base):
    row = lambda i: cum_ref[pl.ds(base + i, 1), :]
    l1 = jnp.broadcast_to(row(31), (CHUNK, GROUP))
    l2 = jnp.where(rid < 32, row(15), row(47))
    l3 = jnp.where(rid < 16, row(7), jnp.where(rid < 32, row(23), jnp.where(rid < 48, row(39), row(55))))
    return l1, l2, l3


def _level_factors(cum, brefs, second):
    out = []
    for bref, sec in zip(brefs, second):
        eq = jnp.where(sec, jnp.exp(jnp.minimum(cum - bref, 0.0)), 0.0)
        ek = jnp.where(sec, 0.0, jnp.exp(jnp.minimum(bref - cum, 0.0)))
        out.append((eq, ek))
    return out


def _gates(q, f, lb):
    sq = _sigmoid(q)
    sf = _sigmoid(f)
    fg = lb + (1.0 - lb) * sf
    return q * sq, sq, sf, fg


def _hgrn_specs(t, nc):
    c = GROUP
    col = lambda off: pl.BlockSpec((t, c), lambda h: (0, h + off))
    hin_specs = [col(16), col(24), col(32), col(40)]
    vec = pl.BlockSpec((1, c), lambda h: (0, h))
    lbs = pl.BlockSpec((2, c), lambda h: (0, h))
    st = pl.BlockSpec((1, nc, c, c), lambda h: (h, 0, 0, 0))
    return col, hin_specs, vec, lbs, st


def _hgrn_fwd(name, hin, lb_logits, hg):
    t = hin.shape[0]
    nc = t // CHUNK
    c = GROUP
    col, hin_specs, vec, lbs, st = _hgrn_specs(t, nc)

    def body(q_ref, f_ref, v_ref, og_ref, lb_ref, hg_ref, o_ref, ob_ref, st_ref,
             s_ref, cum_ref, kk_ref, vc_ref):
        rid, tri, second, same, sub = _chunk_consts()
        lb = _sigmoid(lb_ref[pl.ds(0, 1), :] - lb_ref[pl.ds(1, 1), :])
        gain = hg_ref[...]
        s_ref[...] = jnp.zeros_like(s_ref)

        def chunk(ci, u):
            base = u * CHUNK
            r0 = pl.multiple_of(ci * CHUNK, CHUNK)
            rows = pl.ds(r0, CHUNK)
            qh, _, _, fg = _gates(q_ref[rows, :], f_ref[rows, :], lb)
            v = v_ref[rows, :]
            kk = 1.0 - fg
            cum = _hdot(tri, jnp.log(fg))
            cum_ref[pl.ds(base, CHUNK), :] = cum
            kk_ref[pl.ds(base, CHUNK), :] = kk
            vc_ref[pl.ds(base, CHUNK), :] = v
            sprev = s_ref[...]
            st_ref[0, ci] = sprev
            blast = cum_ref[pl.ds(base + CHUNK - 1, 1), :]
            o = _bdot(qh * jnp.exp(cum), sprev, NT_DIMS)
            s_ref[...] = sprev * jnp.exp(blast) + _bdot(v, kk * jnp.exp(blast - cum), TN_DIMS)
            a = None
            for (eq, ek), msk in zip(_level_factors(cum, _level_refs(cum_ref, rid, base), second), same):
                al = _bdot(qh * eq, kk * ek, NT_DIMS)
                al = al if msk is None else jnp.where(msk, al, 0.0)
                a = al if a is None else a + al
            o = o + _bdot(a, v, NN_DIMS)
            diag = []
            for sb in range(CHUNK // SUB):
                lo = sb * SUB
                qb = qh[lo:lo + SUB]
                cb = cum[lo:lo + SUB]
                od = jnp.zeros((SUB, c), F32)
                for s in range(SUB):
                    e = jnp.where(sub >= s, jnp.exp(jnp.minimum(cb - cum_ref[pl.ds(base + lo + s, 1), :], 0.0)), 0.0)
                    acol = jnp.sum(qb * e * kk_ref[pl.ds(base + lo + s, 1), :], axis=-1, keepdims=True)
                    od = od + acol * vc_ref[pl.ds(base + lo + s, 1), :]
                diag.append(od)
            o = o + jnp.concatenate(diag, axis=0)
            o_ref[rows, :] = o
            y = o * lax.rsqrt(jnp.mean(o * o, axis=-1, keepdims=True) + RMS_EPS) * gain
            og = og_ref[rows, :]
            ob_ref[rows, :] = (y * og * _sigmoid(og)).astype(BF16)

        def chunks(i, carry):
            for u in range(HGRN_UNROLL_FWD):
                chunk(i * HGRN_UNROLL_FWD + u, u)
            return carry

        lax.fori_loop(0, nc // HGRN_UNROLL_FWD, chunks, 0)

    return pl.pallas_call(
        body, name=name, grid=(N_GROUPS,),
        in_specs=hin_specs + [lbs, vec],
        out_specs=[col(0), col(N_GROUPS), st],
        out_shape=[jax.ShapeDtypeStruct((t, HGRN_WIDTH), F32), jax.ShapeDtypeStruct((t, CONV_WIDTH + HGRN_WIDTH), BF16),
                   jax.ShapeDtypeStruct((N_GROUPS, nc, c, c), F32)],
        scratch_shapes=[pltpu.VMEM((c, c), F32)] + [pltpu.VMEM((HGRN_UNROLL_FWD * CHUNK, c), F32)] * 3,
        compiler_params=_cparams(("parallel",)),
    )(hin, hin, hin, hin, lb_logits, hg)


def _hgrn_bwd(name, dcat, hin, o_raw, states, lb_logits, hg):
    t = hin.shape[0]
    nc = t // CHUNK
    c = GROUP
    col, hin_specs, vec, lbs, st = _hgrn_specs(t, nc)

    def body(do_ref, q_ref, f_ref, v_ref, og_ref, o_ref, st_ref, lb_ref, hg_ref,
             dq_ref, df_ref, dv_ref, dog_ref, dhg_ref, dlb_ref,
             ds_ref, cum_ref, kk_ref, vc_ref):
        rid, tri, second, same, sub = _chunk_consts()
        trit = tri.T
        lb = _sigmoid(lb_ref[pl.ds(0, 1), :] - lb_ref[pl.ds(1, 1), :])
        gain = hg_ref[...]
        ds_ref[...] = jnp.zeros_like(ds_ref)

        def chunk(i, carry, u):
            base = u * CHUNK
            dhg, dlb = carry
            ci = nc - 1 - i
            r0 = pl.multiple_of(ci * CHUNK, CHUNK)
            rows = pl.ds(r0, CHUNK)
            q = q_ref[rows, :]
            qh, sq, sf, fg = _gates(q, f_ref[rows, :], lb)
            v = v_ref[rows, :]
            kk = 1.0 - fg
            cum = _hdot(tri, jnp.log(fg))
            cum_ref[pl.ds(base, CHUNK), :] = cum
            kk_ref[pl.ds(base, CHUNK), :] = kk
            vc_ref[pl.ds(base, CHUNK), :] = v
            o = o_ref[rows, :]
            og = og_ref[rows, :]
            sg = _sigmoid(og)
            rinv = lax.rsqrt(jnp.mean(o * o, axis=-1, keepdims=True) + RMS_EPS)
            yn = o * rinv
            dof = do_ref[rows, :]
            dog_ref[rows, :] = (dof * yn * gain * _silu_grad(og, sg)).astype(BF16)
            dz = dof * og * sg
            dhg = dhg + jnp.sum(dz * yn, axis=0, keepdims=True)
            dy = dz * gain
            do = rinv * (dy - yn * jnp.mean(dy * yn, axis=-1, keepdims=True))
            sprev = st_ref[0, ci]
            dsn = ds_ref[...]
            blast = cum_ref[pl.ds(base + CHUNK - 1, 1), :]
            eq0 = jnp.exp(cum)
            ek0 = jnp.exp(blast - cum)
            dqh = _bdot(do, sprev, NN_DIMS) * eq0
            dkk = _bdot(v, dsn, NN_DIMS) * ek0
            dlast = (jnp.sum(kk * dkk, axis=0, keepdims=True)
                     + jnp.exp(blast) * jnp.sum(dsn * sprev, axis=0, keepdims=True))
            dv = _bdot(kk * ek0, dsn, NT_DIMS)
            ds_ref[...] = dsn * jnp.exp(blast) + _bdot(do, qh * eq0, TN_DIMS)
            dg = qh * dqh - kk * dkk
            da = _bdot(do, v, NT_DIMS)
            a = None
            for (eq, ek), msk in zip(_level_factors(cum, _level_refs(cum_ref, rid, base), second), same):
                ql, kl = (qh * eq).astype(BF16), (kk * ek).astype(BF16)
                al = _bdot(ql, kl, NT_DIMS)
                dal = da
                if msk is not None:
                    al = jnp.where(msk, al, 0.0)
                    dal = jnp.where(msk, da, 0.0)
                a = al if a is None else a + al
                dql = _bdot(dal, kl, NN_DIMS)
                dkl = _bdot(dal, ql, TN_DIMS)
                dqh = dqh + dql * eq
                dkk = dkk + dkl * ek
                dg = dg + (ql.astype(F32) * dql - kl.astype(F32) * dkl)
            dv = dv + _bdot(a, do, TN_DIMS)
            dq_d, dk_d, dv_d = [], [], []
            for sb in range(CHUNK // SUB):
                lo = sb * SUB
                qb = qh[lo:lo + SUB]
                cb = cum[lo:lo + SUB]
                dob = do[lo:lo + SUB]
                dqb = jnp.zeros((SUB, c), F32)
                dkb = jnp.zeros((SUB, c), F32)
                dvb = jnp.zeros((SUB, c), F32)
                for s in range(SUB):
                    e = jnp.where(sub >= s, jnp.exp(jnp.minimum(cb - cum_ref[pl.ds(base + lo + s, 1), :], 0.0)), 0.0)
                    ks = kk_ref[pl.ds(base + lo + s, 1), :]
                    qe = qb * e
                    dacol = jnp.sum(dob * vc_ref[pl.ds(base + lo + s, 1), :], axis=-1, keepdims=True)
                    acol = jnp.sum(qe * ks, axis=-1, keepdims=True)
                    dqb = dqb + dacol * (ks * e)
                    dkb = jnp.where(sub == s, jnp.sum(dacol * qe, axis=0, keepdims=True), dkb)
                    dvb = jnp.where(sub == s, jnp.sum(acol * dob, axis=0, keepdims=True), dvb)
                dq_d.append(dqb)
                dk_d.append(dkb)
                dv_d.append(dvb)
            dq_d = jnp.concatenate(dq_d, axis=0)
            dk_d = jnp.concatenate(dk_d, axis=0)
            dqh = dqh + dq_d
            dkk = dkk + dk_d
            dg = dg + (qh * dq_d - kk * dk_d)
            dv = dv + jnp.concatenate(dv_d, axis=0)
            dlf = _hdot(trit, dg) + dlast
            dfg = dlf / fg - dkk
            df_ref[rows, :] = (dfg * (1.0 - lb) * sf * (1.0 - sf)).astype(BF16)
            dlb = dlb + jnp.sum(dfg * (1.0 - sf), axis=0, keepdims=True)
            dq_ref[rows, :] = (dqh * _silu_grad(q, sq)).astype(BF16)
            dv_ref[rows, :] = dv.astype(BF16)
            return dhg, dlb

        def chunks(i, carry):
            for u in range(HGRN_UNROLL):
                carry = chunk(i * HGRN_UNROLL + u, carry, u)
            return carry

        zero = jnp.zeros((1, c), F32)
        dhg, dlb = lax.fori_loop(0, nc // HGRN_UNROLL, chunks, (zero, zero))
        dhg_ref[...] = dhg
        dl0 = dlb * lb * (1.0 - lb)
        dlb_ref[...] = jnp.where(lax.broadcasted_iota(jnp.int32, (2, c), 0) == 0, dl0, -dl0)

    big = jax.ShapeDtypeStruct((t, HGRN_WIDTH), BF16)
    return pl.pallas_call(
        body, name=name, grid=(N_GROUPS,),
        in_specs=[col(8)] + hin_specs + [col(0), st, lbs, vec],
        out_specs=[col(0)] * 4 + [vec, lbs],
        out_shape=[big] * 4 + [jax.ShapeDtypeStruct((1, HGRN_WIDTH), F32), jax.ShapeDtypeStruct((2, HGRN_WIDTH), F32)],
        scratch_shapes=[pltpu.VMEM((c, c), F32)] + [pltpu.VMEM((HGRN_UNROLL * CHUNK, c), F32)] * 3,
        compiler_params=_cparams(("parallel",)),
    )(dcat, hin, hin, hin, hin, o_raw, states, lb_logits, hg)


ANY = pl.BlockSpec(memory_space=pl.ANY)


def _my_place():
    return lax.axis_index("x"), lax.axis_index("y"), lax.axis_index("c")


HBM = pl.BlockSpec(memory_space=pltpu.HBM)
SEM = pl.BlockSpec(memory_space=pltpu.SEMAPHORE)
EFFECT = pltpu.SideEffectType.DATAFLOW_SIDE_EFFECTING


def _peer(k):
    x, y, c = _my_place()
    px = 1 - x if k & 4 else x
    py = 1 - y if k & 2 else y
    pc = 1 - c if k & 1 else c
    return (px, py, pc), 4 * px + 2 * py + pc


def _exchange_copy(k, src_ref, land_ref, send_sems, recv_sems, scatter, landing):
    x, y, c = _my_place()
    me = 4 * x + 2 * y + c
    to, idx = _peer(k)
    return pltpu.make_async_remote_copy(
        src_ref=src_ref.at[idx] if scatter else src_ref,
        dst_ref=land_ref.at[idx] if landing else land_ref.at[me],
        send_sem=send_sems.at[k - 1], recv_sem=recv_sems.at[k - 1], device_id=to, device_id_type=MESH)


ALL_PEERS = tuple(range(1, N_DEV))
NEAR_PEERS = (1, 2, 4, 6)
SAME_CORE_PEERS = (2, 4, 6)


def _exchange_start(name, src, land, scatter, ks=ALL_PEERS):
    def body(src_ref, land_ref, send_sems, recv_sems, src_thru, land_thru, token):
        for k in ks:
            _exchange_copy(k, src_ref, land_ref, send_sems, recv_sems, scatter, landing=False).start()
        token[...] = jnp.zeros_like(token)

    send_sems, recv_sems, src_thru, land_thru, token = pl.pallas_call(
        body, name=name,
        out_shape=(pltpu.SemaphoreType.DMA((N_DEV - 1,)), pltpu.SemaphoreType.DMA((N_DEV - 1,)),
                   pltpu.HBM(src.shape, src.dtype), pltpu.HBM(land.shape, land.dtype),
                   jax.ShapeDtypeStruct((8, 128), F32)),
        in_specs=(HBM, HBM), out_specs=(SEM, SEM, HBM, HBM, pl.BlockSpec(memory_space=pltpu.VMEM)),
        input_output_aliases={0: 2, 1: 3},
        compiler_params=pltpu.CompilerParams(has_side_effects=EFFECT),
    )(pltpu.with_memory_space_constraint(src, pltpu.HBM), pltpu.with_memory_space_constraint(land, pltpu.HBM))
    return (send_sems, recv_sems, src_thru, land_thru, scatter, ks), token


def _exchange_wait(name, handle, after):
    send_sems, recv_sems, src_thru, land_thru, scatter, ks = handle

    def body(src_ref, land_ref, send_sems, recv_sems, after_ref, src_dead, got_ref):
        for k in ks:
            cp = _exchange_copy(k, src_ref, land_ref, send_sems, recv_sems, scatter, landing=True)
            cp.wait_send()
            cp.wait_recv()

    return pl.pallas_call(
        body, name=name,
        out_shape=(pltpu.HBM(src_thru.shape, src_thru.dtype), pltpu.HBM(land_thru.shape, land_thru.dtype)),
        in_specs=(HBM, HBM, SEM, SEM, ANY), out_specs=(HBM, HBM), input_output_aliases={0: 0, 1: 1},
        compiler_params=pltpu.CompilerParams(has_side_effects=EFFECT),
    )(src_thru, land_thru, send_sems, recv_sems, after)[1]


def _relay_copy(j, land_ref, send_sems, recv_sems, landing):
    x, y, c = _my_place()
    k = SAME_CORE_PEERS[j]
    _, sent = _peer(k)
    _, got = _peer(k + 1)
    return pltpu.make_async_remote_copy(
        src_ref=land_ref.at[sent], dst_ref=land_ref.at[got] if landing else land_ref.at[sent],
        send_sem=send_sems.at[j], recv_sem=recv_sems.at[j], device_id=(x, y, 1 - c), device_id_type=MESH)


def _relay_start(name, land):
    n = len(SAME_CORE_PEERS)

    def body(land_ref, send_sems, recv_sems, land_thru, token):
        for j in range(n):
            _relay_copy(j, land_ref, send_sems, recv_sems, landing=False).start()
        token[...] = jnp.zeros_like(token)

    send_sems, recv_sems, land_thru, token = pl.pallas_call(
        body, name=name,
        out_shape=(pltpu.SemaphoreType.DMA((n,)), pltpu.SemaphoreType.DMA((n,)),
                   pltpu.HBM(land.shape, land.dtype), jax.ShapeDtypeStruct((8, 128), F32)),
        in_specs=(HBM,), out_specs=(SEM, SEM, HBM, pl.BlockSpec(memory_space=pltpu.VMEM)),
        input_output_aliases={0: 2},
        compiler_params=pltpu.CompilerParams(has_side_effects=EFFECT),
    )(pltpu.with_memory_space_constraint(land, pltpu.HBM))
    return (send_sems, recv_sems, land_thru), token


def _relay_wait(name, handle, after):
    send_sems, recv_sems, land_thru = handle

    def body(land_ref, send_sems, recv_sems, after_ref, got_ref):
        for j in range(len(SAME_CORE_PEERS)):
            cp = _relay_copy(j, land_ref, send_sems, recv_sems, landing=True)
            cp.wait_send()
            cp.wait_recv()

    return pl.pallas_call(
        body, name=name, out_shape=pltpu.HBM(land_thru.shape, land_thru.dtype),
        in_specs=(HBM, SEM, SEM, ANY), out_specs=HBM, input_output_aliases={0: 0},
        compiler_params=pltpu.CompilerParams(has_side_effects=EFFECT),
    )(land_thru, send_sems, recv_sems, after)


def _own_slot(own, me):
    land = lax.empty((N_DEV,) + own.shape, own.dtype)
    return lax.dynamic_update_slice_in_dim(land, own[None], me, axis=0)


def _adamw_math(w, g, m, v):
    m = ADAM_B1 * m + (1.0 - ADAM_B1) * g
    v = ADAM_B2 * v + (1.0 - ADAM_B2) * (g * g)
    m_hat = m / (1.0 - ADAM_B1 ** ADAM_STEP)
    v_hat = v / (1.0 - ADAM_B2 ** ADAM_STEP)
    delta = -ADAM_LR * (m_hat / (jnp.sqrt(v_hat) + ADAM_EPS) + ADAM_WD * w)
    return delta, m, v


def _adamw_sum(name, recv, w, m, v, tr, row0=0, partial=None):
    r, c = w.shape
    rr = recv.shape[1]
    off = row0 // tr

    def body(recv_ref, w_ref, m_ref, v_ref, *refs):
        g_ref, d_ref, mo_ref, vo_ref = refs[-4:]
        g = recv_ref[0].astype(F32)
        for j in range(1, N_DEV):
            g = g + recv_ref[j].astype(F32)
        g_ref[...] = g
        d_ref[...], mo_ref[...], vo_ref[...] = _adamw_math(w_ref[...], g, m_ref[...], v_ref[...])

    tile = pl.BlockSpec((tr, c), lambda i: (i + off, 0))
    out = jax.ShapeDtypeStruct((r, c), F32)
    prev = list(partial) if partial is not None else []
    return pl.pallas_call(
        body, name=name, grid=(rr // tr,),
        in_specs=[pl.BlockSpec((N_DEV, tr, c), lambda i: (0, i, 0)), tile, tile, tile] + [ANY] * len(prev),
        out_specs=[tile] * 4, out_shape=[out] * 4,
        input_output_aliases={4 + i: i for i in range(len(prev))},
        compiler_params=_cparams(("parallel",)),
    )(recv, w, m, v, *prev)


def _sum_parts(name, parts):
    _, r, c = parts.shape

    def body(p_ref, o_ref):
        acc = p_ref[0]
        for j in range(1, N_DEV):
            acc = acc + p_ref[j]
        o_ref[...] = acc

    return pl.pallas_call(body, name=name, out_shape=jax.ShapeDtypeStruct((r, c), F32),
                          compiler_params=_cparams())(parts)


def _adamw_small(name, w, g, m, v):
    def body(w_ref, g_ref, m_ref, v_ref, d_ref, mo_ref, vo_ref):
        d_ref[...], mo_ref[...], vo_ref[...] = _adamw_math(w_ref[...], g_ref[...], m_ref[...], v_ref[...])

    out = jax.ShapeDtypeStruct(w.shape, F32)
    return pl.pallas_call(body, name=name, out_shape=[out] * 3, compiler_params=_cparams())(w, g, m, v)


def _pack(pieces, rows):
    flat = jnp.concatenate([p.reshape(-1).astype(F32) for p in pieces])
    return jnp.pad(flat, (0, rows * 128 - flat.shape[0])).reshape(rows, 128)


def _unpack(packed, shapes):
    flat = packed.reshape(-1)
    out, off = [], 0
    for s in shapes:
        n = 1
        for d in s:
            n *= d
        out.append(flat[off:off + n].reshape(s))
        off += n
    return out


def kernel(x, emb_ln_g, emb_ln_b, w_in, conv_w, conv_b, conv_norm_g, conv_norm_b, lb_logits, hgrn_norm_g, w_out, ln1_g, ln1_b, w_ffn_up, ffn_conv_w, ffn_conv_b, w_ffn_down, ln2_g, ln2_b, loss_target, m_emb_ln_g, m_emb_ln_b, m_w_in, m_conv_w, m_conv_b, m_conv_norm_g, m_conv_norm_b, m_lb_logits, m_hgrn_norm_g, m_w_out, m_ln1_g, m_ln1_b, m_w_ffn_up, m_ffn_conv_w, m_ffn_conv_b, m_w_ffn_down, m_ln2_g, m_ln2_b, v_emb_ln_g, v_emb_ln_b, v_w_in, v_conv_w, v_conv_b, v_conv_norm_g, v_conv_norm_b, v_lb_logits, v_hgrn_norm_g, v_w_out, v_ln1_g, v_ln1_b, v_w_ffn_up, v_ffn_conv_w, v_ffn_conv_b, v_w_ffn_down, v_ln2_g, v_ln2_b):
    t = x.shape[1]
    me = 4 * lax.axis_index("x") + 2 * lax.axis_index("y") + lax.axis_index("c")
    x2, tgt = x[0], loss_target[0]
    ns_in, ns_up = w_in.shape[2], w_ffn_up.shape[2]
    rs_out, rs_down = w_out.shape[1], w_ffn_down.shape[1]
    cs, fs = conv_w.shape[2], ffn_conv_w.shape[2]

    def gather_start(name, w, prev, ks=ALL_PEERS):
        shard = (w[0] + prev).astype(BF16)
        return _exchange_start(name, shard, _own_slot(shard, me), scatter=False, ks=ks)

    h_in, tok = gather_start("ag_w_in_start", w_in, 0.0, NEAR_PEERS)
    taps = _pack([conv_w[0], ffn_conv_w[0]], 48) + tok[0, 0]
    h_taps, tok = _exchange_start("ag_taps_start", taps, _own_slot(taps, me), scatter=False)
    h_out, tok = gather_start("ag_w_out_start", w_out, tok[0, 0])
    h_up, tok = gather_start("ag_w_up_start", w_ffn_up, tok[0, 0], NEAR_PEERS)
    h_down, tok = gather_start("ag_w_down_start", w_ffn_down, tok[0, 0])

    row = lambda a: a.reshape(1, -1)

    _, h0, h0b, h0bt = _ln_fwd("ln_in", x2, None, row(emb_ln_g) + tok[0, 0], row(emb_ln_b), 1.0)
    h_relay, tok_relay = _relay_start("ag_w_in_relay_start", _exchange_wait("ag_w_in_wait", h_in, h0b))
    win_g = _relay_wait("ag_w_in_relay_wait", h_relay, tok_relay)
    win_n = win_g.transpose(1, 0, 2).reshape(D_MODEL, IN_PROJ)
    hin = _mm_nn("mm_in", h0b, win_n, F32, tm=1024, tn=ns_in, tk=D_MODEL)
    n_cw, n_fw = CONV_KERNEL * cs, FFN_KERNEL * fs
    taps_g = _exchange_wait("ag_taps_wait", h_taps, hin).reshape(N_DEV, -1)
    cw_full = taps_g[:, :n_cw].reshape(N_DEV, CONV_KERNEL, cs).transpose(1, 0, 2).reshape(CONV_KERNEL, CONV_WIDTH)
    fw_full = taps_g[:, n_cw:n_cw + n_fw].reshape(N_DEV, FFN_KERNEL, fs).transpose(1, 0, 2).reshape(FFN_KERNEL, D_FF)

    o_raw, cat_right, states = _hgrn_fwd("hgrn_fwd", hin, lb_logits, hgrn_norm_g)
    u1, catb = _conv_fwd("conv_fwd", hin, cw_full, conv_b, conv_norm_g, conv_norm_b, cat_right)
    wout_g = _exchange_wait("ag_w_out_wait", h_out, catb).reshape(D_MODEL, D_MODEL)
    h_up_relay, tok = _relay_start("ag_w_up_relay_start", _exchange_wait("ag_w_up_wait", h_up, wout_g))
    mix = _mm_nn("mm_out", catb, wout_g, F32, tm=1024, tn=1024, tk=D_MODEL, after=tok)
    r1, h1, h1b, h1bt = _ln_fwd("ln1", h0, mix, ln1_g, ln1_b, ALPHA)
    wup_g = _relay_wait("ag_w_up_relay_wait", h_up_relay, h1b)
    wup_n = wup_g.transpose(1, 0, 2).reshape(D_MODEL, 2 * D_FF)
    hf = _mm_nn("mm_up", h1b, wup_n, BF16, tm=1024, tn=1024, tk=D_MODEL)
    actb = _ffn_act_fwd("ffn_act", hf, fw_full, ffn_conv_b)
    wdown_g = _exchange_wait("ag_w_down_wait", h_down, actb).reshape(D_FF, D_MODEL)
    ffn = _mm_nn("mm_down", actb, wdown_g, F32, tm=512, tn=1024, tk=D_FF)
    dr2, dr2b, g_ln2g, g_ln2b, loss = _ln2_loss_bwd("ln2_loss", h1, ffn, ln2_g, ln2_b, tgt)

    def scatter_start(name, parts):
        own = lax.dynamic_index_in_dim(parts, me, axis=0, keepdims=False)
        return _exchange_start(name, parts, _own_slot(own, me), scatter=True)

    dact = _mm_nt("mm_dact", dr2b, wdown_g, BF16, tm=1024, tn=D_FF // 2, tk=D_MODEL)
    gw_down = _mm_nn("mm_dw_down", actb.T, dr2b, BF16, tm=rs_down, tn=1024, tk=t)
    s_down, tok = scatter_start("a2a_w_down_start", gw_down.reshape(N_DEV, rs_down, D_MODEL))
    dhf, g_fw, g_fb = _ffn_act_bwd("ffn_act_bwd", dact, hf, fw_full, ffn_conv_b + tok[0, 0])
    tm = min(1024, t)
    gw_up = _matmul(
        "mm_dw_up", h1bt, dhf, (N_DEV, D_MODEL, ns_up), BF16, (D_MODEL // 1024, N_DEV, 1),
        pl.BlockSpec((1024, t), lambda i, j, kk: (i, 0)),
        pl.BlockSpec((1, t, ns_up), lambda i, j, kk: (j // 4, 0, j % 4)),
        pl.BlockSpec((1, 1024, ns_up), lambda i, j, kk: (j, i, 0)), nt=False)
    s_up, tok = scatter_start("a2a_w_up_start", gw_up)
    tkf = D_FF // 2
    dh1 = _matmul(
        "mm_dh1", dhf, wup_n, (t, D_MODEL), F32, (t // tm, D_MODEL // 1024, 4),
        pl.BlockSpec((1, tm, tkf), lambda i, j, kk: (kk // 2, i, kk % 2)),
        pl.BlockSpec((1024, tkf), lambda i, j, kk: (j, kk)),
        pl.BlockSpec((tm, 1024), lambda i, j, kk: (i, j)), nt=True, after=tok)
    dr1, dr1b, g_ln1g, g_ln1b = _ln_bwd("ln1_bwd", r1, dr2, dh1, ln1_g + tok[0, 0], ALPHA, True)
    gw_out = _mm_nn("mm_dw_out", catb.T, dr1b, BF16, tm=1024, tn=1024, tk=t)
    s_out, tok = scatter_start("a2a_w_out_start", gw_out.reshape(N_DEV, rs_out, D_MODEL))
    dcat = _mm_nt("mm_dcat", dr1b, wout_g, F32, tm=1024, tn=1024, tk=D_MODEL, after=tok)
    da, dgate, g_cw, g_cb, g_cng, g_cnb = _conv_bwd("conv_bwd", dcat, u1, hin, cw_full, conv_norm_g + tok[0, 0],
                                                    conv_norm_b)
    dq, df, di, dog, g_hg, g_lb = _hgrn_bwd("hgrn_bwd", dcat, hin, o_raw, states, lb_logits, hgrn_norm_g)
    dhin = jnp.concatenate([da, dgate, dq, df, di, dog], axis=1)
    half = D_MODEL // 2
    gw_in_a = _mm_grad_cols("mm_dw_in_a", h0bt, dhin, ns_in, 0, half, after=tok)
    s_in_a, tok = scatter_start("a2a_w_in_a_start", gw_in_a)
    gw_in_b = _mm_grad_cols("mm_dw_in_b", h0bt, dhin, ns_in, half, half, after=tok)
    s_in_b, tok = scatter_start("a2a_w_in_b_start", gw_in_b)
    dh0 = _mm_nt("mm_dh0", dhin, win_n, F32, tm=1024, tn=1024, tk=IN_PROJ // 2, after=tok)
    grad_x, g_eg, g_eb = _ln_bwd("ln_in_bwd", x2, dr1, dh0, row(emb_ln_g), ALPHA, False)

    small_shapes = [(D_MODEL,), (D_MODEL,), (CONV_KERNEL, CONV_WIDTH), (1, CONV_WIDTH), (1, CONV_WIDTH),
                    (1, CONV_WIDTH), (2, HGRN_WIDTH), (1, HGRN_WIDTH), (1, D_MODEL), (1, D_MODEL),
                    (FFN_KERNEL, D_FF), (1, D_FF), (1, D_MODEL), (1, D_MODEL), (128,)]
    rows_small = 569
    packed = _pack([g_eg, g_eb, g_cw[:CONV_KERNEL], g_cb, g_cng, g_cnb, g_lb, g_hg, g_ln1g, g_ln1b,
                    g_fw[:FFN_KERNEL], g_fb, g_ln2g, g_ln2b, loss], rows_small)
    h_small, tok = _exchange_start("ag_small_start", packed, _own_slot(packed, me), scatter=False)

    def big(name, handle, after, w, m, v, tr):
        recv = _exchange_wait("a2a_" + name + "_wait", handle, after)
        return [o[None] for o in _adamw_sum("adamw_" + name, recv, w[0], m[0], v[0], tr)]

    u_down = big("w_down", s_down, tok, w_ffn_down, m_w_ffn_down, v_w_ffn_down, 64)
    u_up = big("w_up", s_up, u_down[1], w_ffn_up, m_w_ffn_up, v_w_ffn_up, 64)
    u_out = big("w_out", s_out, u_up[1], w_out, m_w_out, v_w_out, 64)
    summed = _sum_parts("sum_small", _exchange_wait("ag_small_wait", h_small, u_out[1]))
    (s_eg, s_eb, s_cw, s_cb, s_cng, s_cnb, s_lb, s_hg, s_l1g, s_l1b, s_fw, s_fb, s_l2g, s_l2b,
     s_loss) = _unpack(summed, small_shapes)
    s_cw = lax.dynamic_slice_in_dim(s_cw, me * cs, cs, axis=1)[None]
    s_fw = lax.dynamic_slice_in_dim(s_fw, me * fs, fs, axis=1)[None]
    g_small = [s_eg, s_eb, s_cw, s_cb, s_cng, s_cnb, s_lb, s_hg, s_l1g, s_l1b, s_fw, s_fb, s_l2g, s_l2b]
    w_small = [emb_ln_g, emb_ln_b, conv_w, conv_b, conv_norm_g, conv_norm_b, lb_logits, hgrn_norm_g,
               ln1_g, ln1_b, ffn_conv_w, ffn_conv_b, ln2_g, ln2_b]
    m_small = [m_emb_ln_g, m_emb_ln_b, m_conv_w, m_conv_b, m_conv_norm_g, m_conv_norm_b, m_lb_logits,
               m_hgrn_norm_g, m_ln1_g, m_ln1_b, m_ffn_conv_w, m_ffn_conv_b, m_ln2_g, m_ln2_b]
    v_small = [v_emb_ln_g, v_emb_ln_b, v_conv_w, v_conv_b, v_conv_norm_g, v_conv_norm_b, v_lb_logits,
               v_hgrn_norm_g, v_ln1_g, v_ln1_b, v_ffn_conv_w, v_ffn_conv_b, v_ln2_g, v_ln2_b]
    rows_own = 236
    shapes_own = [w.shape for w in w_small]
    upd = _adamw_small("adamw_small", _pack(w_small, rows_own), _pack(g_small, rows_own),
                       _pack(m_small, rows_own), _pack(v_small, rows_own))
    d_small, nm_small, nv_small = (_unpack(u, shapes_own) for u in upd)
    g_small = [g.reshape(s) for g, s in zip(g_small, shapes_own)]

    recv_a = _exchange_wait("a2a_w_in_a_wait", s_in_a, upd[0])
    part = _adamw_sum("adamw_w_in_a", recv_a, w_in[0], m_w_in[0], v_w_in[0], 128)
    recv_b = _exchange_wait("a2a_w_in_b_wait", s_in_b, part[1])
    u_in = [o[None] for o in _adamw_sum("adamw_w_in_b", recv_b, w_in[0], m_w_in[0], v_w_in[0], 128,
                                        row0=half, partial=part)]

    def ordered(small, i_in, i_out, i_up, i_down):
        (eg, eb, cw, cb, cng, cnb, lb, hg, l1g, l1b, fw, fb, l2g, l2b) = small
        return [eg, eb, i_in, cw, cb, cng, cnb, lb, hg, i_out, l1g, l1b, i_up, fw, fb, i_down, l2g, l2b]

    outs = [s_loss[0], grad_x[None]]
    for k, small in enumerate([g_small, d_small, nm_small, nv_small]):
        outs += ordered(small, u_in[k], u_out[k], u_up[k], u_down[k])
    return tuple(outs)
```

```python
import functools

import jax
import jax.numpy as jnp
from jax import lax
from jax.experimental import pallas as pl
from jax.experimental.pallas import tpu as pltpu

F32 = jnp.float32
BF16 = jnp.bfloat16

N_DEV = 8
D_MODEL = 2048
CONV_WIDTH = 1024
CONV_KERNEL = 31
HGRN_WIDTH = 1024
GROUP = 128
N_GROUPS = 8
IN_PROJ = 2 * CONV_WIDTH + 4 * HGRN_WIDTH
D_FF = 5632
FFN_KERNEL = 3
CHUNK = 64
SUB = 8
LN_EPS = 1e-5
RMS_EPS = 1e-6
ALPHA = 2.0 ** 0.25
ADAM_LR, ADAM_B1, ADAM_B2, ADAM_EPS, ADAM_WD, ADAM_STEP = 0.001, 0.9, 0.999, 1e-08, 0.01, 10

VMEM_LIMIT = 56 * 1024 * 1024
MESH = pl.DeviceIdType.MESH


def _cparams(sem=None):
    return pltpu.CompilerParams(dimension_semantics=sem, vmem_limit_bytes=VMEM_LIMIT)


def _sigmoid(x):
    return 1.0 / (1.0 + jnp.exp(-x))


def _matmul(name, a, b, out_shape, out_dtype, grid, a_spec, b_spec, o_spec, nt, after=None):
    nk = grid[2]
    dims = {True: (((1,), (1,)), ((), ())), False: (((1,), (0,)), ((), ())), "tn": (((0,), (0,)), ((), ()))}[nt]
    extra = [] if after is None else [after]

    def body(a_ref, b_ref, *rest):
        o_ref, *scratch = rest[len(extra):]
        av = a_ref[0] if len(a_ref.shape) == 3 else a_ref[...]
        bv = b_ref[0] if len(b_ref.shape) == 3 else b_ref[...]
        part = lax.dot_general(av, bv, dims, preferred_element_type=F32)

        def write(res):
            if len(o_ref.shape) == 3:
                o_ref[0] = res.astype(out_dtype)
            else:
                o_ref[...] = res.astype(out_dtype)

        if nk == 1:
            write(part)
            return
        acc_ref, = scratch
        k = pl.program_id(2)

        @pl.when(k == 0)
        def _():
            acc_ref[...] = part

        @pl.when(jnp.logical_and(k > 0, k < nk - 1))
        def _():
            acc_ref[...] += part

        @pl.when(k == nk - 1)
        def _():
            write(acc_ref[...] + part)

    acc_shape = o_spec.block_shape[-2:]
    assert all(g >= 1 for g in grid), (name, grid)
    return pl.pallas_call(
        body, name=name, grid=grid, in_specs=[a_spec, b_spec] + [pl.BlockSpec(memory_space=pl.ANY)] * len(extra),
        out_specs=o_spec, out_shape=jax.ShapeDtypeStruct(out_shape, out_dtype),
        scratch_shapes=[pltpu.VMEM(acc_shape, F32)] if nk > 1 else [],
        compiler_params=_cparams(("parallel", "parallel", "arbitrary")),
    )(a, b, *extra)


def _mm_nn(name, a, w, out_dtype, tm, tn, tk, after=None):
    m, k = a.shape
    tm, tk = min(tm, m), min(tk, k)
    n = w.shape[1]
    return _matmul(
        name, a, w, (m, n), out_dtype, (m // tm, n // tn, k // tk),
        pl.BlockSpec((tm, tk), lambda i, j, kk: (i, kk)),
        pl.BlockSpec((tk, tn), lambda i, j, kk: (kk, j)),
        pl.BlockSpec((tm, tn), lambda i, j, kk: (i, j)), nt=False, after=after)


def _mm_nt(name, a, w, out_dtype, tm, tn, tk, after=None):
    m, k = a.shape
    tm = min(tm, m)
    n = w.shape[0]
    return _matmul(
        name, a, w, (m, n), out_dtype, (m // tm, n // tn, k // tk),
        pl.BlockSpec((tm, tk), lambda i, j, kk: (i, kk)),
        pl.BlockSpec((tn, tk), lambda i, j, kk: (j, kk)),
        pl.BlockSpec((tm, tn), lambda i, j, kk: (i, j)), nt=True, after=after)


def _mm_grad_cols(name, at, b, ns, row0, rows, after, tm=1024, tk=4096):
    t = at.shape[1]
    tk = min(tk, t)
    off = row0 // tm
    return _matmul(
        name, at, b, (N_DEV, rows, ns), BF16, (rows // tm, N_DEV, t // tk),
        pl.BlockSpec((tm, tk), lambda i, j, kk: (i + off, kk)),
        pl.BlockSpec((tk, ns), lambda i, j, kk: (kk, j)),
        pl.BlockSpec((1, tm, ns), lambda i, j, kk: (j, i, 0)), nt=False, after=after)


LN_ROWS = 256


def _ln_stats(r):
    mu = jnp.mean(r, axis=-1, keepdims=True)
    xc = r - mu
    var = jnp.mean(xc * xc, axis=-1, keepdims=True)
    rstd = lax.rsqrt(var + LN_EPS)
    return xc * rstd, rstd


def _row_spec(d):
    return pl.BlockSpec((LN_ROWS, d), lambda i: (i, 0))


def _vec_spec(d):
    return pl.BlockSpec((1, d), lambda i: (0, 0))


def _ln_fwd(name, a, m, g, b, alpha):
    t, d = a.shape
    has_m = m is not None

    def body(*refs):
        if has_m:
            a_ref, m_ref, g_ref, b_ref, r_ref, y_ref, yb_ref, yt_ref = refs
            r = alpha * a_ref[...] + m_ref[...]
            r_ref[...] = r
        else:
            a_ref, g_ref, b_ref, y_ref, yb_ref, yt_ref = refs
            r = a_ref[...]
        xhat, _ = _ln_stats(r)
        y = xhat * g_ref[...] + b_ref[...]
        y_ref[...] = y
        yb_ref[...] = y.astype(BF16)
        yt_ref[...] = y.T.astype(BF16)

    ins = [a] + ([m] if has_m else []) + [g, b]
    in_specs = [_row_spec(d)] * (2 if has_m else 1) + [_vec_spec(d)] * 2
    outs = ([jax.ShapeDtypeStruct((t, d), F32)] if has_m else []) + [
        jax.ShapeDtypeStruct((t, d), F32), jax.ShapeDtypeStruct((t, d), BF16), jax.ShapeDtypeStruct((d, t), BF16)]
    res = pl.pallas_call(
        body, name=name, grid=(t // LN_ROWS,), in_specs=in_specs,
        out_specs=[_row_spec(d)] * (len(outs) - 1) + [pl.BlockSpec((d, LN_ROWS), lambda i: (0, i))], out_shape=outs,
        compiler_params=_cparams(("parallel",)),
    )(*ins)
    return res if has_m else (None, *res)


def _ln_bwd_math(r, dy, g):
    xhat, rstd = _ln_stats(r)
    dxhat = dy * g
    m1 = jnp.mean(dxhat, axis=-1, keepdims=True)
    m2 = jnp.mean(dxhat * xhat, axis=-1, keepdims=True)
    dr = rstd * (dxhat - m1 - xhat * m2)
    return dr, jnp.sum(dy * xhat, axis=0, keepdims=True), jnp.sum(dy, axis=0, keepdims=True)


def _ln2_loss_bwd(name, h1, ffn, g, b, tgt):
    t, d = h1.shape

    def body(h1_ref, f_ref, g_ref, b_ref, t_ref, dr_ref, drb_ref, dg_ref, db_ref, loss_ref):
        @pl.when(pl.program_id(0) == 0)
        def _():
            dg_ref[...] = jnp.zeros_like(dg_ref)
            db_ref[...] = jnp.zeros_like(db_ref)
            loss_ref[...] = jnp.zeros_like(loss_ref)

        r = ALPHA * h1_ref[...] + f_ref[...]
        xhat, _ = _ln_stats(r)
        e = xhat * g_ref[...] + b_ref[...] - t_ref[...]
        loss_ref[...] += 0.5 / d * jnp.sum(e * e)
        dr, dg, db = _ln_bwd_math(r, e * (1.0 / d), g_ref[...])
        dr_ref[...] = dr
        drb_ref[...] = dr.astype(BF16)
        dg_ref[...] += dg
        db_ref[...] += db

    return pl.pallas_call(
        body, name=name, grid=(t // LN_ROWS,),
        in_specs=[_row_spec(d), _row_spec(d), _vec_spec(d), _vec_spec(d), _row_spec(d)],
        out_specs=[_row_spec(d), _row_spec(d), _vec_spec(d), _vec_spec(d), _vec_spec(128)],
        out_shape=[jax.ShapeDtypeStruct((t, d), F32), jax.ShapeDtypeStruct((t, d), BF16),
                   jax.ShapeDtypeStruct((1, d), F32), jax.ShapeDtypeStruct((1, d), F32),
                   jax.ShapeDtypeStruct((1, 128), F32)],
        compiler_params=_cparams(("arbitrary",)),
    )(h1, ffn, g, b, tgt)


def _ln_bwd(name, r, dya, dyb, g, alpha, want_bf16):
    t, d = r.shape

    def body(r_ref, dya_ref, dyb_ref, g_ref, *outs):
        dr_ref = outs[0]
        dg_ref, db_ref = outs[-2:]

        @pl.when(pl.program_id(0) == 0)
        def _():
            dg_ref[...] = jnp.zeros_like(dg_ref)
            db_ref[...] = jnp.zeros_like(db_ref)

        dy = alpha * dya_ref[...] + dyb_ref[...]
        dr, dg, db = _ln_bwd_math(r_ref[...], dy, g_ref[...])
        dr_ref[...] = dr
        if want_bf16:
            outs[1][...] = dr.astype(BF16)
        dg_ref[...] += dg
        db_ref[...] += db

    big = [jax.ShapeDtypeStruct((t, d), F32)] + ([jax.ShapeDtypeStruct((t, d), BF16)] if want_bf16 else [])
    return pl.pallas_call(
        body, name=name, grid=(t // LN_ROWS,),
        in_specs=[_row_spec(d)] * 3 + [_vec_spec(d)],
        out_specs=[_row_spec(d)] * len(big) + [_vec_spec(d)] * 2,
        out_shape=big + [jax.ShapeDtypeStruct((1, d), F32)] * 2,
        compiler_params=_cparams(("arbitrary",)),
    )(r, dya, dyb, g)


CONV_ROWS = 64
CONV_UNROLL = 4
FFN_UNROLL = 2


def _unrolled(n, unroll, fn, init):
    def body(i, carry):
        for u in range(unroll):
            carry = fn(i * unroll + u, carry)
        return carry

    return lax.fori_loop(0, n // unroll, body, init)


def _for_shifted(ref, r0, tm, shifts, fn):
    for s in shifts:
        fn(s, ref[pl.ds(r0 + s, tm), :])


def _col_spec(t, cb, off=0):
    return pl.BlockSpec((t, cb), lambda j: (0, j + off))


def _ffn_act_fwd(name, hf, w, b, cb=128):
    t = hf.shape[0]
    f = hf.shape[1] // 2
    nb = f // cb
    tm = CONV_ROWS

    def body(g_ref, v_ref, w_ref, b_ref, act_ref, pad_ref):
        pad_ref[pl.ds(0, 8), :] = jnp.zeros((8, cb), F32)
        pad_ref[pl.ds(8, t), :] = g_ref[...].astype(F32)
        wv = [w_ref[pl.ds(k, 1), :] for k in range(FFN_KERNEL)]
        bias = b_ref[...]

        def tile(i, carry):
            r0 = pl.multiple_of(i * tm, tm)
            acc = [jnp.broadcast_to(bias, (tm, cb))]

            def tap(s, rows):
                acc[0] = acc[0] + wv[s - 6] * rows

            _for_shifted(pad_ref, r0, tm, (6, 7, 8), tap)
            gc = acc[0]
            act_ref[pl.ds(r0, tm), :] = (gc * _sigmoid(gc) * v_ref[pl.ds(r0, tm), :].astype(F32)).astype(BF16)
            return carry

        _unrolled(t // tm, FFN_UNROLL, tile, 0)

    return pl.pallas_call(
        body, name=name, grid=(nb,),
        in_specs=[_col_spec(t, cb), _col_spec(t, cb, nb),
                  pl.BlockSpec((FFN_KERNEL, cb), lambda j: (0, j)), pl.BlockSpec((1, cb), lambda j: (0, j))],
        out_specs=_col_spec(t, cb), out_shape=jax.ShapeDtypeStruct((t, f), BF16),
        scratch_shapes=[pltpu.VMEM((t + 8, cb), F32)],
        compiler_params=_cparams(("parallel",)),
    )(hf, hf, w, b)


def _ffn_act_bwd(name, dact, hf, w, b, cb=128):
    t = hf.shape[0]
    f = hf.shape[1] // 2
    nb = f // cb
    tm = CONV_ROWS

    def body(da_ref, g_ref, v_ref, w_ref, b_ref, dhf_ref, dw_ref, db_ref, pad_ref, dgc_ref):
        pad_ref[pl.ds(0, 8), :] = jnp.zeros((8, cb), F32)
        pad_ref[pl.ds(8, t), :] = g_ref[...].astype(F32)
        dgc_ref[pl.ds(t, 8), :] = jnp.zeros((8, cb), F32)
        wv = [w_ref[pl.ds(k, 1), :] for k in range(FFN_KERNEL)]
        bias = b_ref[...]

        def tile_a(i, carry):
            r0 = pl.multiple_of(i * tm, tm)
            taps = {}
            _for_shifted(pad_ref, r0, tm, (6, 7, 8), lambda s, rows: taps.__setitem__(s, rows))
            gc = bias + wv[0] * taps[6] + wv[1] * taps[7] + wv[2] * taps[8]
            sg = _sigmoid(gc)
            da = da_ref[pl.ds(r0, tm), :].astype(F32)
            dhf_ref[1, pl.ds(r0, tm), :] = (da * gc * sg).astype(BF16)
            dgc = da * v_ref[pl.ds(r0, tm), :].astype(F32) * sg * (1.0 + gc * (1.0 - sg))
            dgc_ref[pl.ds(r0, tm), :] = dgc
            sums = [jnp.sum(dgc * taps[6 + k], axis=0, keepdims=True) for k in range(3)]
            sums.append(jnp.sum(dgc, axis=0, keepdims=True))
            return tuple(c + s for c, s in zip(carry, sums))

        zero = jnp.zeros((1, cb), F32)
        dw0, dw1, dw2, dbias = _unrolled(t // tm, FFN_UNROLL, tile_a, (zero, zero, zero, zero))
        row = lax.broadcasted_iota(jnp.int32, (8, cb), 0)
        dw_ref[...] = jnp.where(row == 0, dw0, jnp.where(row == 1, dw1, jnp.where(row == 2, dw2, 0.0)))
        db_ref[...] = dbias

        def tile_b(i, carry):
            r0 = pl.multiple_of(i * tm, tm)
            acc = [jnp.zeros((tm, cb), F32)]

            def tap(s, rows):
                acc[0] = acc[0] + wv[2 - s] * rows

            _for_shifted(dgc_ref, r0, tm, (0, 1, 2), tap)
            dhf_ref[0, pl.ds(r0, tm), :] = acc[0].astype(BF16)
            return carry

        lax.fori_loop(0, t // tm, tile_b, 0)

    return pl.pallas_call(
        body, name=name, grid=(nb,),
        in_specs=[_col_spec(t, cb), _col_spec(t, cb), _col_spec(t, cb, nb),
                  pl.BlockSpec((FFN_KERNEL, cb), lambda j: (0, j)), pl.BlockSpec((1, cb), lambda j: (0, j))],
        out_specs=[pl.BlockSpec((2, t, cb), lambda j: (0, 0, j)),
                   pl.BlockSpec((8, cb), lambda j: (0, j)), pl.BlockSpec((1, cb), lambda j: (0, j))],
        out_shape=[jax.ShapeDtypeStruct((2, t, f), BF16), jax.ShapeDtypeStruct((8, f), F32),
                   jax.ShapeDtypeStruct((1, f), F32)],
        scratch_shapes=[pltpu.VMEM((t + 8, cb), F32), pltpu.VMEM((t + 8, cb), F32)],
        compiler_params=_cparams(("parallel",)),
    )(dact, hf, hf, w, b)


def _silu_grad(z, sg):
    return sg * (1.0 + z * (1.0 - sg))


def _conv_fwd(name, hin, w, b, ng, nb_, cat):
    t = hin.shape[0]
    c = GROUP
    tm = CONV_ROWS
    pad = 32
    shifts = tuple(2 + k for k in range(CONV_KERNEL))

    def body(a_ref, gt_ref, w_ref, b_ref, ng_ref, nb_ref, cat_ref, u1_ref, u3_ref, pad_ref):
        pad_ref[pl.ds(0, pad), :] = jnp.zeros((pad, c), F32)
        pad_ref[pl.ds(pad, t), :] = a_ref[...] * _sigmoid(gt_ref[...])
        bias, gam, bet = b_ref[...], ng_ref[...], nb_ref[...]

        def tile(i, carry):
            r0 = pl.multiple_of(i * tm, tm)
            acc = [jnp.broadcast_to(bias, (tm, c))]

            def tap(s, rows):
                acc[0] = acc[0] + w_ref[pl.ds(s - 2, 1), :] * rows

            _for_shifted(pad_ref, r0, tm, shifts, tap)
            u1 = acc[0]
            u1_ref[pl.ds(r0, tm), :] = u1
            xhat, _ = _ln_stats(u1)
            u2 = xhat * gam + bet
            u3_ref[pl.ds(r0, tm), :] = (u2 * _sigmoid(u2)).astype(BF16)
            return carry

        _unrolled(t // tm, CONV_UNROLL, tile, 0)

    vec = pl.BlockSpec((1, c), lambda j: (0, j))
    return pl.pallas_call(
        body, name=name, grid=(N_GROUPS,),
        in_specs=[_col_spec(t, c), _col_spec(t, c, N_GROUPS),
                  pl.BlockSpec((CONV_KERNEL, c), lambda j: (0, j)), vec, vec, vec, ANY],
        out_specs=[_col_spec(t, c), _col_spec(t, c)],
        out_shape=[jax.ShapeDtypeStruct((t, CONV_WIDTH), F32), jax.ShapeDtypeStruct(cat.shape, BF16)],
        input_output_aliases={6: 1},
        scratch_shapes=[pltpu.VMEM((t + pad, c), F32)],
        compiler_params=_cparams(("parallel",)),
    )(hin, hin, w, b, ng, nb_, cat)


def _conv_bwd(name, dcat, u1, hin, w, ng, nb_):
    t = hin.shape[0]
    c = GROUP
    tm = CONV_ROWS
    pad = 32
    nk = CONV_KERNEL

    def body(du3_ref, u1_ref, a_ref, gt_ref, w_ref, ng_ref, nb_ref,
             da_ref, dgt_ref, dw_ref, db_ref, dng_ref, dnb_ref, u0_ref, du1_ref, dwp_ref):
        u0_ref[pl.ds(0, pad), :] = jnp.zeros((pad, c), F32)
        u0_ref[pl.ds(pad, t), :] = a_ref[...] * _sigmoid(gt_ref[...])
        du1_ref[pl.ds(t, pad), :] = jnp.zeros((pad, c), F32)
        dwp_ref[...] = jnp.zeros_like(dwp_ref)
        gam, bet = ng_ref[...], nb_ref[...]

        def tile_a(i, carry):
            r0 = pl.multiple_of(i * tm, tm)
            u1 = u1_ref[pl.ds(r0, tm), :]
            xhat, rstd = _ln_stats(u1)
            u2 = xhat * gam + bet
            sg = _sigmoid(u2)
            du2 = du3_ref[pl.ds(r0, tm), :] * _silu_grad(u2, sg)
            dxhat = du2 * gam
            m1 = jnp.mean(dxhat, axis=-1, keepdims=True)
            m2 = jnp.mean(dxhat * xhat, axis=-1, keepdims=True)
            du1 = rstd * (dxhat - m1 - xhat * m2)
            du1_ref[pl.ds(r0, tm), :] = du1
            sums = (jnp.sum(du1, axis=0, keepdims=True), jnp.sum(du2 * xhat, axis=0, keepdims=True),
                    jnp.sum(du2, axis=0, keepdims=True))
            return tuple(x + s for x, s in zip(carry, sums))

        zero = jnp.zeros((1, c), F32)
        dbias, dgam, dbet = _unrolled(t // tm, CONV_UNROLL, tile_a, (zero, zero, zero))
        db_ref[...] = dbias
        dng_ref[...] = dgam
        dnb_ref[...] = dbet

        def tile_b(i, carry):
            r0 = pl.multiple_of(i * tm, tm)
            du1 = du1_ref[pl.ds(r0, tm), :]
            acc = [jnp.zeros((tm, c), F32)]

            def tap_dx(s, rows):
                acc[0] = acc[0] + w_ref[pl.ds(nk - 1 - s, 1), :] * rows

            _for_shifted(du1_ref, r0, tm, tuple(range(nk)), tap_dx)

            def tap_dw(s, rows):
                part = (du1 * rows).reshape(tm // 8, 8, c).sum(axis=0)
                dwp_ref[s - 2] = dwp_ref[s - 2] + part

            _for_shifted(u0_ref, r0, tm, tuple(2 + k for k in range(nk)), tap_dw)
            du0 = acc[0]
            a = a_ref[pl.ds(r0, tm), :]
            sg = _sigmoid(gt_ref[pl.ds(r0, tm), :])
            da_ref[pl.ds(r0, tm), :] = (du0 * sg).astype(BF16)
            dgt_ref[pl.ds(r0, tm), :] = (du0 * a * sg * (1.0 - sg)).astype(BF16)
            return carry

        lax.fori_loop(0, t // tm, tile_b, 0)
        dw_ref[...] = jnp.sum(dwp_ref[...], axis=1)

    vec = pl.BlockSpec((1, c), lambda j: (0, j))
    vshape = jax.ShapeDtypeStruct((1, CONV_WIDTH), F32)
    return pl.pallas_call(
        body, name=name, grid=(N_GROUPS,),
        in_specs=[_col_spec(t, c), _col_spec(t, c), _col_spec(t, c), _col_spec(t, c, N_GROUPS),
                  pl.BlockSpec((nk, c), lambda j: (0, j)), vec, vec],
        out_specs=[_col_spec(t, c), _col_spec(t, c), pl.BlockSpec((32, c), lambda j: (0, j)), vec, vec, vec],
        out_shape=[jax.ShapeDtypeStruct((t, CONV_WIDTH), BF16), jax.ShapeDtypeStruct((t, CONV_WIDTH), BF16),
                   jax.ShapeDtypeStruct((32, CONV_WIDTH), F32), vshape, vshape, vshape],
        scratch_shapes=[pltpu.VMEM((t + pad, c), F32), pltpu.VMEM((t + pad, c), F32),
                        pltpu.VMEM((32, 8, c), F32)],
        compiler_params=_cparams(("parallel",)),
    )(dcat, u1, hin, hin, w, ng, nb_)


LEVELS = (64, 32, 16)
HGRN_UNROLL = 2
HGRN_UNROLL_FWD = 4
NT_DIMS = (((1,), (1,)), ((), ()))
NN_DIMS = (((1,), (0,)), ((), ()))
TN_DIMS = (((0,), (0,)), ((), ()))


def _bdot(a, b, dims):
    return lax.dot_general(a.astype(BF16), b.astype(BF16), dims, preferred_element_type=F32)


def _hdot(a, b):
    return jnp.dot(a, b, precision=lax.Precision.HIGHEST, preferred_element_type=F32)


def _chunk_consts():
    rid = lax.broadcasted_iota(jnp.int32, (CHUNK, GROUP), 0)
    ti = lax.broadcasted_iota(jnp.int32, (CHUNK, CHUNK), 0)
    si = lax.broadcasted_iota(jnp.int32, (CHUNK, CHUNK), 1)
    tri = (si <= ti).astype(F32)
    second = [(rid & (b // 2)) != 0 for b in LEVELS]
    same = [None] + [(ti // b) == (si // b) for b in LEVELS[1:]]
    sub = lax.broadcasted_iota(jnp.int32, (SUB, GROUP), 0)
    return rid, tri, second, same, sub


def _level_refs(cum_ref, rid, base):
    row = lambda i: cum_ref[pl.ds(base + i, 1), :]
    l1 = jnp.broadcast_to(row(31), (CHUNK, GROUP))
    l2 = jnp.where(rid < 32, row(15), row(47))
    l3 = jnp.where(rid < 16, row(7), jnp.where(rid < 32, row(23), jnp.where(rid < 48, row(39), row(55))))
    return l1, l2, l3


def _level_factors(cum, brefs, second):
    out = []
    for bref, sec in zip(brefs, second):
        eq = jnp.where(sec, jnp.exp(jnp.minimum(cum - bref, 0.0)), 0.0)
        ek = jnp.where(sec, 0.0, jnp.exp(jnp.minimum(bref - cum, 0.0)))
        out.append((eq, ek))
    return out


def _gates(q, f, lb):
    sq = _sigmoid(q)
    sf = _sigmoid(f)
    fg = lb + (1.0 - lb) * sf
    return q * sq, sq, sf, fg


def _hgrn_specs(t, nc):
    c = GROUP
    col = lambda off: pl.BlockSpec((t, c), lambda h: (0, h + off))
    hin_specs = [col(16), col(24), col(32), col(40)]
    vec = pl.BlockSpec((1, c), lambda h: (0, h))
    lbs = pl.BlockSpec((2, c), lambda h: (0, h))
    st = pl.BlockSpec((1, nc, c, c), lambda h: (h, 0, 0, 0))
    return col, hin_specs, vec, lbs, st


def _hgrn_fwd(name, hin, lb_logits, hg):
    t = hin.shape[0]
    nc = t // CHUNK
    c = GROUP
    col, hin_specs, vec, lbs, st = _hgrn_specs(t, nc)

    def body(q_ref, f_ref, v_ref, og_ref, lb_ref, hg_ref, o_ref, ob_ref, st_ref,
             s_ref, cum_ref, kk_ref, vc_ref):
        rid, tri, second, same, sub = _chunk_consts()
        lb = _sigmoid(lb_ref[pl.ds(0, 1), :] - lb_ref[pl.ds(1, 1), :])
        gain = hg_ref[...]
        s_ref[...] = jnp.zeros_like(s_ref)

        def chunk(ci, u):
            base = u * CHUNK
            r0 = pl.multiple_of(ci * CHUNK, CHUNK)
            rows = pl.ds(r0, CHUNK)
            qh, _, _, fg = _gates(q_ref[rows, :], f_ref[rows, :], lb)
            v = v_ref[rows, :]
            kk = 1.0 - fg
            cum = _hdot(tri, jnp.log(fg))
            cum_ref[pl.ds(base, CHUNK), :] = cum
            kk_ref[pl.ds(base, CHUNK), :] = kk
            vc_ref[pl.ds(base, CHUNK), :] = v
            sprev = s_ref[...]
            st_ref[0, ci] = sprev
            blast = cum_ref[pl.ds(base + CHUNK - 1, 1), :]
            o = _bdot(qh * jnp.exp(cum), sprev, NT_DIMS)
            s_ref[...] = sprev * jnp.exp(blast) + _bdot(v, kk * jnp.exp(blast - cum), TN_DIMS)
            a = None
            for (eq, ek), msk in zip(_level_factors(cum, _level_refs(cum_ref, rid, base), second), same):
                al = _bdot(qh * eq, kk * ek, NT_DIMS)
                al = al if msk is None else jnp.where(msk, al, 0.0)
                a = al if a is None else a + al
            o = o + _bdot(a, v, NN_DIMS)
            diag = []
            for sb in range(CHUNK // SUB):
                lo = sb * SUB
                qb = qh[lo:lo + SUB]
                cb = cum[lo:lo + SUB]
                od = jnp.zeros((SUB, c), F32)
                for s in range(SUB):
                    e = jnp.where(sub >= s, jnp.exp(jnp.minimum(cb - cum_ref[pl.ds(base + lo + s, 1), :], 0.0)), 0.0)
                    acol = jnp.sum(qb * e * kk_ref[pl.ds(base + lo + s, 1), :], axis=-1, keepdims=True)
                    od = od + acol * vc_ref[pl.ds(base + lo + s, 1), :]
                diag.append(od)
            o = o + jnp.concatenate(diag, axis=0)
            o_ref[rows, :] = o
            y = o * lax.rsqrt(jnp.mean(o * o, axis=-1, keepdims=True) + RMS_EPS) * gain
            og = og_ref[rows, :]
            ob_ref[rows, :] = (y * og * _sigmoid(og)).astype(BF16)

        def chunks(i, carry):
            for u in range(HGRN_UNROLL_FWD):
                chunk(i * HGRN_UNROLL_FWD + u, u)
            return carry

        lax.fori_loop(0, nc // HGRN_UNROLL_FWD, chunks, 0)

    return pl.pallas_call(
        body, name=name, grid=(N_GROUPS,),
        in_specs=hin_specs + [lbs, vec],
        out_specs=[col(0), col(N_GROUPS), st],
        out_shape=[jax.ShapeDtypeStruct((t, HGRN_WIDTH), F32), jax.ShapeDtypeStruct((t, CONV_WIDTH + HGRN_WIDTH), BF16),
                   jax.ShapeDtypeStruct((N_GROUPS, nc, c, c), F32)],
        scratch_shapes=[pltpu.VMEM((c, c), F32)] + [pltpu.VMEM((HGRN_UNROLL_FWD * CHUNK, c), F32)] * 3,
        compiler_params=_cparams(("parallel",)),
    )(hin, hin, hin, hin, lb_logits, hg)


def _hgrn_bwd(name, dcat, hin, o_raw, states, lb_logits, hg):
    t = hin.shape[0]
    nc = t // CHUNK
    c = GROUP
    col, hin_specs, vec, lbs, st = _hgrn_specs(t, nc)

    def body(do_ref, q_ref, f_ref, v_ref, og_ref, o_ref, st_ref, lb_ref, hg_ref,
             dq_ref, df_ref, dv_ref, dog_ref, dhg_ref, dlb_ref,
             ds_ref, cum_ref, kk_ref, vc_ref):
        rid, tri, second, same, sub = _chunk_consts()
        trit = tri.T
        lb = _sigmoid(lb_ref[pl.ds(0, 1), :] - lb_ref[pl.ds(1, 1), :])
        gain = hg_ref[...]
        ds_ref[...] = jnp.zeros_like(ds_ref)

        def chunk(i, carry, u):
            base = u * CHUNK
            dhg, dlb = carry
            ci = nc - 1 - i
            r0 = pl.multiple_of(ci * CHUNK, CHUNK)
            rows = pl.ds(r0, CHUNK)
            q = q_ref[rows, :]
            qh, sq, sf, fg = _gates(q, f_ref[rows, :], lb)
            v = v_ref[rows, :]
            kk = 1.0 - fg
            cum = _hdot(tri, jnp.log(fg))
            cum_ref[pl.ds(base, CHUNK), :] = cum
            kk_ref[pl.ds(base, CHUNK), :] = kk
            vc_ref[pl.ds(base, CHUNK), :] = v
            o = o_ref[rows, :]
            og = og_ref[rows, :]
            sg = _sigmoid(og)
            rinv = lax.rsqrt(jnp.mean(o * o, axis=-1, keepdims=True) + RMS_EPS)
            yn = o * rinv
            dof = do_ref[rows, :]
            dog_ref[rows, :] = (dof * yn * gain * _silu_grad(og, sg)).astype(BF16)
            dz = dof * og * sg
            dhg = dhg + jnp.sum(dz * yn, axis=0, keepdims=True)
            dy = dz * gain
            do = rinv * (dy - yn * jnp.mean(dy * yn, axis=-1, keepdims=True))
            sprev = st_ref[0, ci]
            dsn = ds_ref[...]
            blast = cum_ref[pl.ds(base + CHUNK - 1, 1), :]
            eq0 = jnp.exp(cum)
            ek0 = jnp.exp(blast - cum)
            dqh = _bdot(do, sprev, NN_DIMS) * eq0
            dkk = _bdot(v, dsn, NN_DIMS) * ek0
            dlast = (jnp.sum(kk * dkk, axis=0, keepdims=True)
                     + jnp.exp(blast) * jnp.sum(dsn * sprev, axis=0, keepdims=True))
            dv = _bdot(kk * ek0, dsn, NT_DIMS)
            ds_ref[...] = dsn * jnp.exp(blast) + _bdot(do, qh * eq0, TN_DIMS)
            dg = qh * dqh - kk * dkk
            da = _bdot(do, v, NT_DIMS)
            a = None
            for (eq, ek), msk in zip(_level_factors(cum, _level_refs(cum_ref, rid, base), second), same):
                ql, kl = (qh * eq).astype(BF16), (kk * ek).astype(BF16)
                al = _bdot(ql, kl, NT_DIMS)
                dal = da
                if msk is not None:
                    al = jnp.where(msk, al, 0.0)
                    dal = jnp.where(msk, da, 0.0)
                a = al if a is None else a + al
                dql = _bdot(dal, kl, NN_DIMS)
                dkl = _bdot(dal, ql, TN_DIMS)
                dqh = dqh + dql * eq
                dkk = dkk + dkl * ek
                dg = dg + (ql.astype(F32) * dql - kl.astype(F32) * dkl)
            dv = dv + _bdot(a, do, TN_DIMS)
            dq_d, dk_d, dv_d = [], [], []
            for sb in range(CHUNK // SUB):
                lo = sb * SUB
                qb = qh[lo:lo + SUB]
                cb = cum[lo:lo + SUB]
                dob = do[lo:lo + SUB]
                dqb = jnp.zeros((SUB, c), F32)
                dkb = jnp.zeros((SUB, c), F32)
                dvb = jnp.zeros((SUB, c), F32)
                for s in range(SUB):
                    e = jnp.where(sub >= s, jnp.exp(jnp.minimum(cb - cum_ref[pl.ds(base + lo + s, 1), :], 0.0)), 0.0)
                    ks = kk_ref[pl.ds(base + lo + s, 1), :]
                    qe = qb * e
                    dacol = jnp.sum(dob * vc_ref[pl.ds(base + lo + s, 1), :], axis=-1, keepdims=True)
                    acol = jnp.sum(qe * ks, axis=-1, keepdims=True)
                    dqb = dqb + dacol * (ks * e)
                    dkb = jnp.where(sub == s, jnp.sum(dacol * qe, axis=0, keepdims=True), dkb)
                    dvb = jnp.where(sub == s, jnp.sum(acol * dob, axis=0, keepdims=True), dvb)
                dq_d.append(dqb)
                dk_d.append(dkb)
                dv_d.append(dvb)
            dq_d = jnp.concatenate(dq_d, axis=0)
            dk_d = jnp.concatenate(dk_d, axis=0)
            dqh = dqh + dq_d
            dkk = dkk + dk_d
            dg = dg + (qh * dq_d - kk * dk_d)
            dv = dv + jnp.concatenate(dv_d, axis=0)
            dlf = _hdot(trit, dg) + dlast
            dfg = dlf / fg - dkk
            df_ref[rows, :] = (dfg * (1.0 - lb) * sf * (1.0 - sf)).astype(BF16)
            dlb = dlb + jnp.sum(dfg * (1.0 - sf), axis=0, keepdims=True)
            dq_ref[rows, :] = (dqh * _silu_grad(q, sq)).astype(BF16)
            dv_ref[rows, :] = dv.astype(BF16)
            return dhg, dlb

        def chunks(i, carry):
            for u in range(HGRN_UNROLL):
                carry = chunk(i * HGRN_UNROLL + u, carry, u)
            return carry

        zero = jnp.zeros((1, c), F32)
        dhg, dlb = lax.fori_loop(0, nc // HGRN_UNROLL, chunks, (zero, zero))
        dhg_ref[...] = dhg
        dl0 = dlb * lb * (1.0 - lb)
        dlb_ref[...] = jnp.where(lax.broadcasted_iota(jnp.int32, (2, c), 0) == 0, dl0, -dl0)

    big = jax.ShapeDtypeStruct((t, HGRN_WIDTH), BF16)
    return pl.pallas_call(
        body, name=name, grid=(N_GROUPS,),
        in_specs=[col(8)] + hin_specs + [col(0), st, lbs, vec],
        out_specs=[col(0)] * 4 + [vec, lbs],
        out_shape=[big] * 4 + [jax.ShapeDtypeStruct((1, HGRN_WIDTH), F32), jax.ShapeDtypeStruct((2, HGRN_WIDTH), F32)],
        scratch_shapes=[pltpu.VMEM((c, c), F32)] + [pltpu.VMEM((HGRN_UNROLL * CHUNK, c), F32)] * 3,
        compiler_params=_cparams(("parallel",)),
    )(dcat, hin, hin, hin, hin, o_raw, states, lb_logits, hg)


ANY = pl.BlockSpec(memory_space=pl.ANY)


def _my_place():
    return lax.axis_index("x"), lax.axis_index("y"), lax.axis_index("c")


HBM = pl.BlockSpec(memory_space=pltpu.HBM)
SEM = pl.BlockSpec(memory_space=pltpu.SEMAPHORE)
EFFECT = pltpu.SideEffectType.DATAFLOW_SIDE_EFFECTING


def _peer(k):
    x, y, c = _my_place()
    px = 1 - x if k & 4 else x
    py = 1 - y if k & 2 else y
    pc = 1 - c if k & 1 else c
    return (px, py, pc), 4 * px + 2 * py + pc


def _slot(land_ref, idx):
    if len(land_ref.shape) == 2:
        ns = land_ref.shape[1] // N_DEV
        return land_ref.at[:, pl.ds(pl.multiple_of(idx * ns, 128), ns)]
    return land_ref.at[idx]


def _exchange_copy(k, src_ref, land_ref, send_sems, recv_sems, scatter, landing):
    x, y, c = _my_place()
    me = 4 * x + 2 * y + c
    to, idx = _peer(k)
    return pltpu.make_async_remote_copy(
        src_ref=src_ref.at[idx] if scatter else src_ref,
        dst_ref=_slot(land_ref, idx) if landing else _slot(land_ref, me),
        send_sem=send_sems.at[k - 1], recv_sem=recv_sems.at[k - 1], device_id=to, device_id_type=MESH)


ALL_PEERS = tuple(range(1, N_DEV))
NEAR_PEERS = (1, 2, 4, 6)
SAME_CORE_PEERS = (2, 4, 6)


def _exchange_start(name, src, land, scatter, ks=ALL_PEERS):
    def body(src_ref, land_ref, send_sems, recv_sems, src_thru, land_thru, token):
        for k in ks:
            _exchange_copy(k, src_ref, land_ref, send_sems, recv_sems, scatter, landing=False).start()
        token[...] = jnp.zeros_like(token)

    send_sems, recv_sems, src_thru, land_thru, token = pl.pallas_call(
        body, name=name,
        out_shape=(pltpu.SemaphoreType.DMA((N_DEV - 1,)), pltpu.SemaphoreType.DMA((N_DEV - 1,)),
                   pltpu.HBM(src.shape, src.dtype), pltpu.HBM(land.shape, land.dtype),
                   jax.ShapeDtypeStruct((8, 128), F32)),
        in_specs=(HBM, HBM), out_specs=(SEM, SEM, HBM, HBM, pl.BlockSpec(memory_space=pltpu.VMEM)),
        input_output_aliases={0: 2, 1: 3},
        compiler_params=pltpu.CompilerParams(has_side_effects=EFFECT),
    )(pltpu.with_memory_space_constraint(src, pltpu.HBM), pltpu.with_memory_space_constraint(land, pltpu.HBM))
    return (send_sems, recv_sems, src_thru, land_thru, scatter, ks), token


def _exchange_wait(name, handle, after):
    send_sems, recv_sems, src_thru, land_thru, scatter, ks = handle

    def body(src_ref, land_ref, send_sems, recv_sems, after_ref, src_dead, got_ref):
        for k in ks:
            cp = _exchange_copy(k, src_ref, land_ref, send_sems, recv_sems, scatter, landing=True)
            cp.wait_send()
            cp.wait_recv()

    return pl.pallas_call(
        body, name=name,
        out_shape=(pltpu.HBM(src_thru.shape, src_thru.dtype), pltpu.HBM(land_thru.shape, land_thru.dtype)),
        in_specs=(HBM, HBM, SEM, SEM, ANY), out_specs=(HBM, HBM), input_output_aliases={0: 0, 1: 1},
        compiler_params=pltpu.CompilerParams(has_side_effects=EFFECT),
    )(src_thru, land_thru, send_sems, recv_sems, after)[1]


def _relay_copy(j, land_ref, send_sems, recv_sems, landing):
    x, y, c = _my_place()
    k = SAME_CORE_PEERS[j]
    _, sent = _peer(k)
    _, got = _peer(k + 1)
    return pltpu.make_async_remote_copy(
        src_ref=_slot(land_ref, sent), dst_ref=_slot(land_ref, got) if landing else _slot(land_ref, sent),
        send_sem=send_sems.at[j], recv_sem=recv_sems.at[j], device_id=(x, y, 1 - c), device_id_type=MESH)


def _relay_start(name, land):
    n = len(SAME_CORE_PEERS)

    def body(land_ref, send_sems, recv_sems, land_thru, token):
        for j in range(n):
            _relay_copy(j, land_ref, send_sems, recv_sems, landing=False).start()
        token[...] = jnp.zeros_like(token)

    send_sems, recv_sems, land_thru, token = pl.pallas_call(
        body, name=name,
        out_shape=(pltpu.SemaphoreType.DMA((n,)), pltpu.SemaphoreType.DMA((n,)),
                   pltpu.HBM(land.shape, land.dtype), jax.ShapeDtypeStruct((8, 128), F32)),
        in_specs=(HBM,), out_specs=(SEM, SEM, HBM, pl.BlockSpec(memory_space=pltpu.VMEM)),
        input_output_aliases={0: 2},
        compiler_params=pltpu.CompilerParams(has_side_effects=EFFECT),
    )(pltpu.with_memory_space_constraint(land, pltpu.HBM))
    return (send_sems, recv_sems, land_thru), token


def _relay_wait(name, handle, after):
    send_sems, recv_sems, land_thru = handle

    def body(land_ref, send_sems, recv_sems, after_ref, got_ref):
        for j in range(len(SAME_CORE_PEERS)):
            cp = _relay_copy(j, land_ref, send_sems, recv_sems, landing=True)
            cp.wait_send()
            cp.wait_recv()

    return pl.pallas_call(
        body, name=name, out_shape=pltpu.HBM(land_thru.shape, land_thru.dtype),
        in_specs=(HBM, SEM, SEM, ANY), out_specs=HBM, input_output_aliases={0: 0},
        compiler_params=pltpu.CompilerParams(has_side_effects=EFFECT),
    )(land_thru, send_sems, recv_sems, after)


def _own_cols(own, me):
    r, ns = own.shape
    land = lax.empty((r, N_DEV * ns), own.dtype)
    return lax.dynamic_update_slice_in_dim(land, own, me * ns, axis=1)


def _own_slot(own, me):
    land = lax.empty((N_DEV,) + own.shape, own.dtype)
    return lax.dynamic_update_slice_in_dim(land, own[None], me, axis=0)


def _adamw_math(w, g, m, v):
    m = ADAM_B1 * m + (1.0 - ADAM_B1) * g
    v = ADAM_B2 * v + (1.0 - ADAM_B2) * (g * g)
    m_hat = m / (1.0 - ADAM_B1 ** ADAM_STEP)
    v_hat = v / (1.0 - ADAM_B2 ** ADAM_STEP)
    delta = -ADAM_LR * (m_hat / (jnp.sqrt(v_hat) + ADAM_EPS) + ADAM_WD * w)
    return delta, m, v


def _adamw_sum(name, recv, w, m, v, tr, row0=0, partial=None):
    r, c = w.shape
    rr = recv.shape[1]
    off = row0 // tr

    def body(recv_ref, w_ref, m_ref, v_ref, *refs):
        g_ref, d_ref, mo_ref, vo_ref = refs[-4:]
        g = recv_ref[0].astype(F32)
        for j in range(1, N_DEV):
            g = g + recv_ref[j].astype(F32)
        g_ref[...] = g
        d_ref[...], mo_ref[...], vo_ref[...] = _adamw_math(w_ref[...], g, m_ref[...], v_ref[...])

    tile = pl.BlockSpec((tr, c), lambda i: (i + off, 0))
    out = jax.ShapeDtypeStruct((r, c), F32)
    prev = list(partial) if partial is not None else []
    return pl.pallas_call(
        body, name=name, grid=(rr // tr,),
        in_specs=[pl.BlockSpec((N_DEV, tr, c), lambda i: (0, i, 0)), tile, tile, tile] + [ANY] * len(prev),
        out_specs=[tile] * 4, out_shape=[out] * 4,
        input_output_aliases={4 + i: i for i in range(len(prev))},
        compiler_params=_cparams(("parallel",)),
    )(recv, w, m, v, *prev)


def _sum_parts(name, parts):
    _, r, c = parts.shape

    def body(p_ref, o_ref):
        acc = p_ref[0]
        for j in range(1, N_DEV):
            acc = acc + p_ref[j]
        o_ref[...] = acc

    return pl.pallas_call(body, name=name, out_shape=jax.ShapeDtypeStruct((r, c), F32),
                          compiler_params=_cparams())(parts)


def _adamw_small(name, w, g, m, v):
    def body(w_ref, g_ref, m_ref, v_ref, d_ref, mo_ref, vo_ref):
        d_ref[...], mo_ref[...], vo_ref[...] = _adamw_math(w_ref[...], g_ref[...], m_ref[...], v_ref[...])

    out = jax.ShapeDtypeStruct(w.shape, F32)
    return pl.pallas_call(body, name=name, out_shape=[out] * 3, compiler_params=_cparams())(w, g, m, v)


def _pack(pieces, rows):
    flat = jnp.concatenate([p.reshape(-1).astype(F32) for p in pieces])
    return jnp.pad(flat, (0, rows * 128 - flat.shape[0])).reshape(rows, 128)


def _unpack(packed, shapes):
    flat = packed.reshape(-1)
    out, off = [], 0
    for s in shapes:
        n = 1
        for d in s:
            n *= d
        out.append(flat[off:off + n].reshape(s))
        off += n
    return out


def kernel(x, emb_ln_g, emb_ln_b, w_in, conv_w, conv_b, conv_norm_g, conv_norm_b, lb_logits, hgrn_norm_g, w_out, ln1_g, ln1_b, w_ffn_up, ffn_conv_w, ffn_conv_b, w_ffn_down, ln2_g, ln2_b, loss_target, m_emb_ln_g, m_emb_ln_b, m_w_in, m_conv_w, m_conv_b, m_conv_norm_g, m_conv_norm_b, m_lb_logits, m_hgrn_norm_g, m_w_out, m_ln1_g, m_ln1_b, m_w_ffn_up, m_ffn_conv_w, m_ffn_conv_b, m_w_ffn_down, m_ln2_g, m_ln2_b, v_emb_ln_g, v_emb_ln_b, v_w_in, v_conv_w, v_conv_b, v_conv_norm_g, v_conv_norm_b, v_lb_logits, v_hgrn_norm_g, v_w_out, v_ln1_g, v_ln1_b, v_w_ffn_up, v_ffn_conv_w, v_ffn_conv_b, v_w_ffn_down, v_ln2_g, v_ln2_b):
    t = x.shape[1]
    me = 4 * lax.axis_index("x") + 2 * lax.axis_index("y") + lax.axis_index("c")
    x2, tgt = x[0], loss_target[0]
    ns_in, ns_up = w_in.shape[2], w_ffn_up.shape[2]
    rs_out, rs_down = w_out.shape[1], w_ffn_down.shape[1]
    cs, fs = conv_w.shape[2], ffn_conv_w.shape[2]

    def gather_start(name, w, prev, ks=ALL_PEERS, cols=False):
        shard = (w[0] + prev).astype(BF16)
        land = _own_cols(shard, me) if cols else _own_slot(shard, me)
        return _exchange_start(name, shard, land, scatter=False, ks=ks)

    h_in, tok = gather_start("ag_w_in_start", w_in, 0.0, NEAR_PEERS, cols=True)
    taps = _pack([conv_w[0], ffn_conv_w[0]], 48) + tok[0, 0]
    h_taps, tok = _exchange_start("ag_taps_start", taps, _own_slot(taps, me), scatter=False)
    h_out, tok = gather_start("ag_w_out_start", w_out, tok[0, 0])
    h_up, tok = gather_start("ag_w_up_start", w_ffn_up, tok[0, 0], NEAR_PEERS, cols=True)
    h_down, tok = gather_start("ag_w_down_start", w_ffn_down, tok[0, 0])

    row = lambda a: a.reshape(1, -1)

    _, h0, h0b, h0bt = _ln_fwd("ln_in", x2, None, row(emb_ln_g) + tok[0, 0], row(emb_ln_b), 1.0)
    h_relay, tok_relay = _relay_start("ag_w_in_relay_start", _exchange_wait("ag_w_in_wait", h_in, h0b))
    win_n = _relay_wait("ag_w_in_relay_wait", h_relay, tok_relay)
    hin = _mm_nn("mm_in", h0b, win_n, F32, tm=1024, tn=ns_in, tk=D_MODEL)
    n_cw, n_fw = CONV_KERNEL * cs, FFN_KERNEL * fs
    taps_g = _exchange_wait("ag_taps_wait", h_taps, hin).reshape(N_DEV, -1)
    cw_full = taps_g[:, :n_cw].reshape(N_DEV, CONV_KERNEL, cs).transpose(1, 0, 2).reshape(CONV_KERNEL, CONV_WIDTH)
    fw_full = taps_g[:, n_cw:n_cw + n_fw].reshape(N_DEV, FFN_KERNEL, fs).transpose(1, 0, 2).reshape(FFN_KERNEL, D_FF)

    o_raw, cat_right, states = _hgrn_fwd("hgrn_fwd", hin, lb_logits, hgrn_norm_g)
    u1, catb = _conv_fwd("conv_fwd", hin, cw_full, conv_b, conv_norm_g, conv_norm_b, cat_right)
    wout_g = _exchange_wait("ag_w_out_wait", h_out, catb).reshape(D_MODEL, D_MODEL)
    h_up_relay, tok = _relay_start("ag_w_up_relay_start", _exchange_wait("ag_w_up_wait", h_up, wout_g))
    mix = _mm_nn("mm_out", catb, wout_g, F32, tm=1024, tn=1024, tk=D_MODEL, after=tok)
    r1, h1, h1b, h1bt = _ln_fwd("ln1", h0, mix, ln1_g, ln1_b, ALPHA)
    wup_n = _relay_wait("ag_w_up_relay_wait", h_up_relay, h1b)
    hf = _mm_nn("mm_up", h1b, wup_n, BF16, tm=1024, tn=1024, tk=D_MODEL)
    actb = _ffn_act_fwd("ffn_act", hf, fw_full, ffn_conv_b)
    wdown_g = _exchange_wait("ag_w_down_wait", h_down, actb).reshape(D_FF, D_MODEL)
    ffn = _mm_nn("mm_down", actb, wdown_g, F32, tm=512, tn=1024, tk=D_FF)
    dr2, dr2b, g_ln2g, g_ln2b, loss = _ln2_loss_bwd("ln2_loss", h1, ffn, ln2_g, ln2_b, tgt)

    def scatter_start(name, parts):
        own = lax.dynamic_index_in_dim(parts, me, axis=0, keepdims=False)
        return _exchange_start(name, parts, _own_slot(own, me), scatter=True)

    dact = _mm_nt("mm_dact", dr2b, wdown_g, BF16, tm=1024, tn=D_FF // 2, tk=D_MODEL)
    gw_down = _matmul(
        "mm_dw_down", actb, dr2b, (D_FF, D_MODEL), BF16, (N_DEV // 2, D_MODEL // 1024, 2),
        pl.BlockSpec((t // 2, 2 * rs_down), lambda i, j, kk: (kk, i)),
        pl.BlockSpec((t // 2, 1024), lambda i, j, kk: (kk, j)),
        pl.BlockSpec((2 * rs_down, 1024), lambda i, j, kk: (i, j)), nt="tn")
    s_down, tok = scatter_start("a2a_w_down_start", gw_down.reshape(N_DEV, rs_down, D_MODEL))
    dhf, g_fw, g_fb = _ffn_act_bwd("ffn_act_bwd", dact, hf, fw_full, ffn_conv_b + tok[0, 0])
    tm = min(1024, t)
    gw_up = _matmul(
        "mm_dw_up", h1bt, dhf, (N_DEV, D_MODEL, ns_up), BF16, (D_MODEL // 1024, N_DEV, 1),
        pl.BlockSpec((1024, t), lambda i, j, kk: (i, 0)),
        pl.BlockSpec((1, t, ns_up), lambda i, j, kk: (j // 4, 0, j % 4)),
        pl.BlockSpec((1, 1024, ns_up), lambda i, j, kk: (j, i, 0)), nt=False)
    s_up, tok = scatter_start("a2a_w_up_start", gw_up)
    tkf = D_FF // 2
    dh1 = _matmul(
        "mm_dh1", dhf, wup_n, (t, D_MODEL), F32, (t // tm, D_MODEL // 1024, 4),
        pl.BlockSpec((1, tm, tkf), lambda i, j, kk: (kk // 2, i, kk % 2)),
        pl.BlockSpec((1024, tkf), lambda i, j, kk: (j, kk)),
        pl.BlockSpec((tm, 1024), lambda i, j, kk: (i, j)), nt=True, after=tok)
    dr1, dr1b, g_ln1g, g_ln1b = _ln_bwd("ln1_bwd", r1, dr2, dh1, ln1_g + tok[0, 0], ALPHA, True)
    gw_out = _mm_nn("mm_dw_out", catb.T, dr1b, BF16, tm=1024, tn=1024, tk=t)
    s_out, tok = scatter_start("a2a_w_out_start", gw_out.reshape(N_DEV, rs_out, D_MODEL))
    dcat = _mm_nt("mm_dcat", dr1b, wout_g, F32, tm=1024, tn=1024, tk=D_MODEL, after=tok)
    da, dgate, g_cw, g_cb, g_cng, g_cnb = _conv_bwd("conv_bwd", dcat, u1, hin, cw_full, conv_norm_g + tok[0, 0],
                                                    conv_norm_b)
    dq, df, di, dog, g_hg, g_lb = _hgrn_bwd("hgrn_bwd", dcat, hin, o_raw, states, lb_logits, hgrn_norm_g)
    dhin = jnp.concatenate([da, dgate, dq, df, di, dog], axis=1)
    half = D_MODEL // 2
    gw_in_a = _mm_grad_cols("mm_dw_in_a", h0bt, dhin, ns_in, 0, half, after=tok)
    s_in_a, tok = scatter_start("a2a_w_in_a_start", gw_in_a)
    gw_in_b = _mm_grad_cols("mm_dw_in_b", h0bt, dhin, ns_in, half, half, after=tok)
    s_in_b, tok = scatter_start("a2a_w_in_b_start", gw_in_b)
    dh0 = _mm_nt("mm_dh0", dhin, win_n, F32, tm=1024, tn=1024, tk=IN_PROJ // 2, after=tok)
    grad_x, g_eg, g_eb = _ln_bwd("ln_in_bwd", x2, dr1, dh0, row(emb_ln_g), ALPHA, False)

    small_shapes = [(D_MODEL,), (D_MODEL,), (CONV_KERNEL, CONV_WIDTH), (1, CONV_WIDTH), (1, CONV_WIDTH),
                    (1, CONV_WIDTH), (2, HGRN_WIDTH), (1, HGRN_WIDTH), (1, D_MODEL), (1, D_MODEL),
                    (FFN_KERNEL, D_FF), (1, D_FF), (1, D_MODEL), (1, D_MODEL), (128,)]
    rows_small = 569
    packed = _pack([g_eg, g_eb, g_cw[:CONV_KERNEL], g_cb, g_cng, g_cnb, g_lb, g_hg, g_ln1g, g_ln1b,
                    g_fw[:FFN_KERNEL], g_fb, g_ln2g, g_ln2b, loss], rows_small)
    h_small, tok = _exchange_start("ag_small_start", packed, _own_slot(packed, me), scatter=False)

    def big(name, handle, after, w, m, v, tr):
        recv = _exchange_wait("a2a_" + name + "_wait", handle, after)
        return [o[None] for o in _adamw_sum("adamw_" + name, recv, w[0], m[0], v[0], tr)]

    u_down = big("w_down", s_down, tok, w_ffn_down, m_w_ffn_down, v_w_ffn_down, 64)
    u_up = big("w_up", s_up, u_down[1], w_ffn_up, m_w_ffn_up, v_w_ffn_up, 64)
    u_out = big("w_out", s_out, u_up[1], w_out, m_w_out, v_w_out, 64)
    summed = _sum_parts("sum_small", _exchange_wait("ag_small_wait", h_small, u_out[1]))
    (s_eg, s_eb, s_cw, s_cb, s_cng, s_cnb, s_lb, s_hg, s_l1g, s_l1b, s_fw, s_fb, s_l2g, s_l2b,
     s_loss) = _unpack(summed, small_shapes)
    s_cw = lax.dynamic_slice_in_dim(s_cw, me * cs, cs, axis=1)[None]
    s_fw = lax.dynamic_slice_in_dim(s_fw, me * fs, fs, axis=1)[None]
    g_small = [s_eg, s_eb, s_cw, s_cb, s_cng, s_cnb, s_lb, s_hg, s_l1g, s_l1b, s_fw, s_fb, s_l2g, s_l2b]
    w_small = [emb_ln_g, emb_ln_b, conv_w, conv_b, conv_norm_g, conv_norm_b, lb_logits, hgrn_norm_g,
               ln1_g, ln1_b, ffn_conv_w, ffn_conv_b, ln2_g, ln2_b]
    m_small = [m_emb_ln_g, m_emb_ln_b, m_conv_w, m_conv_b, m_conv_norm_g, m_conv_norm_b, m_lb_logits,
               m_hgrn_norm_g, m_ln1_g, m_ln1_b, m_ffn_conv_w, m_ffn_conv_b, m_ln2_g, m_ln2_b]
    v_small = [v_emb_ln_g, v_emb_ln_b, v_conv_w, v_conv_b, v_conv_norm_g, v_conv_norm_b, v_lb_logits,
               v_hgrn_norm_g, v_ln1_g, v_ln1_b, v_ffn_conv_w, v_ffn_conv_b, v_ln2_g, v_ln2_b]
    rows_own = 236
    shapes_own = [w.shape for w in w_small]
    upd = _adamw_small("adamw_small", _pack(w_small, rows_own), _pack(g_small, rows_own),
                       _pack(m_small, rows_own), _pack(v_small, rows_own))
    d_small, nm_small, nv_small = (_unpack(u, shapes_own) for u in upd)
    g_small = [g.reshape(s) for g, s in zip(g_small, shapes_own)]

    recv_a = _exchange_wait("a2a_w_in_a_wait", s_in_a, upd[0])
    part = _adamw_sum("adamw_w_in_a", recv_a, w_in[0], m_w_in[0], v_w_in[0], 128)
    recv_b = _exchange_wait("a2a_w_in_b_wait", s_in_b, part[1])
    u_in = [o[None] for o in _adamw_sum("adamw_w_in_b", recv_b, w_in[0], m_w_in[0], v_w_in[0], 128,
                                        row0=half, partial=part)]

    def ordered(small, i_in, i_out, i_up, i_down):
        (eg, eb, cw, cb, cng, cnb, lb, hg, l1g, l1b, fw, fb, l2g, l2b) = small
        return [eg, eb, i_in, cw, cb, cng, cnb, lb, hg, i_out, l1g, l1b, i_up, fw, fb, i_down, l2g, l2b]

    outs = [s_loss[0], grad_x[None]]
    for k, small in enumerate([g_small, d_small, nm_small, nv_small]):
        outs += ordered(small, u_in[k], u_out[k], u_up[k], u_down[k])
    return tuple(outs)
```

```python
import functools

import jax
import jax.numpy as jnp
from jax import lax
from jax.experimental import pallas as pl
from jax.experimental.pallas import tpu as pltpu

F32 = jnp.float32
BF16 = jnp.bfloat16

N_DEV = 8
D_MODEL = 2048
CONV_WIDTH = 1024
CONV_KERNEL = 31
HGRN_WIDTH = 1024
GROUP = 128
N_GROUPS = 8
IN_PROJ = 2 * CONV_WIDTH + 4 * HGRN_WIDTH
D_FF = 5632
FFN_KERNEL = 3
CHUNK = 64
SUB = 8
LN_EPS = 1e-5
RMS_EPS = 1e-6
ALPHA = 2.0 ** 0.25
ADAM_LR, ADAM_B1, ADAM_B2, ADAM_EPS, ADAM_WD, ADAM_STEP = 0.001, 0.9, 0.999, 1e-08, 0.01, 10

VMEM_LIMIT = 56 * 1024 * 1024
MESH = pl.DeviceIdType.MESH


def _cparams(sem=None):
    return pltpu.CompilerParams(dimension_semantics=sem, vmem_limit_bytes=VMEM_LIMIT)


def _sigmoid(x):
    return 1.0 / (1.0 + jnp.exp(-x))


def _matmul(name, a, b, out_shape, out_dtype, grid, a_spec, b_spec, o_spec, nt, after=None):
    nk = grid[2]
    dims = {True: (((1,), (1,)), ((), ())), False: (((1,), (0,)), ((), ())), "tn": (((0,), (0,)), ((), ()))}[nt]
    extra = [] if after is None else [after]

    def body(a_ref, b_ref, *rest):
        o_ref, *scratch = rest[len(extra):]
        av = a_ref[0] if len(a_ref.shape) == 3 else a_ref[...]
        bv = b_ref[0] if len(b_ref.shape) == 3 else b_ref[...]
        part = lax.dot_general(av, bv, dims, preferred_element_type=F32)

        def write(res):
            if len(o_ref.shape) == 3:
                o_ref[0] = res.astype(out_dtype)
            else:
                o_ref[...] = res.astype(out_dtype)

        if nk == 1:
            write(part)
            return
        acc_ref, = scratch
        k = pl.program_id(2)

        @pl.when(k == 0)
        def _():
            acc_ref[...] = part

        @pl.when(jnp.logical_and(k > 0, k < nk - 1))
        def _():
            acc_ref[...] += part

        @pl.when(k == nk - 1)
        def _():
            write(acc_ref[...] + part)

    acc_shape = o_spec.block_shape[-2:]
    assert all(g >= 1 for g in grid), (name, grid)
    return pl.pallas_call(
        body, name=name, grid=grid, in_specs=[a_spec, b_spec] + [pl.BlockSpec(memory_space=pl.ANY)] * len(extra),
        out_specs=o_spec, out_shape=jax.ShapeDtypeStruct(out_shape, out_dtype),
        scratch_shapes=[pltpu.VMEM(acc_shape, F32)] if nk > 1 else [],
        compiler_params=_cparams(("parallel", "parallel", "arbitrary")),
    )(a, b, *extra)


def _mm_nn(name, a, w, out_dtype, tm, tn, tk, after=None):
    m, k = a.shape
    tm, tk = min(tm, m), min(tk, k)
    n = w.shape[1]
    return _matmul(
        name, a, w, (m, n), out_dtype, (m // tm, n // tn, k // tk),
        pl.BlockSpec((tm, tk), lambda i, j, kk: (i, kk)),
        pl.BlockSpec((tk, tn), lambda i, j, kk: (kk, j)),
        pl.BlockSpec((tm, tn), lambda i, j, kk: (i, j)), nt=False, after=after)


def _mm_nt(name, a, w, out_dtype, tm, tn, tk, after=None):
    m, k = a.shape
    tm = min(tm, m)
    n = w.shape[0]
    return _matmul(
        name, a, w, (m, n), out_dtype, (m // tm, n // tn, k // tk),
        pl.BlockSpec((tm, tk), lambda i, j, kk: (i, kk)),
        pl.BlockSpec((tn, tk), lambda i, j, kk: (j, kk)),
        pl.BlockSpec((tm, tn), lambda i, j, kk: (i, j)), nt=True, after=after)


def _mm_grad_cols(name, at, b, ns, row0, rows, after, tm=1024, tk=4096):
    t = at.shape[1]
    tk = min(tk, t)
    off = row0 // tm
    return _matmul(
        name, at, b, (N_DEV, rows, ns), BF16, (rows // tm, N_DEV, t // tk),
        pl.BlockSpec((tm, tk), lambda i, j, kk: (i + off, kk)),
        pl.BlockSpec((tk, ns), lambda i, j, kk: (kk, j)),
        pl.BlockSpec((1, tm, ns), lambda i, j, kk: (j, i, 0)), nt=False, after=after)


LN_ROWS = 256


def _ln_stats(r):
    mu = jnp.mean(r, axis=-1, keepdims=True)
    xc = r - mu
    var = jnp.mean(xc * xc, axis=-1, keepdims=True)
    rstd = lax.rsqrt(var + LN_EPS)
    return xc * rstd, rstd


def _row_spec(d):
    return pl.BlockSpec((LN_ROWS, d), lambda i: (i, 0))


def _vec_spec(d):
    return pl.BlockSpec((1, d), lambda i: (0, 0))


def _ln_fwd(name, a, m, g, b, alpha):
    t, d = a.shape
    has_m = m is not None

    def body(*refs):
        if has_m:
            a_ref, m_ref, g_ref, b_ref, r_ref, y_ref, yb_ref, yt_ref = refs
            r = alpha * a_ref[...] + m_ref[...]
            r_ref[...] = r
        else:
            a_ref, g_ref, b_ref, y_ref, yb_ref, yt_ref = refs
            r = a_ref[...]
        xhat, _ = _ln_stats(r)
        y = xhat * g_ref[...] + b_ref[...]
        y_ref[...] = y
        yb_ref[...] = y.astype(BF16)
        yt_ref[...] = y.T.astype(BF16)

    ins = [a] + ([m] if has_m else []) + [g, b]
    in_specs = [_row_spec(d)] * (2 if has_m else 1) + [_vec_spec(d)] * 2
    outs = ([jax.ShapeDtypeStruct((t, d), F32)] if has_m else []) + [
        jax.ShapeDtypeStruct((t, d), F32), jax.ShapeDtypeStruct((t, d), BF16), jax.ShapeDtypeStruct((d, t), BF16)]
    res = pl.pallas_call(
        body, name=name, grid=(t // LN_ROWS,), in_specs=in_specs,
        out_specs=[_row_spec(d)] * (len(outs) - 1) + [pl.BlockSpec((d, LN_ROWS), lambda i: (0, i))], out_shape=outs,
        compiler_params=_cparams(("parallel",)),
    )(*ins)
    return res if has_m else (None, *res)


def _ln_bwd_math(r, dy, g):
    xhat, rstd = _ln_stats(r)
    dxhat = dy * g
    m1 = jnp.mean(dxhat, axis=-1, keepdims=True)
    m2 = jnp.mean(dxhat * xhat, axis=-1, keepdims=True)
    dr = rstd * (dxhat - m1 - xhat * m2)
    return dr, jnp.sum(dy * xhat, axis=0, keepdims=True), jnp.sum(dy, axis=0, keepdims=True)


def _ln2_loss_bwd(name, h1, ffn, g, b, tgt):
    t, d = h1.shape

    def body(h1_ref, f_ref, g_ref, b_ref, t_ref, dr_ref, drb_ref, dg_ref, db_ref, loss_ref):
        @pl.when(pl.program_id(0) == 0)
        def _():
            dg_ref[...] = jnp.zeros_like(dg_ref)
            db_ref[...] = jnp.zeros_like(db_ref)
            loss_ref[...] = jnp.zeros_like(loss_ref)

        r = ALPHA * h1_ref[...] + f_ref[...]
        xhat, _ = _ln_stats(r)
        e = xhat * g_ref[...] + b_ref[...] - t_ref[...]
        loss_ref[...] += 0.5 / d * jnp.sum(e * e)
        dr, dg, db = _ln_bwd_math(r, e * (1.0 / d), g_ref[...])
        dr_ref[...] = dr
        drb_ref[...] = dr.astype(BF16)
        dg_ref[...] += dg
        db_ref[...] += db

    return pl.pallas_call(
        body, name=name, grid=(t // LN_ROWS,),
        in_specs=[_row_spec(d), _row_spec(d), _vec_spec(d), _vec_spec(d), _row_spec(d)],
        out_specs=[_row_spec(d), _row_spec(d), _vec_spec(d), _vec_spec(d), _vec_spec(128)],
        out_shape=[jax.ShapeDtypeStruct((t, d), F32), jax.ShapeDtypeStruct((t, d), BF16),
                   jax.ShapeDtypeStruct((1, d), F32), jax.ShapeDtypeStruct((1, d), F32),
                   jax.ShapeDtypeStruct((1, 128), F32)],
        compiler_params=_cparams(("arbitrary",)),
    )(h1, ffn, g, b, tgt)


def _ln_bwd(name, r, dya, dyb, g, alpha, want_bf16):
    t, d = r.shape

    def body(r_ref, dya_ref, dyb_ref, g_ref, *outs):
        dr_ref = outs[0]
        dg_ref, db_ref = outs[-2:]

        @pl.when(pl.program_id(0) == 0)
        def _():
            dg_ref[...] = jnp.zeros_like(dg_ref)
            db_ref[...] = jnp.zeros_like(db_ref)

        dy = alpha * dya_ref[...] + dyb_ref[...]
        dr, dg, db = _ln_bwd_math(r_ref[...], dy, g_ref[...])
        dr_ref[...] = dr
        if want_bf16:
            outs[1][...] = dr.astype(BF16)
        dg_ref[...] += dg
        db_ref[...] += db

    big = [jax.ShapeDtypeStruct((t, d), F32)] + ([jax.ShapeDtypeStruct((t, d), BF16)] if want_bf16 else [])
    return pl.pallas_call(
        body, name=name, grid=(t // LN_ROWS,),
        in_specs=[_row_spec(d)] * 3 + [_vec_spec(d)],
        out_specs=[_row_spec(d)] * len(big) + [_vec_spec(d)] * 2,
        out_shape=big + [jax.ShapeDtypeStruct((1, d), F32)] * 2,
        compiler_params=_cparams(("arbitrary",)),
    )(r, dya, dyb, g)


CONV_ROWS = 64
CONV_UNROLL = 4
FFN_UNROLL = 2


def _unrolled(n, unroll, fn, init):
    def body(i, carry):
        for u in range(unroll):
            carry = fn(i * unroll + u, carry)
        return carry

    return lax.fori_loop(0, n // unroll, body, init)


def _for_shifted(ref, r0, tm, shifts, fn):
    for s in shifts:
        fn(s, ref[pl.ds(r0 + s, tm), :])


def _col_spec(t, cb, off=0):
    return pl.BlockSpec((t, cb), lambda j: (0, j + off))


def _ffn_act_fwd(name, hf, w, b, cb=128):
    t = hf.shape[0]
    f = hf.shape[1] // 2
    nb = f // cb
    tm = CONV_ROWS

    def body(g_ref, v_ref, w_ref, b_ref, act_ref, pad_ref):
        pad_ref[pl.ds(0, 8), :] = jnp.zeros((8, cb), F32)
        pad_ref[pl.ds(8, t), :] = g_ref[...].astype(F32)
        wv = [w_ref[pl.ds(k, 1), :] for k in range(FFN_KERNEL)]
        bias = b_ref[...]

        def tile(i, carry):
            r0 = pl.multiple_of(i * tm, tm)
            acc = [jnp.broadcast_to(bias, (tm, cb))]

            def tap(s, rows):
                acc[0] = acc[0] + wv[s - 6] * rows

            _for_shifted(pad_ref, r0, tm, (6, 7, 8), tap)
            gc = acc[0]
            act_ref[pl.ds(r0, tm), :] = (gc * _sigmoid(gc) * v_ref[pl.ds(r0, tm), :].astype(F32)).astype(BF16)
            return carry

        _unrolled(t // tm, FFN_UNROLL, tile, 0)

    return pl.pallas_call(
        body, name=name, grid=(nb,),
        in_specs=[_col_spec(t, cb), _col_spec(t, cb, nb),
                  pl.BlockSpec((FFN_KERNEL, cb), lambda j: (0, j)), pl.BlockSpec((1, cb), lambda j: (0, j))],
        out_specs=_col_spec(t, cb), out_shape=jax.ShapeDtypeStruct((t, f), BF16),
        scratch_shapes=[pltpu.VMEM((t + 8, cb), F32)],
        compiler_params=_cparams(("parallel",)),
    )(hf, hf, w, b)


def _ffn_act_bwd(name, dact, hf, w, b, cb=128):
    t = hf.shape[0]
    f = hf.shape[1] // 2
    nb = f // cb
    tm = CONV_ROWS

    def body(da_ref, g_ref, v_ref, w_ref, b_ref, dhf_ref, dw_ref, db_ref, pad_ref, dgc_ref):
        pad_ref[pl.ds(0, 8), :] = jnp.zeros((8, cb), F32)
        pad_ref[pl.ds(8, t), :] = g_ref[...].astype(F32)
        dgc_ref[pl.ds(t, 8), :] = jnp.zeros((8, cb), F32)
        wv = [w_ref[pl.ds(k, 1), :] for k in range(FFN_KERNEL)]
        bias = b_ref[...]

        def tile_a(i, carry):
            r0 = pl.multiple_of(i * tm, tm)
            taps = {}
            _for_shifted(pad_ref, r0, tm, (6, 7, 8), lambda s, rows: taps.__setitem__(s, rows))
            gc = bias + wv[0] * taps[6] + wv[1] * taps[7] + wv[2] * taps[8]
            sg = _sigmoid(gc)
            da = da_ref[pl.ds(r0, tm), :].astype(F32)
            dhf_ref[1, pl.ds(r0, tm), :] = (da * gc * sg).astype(BF16)
            dgc = da * v_ref[pl.ds(r0, tm), :].astype(F32) * sg * (1.0 + gc * (1.0 - sg))
            dgc_ref[pl.ds(r0, tm), :] = dgc
            sums = [jnp.sum(dgc * taps[6 + k], axis=0, keepdims=True) for k in range(3)]
            sums.append(jnp.sum(dgc, axis=0, keepdims=True))
            return tuple(c + s for c, s in zip(carry, sums))

        zero = jnp.zeros((1, cb), F32)
        dw0, dw1, dw2, dbias = _unrolled(t // tm, FFN_UNROLL, tile_a, (zero, zero, zero, zero))
        row = lax.broadcasted_iota(jnp.int32, (8, cb), 0)
        dw_ref[...] = jnp.where(row == 0, dw0, jnp.where(row == 1, dw1, jnp.where(row == 2, dw2, 0.0)))
        db_ref[...] = dbias

        def tile_b(i, carry):
            r0 = pl.multiple_of(i * tm, tm)
            acc = [jnp.zeros((tm, cb), F32)]

            def tap(s, rows):
                acc[0] = acc[0] + wv[2 - s] * rows

            _for_shifted(dgc_ref, r0, tm, (0, 1, 2), tap)
            dhf_ref[0, pl.ds(r0, tm), :] = acc[0].astype(BF16)
            return carry

        lax.fori_loop(0, t // tm, tile_b, 0)

    return pl.pallas_call(
        body, name=name, grid=(nb,),
        in_specs=[_col_spec(t, cb), _col_spec(t, cb), _col_spec(t, cb, nb),
                  pl.BlockSpec((FFN_KERNEL, cb), lambda j: (0, j)), pl.BlockSpec((1, cb), lambda j: (0, j))],
        out_specs=[pl.BlockSpec((2, t, cb), lambda j: (0, 0, j)),
                   pl.BlockSpec((8, cb), lambda j: (0, j)), pl.BlockSpec((1, cb), lambda j: (0, j))],
        out_shape=[jax.ShapeDtypeStruct((2, t, f), BF16), jax.ShapeDtypeStruct((8, f), F32),
                   jax.ShapeDtypeStruct((1, f), F32)],
        scratch_shapes=[pltpu.VMEM((t + 8, cb), F32), pltpu.VMEM((t + 8, cb), F32)],
        compiler_params=_cparams(("parallel",)),
    )(dact, hf, hf, w, b)


def _silu_grad(z, sg):
    return sg * (1.0 + z * (1.0 - sg))


def _conv_fwd(name, hin, w, b, ng, nb_, cat):
    t = hin.shape[0]
    c = GROUP
    tm = CONV_ROWS
    pad = 32
    shifts = tuple(2 + k for k in range(CONV_KERNEL))

    def body(a_ref, gt_ref, w_ref, b_ref, ng_ref, nb_ref, cat_ref, u1_ref, u3_ref, pad_ref):
        pad_ref[pl.ds(0, pad), :] = jnp.zeros((pad, c), F32)
        pad_ref[pl.ds(pad, t), :] = a_ref[...] * _sigmoid(gt_ref[...])
        bias, gam, bet = b_ref[...], ng_ref[...], nb_ref[...]

        def tile(i, carry):
            r0 = pl.multiple_of(i * tm, tm)
            acc = [jnp.broadcast_to(bias, (tm, c))]

            def tap(s, rows):
                acc[0] = acc[0] + w_ref[pl.ds(s - 2, 1), :] * rows

            _for_shifted(pad_ref, r0, tm, shifts, tap)
            u1 = acc[0]
            u1_ref[pl.ds(r0, tm), :] = u1
            xhat, _ = _ln_stats(u1)
            u2 = xhat * gam + bet
            u3_ref[pl.ds(r0, tm), :] = (u2 * _sigmoid(u2)).astype(BF16)
            return carry

        _unrolled(t // tm, CONV_UNROLL, tile, 0)

    vec = pl.BlockSpec((1, c), lambda j: (0, j))
    return pl.pallas_call(
        body, name=name, grid=(N_GROUPS,),
        in_specs=[_col_spec(t, c), _col_spec(t, c, N_GROUPS),
                  pl.BlockSpec((CONV_KERNEL, c), lambda j: (0, j)), vec, vec, vec, ANY],
        out_specs=[_col_spec(t, c), _col_spec(t, c)],
        out_shape=[jax.ShapeDtypeStruct((t, CONV_WIDTH), F32), jax.ShapeDtypeStruct(cat.shape, BF16)],
        input_output_aliases={6: 1},
        scratch_shapes=[pltpu.VMEM((t + pad, c), F32)],
        compiler_params=_cparams(("parallel",)),
    )(hin, hin, w, b, ng, nb_, cat)


def _conv_bwd(name, dcat, u1, hin, w, ng, nb_):
    t = hin.shape[0]
    c = GROUP
    tm = CONV_ROWS
    pad = 32
    nk = CONV_KERNEL

    def body(du3_ref, u1_ref, a_ref, gt_ref, w_ref, ng_ref, nb_ref,
             da_ref, dgt_ref, dw_ref, db_ref, dng_ref, dnb_ref, u0_ref, du1_ref, dwp_ref):
        u0_ref[pl.ds(0, pad), :] = jnp.zeros((pad, c), F32)
        u0_ref[pl.ds(pad, t), :] = a_ref[...] * _sigmoid(gt_ref[...])
        du1_ref[pl.ds(t, pad), :] = jnp.zeros((pad, c), F32)
        dwp_ref[...] = jnp.zeros_like(dwp_ref)
        gam, bet = ng_ref[...], nb_ref[...]

        def tile_a(i, carry):
            r0 = pl.multiple_of(i * tm, tm)
            u1 = u1_ref[pl.ds(r0, tm), :]
            xhat, rstd = _ln_stats(u1)
            u2 = xhat * gam + bet
            sg = _sigmoid(u2)
            du2 = du3_ref[pl.ds(r0, tm), :] * _silu_grad(u2, sg)
            dxhat = du2 * gam
            m1 = jnp.mean(dxhat, axis=-1, keepdims=True)
            m2 = jnp.mean(dxhat * xhat, axis=-1, keepdims=True)
            du1 = rstd * (dxhat - m1 - xhat * m2)
            du1_ref[pl.ds(r0, tm), :] = du1
            sums = (jnp.sum(du1, axis=0, keepdims=True), jnp.sum(du2 * xhat, axis=0, keepdims=True),
                    jnp.sum(du2, axis=0, keepdims=True))
            return tuple(x + s for x, s in zip(carry, sums))

        zero = jnp.zeros((1, c), F32)
        dbias, dgam, dbet = _unrolled(t // tm, CONV_UNROLL, tile_a, (zero, zero, zero))
        db_ref[...] = dbias
        dng_ref[...] = dgam
        dnb_ref[...] = dbet

        def tile_b(i, carry):
            r0 = pl.multiple_of(i * tm, tm)
            du1 = du1_ref[pl.ds(r0, tm), :]
            acc = [jnp.zeros((tm, c), F32)]

            def tap_dx(s, rows):
                acc[0] = acc[0] + w_ref[pl.ds(nk - 1 - s, 1), :] * rows

            _for_shifted(du1_ref, r0, tm, tuple(range(nk)), tap_dx)

            def tap_dw(s, rows):
                part = (du1 * rows).reshape(tm // 8, 8, c).sum(axis=0)
                dwp_ref[s - 2] = dwp_ref[s - 2] + part

            _for_shifted(u0_ref, r0, tm, tuple(2 + k for k in range(nk)), tap_dw)
            du0 = acc[0]
            a = a_ref[pl.ds(r0, tm), :]
            sg = _sigmoid(gt_ref[pl.ds(r0, tm), :])
            da_ref[pl.ds(r0, tm), :] = (du0 * sg).astype(BF16)
            dgt_ref[pl.ds(r0, tm), :] = (du0 * a * sg * (1.0 - sg)).astype(BF16)
            return carry

        lax.fori_loop(0, t // tm, tile_b, 0)
        dw_ref[...] = jnp.sum(dwp_ref[...], axis=1)

    vec = pl.BlockSpec((1, c), lambda j: (0, j))
    vshape = jax.ShapeDtypeStruct((1, CONV_WIDTH), F32)
    return pl.pallas_call(
        body, name=name, grid=(N_GROUPS,),
        in_specs=[_col_spec(t, c), _col_spec(t, c), _col_spec(t, c), _col_spec(t, c, N_GROUPS),
                  pl.BlockSpec((nk, c), lambda j: (0, j)), vec, vec],
        out_specs=[_col_spec(t, c), _col_spec(t, c), pl.BlockSpec((32, c), lambda j: (0, j)), vec, vec, vec],
        out_shape=[jax.ShapeDtypeStruct((t, CONV_WIDTH), BF16), jax.ShapeDtypeStruct((t, CONV_WIDTH), BF16),
                   jax.ShapeDtypeStruct((32, CONV_WIDTH), F32), vshape, vshape, vshape],
        scratch_shapes=[pltpu.VMEM((t + pad, c), F32), pltpu.VMEM((t + pad, c), F32),
                        pltpu.VMEM((32, 8, c), F32)],
        compiler_params=_cparams(("parallel",)),
    )(dcat, u1, hin, hin, w, ng, nb_)


LEVELS = (64, 32, 16)
HGRN_UNROLL = 2
HGRN_UNROLL_FWD = 4
NT_DIMS = (((1,), (1,)), ((), ()))
NN_DIMS = (((1,), (0,)), ((), ()))
TN_DIMS = (((0,), (0,)), ((), ()))


def _bdot(a, b, dims):
    return lax.dot_general(a.astype(BF16), b.astype(BF16), dims, preferred_element_type=F32)


def _hdot(a, b):
    return jnp.dot(a, b, precision=lax.Precision.HIGHEST, preferred_element_type=F32)


def _chunk_consts():
    rid = lax.broadcasted_iota(jnp.int32, (CHUNK, GROUP), 0)
    ti = lax.broadcasted_iota(jnp.int32, (CHUNK, CHUNK), 0)
    si = lax.broadcasted_iota(jnp.int32, (CHUNK, CHUNK), 1)
    tri = (si <= ti).astype(F32)
    second = [(rid & (b // 2)) != 0 for b in LEVELS]
    same = [None] + [(ti // b) == (si // b) for b in LEVELS[1:]]
    sub = lax.broadcasted_iota(jnp.int32, (SUB, GROUP), 0)
    return rid, tri, second, same, sub


def _level_refs(cum_ref, rid, base):
    row = lambda i: cum_ref[pl.ds(base + i, 1), :]
    l1 = jnp.broadcast_to(row(31), (CHUNK, GROUP))
    l2 = jnp.where(rid < 32, row(15), row(47))
    l3 = jnp.where(rid < 16, row(7), jnp.where(rid < 32, row(23), jnp.where(rid < 48, row(39), row(55))))
    return l1, l2, l3


def _level_factors(cum, brefs, second):
    out = []
    for bref, sec in zip(brefs, second):
        eq = jnp.where(sec, jnp.exp(jnp.minimum(cum - bref, 0.0)), 0.0)
        ek = jnp.where(sec, 0.0, jnp.exp(jnp.minimum(bref - cum, 0.0)))
        out.append((eq, ek))
    return out


def _gates(q, f, lb):
    sq = _sigmoid(q)
    sf = _sigmoid(f)
    fg = lb + (1.0 - lb) * sf
    return q * sq, sq, sf, fg


def _hgrn_specs(t, nc):
    c = GROUP
    col = lambda off: pl.BlockSpec((t, c), lambda h: (0, h + off))
    hin_specs = [col(16), col(24), col(32), col(40)]
    vec = pl.BlockSpec((1, c), lambda h: (0, h))
    lbs = pl.BlockSpec((2, c), lambda h: (0, h))
    st = pl.BlockSpec((1, nc, c, c), lambda h: (h, 0, 0, 0))
    return col, hin_specs, vec, lbs, st


def _hgrn_fwd(name, hin, lb_logits, hg):
    t = hin.shape[0]
    nc = t // CHUNK
    c = GROUP
    col, hin_specs, vec, lbs, st = _hgrn_specs(t, nc)

    def body(q_ref, f_ref, v_ref, og_ref, lb_ref, hg_ref, o_ref, ob_ref, st_ref,
             s_ref, cum_ref, kk_ref, vc_ref):
        rid, tri, second, same, sub = _chunk_consts()
        lb = _sigmoid(lb_ref[pl.ds(0, 1), :] - lb_ref[pl.ds(1, 1), :])
        gain = hg_ref[...]
        s_ref[...] = jnp.zeros_like(s_ref)

        def chunk(ci, u):
            base = u * CHUNK
            r0 = pl.multiple_of(ci * CHUNK, CHUNK)
            rows = pl.ds(r0, CHUNK)
            qh, _, _, fg = _gates(q_ref[rows, :], f_ref[rows, :], lb)
            v = v_ref[rows, :]
            kk = 1.0 - fg
            cum = _hdot(tri, jnp.log(fg))
            cum_ref[pl.ds(base, CHUNK), :] = cum
            kk_ref[pl.ds(base, CHUNK), :] = kk
            vc_ref[pl.ds(base, CHUNK), :] = v
            sprev = s_ref[...]
            st_ref[0, ci] = sprev
            blast = cum_ref[pl.ds(base + CHUNK - 1, 1), :]
            o = _bdot(qh * jnp.exp(cum), sprev, NT_DIMS)
            s_ref[...] = sprev * jnp.exp(blast) + _bdot(v, kk * jnp.exp(blast - cum), TN_DIMS)
            a = None
            for (eq, ek), msk in zip(_level_factors(cum, _level_refs(cum_ref, rid, base), second), same):
                al = _bdot(qh * eq, kk * ek, NT_DIMS)
                al = al if msk is None else jnp.where(msk, al, 0.0)
                a = al if a is None else a + al
            o = o + _bdot(a, v, NN_DIMS)
            diag = []
            for sb in range(CHUNK // SUB):
                lo = sb * SUB
                qb = qh[lo:lo + SUB]
                cb = cum[lo:lo + SUB]
                od = jnp.zeros((SUB, c), F32)
                for s in range(SUB):
                    e = jnp.where(sub >= s, jnp.exp(jnp.minimum(cb - cum_ref[pl.ds(base + lo + s, 1), :], 0.0)), 0.0)
                    acol = jnp.sum(qb * e * kk_ref[pl.ds(base + lo + s, 1), :], axis=-1, keepdims=True)
                    od = od + acol * vc_ref[pl.ds(base + lo + s, 1), :]
                diag.append(od)
            o = o + jnp.concatenate(diag, axis=0)
            o_ref[rows, :] = o
            y = o * lax.rsqrt(jnp.mean(o * o, axis=-1, keepdims=True) + RMS_EPS) * gain
            og = og_ref[rows, :]
            ob_ref[rows, :] = (y * og * _sigmoid(og)).astype(BF16)

        def chunks(i, carry):
            for u in range(HGRN_UNROLL_FWD):
                chunk(i * HGRN_UNROLL_FWD + u, u)
            return carry

        lax.fori_loop(0, nc // HGRN_UNROLL_FWD, chunks, 0)

    return pl.pallas_call(
        body, name=name, grid=(N_GROUPS,),
        in_specs=hin_specs + [lbs, vec],
        out_specs=[col(0), col(N_GROUPS), st],
        out_shape=[jax.ShapeDtypeStruct((t, HGRN_WIDTH), F32), jax.ShapeDtypeStruct((t, CONV_WIDTH + HGRN_WIDTH), BF16),
                   jax.ShapeDtypeStruct((N_GROUPS, nc, c, c), F32)],
        scratch_shapes=[pltpu.VMEM((c, c), F32)] + [pltpu.VMEM((HGRN_UNROLL_FWD * CHUNK, c), F32)] * 3,
        compiler_params=_cparams(("parallel",)),
    )(hin, hin, hin, hin, lb_logits, hg)


def _hgrn_bwd(name, dcat, hin, o_raw, states, lb_logits, hg):
    t = hin.shape[0]
    nc = t // CHUNK
    c = GROUP
    col, hin_specs, vec, lbs, st = _hgrn_specs(t, nc)

    def body(do_ref, q_ref, f_ref, v_ref, og_ref, o_ref, st_ref, lb_ref, hg_ref,
             dq_ref, df_ref, dv_ref, dog_ref, dhg_ref, dlb_ref,
             ds_ref, cum_ref, kk_ref, vc_ref):
        rid, tri, second, same, sub = _chunk_consts()
        trit = tri.T
        lb = _sigmoid(lb_ref[pl.ds(0, 1), :] - lb_ref[pl.ds(1, 1), :])
        gain = hg_ref[...]
        ds_ref[...] = jnp.zeros_like(ds_ref)

        def chunk(i, carry, u):
            base = u * CHUNK
            dhg, dlb = carry
            ci = nc - 1 - i
            r0 = pl.multiple_of(ci * CHUNK, CHUNK)
            rows = pl.ds(r0, CHUNK)
            q = q_ref[rows, :]
            qh, sq, sf, fg = _gates(q, f_ref[rows, :], lb)
            v = v_ref[rows, :]
            kk = 1.0 - fg
            cum = _hdot(tri, jnp.log(fg))
            cum_ref[pl.ds(base, CHUNK), :] = cum
            kk_ref[pl.ds(base, CHUNK), :] = kk
            vc_ref[pl.ds(base, CHUNK), :] = v
            o = o_ref[rows, :]
            og = og_ref[rows, :]
            sg = _sigmoid(og)
            rinv = lax.rsqrt(jnp.mean(o * o, axis=-1, keepdims=True) + RMS_EPS)
            yn = o * rinv
            dof = do_ref[rows, :]
            dog_ref[rows, :] = (dof * yn * gain * _silu_grad(og, sg)).astype(BF16)
            dz = dof * og * sg
            dhg = dhg + jnp.sum(dz * yn, axis=0, keepdims=True)
            dy = dz * gain
            do = rinv * (dy - yn * jnp.mean(dy * yn, axis=-1, keepdims=True))
            sprev = st_ref[0, ci]
            dsn = ds_ref[...]
            blast = cum_ref[pl.ds(base + CHUNK - 1, 1), :]
            eq0 = jnp.exp(cum)
            ek0 = jnp.exp(blast - cum)
            dqh = _bdot(do, sprev, NN_DIMS) * eq0
            dkk = _bdot(v, dsn, NN_DIMS) * ek0
            dlast = (jnp.sum(kk * dkk, axis=0, keepdims=True)
                     + jnp.exp(blast) * jnp.sum(dsn * sprev, axis=0, keepdims=True))
            dv = _bdot(kk * ek0, dsn, NT_DIMS)
            ds_ref[...] = dsn * jnp.exp(blast) + _bdot(do, qh * eq0, TN_DIMS)
            dg = qh * dqh - kk * dkk
            da = _bdot(do, v, NT_DIMS)
            a = None
            for (eq, ek), msk in zip(_level_factors(cum, _level_refs(cum_ref, rid, base), second), same):
                ql, kl = (qh * eq).astype(BF16), (kk * ek).astype(BF16)
                al = _bdot(ql, kl, NT_DIMS)
                dal = da
                if msk is not None:
                    al = jnp.where(msk, al, 0.0)
                    dal = jnp.where(msk, da, 0.0)
                a = al if a is None else a + al
                dql = _bdot(dal, kl, NN_DIMS)
                dkl = _bdot(dal, ql, TN_DIMS)
                dqh = dqh + dql * eq
                dkk = dkk + dkl * ek
                dg = dg + (ql.astype(F32) * dql - kl.astype(F32) * dkl)
            dv = dv + _bdot(a, do, TN_DIMS)
            dq_d, dk_d, dv_d = [], [], []
            for sb in range(CHUNK // SUB):
                lo = sb * SUB
                qb = qh[lo:lo + SUB]
                cb = cum[lo:lo + SUB]
                dob = do[lo:lo + SUB]
                dqb = jnp.zeros((SUB, c), F32)
                dkb = jnp.zeros((SUB, c), F32)
                dvb = jnp.zeros((SUB, c), F32)
                for s in range(SUB):
                    e = jnp.where(sub >= s, jnp.exp(jnp.minimum(cb - cum_ref[pl.ds(base + lo + s, 1), :], 0.0)), 0.0)
                    ks = kk_ref[pl.ds(base + lo + s, 1), :]
                    qe = qb * e
                    dacol = jnp.sum(dob * vc_ref[pl.ds(base + lo + s, 1), :], axis=-1, keepdims=True)
                    acol = jnp.sum(qe * ks, axis=-1, keepdims=True)
                    dqb = dqb + dacol * (ks * e)
                    dkb = jnp.where(sub == s, jnp.sum(dacol * qe, axis=0, keepdims=True), dkb)
                    dvb = jnp.where(sub == s, jnp.sum(acol * dob, axis=0, keepdims=True), dvb)
                dq_d.append(dqb)
                dk_d.append(dkb)
                dv_d.append(dvb)
            dq_d = jnp.concatenate(dq_d, axis=0)
            dk_d = jnp.concatenate(dk_d, axis=0)
            dqh = dqh + dq_d
            dkk = dkk + dk_d
            dg = dg + (qh * dq_d - kk * dk_d)
            dv = dv + jnp.concatenate(dv_d, axis=0)
            dlf = _hdot(trit, dg) + dlast
            dfg = dlf / fg - dkk
            df_ref[rows, :] = (dfg * (1.0 - lb) * sf * (1.0 - sf)).astype(BF16)
            dlb = dlb + jnp.sum(dfg * (1.0 - sf), axis=0, keepdims=True)
            dq_ref[rows, :] = (dqh * _silu_grad(q, sq)).astype(BF16)
            dv_ref[rows, :] = dv.astype(BF16)
            return dhg, dlb

        def chunks(i, carry):
            for u in range(HGRN_UNROLL):
                carry = chunk(i * HGRN_UNROLL + u, carry, u)
            return carry

        zero = jnp.zeros((1, c), F32)
        dhg, dlb = lax.fori_loop(0, nc // HGRN_UNROLL, chunks, (zero, zero))
        dhg_ref[...] = dhg
        dl0 = dlb * lb * (1.0 - lb)
        dlb_ref[...] = jnp.where(lax.broadcasted_iota(jnp.int32, (2, c), 0) == 0, dl0, -dl0)

    big = jax.ShapeDtypeStruct((t, HGRN_WIDTH), BF16)
    return pl.pallas_call(
        body, name=name, grid=(N_GROUPS,),
        in_specs=[col(8)] + hin_specs + [col(0), st, lbs, vec],
        out_specs=[col(0)] * 4 + [vec, lbs],
        out_shape=[big] * 4 + [jax.ShapeDtypeStruct((1, HGRN_WIDTH), F32), jax.ShapeDtypeStruct((2, HGRN_WIDTH), F32)],
        scratch_shapes=[pltpu.VMEM((c, c), F32)] + [pltpu.VMEM((HGRN_UNROLL * CHUNK, c), F32)] * 3,
        compiler_params=_cparams(("parallel",)),
    )(dcat, hin, hin, hin, hin, o_raw, states, lb_logits, hg)


ANY = pl.BlockSpec(memory_space=pl.ANY)


def _my_place():
    return lax.axis_index("x"), lax.axis_index("y"), lax.axis_index("c")


HBM = pl.BlockSpec(memory_space=pltpu.HBM)
SEM = pl.BlockSpec(memory_space=pltpu.SEMAPHORE)
EFFECT = pltpu.SideEffectType.DATAFLOW_SIDE_EFFECTING


def _peer(k):
    x, y, c = _my_place()
    px = 1 - x if k & 4 else x
    py = 1 - y if k & 2 else y
    pc = 1 - c if k & 1 else c
    return (px, py, pc), 4 * px + 2 * py + pc


def _slot(land_ref, idx):
    if len(land_ref.shape) == 2:
        ns = land_ref.shape[1] // N_DEV
        return land_ref.at[:, pl.ds(pl.multiple_of(idx * ns, 128), ns)]
    return land_ref.at[idx]


def _exchange_copy(k, src_ref, land_ref, send_sems, recv_sems, scatter, landing):
    x, y, c = _my_place()
    me = 4 * x + 2 * y + c
    to, idx = _peer(k)
    return pltpu.make_async_remote_copy(
        src_ref=_slot(src_ref, idx) if scatter else src_ref,
        dst_ref=_slot(land_ref, idx) if landing else _slot(land_ref, me),
        send_sem=send_sems.at[k - 1], recv_sem=recv_sems.at[k - 1], device_id=to, device_id_type=MESH)


ALL_PEERS = tuple(range(1, N_DEV))
NEAR_PEERS = (1, 2, 4, 6)
SAME_CORE_PEERS = (2, 4, 6)


def _exchange_start(name, src, land, scatter, ks=ALL_PEERS):
    def body(src_ref, land_ref, send_sems, recv_sems, src_thru, land_thru, token):
        for k in ks:
            _exchange_copy(k, src_ref, land_ref, send_sems, recv_sems, scatter, landing=False).start()
        token[...] = jnp.zeros_like(token)

    send_sems, recv_sems, src_thru, land_thru, token = pl.pallas_call(
        body, name=name,
        out_shape=(pltpu.SemaphoreType.DMA((N_DEV - 1,)), pltpu.SemaphoreType.DMA((N_DEV - 1,)),
                   pltpu.HBM(src.shape, src.dtype), pltpu.HBM(land.shape, land.dtype),
                   jax.ShapeDtypeStruct((8, 128), F32)),
        in_specs=(HBM, HBM), out_specs=(SEM, SEM, HBM, HBM, pl.BlockSpec(memory_space=pltpu.VMEM)),
        input_output_aliases={0: 2, 1: 3},
        compiler_params=pltpu.CompilerParams(has_side_effects=EFFECT),
    )(pltpu.with_memory_space_constraint(src, pltpu.HBM), pltpu.with_memory_space_constraint(land, pltpu.HBM))
    return (send_sems, recv_sems, src_thru, land_thru, scatter, ks), token


def _exchange_wait(name, handle, after):
    send_sems, recv_sems, src_thru, land_thru, scatter, ks = handle

    def body(src_ref, land_ref, send_sems, recv_sems, after_ref, src_dead, got_ref):
        for k in ks:
            cp = _exchange_copy(k, src_ref, land_ref, send_sems, recv_sems, scatter, landing=True)
            cp.wait_send()
            cp.wait_recv()

    return pl.pallas_call(
        body, name=name,
        out_shape=(pltpu.HBM(src_thru.shape, src_thru.dtype), pltpu.HBM(land_thru.shape, land_thru.dtype)),
        in_specs=(HBM, HBM, SEM, SEM, ANY), out_specs=(HBM, HBM), input_output_aliases={0: 0, 1: 1},
        compiler_params=pltpu.CompilerParams(has_side_effects=EFFECT),
    )(src_thru, land_thru, send_sems, recv_sems, after)[1]


def _relay_copy(j, land_ref, send_sems, recv_sems, landing):
    x, y, c = _my_place()
    k = SAME_CORE_PEERS[j]
    _, sent = _peer(k)
    _, got = _peer(k + 1)
    return pltpu.make_async_remote_copy(
        src_ref=_slot(land_ref, sent), dst_ref=_slot(land_ref, got) if landing else _slot(land_ref, sent),
        send_sem=send_sems.at[j], recv_sem=recv_sems.at[j], device_id=(x, y, 1 - c), device_id_type=MESH)


def _relay_start(name, land):
    n = len(SAME_CORE_PEERS)

    def body(land_ref, send_sems, recv_sems, land_thru, token):
        for j in range(n):
            _relay_copy(j, land_ref, send_sems, recv_sems, landing=False).start()
        token[...] = jnp.zeros_like(token)

    send_sems, recv_sems, land_thru, token = pl.pallas_call(
        body, name=name,
        out_shape=(pltpu.SemaphoreType.DMA((n,)), pltpu.SemaphoreType.DMA((n,)),
                   pltpu.HBM(land.shape, land.dtype), jax.ShapeDtypeStruct((8, 128), F32)),
        in_specs=(HBM,), out_specs=(SEM, SEM, HBM, pl.BlockSpec(memory_space=pltpu.VMEM)),
        input_output_aliases={0: 2},
        compiler_params=pltpu.CompilerParams(has_side_effects=EFFECT),
    )(pltpu.with_memory_space_constraint(land, pltpu.HBM))
    return (send_sems, recv_sems, land_thru), token


def _relay_wait(name, handle, after):
    send_sems, recv_sems, land_thru = handle

    def body(land_ref, send_sems, recv_sems, after_ref, got_ref):
        for j in range(len(SAME_CORE_PEERS)):
            cp = _relay_copy(j, land_ref, send_sems, recv_sems, landing=True)
            cp.wait_send()
            cp.wait_recv()

    return pl.pallas_call(
        body, name=name, out_shape=pltpu.HBM(land_thru.shape, land_thru.dtype),
        in_specs=(HBM, SEM, SEM, ANY), out_specs=HBM, input_output_aliases={0: 0},
        compiler_params=pltpu.CompilerParams(has_side_effects=EFFECT),
    )(land_thru, send_sems, recv_sems, after)


def _own_cols(name, own):
    r, ns = own.shape

    def body(own_ref, land_ref, sem):
        x, y, c = _my_place()
        cp = pltpu.make_async_copy(own_ref, _slot(land_ref, 4 * x + 2 * y + c), sem)
        cp.start()
        cp.wait()

    return pl.pallas_call(
        body, name=name, out_shape=jax.ShapeDtypeStruct((r, N_DEV * ns), own.dtype),
        in_specs=[ANY], out_specs=ANY, scratch_shapes=[pltpu.SemaphoreType.DMA],
    )(own)


def _own_slot(own, me):
    land = lax.empty((N_DEV,) + own.shape, own.dtype)
    return lax.dynamic_update_slice_in_dim(land, own[None], me, axis=0)


def _adamw_math(w, g, m, v):
    m = ADAM_B1 * m + (1.0 - ADAM_B1) * g
    v = ADAM_B2 * v + (1.0 - ADAM_B2) * (g * g)
    m_hat = m / (1.0 - ADAM_B1 ** ADAM_STEP)
    v_hat = v / (1.0 - ADAM_B2 ** ADAM_STEP)
    delta = -ADAM_LR * (m_hat / (jnp.sqrt(v_hat) + ADAM_EPS) + ADAM_WD * w)
    return delta, m, v


def _adamw_sum(name, recv, w, m, v, tr, row0=0, partial=None):
    r, c = w.shape
    rr = recv.shape[1]
    off = row0 // tr

    def body(recv_ref, w_ref, m_ref, v_ref, *refs):
        g_ref, d_ref, mo_ref, vo_ref = refs[-4:]
        g = recv_ref[0].astype(F32)
        for j in range(1, N_DEV):
            g = g + recv_ref[j].astype(F32)
        g_ref[...] = g
        d_ref[...], mo_ref[...], vo_ref[...] = _adamw_math(w_ref[...], g, m_ref[...], v_ref[...])

    tile = pl.BlockSpec((tr, c), lambda i: (i + off, 0))
    out = jax.ShapeDtypeStruct((r, c), F32)
    prev = list(partial) if partial is not None else []
    return pl.pallas_call(
        body, name=name, grid=(rr // tr,),
        in_specs=[pl.BlockSpec((N_DEV, tr, c), lambda i: (0, i, 0)), tile, tile, tile] + [ANY] * len(prev),
        out_specs=[tile] * 4, out_shape=[out] * 4,
        input_output_aliases={4 + i: i for i in range(len(prev))},
        compiler_params=_cparams(("parallel",)),
    )(recv, w, m, v, *prev)


def _sum_parts(name, parts):
    _, r, c = parts.shape

    def body(p_ref, o_ref):
        acc = p_ref[0]
        for j in range(1, N_DEV):
            acc = acc + p_ref[j]
        o_ref[...] = acc

    return pl.pallas_call(body, name=name, out_shape=jax.ShapeDtypeStruct((r, c), F32),
                          compiler_params=_cparams())(parts)


def _adamw_small(name, w, g, m, v):
    def body(w_ref, g_ref, m_ref, v_ref, d_ref, mo_ref, vo_ref):
        d_ref[...], mo_ref[...], vo_ref[...] = _adamw_math(w_ref[...], g_ref[...], m_ref[...], v_ref[...])

    out = jax.ShapeDtypeStruct(w.shape, F32)
    return pl.pallas_call(body, name=name, out_shape=[out] * 3, compiler_params=_cparams())(w, g, m, v)


def _pack(pieces, rows):
    flat = jnp.concatenate([p.reshape(-1).astype(F32) for p in pieces])
    return jnp.pad(flat, (0, rows * 128 - flat.shape[0])).reshape(rows, 128)


def _unpack(packed, shapes):
    flat = packed.reshape(-1)
    out, off = [], 0
    for s in shapes:
        n = 1
        for d in s:
            n *= d
        out.append(flat[off:off + n].reshape(s))
        off += n
    return out


def kernel(x, emb_ln_g, emb_ln_b, w_in, conv_w, conv_b, conv_norm_g, conv_norm_b, lb_logits, hgrn_norm_g, w_out, ln1_g, ln1_b, w_ffn_up, ffn_conv_w, ffn_conv_b, w_ffn_down, ln2_g, ln2_b, loss_target, m_emb_ln_g, m_emb_ln_b, m_w_in, m_conv_w, m_conv_b, m_conv_norm_g, m_conv_norm_b, m_lb_logits, m_hgrn_norm_g, m_w_out, m_ln1_g, m_ln1_b, m_w_ffn_up, m_ffn_conv_w, m_ffn_conv_b, m_w_ffn_down, m_ln2_g, m_ln2_b, v_emb_ln_g, v_emb_ln_b, v_w_in, v_conv_w, v_conv_b, v_conv_norm_g, v_conv_norm_b, v_lb_logits, v_hgrn_norm_g, v_w_out, v_ln1_g, v_ln1_b, v_w_ffn_up, v_ffn_conv_w, v_ffn_conv_b, v_w_ffn_down, v_ln2_g, v_ln2_b):
    t = x.shape[1]
    me = 4 * lax.axis_index("x") + 2 * lax.axis_index("y") + lax.axis_index("c")
    x2, tgt = x[0], loss_target[0]
    ns_in, ns_up = w_in.shape[2], w_ffn_up.shape[2]
    rs_out, rs_down = w_out.shape[1], w_ffn_down.shape[1]
    cs, fs = conv_w.shape[2], ffn_conv_w.shape[2]

    def gather_start(name, w, prev, ks=ALL_PEERS, cols=False):
        shard = (w[0] + prev).astype(BF16)
        land = _own_cols(name + "_own", shard) if cols else _own_slot(shard, me)
        return _exchange_start(name, shard, land, scatter=False, ks=ks)

    h_in, tok = gather_start("ag_w_in_start", w_in, 0.0, NEAR_PEERS, cols=True)
    taps = _pack([conv_w[0], ffn_conv_w[0]], 48) + tok[0, 0]
    h_taps, tok = _exchange_start("ag_taps_start", taps, _own_slot(taps, me), scatter=False)
    h_out, tok = gather_start("ag_w_out_start", w_out, tok[0, 0])
    h_up, tok = gather_start("ag_w_up_start", w_ffn_up, tok[0, 0], NEAR_PEERS, cols=True)
    h_down, tok = gather_start("ag_w_down_start", w_ffn_down, tok[0, 0])

    row = lambda a: a.reshape(1, -1)

    _, h0, h0b, h0bt = _ln_fwd("ln_in", x2, None, row(emb_ln_g) + tok[0, 0], row(emb_ln_b), 1.0)
    h_relay, tok_relay = _relay_start("ag_w_in_relay_start", _exchange_wait("ag_w_in_wait", h_in, h0b))
    win_n = _relay_wait("ag_w_in_relay_wait", h_relay, tok_relay)
    hin = _mm_nn("mm_in", h0b, win_n, F32, tm=1024, tn=ns_in, tk=D_MODEL)
    n_cw, n_fw = CONV_KERNEL * cs, FFN_KERNEL * fs
    taps_g = _exchange_wait("ag_taps_wait", h_taps, hin).reshape(N_DEV, -1)
    cw_full = taps_g[:, :n_cw].reshape(N_DEV, CONV_KERNEL, cs).transpose(1, 0, 2).reshape(CONV_KERNEL, CONV_WIDTH)
    fw_full = taps_g[:, n_cw:n_cw + n_fw].reshape(N_DEV, FFN_KERNEL, fs).transpose(1, 0, 2).reshape(FFN_KERNEL, D_FF)

    o_raw, cat_right, states = _hgrn_fwd("hgrn_fwd", hin, lb_logits, hgrn_norm_g)
    u1, catb = _conv_fwd("conv_fwd", hin, cw_full, conv_b, conv_norm_g, conv_norm_b, cat_right)
    wout_g = _exchange_wait("ag_w_out_wait", h_out, catb).reshape(D_MODEL, D_MODEL)
    h_up_relay, tok = _relay_start("ag_w_up_relay_start", _exchange_wait("ag_w_up_wait", h_up, wout_g))
    mix = _mm_nn("mm_out", catb, wout_g, F32, tm=1024, tn=1024, tk=D_MODEL, after=tok)
    r1, h1, h1b, h1bt = _ln_fwd("ln1", h0, mix, ln1_g, ln1_b, ALPHA)
    wup_n = _relay_wait("ag_w_up_relay_wait", h_up_relay, h1b)
    hf = _mm_nn("mm_up", h1b, wup_n, BF16, tm=1024, tn=1024, tk=D_MODEL)
    actb = _ffn_act_fwd("ffn_act", hf, fw_full, ffn_conv_b)
    wdown_g = _exchange_wait("ag_w_down_wait", h_down, actb).reshape(D_FF, D_MODEL)
    ffn = _mm_nn("mm_down", actb, wdown_g, F32, tm=512, tn=1024, tk=D_FF)
    dr2, dr2b, g_ln2g, g_ln2b, loss = _ln2_loss_bwd("ln2_loss", h1, ffn, ln2_g, ln2_b, tgt)

    def scatter_start(name, parts):
        if parts.ndim == 2:
            ns = parts.shape[1] // N_DEV
            own = lax.dynamic_slice_in_dim(parts, me * ns, ns, axis=1)
        else:
            own = lax.dynamic_index_in_dim(parts, me, axis=0, keepdims=False)
        return _exchange_start(name, parts, _own_slot(own, me), scatter=True)

    dact = _mm_nt("mm_dact", dr2b, wdown_g, BF16, tm=1024, tn=D_FF // 2, tk=D_MODEL)
    gw_down = _matmul(
        "mm_dw_down", actb, dr2b, (D_FF, D_MODEL), BF16, (N_DEV // 2, D_MODEL // 1024, 2),
        pl.BlockSpec((t // 2, 2 * rs_down), lambda i, j, kk: (kk, i)),
        pl.BlockSpec((t // 2, 1024), lambda i, j, kk: (kk, j)),
        pl.BlockSpec((2 * rs_down, 1024), lambda i, j, kk: (i, j)), nt="tn")
    s_down, tok = scatter_start("a2a_w_down_start", gw_down.reshape(N_DEV, rs_down, D_MODEL))
    dhf, g_fw, g_fb = _ffn_act_bwd("ffn_act_bwd", dact, hf, fw_full, ffn_conv_b + tok[0, 0])
    tm = min(1024, t)
    gw_up = _matmul(
        "mm_dw_up", h1bt, dhf, (D_MODEL, 2 * D_FF), BF16, (D_MODEL // 1024, 2 * D_FF // 512, 1),
        pl.BlockSpec((1024, t), lambda i, j, kk: (i, 0)),
        pl.BlockSpec((1, t, 512), lambda i, j, kk: (j // 11, 0, j % 11)),
        pl.BlockSpec((1024, 512), lambda i, j, kk: (i, j)), nt=False)
    s_up, tok = scatter_start("a2a_w_up_start", gw_up)
    tkf = D_FF // 2
    dh1 = _matmul(
        "mm_dh1", dhf, wup_n, (t, D_MODEL), F32, (t // tm, D_MODEL // 1024, 4),
        pl.BlockSpec((1, tm, tkf), lambda i, j, kk: (kk // 2, i, kk % 2)),
        pl.BlockSpec((1024, tkf), lambda i, j, kk: (j, kk)),
        pl.BlockSpec((tm, 1024), lambda i, j, kk: (i, j)), nt=True, after=tok)
    dr1, dr1b, g_ln1g, g_ln1b = _ln_bwd("ln1_bwd", r1, dr2, dh1, ln1_g + tok[0, 0], ALPHA, True)
    gw_out = _matmul(
        "mm_dw_out", catb, dr1b, (D_MODEL, D_MODEL), BF16, (2, 2, 2),
        pl.BlockSpec((t // 2, 1024), lambda i, j, kk: (kk, i)),
        pl.BlockSpec((t // 2, 1024), lambda i, j, kk: (kk, j)),
        pl.BlockSpec((1024, 1024), lambda i, j, kk: (i, j)), nt="tn")
    s_out, tok = scatter_start("a2a_w_out_start", gw_out.reshape(N_DEV, rs_out, D_MODEL))
    dcat = _mm_nt("mm_dcat", dr1b, wout_g, F32, tm=1024, tn=1024, tk=D_MODEL, after=tok)
    da, dgate, g_cw, g_cb, g_cng, g_cnb = _conv_bwd("conv_bwd", dcat, u1, hin, cw_full, conv_norm_g + tok[0, 0],
                                                    conv_norm_b)
    dq, df, di, dog, g_hg, g_lb = _hgrn_bwd("hgrn_bwd", dcat, hin, o_raw, states, lb_logits, hgrn_norm_g)
    dhin = jnp.concatenate([da, dgate, dq, df, di, dog], axis=1)
    half = D_MODEL // 2
    gw_in_a = _mm_grad_cols("mm_dw_in_a", h0bt, dhin, ns_in, 0, half, after=tok)
    s_in_a, tok = scatter_start("a2a_w_in_a_start", gw_in_a)
    gw_in_b = _mm_grad_cols("mm_dw_in_b", h0bt, dhin, ns_in, half, half, after=tok)
    s_in_b, tok = scatter_start("a2a_w_in_b_start", gw_in_b)
    dh0 = _mm_nt("mm_dh0", dhin, win_n, F32, tm=1024, tn=1024, tk=IN_PROJ // 2, after=tok)
    grad_x, g_eg, g_eb = _ln_bwd("ln_in_bwd", x2, dr1, dh0, row(emb_ln_g), ALPHA, False)

    small_shapes = [(D_MODEL,), (D_MODEL,), (CONV_KERNEL, CONV_WIDTH), (1, CONV_WIDTH), (1, CONV_WIDTH),
                    (1, CONV_WIDTH), (2, HGRN_WIDTH), (1, HGRN_WIDTH), (1, D_MODEL), (1, D_MODEL),
                    (FFN_KERNEL, D_FF), (1, D_FF), (1, D_MODEL), (1, D_MODEL), (128,)]
    rows_small = 569
    packed = _pack([g_eg, g_eb, g_cw[:CONV_KERNEL], g_cb, g_cng, g_cnb, g_lb, g_hg, g_ln1g, g_ln1b,
                    g_fw[:FFN_KERNEL], g_fb, g_ln2g, g_ln2b, loss], rows_small)
    h_small, tok = _exchange_start("ag_small_start", packed, _own_slot(packed, me), scatter=False)

    def big(name, handle, after, w, m, v, tr):
        recv = _exchange_wait("a2a_" + name + "_wait", handle, after)
        return [o[None] for o in _adamw_sum("adamw_" + name, recv, w[0], m[0], v[0], tr)]

    u_down = big("w_down", s_down, tok, w_ffn_down, m_w_ffn_down, v_w_ffn_down, 64)
    u_up = big("w_up", s_up, u_down[1], w_ffn_up, m_w_ffn_up, v_w_ffn_up, 64)
    u_out = big("w_out", s_out, u_up[1], w_out, m_w_out, v_w_out, 64)
    summed = _sum_parts("sum_small", _exchange_wait("ag_small_wait", h_small, u_out[1]))
    (s_eg, s_eb, s_cw, s_cb, s_cng, s_cnb, s_lb, s_hg, s_l1g, s_l1b, s_fw, s_fb, s_l2g, s_l2b,
     s_loss) = _unpack(summed, small_shapes)
    s_cw = lax.dynamic_slice_in_dim(s_cw, me * cs, cs, axis=1)[None]
    s_fw = lax.dynamic_slice_in_dim(s_fw, me * fs, fs, axis=1)[None]
    g_small = [s_eg, s_eb, s_cw, s_cb, s_cng, s_cnb, s_lb, s_hg, s_l1g, s_l1b, s_fw, s_fb, s_l2g, s_l2b]
    w_small = [emb_ln_g, emb_ln_b, conv_w, conv_b, conv_norm_g, conv_norm_b, lb_logits, hgrn_norm_g,
               ln1_g, ln1_b, ffn_conv_w, ffn_conv_b, ln2_g, ln2_b]
    m_small = [m_emb_ln_g, m_emb_ln_b, m_conv_w, m_conv_b, m_conv_norm_g, m_conv_norm_b, m_lb_logits,
               m_hgrn_norm_g, m_ln1_g, m_ln1_b, m_ffn_conv_w, m_ffn_conv_b, m_ln2_g, m_ln2_b]
    v_small = [v_emb_ln_g, v_emb_ln_b, v_conv_w, v_conv_b, v_conv_norm_g, v_conv_norm_b, v_lb_logits,
               v_hgrn_norm_g, v_ln1_g, v_ln1_b, v_ffn_conv_w, v_ffn_conv_b, v_ln2_g, v_ln2_b]
    rows_own = 236
    shapes_own = [w.shape for w in w_small]
    upd = _adamw_small("adamw_small", _pack(w_small, rows_own), _pack(g_small, rows_own),
                       _pack(m_small, rows_own), _pack(v_small, rows_own))
    d_small, nm_small, nv_small = (_unpack(u, shapes_own) for u in upd)
    g_small = [g.reshape(s) for g, s in zip(g_small, shapes_own)]

    recv_a = _exchange_wait("a2a_w_in_a_wait", s_in_a, upd[0])
    part = _adamw_sum("adamw_w_in_a", recv_a, w_in[0], m_w_in[0], v_w_in[0], 128)
    recv_b = _exchange_wait("a2a_w_in_b_wait", s_in_b, part[1])
    u_in = [o[None] for o in _adamw_sum("adamw_w_in_b", recv_b, w_in[0], m_w_in[0], v_w_in[0], 128,
                                        row0=half, partial=part)]

    def ordered(small, i_in, i_out, i_up, i_down):
        (eg, eb, cw, cb, cng, cnb, lb, hg, l1g, l1b, fw, fb, l2g, l2b) = small
        return [eg, eb, i_in, cw, cb, cng, cnb, lb, hg, i_out, l1g, l1b, i_up, fw, fb, i_down, l2g, l2b]

    outs = [s_loss[0], grad_x[None]]
    for k, small in enumerate([g_small, d_small, nm_small, nv_small]):
        outs += ordered(small, u_in[k], u_out[k], u_up[k], u_down[k])
    return tuple(outs)
```

```python
import functools

import jax
import jax.numpy as jnp
from jax import lax
from jax.experimental import pallas as pl
from jax.experimental.pallas import tpu as pltpu

F32 = jnp.float32
BF16 = jnp.bfloat16

N_DEV = 8
D_MODEL = 2048
CONV_WIDTH = 1024
CONV_KERNEL = 31
HGRN_WIDTH = 1024
GROUP = 128
N_GROUPS = 8
IN_PROJ = 2 * CONV_WIDTH + 4 * HGRN_WIDTH
D_FF = 5632
FFN_KERNEL = 3
CHUNK = 64
SUB = 8
LN_EPS = 1e-5
RMS_EPS = 1e-6
ALPHA = 2.0 ** 0.25
ADAM_LR, ADAM_B1, ADAM_B2, ADAM_EPS, ADAM_WD, ADAM_STEP = 0.001, 0.9, 0.999, 1e-08, 0.01, 10

VMEM_LIMIT = 56 * 1024 * 1024
MESH = pl.DeviceIdType.MESH


def _cparams(sem=None):
    return pltpu.CompilerParams(dimension_semantics=sem, vmem_limit_bytes=VMEM_LIMIT)


def _sigmoid(x):
    return 1.0 / (1.0 + jnp.exp(-x))


def _matmul(name, a, b, out_shape, out_dtype, grid, a_spec, b_spec, o_spec, nt, after=None):
    nk = grid[2]
    dims = {True: (((1,), (1,)), ((), ())), False: (((1,), (0,)), ((), ())), "tn": (((0,), (0,)), ((), ()))}[nt]
    extra = [] if after is None else [after]

    def body(a_ref, b_ref, *rest):
        o_ref, *scratch = rest[len(extra):]
        av = a_ref[0] if len(a_ref.shape) == 3 else a_ref[...]
        bv = b_ref[0] if len(b_ref.shape) == 3 else b_ref[...]
        part = lax.dot_general(av, bv, dims, preferred_element_type=F32)

        def write(res):
            if len(o_ref.shape) == 3:
                o_ref[0] = res.astype(out_dtype)
            else:
                o_ref[...] = res.astype(out_dtype)

        if nk == 1:
            write(part)
            return
        acc_ref, = scratch
        k = pl.program_id(2)

        @pl.when(k == 0)
        def _():
            acc_ref[...] = part

        @pl.when(jnp.logical_and(k > 0, k < nk - 1))
        def _():
            acc_ref[...] += part

        @pl.when(k == nk - 1)
        def _():
            write(acc_ref[...] + part)

    acc_shape = o_spec.block_shape[-2:]
    assert all(g >= 1 for g in grid), (name, grid)
    return pl.pallas_call(
        body, name=name, grid=grid, in_specs=[a_spec, b_spec] + [pl.BlockSpec(memory_space=pl.ANY)] * len(extra),
        out_specs=o_spec, out_shape=jax.ShapeDtypeStruct(out_shape, out_dtype),
        scratch_shapes=[pltpu.VMEM(acc_shape, F32)] if nk > 1 else [],
        compiler_params=_cparams(("parallel", "parallel", "arbitrary")),
    )(a, b, *extra)


def _mm_nn(name, a, w, out_dtype, tm, tn, tk, after=None):
    m, k = a.shape
    tm, tk = min(tm, m), min(tk, k)
    n = w.shape[1]
    return _matmul(
        name, a, w, (m, n), out_dtype, (m // tm, n // tn, k // tk),
        pl.BlockSpec((tm, tk), lambda i, j, kk: (i, kk)),
        pl.BlockSpec((tk, tn), lambda i, j, kk: (kk, j)),
        pl.BlockSpec((tm, tn), lambda i, j, kk: (i, j)), nt=False, after=after)


def _mm_nt(name, a, w, out_dtype, tm, tn, tk, after=None):
    m, k = a.shape
    tm = min(tm, m)
    n = w.shape[0]
    return _matmul(
        name, a, w, (m, n), out_dtype, (m // tm, n // tn, k // tk),
        pl.BlockSpec((tm, tk), lambda i, j, kk: (i, kk)),
        pl.BlockSpec((tn, tk), lambda i, j, kk: (j, kk)),
        pl.BlockSpec((tm, tn), lambda i, j, kk: (i, j)), nt=True, after=after)


def _mm_grad_cols(name, at, b, ns, row0, rows, after, tm=1024, tk=4096):
    t = at.shape[1]
    tk = min(tk, t)
    off = row0 // tm
    return _matmul(
        name, at, b, (N_DEV, rows, ns), BF16, (rows // tm, N_DEV, t // tk),
        pl.BlockSpec((tm, tk), lambda i, j, kk: (i + off, kk)),
        pl.BlockSpec((tk, ns), lambda i, j, kk: (kk, j)),
        pl.BlockSpec((1, tm, ns), lambda i, j, kk: (j, i, 0)), nt=False, after=after)


LN_ROWS = 256


def _ln_stats(r):
    mu = jnp.mean(r, axis=-1, keepdims=True)
    xc = r - mu
    var = jnp.mean(xc * xc, axis=-1, keepdims=True)
    rstd = lax.rsqrt(var + LN_EPS)
    return xc * rstd, rstd


def _row_spec(d):
    return pl.BlockSpec((LN_ROWS, d), lambda i: (i, 0))


def _vec_spec(d):
    return pl.BlockSpec((1, d), lambda i: (0, 0))


def _ln_fwd(name, a, m, g, b, alpha):
    t, d = a.shape
    has_m = m is not None

    def body(*refs):
        if has_m:
            a_ref, m_ref, g_ref, b_ref, r_ref, y_ref, yb_ref, yt_ref = refs
            r = alpha * a_ref[...] + m_ref[...]
            r_ref[...] = r
        else:
            a_ref, g_ref, b_ref, y_ref, yb_ref, yt_ref = refs
            r = a_ref[...]
        xhat, _ = _ln_stats(r)
        y = xhat * g_ref[...] + b_ref[...]
        y_ref[...] = y
        yb_ref[...] = y.astype(BF16)
        yt_ref[...] = y.T.astype(BF16)

    ins = [a] + ([m] if has_m else []) + [g, b]
    in_specs = [_row_spec(d)] * (2 if has_m else 1) + [_vec_spec(d)] * 2
    outs = ([jax.ShapeDtypeStruct((t, d), F32)] if has_m else []) + [
        jax.ShapeDtypeStruct((t, d), F32), jax.ShapeDtypeStruct((t, d), BF16), jax.ShapeDtypeStruct((d, t), BF16)]
    res = pl.pallas_call(
        body, name=name, grid=(t // LN_ROWS,), in_specs=in_specs,
        out_specs=[_row_spec(d)] * (len(outs) - 1) + [pl.BlockSpec((d, LN_ROWS), lambda i: (0, i))], out_shape=outs,
        compiler_params=_cparams(("parallel",)),
    )(*ins)
    return res if has_m else (None, *res)


def _ln_bwd_math(r, dy, g):
    xhat, rstd = _ln_stats(r)
    dxhat = dy * g
    m1 = jnp.mean(dxhat, axis=-1, keepdims=True)
    m2 = jnp.mean(dxhat * xhat, axis=-1, keepdims=True)
    dr = rstd * (dxhat - m1 - xhat * m2)
    return dr, jnp.sum(dy * xhat, axis=0, keepdims=True), jnp.sum(dy, axis=0, keepdims=True)


def _ln2_loss_bwd(name, h1, ffn, g, b, tgt):
    t, d = h1.shape

    def body(h1_ref, f_ref, g_ref, b_ref, t_ref, dr_ref, drb_ref, dg_ref, db_ref, loss_ref):
        @pl.when(pl.program_id(0) == 0)
        def _():
            dg_ref[...] = jnp.zeros_like(dg_ref)
            db_ref[...] = jnp.zeros_like(db_ref)
            loss_ref[...] = jnp.zeros_like(loss_ref)

        r = ALPHA * h1_ref[...] + f_ref[...]
        xhat, _ = _ln_stats(r)
        e = xhat * g_ref[...] + b_ref[...] - t_ref[...]
        loss_ref[...] += 0.5 / d * jnp.sum(e * e)
        dr, dg, db = _ln_bwd_math(r, e * (1.0 / d), g_ref[...])
        dr_ref[...] = dr
        drb_ref[...] = dr.astype(BF16)
        dg_ref[...] += dg
        db_ref[...] += db

    return pl.pallas_call(
        body, name=name, grid=(t // LN_ROWS,),
        in_specs=[_row_spec(d), _row_spec(d), _vec_spec(d), _vec_spec(d), _row_spec(d)],
        out_specs=[_row_spec(d), _row_spec(d), _vec_spec(d), _vec_spec(d), _vec_spec(128)],
        out_shape=[jax.ShapeDtypeStruct((t, d), F32), jax.ShapeDtypeStruct((t, d), BF16),
                   jax.ShapeDtypeStruct((1, d), F32), jax.ShapeDtypeStruct((1, d), F32),
                   jax.ShapeDtypeStruct((1, 128), F32)],
        compiler_params=_cparams(("arbitrary",)),
    )(h1, ffn, g, b, tgt)


def _ln_bwd(name, r, dya, dyb, g, alpha, want_bf16):
    t, d = r.shape

    def body(r_ref, dya_ref, dyb_ref, g_ref, *outs):
        dr_ref = outs[0]
        dg_ref, db_ref = outs[-2:]

        @pl.when(pl.program_id(0) == 0)
        def _():
            dg_ref[...] = jnp.zeros_like(dg_ref)
            db_ref[...] = jnp.zeros_like(db_ref)

        dy = alpha * dya_ref[...] + dyb_ref[...]
        dr, dg, db = _ln_bwd_math(r_ref[...], dy, g_ref[...])
        dr_ref[...] = dr
        if want_bf16:
            outs[1][...] = dr.astype(BF16)
        dg_ref[...] += dg
        db_ref[...] += db

    big = [jax.ShapeDtypeStruct((t, d), F32)] + ([jax.ShapeDtypeStruct((t, d), BF16)] if want_bf16 else [])
    return pl.pallas_call(
        body, name=name, grid=(t // LN_ROWS,),
        in_specs=[_row_spec(d)] * 3 + [_vec_spec(d)],
        out_specs=[_row_spec(d)] * len(big) + [_vec_spec(d)] * 2,
        out_shape=big + [jax.ShapeDtypeStruct((1, d), F32)] * 2,
        compiler_params=_cparams(("arbitrary",)),
    )(r, dya, dyb, g)


CONV_ROWS = 64
CONV_UNROLL = 4
FFN_UNROLL = 2


def _unrolled(n, unroll, fn, init):
    def body(i, carry):
        for u in range(unroll):
            carry = fn(i * unroll + u, carry)
        return carry

    return lax.fori_loop(0, n // unroll, body, init)


def _for_shifted(ref, r0, tm, shifts, fn):
    for s in shifts:
        fn(s, ref[pl.ds(r0 + s, tm), :])


def _col_spec(t, cb, off=0):
    return pl.BlockSpec((t, cb), lambda j: (0, j + off))


def _ffn_act_fwd(name, hf, w, b, cb=128):
    t = hf.shape[0]
    f = hf.shape[1] // 2
    nb = f // cb
    tm = CONV_ROWS

    def body(g_ref, v_ref, w_ref, b_ref, act_ref, pad_ref):
        pad_ref[pl.ds(0, 8), :] = jnp.zeros((8, cb), F32)
        pad_ref[pl.ds(8, t), :] = g_ref[...].astype(F32)
        wv = [w_ref[pl.ds(k, 1), :] for k in range(FFN_KERNEL)]
        bias = b_ref[...]

        def tile(i, carry):
            r0 = pl.multiple_of(i * tm, tm)
            acc = [jnp.broadcast_to(bias, (tm, cb))]

            def tap(s, rows):
                acc[0] = acc[0] + wv[s - 6] * rows

            _for_shifted(pad_ref, r0, tm, (6, 7, 8), tap)
            gc = acc[0]
            act_ref[pl.ds(r0, tm), :] = (gc * _sigmoid(gc) * v_ref[pl.ds(r0, tm), :].astype(F32)).astype(BF16)
            return carry

        _unrolled(t // tm, FFN_UNROLL, tile, 0)

    return pl.pallas_call(
        body, name=name, grid=(nb,),
        in_specs=[_col_spec(t, cb), _col_spec(t, cb, nb),
                  pl.BlockSpec((FFN_KERNEL, cb), lambda j: (0, j)), pl.BlockSpec((1, cb), lambda j: (0, j))],
        out_specs=_col_spec(t, cb), out_shape=jax.ShapeDtypeStruct((t, f), BF16),
        scratch_shapes=[pltpu.VMEM((t + 8, cb), F32)],
        compiler_params=_cparams(("parallel",)),
    )(hf, hf, w, b)


def _ffn_act_bwd(name, dact, hf, w, b, cb=128):
    t = hf.shape[0]
    f = hf.shape[1] // 2
    nb = f // cb
    tm = CONV_ROWS

    def body(da_ref, g_ref, v_ref, w_ref, b_ref, dhf_ref, dw_ref, db_ref, pad_ref, dgc_ref):
        pad_ref[pl.ds(0, 8), :] = jnp.zeros((8, cb), F32)
        pad_ref[pl.ds(8, t), :] = g_ref[...].astype(F32)
        dgc_ref[pl.ds(t, 8), :] = jnp.zeros((8, cb), F32)
        wv = [w_ref[pl.ds(k, 1), :] for k in range(FFN_KERNEL)]
        bias = b_ref[...]

        def tile_a(i, carry):
            r0 = pl.multiple_of(i * tm, tm)
            taps = {}
            _for_shifted(pad_ref, r0, tm, (6, 7, 8), lambda s, rows: taps.__setitem__(s, rows))
            gc = bias + wv[0] * taps[6] + wv[1] * taps[7] + wv[2] * taps[8]
            sg = _sigmoid(gc)
            da = da_ref[pl.ds(r0, tm), :].astype(F32)
            dhf_ref[1, pl.ds(r0, tm), :] = (da * gc * sg).astype(BF16)
            dgc = da * v_ref[pl.ds(r0, tm), :].astype(F32) * sg * (1.0 + gc * (1.0 - sg))
            dgc_ref[pl.ds(r0, tm), :] = dgc
            sums = [jnp.sum(dgc * taps[6 + k], axis=0, keepdims=True) for k in range(3)]
            sums.append(jnp.sum(dgc, axis=0, keepdims=True))
            return tuple(c + s for c, s in zip(carry, sums))

        zero = jnp.zeros((1, cb), F32)
        dw0, dw1, dw2, dbias = _unrolled(t // tm, FFN_UNROLL, tile_a, (zero, zero, zero, zero))
        row = lax.broadcasted_iota(jnp.int32, (8, cb), 0)
        dw_ref[...] = jnp.where(row == 0, dw0, jnp.where(row == 1, dw1, jnp.where(row == 2, dw2, 0.0)))
        db_ref[...] = dbias

        def tile_b(i, carry):
            r0 = pl.multiple_of(i * tm, tm)
            acc = [jnp.zeros((tm, cb), F32)]

            def tap(s, rows):
                acc[0] = acc[0] + wv[2 - s] * rows

            _for_shifted(dgc_ref, r0, tm, (0, 1, 2), tap)
            dhf_ref[0, pl.ds(r0, tm), :] = acc[0].astype(BF16)
            return carry

        lax.fori_loop(0, t // tm, tile_b, 0)

    return pl.pallas_call(
        body, name=name, grid=(nb,),
        in_specs=[_col_spec(t, cb), _col_spec(t, cb), _col_spec(t, cb, nb),
                  pl.BlockSpec((FFN_KERNEL, cb), lambda j: (0, j)), pl.BlockSpec((1, cb), lambda j: (0, j))],
        out_specs=[pl.BlockSpec((2, t, cb), lambda j: (0, 0, j)),
                   pl.BlockSpec((8, cb), lambda j: (0, j)), pl.BlockSpec((1, cb), lambda j: (0, j))],
        out_shape=[jax.ShapeDtypeStruct((2, t, f), BF16), jax.ShapeDtypeStruct((8, f), F32),
                   jax.ShapeDtypeStruct((1, f), F32)],
        scratch_shapes=[pltpu.VMEM((t + 8, cb), F32), pltpu.VMEM((t + 8, cb), F32)],
        compiler_params=_cparams(("parallel",)),
    )(dact, hf, hf, w, b)


def _silu_grad(z, sg):
    return sg * (1.0 + z * (1.0 - sg))


def _conv_fwd(name, hin, w, b, ng, nb_, cat):
    t = hin.shape[0]
    c = GROUP
    tm = CONV_ROWS
    pad = 32
    shifts = tuple(2 + k for k in range(CONV_KERNEL))

    def body(a_ref, gt_ref, w_ref, b_ref, ng_ref, nb_ref, cat_ref, u1_ref, u3_ref, pad_ref):
        pad_ref[pl.ds(0, pad), :] = jnp.zeros((pad, c), F32)
        pad_ref[pl.ds(pad, t), :] = a_ref[...] * _sigmoid(gt_ref[...])
        bias, gam, bet = b_ref[...], ng_ref[...], nb_ref[...]

        def tile(i, carry):
            r0 = pl.multiple_of(i * tm, tm)
            acc = [jnp.broadcast_to(bias, (tm, c))]

            def tap(s, rows):
                acc[0] = acc[0] + w_ref[pl.ds(s - 2, 1), :] * rows

            _for_shifted(pad_ref, r0, tm, shifts, tap)
            u1 = acc[0]
            u1_ref[pl.ds(r0, tm), :] = u1
            xhat, _ = _ln_stats(u1)
            u2 = xhat * gam + bet
            u3_ref[pl.ds(r0, tm), :] = (u2 * _sigmoid(u2)).astype(BF16)
            return carry

        _unrolled(t // tm, CONV_UNROLL, tile, 0)

    vec = pl.BlockSpec((1, c), lambda j: (0, j))
    return pl.pallas_call(
        body, name=name, grid=(N_GROUPS,),
        in_specs=[_col_spec(t, c), _col_spec(t, c, N_GROUPS),
                  pl.BlockSpec((CONV_KERNEL, c), lambda j: (0, j)), vec, vec, vec, ANY],
        out_specs=[_col_spec(t, c), _col_spec(t, c)],
        out_shape=[jax.ShapeDtypeStruct((t, CONV_WIDTH), F32), jax.ShapeDtypeStruct(cat.shape, BF16)],
        input_output_aliases={6: 1},
        scratch_shapes=[pltpu.VMEM((t + pad, c), F32)],
        compiler_params=_cparams(("parallel",)),
    )(hin, hin, w, b, ng, nb_, cat)


def _conv_bwd(name, dcat, u1, hin, w, ng, nb_):
    t = hin.shape[0]
    c = GROUP
    tm = CONV_ROWS
    pad = 32
    nk = CONV_KERNEL

    def body(du3_ref, u1_ref, a_ref, gt_ref, w_ref, ng_ref, nb_ref,
             da_ref, dgt_ref, dw_ref, db_ref, dng_ref, dnb_ref, u0_ref, du1_ref, dwp_ref):
        u0_ref[pl.ds(0, pad), :] = jnp.zeros((pad, c), F32)
        u0_ref[pl.ds(pad, t), :] = a_ref[...] * _sigmoid(gt_ref[...])
        du1_ref[pl.ds(t, pad), :] = jnp.zeros((pad, c), F32)
        dwp_ref[...] = jnp.zeros_like(dwp_ref)
        gam, bet = ng_ref[...], nb_ref[...]

        def tile_a(i, carry):
            r0 = pl.multiple_of(i * tm, tm)
            u1 = u1_ref[pl.ds(r0, tm), :]
            xhat, rstd = _ln_stats(u1)
            u2 = xhat * gam + bet
            sg = _sigmoid(u2)
            du2 = du3_ref[pl.ds(r0, tm), :] * _silu_grad(u2, sg)
            dxhat = du2 * gam
            m1 = jnp.mean(dxhat, axis=-1, keepdims=True)
            m2 = jnp.mean(dxhat * xhat, axis=-1, keepdims=True)
            du1 = rstd * (dxhat - m1 - xhat * m2)
            du1_ref[pl.ds(r0, tm), :] = du1
            sums = (jnp.sum(du1, axis=0, keepdims=True), jnp.sum(du2 * xhat, axis=0, keepdims=True),
                    jnp.sum(du2, axis=0, keepdims=True))
            return tuple(x + s for x, s in zip(carry, sums))

        zero = jnp.zeros((1, c), F32)
        dbias, dgam, dbet = _unrolled(t // tm, CONV_UNROLL, tile_a, (zero, zero, zero))
        db_ref[...] = dbias
        dng_ref[...] = dgam
        dnb_ref[...] = dbet

        def tile_b(i, carry):
            r0 = pl.multiple_of(i * tm, tm)
            du1 = du1_ref[pl.ds(r0, tm), :]
            acc = [jnp.zeros((tm, c), F32)]

            def tap_dx(s, rows):
                acc[0] = acc[0] + w_ref[pl.ds(nk - 1 - s, 1), :] * rows

            _for_shifted(du1_ref, r0, tm, tuple(range(nk)), tap_dx)

            def tap_dw(s, rows):
                part = (du1 * rows).reshape(tm // 8, 8, c).sum(axis=0)
                dwp_ref[s - 2] = dwp_ref[s - 2] + part

            _for_shifted(u0_ref, r0, tm, tuple(2 + k for k in range(nk)), tap_dw)
            du0 = acc[0]
            a = a_ref[pl.ds(r0, tm), :]
            sg = _sigmoid(gt_ref[pl.ds(r0, tm), :])
            da_ref[pl.ds(r0, tm), :] = (du0 * sg).astype(BF16)
            dgt_ref[pl.ds(r0, tm), :] = (du0 * a * sg * (1.0 - sg)).astype(BF16)
            return carry

        lax.fori_loop(0, t // tm, tile_b, 0)
        dw_ref[...] = jnp.sum(dwp_ref[...], axis=1)

    vec = pl.BlockSpec((1, c), lambda j: (0, j))
    vshape = jax.ShapeDtypeStruct((1, CONV_WIDTH), F32)
    return pl.pallas_call(
        body, name=name, grid=(N_GROUPS,),
        in_specs=[_col_spec(t, c), _col_spec(t, c), _col_spec(t, c), _col_spec(t, c, N_GROUPS),
                  pl.BlockSpec((nk, c), lambda j: (0, j)), vec, vec],
        out_specs=[_col_spec(t, c), _col_spec(t, c), pl.BlockSpec((32, c), lambda j: (0, j)), vec, vec, vec],
        out_shape=[jax.ShapeDtypeStruct((t, CONV_WIDTH), BF16), jax.ShapeDtypeStruct((t, CONV_WIDTH), BF16),
                   jax.ShapeDtypeStruct((32, CONV_WIDTH), F32), vshape, vshape, vshape],
        scratch_shapes=[pltpu.VMEM((t + pad, c), F32), pltpu.VMEM((t + pad, c), F32),
                        pltpu.VMEM((32, 8, c), F32)],
        compiler_params=_cparams(("parallel",)),
    )(dcat, u1, hin, hin, w, ng, nb_)


LEVELS = (64, 32, 16)
HGRN_UNROLL = 2
HGRN_UNROLL_FWD = 4
NT_DIMS = (((1,), (1,)), ((), ()))
NN_DIMS = (((1,), (0,)), ((), ()))
TN_DIMS = (((0,), (0,)), ((), ()))


def _bdot(a, b, dims):
    return lax.dot_general(a.astype(BF16), b.astype(BF16), dims, preferred_element_type=F32)


def _hdot(a, b):
    return jnp.dot(a, b, precision=lax.Precision.HIGHEST, preferred_element_type=F32)


def _chunk_consts():
    rid = lax.broadcasted_iota(jnp.int32, (CHUNK, GROUP), 0)
    ti = lax.broadcasted_iota(jnp.int32, (CHUNK, CHUNK), 0)
    si = lax.broadcasted_iota(jnp.int32, (CHUNK, CHUNK), 1)
    tri = (si <= ti).astype(F32)
    second = [(rid & (b // 2)) != 0 for b in LEVELS]
    same = [None] + [(ti // b) == (si // b) for b in LEVELS[1:]]
    sub = lax.broadcasted_iota(jnp.int32, (SUB, GROUP), 0)
    return rid, tri, second, same, sub


def _level_refs(cum_ref, rid, base):
    row = lambda i: cum_ref[pl.ds(base + i, 1), :]
    l1 = jnp.broadcast_to(row(31), (CHUNK, GROUP))
    l2 = jnp.where(rid < 32, row(15), row(47))
    l3 = jnp.where(rid < 16, row(7), jnp.where(rid < 32, row(23), jnp.where(rid < 48, row(39), row(55))))
    return l1, l2, l3


def _level_factors(cum, brefs, second):
    out = []
    for bref, sec in zip(brefs, second):
        eq = jnp.where(sec, jnp.exp(jnp.minimum(cum - bref, 0.0)), 0.0)
        ek = jnp.where(sec, 0.0, jnp.exp(jnp.minimum(bref - cum, 0.0)))
        out.append((eq, ek))
    return out


def _gates(q, f, lb):
    sq = _sigmoid(q)
    sf = _sigmoid(f)
    fg = lb + (1.0 - lb) * sf
    return q * sq, sq, sf, fg


def _hgrn_specs(t, nc):
    c = GROUP
    col = lambda off: pl.BlockSpec((t, c), lambda h: (0, h + off))
    hin_specs = [col(16), col(24), col(32), col(40)]
    vec = pl.BlockSpec((1, c), lambda h: (0, h))
    lbs = pl.BlockSpec((2, c), lambda h: (0, h))
    st = pl.BlockSpec((1, nc, c, c), lambda h: (h, 0, 0, 0))
    return col, hin_specs, vec, lbs, st


def _hgrn_fwd(name, hin, lb_logits, hg):
    t = hin.shape[0]
    nc = t // CHUNK
    c = GROUP
    col, hin_specs, vec, lbs, st = _hgrn_specs(t, nc)

    def body(q_ref, f_ref, v_ref, og_ref, lb_ref, hg_ref, o_ref, ob_ref, st_ref,
             s_ref, cum_ref, kk_ref, vc_ref):
        rid, tri, second, same, sub = _chunk_consts()
        lb = _sigmoid(lb_ref[pl.ds(0, 1), :] - lb_ref[pl.ds(1, 1), :])
        gain = hg_ref[...]
        s_ref[...] = jnp.zeros_like(s_ref)

        def chunk(ci, u):
            base = u * CHUNK
            r0 = pl.multiple_of(ci * CHUNK, CHUNK)
            rows = pl.ds(r0, CHUNK)
            qh, _, _, fg = _gates(q_ref[rows, :], f_ref[rows, :], lb)
            v = v_ref[rows, :]
            kk = 1.0 - fg
            cum = _hdot(tri, jnp.log(fg))
            cum_ref[pl.ds(base, CHUNK), :] = cum
            kk_ref[pl.ds(base, CHUNK), :] = kk
            vc_ref[pl.ds(base, CHUNK), :] = v
            sprev = s_ref[...]
            st_ref[0, ci] = sprev
            blast = cum_ref[pl.ds(base + CHUNK - 1, 1), :]
            o = _bdot(qh * jnp.exp(cum), sprev, NT_DIMS)
            s_ref[...] = sprev * jnp.exp(blast) + _bdot(v, kk * jnp.exp(blast - cum), TN_DIMS)
            a = None
            for (eq, ek), msk in zip(_level_factors(cum, _level_refs(cum_ref, rid, base), second), same):
                al = _bdot(qh * eq, kk * ek, NT_DIMS)
                al = al if msk is None else jnp.where(msk, al, 0.0)
                a = al if a is None else a + al
            o = o + _bdot(a, v, NN_DIMS)
            diag = []
            for sb in range(CHUNK // SUB):
                lo = sb * SUB
                qb = qh[lo:lo + SUB]
                cb = cum[lo:lo + SUB]
                od = jnp.zeros((SUB, c), F32)
                for s in range(SUB):
                    e = jnp.where(sub >= s, jnp.exp(jnp.minimum(cb - cum_ref[pl.ds(base + lo + s, 1), :], 0.0)), 0.0)
                    acol = jnp.sum(qb * e * kk_ref[pl.ds(base + lo + s, 1), :], axis=-1, keepdims=True)
                    od = od + acol * vc_ref[pl.ds(base + lo + s, 1), :]
                diag.append(od)
            o = o + jnp.concatenate(diag, axis=0)
            o_ref[rows, :] = o
            y = o * lax.rsqrt(jnp.mean(o * o, axis=-1, keepdims=True) + RMS_EPS) * gain
            og = og_ref[rows, :]
            ob_ref[rows, :] = (y * og * _sigmoid(og)).astype(BF16)

        def chunks(i, carry):
            for u in range(HGRN_UNROLL_FWD):
                chunk(i * HGRN_UNROLL_FWD + u, u)
            return carry

        lax.fori_loop(0, nc // HGRN_UNROLL_FWD, chunks, 0)

    return pl.pallas_call(
        body, name=name, grid=(N_GROUPS,),
        in_specs=hin_specs + [lbs, vec],
        out_specs=[col(0), col(N_GROUPS), st],
        out_shape=[jax.ShapeDtypeStruct((t, HGRN_WIDTH), F32), jax.ShapeDtypeStruct((t, CONV_WIDTH + HGRN_WIDTH), BF16),
                   jax.ShapeDtypeStruct((N_GROUPS, nc, c, c), F32)],
        scratch_shapes=[pltpu.VMEM((c, c), F32)] + [pltpu.VMEM((HGRN_UNROLL_FWD * CHUNK, c), F32)] * 3,
        compiler_params=_cparams(("parallel",)),
    )(hin, hin, hin, hin, lb_logits, hg)


def _hgrn_bwd(name, dcat, hin, o_raw, states, lb_logits, hg):
    t = hin.shape[0]
    nc = t // CHUNK
    c = GROUP
    col, hin_specs, vec, lbs, st = _hgrn_specs(t, nc)

    def body(do_ref, q_ref, f_ref, v_ref, og_ref, o_ref, st_ref, lb_ref, hg_ref,
             dq_ref, df_ref, dv_ref, dog_ref, dhg_ref, dlb_ref,
             ds_ref, cum_ref, kk_ref, vc_ref):
        rid, tri, second, same, sub = _chunk_consts()
        trit = tri.T
        lb = _sigmoid(lb_ref[pl.ds(0, 1), :] - lb_ref[pl.ds(1, 1), :])
        gain = hg_ref[...]
        ds_ref[...] = jnp.zeros_like(ds_ref)

        def chunk(i, carry, u):
            base = u * CHUNK
            dhg, dlb = carry
            ci = nc - 1 - i
            r0 = pl.multiple_of(ci * CHUNK, CHUNK)
            rows = pl.ds(r0, CHUNK)
            q = q_ref[rows, :]
            qh, sq, sf, fg = _gates(q, f_ref[rows, :], lb)
            v = v_ref[rows, :]
            kk = 1.0 - fg
            cum = _hdot(tri, jnp.log(fg))
            cum_ref[pl.ds(base, CHUNK), :] = cum
            kk_ref[pl.ds(base, CHUNK), :] = kk
            vc_ref[pl.ds(base, CHUNK), :] = v
            o = o_ref[rows, :]
            og = og_ref[rows, :]
            sg = _sigmoid(og)
            rinv = lax.rsqrt(jnp.mean(o * o, axis=-1, keepdims=True) + RMS_EPS)
            yn = o * rinv
            dof = do_ref[rows, :]
            dog_ref[rows, :] = (dof * yn * gain * _silu_grad(og, sg)).astype(BF16)
            dz = dof * og * sg
            dhg = dhg + jnp.sum(dz * yn, axis=0, keepdims=True)
            dy = dz * gain
            do = rinv * (dy - yn * jnp.mean(dy * yn, axis=-1, keepdims=True))
            sprev = st_ref[0, ci]
            dsn = ds_ref[...]
            blast = cum_ref[pl.ds(base + CHUNK - 1, 1), :]
            eq0 = jnp.exp(cum)
            ek0 = jnp.exp(blast - cum)
            dqh = _bdot(do, sprev, NN_DIMS) * eq0
            dkk = _bdot(v, dsn, NN_DIMS) * ek0
            dlast = (jnp.sum(kk * dkk, axis=0, keepdims=True)
                     + jnp.exp(blast) * jnp.sum(dsn * sprev, axis=0, keepdims=True))
            dv = _bdot(kk * ek0, dsn, NT_DIMS)
            ds_ref[...] = dsn * jnp.exp(blast) + _bdot(do, qh * eq0, TN_DIMS)
            dg = qh * dqh - kk * dkk
            da = _bdot(do, v, NT_DIMS)
            a = None
            for (eq, ek), msk in zip(_level_factors(cum, _level_refs(cum_ref, rid, base), second), same):
                ql, kl = (qh * eq).astype(BF16), (kk * ek).astype(BF16)
                al = _bdot(ql, kl, NT_DIMS)
                dal = da
                if msk is not None:
                    al = jnp.where(msk, al, 0.0)
                    dal = jnp.where(msk, da, 0.0)
                a = al if a is None else a + al
                dql = _bdot(dal, kl, NN_DIMS)
                dkl = _bdot(dal, ql, TN_DIMS)
                dqh = dqh + dql * eq
                dkk = dkk + dkl * ek
                dg = dg + (ql.astype(F32) * dql - kl.astype(F32) * dkl)
            dv = dv + _bdot(a, do, TN_DIMS)
            dq_d, dk_d, dv_d = [], [], []
            for sb in range(CHUNK // SUB):
                lo = sb * SUB
                qb = qh[lo:lo + SUB]
                cb = cum[lo:lo + SUB]
                dob = do[lo:lo + SUB]
                dqb = jnp.zeros((SUB, c), F32)
                dkb = jnp.zeros((SUB, c), F32)
                dvb = jnp.zeros((SUB, c), F32)
                for s in range(SUB):
                    e = jnp.where(sub >= s, jnp.exp(jnp.minimum(cb - cum_ref[pl.ds(base + lo + s, 1), :], 0.0)), 0.0)
                    ks = kk_ref[pl.ds(base + lo + s, 1), :]
                    qe = qb * e
                    dacol = jnp.sum(dob * vc_ref[pl.ds(base + lo + s, 1), :], axis=-1, keepdims=True)
                    acol = jnp.sum(qe * ks, axis=-1, keepdims=True)
                    dqb = dqb + dacol * (ks * e)
                    dkb = jnp.where(sub == s, jnp.sum(dacol * qe, axis=0, keepdims=True), dkb)
                    dvb = jnp.where(sub == s, jnp.sum(acol * dob, axis=0, keepdims=True), dvb)
                dq_d.append(dqb)
                dk_d.append(dkb)
                dv_d.append(dvb)
            dq_d = jnp.concatenate(dq_d, axis=0)
            dk_d = jnp.concatenate(dk_d, axis=0)
            dqh = dqh + dq_d
            dkk = dkk + dk_d
            dg = dg + (qh * dq_d - kk * dk_d)
            dv = dv + jnp.concatenate(dv_d, axis=0)
            dlf = _hdot(trit, dg) + dlast
            dfg = dlf / fg - dkk
            df_ref[rows, :] = (dfg * (1.0 - lb) * sf * (1.0 - sf)).astype(BF16)
            dlb = dlb + jnp.sum(dfg * (1.0 - sf), axis=0, keepdims=True)
            dq_ref[rows, :] = (dqh * _silu_grad(q, sq)).astype(BF16)
            dv_ref[rows, :] = dv.astype(BF16)
            return dhg, dlb

        def chunks(i, carry):
            for u in range(HGRN_UNROLL):
                carry = chunk(i * HGRN_UNROLL + u, carry, u)
            return carry

        zero = jnp.zeros((1, c), F32)
        dhg, dlb = lax.fori_loop(0, nc // HGRN_UNROLL, chunks, (zero, zero))
        dhg_ref[...] = dhg
        dl0 = dlb * lb * (1.0 - lb)
        dlb_ref[...] = jnp.where(lax.broadcasted_iota(jnp.int32, (2, c), 0) == 0, dl0, -dl0)

    big = jax.ShapeDtypeStruct((t, HGRN_WIDTH), BF16)
    return pl.pallas_call(
        body, name=name, grid=(N_GROUPS,),
        in_specs=[col(8)] + hin_specs + [col(0), st, lbs, vec],
        out_specs=[col(0)] * 4 + [vec, lbs],
        out_shape=[big] * 4 + [jax.ShapeDtypeStruct((1, HGRN_WIDTH), F32), jax.ShapeDtypeStruct((2, HGRN_WIDTH), F32)],
        scratch_shapes=[pltpu.VMEM((c, c), F32)] + [pltpu.VMEM((HGRN_UNROLL * CHUNK, c), F32)] * 3,
        compiler_params=_cparams(("parallel",)),
    )(dcat, hin, hin, hin, hin, o_raw, states, lb_logits, hg)


ANY = pl.BlockSpec(memory_space=pl.ANY)


def _my_place():
    return lax.axis_index("x"), lax.axis_index("y"), lax.axis_index("c")


HBM = pl.BlockSpec(memory_space=pltpu.HBM)
SEM = pl.BlockSpec(memory_space=pltpu.SEMAPHORE)
EFFECT = pltpu.SideEffectType.DATAFLOW_SIDE_EFFECTING


def _peer(k):
    x, y, c = _my_place()
    px = 1 - x if k & 4 else x
    py = 1 - y if k & 2 else y
    pc = 1 - c if k & 1 else c
    return (px, py, pc), 4 * px + 2 * py + pc


def _slot(land_ref, idx):
    if len(land_ref.shape) == 2:
        ns = land_ref.shape[1] // N_DEV
        return land_ref.at[:, pl.ds(pl.multiple_of(idx * ns, 128), ns)]
    return land_ref.at[idx]


def _exchange_copy(k, src_ref, land_ref, send_sems, recv_sems, scatter, landing):
    x, y, c = _my_place()
    me = 4 * x + 2 * y + c
    to, idx = _peer(k)
    return pltpu.make_async_remote_copy(
        src_ref=_slot(src_ref, idx) if scatter else src_ref,
        dst_ref=_slot(land_ref, idx) if landing else _slot(land_ref, me),
        send_sem=send_sems.at[k - 1], recv_sem=recv_sems.at[k - 1], device_id=to, device_id_type=MESH)


ALL_PEERS = tuple(range(1, N_DEV))
NEAR_PEERS = (1, 2, 4, 6)
SAME_CORE_PEERS = (2, 4, 6)


def _exchange_start(name, src, land, scatter, ks=ALL_PEERS):
    def body(src_ref, land_ref, send_sems, recv_sems, src_thru, land_thru, token):
        for k in ks:
            _exchange_copy(k, src_ref, land_ref, send_sems, recv_sems, scatter, landing=False).start()
        token[...] = jnp.zeros_like(token)

    send_sems, recv_sems, src_thru, land_thru, token = pl.pallas_call(
        body, name=name,
        out_shape=(pltpu.SemaphoreType.DMA((N_DEV - 1,)), pltpu.SemaphoreType.DMA((N_DEV - 1,)),
                   pltpu.HBM(src.shape, src.dtype), pltpu.HBM(land.shape, land.dtype),
                   jax.ShapeDtypeStruct((8, 128), F32)),
        in_specs=(HBM, HBM), out_specs=(SEM, SEM, HBM, HBM, pl.BlockSpec(memory_space=pltpu.VMEM)),
        input_output_aliases={0: 2, 1: 3},
        compiler_params=pltpu.CompilerParams(has_side_effects=EFFECT),
    )(pltpu.with_memory_space_constraint(src, pltpu.HBM), pltpu.with_memory_space_constraint(land, pltpu.HBM))
    return (send_sems, recv_sems, src_thru, land_thru, scatter, ks), token


def _exchange_wait(name, handle, after):
    send_sems, recv_sems, src_thru, land_thru, scatter, ks = handle

    def body(src_ref, land_ref, send_sems, recv_sems, after_ref, src_dead, got_ref):
        for k in ks:
            cp = _exchange_copy(k, src_ref, land_ref, send_sems, recv_sems, scatter, landing=True)
            cp.wait_send()
            cp.wait_recv()

    return pl.pallas_call(
        body, name=name,
        out_shape=(pltpu.HBM(src_thru.shape, src_thru.dtype), pltpu.HBM(land_thru.shape, land_thru.dtype)),
        in_specs=(HBM, HBM, SEM, SEM, ANY), out_specs=(HBM, HBM), input_output_aliases={0: 0, 1: 1},
        compiler_params=pltpu.CompilerParams(has_side_effects=EFFECT),
    )(src_thru, land_thru, send_sems, recv_sems, after)[1]


def _relay_copy(j, land_ref, send_sems, recv_sems, landing):
    x, y, c = _my_place()
    k = SAME_CORE_PEERS[j]
    _, sent = _peer(k)
    _, got = _peer(k + 1)
    return pltpu.make_async_remote_copy(
        src_ref=_slot(land_ref, sent), dst_ref=_slot(land_ref, got) if landing else _slot(land_ref, sent),
        send_sem=send_sems.at[j], recv_sem=recv_sems.at[j], device_id=(x, y, 1 - c), device_id_type=MESH)


def _relay_start(name, land):
    n = len(SAME_CORE_PEERS)

    def body(land_ref, send_sems, recv_sems, land_thru, token):
        for j in range(n):
            _relay_copy(j, land_ref, send_sems, recv_sems, landing=False).start()
        token[...] = jnp.zeros_like(token)

    send_sems, recv_sems, land_thru, token = pl.pallas_call(
        body, name=name,
        out_shape=(pltpu.SemaphoreType.DMA((n,)), pltpu.SemaphoreType.DMA((n,)),
                   pltpu.HBM(land.shape, land.dtype), jax.ShapeDtypeStruct((8, 128), F32)),
        in_specs=(HBM,), out_specs=(SEM, SEM, HBM, pl.BlockSpec(memory_space=pltpu.VMEM)),
        input_output_aliases={0: 2},
        compiler_params=pltpu.CompilerParams(has_side_effects=EFFECT),
    )(pltpu.with_memory_space_constraint(land, pltpu.HBM))
    return (send_sems, recv_sems, land_thru), token


def _relay_wait(name, handle, after):
    send_sems, recv_sems, land_thru = handle

    def body(land_ref, send_sems, recv_sems, after_ref, got_ref):
        for j in range(len(SAME_CORE_PEERS)):
            cp = _relay_copy(j, land_ref, send_sems, recv_sems, landing=True)
            cp.wait_send()
            cp.wait_recv()

    return pl.pallas_call(
        body, name=name, out_shape=pltpu.HBM(land_thru.shape, land_thru.dtype),
        in_specs=(HBM, SEM, SEM, ANY), out_specs=HBM, input_output_aliases={0: 0},
        compiler_params=pltpu.CompilerParams(has_side_effects=EFFECT),
    )(land_thru, send_sems, recv_sems, after)


def _own_cols(name, own, me):
    r, ns = own.shape
    tr = 256

    def body(me_ref, own_ref, land_ref):
        land_ref[...] = own_ref[...]

    return pl.pallas_call(
        body, name=name,
        grid_spec=pltpu.PrefetchScalarGridSpec(
            num_scalar_prefetch=1, grid=(r // tr,),
            in_specs=[pl.BlockSpec((tr, ns), lambda i, me_ref: (i, 0))],
            out_specs=pl.BlockSpec((tr, ns), lambda i, me_ref: (i, me_ref[0]))),
        out_shape=jax.ShapeDtypeStruct((r, N_DEV * ns), own.dtype),
    )(jnp.reshape(me, (1,)).astype(jnp.int32), own)


def _own_slot(own, me):
    land = lax.empty((N_DEV,) + own.shape, own.dtype)
    return lax.dynamic_update_slice_in_dim(land, own[None], me, axis=0)


def _adamw_math(w, g, m, v):
    m = ADAM_B1 * m + (1.0 - ADAM_B1) * g
    v = ADAM_B2 * v + (1.0 - ADAM_B2) * (g * g)
    m_hat = m / (1.0 - ADAM_B1 ** ADAM_STEP)
    v_hat = v / (1.0 - ADAM_B2 ** ADAM_STEP)
    delta = -ADAM_LR * (m_hat / (jnp.sqrt(v_hat) + ADAM_EPS) + ADAM_WD * w)
    return delta, m, v


def _adamw_sum(name, recv, w, m, v, tr, row0=0, partial=None):
    r, c = w.shape
    rr = recv.shape[1]
    off = row0 // tr

    def body(recv_ref, w_ref, m_ref, v_ref, *refs):
        g_ref, d_ref, mo_ref, vo_ref = refs[-4:]
        g = recv_ref[0].astype(F32)
        for j in range(1, N_DEV):
            g = g + recv_ref[j].astype(F32)
        g_ref[...] = g
        d_ref[...], mo_ref[...], vo_ref[...] = _adamw_math(w_ref[...], g, m_ref[...], v_ref[...])

    tile = pl.BlockSpec((tr, c), lambda i: (i + off, 0))
    out = jax.ShapeDtypeStruct((r, c), F32)
    prev = list(partial) if partial is not None else []
    return pl.pallas_call(
        body, name=name, grid=(rr // tr,),
        in_specs=[pl.BlockSpec((N_DEV, tr, c), lambda i: (0, i, 0)), tile, tile, tile] + [ANY] * len(prev),
        out_specs=[tile] * 4, out_shape=[out] * 4,
        input_output_aliases={4 + i: i for i in range(len(prev))},
        compiler_params=_cparams(("parallel",)),
    )(recv, w, m, v, *prev)


def _sum_parts(name, parts):
    _, r, c = parts.shape

    def body(p_ref, o_ref):
        acc = p_ref[0]
        for j in range(1, N_DEV):
            acc = acc + p_ref[j]
        o_ref[...] = acc

    return pl.pallas_call(body, name=name, out_shape=jax.ShapeDtypeStruct((r, c), F32),
                          compiler_params=_cparams())(parts)


def _adamw_small(name, w, g, m, v):
    def body(w_ref, g_ref, m_ref, v_ref, d_ref, mo_ref, vo_ref):
        d_ref[...], mo_ref[...], vo_ref[...] = _adamw_math(w_ref[...], g_ref[...], m_ref[...], v_ref[...])

    out = jax.ShapeDtypeStruct(w.shape, F32)
    return pl.pallas_call(body, name=name, out_shape=[out] * 3, compiler_params=_cparams())(w, g, m, v)


def _pack(pieces, rows):
    flat = jnp.concatenate([p.reshape(-1).astype(F32) for p in pieces])
    return jnp.pad(flat, (0, rows * 128 - flat.shape[0])).reshape(rows, 128)


def _unpack(packed, shapes):
    flat = packed.reshape(-1)
    out, off = [], 0
    for s in shapes:
        n = 1
        for d in s:
            n *= d
        out.append(flat[off:off + n].reshape(s))
        off += n
    return out


def kernel(x, emb_ln_g, emb_ln_b, w_in, conv_w, conv_b, conv_norm_g, conv_norm_b, lb_logits, hgrn_norm_g, w_out, ln1_g, ln1_b, w_ffn_up, ffn_conv_w, ffn_conv_b, w_ffn_down, ln2_g, ln2_b, loss_target, m_emb_ln_g, m_emb_ln_b, m_w_in, m_conv_w, m_conv_b, m_conv_norm_g, m_conv_norm_b, m_lb_logits, m_hgrn_norm_g, m_w_out, m_ln1_g, m_ln1_b, m_w_ffn_up, m_ffn_conv_w, m_ffn_conv_b, m_w_ffn_down, m_ln2_g, m_ln2_b, v_emb_ln_g, v_emb_ln_b, v_w_in, v_conv_w, v_conv_b, v_conv_norm_g, v_conv_norm_b, v_lb_logits, v_hgrn_norm_g, v_w_out, v_ln1_g, v_ln1_b, v_w_ffn_up, v_ffn_conv_w, v_ffn_conv_b, v_w_ffn_down, v_ln2_g, v_ln2_b):
    t = x.shape[1]
    me = 4 * lax.axis_index("x") + 2 * lax.axis_index("y") + lax.axis_index("c")
    x2, tgt = x[0], loss_target[0]
    ns_in, ns_up = w_in.shape[2], w_ffn_up.shape[2]
    rs_out, rs_down = w_out.shape[1], w_ffn_down.shape[1]
    cs, fs = conv_w.shape[2], ffn_conv_w.shape[2]

    def gather_start(name, w, prev, ks=ALL_PEERS, cols=False):
        shard = (w[0] + prev).astype(BF16)
        land = _own_cols(name.replace("ag_", "own_"), shard, me) if cols else _own_slot(shard, me)
        return _exchange_start(name, shard, land, scatter=False, ks=ks)

    h_in, tok = gather_start("ag_w_in_start", w_in, 0.0, NEAR_PEERS, cols=True)
    taps = _pack([conv_w[0], ffn_conv_w[0]], 48) + tok[0, 0]
    h_taps, tok = _exchange_start("ag_taps_start", taps, _own_slot(taps, me), scatter=False)
    h_out, tok = gather_start("ag_w_out_start", w_out, tok[0, 0])
    h_up, tok = gather_start("ag_w_up_start", w_ffn_up, tok[0, 0], NEAR_PEERS, cols=True)
    h_down, tok = gather_start("ag_w_down_start", w_ffn_down, tok[0, 0])

    row = lambda a: a.reshape(1, -1)

    _, h0, h0b, h0bt = _ln_fwd("ln_in", x2, None, row(emb_ln_g) + tok[0, 0], row(emb_ln_b), 1.0)
    h_relay, tok_relay = _relay_start("ag_w_in_relay_start", _exchange_wait("ag_w_in_wait", h_in, h0b))
    win_n = _relay_wait("ag_w_in_relay_wait", h_relay, tok_relay)
    hin = _mm_nn("mm_in", h0b, win_n, F32, tm=1024, tn=ns_in, tk=D_MODEL)
    n_cw, n_fw = CONV_KERNEL * cs, FFN_KERNEL * fs
    taps_g = _exchange_wait("ag_taps_wait", h_taps, hin).reshape(N_DEV, -1)
    cw_full = taps_g[:, :n_cw].reshape(N_DEV, CONV_KERNEL, cs).transpose(1, 0, 2).reshape(CONV_KERNEL, CONV_WIDTH)
    fw_full = taps_g[:, n_cw:n_cw + n_fw].reshape(N_DEV, FFN_KERNEL, fs).transpose(1, 0, 2).reshape(FFN_KERNEL, D_FF)

    o_raw, cat_right, states = _hgrn_fwd("hgrn_fwd", hin, lb_logits, hgrn_norm_g)
    u1, catb = _conv_fwd("conv_fwd", hin, cw_full, conv_b, conv_norm_g, conv_norm_b, cat_right)
    wout_g = _exchange_wait("ag_w_out_wait", h_out, catb).reshape(D_MODEL, D_MODEL)
    h_up_relay, tok = _relay_start("ag_w_up_relay_start", _exchange_wait("ag_w_up_wait", h_up, wout_g))
    mix = _mm_nn("mm_out", catb, wout_g, F32, tm=1024, tn=1024, tk=D_MODEL, after=tok)
    r1, h1, h1b, h1bt = _ln_fwd("ln1", h0, mix, ln1_g, ln1_b, ALPHA)
    wup_n = _relay_wait("ag_w_up_relay_wait", h_up_relay, h1b)
    hf = _mm_nn("mm_up", h1b, wup_n, BF16, tm=1024, tn=1024, tk=D_MODEL)
    actb = _ffn_act_fwd("ffn_act", hf, fw_full, ffn_conv_b)
    wdown_g = _exchange_wait("ag_w_down_wait", h_down, actb).reshape(D_FF, D_MODEL)
    ffn = _mm_nn("mm_down", actb, wdown_g, F32, tm=512, tn=1024, tk=D_FF)
    dr2, dr2b, g_ln2g, g_ln2b, loss = _ln2_loss_bwd("ln2_loss", h1, ffn, ln2_g, ln2_b, tgt)

    def scatter_start(name, parts):
        if parts.ndim == 2:
            ns = parts.shape[1] // N_DEV
            own = lax.dynamic_slice_in_dim(parts, me * ns, ns, axis=1)
        else:
            own = lax.dynamic_index_in_dim(parts, me, axis=0, keepdims=False)
        return _exchange_start(name, parts, _own_slot(own, me), scatter=True)

    dact = _mm_nt("mm_dact", dr2b, wdown_g, BF16, tm=1024, tn=D_FF // 2, tk=D_MODEL)
    gw_down = _matmul(
        "mm_dw_down", actb, dr2b, (D_FF, D_MODEL), BF16, (N_DEV // 2, D_MODEL // 1024, 2),
        pl.BlockSpec((t // 2, 2 * rs_down), lambda i, j, kk: (kk, i)),
        pl.BlockSpec((t // 2, 1024), lambda i, j, kk: (kk, j)),
        pl.BlockSpec((2 * rs_down, 1024), lambda i, j, kk: (i, j)), nt="tn")
    s_down, tok = scatter_start("a2a_w_down_start", gw_down.reshape(N_DEV, rs_down, D_MODEL))
    dhf, g_fw, g_fb = _ffn_act_bwd("ffn_act_bwd", dact, hf, fw_full, ffn_conv_b + tok[0, 0])
    tm = min(1024, t)
    gw_up = _matmul(
        "mm_dw_up", h1bt, dhf, (D_MODEL, 2 * D_FF), BF16, (D_MODEL // 1024, 2 * D_FF // 512, 1),
        pl.BlockSpec((1024, t), lambda i, j, kk: (i, 0)),
        pl.BlockSpec((1, t, 512), lambda i, j, kk: (j // 11, 0, j % 11)),
        pl.BlockSpec((1024, 512), lambda i, j, kk: (i, j)), nt=False)
    s_up, tok = scatter_start("a2a_w_up_start", gw_up)
    tkf = D_FF // 2
    dh1 = _matmul(
        "mm_dh1", dhf, wup_n, (t, D_MODEL), F32, (t // tm, D_MODEL // 1024, 4),
        pl.BlockSpec((1, tm, tkf), lambda i, j, kk: (kk // 2, i, kk % 2)),
        pl.BlockSpec((1024, tkf), lambda i, j, kk: (j, kk)),
        pl.BlockSpec((tm, 1024), lambda i, j, kk: (i, j)), nt=True, after=tok)
    dr1, dr1b, g_ln1g, g_ln1b = _ln_bwd("ln1_bwd", r1, dr2, dh1, ln1_g + tok[0, 0], ALPHA, True)
    gw_out = _matmul(
        "mm_dw_out", catb, dr1b, (D_MODEL, D_MODEL), BF16, (2, 2, 2),
        pl.BlockSpec((t // 2, 1024), lambda i, j, kk: (kk, i)),
        pl.BlockSpec((t // 2, 1024), lambda i, j, kk: (kk, j)),
        pl.BlockSpec((1024, 1024), lambda i, j, kk: (i, j)), nt="tn")
    s_out, tok = scatter_start("a2a_w_out_start", gw_out.reshape(N_DEV, rs_out, D_MODEL))
    dcat = _mm_nt("mm_dcat", dr1b, wout_g, F32, tm=1024, tn=1024, tk=D_MODEL, after=tok)
    da, dgate, g_cw, g_cb, g_cng, g_cnb = _conv_bwd("conv_bwd", dcat, u1, hin, cw_full, conv_norm_g + tok[0, 0],
                                                    conv_norm_b)
    dq, df, di, dog, g_hg, g_lb = _hgrn_bwd("hgrn_bwd", dcat, hin, o_raw, states, lb_logits, hgrn_norm_g)
    dhin = jnp.concatenate([da, dgate, dq, df, di, dog], axis=1)
    half = D_MODEL // 2
    gw_in_a = _mm_grad_cols("mm_dw_in_a", h0bt, dhin, ns_in, 0, half, after=tok)
    s_in_a, tok = scatter_start("a2a_w_in_a_start", gw_in_a)
    gw_in_b = _mm_grad_cols("mm_dw_in_b", h0bt, dhin, ns_in, half, half, after=tok)
    s_in_b, tok = scatter_start("a2a_w_in_b_start", gw_in_b)
    dh0 = _mm_nt("mm_dh0", dhin, win_n, F32, tm=1024, tn=1024, tk=IN_PROJ // 2, after=tok)
    grad_x, g_eg, g_eb = _ln_bwd("ln_in_bwd", x2, dr1, dh0, row(emb_ln_g), ALPHA, False)

    small_shapes = [(D_MODEL,), (D_MODEL,), (CONV_KERNEL, CONV_WIDTH), (1, CONV_WIDTH), (1, CONV_WIDTH),
                    (1, CONV_WIDTH), (2, HGRN_WIDTH), (1, HGRN_WIDTH), (1, D_MODEL), (1, D_MODEL),
                    (FFN_KERNEL, D_FF), (1, D_FF), (1, D_MODEL), (1, D_MODEL), (128,)]
    rows_small = 569
    packed = _pack([g_eg, g_eb, g_cw[:CONV_KERNEL], g_cb, g_cng, g_cnb, g_lb, g_hg, g_ln1g, g_ln1b,
                    g_fw[:FFN_KERNEL], g_fb, g_ln2g, g_ln2b, loss], rows_small)
    h_small, tok = _exchange_start("ag_small_start", packed, _own_slot(packed, me), scatter=False)

    def big(name, handle, after, w, m, v, tr):
        recv = _exchange_wait("a2a_" + name + "_wait", handle, after)
        return [o[None] for o in _adamw_sum("adamw_" + name, recv, w[0], m[0], v[0], tr)]

    u_down = big("w_down", s_down, tok, w_ffn_down, m_w_ffn_down, v_w_ffn_down, 64)
    u_up = big("w_up", s_up, u_down[1], w_ffn_up, m_w_ffn_up, v_w_ffn_up, 64)
    u_out = big("w_out", s_out, u_up[1], w_out, m_w_out, v_w_out, 64)
    summed = _sum_parts("sum_small", _exchange_wait("ag_small_wait", h_small, u_out[1]))
    (s_eg, s_eb, s_cw, s_cb, s_cng, s_cnb, s_lb, s_hg, s_l1g, s_l1b, s_fw, s_fb, s_l2g, s_l2b,
     s_loss) = _unpack(summed, small_shapes)
    s_cw = lax.dynamic_slice_in_dim(s_cw, me * cs, cs, axis=1)[None]
    s_fw = lax.dynamic_slice_in_dim(s_fw, me * fs, fs, axis=1)[None]
    g_small = [s_eg, s_eb, s_cw, s_cb, s_cng, s_cnb, s_lb, s_hg, s_l1g, s_l1b, s_fw, s_fb, s_l2g, s_l2b]
    w_small = [emb_ln_g, emb_ln_b, conv_w, conv_b, conv_norm_g, conv_norm_b, lb_logits, hgrn_norm_g,
               ln1_g, ln1_b, ffn_conv_w, ffn_conv_b, ln2_g, ln2_b]
    m_small = [m_emb_ln_g, m_emb_ln_b, m_conv_w, m_conv_b, m_conv_norm_g, m_conv_norm_b, m_lb_logits,
               m_hgrn_norm_g, m_ln1_g, m_ln1_b, m_ffn_conv_w, m_ffn_conv_b, m_ln2_g, m_ln2_b]
    v_small = [v_emb_ln_g, v_emb_ln_b, v_conv_w, v_conv_b, v_conv_norm_g, v_conv_norm_b, v_lb_logits,
               v_hgrn_norm_g, v_ln1_g, v_ln1_b, v_ffn_conv_w, v_ffn_conv_b, v_ln2_g, v_ln2_b]
    rows_own = 236
    shapes_own = [w.shape for w in w_small]
    upd = _adamw_small("adamw_small", _pack(w_small, rows_own), _pack(g_small, rows_own),
                       _pack(m_small, rows_own), _pack(v_small, rows_own))
    d_small, nm_small, nv_small = (_unpack(u, shapes_own) for u in upd)
    g_small = [g.reshape(s) for g, s in zip(g_small, shapes_own)]

    recv_a = _exchange_wait("a2a_w_in_a_wait", s_in_a, upd[0])
    part = _adamw_sum("adamw_w_in_a", recv_a, w_in[0], m_w_in[0], v_w_in[0], 128)
    recv_b = _exchange_wait("a2a_w_in_b_wait", s_in_b, part[1])
    u_in = [o[None] for o in _adamw_sum("adamw_w_in_b", recv_b, w_in[0], m_w_in[0], v_w_in[0], 128,
                                        row0=half, partial=part)]

    def ordered(small, i_in, i_out, i_up, i_down):
        (eg, eb, cw, cb, cng, cnb, lb, hg, l1g, l1b, fw, fb, l2g, l2b) = small
        return [eg, eb, i_in, cw, cb, cng, cnb, lb, hg, i_out, l1g, l1b, i_up, fw, fb, i_down, l2g, l2b]

    outs = [s_loss[0], grad_x[None]]
    for k, small in enumerate([g_small, d_small, nm_small, nv_small]):
        outs += ordered(small, u_in[k], u_out[k], u_up[k], u_down[k])
    return tuple(outs)
```

```python
import functools

import jax
import jax.numpy as jnp
from jax import lax
from jax.experimental import pallas as pl
from jax.experimental.pallas import tpu as pltpu

F32 = jnp.float32
BF16 = jnp.bfloat16

N_DEV = 8
D_MODEL = 2048
CONV_WIDTH = 1024
CONV_KERNEL = 31
HGRN_WIDTH = 1024
GROUP = 128
N_GROUPS = 8
IN_PROJ = 2 * CONV_WIDTH + 4 * HGRN_WIDTH
D_FF = 5632
FFN_KERNEL = 3
CHUNK = 64
SUB = 8
LN_EPS = 1e-5
RMS_EPS = 1e-6
ALPHA = 2.0 ** 0.25
ADAM_LR, ADAM_B1, ADAM_B2, ADAM_EPS, ADAM_WD, ADAM_STEP = 0.001, 0.9, 0.999, 1e-08, 0.01, 10

VMEM_LIMIT = 56 * 1024 * 1024
MESH = pl.DeviceIdType.MESH


def _cparams(sem=None):
    return pltpu.CompilerParams(dimension_semantics=sem, vmem_limit_bytes=VMEM_LIMIT)


def _sigmoid(x):
    return 1.0 / (1.0 + jnp.exp(-x))


def _matmul(name, a, b, out_shape, out_dtype, grid, a_spec, b_spec, o_spec, nt, after=None):
    nk = grid[2]
    dims = {True: (((1,), (1,)), ((), ())), False: (((1,), (0,)), ((), ())), "tn": (((0,), (0,)), ((), ()))}[nt]
    extra = [] if after is None else [after]

    def body(a_ref, b_ref, *rest):
        o_ref, *scratch = rest[len(extra):]
        if len(a_ref.shape) == 3 and a_ref.shape[0] > 1:
            kp = a_ref.shape[2]
            part = None
            for p in range(a_ref.shape[0]):
                d = lax.dot_general(a_ref[p], b_ref[:, p * kp:(p + 1) * kp], dims, preferred_element_type=F32)
                part = d if part is None else part + d
        else:
            av = a_ref[0] if len(a_ref.shape) == 3 else a_ref[...]
            bv = b_ref[0] if len(b_ref.shape) == 3 else b_ref[...]
            part = lax.dot_general(av, bv, dims, preferred_element_type=F32)

        def write(res):
            if len(o_ref.shape) == 3:
                o_ref[0] = res.astype(out_dtype)
            else:
                o_ref[...] = res.astype(out_dtype)

        if nk == 1:
            write(part)
            return
        acc_ref, = scratch
        k = pl.program_id(2)

        @pl.when(k == 0)
        def _():
            acc_ref[...] = part

        @pl.when(jnp.logical_and(k > 0, k < nk - 1))
        def _():
            acc_ref[...] += part

        @pl.when(k == nk - 1)
        def _():
            write(acc_ref[...] + part)

    acc_shape = o_spec.block_shape[-2:]
    assert all(g >= 1 for g in grid), (name, grid)
    return pl.pallas_call(
        body, name=name, grid=grid, in_specs=[a_spec, b_spec] + [pl.BlockSpec(memory_space=pl.ANY)] * len(extra),
        out_specs=o_spec, out_shape=jax.ShapeDtypeStruct(out_shape, out_dtype),
        scratch_shapes=[pltpu.VMEM(acc_shape, F32)] if nk > 1 else [],
        compiler_params=_cparams(("parallel", "parallel", "arbitrary")),
    )(a, b, *extra)


def _mm_nn(name, a, w, out_dtype, tm, tn, tk, after=None):
    m, k = a.shape
    tm, tk = min(tm, m), min(tk, k)
    n = w.shape[1]
    return _matmul(
        name, a, w, (m, n), out_dtype, (m // tm, n // tn, k // tk),
        pl.BlockSpec((tm, tk), lambda i, j, kk: (i, kk)),
        pl.BlockSpec((tk, tn), lambda i, j, kk: (kk, j)),
        pl.BlockSpec((tm, tn), lambda i, j, kk: (i, j)), nt=False, after=after)


def _mm_nt(name, a, w, out_dtype, tm, tn, tk, after=None):
    m, k = a.shape
    tm = min(tm, m)
    n = w.shape[0]
    return _matmul(
        name, a, w, (m, n), out_dtype, (m // tm, n // tn, k // tk),
        pl.BlockSpec((tm, tk), lambda i, j, kk: (i, kk)),
        pl.BlockSpec((tn, tk), lambda i, j, kk: (j, kk)),
        pl.BlockSpec((tm, tn), lambda i, j, kk: (i, j)), nt=True, after=after)


def _mm_grad_cols(name, at, b, ns, row0, rows, after, tm=1024, tk=4096):
    t = at.shape[1]
    tk = min(tk, t)
    off = row0 // tm
    return _matmul(
        name, at, b, (N_DEV, rows, ns), BF16, (rows // tm, N_DEV, t // tk),
        pl.BlockSpec((tm, tk), lambda i, j, kk: (i + off, kk)),
        pl.BlockSpec((tk, ns), lambda i, j, kk: (kk, j)),
        pl.BlockSpec((1, tm, ns), lambda i, j, kk: (j, i, 0)), nt=False, after=after)


LN_ROWS = 256


def _ln_stats(r):
    mu = jnp.mean(r, axis=-1, keepdims=True)
    xc = r - mu
    var = jnp.mean(xc * xc, axis=-1, keepdims=True)
    rstd = lax.rsqrt(var + LN_EPS)
    return xc * rstd, rstd


def _row_spec(d):
    return pl.BlockSpec((LN_ROWS, d), lambda i: (i, 0))


def _vec_spec(d):
    return pl.BlockSpec((1, d), lambda i: (0, 0))


def _ln_fwd(name, a, m, g, b, alpha):
    t, d = a.shape
    has_m = m is not None

    def body(*refs):
        if has_m:
            a_ref, m_ref, g_ref, b_ref, r_ref, y_ref, yb_ref, yt_ref = refs
            r = alpha * a_ref[...] + m_ref[...]
            r_ref[...] = r
        else:
            a_ref, g_ref, b_ref, y_ref, yb_ref, yt_ref = refs
            r = a_ref[...]
        xhat, _ = _ln_stats(r)
        y = xhat * g_ref[...] + b_ref[...]
        y_ref[...] = y
        yb_ref[...] = y.astype(BF16)
        yt_ref[...] = y.T.astype(BF16)

    ins = [a] + ([m] if has_m else []) + [g, b]
    in_specs = [_row_spec(d)] * (2 if has_m else 1) + [_vec_spec(d)] * 2
    outs = ([jax.ShapeDtypeStruct((t, d), F32)] if has_m else []) + [
        jax.ShapeDtypeStruct((t, d), F32), jax.ShapeDtypeStruct((t, d), BF16), jax.ShapeDtypeStruct((d, t), BF16)]
    res = pl.pallas_call(
        body, name=name, grid=(t // LN_ROWS,), in_specs=in_specs,
        out_specs=[_row_spec(d)] * (len(outs) - 1) + [pl.BlockSpec((d, LN_ROWS), lambda i: (0, i))], out_shape=outs,
        compiler_params=_cparams(("parallel",)),
    )(*ins)
    return res if has_m else (None, *res)


def _ln_bwd_math(r, dy, g):
    xhat, rstd = _ln_stats(r)
    dxhat = dy * g
    m1 = jnp.mean(dxhat, axis=-1, keepdims=True)
    m2 = jnp.mean(dxhat * xhat, axis=-1, keepdims=True)
    dr = rstd * (dxhat - m1 - xhat * m2)
    return dr, jnp.sum(dy * xhat, axis=0, keepdims=True), jnp.sum(dy, axis=0, keepdims=True)


def _ln2_loss_bwd(name, h1, ffn, g, b, tgt):
    t, d = h1.shape

    def body(h1_ref, f_ref, g_ref, b_ref, t_ref, dr_ref, drb_ref, dg_ref, db_ref, loss_ref):
        @pl.when(pl.program_id(0) == 0)
        def _():
            dg_ref[...] = jnp.zeros_like(dg_ref)
            db_ref[...] = jnp.zeros_like(db_ref)
            loss_ref[...] = jnp.zeros_like(loss_ref)

        r = ALPHA * h1_ref[...] + f_ref[...]
        xhat, _ = _ln_stats(r)
        e = xhat * g_ref[...] + b_ref[...] - t_ref[...]
        loss_ref[...] += 0.5 / d * jnp.sum(e * e)
        dr, dg, db = _ln_bwd_math(r, e * (1.0 / d), g_ref[...])
        dr_ref[...] = dr
        drb_ref[...] = dr.astype(BF16)
        dg_ref[...] += dg
        db_ref[...] += db

    return pl.pallas_call(
        body, name=name, grid=(t // LN_ROWS,),
        in_specs=[_row_spec(d), _row_spec(d), _vec_spec(d), _vec_spec(d), _row_spec(d)],
        out_specs=[_row_spec(d), _row_spec(d), _vec_spec(d), _vec_spec(d), _vec_spec(128)],
        out_shape=[jax.ShapeDtypeStruct((t, d), F32), jax.ShapeDtypeStruct((t, d), BF16),
                   jax.ShapeDtypeStruct((1, d), F32), jax.ShapeDtypeStruct((1, d), F32),
                   jax.ShapeDtypeStruct((1, 128), F32)],
        compiler_params=_cparams(("arbitrary",)),
    )(h1, ffn, g, b, tgt)


def _ln_bwd(name, r, dya, dyb, g, alpha, want_bf16):
    t, d = r.shape

    def body(r_ref, dya_ref, dyb_ref, g_ref, *outs):
        dr_ref = outs[0]
        dg_ref, db_ref = outs[-2:]

        @pl.when(pl.program_id(0) == 0)
        def _():
            dg_ref[...] = jnp.zeros_like(dg_ref)
            db_ref[...] = jnp.zeros_like(db_ref)

        dy = alpha * dya_ref[...] + dyb_ref[...]
        dr, dg, db = _ln_bwd_math(r_ref[...], dy, g_ref[...])
        dr_ref[...] = dr
        if want_bf16:
            outs[1][...] = dr.astype(BF16)
        dg_ref[...] += dg
        db_ref[...] += db

    big = [jax.ShapeDtypeStruct((t, d), F32)] + ([jax.ShapeDtypeStruct((t, d), BF16)] if want_bf16 else [])
    return pl.pallas_call(
        body, name=name, grid=(t // LN_ROWS,),
        in_specs=[_row_spec(d)] * 3 + [_vec_spec(d)],
        out_specs=[_row_spec(d)] * len(big) + [_vec_spec(d)] * 2,
        out_shape=big + [jax.ShapeDtypeStruct((1, d), F32)] * 2,
        compiler_params=_cparams(("arbitrary",)),
    )(r, dya, dyb, g)


CONV_ROWS = 64
CONV_UNROLL = 4
FFN_UNROLL = 2


def _unrolled(n, unroll, fn, init):
    def body(i, carry):
        for u in range(unroll):
            carry = fn(i * unroll + u, carry)
        return carry

    return lax.fori_loop(0, n // unroll, body, init)


def _for_shifted(ref, r0, tm, shifts, fn):
    for s in shifts:
        fn(s, ref[pl.ds(r0 + s, tm), :])


def _col_spec(t, cb, off=0):
    return pl.BlockSpec((t, cb), lambda j: (0, j + off))


def _ffn_act_fwd(name, hf, w, b, cb=128):
    t = hf.shape[0]
    f = hf.shape[1] // 2
    nb = f // cb
    tm = CONV_ROWS

    def body(g_ref, v_ref, w_ref, b_ref, act_ref, pad_ref):
        pad_ref[pl.ds(0, 8), :] = jnp.zeros((8, cb), F32)
        pad_ref[pl.ds(8, t), :] = g_ref[...].astype(F32)
        wv = [w_ref[pl.ds(k, 1), :] for k in range(FFN_KERNEL)]
        bias = b_ref[...]

        def tile(i, carry):
            r0 = pl.multiple_of(i * tm, tm)
            acc = [jnp.broadcast_to(bias, (tm, cb))]

            def tap(s, rows):
                acc[0] = acc[0] + wv[s - 6] * rows

            _for_shifted(pad_ref, r0, tm, (6, 7, 8), tap)
            gc = acc[0]
            act_ref[pl.ds(r0, tm), :] = (gc * _sigmoid(gc) * v_ref[pl.ds(r0, tm), :].astype(F32)).astype(BF16)
            return carry

        _unrolled(t // tm, FFN_UNROLL, tile, 0)

    return pl.pallas_call(
        body, name=name, grid=(nb,),
        in_specs=[_col_spec(t, cb), _col_spec(t, cb, nb),
                  pl.BlockSpec((FFN_KERNEL, cb), lambda j: (0, j)), pl.BlockSpec((1, cb), lambda j: (0, j))],
        out_specs=_col_spec(t, cb), out_shape=jax.ShapeDtypeStruct((t, f), BF16),
        scratch_shapes=[pltpu.VMEM((t + 8, cb), F32)],
        compiler_params=_cparams(("parallel",)),
    )(hf, hf, w, b)


def _ffn_act_bwd(name, dact, hf, w, b, cb=128):
    t = hf.shape[0]
    f = hf.shape[1] // 2
    nb = f // cb
    tm = CONV_ROWS

    def body(da_ref, g_ref, v_ref, w_ref, b_ref, dhf_ref, dw_ref, db_ref, pad_ref, dgc_ref):
        pad_ref[pl.ds(0, 8), :] = jnp.zeros((8, cb), F32)
        pad_ref[pl.ds(8, t), :] = g_ref[...].astype(F32)
        dgc_ref[pl.ds(t, 8), :] = jnp.zeros((8, cb), F32)
        wv = [w_ref[pl.ds(k, 1), :] for k in range(FFN_KERNEL)]
        bias = b_ref[...]

        def tile_a(i, carry):
            r0 = pl.multiple_of(i * tm, tm)
            taps = {}
            _for_shifted(pad_ref, r0, tm, (6, 7, 8), lambda s, rows: taps.__setitem__(s, rows))
            gc = bias + wv[0] * taps[6] + wv[1] * taps[7] + wv[2] * taps[8]
            sg = _sigmoid(gc)
            da = da_ref[pl.ds(r0, tm), :].astype(F32)
            dhf_ref[1, pl.ds(r0, tm), :] = (da * gc * sg).astype(BF16)
            dgc = da * v_ref[pl.ds(r0, tm), :].astype(F32) * sg * (1.0 + gc * (1.0 - sg))
            dgc_ref[pl.ds(r0, tm), :] = dgc
            sums = [jnp.sum(dgc * taps[6 + k], axis=0, keepdims=True) for k in range(3)]
            sums.append(jnp.sum(dgc, axis=0, keepdims=True))
            return tuple(c + s for c, s in zip(carry, sums))

        zero = jnp.zeros((1, cb), F32)
        dw0, dw1, dw2, dbias = _unrolled(t // tm, FFN_UNROLL, tile_a, (zero, zero, zero, zero))
        row = lax.broadcasted_iota(jnp.int32, (8, cb), 0)
        dw_ref[...] = jnp.where(row == 0, dw0, jnp.where(row == 1, dw1, jnp.where(row == 2, dw2, 0.0)))
        db_ref[...] = dbias

        def tile_b(i, carry):
            r0 = pl.multiple_of(i * tm, tm)
            acc = [jnp.zeros((tm, cb), F32)]

            def tap(s, rows):
                acc[0] = acc[0] + wv[2 - s] * rows

            _for_shifted(dgc_ref, r0, tm, (0, 1, 2), tap)
            dhf_ref[0, pl.ds(r0, tm), :] = acc[0].astype(BF16)
            return carry

        lax.fori_loop(0, t // tm, tile_b, 0)

    return pl.pallas_call(
        body, name=name, grid=(nb,),
        in_specs=[_col_spec(t, cb), _col_spec(t, cb), _col_spec(t, cb, nb),
                  pl.BlockSpec((FFN_KERNEL, cb), lambda j: (0, j)), pl.BlockSpec((1, cb), lambda j: (0, j))],
        out_specs=[pl.BlockSpec((2, t, cb), lambda j: (0, 0, j)),
                   pl.BlockSpec((8, cb), lambda j: (0, j)), pl.BlockSpec((1, cb), lambda j: (0, j))],
        out_shape=[jax.ShapeDtypeStruct((2, t, f), BF16), jax.ShapeDtypeStruct((8, f), F32),
                   jax.ShapeDtypeStruct((1, f), F32)],
        scratch_shapes=[pltpu.VMEM((t + 8, cb), F32), pltpu.VMEM((t + 8, cb), F32)],
        compiler_params=_cparams(("parallel",)),
    )(dact, hf, hf, w, b)


def _silu_grad(z, sg):
    return sg * (1.0 + z * (1.0 - sg))


def _conv_fwd(name, hin, w, b, ng, nb_, cat):
    t = hin.shape[0]
    c = GROUP
    tm = CONV_ROWS
    pad = 32
    shifts = tuple(2 + k for k in range(CONV_KERNEL))

    def body(a_ref, gt_ref, w_ref, b_ref, ng_ref, nb_ref, cat_ref, u1_ref, u3_ref, pad_ref):
        pad_ref[pl.ds(0, pad), :] = jnp.zeros((pad, c), F32)
        pad_ref[pl.ds(pad, t), :] = a_ref[...] * _sigmoid(gt_ref[...])
        bias, gam, bet = b_ref[...], ng_ref[...], nb_ref[...]

        def tile(i, carry):
            r0 = pl.multiple_of(i * tm, tm)
            acc = [jnp.broadcast_to(bias, (tm, c))]

            def tap(s, rows):
                acc[0] = acc[0] + w_ref[pl.ds(s - 2, 1), :] * rows

            _for_shifted(pad_ref, r0, tm, shifts, tap)
            u1 = acc[0]
            u1_ref[pl.ds(r0, tm), :] = u1
            xhat, _ = _ln_stats(u1)
            u2 = xhat * gam + bet
            u3_ref[pl.ds(r0, tm), :] = (u2 * _sigmoid(u2)).astype(BF16)
            return carry

        _unrolled(t // tm, CONV_UNROLL, tile, 0)

    vec = pl.BlockSpec((1, c), lambda j: (0, j))
    return pl.pallas_call(
        body, name=name, grid=(N_GROUPS,),
        in_specs=[_col_spec(t, c), _col_spec(t, c, N_GROUPS),
                  pl.BlockSpec((CONV_KERNEL, c), lambda j: (0, j)), vec, vec, vec, ANY],
        out_specs=[_col_spec(t, c), _col_spec(t, c)],
        out_shape=[jax.ShapeDtypeStruct((t, CONV_WIDTH), F32), jax.ShapeDtypeStruct(cat.shape, BF16)],
        input_output_aliases={6: 1},
        scratch_shapes=[pltpu.VMEM((t + pad, c), F32)],
        compiler_params=_cparams(("parallel",)),
    )(hin, hin, w, b, ng, nb_, cat)


def _conv_bwd(name, dcat, u1, hin, w, ng, nb_):
    t = hin.shape[0]
    c = GROUP
    tm = CONV_ROWS
    pad = 32
    nk = CONV_KERNEL

    def body(du3_ref, u1_ref, a_ref, gt_ref, w_ref, ng_ref, nb_ref,
             da_ref, dgt_ref, dw_ref, db_ref, dng_ref, dnb_ref, u0_ref, du1_ref, dwp_ref):
        u0_ref[pl.ds(0, pad), :] = jnp.zeros((pad, c), F32)
        u0_ref[pl.ds(pad, t), :] = a_ref[...] * _sigmoid(gt_ref[...])
        du1_ref[pl.ds(t, pad), :] = jnp.zeros((pad, c), F32)
        dwp_ref[...] = jnp.zeros_like(dwp_ref)
        gam, bet = ng_ref[...], nb_ref[...]

        def tile_a(i, carry):
            r0 = pl.multiple_of(i * tm, tm)
            u1 = u1_ref[pl.ds(r0, tm), :]
            xhat, rstd = _ln_stats(u1)
            u2 = xhat * gam + bet
            sg = _sigmoid(u2)
            du2 = du3_ref[pl.ds(r0, tm), :] * _silu_grad(u2, sg)
            dxhat = du2 * gam
            m1 = jnp.mean(dxhat, axis=-1, keepdims=True)
            m2 = jnp.mean(dxhat * xhat, axis=-1, keepdims=True)
            du1 = rstd * (dxhat - m1 - xhat * m2)
            du1_ref[pl.ds(r0, tm), :] = du1
            sums = (jnp.sum(du1, axis=0, keepdims=True), jnp.sum(du2 * xhat, axis=0, keepdims=True),
                    jnp.sum(du2, axis=0, keepdims=True))
            return tuple(x + s for x, s in zip(carry, sums))

        zero = jnp.zeros((1, c), F32)
        dbias, dgam, dbet = _unrolled(t // tm, CONV_UNROLL, tile_a, (zero, zero, zero))
        db_ref[...] = dbias
        dng_ref[...] = dgam
        dnb_ref[...] = dbet

        def tile_b(i, carry):
            r0 = pl.multiple_of(i * tm, tm)
            du1 = du1_ref[pl.ds(r0, tm), :]
            acc = [jnp.zeros((tm, c), F32)]

            def tap_dx(s, rows):
                acc[0] = acc[0] + w_ref[pl.ds(nk - 1 - s, 1), :] * rows

            _for_shifted(du1_ref, r0, tm, tuple(range(nk)), tap_dx)

            def tap_dw(s, rows):
                part = (du1 * rows).reshape(tm // 8, 8, c).sum(axis=0)
                dwp_ref[s - 2] = dwp_ref[s - 2] + part

            _for_shifted(u0_ref, r0, tm, tuple(2 + k for k in range(nk)), tap_dw)
            du0 = acc[0]
            a = a_ref[pl.ds(r0, tm), :]
            sg = _sigmoid(gt_ref[pl.ds(r0, tm), :])
            da_ref[pl.ds(r0, tm), :] = (du0 * sg).astype(BF16)
            dgt_ref[pl.ds(r0, tm), :] = (du0 * a * sg * (1.0 - sg)).astype(BF16)
            return carry

        lax.fori_loop(0, t // tm, tile_b, 0)
        dw_ref[...] = jnp.sum(dwp_ref[...], axis=1)

    vec = pl.BlockSpec((1, c), lambda j: (0, j))
    vshape = jax.ShapeDtypeStruct((1, CONV_WIDTH), F32)
    return pl.pallas_call(
        body, name=name, grid=(N_GROUPS,),
        in_specs=[_col_spec(t, c), _col_spec(t, c), _col_spec(t, c), _col_spec(t, c, N_GROUPS),
                  pl.BlockSpec((nk, c), lambda j: (0, j)), vec, vec],
        out_specs=[_col_spec(t, c), _col_spec(t, c), pl.BlockSpec((32, c), lambda j: (0, j)), vec, vec, vec],
        out_shape=[jax.ShapeDtypeStruct((t, CONV_WIDTH), BF16), jax.ShapeDtypeStruct((t, CONV_WIDTH), BF16),
                   jax.ShapeDtypeStruct((32, CONV_WIDTH), F32), vshape, vshape, vshape],
        scratch_shapes=[pltpu.VMEM((t + pad, c), F32), pltpu.VMEM((t + pad, c), F32),
                        pltpu.VMEM((32, 8, c), F32)],
        compiler_params=_cparams(("parallel",)),
    )(dcat, u1, hin, hin, w, ng, nb_)


LEVELS = (64, 32, 16)
HGRN_UNROLL = 4
HGRN_UNROLL_FWD = 4
NT_DIMS = (((1,), (1,)), ((), ()))
NN_DIMS = (((1,), (0,)), ((), ()))
TN_DIMS = (((0,), (0,)), ((), ()))


def _bdot(a, b, dims):
    return lax.dot_general(a.astype(BF16), b.astype(BF16), dims, preferred_element_type=F32)


def _hdot(a, b):
    return jnp.dot(a, b, precision=lax.Precision.HIGHEST, preferred_element_type=F32)


def _chunk_consts():
    rid = lax.broadcasted_iota(jnp.int32, (CHUNK, GROUP), 0)
    ti = lax.broadcasted_iota(jnp.int32, (CHUNK, CHUNK), 0)
    si = lax.broadcasted_iota(jnp.int32, (CHUNK, CHUNK), 1)
    tri = (si <= ti).astype(F32)
    second = [(rid & (b // 2)) != 0 for b in LEVELS]
    same = [None] + [(ti // b) == (si // b) for b in LEVELS[1:]]
    sub = lax.broadcasted_iota(jnp.int32, (SUB, GROUP), 0)
    return rid, tri, second, same, sub


def _level_refs(cum_ref, rid, base):
    row = lambda i: cum_ref[pl.ds(base + i, 1), :]
    l1 = jnp.broadcast_to(row(31), (CHUNK, GROUP))
    l2 = jnp.where(rid < 32, row(15), row(47))
    l3 = jnp.where(rid < 16, row(7), jnp.where(rid < 32, row(23), jnp.where(rid < 48, row(39), row(55))))
    return l1, l2, l3


def _level_factors(cum, brefs, second):
    out = []
    for bref, sec in zip(brefs, second):
        eq = jnp.where(sec, jnp.exp(jnp.minimum(cum - bref, 0.0)), 0.0)
        ek = jnp.where(sec, 0.0, jnp.exp(jnp.minimum(bref - cum, 0.0)))
        out.append((eq, ek))
    return out


def _gates(q, f, lb):
    sq = _sigmoid(q)
    sf = _sigmoid(f)
    fg = lb + (1.0 - lb) * sf
    return q * sq, sq, sf, fg


def _hgrn_specs(t, nc):
    c = GROUP
    col = lambda off: pl.BlockSpec((t, c), lambda h: (0, h + off))
    hin_specs = [col(16), col(24), col(32), col(40)]
    vec = pl.BlockSpec((1, c), lambda h: (0, h))
    lbs = pl.BlockSpec((2, c), lambda h: (0, h))
    st = pl.BlockSpec((1, nc, c, c), lambda h: (h, 0, 0, 0))
    return col, hin_specs, vec, lbs, st


def _hgrn_fwd(name, hin, lb_logits, hg):
    t = hin.shape[0]
    nc = t // CHUNK
    c = GROUP
    col, hin_specs, vec, lbs, st = _hgrn_specs(t, nc)

    def body(q_ref, f_ref, v_ref, og_ref, lb_ref, hg_ref, o_ref, ob_ref, st_ref,
             s_ref, cum_ref, kk_ref, vc_ref):
        rid, tri, second, same, sub = _chunk_consts()
        lb = _sigmoid(lb_ref[pl.ds(0, 1), :] - lb_ref[pl.ds(1, 1), :])
        gain = hg_ref[...]
        s_ref[...] = jnp.zeros_like(s_ref)

        def chunk(ci, u):
            base = u * CHUNK
            r0 = pl.multiple_of(ci * CHUNK, CHUNK)
            rows = pl.ds(r0, CHUNK)
            qh, _, _, fg = _gates(q_ref[rows, :], f_ref[rows, :], lb)
            v = v_ref[rows, :]
            kk = 1.0 - fg
            cum = _hdot(tri, jnp.log(fg))
            cum_ref[pl.ds(base, CHUNK), :] = cum
            kk_ref[pl.ds(base, CHUNK), :] = kk
            vc_ref[pl.ds(base, CHUNK), :] = v
            sprev = s_ref[...]
            st_ref[0, ci] = sprev
            blast = cum_ref[pl.ds(base + CHUNK - 1, 1), :]
            o = _bdot(qh * jnp.exp(cum), sprev, NT_DIMS)
            s_ref[...] = sprev * jnp.exp(blast) + _bdot(v, kk * jnp.exp(blast - cum), TN_DIMS)
            a = None
            for (eq, ek), msk in zip(_level_factors(cum, _level_refs(cum_ref, rid, base), second), same):
                al = _bdot(qh * eq, kk * ek, NT_DIMS)
                al = al if msk is None else jnp.where(msk, al, 0.0)
                a = al if a is None else a + al
            o = o + _bdot(a, v, NN_DIMS)
            diag = []
            for sb in range(CHUNK // SUB):
                lo = sb * SUB
                qb = qh[lo:lo + SUB]
                cb = cum[lo:lo + SUB]
                od = jnp.zeros((SUB, c), F32)
                for s in range(SUB):
                    e = jnp.where(sub >= s, jnp.exp(jnp.minimum(cb - cum_ref[pl.ds(base + lo + s, 1), :], 0.0)), 0.0)
                    acol = jnp.sum(qb * e * kk_ref[pl.ds(base + lo + s, 1), :], axis=-1, keepdims=True)
                    od = od + acol * vc_ref[pl.ds(base + lo + s, 1), :]
                diag.append(od)
            o = o + jnp.concatenate(diag, axis=0)
            o_ref[rows, :] = o
            y = o * lax.rsqrt(jnp.mean(o * o, axis=-1, keepdims=True) + RMS_EPS) * gain
            og = og_ref[rows, :]
            ob_ref[rows, :] = (y * og * _sigmoid(og)).astype(BF16)

        def chunks(i, carry):
            for u in range(HGRN_UNROLL_FWD):
                chunk(i * HGRN_UNROLL_FWD + u, u)
            return carry

        lax.fori_loop(0, nc // HGRN_UNROLL_FWD, chunks, 0)

    return pl.pallas_call(
        body, name=name, grid=(N_GROUPS,),
        in_specs=hin_specs + [lbs, vec],
        out_specs=[col(0), col(N_GROUPS), st],
        out_shape=[jax.ShapeDtypeStruct((t, HGRN_WIDTH), F32), jax.ShapeDtypeStruct((t, CONV_WIDTH + HGRN_WIDTH), BF16),
                   jax.ShapeDtypeStruct((N_GROUPS, nc, c, c), F32)],
        scratch_shapes=[pltpu.VMEM((c, c), F32)] + [pltpu.VMEM((HGRN_UNROLL_FWD * CHUNK, c), F32)] * 3,
        compiler_params=_cparams(("parallel",)),
    )(hin, hin, hin, hin, lb_logits, hg)


def _hgrn_bwd(name, dcat, hin, o_raw, states, lb_logits, hg):
    t = hin.shape[0]
    nc = t // CHUNK
    c = GROUP
    col, hin_specs, vec, lbs, st = _hgrn_specs(t, nc)

    def body(do_ref, q_ref, f_ref, v_ref, og_ref, o_ref, st_ref, lb_ref, hg_ref,
             dq_ref, df_ref, dv_ref, dog_ref, dhg_ref, dlb_ref,
             ds_ref, cum_ref, kk_ref, vc_ref):
        rid, tri, second, same, sub = _chunk_consts()
        trit = tri.T
        lb = _sigmoid(lb_ref[pl.ds(0, 1), :] - lb_ref[pl.ds(1, 1), :])
        gain = hg_ref[...]
        ds_ref[...] = jnp.zeros_like(ds_ref)

        def chunk(i, carry, u):
            base = u * CHUNK
            dhg, dlb = carry
            ci = nc - 1 - i
            r0 = pl.multiple_of(ci * CHUNK, CHUNK)
            rows = pl.ds(r0, CHUNK)
            q = q_ref[rows, :]
            qh, sq, sf, fg = _gates(q, f_ref[rows, :], lb)
            v = v_ref[rows, :]
            kk = 1.0 - fg
            cum = _hdot(tri, jnp.log(fg))
            cum_ref[pl.ds(base, CHUNK), :] = cum
            kk_ref[pl.ds(base, CHUNK), :] = kk
            vc_ref[pl.ds(base, CHUNK), :] = v
            o = o_ref[rows, :]
            og = og_ref[rows, :]
            sg = _sigmoid(og)
            rinv = lax.rsqrt(jnp.mean(o * o, axis=-1, keepdims=True) + RMS_EPS)
            yn = o * rinv
            dof = do_ref[rows, :]
            dog_ref[rows, :] = (dof * yn * gain * _silu_grad(og, sg)).astype(BF16)
            dz = dof * og * sg
            dhg = dhg + jnp.sum(dz * yn, axis=0, keepdims=True)
            dy = dz * gain
            do = rinv * (dy - yn * jnp.mean(dy * yn, axis=-1, keepdims=True))
            sprev = st_ref[0, ci]
            dsn = ds_ref[...]
            blast = cum_ref[pl.ds(base + CHUNK - 1, 1), :]
            eq0 = jnp.exp(cum)
            ek0 = jnp.exp(blast - cum)
            dqh = _bdot(do, sprev, NN_DIMS) * eq0
            dkk = _bdot(v, dsn, NN_DIMS) * ek0
            dlast = (jnp.sum(kk * dkk, axis=0, keepdims=True)
                     + jnp.exp(blast) * jnp.sum(dsn * sprev, axis=0, keepdims=True))
            dv = _bdot(kk * ek0, dsn, NT_DIMS)
            ds_ref[...] = dsn * jnp.exp(blast) + _bdot(do, qh * eq0, TN_DIMS)
            dg = qh * dqh - kk * dkk
            da = _bdot(do, v, NT_DIMS)
            a = None
            for (eq, ek), msk in zip(_level_factors(cum, _level_refs(cum_ref, rid, base), second), same):
                ql, kl = (qh * eq).astype(BF16), (kk * ek).astype(BF16)
                al = _bdot(ql, kl, NT_DIMS)
                dal = da
                if msk is not None:
                    al = jnp.where(msk, al, 0.0)
                    dal = jnp.where(msk, da, 0.0)
                a = al if a is None else a + al
                dql = _bdot(dal, kl, NN_DIMS)
                dkl = _bdot(dal, ql, TN_DIMS)
                dqh = dqh + dql * eq
                dkk = dkk + dkl * ek
                dg = dg + (ql.astype(F32) * dql - kl.astype(F32) * dkl)
            dv = dv + _bdot(a, do, TN_DIMS)
            dq_d, dk_d, dv_d = [], [], []
            for sb in range(CHUNK // SUB):
                lo = sb * SUB
                qb = qh[lo:lo + SUB]
                cb = cum[lo:lo + SUB]
                dob = do[lo:lo + SUB]
                dqb = jnp.zeros((SUB, c), F32)
                dkb = jnp.zeros((SUB, c), F32)
                dvb = jnp.zeros((SUB, c), F32)
                for s in range(SUB):
                    e = jnp.where(sub >= s, jnp.exp(jnp.minimum(cb - cum_ref[pl.ds(base + lo + s, 1), :], 0.0)), 0.0)
                    ks = kk_ref[pl.ds(base + lo + s, 1), :]
                    qe = qb * e
                    dacol = jnp.sum(dob * vc_ref[pl.ds(base + lo + s, 1), :], axis=-1, keepdims=True)
                    acol = jnp.sum(qe * ks, axis=-1, keepdims=True)
                    dqb = dqb + dacol * (ks * e)
                    dkb = jnp.where(sub == s, jnp.sum(dacol * qe, axis=0, keepdims=True), dkb)
                    dvb = jnp.where(sub == s, jnp.sum(acol * dob, axis=0, keepdims=True), dvb)
                dq_d.append(dqb)
                dk_d.append(dkb)
                dv_d.append(dvb)
            dq_d = jnp.concatenate(dq_d, axis=0)
            dk_d = jnp.concatenate(dk_d, axis=0)
            dqh = dqh + dq_d
            dkk = dkk + dk_d
            dg = dg + (qh * dq_d - kk * dk_d)
            dv = dv + jnp.concatenate(dv_d, axis=0)
            dlf = _hdot(trit, dg) + dlast
            dfg = dlf / fg - dkk
            df_ref[rows, :] = (dfg * (1.0 - lb) * sf * (1.0 - sf)).astype(BF16)
            dlb = dlb + jnp.sum(dfg * (1.0 - sf), axis=0, keepdims=True)
            dq_ref[rows, :] = (dqh * _silu_grad(q, sq)).astype(BF16)
            dv_ref[rows, :] = dv.astype(BF16)
            return dhg, dlb

        def chunks(i, carry):
            for u in range(HGRN_UNROLL):
                carry = chunk(i * HGRN_UNROLL + u, carry, u)
            return carry

        zero = jnp.zeros((1, c), F32)
        dhg, dlb = lax.fori_loop(0, nc // HGRN_UNROLL, chunks, (zero, zero))
        dhg_ref[...] = dhg
        dl0 = dlb * lb * (1.0 - lb)
        dlb_ref[...] = jnp.where(lax.broadcasted_iota(jnp.int32, (2, c), 0) == 0, dl0, -dl0)

    big = jax.ShapeDtypeStruct((t, HGRN_WIDTH), BF16)
    return pl.pallas_call(
        body, name=name, grid=(N_GROUPS,),
        in_specs=[col(8)] + hin_specs + [col(0), st, lbs, vec],
        out_specs=[col(0)] * 4 + [vec, lbs],
        out_shape=[big] * 4 + [jax.ShapeDtypeStruct((1, HGRN_WIDTH), F32), jax.ShapeDtypeStruct((2, HGRN_WIDTH), F32)],
        scratch_shapes=[pltpu.VMEM((c, c), F32)] + [pltpu.VMEM((HGRN_UNROLL * CHUNK, c), F32)] * 3,
        compiler_params=_cparams(("parallel",)),
    )(dcat, hin, hin, hin, hin, o_raw, states, lb_logits, hg)


ANY = pl.BlockSpec(memory_space=pl.ANY)


def _my_place():
    return lax.axis_index("x"), lax.axis_index("y"), lax.axis_index("c")


HBM = pl.BlockSpec(memory_space=pltpu.HBM)
SEM = pl.BlockSpec(memory_space=pltpu.SEMAPHORE)
EFFECT = pltpu.SideEffectType.DATAFLOW_SIDE_EFFECTING


def _peer(k):
    x, y, c = _my_place()
    px = 1 - x if k & 4 else x
    py = 1 - y if k & 2 else y
    pc = 1 - c if k & 1 else c
    return (px, py, pc), 4 * px + 2 * py + pc


def _slot(land_ref, idx):
    if len(land_ref.shape) == 2:
        ns = land_ref.shape[1] // N_DEV
        return land_ref.at[:, pl.ds(pl.multiple_of(idx * ns, 128), ns)]
    return land_ref.at[idx]


def _exchange_copy(k, src_ref, land_ref, send_sems, recv_sems, scatter, landing):
    x, y, c = _my_place()
    me = 4 * x + 2 * y + c
    to, idx = _peer(k)
    return pltpu.make_async_remote_copy(
        src_ref=_slot(src_ref, idx) if scatter else src_ref,
        dst_ref=_slot(land_ref, idx) if landing else _slot(land_ref, me),
        send_sem=send_sems.at[k - 1], recv_sem=recv_sems.at[k - 1], device_id=to, device_id_type=MESH)


ALL_PEERS = tuple(range(1, N_DEV))
NEAR_PEERS = (1, 2, 4, 6)
SAME_CORE_PEERS = (2, 4, 6)


def _exchange_start(name, src, land, scatter, ks=ALL_PEERS):
    def body(src_ref, land_ref, send_sems, recv_sems, src_thru, land_thru, token):
        for k in ks:
            _exchange_copy(k, src_ref, land_ref, send_sems, recv_sems, scatter, landing=False).start()
        token[...] = jnp.zeros_like(token)

    send_sems, recv_sems, src_thru, land_thru, token = pl.pallas_call(
        body, name=name,
        out_shape=(pltpu.SemaphoreType.DMA((N_DEV - 1,)), pltpu.SemaphoreType.DMA((N_DEV - 1,)),
                   pltpu.HBM(src.shape, src.dtype), pltpu.HBM(land.shape, land.dtype),
                   jax.ShapeDtypeStruct((8, 128), F32)),
        in_specs=(HBM, HBM), out_specs=(SEM, SEM, HBM, HBM, pl.BlockSpec(memory_space=pltpu.VMEM)),
        input_output_aliases={0: 2, 1: 3},
        compiler_params=pltpu.CompilerParams(has_side_effects=EFFECT),
    )(pltpu.with_memory_space_constraint(src, pltpu.HBM), pltpu.with_memory_space_constraint(land, pltpu.HBM))
    return (send_sems, recv_sems, src_thru, land_thru, scatter, ks), token


def _exchange_wait(name, handle, after):
    send_sems, recv_sems, src_thru, land_thru, scatter, ks = handle

    def body(src_ref, land_ref, send_sems, recv_sems, after_ref, src_dead, got_ref):
        for k in ks:
            cp = _exchange_copy(k, src_ref, land_ref, send_sems, recv_sems, scatter, landing=True)
            cp.wait_send()
            cp.wait_recv()

    return pl.pallas_call(
        body, name=name,
        out_shape=(pltpu.HBM(src_thru.shape, src_thru.dtype), pltpu.HBM(land_thru.shape, land_thru.dtype)),
        in_specs=(HBM, HBM, SEM, SEM, ANY), out_specs=(HBM, HBM), input_output_aliases={0: 0, 1: 1},
        compiler_params=pltpu.CompilerParams(has_side_effects=EFFECT),
    )(src_thru, land_thru, send_sems, recv_sems, after)[1]


def _relay_copy(j, land_ref, send_sems, recv_sems, landing):
    x, y, c = _my_place()
    k = SAME_CORE_PEERS[j]
    _, sent = _peer(k)
    _, got = _peer(k + 1)
    return pltpu.make_async_remote_copy(
        src_ref=_slot(land_ref, sent), dst_ref=_slot(land_ref, got) if landing else _slot(land_ref, sent),
        send_sem=send_sems.at[j], recv_sem=recv_sems.at[j], device_id=(x, y, 1 - c), device_id_type=MESH)


def _relay_start(name, land):
    n = len(SAME_CORE_PEERS)

    def body(land_ref, send_sems, recv_sems, land_thru, token):
        for j in range(n):
            _relay_copy(j, land_ref, send_sems, recv_sems, landing=False).start()
        token[...] = jnp.zeros_like(token)

    send_sems, recv_sems, land_thru, token = pl.pallas_call(
        body, name=name,
        out_shape=(pltpu.SemaphoreType.DMA((n,)), pltpu.SemaphoreType.DMA((n,)),
                   pltpu.HBM(land.shape, land.dtype), jax.ShapeDtypeStruct((8, 128), F32)),
        in_specs=(HBM,), out_specs=(SEM, SEM, HBM, pl.BlockSpec(memory_space=pltpu.VMEM)),
        input_output_aliases={0: 2},
        compiler_params=pltpu.CompilerParams(has_side_effects=EFFECT),
    )(pltpu.with_memory_space_constraint(land, pltpu.HBM))
    return (send_sems, recv_sems, land_thru), token


def _relay_wait(name, handle, after):
    send_sems, recv_sems, land_thru = handle

    def body(land_ref, send_sems, recv_sems, after_ref, got_ref):
        for j in range(len(SAME_CORE_PEERS)):
            cp = _relay_copy(j, land_ref, send_sems, recv_sems, landing=True)
            cp.wait_send()
            cp.wait_recv()

    return pl.pallas_call(
        body, name=name, out_shape=pltpu.HBM(land_thru.shape, land_thru.dtype),
        in_specs=(HBM, SEM, SEM, ANY), out_specs=HBM, input_output_aliases={0: 0},
        compiler_params=pltpu.CompilerParams(has_side_effects=EFFECT),
    )(land_thru, send_sems, recv_sems, after)


def _own_cols(name, own, me):
    r, ns = own.shape
    tr = 256

    def body(me_ref, own_ref, land_ref):
        land_ref[...] = own_ref[...]

    return pl.pallas_call(
        body, name=name,
        grid_spec=pltpu.PrefetchScalarGridSpec(
            num_scalar_prefetch=1, grid=(r // tr,),
            in_specs=[pl.BlockSpec((tr, ns), lambda i, me_ref: (i, 0))],
            out_specs=pl.BlockSpec((tr, ns), lambda i, me_ref: (i, me_ref[0]))),
        out_shape=jax.ShapeDtypeStruct((r, N_DEV * ns), own.dtype),
    )(jnp.reshape(me, (1,)).astype(jnp.int32), own)


def _own_slot(own, me):
    land = lax.empty((N_DEV,) + own.shape, own.dtype)
    return lax.dynamic_update_slice_in_dim(land, own[None], me, axis=0)


def _adamw_math(w, g, m, v):
    m = ADAM_B1 * m + (1.0 - ADAM_B1) * g
    v = ADAM_B2 * v + (1.0 - ADAM_B2) * (g * g)
    m_hat = m / (1.0 - ADAM_B1 ** ADAM_STEP)
    v_hat = v / (1.0 - ADAM_B2 ** ADAM_STEP)
    delta = -ADAM_LR * (m_hat / (jnp.sqrt(v_hat) + ADAM_EPS) + ADAM_WD * w)
    return delta, m, v


def _adamw_sum(name, recv, w, m, v, tr, row0=0, partial=None):
    r, c = w.shape
    rr = recv.shape[1]
    off = row0 // tr

    def body(recv_ref, w_ref, m_ref, v_ref, *refs):
        g_ref, d_ref, mo_ref, vo_ref = refs[-4:]
        g = recv_ref[0].astype(F32)
        for j in range(1, N_DEV):
            g = g + recv_ref[j].astype(F32)
        g_ref[...] = g
        d_ref[...], mo_ref[...], vo_ref[...] = _adamw_math(w_ref[...], g, m_ref[...], v_ref[...])

    tile = pl.BlockSpec((tr, c), lambda i: (i + off, 0))
    out = jax.ShapeDtypeStruct((r, c), F32)
    prev = list(partial) if partial is not None else []
    return pl.pallas_call(
        body, name=name, grid=(rr // tr,),
        in_specs=[pl.BlockSpec((N_DEV, tr, c), lambda i: (0, i, 0)), tile, tile, tile] + [ANY] * len(prev),
        out_specs=[tile] * 4, out_shape=[out] * 4,
        input_output_aliases={4 + i: i for i in range(len(prev))},
        compiler_params=_cparams(("parallel",)),
    )(recv, w, m, v, *prev)


def _sum_parts(name, parts):
    _, r, c = parts.shape

    def body(p_ref, o_ref):
        acc = p_ref[0]
        for j in range(1, N_DEV):
            acc = acc + p_ref[j]
        o_ref[...] = acc

    return pl.pallas_call(body, name=name, out_shape=jax.ShapeDtypeStruct((r, c), F32),
                          compiler_params=_cparams())(parts)


def _adamw_small(name, w, g, m, v):
    def body(w_ref, g_ref, m_ref, v_ref, d_ref, mo_ref, vo_ref):
        d_ref[...], mo_ref[...], vo_ref[...] = _adamw_math(w_ref[...], g_ref[...], m_ref[...], v_ref[...])

    out = jax.ShapeDtypeStruct(w.shape, F32)
    return pl.pallas_call(body, name=name, out_shape=[out] * 3, compiler_params=_cparams())(w, g, m, v)


def _pack(pieces, rows):
    flat = jnp.concatenate([p.reshape(-1).astype(F32) for p in pieces])
    return jnp.pad(flat, (0, rows * 128 - flat.shape[0])).reshape(rows, 128)


def _unpack(packed, shapes):
    flat = packed.reshape(-1)
    out, off = [], 0
    for s in shapes:
        n = 1
        for d in s:
            n *= d
        out.append(flat[off:off + n].reshape(s))
        off += n
    return out


def kernel(x, emb_ln_g, emb_ln_b, w_in, conv_w, conv_b, conv_norm_g, conv_norm_b, lb_logits, hgrn_norm_g, w_out, ln1_g, ln1_b, w_ffn_up, ffn_conv_w, ffn_conv_b, w_ffn_down, ln2_g, ln2_b, loss_target, m_emb_ln_g, m_emb_ln_b, m_w_in, m_conv_w, m_conv_b, m_conv_norm_g, m_conv_norm_b, m_lb_logits, m_hgrn_norm_g, m_w_out, m_ln1_g, m_ln1_b, m_w_ffn_up, m_ffn_conv_w, m_ffn_conv_b, m_w_ffn_down, m_ln2_g, m_ln2_b, v_emb_ln_g, v_emb_ln_b, v_w_in, v_conv_w, v_conv_b, v_conv_norm_g, v_conv_norm_b, v_lb_logits, v_hgrn_norm_g, v_w_out, v_ln1_g, v_ln1_b, v_w_ffn_up, v_ffn_conv_w, v_ffn_conv_b, v_w_ffn_down, v_ln2_g, v_ln2_b):
    t = x.shape[1]
    me = 4 * lax.axis_index("x") + 2 * lax.axis_index("y") + lax.axis_index("c")
    x2, tgt = x[0], loss_target[0]
    ns_in, ns_up = w_in.shape[2], w_ffn_up.shape[2]
    rs_out, rs_down = w_out.shape[1], w_ffn_down.shape[1]
    cs, fs = conv_w.shape[2], ffn_conv_w.shape[2]

    def gather_start(name, w, prev, ks=ALL_PEERS, cols=False):
        shard = (w[0] + prev).astype(BF16)
        land = _own_cols(name.replace("ag_", "own_"), shard, me) if cols else _own_slot(shard, me)
        return _exchange_start(name, shard, land, scatter=False, ks=ks)

    h_in, tok = gather_start("ag_w_in_start", w_in, 0.0, NEAR_PEERS, cols=True)
    taps = _pack([conv_w[0], ffn_conv_w[0]], 48) + tok[0, 0]
    h_taps, tok = _exchange_start("ag_taps_start", taps, _own_slot(taps, me), scatter=False)
    h_out, tok = gather_start("ag_w_out_start", w_out, tok[0, 0])
    h_up, tok = gather_start("ag_w_up_start", w_ffn_up, tok[0, 0], NEAR_PEERS, cols=True)
    h_down, tok = gather_start("ag_w_down_start", w_ffn_down, tok[0, 0])

    row = lambda a: a.reshape(1, -1)

    _, h0, h0b, h0bt = _ln_fwd("ln_in", x2, None, row(emb_ln_g) + tok[0, 0], row(emb_ln_b), 1.0)
    h_relay, tok_relay = _relay_start("ag_w_in_relay_start", _exchange_wait("ag_w_in_wait", h_in, h0b))
    win_n = _relay_wait("ag_w_in_relay_wait", h_relay, tok_relay)
    hin = _mm_nn("mm_in", h0b, win_n, F32, tm=1024, tn=ns_in, tk=D_MODEL)
    n_cw, n_fw = CONV_KERNEL * cs, FFN_KERNEL * fs
    taps_g = _exchange_wait("ag_taps_wait", h_taps, hin).reshape(N_DEV, -1)
    cw_full = taps_g[:, :n_cw].reshape(N_DEV, CONV_KERNEL, cs).transpose(1, 0, 2).reshape(CONV_KERNEL, CONV_WIDTH)
    fw_full = taps_g[:, n_cw:n_cw + n_fw].reshape(N_DEV, FFN_KERNEL, fs).transpose(1, 0, 2).reshape(FFN_KERNEL, D_FF)

    o_raw, cat_right, states = _hgrn_fwd("hgrn_fwd", hin, lb_logits, hgrn_norm_g)
    u1, catb = _conv_fwd("conv_fwd", hin, cw_full, conv_b, conv_norm_g, conv_norm_b, cat_right)
    wout_g = _exchange_wait("ag_w_out_wait", h_out, catb).reshape(D_MODEL, D_MODEL)
    h_up_relay, tok = _relay_start("ag_w_up_relay_start", _exchange_wait("ag_w_up_wait", h_up, wout_g))
    mix = _mm_nn("mm_out", catb, wout_g, F32, tm=1024, tn=1024, tk=D_MODEL, after=tok)
    r1, h1, h1b, h1bt = _ln_fwd("ln1", h0, mix, ln1_g, ln1_b, ALPHA)
    wup_n = _relay_wait("ag_w_up_relay_wait", h_up_relay, h1b)
    hf = _mm_nn("mm_up", h1b, wup_n, BF16, tm=1024, tn=1024, tk=D_MODEL)
    actb = _ffn_act_fwd("ffn_act", hf, fw_full, ffn_conv_b)
    wdown_g = _exchange_wait("ag_w_down_wait", h_down, actb).reshape(D_FF, D_MODEL)
    ffn = _mm_nn("mm_down", actb, wdown_g, F32, tm=512, tn=1024, tk=D_FF)
    dr2, dr2b, g_ln2g, g_ln2b, loss = _ln2_loss_bwd("ln2_loss", h1, ffn, ln2_g, ln2_b, tgt)

    def scatter_start(name, parts):
        if parts.ndim == 2:
            ns = parts.shape[1] // N_DEV
            own = lax.dynamic_slice_in_dim(parts, me * ns, ns, axis=1)
        else:
            own = lax.dynamic_index_in_dim(parts, me, axis=0, keepdims=False)
        return _exchange_start(name, parts, _own_slot(own, me), scatter=True)

    dact = _mm_nt("mm_dact", dr2b, wdown_g, BF16, tm=1024, tn=D_FF // 2, tk=D_MODEL)
    gw_down = _matmul(
        "mm_dw_down", actb, dr2b, (D_FF, D_MODEL), BF16, (N_DEV // 2, D_MODEL // 1024, 2),
        pl.BlockSpec((t // 2, 2 * rs_down), lambda i, j, kk: (kk, i)),
        pl.BlockSpec((t // 2, 1024), lambda i, j, kk: (kk, j)),
        pl.BlockSpec((2 * rs_down, 1024), lambda i, j, kk: (i, j)), nt="tn")
    s_down, tok = scatter_start("a2a_w_down_start", gw_down.reshape(N_DEV, rs_down, D_MODEL))
    dhf, g_fw, g_fb = _ffn_act_bwd("ffn_act_bwd", dact, hf, fw_full, ffn_conv_b + tok[0, 0])
    tm = min(1024, t)
    gw_up = _matmul(
        "mm_dw_up", h1bt, dhf, (D_MODEL, 2 * D_FF), BF16, (D_MODEL // 1024, 2 * D_FF // 512, 1),
        pl.BlockSpec((1024, t), lambda i, j, kk: (i, 0)),
        pl.BlockSpec((1, t, 512), lambda i, j, kk: (j // 11, 0, j % 11)),
        pl.BlockSpec((1024, 512), lambda i, j, kk: (i, j)), nt=False)
    s_up, tok = scatter_start("a2a_w_up_start", gw_up)
    tmh = min(512, t)
    dh1 = _matmul(
        "mm_dh1", dhf, wup_n, (t, D_MODEL), F32, (t // tmh, D_MODEL // 512, 1),
        pl.BlockSpec((2, tmh, D_FF), lambda i, j, kk: (0, i, 0)),
        pl.BlockSpec((512, 2 * D_FF), lambda i, j, kk: (j, 0)),
        pl.BlockSpec((tmh, 512), lambda i, j, kk: (i, j)), nt=True, after=tok)
    dr1, dr1b, g_ln1g, g_ln1b = _ln_bwd("ln1_bwd", r1, dr2, dh1, ln1_g + tok[0, 0], ALPHA, True)
    gw_out = _matmul(
        "mm_dw_out", catb, dr1b, (D_MODEL, D_MODEL), BF16, (2, 2, 2),
        pl.BlockSpec((t // 2, 1024), lambda i, j, kk: (kk, i)),
        pl.BlockSpec((t // 2, 1024), lambda i, j, kk: (kk, j)),
        pl.BlockSpec((1024, 1024), lambda i, j, kk: (i, j)), nt="tn")
    s_out, tok = scatter_start("a2a_w_out_start", gw_out.reshape(N_DEV, rs_out, D_MODEL))
    dcat = _mm_nt("mm_dcat", dr1b, wout_g, F32, tm=1024, tn=1024, tk=D_MODEL, after=tok)
    da, dgate, g_cw, g_cb, g_cng, g_cnb = _conv_bwd("conv_bwd", dcat, u1, hin, cw_full, conv_norm_g + tok[0, 0],
                                                    conv_norm_b)
    dq, df, di, dog, g_hg, g_lb = _hgrn_bwd("hgrn_bwd", dcat, hin, o_raw, states, lb_logits, hgrn_norm_g)
    dhin = jnp.concatenate([da, dgate, dq, df, di, dog], axis=1)
    half = D_MODEL // 2
    gw_in_a = _mm_grad_cols("mm_dw_in_a", h0bt, dhin, ns_in, 0, half, after=tok)
    s_in_a, tok = scatter_start("a2a_w_in_a_start", gw_in_a)
    gw_in_b = _mm_grad_cols("mm_dw_in_b", h0bt, dhin, ns_in, half, half, after=tok)
    s_in_b, tok = scatter_start("a2a_w_in_b_start", gw_in_b)
    dh0 = _mm_nt("mm_dh0", dhin, win_n, F32, tm=1024, tn=512, tk=IN_PROJ, after=tok)
    grad_x, g_eg, g_eb = _ln_bwd("ln_in_bwd", x2, dr1, dh0, row(emb_ln_g), ALPHA, False)

    small_shapes = [(D_MODEL,), (D_MODEL,), (CONV_KERNEL, CONV_WIDTH), (1, CONV_WIDTH), (1, CONV_WIDTH),
                    (1, CONV_WIDTH), (2, HGRN_WIDTH), (1, HGRN_WIDTH), (1, D_MODEL), (1, D_MODEL),
                    (FFN_KERNEL, D_FF), (1, D_FF), (1, D_MODEL), (1, D_MODEL), (128,)]
    rows_small = 569
    packed = _pack([g_eg, g_eb, g_cw[:CONV_KERNEL], g_cb, g_cng, g_cnb, g_lb, g_hg, g_ln1g, g_ln1b,
                    g_fw[:FFN_KERNEL], g_fb, g_ln2g, g_ln2b, loss], rows_small)
    h_small, tok = _exchange_start("ag_small_start", packed, _own_slot(packed, me), scatter=False)

    def big(name, handle, after, w, m, v, tr):
        recv = _exchange_wait("a2a_" + name + "_wait", handle, after)
        return [o[None] for o in _adamw_sum("adamw_" + name, recv, w[0], m[0], v[0], tr)]

    u_down = big("w_down", s_down, tok, w_ffn_down, m_w_ffn_down, v_w_ffn_down, 64)
    u_up = big("w_up", s_up, u_down[1], w_ffn_up, m_w_ffn_up, v_w_ffn_up, 64)
    u_out = big("w_out", s_out, u_up[1], w_out, m_w_out, v_w_out, 64)
    summed = _sum_parts("sum_small", _exchange_wait("ag_small_wait", h_small, u_out[1]))
    (s_eg, s_eb, s_cw, s_cb, s_cng, s_cnb, s_lb, s_hg, s_l1g, s_l1b, s_fw, s_fb, s_l2g, s_l2b,
     s_loss) = _unpack(summed, small_shapes)
    s_cw = lax.dynamic_slice_in_dim(s_cw, me * cs, cs, axis=1)[None]
    s_fw = lax.dynamic_slice_in_dim(s_fw, me * fs, fs, axis=1)[None]
    g_small = [s_eg, s_eb, s_cw, s_cb, s_cng, s_cnb, s_lb, s_hg, s_l1g, s_l1b, s_fw, s_fb, s_l2g, s_l2b]
    w_small = [emb_ln_g, emb_ln_b, conv_w, conv_b, conv_norm_g, conv_norm_b, lb_logits, hgrn_norm_g,
               ln1_g, ln1_b, ffn_conv_w, ffn_conv_b, ln2_g, ln2_b]
    m_small = [m_emb_ln_g, m_emb_ln_b, m_conv_w, m_conv_b, m_conv_norm_g, m_conv_norm_b, m_lb_logits,
               m_hgrn_norm_g, m_ln1_g, m_ln1_b, m_ffn_conv_w, m_ffn_conv_b, m_ln2_g, m_ln2_b]
    v_small = [v_emb_ln_g, v_emb_ln_b, v_conv_w, v_conv_b, v_conv_norm_g, v_conv_norm_b, v_lb_logits,
               v_hgrn_norm_g, v_ln1_g, v_ln1_b, v_ffn_conv_w, v_ffn_conv_b, v_ln2_g, v_ln2_b]
    rows_own = 236
    shapes_own = [w.shape for w in w_small]
    upd = _adamw_small("adamw_small", _pack(w_small, rows_own), _pack(g_small, rows_own),
                       _pack(m_small, rows_own), _pack(v_small, rows_own))
    d_small, nm_small, nv_small = (_unpack(u, shapes_own) for u in upd)
    g_small = [g.reshape(s) for g, s in zip(g_small, shapes_own)]

    recv_a = _exchange_wait("a2a_w_in_a_wait", s_in_a, upd[0])
    part = _adamw_sum("adamw_w_in_a", recv_a, w_in[0], m_w_in[0], v_w_in[0], 128)
    recv_b = _exchange_wait("a2a_w_in_b_wait", s_in_b, part[1])
    u_in = [o[None] for o in _adamw_sum("adamw_w_in_b", recv_b, w_in[0], m_w_in[0], v_w_in[0], 128,
                                        row0=half, partial=part)]

    def ordered(small, i_in, i_out, i_up, i_down):
        (eg, eb, cw, cb, cng, cnb, lb, hg, l1g, l1b, fw, fb, l2g, l2b) = small
        return [eg, eb, i_in, cw, cb, cng, cnb, lb, hg, i_out, l1g, l1b, i_up, fw, fb, i_down, l2g, l2b]

    outs = [s_loss[0], grad_x[None]]
    for k, small in enumerate([g_small, d_small, nm_small, nv_small]):
        outs += ordered(small, u_in[k], u_out[k], u_up[k], u_down[k])
    return tuple(outs)
```

```python
import functools

import jax
import jax.numpy as jnp
from jax import lax
from jax.experimental import pallas as pl
from jax.experimental.pallas import tpu as pltpu

F32 = jnp.float32
BF16 = jnp.bfloat16

N_DEV = 8
D_MODEL = 2048
CONV_WIDTH = 1024
CONV_KERNEL = 31
HGRN_WIDTH = 1024
GROUP = 128
N_GROUPS = 8
IN_PROJ = 2 * CONV_WIDTH + 4 * HGRN_WIDTH
D_FF = 5632
FFN_KERNEL = 3
CHUNK = 64
SUB = 8
LN_EPS = 1e-5
RMS_EPS = 1e-6
ALPHA = 2.0 ** 0.25
ADAM_LR, ADAM_B1, ADAM_B2, ADAM_EPS, ADAM_WD, ADAM_STEP = 0.001, 0.9, 0.999, 1e-08, 0.01, 10

VMEM_LIMIT = 56 * 1024 * 1024
MESH = pl.DeviceIdType.MESH


def _cparams(sem=None):
    return pltpu.CompilerParams(dimension_semantics=sem, vmem_limit_bytes=VMEM_LIMIT)


def _sigmoid(x):
    return 0.5 * jnp.tanh(0.5 * x) + 0.5


def _matmul(name, a, b, out_shape, out_dtype, grid, a_spec, b_spec, o_spec, nt, after=None):
    nk = grid[2]
    dims = {True: (((1,), (1,)), ((), ())), False: (((1,), (0,)), ((), ())), "tn": (((0,), (0,)), ((), ()))}[nt]
    extra = [] if after is None else [after]

    def body(a_ref, b_ref, *rest):
        o_ref, *scratch = rest[len(extra):]
        if len(a_ref.shape) == 3 and a_ref.shape[0] > 1:
            kp = a_ref.shape[2]
            part = None
            for p in range(a_ref.shape[0]):
                d = lax.dot_general(a_ref[p], b_ref[:, p * kp:(p + 1) * kp], dims, preferred_element_type=F32)
                part = d if part is None else part + d
        else:
            av = a_ref[0] if len(a_ref.shape) == 3 else a_ref[...]
            bv = b_ref[0] if len(b_ref.shape) == 3 else b_ref[...]
            part = lax.dot_general(av, bv, dims, preferred_element_type=F32)

        def write(res):
            if len(o_ref.shape) == 3:
                o_ref[0] = res.astype(out_dtype)
            else:
                o_ref[...] = res.astype(out_dtype)

        if nk == 1:
            write(part)
            return
        acc_ref, = scratch
        k = pl.program_id(2)

        @pl.when(k == 0)
        def _():
            acc_ref[...] = part

        @pl.when(jnp.logical_and(k > 0, k < nk - 1))
        def _():
            acc_ref[...] += part

        @pl.when(k == nk - 1)
        def _():
            write(acc_ref[...] + part)

    acc_shape = o_spec.block_shape[-2:]
    assert all(g >= 1 for g in grid), (name, grid)
    return pl.pallas_call(
        body, name=name, grid=grid, in_specs=[a_spec, b_spec] + [pl.BlockSpec(memory_space=pl.ANY)] * len(extra),
        out_specs=o_spec, out_shape=jax.ShapeDtypeStruct(out_shape, out_dtype),
        scratch_shapes=[pltpu.VMEM(acc_shape, F32)] if nk > 1 else [],
        compiler_params=_cparams(("parallel", "parallel", "arbitrary")),
    )(a, b, *extra)


def _mm_nn(name, a, w, out_dtype, tm, tn, tk, after=None):
    m, k = a.shape
    tm, tk = min(tm, m), min(tk, k)
    n = w.shape[1]
    return _matmul(
        name, a, w, (m, n), out_dtype, (m // tm, n // tn, k // tk),
        pl.BlockSpec((tm, tk), lambda i, j, kk: (i, kk)),
        pl.BlockSpec((tk, tn), lambda i, j, kk: (kk, j)),
        pl.BlockSpec((tm, tn), lambda i, j, kk: (i, j)), nt=False, after=after)


def _mm_nt(name, a, w, out_dtype, tm, tn, tk, after=None):
    m, k = a.shape
    tm = min(tm, m)
    n = w.shape[0]
    return _matmul(
        name, a, w, (m, n), out_dtype, (m // tm, n // tn, k // tk),
        pl.BlockSpec((tm, tk), lambda i, j, kk: (i, kk)),
        pl.BlockSpec((tn, tk), lambda i, j, kk: (j, kk)),
        pl.BlockSpec((tm, tn), lambda i, j, kk: (i, j)), nt=True, after=after)


def _mm_grad_cols(name, at, b, ns, row0, rows, after, tm=1024, tk=4096):
    t = at.shape[1]
    tk = min(tk, t)
    off = row0 // tm
    return _matmul(
        name, at, b, (N_DEV, rows, ns), BF16, (rows // tm, N_DEV, t // tk),
        pl.BlockSpec((tm, tk), lambda i, j, kk: (i + off, kk)),
        pl.BlockSpec((tk, ns), lambda i, j, kk: (kk, j)),
        pl.BlockSpec((1, tm, ns), lambda i, j, kk: (j, i, 0)), nt=False, after=after)


LN_ROWS = 256


def _ln_stats(r):
    mu = jnp.mean(r, axis=-1, keepdims=True)
    xc = r - mu
    var = jnp.mean(xc * xc, axis=-1, keepdims=True)
    rstd = lax.rsqrt(var + LN_EPS)
    return xc * rstd, rstd


def _row_spec(d):
    return pl.BlockSpec((LN_ROWS, d), lambda i: (i, 0))


def _vec_spec(d):
    return pl.BlockSpec((1, d), lambda i: (0, 0))


def _ln_fwd(name, a, m, g, b, alpha):
    t, d = a.shape
    has_m = m is not None

    def body(*refs):
        if has_m:
            a_ref, m_ref, g_ref, b_ref, r_ref, y_ref, yb_ref, yt_ref = refs
            r = alpha * a_ref[...] + m_ref[...]
            r_ref[...] = r
        else:
            a_ref, g_ref, b_ref, y_ref, yb_ref, yt_ref = refs
            r = a_ref[...]
        xhat, _ = _ln_stats(r)
        y = xhat * g_ref[...] + b_ref[...]
        y_ref[...] = y
        yb_ref[...] = y.astype(BF16)
        yt_ref[...] = y.T.astype(BF16)

    ins = [a] + ([m] if has_m else []) + [g, b]
    in_specs = [_row_spec(d)] * (2 if has_m else 1) + [_vec_spec(d)] * 2
    outs = ([jax.ShapeDtypeStruct((t, d), F32)] if has_m else []) + [
        jax.ShapeDtypeStruct((t, d), F32), jax.ShapeDtypeStruct((t, d), BF16), jax.ShapeDtypeStruct((d, t), BF16)]
    res = pl.pallas_call(
        body, name=name, grid=(t // LN_ROWS,), in_specs=in_specs,
        out_specs=[_row_spec(d)] * (len(outs) - 1) + [pl.BlockSpec((d, LN_ROWS), lambda i: (0, i))], out_shape=outs,
        compiler_params=_cparams(("parallel",)),
    )(*ins)
    return res if has_m else (None, *res)


def _ln_bwd_math(r, dy, g):
    xhat, rstd = _ln_stats(r)
    dxhat = dy * g
    m1 = jnp.mean(dxhat, axis=-1, keepdims=True)
    m2 = jnp.mean(dxhat * xhat, axis=-1, keepdims=True)
    dr = rstd * (dxhat - m1 - xhat * m2)
    return dr, jnp.sum(dy * xhat, axis=0, keepdims=True), jnp.sum(dy, axis=0, keepdims=True)


def _ln2_loss_bwd(name, h1, ffn, g, b, tgt):
    t, d = h1.shape

    def body(h1_ref, f_ref, g_ref, b_ref, t_ref, dr_ref, drb_ref, dg_ref, db_ref, loss_ref):
        @pl.when(pl.program_id(0) == 0)
        def _():
            dg_ref[...] = jnp.zeros_like(dg_ref)
            db_ref[...] = jnp.zeros_like(db_ref)
            loss_ref[...] = jnp.zeros_like(loss_ref)

        r = ALPHA * h1_ref[...] + f_ref[...]
        xhat, _ = _ln_stats(r)
        e = xhat * g_ref[...] + b_ref[...] - t_ref[...]
        loss_ref[...] += 0.5 / d * jnp.sum(e * e)
        dr, dg, db = _ln_bwd_math(r, e * (1.0 / d), g_ref[...])
        dr_ref[...] = dr
        drb_ref[...] = dr.astype(BF16)
        dg_ref[...] += dg
        db_ref[...] += db

    return pl.pallas_call(
        body, name=name, grid=(t // LN_ROWS,),
        in_specs=[_row_spec(d), _row_spec(d), _vec_spec(d), _vec_spec(d), _row_spec(d)],
        out_specs=[_row_spec(d), _row_spec(d), _vec_spec(d), _vec_spec(d), _vec_spec(128)],
        out_shape=[jax.ShapeDtypeStruct((t, d), F32), jax.ShapeDtypeStruct((t, d), BF16),
                   jax.ShapeDtypeStruct((1, d), F32), jax.ShapeDtypeStruct((1, d), F32),
                   jax.ShapeDtypeStruct((1, 128), F32)],
        compiler_params=_cparams(("arbitrary",)),
    )(h1, ffn, g, b, tgt)


def _ln_bwd(name, r, dya, dyb, g, alpha, want_bf16):
    t, d = r.shape

    def body(r_ref, dya_ref, dyb_ref, g_ref, *outs):
        dr_ref = outs[0]
        dg_ref, db_ref = outs[-2:]

        @pl.when(pl.program_id(0) == 0)
        def _():
            dg_ref[...] = jnp.zeros_like(dg_ref)
            db_ref[...] = jnp.zeros_like(db_ref)

        dy = alpha * dya_ref[...] + dyb_ref[...]
        dr, dg, db = _ln_bwd_math(r_ref[...], dy, g_ref[...])
        dr_ref[...] = dr
        if want_bf16:
            outs[1][...] = dr.astype(BF16)
        dg_ref[...] += dg
        db_ref[...] += db

    big = [jax.ShapeDtypeStruct((t, d), F32)] + ([jax.ShapeDtypeStruct((t, d), BF16)] if want_bf16 else [])
    return pl.pallas_call(
        body, name=name, grid=(t // LN_ROWS,),
        in_specs=[_row_spec(d)] * 3 + [_vec_spec(d)],
        out_specs=[_row_spec(d)] * len(big) + [_vec_spec(d)] * 2,
        out_shape=big + [jax.ShapeDtypeStruct((1, d), F32)] * 2,
        compiler_params=_cparams(("arbitrary",)),
    )(r, dya, dyb, g)


CONV_ROWS = 64
CONV_UNROLL = 4
FFN_UNROLL = 2


def _unrolled(n, unroll, fn, init):
    def body(i, carry):
        for u in range(unroll):
            carry = fn(i * unroll + u, carry)
        return carry

    return lax.fori_loop(0, n // unroll, body, init)


def _for_shifted(ref, r0, tm, shifts, fn):
    for s in shifts:
        fn(s, ref[pl.ds(r0 + s, tm), :])


def _col_spec(t, cb, off=0):
    return pl.BlockSpec((t, cb), lambda j: (0, j + off))


def _ffn_act_fwd(name, hf, w, b, cb=128):
    t = hf.shape[0]
    f = hf.shape[1] // 2
    nb = f // cb
    tm = CONV_ROWS

    def body(g_ref, v_ref, w_ref, b_ref, act_ref, pad_ref):
        pad_ref[pl.ds(0, 8), :] = jnp.zeros((8, cb), F32)
        pad_ref[pl.ds(8, t), :] = g_ref[...].astype(F32)
        wv = [w_ref[pl.ds(k, 1), :] for k in range(FFN_KERNEL)]
        bias = b_ref[...]

        def tile(i, carry):
            r0 = pl.multiple_of(i * tm, tm)
            acc = [jnp.broadcast_to(bias, (tm, cb))]

            def tap(s, rows):
                acc[0] = acc[0] + wv[s - 6] * rows

            _for_shifted(pad_ref, r0, tm, (6, 7, 8), tap)
            gc = acc[0]
            act_ref[pl.ds(r0, tm), :] = (gc * _sigmoid(gc) * v_ref[pl.ds(r0, tm), :].astype(F32)).astype(BF16)
            return carry

        _unrolled(t // tm, FFN_UNROLL, tile, 0)

    return pl.pallas_call(
        body, name=name, grid=(nb,),
        in_specs=[_col_spec(t, cb), _col_spec(t, cb, nb),
                  pl.BlockSpec((FFN_KERNEL, cb), lambda j: (0, j)), pl.BlockSpec((1, cb), lambda j: (0, j))],
        out_specs=_col_spec(t, cb), out_shape=jax.ShapeDtypeStruct((t, f), BF16),
        scratch_shapes=[pltpu.VMEM((t + 8, cb), F32)],
        compiler_params=_cparams(("parallel",)),
    )(hf, hf, w, b)


def _ffn_act_bwd(name, dact, hf, w, b, cb=128):
    t = hf.shape[0]
    f = hf.shape[1] // 2
    nb = f // cb
    tm = CONV_ROWS

    def body(da_ref, g_ref, v_ref, w_ref, b_ref, dhf_ref, dw_ref, db_ref, pad_ref, dgc_ref):
        pad_ref[pl.ds(0, 8), :] = jnp.zeros((8, cb), F32)
        pad_ref[pl.ds(8, t), :] = g_ref[...].astype(F32)
        dgc_ref[pl.ds(t, 8), :] = jnp.zeros((8, cb), F32)
        wv = [w_ref[pl.ds(k, 1), :] for k in range(FFN_KERNEL)]
        bias = b_ref[...]

        def tile_a(i, carry):
            r0 = pl.multiple_of(i * tm, tm)
            taps = {}
            _for_shifted(pad_ref, r0, tm, (6, 7, 8), lambda s, rows: taps.__setitem__(s, rows))
            gc = bias + wv[0] * taps[6] + wv[1] * taps[7] + wv[2] * taps[8]
            sg = _sigmoid(gc)
            da = da_ref[pl.ds(r0, tm), :].astype(F32)
            dhf_ref[1, pl.ds(r0, tm), :] = (da * gc * sg).astype(BF16)
            dgc = da * v_ref[pl.ds(r0, tm), :].astype(F32) * sg * (1.0 + gc * (1.0 - sg))
            dgc_ref[pl.ds(r0, tm), :] = dgc
            sums = [jnp.sum(dgc * taps[6 + k], axis=0, keepdims=True) for k in range(3)]
            sums.append(jnp.sum(dgc, axis=0, keepdims=True))
            return tuple(c + s for c, s in zip(carry, sums))

        zero = jnp.zeros((1, cb), F32)
        dw0, dw1, dw2, dbias = _unrolled(t // tm, FFN_UNROLL, tile_a, (zero, zero, zero, zero))
        row = lax.broadcasted_iota(jnp.int32, (8, cb), 0)
        dw_ref[...] = jnp.where(row == 0, dw0, jnp.where(row == 1, dw1, jnp.where(row == 2, dw2, 0.0)))
        db_ref[...] = dbias

        def tile_b(i, carry):
            r0 = pl.multiple_of(i * tm, tm)
            acc = [jnp.zeros((tm, cb), F32)]

            def tap(s, rows):
                acc[0] = acc[0] + wv[2 - s] * rows

            _for_shifted(dgc_ref, r0, tm, (0, 1, 2), tap)
            dhf_ref[0, pl.ds(r0, tm), :] = acc[0].astype(BF16)
            return carry

        lax.fori_loop(0, t // tm, tile_b, 0)

    return pl.pallas_call(
        body, name=name, grid=(nb,),
        in_specs=[_col_spec(t, cb), _col_spec(t, cb), _col_spec(t, cb, nb),
                  pl.BlockSpec((FFN_KERNEL, cb), lambda j: (0, j)), pl.BlockSpec((1, cb), lambda j: (0, j))],
        out_specs=[pl.BlockSpec((2, t, cb), lambda j: (0, 0, j)),
                   pl.BlockSpec((8, cb), lambda j: (0, j)), pl.BlockSpec((1, cb), lambda j: (0, j))],
        out_shape=[jax.ShapeDtypeStruct((2, t, f), BF16), jax.ShapeDtypeStruct((8, f), F32),
                   jax.ShapeDtypeStruct((1, f), F32)],
        scratch_shapes=[pltpu.VMEM((t + 8, cb), F32), pltpu.VMEM((t + 8, cb), F32)],
        compiler_params=_cparams(("parallel",)),
    )(dact, hf, hf, w, b)


def _silu_grad(z, sg):
    return sg * (1.0 + z * (1.0 - sg))


def _conv_fwd(name, hin, w, b, ng, nb_, cat):
    t = hin.shape[0]
    c = GROUP
    tm = CONV_ROWS
    pad = 32
    shifts = tuple(2 + k for k in range(CONV_KERNEL))

    def body(a_ref, gt_ref, w_ref, b_ref, ng_ref, nb_ref, cat_ref, u1_ref, u3_ref, pad_ref):
        pad_ref[pl.ds(0, pad), :] = jnp.zeros((pad, c), F32)
        pad_ref[pl.ds(pad, t), :] = a_ref[...] * _sigmoid(gt_ref[...])
        bias, gam, bet = b_ref[...], ng_ref[...], nb_ref[...]

        def tile(i, carry):
            r0 = pl.multiple_of(i * tm, tm)
            acc = [jnp.broadcast_to(bias, (tm, c))]

            def tap(s, rows):
                acc[0] = acc[0] + w_ref[pl.ds(s - 2, 1), :] * rows

            _for_shifted(pad_ref, r0, tm, shifts, tap)
            u1 = acc[0]
            u1_ref[pl.ds(r0, tm), :] = u1
            xhat, _ = _ln_stats(u1)
            u2 = xhat * gam + bet
            u3_ref[pl.ds(r0, tm), :] = (u2 * _sigmoid(u2)).astype(BF16)
            return carry

        _unrolled(t // tm, CONV_UNROLL, tile, 0)

    vec = pl.BlockSpec((1, c), lambda j: (0, j))
    return pl.pallas_call(
        body, name=name, grid=(N_GROUPS,),
        in_specs=[_col_spec(t, c), _col_spec(t, c, N_GROUPS),
                  pl.BlockSpec((CONV_KERNEL, c), lambda j: (0, j)), vec, vec, vec, ANY],
        out_specs=[_col_spec(t, c), _col_spec(t, c)],
        out_shape=[jax.ShapeDtypeStruct((t, CONV_WIDTH), F32), jax.ShapeDtypeStruct(cat.shape, BF16)],
        input_output_aliases={6: 1},
        scratch_shapes=[pltpu.VMEM((t + pad, c), F32)],
        compiler_params=_cparams(("parallel",)),
    )(hin, hin, w, b, ng, nb_, cat)


def _conv_bwd(name, dcat, u1, hin, w, ng, nb_):
    t = hin.shape[0]
    c = GROUP
    tm = CONV_ROWS
    pad = 32
    nk = CONV_KERNEL

    def body(du3_ref, u1_ref, a_ref, gt_ref, w_ref, ng_ref, nb_ref,
             da_ref, dgt_ref, dw_ref, db_ref, dng_ref, dnb_ref, u0_ref, du1_ref, dwp_ref):
        u0_ref[pl.ds(0, pad), :] = jnp.zeros((pad, c), F32)
        u0_ref[pl.ds(pad, t), :] = a_ref[...] * _sigmoid(gt_ref[...])
        du1_ref[pl.ds(t, pad), :] = jnp.zeros((pad, c), F32)
        dwp_ref[...] = jnp.zeros_like(dwp_ref)
        gam, bet = ng_ref[...], nb_ref[...]

        def tile_a(i, carry):
            r0 = pl.multiple_of(i * tm, tm)
            u1 = u1_ref[pl.ds(r0, tm), :]
            xhat, rstd = _ln_stats(u1)
            u2 = xhat * gam + bet
            sg = _sigmoid(u2)
            du2 = du3_ref[pl.ds(r0, tm), :] * _silu_grad(u2, sg)
            dxhat = du2 * gam
            m1 = jnp.mean(dxhat, axis=-1, keepdims=True)
            m2 = jnp.mean(dxhat * xhat, axis=-1, keepdims=True)
            du1 = rstd * (dxhat - m1 - xhat * m2)
            du1_ref[pl.ds(r0, tm), :] = du1
            sums = (jnp.sum(du1, axis=0, keepdims=True), jnp.sum(du2 * xhat, axis=0, keepdims=True),
                    jnp.sum(du2, axis=0, keepdims=True))
            return tuple(x + s for x, s in zip(carry, sums))

        zero = jnp.zeros((1, c), F32)
        dbias, dgam, dbet = _unrolled(t // tm, CONV_UNROLL, tile_a, (zero, zero, zero))
        db_ref[...] = dbias
        dng_ref[...] = dgam
        dnb_ref[...] = dbet

        def tile_b(i, carry):
            r0 = pl.multiple_of(i * tm, tm)
            du1 = du1_ref[pl.ds(r0, tm), :]
            acc = [jnp.zeros((tm, c), F32)]

            def tap_dx(s, rows):
                acc[0] = acc[0] + w_ref[pl.ds(nk - 1 - s, 1), :] * rows

            _for_shifted(du1_ref, r0, tm, tuple(range(nk)), tap_dx)

            def tap_dw(s, rows):
                part = (du1 * rows).reshape(tm // 8, 8, c).sum(axis=0)
                dwp_ref[s - 2] = dwp_ref[s - 2] + part

            _for_shifted(u0_ref, r0, tm, tuple(2 + k for k in range(nk)), tap_dw)
            du0 = acc[0]
            a = a_ref[pl.ds(r0, tm), :]
            sg = _sigmoid(gt_ref[pl.ds(r0, tm), :])
            da_ref[pl.ds(r0, tm), :] = (du0 * sg).astype(BF16)
            dgt_ref[pl.ds(r0, tm), :] = (du0 * a * sg * (1.0 - sg)).astype(BF16)
            return carry

        lax.fori_loop(0, t // tm, tile_b, 0)
        dw_ref[...] = jnp.sum(dwp_ref[...], axis=1)

    vec = pl.BlockSpec((1, c), lambda j: (0, j))
    vshape = jax.ShapeDtypeStruct((1, CONV_WIDTH), F32)
    return pl.pallas_call(
        body, name=name, grid=(N_GROUPS,),
        in_specs=[_col_spec(t, c), _col_spec(t, c), _col_spec(t, c), _col_spec(t, c, N_GROUPS),
                  pl.BlockSpec((nk, c), lambda j: (0, j)), vec, vec],
        out_specs=[_col_spec(t, c), _col_spec(t, c), pl.BlockSpec((32, c), lambda j: (0, j)), vec, vec, vec],
        out_shape=[jax.ShapeDtypeStruct((t, CONV_WIDTH), BF16), jax.ShapeDtypeStruct((t, CONV_WIDTH), BF16),
                   jax.ShapeDtypeStruct((32, CONV_WIDTH), F32), vshape, vshape, vshape],
        scratch_shapes=[pltpu.VMEM((t + pad, c), F32), pltpu.VMEM((t + pad, c), F32),
                        pltpu.VMEM((32, 8, c), F32)],
        compiler_params=_cparams(("parallel",)),
    )(dcat, u1, hin, hin, w, ng, nb_)


LEVELS = (64, 32, 16)
HGRN_UNROLL = 4
HGRN_UNROLL_FWD = 8
NT_DIMS = (((1,), (1,)), ((), ()))
NN_DIMS = (((1,), (0,)), ((), ()))
TN_DIMS = (((0,), (0,)), ((), ()))


def _bdot(a, b, dims):
    return lax.dot_general(a.astype(BF16), b.astype(BF16), dims, preferred_element_type=F32)


def _hdot(a, b):
    return jnp.dot(a, b, precision=lax.Precision.HIGHEST, preferred_element_type=F32)


def _chunk_consts():
    rid = lax.broadcasted_iota(jnp.int32, (CHUNK, GROUP), 0)
    ti = lax.broadcasted_iota(jnp.int32, (CHUNK, CHUNK), 0)
    si = lax.broadcasted_iota(jnp.int32, (CHUNK, CHUNK), 1)
    tri = (si <= ti).astype(F32)
    second = [(rid & (b // 2)) != 0 for b in LEVELS]
    same = [None] + [(ti // b) == (si // b) for b in LEVELS[1:]]
    sub = lax.broadcasted_iota(jnp.int32, (SUB, GROUP), 0)
    return rid, tri, second, same, sub


def _level_refs(cum_ref, rid, base):
    row = lambda i: cum_ref[pl.ds(base + i, 1), :]
    l1 = jnp.broadcast_to(row(31), (CHUNK, GROUP))
    l2 = jnp.where(rid < 32, row(15), row(47))
    l3 = jnp.where(rid < 16, row(7), jnp.where(rid < 32, row(23), jnp.where(rid < 48, row(39), row(55))))
    return l1, l2, l3


def _level_factors(cum, brefs, second):
    out = []
    for bref, sec in zip(brefs, second):
        eq = jnp.where(sec, jnp.exp(jnp.minimum(cum - bref, 0.0)), 0.0)
        ek = jnp.where(sec, 0.0, jnp.exp(jnp.minimum(bref - cum, 0.0)))
        out.append((eq, ek))
    return out


def _gates(q, f, lb):
    sq = _sigmoid(q)
    sf = _sigmoid(f)
    fg = lb + (1.0 - lb) * sf
    return q * sq, sq, sf, fg


def _hgrn_specs(t, nc):
    c = GROUP
    col = lambda off: pl.BlockSpec((t, c), lambda h: (0, h + off))
    hin_specs = [col(16), col(24), col(32), col(40)]
    vec = pl.BlockSpec((1, c), lambda h: (0, h))
    lbs = pl.BlockSpec((2, c), lambda h: (0, h))
    st = pl.BlockSpec((1, nc, c, c), lambda h: (h, 0, 0, 0))
    return col, hin_specs, vec, lbs, st


def _hgrn_fwd(name, hin, lb_logits, hg):
    t = hin.shape[0]
    nc = t // CHUNK
    c = GROUP
    col, hin_specs, vec, lbs, st = _hgrn_specs(t, nc)

    def body(q_ref, f_ref, v_ref, og_ref, lb_ref, hg_ref, o_ref, ob_ref, st_ref,
             s_ref, cum_ref, kk_ref, vc_ref):
        rid, tri, second, same, sub = _chunk_consts()
        lb = _sigmoid(lb_ref[pl.ds(0, 1), :] - lb_ref[pl.ds(1, 1), :])
        gain = hg_ref[...]
        s_ref[...] = jnp.zeros_like(s_ref)

        def chunk(ci, u):
            base = u * CHUNK
            r0 = pl.multiple_of(ci * CHUNK, CHUNK)
            rows = pl.ds(r0, CHUNK)
            qh, _, _, fg = _gates(q_ref[rows, :], f_ref[rows, :], lb)
            v = v_ref[rows, :]
            kk = 1.0 - fg
            cum = _hdot(tri, jnp.log(fg))
            cum_ref[pl.ds(base, CHUNK), :] = cum
            kk_ref[pl.ds(base, CHUNK), :] = kk
            vc_ref[pl.ds(base, CHUNK), :] = v
            sprev = s_ref[...]
            st_ref[0, ci] = sprev
            blast = cum_ref[pl.ds(base + CHUNK - 1, 1), :]
            o = _bdot(qh * jnp.exp(cum), sprev, NT_DIMS)
            s_ref[...] = sprev * jnp.exp(blast) + _bdot(v, kk * jnp.exp(blast - cum), TN_DIMS)
            a = None
            for (eq, ek), msk in zip(_level_factors(cum, _level_refs(cum_ref, rid, base), second), same):
                al = _bdot(qh * eq, kk * ek, NT_DIMS)
                al = al if msk is None else jnp.where(msk, al, 0.0)
                a = al if a is None else a + al
            o = o + _bdot(a, v, NN_DIMS)
            diag = []
            for sb in range(CHUNK // SUB):
                lo = sb * SUB
                qb = qh[lo:lo + SUB]
                cb = cum[lo:lo + SUB]
                od = jnp.zeros((SUB, c), F32)
                for s in range(SUB):
                    e = jnp.where(sub >= s, jnp.exp(jnp.minimum(cb - cum_ref[pl.ds(base + lo + s, 1), :], 0.0)), 0.0)
                    acol = jnp.sum(qb * e * kk_ref[pl.ds(base + lo + s, 1), :], axis=-1, keepdims=True)
                    od = od + acol * vc_ref[pl.ds(base + lo + s, 1), :]
                diag.append(od)
            o = o + jnp.concatenate(diag, axis=0)
            o_ref[rows, :] = o
            y = o * lax.rsqrt(jnp.mean(o * o, axis=-1, keepdims=True) + RMS_EPS) * gain
            og = og_ref[rows, :]
            ob_ref[rows, :] = (y * og * _sigmoid(og)).astype(BF16)

        def chunks(i, carry):
            for u in range(HGRN_UNROLL_FWD):
                chunk(i * HGRN_UNROLL_FWD + u, u)
            return carry

        lax.fori_loop(0, nc // HGRN_UNROLL_FWD, chunks, 0)

    return pl.pallas_call(
        body, name=name, grid=(N_GROUPS,),
        in_specs=hin_specs + [lbs, vec],
        out_specs=[col(0), col(N_GROUPS), st],
        out_shape=[jax.ShapeDtypeStruct((t, HGRN_WIDTH), F32), jax.ShapeDtypeStruct((t, CONV_WIDTH + HGRN_WIDTH), BF16),
                   jax.ShapeDtypeStruct((N_GROUPS, nc, c, c), F32)],
        scratch_shapes=[pltpu.VMEM((c, c), F32)] + [pltpu.VMEM((HGRN_UNROLL_FWD * CHUNK, c), F32)] * 3,
        compiler_params=_cparams(("parallel",)),
    )(hin, hin, hin, hin, lb_logits, hg)


def _hgrn_bwd(name, dcat, hin, o_raw, states, lb_logits, hg):
    t = hin.shape[0]
    nc = t // CHUNK
    c = GROUP
    col, hin_specs, vec, lbs, st = _hgrn_specs(t, nc)

    def body(do_ref, q_ref, f_ref, v_ref, og_ref, o_ref, st_ref, lb_ref, hg_ref,
             dq_ref, df_ref, dv_ref, dog_ref, dhg_ref, dlb_ref,
             ds_ref, cum_ref, kk_ref, vc_ref):
        rid, tri, second, same, sub = _chunk_consts()
        trit = tri.T
        lb = _sigmoid(lb_ref[pl.ds(0, 1), :] - lb_ref[pl.ds(1, 1), :])
        gain = hg_ref[...]
        ds_ref[...] = jnp.zeros_like(ds_ref)

        def chunk(i, carry, u):
            base = u * CHUNK
            dhg, dlb = carry
            ci = nc - 1 - i
            r0 = pl.multiple_of(ci * CHUNK, CHUNK)
            rows = pl.ds(r0, CHUNK)
            q = q_ref[rows, :]
            qh, sq, sf, fg = _gates(q, f_ref[rows, :], lb)
            v = v_ref[rows, :]
            kk = 1.0 - fg
            cum = _hdot(tri, jnp.log(fg))
            cum_ref[pl.ds(base, CHUNK), :] = cum
            kk_ref[pl.ds(base, CHUNK), :] = kk
            vc_ref[pl.ds(base, CHUNK), :] = v
            o = o_ref[rows, :]
            og = og_ref[rows, :]
            sg = _sigmoid(og)
            rinv = lax.rsqrt(jnp.mean(o * o, axis=-1, keepdims=True) + RMS_EPS)
            yn = o * rinv
            dof = do_ref[rows, :]
            dog_ref[rows, :] = (dof * yn * gain * _silu_grad(og, sg)).astype(BF16)
            dz = dof * og * sg
            dhg = dhg + jnp.sum(dz * yn, axis=0, keepdims=True)
            dy = dz * gain
            do = rinv * (dy - yn * jnp.mean(dy * yn, axis=-1, keepdims=True))
            sprev = st_ref[0, ci]
            dsn = ds_ref[...]
            blast = cum_ref[pl.ds(base + CHUNK - 1, 1), :]
            eq0 = jnp.exp(cum)
            ek0 = jnp.exp(blast - cum)
            dqh = _bdot(do, sprev, NN_DIMS) * eq0
            dkk = _bdot(v, dsn, NN_DIMS) * ek0
            dlast = (jnp.sum(kk * dkk, axis=0, keepdims=True)
                     + jnp.exp(blast) * jnp.sum(dsn * sprev, axis=0, keepdims=True))
            dv = _bdot(kk * ek0, dsn, NT_DIMS)
            ds_ref[...] = dsn * jnp.exp(blast) + _bdot(do, qh * eq0, TN_DIMS)
            dg = qh * dqh - kk * dkk
            da = _bdot(do, v, NT_DIMS)
            a = None
            for (eq, ek), msk in zip(_level_factors(cum, _level_refs(cum_ref, rid, base), second), same):
                ql, kl = (qh * eq).astype(BF16), (kk * ek).astype(BF16)
                al = _bdot(ql, kl, NT_DIMS)
                dal = da
                if msk is not None:
                    al = jnp.where(msk, al, 0.0)
                    dal = jnp.where(msk, da, 0.0)
                a = al if a is None else a + al
                dql = _bdot(dal, kl, NN_DIMS)
                dkl = _bdot(dal, ql, TN_DIMS)
                dqh = dqh + dql * eq
                dkk = dkk + dkl * ek
                dg = dg + (ql.astype(F32) * dql - kl.astype(F32) * dkl)
            dv = dv + _bdot(a, do, TN_DIMS)
            dq_d, dk_d, dv_d = [], [], []
            for sb in range(CHUNK // SUB):
                lo = sb * SUB
                qb = qh[lo:lo + SUB]
                cb = cum[lo:lo + SUB]
                dob = do[lo:lo + SUB]
                dqb = jnp.zeros((SUB, c), F32)
                dkb = jnp.zeros((SUB, c), F32)
                dvb = jnp.zeros((SUB, c), F32)
                for s in range(SUB):
                    e = jnp.where(sub >= s, jnp.exp(jnp.minimum(cb - cum_ref[pl.ds(base + lo + s, 1), :], 0.0)), 0.0)
                    ks = kk_ref[pl.ds(base + lo + s, 1), :]
                    qe = qb * e
                    dacol = jnp.sum(dob * vc_ref[pl.ds(base + lo + s, 1), :], axis=-1, keepdims=True)
                    acol = jnp.sum(qe * ks, axis=-1, keepdims=True)
                    dqb = dqb + dacol * (ks * e)
                    dkb = jnp.where(sub == s, jnp.sum(dacol * qe, axis=0, keepdims=True), dkb)
                    dvb = jnp.where(sub == s, jnp.sum(acol * dob, axis=0, keepdims=True), dvb)
                dq_d.append(dqb)
                dk_d.append(dkb)
                dv_d.append(dvb)
            dq_d = jnp.concatenate(dq_d, axis=0)
            dk_d = jnp.concatenate(dk_d, axis=0)
            dqh = dqh + dq_d
            dkk = dkk + dk_d
            dg = dg + (qh * dq_d - kk * dk_d)
            dv = dv + jnp.concatenate(dv_d, axis=0)
            dlf = _hdot(trit, dg) + dlast
            dfg = dlf / fg - dkk
            df_ref[rows, :] = (dfg * (1.0 - lb) * sf * (1.0 - sf)).astype(BF16)
            dlb = dlb + jnp.sum(dfg * (1.0 - sf), axis=0, keepdims=True)
            dq_ref[rows, :] = (dqh * _silu_grad(q, sq)).astype(BF16)
            dv_ref[rows, :] = dv.astype(BF16)
            return dhg, dlb

        def chunks(i, carry):
            for u in range(HGRN_UNROLL):
                carry = chunk(i * HGRN_UNROLL + u, carry, u)
            return carry

        zero = jnp.zeros((1, c), F32)
        dhg, dlb = lax.fori_loop(0, nc // HGRN_UNROLL, chunks, (zero, zero))
        dhg_ref[...] = dhg
        dl0 = dlb * lb * (1.0 - lb)
        dlb_ref[...] = jnp.where(lax.broadcasted_iota(jnp.int32, (2, c), 0) == 0, dl0, -dl0)

    big = jax.ShapeDtypeStruct((t, HGRN_WIDTH), BF16)
    return pl.pallas_call(
        body, name=name, grid=(N_GROUPS,),
        in_specs=[col(8)] + hin_specs + [col(0), st, lbs, vec],
        out_specs=[col(0)] * 4 + [vec, lbs],
        out_shape=[big] * 4 + [jax.ShapeDtypeStruct((1, HGRN_WIDTH), F32), jax.ShapeDtypeStruct((2, HGRN_WIDTH), F32)],
        scratch_shapes=[pltpu.VMEM((c, c), F32)] + [pltpu.VMEM((HGRN_UNROLL * CHUNK, c), F32)] * 3,
        compiler_params=_cparams(("parallel",)),
    )(dcat, hin, hin, hin, hin, o_raw, states, lb_logits, hg)


ANY = pl.BlockSpec(memory_space=pl.ANY)


def _my_place():
    return lax.axis_index("x"), lax.axis_index("y"), lax.axis_index("c")


HBM = pl.BlockSpec(memory_space=pltpu.HBM)
SEM = pl.BlockSpec(memory_space=pltpu.SEMAPHORE)
EFFECT = pltpu.SideEffectType.DATAFLOW_SIDE_EFFECTING


def _peer(k):
    x, y, c = _my_place()
    px = 1 - x if k & 4 else x
    py = 1 - y if k & 2 else y
    pc = 1 - c if k & 1 else c
    return (px, py, pc), 4 * px + 2 * py + pc


def _slot(land_ref, idx):
    if len(land_ref.shape) == 2:
        ns = land_ref.shape[1] // N_DEV
        return land_ref.at[:, pl.ds(pl.multiple_of(idx * ns, 128), ns)]
    return land_ref.at[idx]


def _exchange_copy(k, src_ref, land_ref, send_sems, recv_sems, scatter, landing):
    x, y, c = _my_place()
    me = 4 * x + 2 * y + c
    to, idx = _peer(k)
    return pltpu.make_async_remote_copy(
        src_ref=_slot(src_ref, idx) if scatter else src_ref,
        dst_ref=_slot(land_ref, idx) if landing else _slot(land_ref, me),
        send_sem=send_sems.at[k - 1], recv_sem=recv_sems.at[k - 1], device_id=to, device_id_type=MESH)


ALL_PEERS = tuple(range(1, N_DEV))
NEAR_PEERS = (1, 2, 4, 6)
SAME_CORE_PEERS = (2, 4, 6)


def _exchange_start(name, src, land, scatter, ks=ALL_PEERS):
    def body(src_ref, land_ref, send_sems, recv_sems, src_thru, land_thru, token):
        for k in ks:
            _exchange_copy(k, src_ref, land_ref, send_sems, recv_sems, scatter, landing=False).start()
        token[...] = jnp.zeros_like(token)

    send_sems, recv_sems, src_thru, land_thru, token = pl.pallas_call(
        body, name=name,
        out_shape=(pltpu.SemaphoreType.DMA((N_DEV - 1,)), pltpu.SemaphoreType.DMA((N_DEV - 1,)),
                   pltpu.HBM(src.shape, src.dtype), pltpu.HBM(land.shape, land.dtype),
                   jax.ShapeDtypeStruct((8, 128), F32)),
        in_specs=(HBM, HBM), out_specs=(SEM, SEM, HBM, HBM, pl.BlockSpec(memory_space=pltpu.VMEM)),
        input_output_aliases={0: 2, 1: 3},
        compiler_params=pltpu.CompilerParams(has_side_effects=EFFECT),
    )(pltpu.with_memory_space_constraint(src, pltpu.HBM), pltpu.with_memory_space_constraint(land, pltpu.HBM))
    return (send_sems, recv_sems, src_thru, land_thru, scatter, ks), token


def _exchange_wait(name, handle, after):
    send_sems, recv_sems, src_thru, land_thru, scatter, ks = handle

    def body(src_ref, land_ref, send_sems, recv_sems, after_ref, src_dead, got_ref):
        for k in ks:
            cp = _exchange_copy(k, src_ref, land_ref, send_sems, recv_sems, scatter, landing=True)
            cp.wait_send()
            cp.wait_recv()

    return pl.pallas_call(
        body, name=name,
        out_shape=(pltpu.HBM(src_thru.shape, src_thru.dtype), pltpu.HBM(land_thru.shape, land_thru.dtype)),
        in_specs=(HBM, HBM, SEM, SEM, ANY), out_specs=(HBM, HBM), input_output_aliases={0: 0, 1: 1},
        compiler_params=pltpu.CompilerParams(has_side_effects=EFFECT),
    )(src_thru, land_thru, send_sems, recv_sems, after)[1]


def _relay_copy(j, land_ref, send_sems, recv_sems, landing):
    x, y, c = _my_place()
    k = SAME_CORE_PEERS[j]
    _, sent = _peer(k)
    _, got = _peer(k + 1)
    return pltpu.make_async_remote_copy(
        src_ref=_slot(land_ref, sent), dst_ref=_slot(land_ref, got) if landing else _slot(land_ref, sent),
        send_sem=send_sems.at[j], recv_sem=recv_sems.at[j], device_id=(x, y, 1 - c), device_id_type=MESH)


def _relay_start(name, land):
    n = len(SAME_CORE_PEERS)

    def body(land_ref, send_sems, recv_sems, land_thru, token):
        for j in range(n):
            _relay_copy(j, land_ref, send_sems, recv_sems, landing=False).start()
        token[...] = jnp.zeros_like(token)

    send_sems, recv_sems, land_thru, token = pl.pallas_call(
        body, name=name,
        out_shape=(pltpu.SemaphoreType.DMA((n,)), pltpu.SemaphoreType.DMA((n,)),
                   pltpu.HBM(land.shape, land.dtype), jax.ShapeDtypeStruct((8, 128), F32)),
        in_specs=(HBM,), out_specs=(SEM, SEM, HBM, pl.BlockSpec(memory_space=pltpu.VMEM)),
        input_output_aliases={0: 2},
        compiler_params=pltpu.CompilerParams(has_side_effects=EFFECT),
    )(pltpu.with_memory_space_constraint(land, pltpu.HBM))
    return (send_sems, recv_sems, land_thru), token


def _relay_wait(name, handle, after):
    send_sems, recv_sems, land_thru = handle

    def body(land_ref, send_sems, recv_sems, after_ref, got_ref):
        for j in range(len(SAME_CORE_PEERS)):
            cp = _relay_copy(j, land_ref, send_sems, recv_sems, landing=True)
            cp.wait_send()
            cp.wait_recv()

    return pl.pallas_call(
        body, name=name, out_shape=pltpu.HBM(land_thru.shape, land_thru.dtype),
        in_specs=(HBM, SEM, SEM, ANY), out_specs=HBM, input_output_aliases={0: 0},
        compiler_params=pltpu.CompilerParams(has_side_effects=EFFECT),
    )(land_thru, send_sems, recv_sems, after)


def _own_cols(name, own, me):
    r, ns = own.shape
    tr = 256

    def body(me_ref, own_ref, land_ref):
        land_ref[...] = own_ref[...]

    return pl.pallas_call(
        body, name=name,
        grid_spec=pltpu.PrefetchScalarGridSpec(
            num_scalar_prefetch=1, grid=(r // tr,),
            in_specs=[pl.BlockSpec((tr, ns), lambda i, me_ref: (i, 0))],
            out_specs=pl.BlockSpec((tr, ns), lambda i, me_ref: (i, me_ref[0]))),
        out_shape=jax.ShapeDtypeStruct((r, N_DEV * ns), own.dtype),
    )(jnp.reshape(me, (1,)).astype(jnp.int32), own)


def _own_slot(own, me):
    land = lax.empty((N_DEV,) + own.shape, own.dtype)
    return lax.dynamic_update_slice_in_dim(land, own[None], me, axis=0)


def _adamw_math(w, g, m, v):
    m = ADAM_B1 * m + (1.0 - ADAM_B1) * g
    v = ADAM_B2 * v + (1.0 - ADAM_B2) * (g * g)
    m_hat = m / (1.0 - ADAM_B1 ** ADAM_STEP)
    v_hat = v / (1.0 - ADAM_B2 ** ADAM_STEP)
    delta = -ADAM_LR * (m_hat / (jnp.sqrt(v_hat) + ADAM_EPS) + ADAM_WD * w)
    return delta, m, v


def _adamw_sum(name, recv, w, m, v, tr, row0=0, partial=None):
    r, c = w.shape
    rr = recv.shape[1]
    off = row0 // tr

    def body(recv_ref, w_ref, m_ref, v_ref, *refs):
        g_ref, d_ref, mo_ref, vo_ref = refs[-4:]
        g = recv_ref[0].astype(F32)
        for j in range(1, N_DEV):
            g = g + recv_ref[j].astype(F32)
        g_ref[...] = g
        d_ref[...], mo_ref[...], vo_ref[...] = _adamw_math(w_ref[...], g, m_ref[...], v_ref[...])

    tile = pl.BlockSpec((tr, c), lambda i: (i + off, 0))
    out = jax.ShapeDtypeStruct((r, c), F32)
    prev = list(partial) if partial is not None else []
    return pl.pallas_call(
        body, name=name, grid=(rr // tr,),
        in_specs=[pl.BlockSpec((N_DEV, tr, c), lambda i: (0, i, 0)), tile, tile, tile] + [ANY] * len(prev),
        out_specs=[tile] * 4, out_shape=[out] * 4,
        input_output_aliases={4 + i: i for i in range(len(prev))},
        compiler_params=_cparams(("parallel",)),
    )(recv, w, m, v, *prev)


def _sum_parts(name, parts):
    _, r, c = parts.shape

    def body(p_ref, o_ref):
        acc = p_ref[0]
        for j in range(1, N_DEV):
            acc = acc + p_ref[j]
        o_ref[...] = acc

    return pl.pallas_call(body, name=name, out_shape=jax.ShapeDtypeStruct((r, c), F32),
                          compiler_params=_cparams())(parts)


def _adamw_small(name, w, g, m, v):
    def body(w_ref, g_ref, m_ref, v_ref, d_ref, mo_ref, vo_ref):
        d_ref[...], mo_ref[...], vo_ref[...] = _adamw_math(w_ref[...], g_ref[...], m_ref[...], v_ref[...])

    out = jax.ShapeDtypeStruct(w.shape, F32)
    return pl.pallas_call(body, name=name, out_shape=[out] * 3, compiler_params=_cparams())(w, g, m, v)


def _pack(pieces, rows):
    flat = jnp.concatenate([p.reshape(-1).astype(F32) for p in pieces])
    return jnp.pad(flat, (0, rows * 128 - flat.shape[0])).reshape(rows, 128)


def _unpack(packed, shapes):
    flat = packed.reshape(-1)
    out, off = [], 0
    for s in shapes:
        n = 1
        for d in s:
            n *= d
        out.append(flat[off:off + n].reshape(s))
        off += n
    return out


def kernel(x, emb_ln_g, emb_ln_b, w_in, conv_w, conv_b, conv_norm_g, conv_norm_b, lb_logits, hgrn_norm_g, w_out, ln1_g, ln1_b, w_ffn_up, ffn_conv_w, ffn_conv_b, w_ffn_down, ln2_g, ln2_b, loss_target, m_emb_ln_g, m_emb_ln_b, m_w_in, m_conv_w, m_conv_b, m_conv_norm_g, m_conv_norm_b, m_lb_logits, m_hgrn_norm_g, m_w_out, m_ln1_g, m_ln1_b, m_w_ffn_up, m_ffn_conv_w, m_ffn_conv_b, m_w_ffn_down, m_ln2_g, m_ln2_b, v_emb_ln_g, v_emb_ln_b, v_w_in, v_conv_w, v_conv_b, v_conv_norm_g, v_conv_norm_b, v_lb_logits, v_hgrn_norm_g, v_w_out, v_ln1_g, v_ln1_b, v_w_ffn_up, v_ffn_conv_w, v_ffn_conv_b, v_w_ffn_down, v_ln2_g, v_ln2_b):
    t = x.shape[1]
    me = 4 * lax.axis_index("x") + 2 * lax.axis_index("y") + lax.axis_index("c")
    x2, tgt = x[0], loss_target[0]
    ns_in, ns_up = w_in.shape[2], w_ffn_up.shape[2]
    rs_out, rs_down = w_out.shape[1], w_ffn_down.shape[1]
    cs, fs = conv_w.shape[2], ffn_conv_w.shape[2]

    def gather_start(name, w, prev, ks=ALL_PEERS, cols=False):
        shard = (w[0] + prev).astype(BF16)
        land = _own_cols(name.replace("ag_", "own_"), shard, me) if cols else _own_slot(shard, me)
        return _exchange_start(name, shard, land, scatter=False, ks=ks)

    h_in, tok = gather_start("ag_w_in_start", w_in, 0.0, NEAR_PEERS, cols=True)
    taps = _pack([conv_w[0], ffn_conv_w[0]], 48) + tok[0, 0]
    h_taps, tok = _exchange_start("ag_taps_start", taps, _own_slot(taps, me), scatter=False)
    h_out, tok = gather_start("ag_w_out_start", w_out, tok[0, 0])
    h_up, tok = gather_start("ag_w_up_start", w_ffn_up, tok[0, 0], NEAR_PEERS, cols=True)
    h_down, tok = gather_start("ag_w_down_start", w_ffn_down, tok[0, 0])

    row = lambda a: a.reshape(1, -1)

    _, h0, h0b, h0bt = _ln_fwd("ln_in", x2, None, row(emb_ln_g) + tok[0, 0], row(emb_ln_b), 1.0)
    h_relay, tok_relay = _relay_start("ag_w_in_relay_start", _exchange_wait("ag_w_in_wait", h_in, h0b))
    win_n = _relay_wait("ag_w_in_relay_wait", h_relay, tok_relay)
    hin = _mm_nn("mm_in", h0b, win_n, F32, tm=2048, tn=ns_in, tk=D_MODEL)
    n_cw, n_fw = CONV_KERNEL * cs, FFN_KERNEL * fs
    taps_g = _exchange_wait("ag_taps_wait", h_taps, hin).reshape(N_DEV, -1)
    cw_full = taps_g[:, :n_cw].reshape(N_DEV, CONV_KERNEL, cs).transpose(1, 0, 2).reshape(CONV_KERNEL, CONV_WIDTH)
    fw_full = taps_g[:, n_cw:n_cw + n_fw].reshape(N_DEV, FFN_KERNEL, fs).transpose(1, 0, 2).reshape(FFN_KERNEL, D_FF)

    o_raw, cat_right, states = _hgrn_fwd("hgrn_fwd", hin, lb_logits, hgrn_norm_g)
    u1, catb = _conv_fwd("conv_fwd", hin, cw_full, conv_b, conv_norm_g, conv_norm_b, cat_right)
    wout_g = _exchange_wait("ag_w_out_wait", h_out, catb).reshape(D_MODEL, D_MODEL)
    h_up_relay, tok = _relay_start("ag_w_up_relay_start", _exchange_wait("ag_w_up_wait", h_up, wout_g))
    mix = _mm_nn("mm_out", catb, wout_g, F32, tm=2048, tn=1024, tk=D_MODEL, after=tok)
    r1, h1, h1b, h1bt = _ln_fwd("ln1", h0, mix, ln1_g, ln1_b, ALPHA)
    wup_n = _relay_wait("ag_w_up_relay_wait", h_up_relay, h1b)
    hf = _mm_nn("mm_up", h1b, wup_n, BF16, tm=1024, tn=1024, tk=D_MODEL)
    actb = _ffn_act_fwd("ffn_act", hf, fw_full, ffn_conv_b)
    wdown_g = _exchange_wait("ag_w_down_wait", h_down, actb).reshape(D_FF, D_MODEL)
    ffn = _mm_nn("mm_down", actb, wdown_g, F32, tm=1024, tn=512, tk=D_FF)
    dr2, dr2b, g_ln2g, g_ln2b, loss = _ln2_loss_bwd("ln2_loss", h1, ffn, ln2_g, ln2_b, tgt)

    def scatter_start(name, parts):
        if parts.ndim == 2:
            ns = parts.shape[1] // N_DEV
            own = lax.dynamic_slice_in_dim(parts, me * ns, ns, axis=1)
        else:
            own = lax.dynamic_index_in_dim(parts, me, axis=0, keepdims=False)
        return _exchange_start(name, parts, _own_slot(own, me), scatter=True)

    dact = _mm_nt("mm_dact", dr2b, wdown_g, BF16, tm=1024, tn=D_FF // 2, tk=D_MODEL)
    gw_down = _matmul(
        "mm_dw_down", actb, dr2b, (D_FF, D_MODEL), BF16, (N_DEV // 2, D_MODEL // 1024, 2),
        pl.BlockSpec((t // 2, 2 * rs_down), lambda i, j, kk: (kk, i)),
        pl.BlockSpec((t // 2, 1024), lambda i, j, kk: (kk, j)),
        pl.BlockSpec((2 * rs_down, 1024), lambda i, j, kk: (i, j)), nt="tn")
    s_down, tok = scatter_start("a2a_w_down_start", gw_down.reshape(N_DEV, rs_down, D_MODEL))
    dhf, g_fw, g_fb = _ffn_act_bwd("ffn_act_bwd", dact, hf, fw_full, ffn_conv_b + tok[0, 0])
    tm = min(1024, t)
    gw_up = _matmul(
        "mm_dw_up", h1bt, dhf, (D_MODEL, 2 * D_FF), BF16, (D_MODEL // 1024, 2 * D_FF // 512, 1),
        pl.BlockSpec((1024, t), lambda i, j, kk: (i, 0)),
        pl.BlockSpec((1, t, 512), lambda i, j, kk: (j // 11, 0, j % 11)),
        pl.BlockSpec((1024, 512), lambda i, j, kk: (i, j)), nt=False)
    s_up, tok = scatter_start("a2a_w_up_start", gw_up)
    tkf = D_FF // 2
    dh1 = _matmul(
        "mm_dh1", dhf, wup_n, (t, D_MODEL), F32, (t // tm, D_MODEL // 1024, 4),
        pl.BlockSpec((1, tm, tkf), lambda i, j, kk: (kk // 2, i, kk % 2)),
        pl.BlockSpec((1024, tkf), lambda i, j, kk: (j, kk)),
        pl.BlockSpec((tm, 1024), lambda i, j, kk: (i, j)), nt=True, after=tok)
    dr1, dr1b, g_ln1g, g_ln1b = _ln_bwd("ln1_bwd", r1, dr2, dh1, ln1_g + tok[0, 0], ALPHA, True)
    gw_out = _matmul(
        "mm_dw_out", catb, dr1b, (D_MODEL, D_MODEL), BF16, (2, 2, 2),
        pl.BlockSpec((t // 2, 1024), lambda i, j, kk: (kk, i)),
        pl.BlockSpec((t // 2, 1024), lambda i, j, kk: (kk, j)),
        pl.BlockSpec((1024, 1024), lambda i, j, kk: (i, j)), nt="tn")
    s_out, tok = scatter_start("a2a_w_out_start", gw_out.reshape(N_DEV, rs_out, D_MODEL))
    dcat = _mm_nt("mm_dcat", dr1b, wout_g, F32, tm=2048, tn=1024, tk=D_MODEL, after=tok)
    da, dgate, g_cw, g_cb, g_cng, g_cnb = _conv_bwd("conv_bwd", dcat, u1, hin, cw_full, conv_norm_g + tok[0, 0],
                                                    conv_norm_b)
    dq, df, di, dog, g_hg, g_lb = _hgrn_bwd("hgrn_bwd", dcat, hin, o_raw, states, lb_logits, hgrn_norm_g)
    dhin = jnp.concatenate([da, dgate, dq, df, di, dog], axis=1)
    half = D_MODEL // 2
    gw_in_a = _mm_grad_cols("mm_dw_in_a", h0bt, dhin, ns_in, 0, half, after=tok)
    s_in_a, tok = scatter_start("a2a_w_in_a_start", gw_in_a)
    gw_in_b = _mm_grad_cols("mm_dw_in_b", h0bt, dhin, ns_in, half, half, after=tok)
    s_in_b, tok = scatter_start("a2a_w_in_b_start", gw_in_b)
    dh0 = _mm_nt("mm_dh0", dhin, win_n, F32, tm=1024, tn=512, tk=IN_PROJ, after=tok)
    grad_x, g_eg, g_eb = _ln_bwd("ln_in_bwd", x2, dr1, dh0, row(emb_ln_g), ALPHA, False)

    small_shapes = [(D_MODEL,), (D_MODEL,), (CONV_KERNEL, CONV_WIDTH), (1, CONV_WIDTH), (1, CONV_WIDTH),
                    (1, CONV_WIDTH), (2, HGRN_WIDTH), (1, HGRN_WIDTH), (1, D_MODEL), (1, D_MODEL),
                    (FFN_KERNEL, D_FF), (1, D_FF), (1, D_MODEL), (1, D_MODEL), (128,)]
    rows_small = 569
    packed = _pack([g_eg, g_eb, g_cw[:CONV_KERNEL], g_cb, g_cng, g_cnb, g_lb, g_hg, g_ln1g, g_ln1b,
                    g_fw[:FFN_KERNEL], g_fb, g_ln2g, g_ln2b, loss], rows_small)
    h_small, tok = _exchange_start("ag_small_start", packed, _own_slot(packed, me), scatter=False)

    def big(name, handle, after, w, m, v, tr):
        recv = _exchange_wait("a2a_" + name + "_wait", handle, after)
        return [o[None] for o in _adamw_sum("adamw_" + name, recv, w[0], m[0], v[0], tr)]

    u_down = big("w_down", s_down, tok, w_ffn_down, m_w_ffn_down, v_w_ffn_down, 64)
    u_up = big("w_up", s_up, u_down[1], w_ffn_up, m_w_ffn_up, v_w_ffn_up, 64)
    u_out = big("w_out", s_out, u_up[1], w_out, m_w_out, v_w_out, 64)
    summed = _sum_parts("sum_small", _exchange_wait("ag_small_wait", h_small, u_out[1]))
    (s_eg, s_eb, s_cw, s_cb, s_cng, s_cnb, s_lb, s_hg, s_l1g, s_l1b, s_fw, s_fb, s_l2g, s_l2b,
     s_loss) = _unpack(summed, small_shapes)
    s_cw = lax.dynamic_slice_in_dim(s_cw, me * cs, cs, axis=1)[None]
    s_fw = lax.dynamic_slice_in_dim(s_fw, me * fs, fs, axis=1)[None]
    g_small = [s_eg, s_eb, s_cw, s_cb, s_cng, s_cnb, s_lb, s_hg, s_l1g, s_l1b, s_fw, s_fb, s_l2g, s_l2b]
    w_small = [emb_ln_g, emb_ln_b, conv_w, conv_b, conv_norm_g, conv_norm_b, lb_logits, hgrn_norm_g,
               ln1_g, ln1_b, ffn_conv_w, ffn_conv_b, ln2_g, ln2_b]
    m_small = [m_emb_ln_g, m_emb_ln_b, m_conv_w, m_conv_b, m_conv_norm_g, m_conv_norm_b, m_lb_logits,
               m_hgrn_norm_g, m_ln1_g, m_ln1_b, m_ffn_conv_w, m_ffn_conv_b, m_ln2_g, m_ln2_b]
    v_small = [v_emb_ln_g, v_emb_ln_b, v_conv_w, v_conv_b, v_conv_norm_g, v_conv_norm_b, v_lb_logits,
               v_hgrn_norm_g, v_ln1_g, v_ln1_b, v_ffn_conv_w, v_ffn_conv_b, v_ln2_g, v_ln2_b]
    rows_own = 236
    shapes_own = [w.shape for w in w_small]
    upd = _adamw_small("adamw_small", _pack(w_small, rows_own), _pack(g_small, rows_own),
                       _pack(m_small, rows_own), _pack(v_small, rows_own))
    d_small, nm_small, nv_small = (_unpack(u, shapes_own) for u in upd)
    g_small = [g.reshape(s) for g, s in zip(g_small, shapes_own)]

    recv_a = _exchange_wait("a2a_w_in_a_wait", s_in_a, upd[0])
    part = _adamw_sum("adamw_w_in_a", recv_a, w_in[0], m_w_in[0], v_w_in[0], 128)
    recv_b = _exchange_wait("a2a_w_in_b_wait", s_in_b, part[1])
    u_in = [o[None] for o in _adamw_sum("adamw_w_in_b", recv_b, w_in[0], m_w_in[0], v_w_in[0], 128,
                                        row0=half, partial=part)]

    def ordered(small, i_in, i_out, i_up, i_down):
        (eg, eb, cw, cb, cng, cnb, lb, hg, l1g, l1b, fw, fb, l2g, l2b) = small
        return [eg, eb, i_in, cw, cb, cng, cnb, lb, hg, i_out, l1g, l1b, i_up, fw, fb, i_down, l2g, l2b]

    outs = [s_loss[0], grad_x[None]]
    for k, small in enumerate([g_small, d_small, nm_small, nv_small]):
        outs += ordered(small, u_in[k], u_out[k], u_up[k], u_down[k])
    return tuple(outs)
```

```python
import functools

import jax
import jax.numpy as jnp
from jax import lax
from jax.experimental import pallas as pl
from jax.experimental.pallas import tpu as pltpu

F32 = jnp.float32
BF16 = jnp.bfloat16

N_DEV = 8
D_MODEL = 2048
CONV_WIDTH = 1024
CONV_KERNEL = 31
HGRN_WIDTH = 1024
GROUP = 128
N_GROUPS = 8
IN_PROJ = 2 * CONV_WIDTH + 4 * HGRN_WIDTH
D_FF = 5632
FFN_KERNEL = 3
CHUNK = 64
SUB = 8
LN_EPS = 1e-5
RMS_EPS = 1e-6
ALPHA = 2.0 ** 0.25
ADAM_LR, ADAM_B1, ADAM_B2, ADAM_EPS, ADAM_WD, ADAM_STEP = 0.001, 0.9, 0.999, 1e-08, 0.01, 10

VMEM_LIMIT = 56 * 1024 * 1024
MESH = pl.DeviceIdType.MESH


def _cparams(sem=None):
    return pltpu.CompilerParams(dimension_semantics=sem, vmem_limit_bytes=VMEM_LIMIT)


def _sigmoid(x):
    return 0.5 * jnp.tanh(0.5 * x) + 0.5


def _matmul(name, a, b, out_shape, out_dtype, grid, a_spec, b_spec, o_spec, nt, after=None):
    nk = grid[2]
    dims = {True: (((1,), (1,)), ((), ())), False: (((1,), (0,)), ((), ())), "tn": (((0,), (0,)), ((), ()))}[nt]
    extra = [] if after is None else [after]

    def body(a_ref, b_ref, *rest):
        o_ref, *scratch = rest[len(extra):]
        if len(a_ref.shape) == 3 and a_ref.shape[0] > 1:
            kp = a_ref.shape[2]
            part = None
            for p in range(a_ref.shape[0]):
                d = lax.dot_general(a_ref[p], b_ref[:, p * kp:(p + 1) * kp], dims, preferred_element_type=F32)
                part = d if part is None else part + d
        else:
            av = a_ref[0] if len(a_ref.shape) == 3 else a_ref[...]
            bv = b_ref[0] if len(b_ref.shape) == 3 else b_ref[...]
            part = lax.dot_general(av, bv, dims, preferred_element_type=F32)

        def write(res):
            if len(o_ref.shape) == 3:
                o_ref[0] = res.astype(out_dtype)
            else:
                o_ref[...] = res.astype(out_dtype)

        if nk == 1:
            write(part)
            return
        acc_ref, = scratch
        k = pl.program_id(2)

        @pl.when(k == 0)
        def _():
            acc_ref[...] = part

        @pl.when(jnp.logical_and(k > 0, k < nk - 1))
        def _():
            acc_ref[...] += part

        @pl.when(k == nk - 1)
        def _():
            write(acc_ref[...] + part)

    acc_shape = o_spec.block_shape[-2:]
    assert all(g >= 1 for g in grid), (name, grid)
    return pl.pallas_call(
        body, name=name, grid=grid, in_specs=[a_spec, b_spec] + [pl.BlockSpec(memory_space=pl.ANY)] * len(extra),
        out_specs=o_spec, out_shape=jax.ShapeDtypeStruct(out_shape, out_dtype),
        scratch_shapes=[pltpu.VMEM(acc_shape, F32)] if nk > 1 else [],
        compiler_params=_cparams(("parallel", "parallel", "arbitrary")),
    )(a, b, *extra)


def _mm_nn(name, a, w, out_dtype, tm, tn, tk, after=None):
    m, k = a.shape
    tm, tk = min(tm, m), min(tk, k)
    n = w.shape[1]
    return _matmul(
        name, a, w, (m, n), out_dtype, (m // tm, n // tn, k // tk),
        pl.BlockSpec((tm, tk), lambda i, j, kk: (i, kk)),
        pl.BlockSpec((tk, tn), lambda i, j, kk: (kk, j)),
        pl.BlockSpec((tm, tn), lambda i, j, kk: (i, j)), nt=False, after=after)


def _mm_nt(name, a, w, out_dtype, tm, tn, tk, after=None):
    m, k = a.shape
    tm = min(tm, m)
    n = w.shape[0]
    return _matmul(
        name, a, w, (m, n), out_dtype, (m // tm, n // tn, k // tk),
        pl.BlockSpec((tm, tk), lambda i, j, kk: (i, kk)),
        pl.BlockSpec((tn, tk), lambda i, j, kk: (j, kk)),
        pl.BlockSpec((tm, tn), lambda i, j, kk: (i, j)), nt=True, after=after)


def _mm_grad_cols(name, at, b, ns, row0, rows, after, tm=1024, tk=4096):
    t = at.shape[1]
    tk = min(tk, t)
    off = row0 // tm
    return _matmul(
        name, at, b, (N_DEV, rows, ns), BF16, (rows // tm, N_DEV, t // tk),
        pl.BlockSpec((tm, tk), lambda i, j, kk: (i + off, kk)),
        pl.BlockSpec((tk, ns), lambda i, j, kk: (kk, j)),
        pl.BlockSpec((1, tm, ns), lambda i, j, kk: (j, i, 0)), nt=False, after=after)


LN_ROWS = 256


def _ln_stats(r):
    mu = jnp.mean(r, axis=-1, keepdims=True)
    xc = r - mu
    var = jnp.mean(xc * xc, axis=-1, keepdims=True)
    rstd = lax.rsqrt(var + LN_EPS)
    return xc * rstd, rstd


def _row_spec(d):
    return pl.BlockSpec((LN_ROWS, d), lambda i: (i, 0))


def _vec_spec(d):
    return pl.BlockSpec((1, d), lambda i: (0, 0))


def _ln_fwd(name, a, m, g, b, alpha):
    t, d = a.shape
    has_m = m is not None

    def body(*refs):
        if has_m:
            a_ref, m_ref, g_ref, b_ref, r_ref, y_ref, yb_ref, yt_ref = refs
            r = alpha * a_ref[...] + m_ref[...]
            r_ref[...] = r
        else:
            a_ref, g_ref, b_ref, y_ref, yb_ref, yt_ref = refs
            r = a_ref[...]
        xhat, _ = _ln_stats(r)
        y = xhat * g_ref[...] + b_ref[...]
        y_ref[...] = y
        yb_ref[...] = y.astype(BF16)
        yt_ref[...] = y.T.astype(BF16)

    ins = [a] + ([m] if has_m else []) + [g, b]
    in_specs = [_row_spec(d)] * (2 if has_m else 1) + [_vec_spec(d)] * 2
    outs = ([jax.ShapeDtypeStruct((t, d), F32)] if has_m else []) + [
        jax.ShapeDtypeStruct((t, d), F32), jax.ShapeDtypeStruct((t, d), BF16), jax.ShapeDtypeStruct((d, t), BF16)]
    res = pl.pallas_call(
        body, name=name, grid=(t // LN_ROWS,), in_specs=in_specs,
        out_specs=[_row_spec(d)] * (len(outs) - 1) + [pl.BlockSpec((d, LN_ROWS), lambda i: (0, i))], out_shape=outs,
        compiler_params=_cparams(("parallel",)),
    )(*ins)
    return res if has_m else (None, *res)


def _ln_bwd_math(r, dy, g):
    xhat, rstd = _ln_stats(r)
    dxhat = dy * g
    m1 = jnp.mean(dxhat, axis=-1, keepdims=True)
    m2 = jnp.mean(dxhat * xhat, axis=-1, keepdims=True)
    dr = rstd * (dxhat - m1 - xhat * m2)
    return dr, jnp.sum(dy * xhat, axis=0, keepdims=True), jnp.sum(dy, axis=0, keepdims=True)


def _ln2_loss_bwd(name, h1, ffn, g, b, tgt):
    t, d = h1.shape

    def body(h1_ref, f_ref, g_ref, b_ref, t_ref, dr_ref, drb_ref, dg_ref, db_ref, loss_ref):
        @pl.when(pl.program_id(0) == 0)
        def _():
            dg_ref[...] = jnp.zeros_like(dg_ref)
            db_ref[...] = jnp.zeros_like(db_ref)
            loss_ref[...] = jnp.zeros_like(loss_ref)

        r = ALPHA * h1_ref[...] + f_ref[...]
        xhat, _ = _ln_stats(r)
        e = xhat * g_ref[...] + b_ref[...] - t_ref[...]
        loss_ref[...] += 0.5 / d * jnp.sum(e * e)
        dr, dg, db = _ln_bwd_math(r, e * (1.0 / d), g_ref[...])
        dr_ref[...] = dr
        drb_ref[...] = dr.astype(BF16)
        dg_ref[...] += dg
        db_ref[...] += db

    return pl.pallas_call(
        body, name=name, grid=(t // LN_ROWS,),
        in_specs=[_row_spec(d), _row_spec(d), _vec_spec(d), _vec_spec(d), _row_spec(d)],
        out_specs=[_row_spec(d), _row_spec(d), _vec_spec(d), _vec_spec(d), _vec_spec(128)],
        out_shape=[jax.ShapeDtypeStruct((t, d), F32), jax.ShapeDtypeStruct((t, d), BF16),
                   jax.ShapeDtypeStruct((1, d), F32), jax.ShapeDtypeStruct((1, d), F32),
                   jax.ShapeDtypeStruct((1, 128), F32)],
        compiler_params=_cparams(("arbitrary",)),
    )(h1, ffn, g, b, tgt)


def _ln_bwd(name, r, dya, dyb, g, alpha, want_bf16):
    t, d = r.shape

    def body(r_ref, dya_ref, dyb_ref, g_ref, *outs):
        dr_ref = outs[0]
        dg_ref, db_ref = outs[-2:]

        @pl.when(pl.program_id(0) == 0)
        def _():
            dg_ref[...] = jnp.zeros_like(dg_ref)
            db_ref[...] = jnp.zeros_like(db_ref)

        dy = alpha * dya_ref[...] + dyb_ref[...]
        dr, dg, db = _ln_bwd_math(r_ref[...], dy, g_ref[...])
        dr_ref[...] = dr
        if want_bf16:
            outs[1][...] = dr.astype(BF16)
        dg_ref[...] += dg
        db_ref[...] += db

    big = [jax.ShapeDtypeStruct((t, d), F32)] + ([jax.ShapeDtypeStruct((t, d), BF16)] if want_bf16 else [])
    return pl.pallas_call(
        body, name=name, grid=(t // LN_ROWS,),
        in_specs=[_row_spec(d)] * 3 + [_vec_spec(d)],
        out_specs=[_row_spec(d)] * len(big) + [_vec_spec(d)] * 2,
        out_shape=big + [jax.ShapeDtypeStruct((1, d), F32)] * 2,
        compiler_params=_cparams(("arbitrary",)),
    )(r, dya, dyb, g)


CONV_ROWS = 64
CONV_UNROLL = 8
FFN_UNROLL = 2


def _unrolled(n, unroll, fn, init):
    def body(i, carry):
        for u in range(unroll):
            carry = fn(i * unroll + u, carry)
        return carry

    return lax.fori_loop(0, n // unroll, body, init)


def _for_shifted(ref, r0, tm, shifts, fn):
    for s in shifts:
        fn(s, ref[pl.ds(r0 + s, tm), :])


def _col_spec(t, cb, off=0):
    return pl.BlockSpec((t, cb), lambda j: (0, j + off))


def _ffn_act_fwd(name, hf, w, b, cb=128):
    t = hf.shape[0]
    f = hf.shape[1] // 2
    nb = f // cb
    tm = CONV_ROWS

    def body(g_ref, v_ref, w_ref, b_ref, act_ref, pad_ref):
        pad_ref[pl.ds(0, 8), :] = jnp.zeros((8, cb), F32)
        pad_ref[pl.ds(8, t), :] = g_ref[...].astype(F32)
        wv = [w_ref[pl.ds(k, 1), :] for k in range(FFN_KERNEL)]
        bias = b_ref[...]

        def tile(i, carry):
            r0 = pl.multiple_of(i * tm, tm)
            acc = [jnp.broadcast_to(bias, (tm, cb))]

            def tap(s, rows):
                acc[0] = acc[0] + wv[s - 6] * rows

            _for_shifted(pad_ref, r0, tm, (6, 7, 8), tap)
            gc = acc[0]
            act_ref[pl.ds(r0, tm), :] = (gc * _sigmoid(gc) * v_ref[pl.ds(r0, tm), :].astype(F32)).astype(BF16)
            return carry

        _unrolled(t // tm, FFN_UNROLL, tile, 0)

    return pl.pallas_call(
        body, name=name, grid=(nb,),
        in_specs=[_col_spec(t, cb), _col_spec(t, cb, nb),
                  pl.BlockSpec((FFN_KERNEL, cb), lambda j: (0, j)), pl.BlockSpec((1, cb), lambda j: (0, j))],
        out_specs=_col_spec(t, cb), out_shape=jax.ShapeDtypeStruct((t, f), BF16),
        scratch_shapes=[pltpu.VMEM((t + 8, cb), F32)],
        compiler_params=_cparams(("parallel",)),
    )(hf, hf, w, b)


def _ffn_act_bwd(name, dact, hf, w, b, cb=128):
    t = hf.shape[0]
    f = hf.shape[1] // 2
    nb = f // cb
    tm = CONV_ROWS

    def body(da_ref, g_ref, v_ref, w_ref, b_ref, dhf_ref, dw_ref, db_ref, pad_ref, dgc_ref):
        pad_ref[pl.ds(0, 8), :] = jnp.zeros((8, cb), F32)
        pad_ref[pl.ds(8, t), :] = g_ref[...].astype(F32)
        dgc_ref[pl.ds(t, 8), :] = jnp.zeros((8, cb), F32)
        wv = [w_ref[pl.ds(k, 1), :] for k in range(FFN_KERNEL)]
        bias = b_ref[...]

        def tile_a(i, carry):
            r0 = pl.multiple_of(i * tm, tm)
            taps = {}
            _for_shifted(pad_ref, r0, tm, (6, 7, 8), lambda s, rows: taps.__setitem__(s, rows))
            gc = bias + wv[0] * taps[6] + wv[1] * taps[7] + wv[2] * taps[8]
            sg = _sigmoid(gc)
            da = da_ref[pl.ds(r0, tm), :].astype(F32)
            dhf_ref[1, pl.ds(r0, tm), :] = (da * gc * sg).astype(BF16)
            dgc = da * v_ref[pl.ds(r0, tm), :].astype(F32) * sg * (1.0 + gc * (1.0 - sg))
            dgc_ref[pl.ds(r0, tm), :] = dgc
            sums = [jnp.sum(dgc * taps[6 + k], axis=0, keepdims=True) for k in range(3)]
            sums.append(jnp.sum(dgc, axis=0, keepdims=True))
            return tuple(c + s for c, s in zip(carry, sums))

        zero = jnp.zeros((1, cb), F32)
        dw0, dw1, dw2, dbias = _unrolled(t // tm, FFN_UNROLL, tile_a, (zero, zero, zero, zero))
        row = lax.broadcasted_iota(jnp.int32, (8, cb), 0)
        dw_ref[...] = jnp.where(row == 0, dw0, jnp.where(row == 1, dw1, jnp.where(row == 2, dw2, 0.0)))
        db_ref[...] = dbias

        def tile_b(i, carry):
            r0 = pl.multiple_of(i * tm, tm)
            acc = [jnp.zeros((tm, cb), F32)]

            def tap(s, rows):
                acc[0] = acc[0] + wv[2 - s] * rows

            _for_shifted(dgc_ref, r0, tm, (0, 1, 2), tap)
            dhf_ref[0, pl.ds(r0, tm), :] = acc[0].astype(BF16)
            return carry

        lax.fori_loop(0, t // tm, tile_b, 0)

    return pl.pallas_call(
        body, name=name, grid=(nb,),
        in_specs=[_col_spec(t, cb), _col_spec(t, cb), _col_spec(t, cb, nb),
                  pl.BlockSpec((FFN_KERNEL, cb), lambda j: (0, j)), pl.BlockSpec((1, cb), lambda j: (0, j))],
        out_specs=[pl.BlockSpec((2, t, cb), lambda j: (0, 0, j)),
                   pl.BlockSpec((8, cb), lambda j: (0, j)), pl.BlockSpec((1, cb), lambda j: (0, j))],
        out_shape=[jax.ShapeDtypeStruct((2, t, f), BF16), jax.ShapeDtypeStruct((8, f), F32),
                   jax.ShapeDtypeStruct((1, f), F32)],
        scratch_shapes=[pltpu.VMEM((t + 8, cb), F32), pltpu.VMEM((t + 8, cb), F32)],
        compiler_params=_cparams(("parallel",)),
    )(dact, hf, hf, w, b)


def _silu_grad(z, sg):
    return sg * (1.0 + z * (1.0 - sg))


def _conv_fwd(name, hin, w, b, ng, nb_, cat):
    t = hin.shape[0]
    c = GROUP
    tm = CONV_ROWS
    pad = 32
    shifts = tuple(2 + k for k in range(CONV_KERNEL))

    def body(a_ref, gt_ref, w_ref, b_ref, ng_ref, nb_ref, cat_ref, u1_ref, u3_ref, pad_ref):
        pad_ref[pl.ds(0, pad), :] = jnp.zeros((pad, c), F32)
        pad_ref[pl.ds(pad, t), :] = a_ref[...] * _sigmoid(gt_ref[...])
        bias, gam, bet = b_ref[...], ng_ref[...], nb_ref[...]

        def tile(i, carry):
            r0 = pl.multiple_of(i * tm, tm)
            acc = [jnp.broadcast_to(bias, (tm, c))]

            def tap(s, rows):
                acc[0] = acc[0] + w_ref[pl.ds(s - 2, 1), :] * rows

            _for_shifted(pad_ref, r0, tm, shifts, tap)
            u1 = acc[0]
            u1_ref[pl.ds(r0, tm), :] = u1
            xhat, _ = _ln_stats(u1)
            u2 = xhat * gam + bet
            u3_ref[pl.ds(r0, tm), :] = (u2 * _sigmoid(u2)).astype(BF16)
            return carry

        _unrolled(t // tm, CONV_UNROLL, tile, 0)

    vec = pl.BlockSpec((1, c), lambda j: (0, j))
    return pl.pallas_call(
        body, name=name, grid=(N_GROUPS,),
        in_specs=[_col_spec(t, c), _col_spec(t, c, N_GROUPS),
                  pl.BlockSpec((CONV_KERNEL, c), lambda j: (0, j)), vec, vec, vec, ANY],
        out_specs=[_col_spec(t, c), _col_spec(t, c)],
        out_shape=[jax.ShapeDtypeStruct((t, CONV_WIDTH), F32), jax.ShapeDtypeStruct(cat.shape, BF16)],
        input_output_aliases={6: 1},
        scratch_shapes=[pltpu.VMEM((t + pad, c), F32)],
        compiler_params=_cparams(("parallel",)),
    )(hin, hin, w, b, ng, nb_, cat)


def _conv_bwd(name, dcat, u1, hin, w, ng, nb_):
    t = hin.shape[0]
    c = GROUP
    tm = CONV_ROWS
    pad = 32
    nk = CONV_KERNEL

    def body(du3_ref, u1_ref, a_ref, gt_ref, w_ref, ng_ref, nb_ref,
             da_ref, dgt_ref, dw_ref, db_ref, dng_ref, dnb_ref, u0_ref, du1_ref, dwp_ref):
        u0_ref[pl.ds(0, pad), :] = jnp.zeros((pad, c), F32)
        u0_ref[pl.ds(pad, t), :] = a_ref[...] * _sigmoid(gt_ref[...])
        du1_ref[pl.ds(t, pad), :] = jnp.zeros((pad, c), F32)
        dwp_ref[...] = jnp.zeros_like(dwp_ref)
        gam, bet = ng_ref[...], nb_ref[...]

        def tile_a(i, carry):
            r0 = pl.multiple_of(i * tm, tm)
            u1 = u1_ref[pl.ds(r0, tm), :]
            xhat, rstd = _ln_stats(u1)
            u2 = xhat * gam + bet
            sg = _sigmoid(u2)
            du2 = du3_ref[pl.ds(r0, tm), :] * _silu_grad(u2, sg)
            dxhat = du2 * gam
            m1 = jnp.mean(dxhat, axis=-1, keepdims=True)
            m2 = jnp.mean(dxhat * xhat, axis=-1, keepdims=True)
            du1 = rstd * (dxhat - m1 - xhat * m2)
            du1_ref[pl.ds(r0, tm), :] = du1
            sums = (jnp.sum(du1, axis=0, keepdims=True), jnp.sum(du2 * xhat, axis=0, keepdims=True),
                    jnp.sum(du2, axis=0, keepdims=True))
            return tuple(x + s for x, s in zip(carry, sums))

        zero = jnp.zeros((1, c), F32)
        dbias, dgam, dbet = _unrolled(t // tm, CONV_UNROLL, tile_a, (zero, zero, zero))
        db_ref[...] = dbias
        dng_ref[...] = dgam
        dnb_ref[...] = dbet

        def tile_b(i, carry):
            r0 = pl.multiple_of(i * tm, tm)
            du1 = du1_ref[pl.ds(r0, tm), :]
            acc = [jnp.zeros((tm, c), F32)]

            def tap_dx(s, rows):
                acc[0] = acc[0] + w_ref[pl.ds(nk - 1 - s, 1), :] * rows

            _for_shifted(du1_ref, r0, tm, tuple(range(nk)), tap_dx)

            def tap_dw(s, rows):
                part = (du1 * rows).reshape(tm // 8, 8, c).sum(axis=0)
                dwp_ref[s - 2] = dwp_ref[s - 2] + part

            _for_shifted(u0_ref, r0, tm, tuple(2 + k for k in range(nk)), tap_dw)
            du0 = acc[0]
            a = a_ref[pl.ds(r0, tm), :]
            sg = _sigmoid(gt_ref[pl.ds(r0, tm), :])
            da_ref[pl.ds(r0, tm), :] = (du0 * sg).astype(BF16)
            dgt_ref[pl.ds(r0, tm), :] = (du0 * a * sg * (1.0 - sg)).astype(BF16)
            return carry

        lax.fori_loop(0, t // tm, tile_b, 0)
        dw_ref[...] = jnp.sum(dwp_ref[...], axis=1)

    vec = pl.BlockSpec((1, c), lambda j: (0, j))
    vshape = jax.ShapeDtypeStruct((1, CONV_WIDTH), F32)
    return pl.pallas_call(
        body, name=name, grid=(N_GROUPS,),
        in_specs=[_col_spec(t, c), _col_spec(t, c), _col_spec(t, c), _col_spec(t, c, N_GROUPS),
                  pl.BlockSpec((nk, c), lambda j: (0, j)), vec, vec],
        out_specs=[_col_spec(t, c), _col_spec(t, c), pl.BlockSpec((32, c), lambda j: (0, j)), vec, vec, vec],
        out_shape=[jax.ShapeDtypeStruct((t, CONV_WIDTH), BF16), jax.ShapeDtypeStruct((t, CONV_WIDTH), BF16),
                   jax.ShapeDtypeStruct((32, CONV_WIDTH), F32), vshape, vshape, vshape],
        scratch_shapes=[pltpu.VMEM((t + pad, c), F32), pltpu.VMEM((t + pad, c), F32),
                        pltpu.VMEM((32, 8, c), F32)],
        compiler_params=_cparams(("parallel",)),
    )(dcat, u1, hin, hin, w, ng, nb_)


LEVELS = (64, 32, 16)
HGRN_UNROLL = 4
HGRN_UNROLL_FWD = 16
NT_DIMS = (((1,), (1,)), ((), ()))
NN_DIMS = (((1,), (0,)), ((), ()))
TN_DIMS = (((0,), (0,)), ((), ()))


def _bdot(a, b, dims):
    return lax.dot_general(a.astype(BF16), b.astype(BF16), dims, preferred_element_type=F32)


def _hdot(a, b):
    return jnp.dot(a, b, precision=lax.Precision.HIGHEST, preferred_element_type=F32)


def _chunk_consts():
    rid = lax.broadcasted_iota(jnp.int32, (CHUNK, GROUP), 0)
    ti = lax.broadcasted_iota(jnp.int32, (CHUNK, CHUNK), 0)
    si = lax.broadcasted_iota(jnp.int32, (CHUNK, CHUNK), 1)
    tri = (si <= ti).astype(F32)
    second = [(rid & (b // 2)) != 0 for b in LEVELS]
    same = [None] + [(ti // b) == (si // b) for b in LEVELS[1:]]
    sub = lax.broadcasted_iota(jnp.int32, (SUB, GROUP), 0)
    return rid, tri, second, same, sub


def _level_refs(cum_ref, rid, base):
    row = lambda i: cum_ref[pl.ds(base + i, 1), :]
    l1 = jnp.broadcast_to(row(31), (CHUNK, GROUP))
    l2 = jnp.where(rid < 32, row(15), row(47))
    l3 = jnp.where(rid < 16, row(7), jnp.where(rid < 32, row(23), jnp.where(rid < 48, row(39), row(55))))
    return l1, l2, l3


def _level_factors(cum, brefs, second):
    out = []
    for bref, sec in zip(brefs, second):
        eq = jnp.where(sec, jnp.exp(jnp.minimum(cum - bref, 0.0)), 0.0)
        ek = jnp.where(sec, 0.0, jnp.exp(jnp.minimum(bref - cum, 0.0)))
        out.append((eq, ek))
    return out


def _gates(q, f, lb):
    sq = _sigmoid(q)
    sf = _sigmoid(f)
    fg = lb + (1.0 - lb) * sf
    return q * sq, sq, sf, fg


def _hgrn_specs(t, nc):
    c = GROUP
    col = lambda off: pl.BlockSpec((t, c), lambda h: (0, h + off))
    hin_specs = [col(16), col(24), col(32), col(40)]
    vec = pl.BlockSpec((1, c), lambda h: (0, h))
    lbs = pl.BlockSpec((2, c), lambda h: (0, h))
    st = pl.BlockSpec((1, nc, c, c), lambda h: (h, 0, 0, 0))
    return col, hin_specs, vec, lbs, st


def _hgrn_fwd(name, hin, lb_logits, hg):
    t = hin.shape[0]
    nc = t // CHUNK
    c = GROUP
    col, hin_specs, vec, lbs, st = _hgrn_specs(t, nc)

    def body(q_ref, f_ref, v_ref, og_ref, lb_ref, hg_ref, o_ref, ob_ref, st_ref,
             s_ref, cum_ref, kk_ref, vc_ref):
        rid, tri, second, same, sub = _chunk_consts()
        lb = _sigmoid(lb_ref[pl.ds(0, 1), :] - lb_ref[pl.ds(1, 1), :])
        gain = hg_ref[...]
        s_ref[...] = jnp.zeros_like(s_ref)

        def chunk(ci, u):
            base = u * CHUNK
            r0 = pl.multiple_of(ci * CHUNK, CHUNK)
            rows = pl.ds(r0, CHUNK)
            qh, _, _, fg = _gates(q_ref[rows, :], f_ref[rows, :], lb)
            v = v_ref[rows, :]
            kk = 1.0 - fg
            cum = _hdot(tri, jnp.log(fg))
            cum_ref[pl.ds(base, CHUNK), :] = cum
            kk_ref[pl.ds(base, CHUNK), :] = kk
            vc_ref[pl.ds(base, CHUNK), :] = v
            sprev = s_ref[...]
            st_ref[0, ci] = sprev
            blast = cum_ref[pl.ds(base + CHUNK - 1, 1), :]
            o = _bdot(qh * jnp.exp(cum), sprev, NT_DIMS)
            s_ref[...] = sprev * jnp.exp(blast) + _bdot(v, kk * jnp.exp(blast - cum), TN_DIMS)
            a = None
            for (eq, ek), msk in zip(_level_factors(cum, _level_refs(cum_ref, rid, base), second), same):
                al = _bdot(qh * eq, kk * ek, NT_DIMS)
                al = al if msk is None else jnp.where(msk, al, 0.0)
                a = al if a is None else a + al
            o = o + _bdot(a, v, NN_DIMS)
            diag = []
            for sb in range(CHUNK // SUB):
                lo = sb * SUB
                qb = qh[lo:lo + SUB]
                cb = cum[lo:lo + SUB]
                od = jnp.zeros((SUB, c), F32)
                for s in range(SUB):
                    e = jnp.where(sub >= s, jnp.exp(jnp.minimum(cb - cum_ref[pl.ds(base + lo + s, 1), :], 0.0)), 0.0)
                    acol = jnp.sum(qb * e * kk_ref[pl.ds(base + lo + s, 1), :], axis=-1, keepdims=True)
                    od = od + acol * vc_ref[pl.ds(base + lo + s, 1), :]
                diag.append(od)
            o = o + jnp.concatenate(diag, axis=0)
            o_ref[rows, :] = o
            y = o * lax.rsqrt(jnp.mean(o * o, axis=-1, keepdims=True) + RMS_EPS) * gain
            og = og_ref[rows, :]
            ob_ref[rows, :] = (y * og * _sigmoid(og)).astype(BF16)

        def chunks(i, carry):
            for u in range(HGRN_UNROLL_FWD):
                chunk(i * HGRN_UNROLL_FWD + u, u)
            return carry

        lax.fori_loop(0, nc // HGRN_UNROLL_FWD, chunks, 0)

    return pl.pallas_call(
        body, name=name, grid=(N_GROUPS,),
        in_specs=hin_specs + [lbs, vec],
        out_specs=[col(0), col(N_GROUPS), st],
        out_shape=[jax.ShapeDtypeStruct((t, HGRN_WIDTH), F32), jax.ShapeDtypeStruct((t, CONV_WIDTH + HGRN_WIDTH), BF16),
                   jax.ShapeDtypeStruct((N_GROUPS, nc, c, c), F32)],
        scratch_shapes=[pltpu.VMEM((c, c), F32)] + [pltpu.VMEM((HGRN_UNROLL_FWD * CHUNK, c), F32)] * 3,
        compiler_params=_cparams(("parallel",)),
    )(hin, hin, hin, hin, lb_logits, hg)


def _hgrn_bwd(name, dcat, hin, o_raw, states, lb_logits, hg):
    t = hin.shape[0]
    nc = t // CHUNK
    c = GROUP
    col, hin_specs, vec, lbs, st = _hgrn_specs(t, nc)

    def body(do_ref, q_ref, f_ref, v_ref, og_ref, o_ref, st_ref, lb_ref, hg_ref,
             dq_ref, df_ref, dv_ref, dog_ref, dhg_ref, dlb_ref,
             ds_ref, cum_ref, kk_ref, vc_ref):
        rid, tri, second, same, sub = _chunk_consts()
        trit = tri.T
        lb = _sigmoid(lb_ref[pl.ds(0, 1), :] - lb_ref[pl.ds(1, 1), :])
        gain = hg_ref[...]
        ds_ref[...] = jnp.zeros_like(ds_ref)

        def chunk(i, carry, u):
            base = u * CHUNK
            dhg, dlb = carry
            ci = nc - 1 - i
            r0 = pl.multiple_of(ci * CHUNK, CHUNK)
            rows = pl.ds(r0, CHUNK)
            q = q_ref[rows, :]
            qh, sq, sf, fg = _gates(q, f_ref[rows, :], lb)
            v = v_ref[rows, :]
            kk = 1.0 - fg
            cum = _hdot(tri, jnp.log(fg))
            cum_ref[pl.ds(base, CHUNK), :] = cum
            kk_ref[pl.ds(base, CHUNK), :] = kk
            vc_ref[pl.ds(base, CHUNK), :] = v
            o = o_ref[rows, :]
            og = og_ref[rows, :]
            sg = _sigmoid(og)
            rinv = lax.rsqrt(jnp.mean(o * o, axis=-1, keepdims=True) + RMS_EPS)
            yn = o * rinv
            dof = do_ref[rows, :]
            dog_ref[rows, :] = (dof * yn * gain * _silu_grad(og, sg)).astype(BF16)
            dz = dof * og * sg
            dhg = dhg + jnp.sum(dz * yn, axis=0, keepdims=True)
            dy = dz * gain
            do = rinv * (dy - yn * jnp.mean(dy * yn, axis=-1, keepdims=True))
            sprev = st_ref[0, ci]
            dsn = ds_ref[...]
            blast = cum_ref[pl.ds(base + CHUNK - 1, 1), :]
            eq0 = jnp.exp(cum)
            ek0 = jnp.exp(blast - cum)
            dqh = _bdot(do, sprev, NN_DIMS) * eq0
            dkk = _bdot(v, dsn, NN_DIMS) * ek0
            dlast = (jnp.sum(kk * dkk, axis=0, keepdims=True)
                     + jnp.exp(blast) * jnp.sum(dsn * sprev, axis=0, keepdims=True))
            dv = _bdot(kk * ek0, dsn, NT_DIMS)
            ds_ref[...] = dsn * jnp.exp(blast) + _bdot(do, qh * eq0, TN_DIMS)
            dg = qh * dqh - kk * dkk
            da = _bdot(do, v, NT_DIMS)
            a = None
            for (eq, ek), msk in zip(_level_factors(cum, _level_refs(cum_ref, rid, base), second), same):
                ql, kl = (qh * eq).astype(BF16), (kk * ek).astype(BF16)
                al = _bdot(ql, kl, NT_DIMS)
                dal = da
                if msk is not None:
                    al = jnp.where(msk, al, 0.0)
                    dal = jnp.where(msk, da, 0.0)
                a = al if a is None else a + al
                dql = _bdot(dal, kl, NN_DIMS)
                dkl = _bdot(dal, ql, TN_DIMS)
                dqh = dqh + dql * eq
                dkk = dkk + dkl * ek
                dg = dg + (ql.astype(F32) * dql - kl.astype(F32) * dkl)
            dv = dv + _bdot(a, do, TN_DIMS)
            dq_d, dk_d, dv_d = [], [], []
            for sb in range(CHUNK // SUB):
                lo = sb * SUB
                qb = qh[lo:lo + SUB]
                cb = cum[lo:lo + SUB]
                dob = do[lo:lo + SUB]
                dqb = jnp.zeros((SUB, c), F32)
                dkb = jnp.zeros((SUB, c), F32)
                dvb = jnp.zeros((SUB, c), F32)
                for s in range(SUB):
                    e = jnp.where(sub >= s, jnp.exp(jnp.minimum(cb - cum_ref[pl.ds(base + lo + s, 1), :], 0.0)), 0.0)
                    ks = kk_ref[pl.ds(base + lo + s, 1), :]
                    qe = qb * e
                    dacol = jnp.sum(dob * vc_ref[pl.ds(base + lo + s, 1), :], axis=-1, keepdims=True)
                    acol = jnp.sum(qe * ks, axis=-1, keepdims=True)
                    dqb = dqb + dacol * (ks * e)
                    dkb = jnp.where(sub == s, jnp.sum(dacol * qe, axis=0, keepdims=True), dkb)
                    dvb = jnp.where(sub == s, jnp.sum(acol * dob, axis=0, keepdims=True), dvb)
                dq_d.append(dqb)
                dk_d.append(dkb)
                dv_d.append(dvb)
            dq_d = jnp.concatenate(dq_d, axis=0)
            dk_d = jnp.concatenate(dk_d, axis=0)
            dqh = dqh + dq_d
            dkk = dkk + dk_d
            dg = dg + (qh * dq_d - kk * dk_d)
            dv = dv + jnp.concatenate(dv_d, axis=0)
            dlf = _hdot(trit, dg) + dlast
            dfg = dlf / fg - dkk
            df_ref[rows, :] = (dfg * (1.0 - lb) * sf * (1.0 - sf)).astype(BF16)
            dlb = dlb + jnp.sum(dfg * (1.0 - sf), axis=0, keepdims=True)
            dq_ref[rows, :] = (dqh * _silu_grad(q, sq)).astype(BF16)
            dv_ref[rows, :] = dv.astype(BF16)
            return dhg, dlb

        def chunks(i, carry):
            for u in range(HGRN_UNROLL):
                carry = chunk(i * HGRN_UNROLL + u, carry, u)
            return carry

        zero = jnp.zeros((1, c), F32)
        dhg, dlb = lax.fori_loop(0, nc // HGRN_UNROLL, chunks, (zero, zero))
        dhg_ref[...] = dhg
        dl0 = dlb * lb * (1.0 - lb)
        dlb_ref[...] = jnp.where(lax.broadcasted_iota(jnp.int32, (2, c), 0) == 0, dl0, -dl0)

    big = jax.ShapeDtypeStruct((t, HGRN_WIDTH), BF16)
    return pl.pallas_call(
        body, name=name, grid=(N_GROUPS,),
        in_specs=[col(8)] + hin_specs + [col(0), st, lbs, vec],
        out_specs=[col(0)] * 4 + [vec, lbs],
        out_shape=[big] * 4 + [jax.ShapeDtypeStruct((1, HGRN_WIDTH), F32), jax.ShapeDtypeStruct((2, HGRN_WIDTH), F32)],
        scratch_shapes=[pltpu.VMEM((c, c), F32)] + [pltpu.VMEM((HGRN_UNROLL * CHUNK, c), F32)] * 3,
        compiler_params=_cparams(("parallel",)),
    )(dcat, hin, hin, hin, hin, o_raw, states, lb_logits, hg)


ANY = pl.BlockSpec(memory_space=pl.ANY)


def _my_place():
    return lax.axis_index("x"), lax.axis_index("y"), lax.axis_index("c")


HBM = pl.BlockSpec(memory_space=pltpu.HBM)
SEM = pl.BlockSpec(memory_space=pltpu.SEMAPHORE)
EFFECT = pltpu.SideEffectType.DATAFLOW_SIDE_EFFECTING


def _peer(k):
    x, y, c = _my_place()
    px = 1 - x if k & 4 else x
    py = 1 - y if k & 2 else y
    pc = 1 - c if k & 1 else c
    return (px, py, pc), 4 * px + 2 * py + pc


def _slot(land_ref, idx):
    if len(land_ref.shape) == 2:
        ns = land_ref.shape[1] // N_DEV
        return land_ref.at[:, pl.ds(pl.multiple_of(idx * ns, 128), ns)]
    return land_ref.at[idx]


def _exchange_copy(k, src_ref, land_ref, send_sems, recv_sems, scatter, landing):
    x, y, c = _my_place()
    me = 4 * x + 2 * y + c
    to, idx = _peer(k)
    return pltpu.make_async_remote_copy(
        src_ref=_slot(src_ref, idx) if scatter else src_ref,
        dst_ref=_slot(land_ref, idx) if landing else _slot(land_ref, me),
        send_sem=send_sems.at[k - 1], recv_sem=recv_sems.at[k - 1], device_id=to, device_id_type=MESH)


ALL_PEERS = tuple(range(1, N_DEV))
NEAR_PEERS = (1, 2, 4, 6)
SAME_CORE_PEERS = (2, 4, 6)


def _exchange_start(name, src, land, scatter, ks=ALL_PEERS):
    def body(src_ref, land_ref, send_sems, recv_sems, src_thru, land_thru, token):
        for k in ks:
            _exchange_copy(k, src_ref, land_ref, send_sems, recv_sems, scatter, landing=False).start()
        token[...] = jnp.zeros_like(token)

    send_sems, recv_sems, src_thru, land_thru, token = pl.pallas_call(
        body, name=name,
        out_shape=(pltpu.SemaphoreType.DMA((N_DEV - 1,)), pltpu.SemaphoreType.DMA((N_DEV - 1,)),
                   pltpu.HBM(src.shape, src.dtype), pltpu.HBM(land.shape, land.dtype),
                   jax.ShapeDtypeStruct((8, 128), F32)),
        in_specs=(HBM, HBM), out_specs=(SEM, SEM, HBM, HBM, pl.BlockSpec(memory_space=pltpu.VMEM)),
        input_output_aliases={0: 2, 1: 3},
        compiler_params=pltpu.CompilerParams(has_side_effects=EFFECT),
    )(pltpu.with_memory_space_constraint(src, pltpu.HBM), pltpu.with_memory_space_constraint(land, pltpu.HBM))
    return (send_sems, recv_sems, src_thru, land_thru, scatter, ks), token


def _exchange_wait(name, handle, after):
    send_sems, recv_sems, src_thru, land_thru, scatter, ks = handle

    def body(src_ref, land_ref, send_sems, recv_sems, after_ref, src_dead, got_ref):
        for k in ks:
            cp = _exchange_copy(k, src_ref, land_ref, send_sems, recv_sems, scatter, landing=True)
            cp.wait_send()
            cp.wait_recv()

    return pl.pallas_call(
        body, name=name,
        out_shape=(pltpu.HBM(src_thru.shape, src_thru.dtype), pltpu.HBM(land_thru.shape, land_thru.dtype)),
        in_specs=(HBM, HBM, SEM, SEM, ANY), out_specs=(HBM, HBM), input_output_aliases={0: 0, 1: 1},
        compiler_params=pltpu.CompilerParams(has_side_effects=EFFECT),
    )(src_thru, land_thru, send_sems, recv_sems, after)[1]


def _relay_copy(j, land_ref, send_sems, recv_sems, landing):
    x, y, c = _my_place()
    k = SAME_CORE_PEERS[j]
    _, sent = _peer(k)
    _, got = _peer(k + 1)
    return pltpu.make_async_remote_copy(
        src_ref=_slot(land_ref, sent), dst_ref=_slot(land_ref, got) if landing else _slot(land_ref, sent),
        send_sem=send_sems.at[j], recv_sem=recv_sems.at[j], device_id=(x, y, 1 - c), device_id_type=MESH)


def _relay_start(name, land):
    n = len(SAME_CORE_PEERS)

    def body(land_ref, send_sems, recv_sems, land_thru, token):
        for j in range(n):
            _relay_copy(j, land_ref, send_sems, recv_sems, landing=False).start()
        token[...] = jnp.zeros_like(token)

    send_sems, recv_sems, land_thru, token = pl.pallas_call(
        body, name=name,
        out_shape=(pltpu.SemaphoreType.DMA((n,)), pltpu.SemaphoreType.DMA((n,)),
                   pltpu.HBM(land.shape, land.dtype), jax.ShapeDtypeStruct((8, 128), F32)),
        in_specs=(HBM,), out_specs=(SEM, SEM, HBM, pl.BlockSpec(memory_space=pltpu.VMEM)),
        input_output_aliases={0: 2},
        compiler_params=pltpu.CompilerParams(has_side_effects=EFFECT),
    )(pltpu.with_memory_space_constraint(land, pltpu.HBM))
    return (send_sems, recv_sems, land_thru), token


def _relay_wait(name, handle, after):
    send_sems, recv_sems, land_thru = handle

    def body(land_ref, send_sems, recv_sems, after_ref, got_ref):
        for j in range(len(SAME_CORE_PEERS)):
            cp = _relay_copy(j, land_ref, send_sems, recv_sems, landing=True)
            cp.wait_send()
            cp.wait_recv()

    return pl.pallas_call(
        body, name=name, out_shape=pltpu.HBM(land_thru.shape, land_thru.dtype),
        in_specs=(HBM, SEM, SEM, ANY), out_specs=HBM, input_output_aliases={0: 0},
        compiler_params=pltpu.CompilerParams(has_side_effects=EFFECT),
    )(land_thru, send_sems, recv_sems, after)


def _own_cols(name, own, me):
    r, ns = own.shape
    tr = 256

    def body(me_ref, own_ref, land_ref):
        land_ref[...] = own_ref[...]

    return pl.pallas_call(
        body, name=name,
        grid_spec=pltpu.PrefetchScalarGridSpec(
            num_scalar_prefetch=1, grid=(r // tr,),
            in_specs=[pl.BlockSpec((tr, ns), lambda i, me_ref: (i, 0))],
            out_specs=pl.BlockSpec((tr, ns), lambda i, me_ref: (i, me_ref[0]))),
        out_shape=jax.ShapeDtypeStruct((r, N_DEV * ns), own.dtype),
    )(jnp.reshape(me, (1,)).astype(jnp.int32), own)


def _own_slot(own, me):
    land = lax.empty((N_DEV,) + own.shape, own.dtype)
    return lax.dynamic_update_slice_in_dim(land, own[None], me, axis=0)


def _adamw_math(w, g, m, v):
    m = ADAM_B1 * m + (1.0 - ADAM_B1) * g
    v = ADAM_B2 * v + (1.0 - ADAM_B2) * (g * g)
    m_hat = m / (1.0 - ADAM_B1 ** ADAM_STEP)
    v_hat = v / (1.0 - ADAM_B2 ** ADAM_STEP)
    delta = -ADAM_LR * (m_hat / (jnp.sqrt(v_hat) + ADAM_EPS) + ADAM_WD * w)
    return delta, m, v


def _adamw_sum(name, recv, w, m, v, tr, row0=0, partial=None):
    r, c = w.shape
    rr = recv.shape[1]
    off = row0 // tr

    def body(recv_ref, w_ref, m_ref, v_ref, *refs):
        g_ref, d_ref, mo_ref, vo_ref = refs[-4:]
        g = recv_ref[0].astype(F32)
        for j in range(1, N_DEV):
            g = g + recv_ref[j].astype(F32)
        g_ref[...] = g
        d_ref[...], mo_ref[...], vo_ref[...] = _adamw_math(w_ref[...], g, m_ref[...], v_ref[...])

    tile = pl.BlockSpec((tr, c), lambda i: (i + off, 0))
    out = jax.ShapeDtypeStruct((r, c), F32)
    prev = list(partial) if partial is not None else []
    return pl.pallas_call(
        body, name=name, grid=(rr // tr,),
        in_specs=[pl.BlockSpec((N_DEV, tr, c), lambda i: (0, i, 0)), tile, tile, tile] + [ANY] * len(prev),
        out_specs=[tile] * 4, out_shape=[out] * 4,
        input_output_aliases={4 + i: i for i in range(len(prev))},
        compiler_params=_cparams(("parallel",)),
    )(recv, w, m, v, *prev)


def _sum_parts(name, parts):
    _, r, c = parts.shape

    def body(p_ref, o_ref):
        acc = p_ref[0]
        for j in range(1, N_DEV):
            acc = acc + p_ref[j]
        o_ref[...] = acc

    return pl.pallas_call(body, name=name, out_shape=jax.ShapeDtypeStruct((r, c), F32),
                          compiler_params=_cparams())(parts)


def _adamw_small(name, w, g, m, v):
    def body(w_ref, g_ref, m_ref, v_ref, d_ref, mo_ref, vo_ref):
        d_ref[...], mo_ref[...], vo_ref[...] = _adamw_math(w_ref[...], g_ref[...], m_ref[...], v_ref[...])

    out = jax.ShapeDtypeStruct(w.shape, F32)
    return pl.pallas_call(body, name=name, out_shape=[out] * 3, compiler_params=_cparams())(w, g, m, v)


def _pack(pieces, rows):
    flat = jnp.concatenate([p.reshape(-1).astype(F32) for p in pieces])
    return jnp.pad(flat, (0, rows * 128 - flat.shape[0])).reshape(rows, 128)


def _unpack(packed, shapes):
    flat = packed.reshape(-1)
    out, off = [], 0
    for s in shapes:
        n = 1
        for d in s:
            n *= d
        out.append(flat[off:off + n].reshape(s))
        off += n
    return out


def kernel(x, emb_ln_g, emb_ln_b, w_in, conv_w, conv_b, conv_norm_g, conv_norm_b, lb_logits, hgrn_norm_g, w_out, ln1_g, ln1_b, w_ffn_up, ffn_conv_w, ffn_conv_b, w_ffn_down, ln2_g, ln2_b, loss_target, m_emb_ln_g, m_emb_ln_b, m_w_in, m_conv_w, m_conv_b, m_conv_norm_g, m_conv_norm_b, m_lb_logits, m_hgrn_norm_g, m_w_out, m_ln1_g, m_ln1_b, m_w_ffn_up, m_ffn_conv_w, m_ffn_conv_b, m_w_ffn_down, m_ln2_g, m_ln2_b, v_emb_ln_g, v_emb_ln_b, v_w_in, v_conv_w, v_conv_b, v_conv_norm_g, v_conv_norm_b, v_lb_logits, v_hgrn_norm_g, v_w_out, v_ln1_g, v_ln1_b, v_w_ffn_up, v_ffn_conv_w, v_ffn_conv_b, v_w_ffn_down, v_ln2_g, v_ln2_b):
    t = x.shape[1]
    me = 4 * lax.axis_index("x") + 2 * lax.axis_index("y") + lax.axis_index("c")
    x2, tgt = x[0], loss_target[0]
    ns_in, ns_up = w_in.shape[2], w_ffn_up.shape[2]
    rs_out, rs_down = w_out.shape[1], w_ffn_down.shape[1]
    cs, fs = conv_w.shape[2], ffn_conv_w.shape[2]

    def gather_start(name, w, prev, ks=ALL_PEERS, cols=False):
        shard = (w[0] + prev).astype(BF16)
        land = _own_cols(name.replace("ag_", "own_"), shard, me) if cols else _own_slot(shard, me)
        return _exchange_start(name, shard, land, scatter=False, ks=ks)

    h_in, tok = gather_start("ag_w_in_start", w_in, 0.0, NEAR_PEERS, cols=True)
    taps = _pack([conv_w[0], ffn_conv_w[0]], 48) + tok[0, 0]
    h_taps, tok = _exchange_start("ag_taps_start", taps, _own_slot(taps, me), scatter=False)
    h_out, tok = gather_start("ag_w_out_start", w_out, tok[0, 0])
    h_up, tok = gather_start("ag_w_up_start", w_ffn_up, tok[0, 0], NEAR_PEERS, cols=True)
    h_down, tok = gather_start("ag_w_down_start", w_ffn_down, tok[0, 0])

    row = lambda a: a.reshape(1, -1)

    _, h0, h0b, h0bt = _ln_fwd("ln_in", x2, None, row(emb_ln_g) + tok[0, 0], row(emb_ln_b), 1.0)
    h_relay, tok_relay = _relay_start("ag_w_in_relay_start", _exchange_wait("ag_w_in_wait", h_in, h0b))
    win_n = _relay_wait("ag_w_in_relay_wait", h_relay, tok_relay)
    hin = _mm_nn("mm_in", h0b, win_n, F32, tm=2048, tn=ns_in, tk=D_MODEL)
    n_cw, n_fw = CONV_KERNEL * cs, FFN_KERNEL * fs
    taps_g = _exchange_wait("ag_taps_wait", h_taps, hin).reshape(N_DEV, -1)
    cw_full = taps_g[:, :n_cw].reshape(N_DEV, CONV_KERNEL, cs).transpose(1, 0, 2).reshape(CONV_KERNEL, CONV_WIDTH)
    fw_full = taps_g[:, n_cw:n_cw + n_fw].reshape(N_DEV, FFN_KERNEL, fs).transpose(1, 0, 2).reshape(FFN_KERNEL, D_FF)

    o_raw, cat_right, states = _hgrn_fwd("hgrn_fwd", hin, lb_logits, hgrn_norm_g)
    u1, catb = _conv_fwd("conv_fwd", hin, cw_full, conv_b, conv_norm_g, conv_norm_b, cat_right)
    wout_g = _exchange_wait("ag_w_out_wait", h_out, catb).reshape(D_MODEL, D_MODEL)
    h_up_relay, tok = _relay_start("ag_w_up_relay_start", _exchange_wait("ag_w_up_wait", h_up, wout_g))
    mix = _mm_nn("mm_out", catb, wout_g, F32, tm=2048, tn=1024, tk=D_MODEL, after=tok)
    r1, h1, h1b, h1bt = _ln_fwd("ln1", h0, mix, ln1_g, ln1_b, ALPHA)
    wup_n = _relay_wait("ag_w_up_relay_wait", h_up_relay, h1b)
    hf = _mm_nn("mm_up", h1b, wup_n, BF16, tm=1024, tn=1024, tk=D_MODEL)
    actb = _ffn_act_fwd("ffn_act", hf, fw_full, ffn_conv_b)
    wdown_g = _exchange_wait("ag_w_down_wait", h_down, actb).reshape(D_FF, D_MODEL)
    ffn = _mm_nn("mm_down", actb, wdown_g, F32, tm=1024, tn=512, tk=D_FF)
    dr2, dr2b, g_ln2g, g_ln2b, loss = _ln2_loss_bwd("ln2_loss", h1, ffn, ln2_g, ln2_b, tgt)

    def scatter_start(name, parts):
        if parts.ndim == 2:
            ns = parts.shape[1] // N_DEV
            own = lax.dynamic_slice_in_dim(parts, me * ns, ns, axis=1)
        else:
            own = lax.dynamic_index_in_dim(parts, me, axis=0, keepdims=False)
        return _exchange_start(name, parts, _own_slot(own, me), scatter=True)

    dact = _mm_nt("mm_dact", dr2b, wdown_g, BF16, tm=1024, tn=D_FF // 2, tk=D_MODEL)
    gw_down = _matmul(
        "mm_dw_down", actb, dr2b, (D_FF, D_MODEL), BF16, (N_DEV // 2, D_MODEL // 1024, 2),
        pl.BlockSpec((t // 2, 2 * rs_down), lambda i, j, kk: (kk, i)),
        pl.BlockSpec((t // 2, 1024), lambda i, j, kk: (kk, j)),
        pl.BlockSpec((2 * rs_down, 1024), lambda i, j, kk: (i, j)), nt="tn")
    s_down, tok = scatter_start("a2a_w_down_start", gw_down.reshape(N_DEV, rs_down, D_MODEL))
    dhf, g_fw, g_fb = _ffn_act_bwd("ffn_act_bwd", dact, hf, fw_full, ffn_conv_b + tok[0, 0])
    tm = min(1024, t)
    gw_up = _matmul(
        "mm_dw_up", h1bt, dhf, (D_MODEL, 2 * D_FF), BF16, (D_MODEL // 1024, 2 * D_FF // 512, 1),
        pl.BlockSpec((1024, t), lambda i, j, kk: (i, 0)),
        pl.BlockSpec((1, t, 512), lambda i, j, kk: (j // 11, 0, j % 11)),
        pl.BlockSpec((1024, 512), lambda i, j, kk: (i, j)), nt=False)
    s_up, tok = scatter_start("a2a_w_up_start", gw_up)
    tkf = D_FF // 2
    dh1 = _matmul(
        "mm_dh1", dhf, wup_n, (t, D_MODEL), F32, (t // tm, D_MODEL // 1024, 4),
        pl.BlockSpec((1, tm, tkf), lambda i, j, kk: (kk // 2, i, kk % 2)),
        pl.BlockSpec((1024, tkf), lambda i, j, kk: (j, kk)),
        pl.BlockSpec((tm, 1024), lambda i, j, kk: (i, j)), nt=True, after=tok)
    dr1, dr1b, g_ln1g, g_ln1b = _ln_bwd("ln1_bwd", r1, dr2, dh1, ln1_g + tok[0, 0], ALPHA, True)
    gw_out = _matmul(
        "mm_dw_out", catb, dr1b, (D_MODEL, D_MODEL), BF16, (2, 2, 2),
        pl.BlockSpec((t // 2, 1024), lambda i, j, kk: (kk, i)),
        pl.BlockSpec((t // 2, 1024), lambda i, j, kk: (kk, j)),
        pl.BlockSpec((1024, 1024), lambda i, j, kk: (i, j)), nt="tn")
    s_out, tok = scatter_start("a2a_w_out_start", gw_out.reshape(N_DEV, rs_out, D_MODEL))
    dcat = _mm_nt("mm_dcat", dr1b, wout_g, F32, tm=2048, tn=1024, tk=D_MODEL, after=tok)
    da, dgate, g_cw, g_cb, g_cng, g_cnb = _conv_bwd("conv_bwd", dcat, u1, hin, cw_full, conv_norm_g + tok[0, 0],
                                                    conv_norm_b)
    dq, df, di, dog, g_hg, g_lb = _hgrn_bwd("hgrn_bwd", dcat, hin, o_raw, states, lb_logits, hgrn_norm_g)
    dhin = jnp.concatenate([da, dgate, dq, df, di, dog], axis=1)
    half = D_MODEL // 2
    gw_in_a = _mm_grad_cols("mm_dw_in_a", h0bt, dhin, ns_in, 0, half, after=tok)
    s_in_a, tok = scatter_start("a2a_w_in_a_start", gw_in_a)
    gw_in_b = _mm_grad_cols("mm_dw_in_b", h0bt, dhin, ns_in, half, half, after=tok)
    s_in_b, tok = scatter_start("a2a_w_in_b_start", gw_in_b)
    dh0 = _mm_nt("mm_dh0", dhin, win_n, F32, tm=1024, tn=512, tk=IN_PROJ, after=tok)
    grad_x, g_eg, g_eb = _ln_bwd("ln_in_bwd", x2, dr1, dh0, row(emb_ln_g), ALPHA, False)

    small_shapes = [(D_MODEL,), (D_MODEL,), (CONV_KERNEL, CONV_WIDTH), (1, CONV_WIDTH), (1, CONV_WIDTH),
                    (1, CONV_WIDTH), (2, HGRN_WIDTH), (1, HGRN_WIDTH), (1, D_MODEL), (1, D_MODEL),
                    (FFN_KERNEL, D_FF), (1, D_FF), (1, D_MODEL), (1, D_MODEL), (128,)]
    rows_small = 569
    packed = _pack([g_eg, g_eb, g_cw[:CONV_KERNEL], g_cb, g_cng, g_cnb, g_lb, g_hg, g_ln1g, g_ln1b,
                    g_fw[:FFN_KERNEL], g_fb, g_ln2g, g_ln2b, loss], rows_small)
    h_small, tok = _exchange_start("ag_small_start", packed, _own_slot(packed, me), scatter=False)

    def big(name, handle, after, w, m, v, tr):
        recv = _exchange_wait("a2a_" + name + "_wait", handle, after)
        return [o[None] for o in _adamw_sum("adamw_" + name, recv, w[0], m[0], v[0], tr)]

    u_down = big("w_down", s_down, tok, w_ffn_down, m_w_ffn_down, v_w_ffn_down, 64)
    u_up = big("w_up", s_up, u_down[1], w_ffn_up, m_w_ffn_up, v_w_ffn_up, 64)
    u_out = big("w_out", s_out, u_up[1], w_out, m_w_out, v_w_out, 64)
    summed = _sum_parts("sum_small", _exchange_wait("ag_small_wait", h_small, u_out[1]))
    (s_eg, s_eb, s_cw, s_cb, s_cng, s_cnb, s_lb, s_hg, s_l1g, s_l1b, s_fw, s_fb, s_l2g, s_l2b,
     s_loss) = _unpack(summed, small_shapes)
    s_cw = lax.dynamic_slice_in_dim(s_cw, me * cs, cs, axis=1)[None]
    s_fw = lax.dynamic_slice_in_dim(s_fw, me * fs, fs, axis=1)[None]
    g_small = [s_eg, s_eb, s_cw, s_cb, s_cng, s_cnb, s_lb, s_hg, s_l1g, s_l1b, s_fw, s_fb, s_l2g, s_l2b]
    w_small = [emb_ln_g, emb_ln_b, conv_w, conv_b, conv_norm_g, conv_norm_b, lb_logits, hgrn_norm_g,
               ln1_g, ln1_b, ffn_conv_w, ffn_conv_b, ln2_g, ln2_b]
    m_small = [m_emb_ln_g, m_emb_ln_b, m_conv_w, m_conv_b, m_conv_norm_g, m_conv_norm_b, m_lb_logits,
               m_hgrn_norm_g, m_ln1_g, m_ln1_b, m_ffn_conv_w, m_ffn_conv_b, m_ln2_g, m_ln2_b]
    v_small = [v_emb_ln_g, v_emb_ln_b, v_conv_w, v_conv_b, v_conv_norm_g, v_conv_norm_b, v_lb_logits,
               v_hgrn_norm_g, v_ln1_g, v_ln1_b, v_ffn_conv_w, v_ffn_conv_b, v_ln2_g, v_ln2_b]
    rows_own = 236
    shapes_own = [w.shape for w in w_small]
    upd = _adamw_small("adamw_small", _pack(w_small, rows_own), _pack(g_small, rows_own),
                       _pack(m_small, rows_own), _pack(v_small, rows_own))
    d_small, nm_small, nv_small = (_unpack(u, shapes_own) for u in upd)
    g_small = [g.reshape(s) for g, s in zip(g_small, shapes_own)]

    recv_a = _exchange_wait("a2a_w_in_a_wait", s_in_a, upd[0])
    part = _adamw_sum("adamw_w_in_a", recv_a, w_in[0], m_w_in[0], v_w_in[0], 128)
    recv_b = _exchange_wait("a2a_w_in_b_wait", s_in_b, part[1])
    u_in = [o[None] for o in _adamw_sum("adamw_w_in_b", recv_b, w_in[0], m_w_in[0], v_w_in[0], 128,
                                        row0=half, partial=part)]

    def ordered(small, i_in, i_out, i_up, i_down):
        (eg, eb, cw, cb, cng, cnb, lb, hg, l1g, l1b, fw, fb, l2g, l2b) = small
        return [eg, eb, i_in, cw, cb, cng, cnb, lb, hg, i_out, l1g, l1b, i_up, fw, fb, i_down, l2g, l2b]

    outs = [s_loss[0], grad_x[None]]
    for k, small in enumerate([g_small, d_small, nm_small, nv_small]):
        outs += ordered(small, u_in[k], u_out[k], u_up[k], u_down[k])
    return tuple(outs)
```

```python
import functools

import jax
import jax.numpy as jnp
from jax import lax
from jax.experimental import pallas as pl
from jax.experimental.pallas import tpu as pltpu

F32 = jnp.float32
BF16 = jnp.bfloat16

N_DEV = 8
D_MODEL = 2048
CONV_WIDTH = 1024
CONV_KERNEL = 31
HGRN_WIDTH = 1024
GROUP = 128
N_GROUPS = 8
IN_PROJ = 2 * CONV_WIDTH + 4 * HGRN_WIDTH
D_FF = 5632
FFN_KERNEL = 3
CHUNK = 64
SUB = 8
LN_EPS = 1e-5
RMS_EPS = 1e-6
ALPHA = 2.0 ** 0.25
ADAM_LR, ADAM_B1, ADAM_B2, ADAM_EPS, ADAM_WD, ADAM_STEP = 0.001, 0.9, 0.999, 1e-08, 0.01, 10

VMEM_LIMIT = 56 * 1024 * 1024
MESH = pl.DeviceIdType.MESH


def _cparams(sem=None):
    return pltpu.CompilerParams(dimension_semantics=sem, vmem_limit_bytes=VMEM_LIMIT)


def _sigmoid(x):
    return 0.5 * jnp.tanh(0.5 * x) + 0.5


def _matmul(name, a, b, out_shape, out_dtype, grid, a_spec, b_spec, o_spec, nt, after=None):
    nk = grid[2]
    dims = {True: (((1,), (1,)), ((), ())), False: (((1,), (0,)), ((), ())), "tn": (((0,), (0,)), ((), ()))}[nt]
    extra = [] if after is None else [after]

    def body(a_ref, b_ref, *rest):
        o_ref, *scratch = rest[len(extra):]
        if len(a_ref.shape) == 3 and a_ref.shape[0] > 1:
            kp = a_ref.shape[2]
            part = None
            for p in range(a_ref.shape[0]):
                d = lax.dot_general(a_ref[p], b_ref[:, p * kp:(p + 1) * kp], dims, preferred_element_type=F32)
                part = d if part is None else part + d
        else:
            av = a_ref[0] if len(a_ref.shape) == 3 else a_ref[...]
            bv = b_ref[0] if len(b_ref.shape) == 3 else b_ref[...]
            part = lax.dot_general(av, bv, dims, preferred_element_type=F32)

        def write(res):
            if len(o_ref.shape) == 3:
                o_ref[0] = res.astype(out_dtype)
            else:
                o_ref[...] = res.astype(out_dtype)

        if nk == 1:
            write(part)
            return
        acc_ref, = scratch
        k = pl.program_id(2)

        @pl.when(k == 0)
        def _():
            acc_ref[...] = part

        @pl.when(jnp.logical_and(k > 0, k < nk - 1))
        def _():
            acc_ref[...] += part

        @pl.when(k == nk - 1)
        def _():
            write(acc_ref[...] + part)

    acc_shape = o_spec.block_shape[-2:]
    assert all(g >= 1 for g in grid), (name, grid)
    return pl.pallas_call(
        body, name=name, grid=grid, in_specs=[a_spec, b_spec] + [pl.BlockSpec(memory_space=pl.ANY)] * len(extra),
        out_specs=o_spec, out_shape=jax.ShapeDtypeStruct(out_shape, out_dtype),
        scratch_shapes=[pltpu.VMEM(acc_shape, F32)] if nk > 1 else [],
        compiler_params=_cparams(("parallel", "parallel", "arbitrary")),
    )(a, b, *extra)


def _mm_nn(name, a, w, out_dtype, tm, tn, tk, after=None):
    m, k = a.shape
    tm, tk = min(tm, m), min(tk, k)
    n = w.shape[1]
    return _matmul(
        name, a, w, (m, n), out_dtype, (m // tm, n // tn, k // tk),
        pl.BlockSpec((tm, tk), lambda i, j, kk: (i, kk)),
        pl.BlockSpec((tk, tn), lambda i, j, kk: (kk, j)),
        pl.BlockSpec((tm, tn), lambda i, j, kk: (i, j)), nt=False, after=after)


def _mm_nt(name, a, w, out_dtype, tm, tn, tk, after=None):
    m, k = a.shape
    tm = min(tm, m)
    n = w.shape[0]
    return _matmul(
        name, a, w, (m, n), out_dtype, (m // tm, n // tn, k // tk),
        pl.BlockSpec((tm, tk), lambda i, j, kk: (i, kk)),
        pl.BlockSpec((tn, tk), lambda i, j, kk: (j, kk)),
        pl.BlockSpec((tm, tn), lambda i, j, kk: (i, j)), nt=True, after=after)


def _mm_grad_cols(name, at, b, ns, row0, rows, after, tm=1024, tk=4096):
    t = at.shape[1]
    tk = min(tk, t)
    off = row0 // tm
    return _matmul(
        name, at, b, (N_DEV, rows, ns), BF16, (rows // tm, N_DEV, t // tk),
        pl.BlockSpec((tm, tk), lambda i, j, kk: (i + off, kk)),
        pl.BlockSpec((tk, ns), lambda i, j, kk: (kk, j)),
        pl.BlockSpec((1, tm, ns), lambda i, j, kk: (j, i, 0)), nt=False, after=after)


LN_ROWS = 256


def _ln_stats(r):
    mu = jnp.mean(r, axis=-1, keepdims=True)
    xc = r - mu
    var = jnp.mean(xc * xc, axis=-1, keepdims=True)
    rstd = lax.rsqrt(var + LN_EPS)
    return xc * rstd, rstd


def _row_spec(d):
    return pl.BlockSpec((LN_ROWS, d), lambda i: (i, 0))


def _vec_spec(d):
    return pl.BlockSpec((1, d), lambda i: (0, 0))


def _ln_apply(r, g, b):
    xhat, _ = _ln_stats(r)
    return xhat * g + b


def _ln_fwd(name, a, m, g, b, alpha, pre=None):
    t, d = a.shape
    has_m = m is not None
    pre = list(pre) if pre is not None else []

    def body(*refs):
        a_ref, refs = refs[0], refs[1:]
        av = a_ref[...]
        if pre:
            av = _ln_apply(av, refs[0][...], refs[1][...])
            refs = refs[2:]
        if has_m:
            m_ref, g_ref, b_ref, r_ref, yb_ref, yt_ref = refs
            r = alpha * av + m_ref[...]
            r_ref[...] = r
        else:
            g_ref, b_ref, yb_ref, yt_ref = refs
            r = av
        y = _ln_apply(r, g_ref[...], b_ref[...])
        yb_ref[...] = y.astype(BF16)
        yt_ref[...] = y.T.astype(BF16)

    ins = [a] + pre + ([m] if has_m else []) + [g, b]
    in_specs = [_row_spec(d)] + [_vec_spec(d)] * len(pre) + [_row_spec(d)] * has_m + [_vec_spec(d)] * 2
    outs = ([jax.ShapeDtypeStruct((t, d), F32)] if has_m else []) + [
        jax.ShapeDtypeStruct((t, d), BF16), jax.ShapeDtypeStruct((d, t), BF16)]
    res = pl.pallas_call(
        body, name=name, grid=(t // LN_ROWS,), in_specs=in_specs,
        out_specs=[_row_spec(d)] * (len(outs) - 1) + [pl.BlockSpec((d, LN_ROWS), lambda i: (0, i))], out_shape=outs,
        compiler_params=_cparams(("parallel",)),
    )(*ins)
    return res if has_m else (None, *res)


def _ln_bwd_math(r, dy, g):
    xhat, rstd = _ln_stats(r)
    dxhat = dy * g
    m1 = jnp.mean(dxhat, axis=-1, keepdims=True)
    m2 = jnp.mean(dxhat * xhat, axis=-1, keepdims=True)
    dr = rstd * (dxhat - m1 - xhat * m2)
    return dr, jnp.sum(dy * xhat, axis=0, keepdims=True), jnp.sum(dy, axis=0, keepdims=True)


def _ln2_loss_bwd(name, r1, g1, b1, ffn, g, b, tgt):
    t, d = r1.shape

    def body(r1_ref, g1_ref, b1_ref, f_ref, g_ref, b_ref, t_ref, dr_ref, drb_ref, dg_ref, db_ref, loss_ref):
        @pl.when(pl.program_id(0) == 0)
        def _():
            dg_ref[...] = jnp.zeros_like(dg_ref)
            db_ref[...] = jnp.zeros_like(db_ref)
            loss_ref[...] = jnp.zeros_like(loss_ref)

        r = ALPHA * _ln_apply(r1_ref[...], g1_ref[...], b1_ref[...]) + f_ref[...]
        xhat, _ = _ln_stats(r)
        e = xhat * g_ref[...] + b_ref[...] - t_ref[...]
        loss_ref[...] += 0.5 / d * jnp.sum(e * e)
        dr, dg, db = _ln_bwd_math(r, e * (1.0 / d), g_ref[...])
        dr_ref[...] = dr
        drb_ref[...] = dr.astype(BF16)
        dg_ref[...] += dg
        db_ref[...] += db

    return pl.pallas_call(
        body, name=name, grid=(t // LN_ROWS,),
        in_specs=[_row_spec(d), _vec_spec(d), _vec_spec(d), _row_spec(d), _vec_spec(d), _vec_spec(d), _row_spec(d)],
        out_specs=[_row_spec(d), _row_spec(d), _vec_spec(d), _vec_spec(d), _vec_spec(128)],
        out_shape=[jax.ShapeDtypeStruct((t, d), F32), jax.ShapeDtypeStruct((t, d), BF16),
                   jax.ShapeDtypeStruct((1, d), F32), jax.ShapeDtypeStruct((1, d), F32),
                   jax.ShapeDtypeStruct((1, 128), F32)],
        compiler_params=_cparams(("arbitrary",)),
    )(r1, g1, b1, ffn, g, b, tgt)


def _ln_bwd(name, r, dya, dyb, g, alpha, want_bf16):
    t, d = r.shape

    def body(r_ref, dya_ref, dyb_ref, g_ref, *outs):
        dr_ref = outs[0]
        dg_ref, db_ref = outs[-2:]

        @pl.when(pl.program_id(0) == 0)
        def _():
            dg_ref[...] = jnp.zeros_like(dg_ref)
            db_ref[...] = jnp.zeros_like(db_ref)

        dy = alpha * dya_ref[...] + dyb_ref[...]
        dr, dg, db = _ln_bwd_math(r_ref[...], dy, g_ref[...])
        dr_ref[...] = dr
        if want_bf16:
            outs[1][...] = dr.astype(BF16)
        dg_ref[...] += dg
        db_ref[...] += db

    big = [jax.ShapeDtypeStruct((t, d), F32)] + ([jax.ShapeDtypeStruct((t, d), BF16)] if want_bf16 else [])
    return pl.pallas_call(
        body, name=name, grid=(t // LN_ROWS,),
        in_specs=[_row_spec(d)] * 3 + [_vec_spec(d)],
        out_specs=[_row_spec(d)] * len(big) + [_vec_spec(d)] * 2,
        out_shape=big + [jax.ShapeDtypeStruct((1, d), F32)] * 2,
        compiler_params=_cparams(("arbitrary",)),
    )(r, dya, dyb, g)


CONV_ROWS = 64
CONV_UNROLL = 8
FFN_UNROLL = 2


def _unrolled(n, unroll, fn, init):
    def body(i, carry):
        for u in range(unroll):
            carry = fn(i * unroll + u, carry)
        return carry

    return lax.fori_loop(0, n // unroll, body, init)


def _for_shifted(ref, r0, tm, shifts, fn):
    for s in shifts:
        fn(s, ref[pl.ds(r0 + s, tm), :])


def _col_spec(t, cb, off=0):
    return pl.BlockSpec((t, cb), lambda j: (0, j + off))


def _ffn_act_fwd(name, hf, w, b, cb=128):
    t = hf.shape[0]
    f = hf.shape[1] // 2
    nb = f // cb
    tm = CONV_ROWS

    def body(g_ref, v_ref, w_ref, b_ref, act_ref, pad_ref):
        pad_ref[pl.ds(0, 8), :] = jnp.zeros((8, cb), F32)
        pad_ref[pl.ds(8, t), :] = g_ref[...].astype(F32)
        wv = [w_ref[pl.ds(k, 1), :] for k in range(FFN_KERNEL)]
        bias = b_ref[...]

        def tile(i, carry):
            r0 = pl.multiple_of(i * tm, tm)
            acc = [jnp.broadcast_to(bias, (tm, cb))]

            def tap(s, rows):
                acc[0] = acc[0] + wv[s - 6] * rows

            _for_shifted(pad_ref, r0, tm, (6, 7, 8), tap)
            gc = acc[0]
            act_ref[pl.ds(r0, tm), :] = (gc * _sigmoid(gc) * v_ref[pl.ds(r0, tm), :].astype(F32)).astype(BF16)
            return carry

        _unrolled(t // tm, FFN_UNROLL, tile, 0)

    return pl.pallas_call(
        body, name=name, grid=(nb,),
        in_specs=[_col_spec(t, cb), _col_spec(t, cb, nb),
                  pl.BlockSpec((FFN_KERNEL, cb), lambda j: (0, j)), pl.BlockSpec((1, cb), lambda j: (0, j))],
        out_specs=_col_spec(t, cb), out_shape=jax.ShapeDtypeStruct((t, f), BF16),
        scratch_shapes=[pltpu.VMEM((t + 8, cb), F32)],
        compiler_params=_cparams(("parallel",)),
    )(hf, hf, w, b)


def _ffn_act_bwd(name, dact, hf, w, b, cb=128):
    t = hf.shape[0]
    f = hf.shape[1] // 2
    nb = f // cb
    tm = CONV_ROWS

    def body(da_ref, g_ref, v_ref, w_ref, b_ref, dhf_ref, dw_ref, db_ref, pad_ref, dgc_ref):
        pad_ref[pl.ds(0, 8), :] = jnp.zeros((8, cb), F32)
        pad_ref[pl.ds(8, t), :] = g_ref[...].astype(F32)
        dgc_ref[pl.ds(t, 8), :] = jnp.zeros((8, cb), F32)
        wv = [w_ref[pl.ds(k, 1), :] for k in range(FFN_KERNEL)]
        bias = b_ref[...]

        def tile_a(i, carry):
            r0 = pl.multiple_of(i * tm, tm)
            taps = {}
            _for_shifted(pad_ref, r0, tm, (6, 7, 8), lambda s, rows: taps.__setitem__(s, rows))
            gc = bias + wv[0] * taps[6] + wv[1] * taps[7] + wv[2] * taps[8]
            sg = _sigmoid(gc)
            da = da_ref[pl.ds(r0, tm), :].astype(F32)
            dhf_ref[1, pl.ds(r0, tm), :] = (da * gc * sg).astype(BF16)
            dgc = da * v_ref[pl.ds(r0, tm), :].astype(F32) * sg * (1.0 + gc * (1.0 - sg))
            dgc_ref[pl.ds(r0, tm), :] = dgc
            sums = [jnp.sum(dgc * taps[6 + k], axis=0, keepdims=True) for k in range(3)]
            sums.append(jnp.sum(dgc, axis=0, keepdims=True))
            return tuple(c + s for c, s in zip(carry, sums))

        zero = jnp.zeros((1, cb), F32)
        dw0, dw1, dw2, dbias = _unrolled(t // tm, FFN_UNROLL, tile_a, (zero, zero, zero, zero))
        row = lax.broadcasted_iota(jnp.int32, (8, cb), 0)
        dw_ref[...] = jnp.where(row == 0, dw0, jnp.where(row == 1, dw1, jnp.where(row == 2, dw2, 0.0)))
        db_ref[...] = dbias

        def tile_b(i, carry):
            r0 = pl.multiple_of(i * tm, tm)
            acc = [jnp.zeros((tm, cb), F32)]

            def tap(s, rows):
                acc[0] = acc[0] + wv[2 - s] * rows

            _for_shifted(dgc_ref, r0, tm, (0, 1, 2), tap)
            dhf_ref[0, pl.ds(r0, tm), :] = acc[0].astype(BF16)
            return carry

        lax.fori_loop(0, t // tm, tile_b, 0)

    return pl.pallas_call(
        body, name=name, grid=(nb,),
        in_specs=[_col_spec(t, cb), _col_spec(t, cb), _col_spec(t, cb, nb),
                  pl.BlockSpec((FFN_KERNEL, cb), lambda j: (0, j)), pl.BlockSpec((1, cb), lambda j: (0, j))],
        out_specs=[pl.BlockSpec((2, t, cb), lambda j: (0, 0, j)),
                   pl.BlockSpec((8, cb), lambda j: (0, j)), pl.BlockSpec((1, cb), lambda j: (0, j))],
        out_shape=[jax.ShapeDtypeStruct((2, t, f), BF16), jax.ShapeDtypeStruct((8, f), F32),
                   jax.ShapeDtypeStruct((1, f), F32)],
        scratch_shapes=[pltpu.VMEM((t + 8, cb), F32), pltpu.VMEM((t + 8, cb), F32)],
        compiler_params=_cparams(("parallel",)),
    )(dact, hf, hf, w, b)


def _silu_grad(z, sg):
    return sg * (1.0 + z * (1.0 - sg))


def _conv_fwd(name, hin, w, b, ng, nb_, cat):
    t = hin.shape[0]
    c = GROUP
    tm = CONV_ROWS
    pad = 32
    shifts = tuple(2 + k for k in range(CONV_KERNEL))

    def body(a_ref, gt_ref, w_ref, b_ref, ng_ref, nb_ref, cat_ref, u1_ref, u3_ref, pad_ref):
        pad_ref[pl.ds(0, pad), :] = jnp.zeros((pad, c), F32)
        pad_ref[pl.ds(pad, t), :] = a_ref[...] * _sigmoid(gt_ref[...])
        bias, gam, bet = b_ref[...], ng_ref[...], nb_ref[...]

        def tile(i, carry):
            r0 = pl.multiple_of(i * tm, tm)
            acc = [jnp.broadcast_to(bias, (tm, c))]

            def tap(s, rows):
                acc[0] = acc[0] + w_ref[pl.ds(s - 2, 1), :] * rows

            _for_shifted(pad_ref, r0, tm, shifts, tap)
            u1 = acc[0]
            u1_ref[pl.ds(r0, tm), :] = u1
            xhat, _ = _ln_stats(u1)
            u2 = xhat * gam + bet
            u3_ref[pl.ds(r0, tm), :] = (u2 * _sigmoid(u2)).astype(BF16)
            return carry

        _unrolled(t // tm, CONV_UNROLL, tile, 0)

    vec = pl.BlockSpec((1, c), lambda j: (0, j))
    return pl.pallas_call(
        body, name=name, grid=(N_GROUPS,),
        in_specs=[_col_spec(t, c), _col_spec(t, c, N_GROUPS),
                  pl.BlockSpec((CONV_KERNEL, c), lambda j: (0, j)), vec, vec, vec, ANY],
        out_specs=[_col_spec(t, c), _col_spec(t, c)],
        out_shape=[jax.ShapeDtypeStruct((t, CONV_WIDTH), F32), jax.ShapeDtypeStruct(cat.shape, BF16)],
        input_output_aliases={6: 1},
        scratch_shapes=[pltpu.VMEM((t + pad, c), F32)],
        compiler_params=_cparams(("parallel",)),
    )(hin, hin, w, b, ng, nb_, cat)


def _conv_bwd(name, dcat, u1, hin, w, ng, nb_):
    t = hin.shape[0]
    c = GROUP
    tm = CONV_ROWS
    pad = 32
    nk = CONV_KERNEL

    def body(du3_ref, u1_ref, a_ref, gt_ref, w_ref, ng_ref, nb_ref,
             da_ref, dgt_ref, dw_ref, db_ref, dng_ref, dnb_ref, u0_ref, du1_ref, dwp_ref):
        u0_ref[pl.ds(0, pad), :] = jnp.zeros((pad, c), F32)
        u0_ref[pl.ds(pad, t), :] = a_ref[...] * _sigmoid(gt_ref[...])
        du1_ref[pl.ds(t, pad), :] = jnp.zeros((pad, c), F32)
        dwp_ref[...] = jnp.zeros_like(dwp_ref)
        gam, bet = ng_ref[...], nb_ref[...]

        def tile_a(i, carry):
            r0 = pl.multiple_of(i * tm, tm)
            u1 = u1_ref[pl.ds(r0, tm), :]
            xhat, rstd = _ln_stats(u1)
            u2 = xhat * gam + bet
            sg = _sigmoid(u2)
            du2 = du3_ref[pl.ds(r0, tm), :] * _silu_grad(u2, sg)
            dxhat = du2 * gam
            m1 = jnp.mean(dxhat, axis=-1, keepdims=True)
            m2 = jnp.mean(dxhat * xhat, axis=-1, keepdims=True)
            du1 = rstd * (dxhat - m1 - xhat * m2)
            du1_ref[pl.ds(r0, tm), :] = du1
            sums = (jnp.sum(du1, axis=0, keepdims=True), jnp.sum(du2 * xhat, axis=0, keepdims=True),
                    jnp.sum(du2, axis=0, keepdims=True))
            return tuple(x + s for x, s in zip(carry, sums))

        zero = jnp.zeros((1, c), F32)
        dbias, dgam, dbet = _unrolled(t // tm, CONV_UNROLL, tile_a, (zero, zero, zero))
        db_ref[...] = dbias
        dng_ref[...] = dgam
        dnb_ref[...] = dbet

        def tile_b(i, carry):
            r0 = pl.multiple_of(i * tm, tm)
            du1 = du1_ref[pl.ds(r0, tm), :]
            acc = [jnp.zeros((tm, c), F32)]

            def tap_dx(s, rows):
                acc[0] = acc[0] + w_ref[pl.ds(nk - 1 - s, 1), :] * rows

            _for_shifted(du1_ref, r0, tm, tuple(range(nk)), tap_dx)

            def tap_dw(s, rows):
                part = (du1 * rows).reshape(tm // 8, 8, c).sum(axis=0)
                dwp_ref[s - 2] = dwp_ref[s - 2] + part

            _for_shifted(u0_ref, r0, tm, tuple(2 + k for k in range(nk)), tap_dw)
            du0 = acc[0]
            a = a_ref[pl.ds(r0, tm), :]
            sg = _sigmoid(gt_ref[pl.ds(r0, tm), :])
            da_ref[pl.ds(r0, tm), :] = (du0 * sg).astype(BF16)
            dgt_ref[pl.ds(r0, tm), :] = (du0 * a * sg * (1.0 - sg)).astype(BF16)
            return carry

        lax.fori_loop(0, t // tm, tile_b, 0)
        dw_ref[...] = jnp.sum(dwp_ref[...], axis=1)

    vec = pl.BlockSpec((1, c), lambda j: (0, j))
    vshape = jax.ShapeDtypeStruct((1, CONV_WIDTH), F32)
    return pl.pallas_call(
        body, name=name, grid=(N_GROUPS,),
        in_specs=[_col_spec(t, c), _col_spec(t, c), _col_spec(t, c), _col_spec(t, c, N_GROUPS),
                  pl.BlockSpec((nk, c), lambda j: (0, j)), vec, vec],
        out_specs=[_col_spec(t, c), _col_spec(t, c), pl.BlockSpec((32, c), lambda j: (0, j)), vec, vec, vec],
        out_shape=[jax.ShapeDtypeStruct((t, CONV_WIDTH), BF16), jax.ShapeDtypeStruct((t, CONV_WIDTH), BF16),
                   jax.ShapeDtypeStruct((32, CONV_WIDTH), F32), vshape, vshape, vshape],
        scratch_shapes=[pltpu.VMEM((t + pad, c), F32), pltpu.VMEM((t + pad, c), F32),
                        pltpu.VMEM((32, 8, c), F32)],
        compiler_params=_cparams(("parallel",)),
    )(dcat, u1, hin, hin, w, ng, nb_)


LEVELS = (64, 32, 16)
HGRN_UNROLL = 4
HGRN_UNROLL_FWD = 16
NT_DIMS = (((1,), (1,)), ((), ()))
NN_DIMS = (((1,), (0,)), ((), ()))
TN_DIMS = (((0,), (0,)), ((), ()))


def _bdot(a, b, dims):
    return lax.dot_general(a.astype(BF16), b.astype(BF16), dims, preferred_element_type=F32)


def _hdot(a, b):
    return jnp.dot(a, b, precision=lax.Precision.HIGHEST, preferred_element_type=F32)


def _chunk_consts():
    rid = lax.broadcasted_iota(jnp.int32, (CHUNK, GROUP), 0)
    ti = lax.broadcasted_iota(jnp.int32, (CHUNK, CHUNK), 0)
    si = lax.broadcasted_iota(jnp.int32, (CHUNK, CHUNK), 1)
    tri = (si <= ti).astype(F32)
    second = [(rid & (b // 2)) != 0 for b in LEVELS]
    same = [None] + [(ti // b) == (si // b) for b in LEVELS[1:]]
    sub = lax.broadcasted_iota(jnp.int32, (SUB, GROUP), 0)
    return rid, tri, second, same, sub


def _level_refs(cum_ref, rid, base):
    row = lambda i: cum_ref[pl.ds(base + i, 1), :]
    l1 = jnp.broadcast_to(row(31), (CHUNK, GROUP))
    l2 = jnp.where(rid < 32, row(15), row(47))
    l3 = jnp.where(rid < 16, row(7), jnp.where(rid < 32, row(23), jnp.where(rid < 48, row(39), row(55))))
    return l1, l2, l3


def _level_factors(cum, brefs, second):
    out = []
    for bref, sec in zip(brefs, second):
        eq = jnp.where(sec, jnp.exp(jnp.minimum(cum - bref, 0.0)), 0.0)
        ek = jnp.where(sec, 0.0, jnp.exp(jnp.minimum(bref - cum, 0.0)))
        out.append((eq, ek))
    return out


def _gates(q, f, lb):
    sq = _sigmoid(q)
    sf = _sigmoid(f)
    fg = lb + (1.0 - lb) * sf
    return q * sq, sq, sf, fg


def _hgrn_specs(t, nc):
    c = GROUP
    col = lambda off: pl.BlockSpec((t, c), lambda h: (0, h + off))
    hin_specs = [col(16), col(24), col(32), col(40)]
    vec = pl.BlockSpec((1, c), lambda h: (0, h))
    lbs = pl.BlockSpec((2, c), lambda h: (0, h))
    st = pl.BlockSpec((1, nc, c, c), lambda h: (h, 0, 0, 0))
    return col, hin_specs, vec, lbs, st


def _hgrn_fwd(name, hin, lb_logits, hg):
    t = hin.shape[0]
    nc = t // CHUNK
    c = GROUP
    col, hin_specs, vec, lbs, st = _hgrn_specs(t, nc)

    def body(q_ref, f_ref, v_ref, og_ref, lb_ref, hg_ref, o_ref, ob_ref, st_ref,
             s_ref, cum_ref, kk_ref, vc_ref):
        rid, tri, second, same, sub = _chunk_consts()
        lb = _sigmoid(lb_ref[pl.ds(0, 1), :] - lb_ref[pl.ds(1, 1), :])
        gain = hg_ref[...]
        s_ref[...] = jnp.zeros_like(s_ref)

        def chunk(ci, u):
            base = u * CHUNK
            r0 = pl.multiple_of(ci * CHUNK, CHUNK)
            rows = pl.ds(r0, CHUNK)
            qh, _, _, fg = _gates(q_ref[rows, :], f_ref[rows, :], lb)
            v = v_ref[rows, :]
            kk = 1.0 - fg
            cum = _hdot(tri, jnp.log(fg))
            cum_ref[pl.ds(base, CHUNK), :] = cum
            kk_ref[pl.ds(base, CHUNK), :] = kk
            vc_ref[pl.ds(base, CHUNK), :] = v
            sprev = s_ref[...]
            st_ref[0, ci] = sprev
            blast = cum_ref[pl.ds(base + CHUNK - 1, 1), :]
            o = _bdot(qh * jnp.exp(cum), sprev, NT_DIMS)
            s_ref[...] = sprev * jnp.exp(blast) + _bdot(v, kk * jnp.exp(blast - cum), TN_DIMS)
            a = None
            for (eq, ek), msk in zip(_level_factors(cum, _level_refs(cum_ref, rid, base), second), same):
                al = _bdot(qh * eq, kk * ek, NT_DIMS)
                al = al if msk is None else jnp.where(msk, al, 0.0)
                a = al if a is None else a + al
            o = o + _bdot(a, v, NN_DIMS)
            diag = []
            for sb in range(CHUNK // SUB):
                lo = sb * SUB
                qb = qh[lo:lo + SUB]
                cb = cum[lo:lo + SUB]
                od = jnp.zeros((SUB, c), F32)
                for s in range(SUB):
                    e = jnp.where(sub >= s, jnp.exp(jnp.minimum(cb - cum_ref[pl.ds(base + lo + s, 1), :], 0.0)), 0.0)
                    acol = jnp.sum(qb * e * kk_ref[pl.ds(base + lo + s, 1), :], axis=-1, keepdims=True)
                    od = od + acol * vc_ref[pl.ds(base + lo + s, 1), :]
                diag.append(od)
            o = o + jnp.concatenate(diag, axis=0)
            o_ref[rows, :] = o
            y = o * lax.rsqrt(jnp.mean(o * o, axis=-1, keepdims=True) + RMS_EPS) * gain
            og = og_ref[rows, :]
            ob_ref[rows, :] = (y * og * _sigmoid(og)).astype(BF16)

        def chunks(i, carry):
            for u in range(HGRN_UNROLL_FWD):
                chunk(i * HGRN_UNROLL_FWD + u, u)
            return carry

        lax.fori_loop(0, nc // HGRN_UNROLL_FWD, chunks, 0)

    return pl.pallas_call(
        body, name=name, grid=(N_GROUPS,),
        in_specs=hin_specs + [lbs, vec],
        out_specs=[col(0), col(N_GROUPS), st],
        out_shape=[jax.ShapeDtypeStruct((t, HGRN_WIDTH), F32), jax.ShapeDtypeStruct((t, CONV_WIDTH + HGRN_WIDTH), BF16),
                   jax.ShapeDtypeStruct((N_GROUPS, nc, c, c), F32)],
        scratch_shapes=[pltpu.VMEM((c, c), F32)] + [pltpu.VMEM((HGRN_UNROLL_FWD * CHUNK, c), F32)] * 3,
        compiler_params=_cparams(("parallel",)),
    )(hin, hin, hin, hin, lb_logits, hg)


def _hgrn_bwd(name, dcat, hin, o_raw, states, lb_logits, hg):
    t = hin.shape[0]
    nc = t // CHUNK
    c = GROUP
    col, hin_specs, vec, lbs, st = _hgrn_specs(t, nc)

    def body(do_ref, q_ref, f_ref, v_ref, og_ref, o_ref, st_ref, lb_ref, hg_ref,
             dq_ref, df_ref, dv_ref, dog_ref, dhg_ref, dlb_ref,
             ds_ref, cum_ref, kk_ref, vc_ref):
        rid, tri, second, same, sub = _chunk_consts()
        trit = tri.T
        lb = _sigmoid(lb_ref[pl.ds(0, 1), :] - lb_ref[pl.ds(1, 1), :])
        gain = hg_ref[...]
        ds_ref[...] = jnp.zeros_like(ds_ref)

        def chunk(i, carry, u):
            base = u * CHUNK
            dhg, dlb = carry
            ci = nc - 1 - i
            r0 = pl.multiple_of(ci * CHUNK, CHUNK)
            rows = pl.ds(r0, CHUNK)
            q = q_ref[rows, :]
            qh, sq, sf, fg = _gates(q, f_ref[rows, :], lb)
            v = v_ref[rows, :]
            kk = 1.0 - fg
            cum = _hdot(tri, jnp.log(fg))
            cum_ref[pl.ds(base, CHUNK), :] = cum
            kk_ref[pl.ds(base, CHUNK), :] = kk
            vc_ref[pl.ds(base, CHUNK), :] = v
            o = o_ref[rows, :]
            og = og_ref[rows, :]
            sg = _sigmoid(og)
            rinv = lax.rsqrt(jnp.mean(o * o, axis=-1, keepdims=True) + RMS_EPS)
            yn = o * rinv
            dof = do_ref[rows, :]
            dog_ref[rows, :] = (dof * yn * gain * _silu_grad(og, sg)).astype(BF16)
            dz = dof * og * sg
            dhg = dhg + jnp.sum(dz * yn, axis=0, keepdims=True)
            dy = dz * gain
            do = rinv * (dy - yn * jnp.mean(dy * yn, axis=-1, keepdims=True))
            sprev = st_ref[0, ci]
            dsn = ds_ref[...]
            blast = cum_ref[pl.ds(base + CHUNK - 1, 1), :]
            eq0 = jnp.exp(cum)
            ek0 = jnp.exp(blast - cum)
            dqh = _bdot(do, sprev, NN_DIMS) * eq0
            dkk = _bdot(v, dsn, NN_DIMS) * ek0
            dlast = (jnp.sum(kk * dkk, axis=0, keepdims=True)
                     + jnp.exp(blast) * jnp.sum(dsn * sprev, axis=0, keepdims=True))
            dv = _bdot(kk * ek0, dsn, NT_DIMS)
            ds_ref[...] = dsn * jnp.exp(blast) + _bdot(do, qh * eq0, TN_DIMS)
            dg = qh * dqh - kk * dkk
            da = _bdot(do, v, NT_DIMS)
            a = None
            for (eq, ek), msk in zip(_level_factors(cum, _level_refs(cum_ref, rid, base), second), same):
                ql, kl = (qh * eq).astype(BF16), (kk * ek).astype(BF16)
                al = _bdot(ql, kl, NT_DIMS)
                dal = da
                if msk is not None:
                    al = jnp.where(msk, al, 0.0)
                    dal = jnp.where(msk, da, 0.0)
                a = al if a is None else a + al
                dql = _bdot(dal, kl, NN_DIMS)
                dkl = _bdot(dal, ql, TN_DIMS)
                dqh = dqh + dql * eq
                dkk = dkk + dkl * ek
                dg = dg + (ql.astype(F32) * dql - kl.astype(F32) * dkl)
            dv = dv + _bdot(a, do, TN_DIMS)
            dq_d, dk_d, dv_d = [], [], []
            for sb in range(CHUNK // SUB):
                lo = sb * SUB
                qb = qh[lo:lo + SUB]
                cb = cum[lo:lo + SUB]
                dob = do[lo:lo + SUB]
                dqb = jnp.zeros((SUB, c), F32)
                dkb = jnp.zeros((SUB, c), F32)
                dvb = jnp.zeros((SUB, c), F32)
                for s in range(SUB):
                    e = jnp.where(sub >= s, jnp.exp(jnp.minimum(cb - cum_ref[pl.ds(base + lo + s, 1), :], 0.0)), 0.0)
                    ks = kk_ref[pl.ds(base + lo + s, 1), :]
                    qe = qb * e
                    dacol = jnp.sum(dob * vc_ref[pl.ds(base + lo + s, 1), :], axis=-1, keepdims=True)
                    acol = jnp.sum(qe * ks, axis=-1, keepdims=True)
                    dqb = dqb + dacol * (ks * e)
                    dkb = jnp.where(sub == s, jnp.sum(dacol * qe, axis=0, keepdims=True), dkb)
                    dvb = jnp.where(sub == s, jnp.sum(acol * dob, axis=0, keepdims=True), dvb)
                dq_d.append(dqb)
                dk_d.append(dkb)
                dv_d.append(dvb)
            dq_d = jnp.concatenate(dq_d, axis=0)
            dk_d = jnp.concatenate(dk_d, axis=0)
            dqh = dqh + dq_d
            dkk = dkk + dk_d
            dg = dg + (qh * dq_d - kk * dk_d)
            dv = dv + jnp.concatenate(dv_d, axis=0)
            dlf = _hdot(trit, dg) + dlast
            dfg = dlf / fg - dkk
            df_ref[rows, :] = (dfg * (1.0 - lb) * sf * (1.0 - sf)).astype(BF16)
            dlb = dlb + jnp.sum(dfg * (1.0 - sf), axis=0, keepdims=True)
            dq_ref[rows, :] = (dqh * _silu_grad(q, sq)).astype(BF16)
            dv_ref[rows, :] = dv.astype(BF16)
            return dhg, dlb

        def chunks(i, carry):
            for u in range(HGRN_UNROLL):
                carry = chunk(i * HGRN_UNROLL + u, carry, u)
            return carry

        zero = jnp.zeros((1, c), F32)
        dhg, dlb = lax.fori_loop(0, nc // HGRN_UNROLL, chunks, (zero, zero))
        dhg_ref[...] = dhg
        dl0 = dlb * lb * (1.0 - lb)
        dlb_ref[...] = jnp.where(lax.broadcasted_iota(jnp.int32, (2, c), 0) == 0, dl0, -dl0)

    big = jax.ShapeDtypeStruct((t, HGRN_WIDTH), BF16)
    return pl.pallas_call(
        body, name=name, grid=(N_GROUPS,),
        in_specs=[col(8)] + hin_specs + [col(0), st, lbs, vec],
        out_specs=[col(0)] * 4 + [vec, lbs],
        out_shape=[big] * 4 + [jax.ShapeDtypeStruct((1, HGRN_WIDTH), F32), jax.ShapeDtypeStruct((2, HGRN_WIDTH), F32)],
        scratch_shapes=[pltpu.VMEM((c, c), F32)] + [pltpu.VMEM((HGRN_UNROLL * CHUNK, c), F32)] * 3,
        compiler_params=_cparams(("parallel",)),
    )(dcat, hin, hin, hin, hin, o_raw, states, lb_logits, hg)


ANY = pl.BlockSpec(memory_space=pl.ANY)


def _my_place():
    return lax.axis_index("x"), lax.axis_index("y"), lax.axis_index("c")


HBM = pl.BlockSpec(memory_space=pltpu.HBM)
SEM = pl.BlockSpec(memory_space=pltpu.SEMAPHORE)
EFFECT = pltpu.SideEffectType.DATAFLOW_SIDE_EFFECTING


def _peer(k):
    x, y, c = _my_place()
    px = 1 - x if k & 4 else x
    py = 1 - y if k & 2 else y
    pc = 1 - c if k & 1 else c
    return (px, py, pc), 4 * px + 2 * py + pc


def _slot(land_ref, idx):
    if len(land_ref.shape) == 2:
        ns = land_ref.shape[1] // N_DEV
        return land_ref.at[:, pl.ds(pl.multiple_of(idx * ns, 128), ns)]
    return land_ref.at[idx]


def _exchange_copy(k, src_ref, land_ref, send_sems, recv_sems, scatter, landing):
    x, y, c = _my_place()
    me = 4 * x + 2 * y + c
    to, idx = _peer(k)
    return pltpu.make_async_remote_copy(
        src_ref=_slot(src_ref, idx) if scatter else src_ref,
        dst_ref=_slot(land_ref, idx) if landing else _slot(land_ref, me),
        send_sem=send_sems.at[k - 1], recv_sem=recv_sems.at[k - 1], device_id=to, device_id_type=MESH)


ALL_PEERS = tuple(range(1, N_DEV))
NEAR_PEERS = (1, 2, 4, 6)
SAME_CORE_PEERS = (2, 4, 6)


def _exchange_start(name, src, land, scatter, ks=ALL_PEERS):
    def body(src_ref, land_ref, send_sems, recv_sems, src_thru, land_thru, token):
        for k in ks:
            _exchange_copy(k, src_ref, land_ref, send_sems, recv_sems, scatter, landing=False).start()
        token[...] = jnp.zeros_like(token)

    send_sems, recv_sems, src_thru, land_thru, token = pl.pallas_call(
        body, name=name,
        out_shape=(pltpu.SemaphoreType.DMA((N_DEV - 1,)), pltpu.SemaphoreType.DMA((N_DEV - 1,)),
                   pltpu.HBM(src.shape, src.dtype), pltpu.HBM(land.shape, land.dtype),
                   jax.ShapeDtypeStruct((8, 128), F32)),
        in_specs=(HBM, HBM), out_specs=(SEM, SEM, HBM, HBM, pl.BlockSpec(memory_space=pltpu.VMEM)),
        input_output_aliases={0: 2, 1: 3},
        compiler_params=pltpu.CompilerParams(has_side_effects=EFFECT),
    )(pltpu.with_memory_space_constraint(src, pltpu.HBM), pltpu.with_memory_space_constraint(land, pltpu.HBM))
    return (send_sems, recv_sems, src_thru, land_thru, scatter, ks), token


def _exchange_wait(name, handle, after):
    send_sems, recv_sems, src_thru, land_thru, scatter, ks = handle

    def body(src_ref, land_ref, send_sems, recv_sems, after_ref, src_dead, got_ref):
        for k in ks:
            cp = _exchange_copy(k, src_ref, land_ref, send_sems, recv_sems, scatter, landing=True)
            cp.wait_send()
            cp.wait_recv()

    return pl.pallas_call(
        body, name=name,
        out_shape=(pltpu.HBM(src_thru.shape, src_thru.dtype), pltpu.HBM(land_thru.shape, land_thru.dtype)),
        in_specs=(HBM, HBM, SEM, SEM, ANY), out_specs=(HBM, HBM), input_output_aliases={0: 0, 1: 1},
        compiler_params=pltpu.CompilerParams(has_side_effects=EFFECT),
    )(src_thru, land_thru, send_sems, recv_sems, after)[1]


def _relay_copy(j, land_ref, send_sems, recv_sems, landing):
    x, y, c = _my_place()
    k = SAME_CORE_PEERS[j]
    _, sent = _peer(k)
    _, got = _peer(k + 1)
    return pltpu.make_async_remote_copy(
        src_ref=_slot(land_ref, sent), dst_ref=_slot(land_ref, got) if landing else _slot(land_ref, sent),
        send_sem=send_sems.at[j], recv_sem=recv_sems.at[j], device_id=(x, y, 1 - c), device_id_type=MESH)


def _relay_start(name, land):
    n = len(SAME_CORE_PEERS)

    def body(land_ref, send_sems, recv_sems, land_thru, token):
        for j in range(n):
            _relay_copy(j, land_ref, send_sems, recv_sems, landing=False).start()
        token[...] = jnp.zeros_like(token)

    send_sems, recv_sems, land_thru, token = pl.pallas_call(
        body, name=name,
        out_shape=(pltpu.SemaphoreType.DMA((n,)), pltpu.SemaphoreType.DMA((n,)),
                   pltpu.HBM(land.shape, land.dtype), jax.ShapeDtypeStruct((8, 128), F32)),
        in_specs=(HBM,), out_specs=(SEM, SEM, HBM, pl.BlockSpec(memory_space=pltpu.VMEM)),
        input_output_aliases={0: 2},
        compiler_params=pltpu.CompilerParams(has_side_effects=EFFECT),
    )(pltpu.with_memory_space_constraint(land, pltpu.HBM))
    return (send_sems, recv_sems, land_thru), token


def _relay_wait(name, handle, after):
    send_sems, recv_sems, land_thru = handle

    def body(land_ref, send_sems, recv_sems, after_ref, got_ref):
        for j in range(len(SAME_CORE_PEERS)):
            cp = _relay_copy(j, land_ref, send_sems, recv_sems, landing=True)
            cp.wait_send()
            cp.wait_recv()

    return pl.pallas_call(
        body, name=name, out_shape=pltpu.HBM(land_thru.shape, land_thru.dtype),
        in_specs=(HBM, SEM, SEM, ANY), out_specs=HBM, input_output_aliases={0: 0},
        compiler_params=pltpu.CompilerParams(has_side_effects=EFFECT),
    )(land_thru, send_sems, recv_sems, after)


def _own_cols(name, own, me):
    r, ns = own.shape
    tr = 256

    def body(me_ref, own_ref, land_ref):
        land_ref[...] = own_ref[...]

    return pl.pallas_call(
        body, name=name,
        grid_spec=pltpu.PrefetchScalarGridSpec(
            num_scalar_prefetch=1, grid=(r // tr,),
            in_specs=[pl.BlockSpec((tr, ns), lambda i, me_ref: (i, 0))],
            out_specs=pl.BlockSpec((tr, ns), lambda i, me_ref: (i, me_ref[0]))),
        out_shape=jax.ShapeDtypeStruct((r, N_DEV * ns), own.dtype),
    )(jnp.reshape(me, (1,)).astype(jnp.int32), own)


def _own_slot(own, me):
    land = lax.empty((N_DEV,) + own.shape, own.dtype)
    return lax.dynamic_update_slice_in_dim(land, own[None], me, axis=0)


def _adamw_math(w, g, m, v):
    m = ADAM_B1 * m + (1.0 - ADAM_B1) * g
    v = ADAM_B2 * v + (1.0 - ADAM_B2) * (g * g)
    m_hat = m / (1.0 - ADAM_B1 ** ADAM_STEP)
    v_hat = v / (1.0 - ADAM_B2 ** ADAM_STEP)
    delta = -ADAM_LR * (m_hat / (jnp.sqrt(v_hat) + ADAM_EPS) + ADAM_WD * w)
    return delta, m, v


def _adamw_sum(name, recv, w, m, v, tr, row0=0, partial=None):
    r, c = w.shape
    rr = recv.shape[1]
    off = row0 // tr

    def body(recv_ref, w_ref, m_ref, v_ref, *refs):
        g_ref, d_ref, mo_ref, vo_ref = refs[-4:]
        g = recv_ref[0].astype(F32)
        for j in range(1, N_DEV):
            g = g + recv_ref[j].astype(F32)
        g_ref[...] = g
        d_ref[...], mo_ref[...], vo_ref[...] = _adamw_math(w_ref[...], g, m_ref[...], v_ref[...])

    tile = pl.BlockSpec((tr, c), lambda i: (i + off, 0))
    out = jax.ShapeDtypeStruct((r, c), F32)
    prev = list(partial) if partial is not None else []
    return pl.pallas_call(
        body, name=name, grid=(rr // tr,),
        in_specs=[pl.BlockSpec((N_DEV, tr, c), lambda i: (0, i, 0)), tile, tile, tile] + [ANY] * len(prev),
        out_specs=[tile] * 4, out_shape=[out] * 4,
        input_output_aliases={4 + i: i for i in range(len(prev))},
        compiler_params=_cparams(("parallel",)),
    )(recv, w, m, v, *prev)


def _sum_parts(name, parts):
    _, r, c = parts.shape

    def body(p_ref, o_ref):
        acc = p_ref[0]
        for j in range(1, N_DEV):
            acc = acc + p_ref[j]
        o_ref[...] = acc

    return pl.pallas_call(body, name=name, out_shape=jax.ShapeDtypeStruct((r, c), F32),
                          compiler_params=_cparams())(parts)


def _adamw_small(name, w, g, m, v):
    def body(w_ref, g_ref, m_ref, v_ref, d_ref, mo_ref, vo_ref):
        d_ref[...], mo_ref[...], vo_ref[...] = _adamw_math(w_ref[...], g_ref[...], m_ref[...], v_ref[...])

    out = jax.ShapeDtypeStruct(w.shape, F32)
    return pl.pallas_call(body, name=name, out_shape=[out] * 3, compiler_params=_cparams())(w, g, m, v)


def _pack(pieces, rows):
    flat = jnp.concatenate([p.reshape(-1).astype(F32) for p in pieces])
    return jnp.pad(flat, (0, rows * 128 - flat.shape[0])).reshape(rows, 128)


def _unpack(packed, shapes):
    flat = packed.reshape(-1)
    out, off = [], 0
    for s in shapes:
        n = 1
        for d in s:
            n *= d
        out.append(flat[off:off + n].reshape(s))
        off += n
    return out


def kernel(x, emb_ln_g, emb_ln_b, w_in, conv_w, conv_b, conv_norm_g, conv_norm_b, lb_logits, hgrn_norm_g, w_out, ln1_g, ln1_b, w_ffn_up, ffn_conv_w, ffn_conv_b, w_ffn_down, ln2_g, ln2_b, loss_target, m_emb_ln_g, m_emb_ln_b, m_w_in, m_conv_w, m_conv_b, m_conv_norm_g, m_conv_norm_b, m_lb_logits, m_hgrn_norm_g, m_w_out, m_ln1_g, m_ln1_b, m_w_ffn_up, m_ffn_conv_w, m_ffn_conv_b, m_w_ffn_down, m_ln2_g, m_ln2_b, v_emb_ln_g, v_emb_ln_b, v_w_in, v_conv_w, v_conv_b, v_conv_norm_g, v_conv_norm_b, v_lb_logits, v_hgrn_norm_g, v_w_out, v_ln1_g, v_ln1_b, v_w_ffn_up, v_ffn_conv_w, v_ffn_conv_b, v_w_ffn_down, v_ln2_g, v_ln2_b):
    t = x.shape[1]
    me = 4 * lax.axis_index("x") + 2 * lax.axis_index("y") + lax.axis_index("c")
    x2, tgt = x[0], loss_target[0]
    ns_in, ns_up = w_in.shape[2], w_ffn_up.shape[2]
    rs_out, rs_down = w_out.shape[1], w_ffn_down.shape[1]
    cs, fs = conv_w.shape[2], ffn_conv_w.shape[2]

    def gather_start(name, w, prev, ks=ALL_PEERS, cols=False):
        shard = (w[0] + prev).astype(BF16)
        land = _own_cols(name.replace("ag_", "own_"), shard, me) if cols else _own_slot(shard, me)
        return _exchange_start(name, shard, land, scatter=False, ks=ks)

    h_in, tok = gather_start("ag_w_in_start", w_in, 0.0, NEAR_PEERS, cols=True)
    taps = _pack([conv_w[0], ffn_conv_w[0]], 48) + tok[0, 0]
    h_taps, tok = _exchange_start("ag_taps_start", taps, _own_slot(taps, me), scatter=False)
    h_out, tok = gather_start("ag_w_out_start", w_out, tok[0, 0])
    h_up, tok = gather_start("ag_w_up_start", w_ffn_up, tok[0, 0], NEAR_PEERS, cols=True)
    h_down, tok = gather_start("ag_w_down_start", w_ffn_down, tok[0, 0])

    row = lambda a: a.reshape(1, -1)

    _, h0b, h0bt = _ln_fwd("ln_in", x2, None, row(emb_ln_g) + tok[0, 0], row(emb_ln_b), 1.0)
    h_relay, tok_relay = _relay_start("ag_w_in_relay_start", _exchange_wait("ag_w_in_wait", h_in, h0b))
    win_n = _relay_wait("ag_w_in_relay_wait", h_relay, tok_relay)
    hin = _mm_nn("mm_in", h0b, win_n, F32, tm=2048, tn=ns_in, tk=D_MODEL)
    n_cw, n_fw = CONV_KERNEL * cs, FFN_KERNEL * fs
    taps_g = _exchange_wait("ag_taps_wait", h_taps, hin).reshape(N_DEV, -1)
    cw_full = taps_g[:, :n_cw].reshape(N_DEV, CONV_KERNEL, cs).transpose(1, 0, 2).reshape(CONV_KERNEL, CONV_WIDTH)
    fw_full = taps_g[:, n_cw:n_cw + n_fw].reshape(N_DEV, FFN_KERNEL, fs).transpose(1, 0, 2).reshape(FFN_KERNEL, D_FF)

    o_raw, cat_right, states = _hgrn_fwd("hgrn_fwd", hin, lb_logits, hgrn_norm_g)
    u1, catb = _conv_fwd("conv_fwd", hin, cw_full, conv_b, conv_norm_g, conv_norm_b, cat_right)
    wout_g = _exchange_wait("ag_w_out_wait", h_out, catb).reshape(D_MODEL, D_MODEL)
    h_up_relay, tok = _relay_start("ag_w_up_relay_start", _exchange_wait("ag_w_up_wait", h_up, wout_g))
    mix = _mm_nn("mm_out", catb, wout_g, F32, tm=2048, tn=1024, tk=D_MODEL, after=tok)
    r1, h1b, h1bt = _ln_fwd("ln1", x2, mix, ln1_g, ln1_b, ALPHA, pre=(row(emb_ln_g), row(emb_ln_b)))
    wup_n = _relay_wait("ag_w_up_relay_wait", h_up_relay, h1b)
    hf = _mm_nn("mm_up", h1b, wup_n, BF16, tm=1024, tn=1024, tk=D_MODEL)
    actb = _ffn_act_fwd("ffn_act", hf, fw_full, ffn_conv_b)
    wdown_g = _exchange_wait("ag_w_down_wait", h_down, actb).reshape(D_FF, D_MODEL)
    ffn = _mm_nn("mm_down", actb, wdown_g, F32, tm=1024, tn=512, tk=D_FF)
    dr2, dr2b, g_ln2g, g_ln2b, loss = _ln2_loss_bwd("ln2_loss", r1, ln1_g, ln1_b, ffn, ln2_g, ln2_b, tgt)

    def scatter_start(name, parts):
        if parts.ndim == 2:
            ns = parts.shape[1] // N_DEV
            own = lax.dynamic_slice_in_dim(parts, me * ns, ns, axis=1)
        else:
            own = lax.dynamic_index_in_dim(parts, me, axis=0, keepdims=False)
        return _exchange_start(name, parts, _own_slot(own, me), scatter=True)

    dact = _mm_nt("mm_dact", dr2b, wdown_g, BF16, tm=1024, tn=D_FF // 2, tk=D_MODEL)
    gw_down = _matmul(
        "mm_dw_down", actb, dr2b, (D_FF, D_MODEL), BF16, (N_DEV // 2, D_MODEL // 1024, 2),
        pl.BlockSpec((t // 2, 2 * rs_down), lambda i, j, kk: (kk, i)),
        pl.BlockSpec((t // 2, 1024), lambda i, j, kk: (kk, j)),
        pl.BlockSpec((2 * rs_down, 1024), lambda i, j, kk: (i, j)), nt="tn")
    s_down, tok = scatter_start("a2a_w_down_start", gw_down.reshape(N_DEV, rs_down, D_MODEL))
    dhf, g_fw, g_fb = _ffn_act_bwd("ffn_act_bwd", dact, hf, fw_full, ffn_conv_b + tok[0, 0])
    tm = min(1024, t)
    gw_up = _matmul(
        "mm_dw_up", h1bt, dhf, (D_MODEL, 2 * D_FF), BF16, (D_MODEL // 1024, 2 * D_FF // 512, 1),
        pl.BlockSpec((1024, t), lambda i, j, kk: (i, 0)),
        pl.BlockSpec((1, t, 512), lambda i, j, kk: (j // 11, 0, j % 11)),
        pl.BlockSpec((1024, 512), lambda i, j, kk: (i, j)), nt=False)
    s_up, tok = scatter_start("a2a_w_up_start", gw_up)
    tkf = D_FF // 2
    dh1 = _matmul(
        "mm_dh1", dhf, wup_n, (t, D_MODEL), F32, (t // tm, D_MODEL // 1024, 4),
        pl.BlockSpec((1, tm, tkf), lambda i, j, kk: (kk // 2, i, kk % 2)),
        pl.BlockSpec((1024, tkf), lambda i, j, kk: (j, kk)),
        pl.BlockSpec((tm, 1024), lambda i, j, kk: (i, j)), nt=True, after=tok)
    dr1, dr1b, g_ln1g, g_ln1b = _ln_bwd("ln1_bwd", r1, dr2, dh1, ln1_g + tok[0, 0], ALPHA, True)
    gw_out = _matmul(
        "mm_dw_out", catb, dr1b, (D_MODEL, D_MODEL), BF16, (2, 2, 2),
        pl.BlockSpec((t // 2, 1024), lambda i, j, kk: (kk, i)),
        pl.BlockSpec((t // 2, 1024), lambda i, j, kk: (kk, j)),
        pl.BlockSpec((1024, 1024), lambda i, j, kk: (i, j)), nt="tn")
    s_out, tok = scatter_start("a2a_w_out_start", gw_out.reshape(N_DEV, rs_out, D_MODEL))
    dcat = _mm_nt("mm_dcat", dr1b, wout_g, F32, tm=2048, tn=1024, tk=D_MODEL, after=tok)
    da, dgate, g_cw, g_cb, g_cng, g_cnb = _conv_bwd("conv_bwd", dcat, u1, hin, cw_full, conv_norm_g + tok[0, 0],
                                                    conv_norm_b)
    dq, df, di, dog, g_hg, g_lb = _hgrn_bwd("hgrn_bwd", dcat, hin, o_raw, states, lb_logits, hgrn_norm_g)
    dhin = jnp.concatenate([da, dgate, dq, df, di, dog], axis=1)
    half = D_MODEL // 2
    gw_in_a = _mm_grad_cols("mm_dw_in_a", h0bt, dhin, ns_in, 0, half, after=tok)
    s_in_a, tok = scatter_start("a2a_w_in_a_start", gw_in_a)
    gw_in_b = _mm_grad_cols("mm_dw_in_b", h0bt, dhin, ns_in, half, half, after=tok)
    s_in_b, tok = scatter_start("a2a_w_in_b_start", gw_in_b)
    dh0 = _mm_nt("mm_dh0", dhin, win_n, F32, tm=1024, tn=512, tk=IN_PROJ, after=tok)
    grad_x, g_eg, g_eb = _ln_bwd("ln_in_bwd", x2, dr1, dh0, row(emb_ln_g), ALPHA, False)

    small_shapes = [(D_MODEL,), (D_MODEL,), (CONV_KERNEL, CONV_WIDTH), (1, CONV_WIDTH), (1, CONV_WIDTH),
                    (1, CONV_WIDTH), (2, HGRN_WIDTH), (1, HGRN_WIDTH), (1, D_MODEL), (1, D_MODEL),
                    (FFN_KERNEL, D_FF), (1, D_FF), (1, D_MODEL), (1, D_MODEL), (128,)]
    rows_small = 569
    packed = _pack([g_eg, g_eb, g_cw[:CONV_KERNEL], g_cb, g_cng, g_cnb, g_lb, g_hg, g_ln1g, g_ln1b,
                    g_fw[:FFN_KERNEL], g_fb, g_ln2g, g_ln2b, loss], rows_small)
    h_small, tok = _exchange_start("ag_small_start", packed, _own_slot(packed, me), scatter=False)

    def big(name, handle, after, w, m, v, tr):
        recv = _exchange_wait("a2a_" + name + "_wait", handle, after)
        return [o[None] for o in _adamw_sum("adamw_" + name, recv, w[0], m[0], v[0], tr)]

    u_down = big("w_down", s_down, tok, w_ffn_down, m_w_ffn_down, v_w_ffn_down, 64)
    u_up = big("w_up", s_up, u_down[1], w_ffn_up, m_w_ffn_up, v_w_ffn_up, 64)
    u_out = big("w_out", s_out, u_up[1], w_out, m_w_out, v_w_out, 64)
    summed = _sum_parts("sum_small", _exchange_wait("ag_small_wait", h_small, u_out[1]))
    (s_eg, s_eb, s_cw, s_cb, s_cng, s_cnb, s_lb, s_hg, s_l1g, s_l1b, s_fw, s_fb, s_l2g, s_l2b,
     s_loss) = _unpack(summed, small_shapes)
    s_cw = lax.dynamic_slice_in_dim(s_cw, me * cs, cs, axis=1)[None]
    s_fw = lax.dynamic_slice_in_dim(s_fw, me * fs, fs, axis=1)[None]
    g_small = [s_eg, s_eb, s_cw, s_cb, s_cng, s_cnb, s_lb, s_hg, s_l1g, s_l1b, s_fw, s_fb, s_l2g, s_l2b]
    w_small = [emb_ln_g, emb_ln_b, conv_w, conv_b, conv_norm_g, conv_norm_b, lb_logits, hgrn_norm_g,
               ln1_g, ln1_b, ffn_conv_w, ffn_conv_b, ln2_g, ln2_b]
    m_small = [m_emb_ln_g, m_emb_ln_b, m_conv_w, m_conv_b, m_conv_norm_g, m_conv_norm_b, m_lb_logits,
               m_hgrn_norm_g, m_ln1_g, m_ln1_b, m_ffn_conv_w, m_ffn_conv_b, m_ln2_g, m_ln2_b]
    v_small = [v_emb_ln_g, v_emb_ln_b, v_conv_w, v_conv_b, v_conv_norm_g, v_conv_norm_b, v_lb_logits,
               v_hgrn_norm_g, v_ln1_g, v_ln1_b, v_ffn_conv_w, v_ffn_conv_b, v_ln2_g, v_ln2_b]
    rows_own = 236
    shapes_own = [w.shape for w in w_small]
    upd = _adamw_small("adamw_small", _pack(w_small, rows_own), _pack(g_small, rows_own),
                       _pack(m_small, rows_own), _pack(v_small, rows_own))
    d_small, nm_small, nv_small = (_unpack(u, shapes_own) for u in upd)
    g_small = [g.reshape(s) for g, s in zip(g_small, shapes_own)]

    recv_a = _exchange_wait("a2a_w_in_a_wait", s_in_a, upd[0])
    part = _adamw_sum("adamw_w_in_a", recv_a, w_in[0], m_w_in[0], v_w_in[0], 128)
    recv_b = _exchange_wait("a2a_w_in_b_wait", s_in_b, part[1])
    u_in = [o[None] for o in _adamw_sum("adamw_w_in_b", recv_b, w_in[0], m_w_in[0], v_w_in[0], 128,
                                        row0=half, partial=part)]

    def ordered(small, i_in, i_out, i_up, i_down):
        (eg, eb, cw, cb, cng, cnb, lb, hg, l1g, l1b, fw, fb, l2g, l2b) = small
        return [eg, eb, i_in, cw, cb, cng, cnb, lb, hg, i_out, l1g, l1b, i_up, fw, fb, i_down, l2g, l2b]

    outs = [s_loss[0], grad_x[None]]
    for k, small in enumerate([g_small, d_small, nm_small, nv_small]):
        outs += ordered(small, u_in[k], u_out[k], u_up[k], u_down[k])
    return tuple(outs)
```

```python
import functools

import jax
import jax.numpy as jnp
from jax import lax
from jax.experimental import pallas as pl
from jax.experimental.pallas import tpu as pltpu

F32 = jnp.float32
BF16 = jnp.bfloat16

N_DEV = 8
D_MODEL = 2048
CONV_WIDTH = 1024
CONV_KERNEL = 31
HGRN_WIDTH = 1024
GROUP = 128
N_GROUPS = 8
IN_PROJ = 2 * CONV_WIDTH + 4 * HGRN_WIDTH
D_FF = 5632
FFN_KERNEL = 3
CHUNK = 64
SUB = 8
LN_EPS = 1e-5
RMS_EPS = 1e-6
ALPHA = 2.0 ** 0.25
ADAM_LR, ADAM_B1, ADAM_B2, ADAM_EPS, ADAM_WD, ADAM_STEP = 0.001, 0.9, 0.999, 1e-08, 0.01, 10

VMEM_LIMIT = 56 * 1024 * 1024
MESH = pl.DeviceIdType.MESH


def _cparams(sem=None):
    return pltpu.CompilerParams(dimension_semantics=sem, vmem_limit_bytes=VMEM_LIMIT)


def _sigmoid(x):
    return 0.5 * jnp.tanh(0.5 * x) + 0.5


def _matmul(name, a, b, out_shape, out_dtype, grid, a_spec, b_spec, o_spec, nt, after=None):
    nk = grid[2]
    dims = {True: (((1,), (1,)), ((), ())), False: (((1,), (0,)), ((), ())), "tn": (((0,), (0,)), ((), ()))}[nt]
    extra = [] if after is None else [after]

    def body(a_ref, b_ref, *rest):
        o_ref, *scratch = rest[len(extra):]
        if len(a_ref.shape) == 3 and a_ref.shape[0] > 1:
            kp = a_ref.shape[2]
            part = None
            for p in range(a_ref.shape[0]):
                d = lax.dot_general(a_ref[p], b_ref[:, p * kp:(p + 1) * kp], dims, preferred_element_type=F32)
                part = d if part is None else part + d
        else:
            av = a_ref[0] if len(a_ref.shape) == 3 else a_ref[...]
            bv = b_ref[0] if len(b_ref.shape) == 3 else b_ref[...]
            part = lax.dot_general(av, bv, dims, preferred_element_type=F32)

        def write(res):
            if len(o_ref.shape) == 3:
                o_ref[0] = res.astype(out_dtype)
            else:
                o_ref[...] = res.astype(out_dtype)

        if nk == 1:
            write(part)
            return
        acc_ref, = scratch
        k = pl.program_id(2)

        @pl.when(k == 0)
        def _():
            acc_ref[...] = part

        @pl.when(jnp.logical_and(k > 0, k < nk - 1))
        def _():
            acc_ref[...] += part

        @pl.when(k == nk - 1)
        def _():
            write(acc_ref[...] + part)

    acc_shape = o_spec.block_shape[-2:]
    assert all(g >= 1 for g in grid), (name, grid)
    return pl.pallas_call(
        body, name=name, grid=grid, in_specs=[a_spec, b_spec] + [pl.BlockSpec(memory_space=pl.ANY)] * len(extra),
        out_specs=o_spec, out_shape=jax.ShapeDtypeStruct(out_shape, out_dtype),
        scratch_shapes=[pltpu.VMEM(acc_shape, F32)] if nk > 1 else [],
        compiler_params=_cparams(("parallel", "parallel", "arbitrary")),
    )(a, b, *extra)


def _mm_nn(name, a, w, out_dtype, tm, tn, tk, after=None):
    m, k = a.shape
    tm, tk = min(tm, m), min(tk, k)
    n = w.shape[1]
    return _matmul(
        name, a, w, (m, n), out_dtype, (m // tm, n // tn, k // tk),
        pl.BlockSpec((tm, tk), lambda i, j, kk: (i, kk)),
        pl.BlockSpec((tk, tn), lambda i, j, kk: (kk, j)),
        pl.BlockSpec((tm, tn), lambda i, j, kk: (i, j)), nt=False, after=after)


def _mm_nt(name, a, w, out_dtype, tm, tn, tk, after=None):
    m, k = a.shape
    tm = min(tm, m)
    n = w.shape[0]
    return _matmul(
        name, a, w, (m, n), out_dtype, (m // tm, n // tn, k // tk),
        pl.BlockSpec((tm, tk), lambda i, j, kk: (i, kk)),
        pl.BlockSpec((tn, tk), lambda i, j, kk: (j, kk)),
        pl.BlockSpec((tm, tn), lambda i, j, kk: (i, j)), nt=True, after=after)


def _mm_grad_cols(name, at, b, ns, row0, rows, after, tm=1024, tk=4096):
    t = at.shape[1]
    tk = min(tk, t)
    off = row0 // tm
    return _matmul(
        name, at, b, (N_DEV, rows, ns), BF16, (rows // tm, N_DEV, t // tk),
        pl.BlockSpec((tm, tk), lambda i, j, kk: (i + off, kk)),
        pl.BlockSpec((tk, ns), lambda i, j, kk: (kk, j)),
        pl.BlockSpec((1, tm, ns), lambda i, j, kk: (j, i, 0)), nt=False, after=after)


LN_ROWS = 256


def _ln_stats(r):
    mu = jnp.mean(r, axis=-1, keepdims=True)
    xc = r - mu
    var = jnp.mean(xc * xc, axis=-1, keepdims=True)
    rstd = lax.rsqrt(var + LN_EPS)
    return xc * rstd, rstd


def _row_spec(d):
    return pl.BlockSpec((LN_ROWS, d), lambda i: (i, 0))


def _vec_spec(d):
    return pl.BlockSpec((1, d), lambda i: (0, 0))


def _ln_apply(r, g, b):
    xhat, _ = _ln_stats(r)
    return xhat * g + b


def _ln_fwd(name, a, m, g, b, alpha, pre=None):
    t, d = a.shape
    has_m = m is not None
    pre = list(pre) if pre is not None else []

    def body(*refs):
        a_ref, refs = refs[0], refs[1:]
        av = a_ref[...]
        if pre:
            av = _ln_apply(av, refs[0][...], refs[1][...])
            refs = refs[2:]
        if has_m:
            m_ref, g_ref, b_ref, r_ref, yb_ref, yt_ref = refs
            r = alpha * av + m_ref[...]
            r_ref[...] = r
        else:
            g_ref, b_ref, yb_ref, yt_ref = refs
            r = av
        y = _ln_apply(r, g_ref[...], b_ref[...])
        yb_ref[...] = y.astype(BF16)
        yt_ref[...] = y.T.astype(BF16)

    ins = [a] + pre + ([m] if has_m else []) + [g, b]
    in_specs = [_row_spec(d)] + [_vec_spec(d)] * len(pre) + [_row_spec(d)] * has_m + [_vec_spec(d)] * 2
    outs = ([jax.ShapeDtypeStruct((t, d), F32)] if has_m else []) + [
        jax.ShapeDtypeStruct((t, d), BF16), jax.ShapeDtypeStruct((d, t), BF16)]
    res = pl.pallas_call(
        body, name=name, grid=(t // LN_ROWS,), in_specs=in_specs,
        out_specs=[_row_spec(d)] * (len(outs) - 1) + [pl.BlockSpec((d, LN_ROWS), lambda i: (0, i))], out_shape=outs,
        compiler_params=_cparams(("parallel",)),
    )(*ins)
    return res if has_m else (None, *res)


def _ln_bwd_math(r, dy, g):
    xhat, rstd = _ln_stats(r)
    dxhat = dy * g
    m1 = jnp.mean(dxhat, axis=-1, keepdims=True)
    m2 = jnp.mean(dxhat * xhat, axis=-1, keepdims=True)
    dr = rstd * (dxhat - m1 - xhat * m2)
    return dr, jnp.sum(dy * xhat, axis=0, keepdims=True), jnp.sum(dy, axis=0, keepdims=True)


def _ln2_loss_bwd(name, r1, g1, b1, ffn, g, b, tgt):
    t, d = r1.shape

    def body(r1_ref, g1_ref, b1_ref, f_ref, g_ref, b_ref, t_ref, dr_ref, drb_ref, dg_ref, db_ref, loss_ref):
        @pl.when(pl.program_id(0) == 0)
        def _():
            dg_ref[...] = jnp.zeros_like(dg_ref)
            db_ref[...] = jnp.zeros_like(db_ref)
            loss_ref[...] = jnp.zeros_like(loss_ref)

        r = ALPHA * _ln_apply(r1_ref[...], g1_ref[...], b1_ref[...]) + f_ref[...]
        xhat, _ = _ln_stats(r)
        e = xhat * g_ref[...] + b_ref[...] - t_ref[...]
        loss_ref[...] += 0.5 / d * jnp.sum(e * e)
        dr, dg, db = _ln_bwd_math(r, e * (1.0 / d), g_ref[...])
        dr_ref[...] = dr
        drb_ref[...] = dr.astype(BF16)
        dg_ref[...] += dg
        db_ref[...] += db

    return pl.pallas_call(
        body, name=name, grid=(t // LN_ROWS,),
        in_specs=[_row_spec(d), _vec_spec(d), _vec_spec(d), _row_spec(d), _vec_spec(d), _vec_spec(d), _row_spec(d)],
        out_specs=[_row_spec(d), _row_spec(d), _vec_spec(d), _vec_spec(d), _vec_spec(128)],
        out_shape=[jax.ShapeDtypeStruct((t, d), F32), jax.ShapeDtypeStruct((t, d), BF16),
                   jax.ShapeDtypeStruct((1, d), F32), jax.ShapeDtypeStruct((1, d), F32),
                   jax.ShapeDtypeStruct((1, 128), F32)],
        compiler_params=_cparams(("arbitrary",)),
    )(r1, g1, b1, ffn, g, b, tgt)


def _ln_bwd(name, r, dya, dyb, g, alpha, want_bf16):
    t, d = r.shape

    def body(r_ref, dya_ref, dyb_ref, g_ref, *outs):
        dr_ref = outs[0]
        dg_ref, db_ref = outs[-2:]

        @pl.when(pl.program_id(0) == 0)
        def _():
            dg_ref[...] = jnp.zeros_like(dg_ref)
            db_ref[...] = jnp.zeros_like(db_ref)

        dy = alpha * dya_ref[...] + dyb_ref[...]
        dr, dg, db = _ln_bwd_math(r_ref[...], dy, g_ref[...])
        dr_ref[...] = dr
        if want_bf16:
            outs[1][...] = dr.astype(BF16)
        dg_ref[...] += dg
        db_ref[...] += db

    big = [jax.ShapeDtypeStruct((t, d), F32)] + ([jax.ShapeDtypeStruct((t, d), BF16)] if want_bf16 else [])
    return pl.pallas_call(
        body, name=name, grid=(t // LN_ROWS,),
        in_specs=[_row_spec(d)] * 3 + [_vec_spec(d)],
        out_specs=[_row_spec(d)] * len(big) + [_vec_spec(d)] * 2,
        out_shape=big + [jax.ShapeDtypeStruct((1, d), F32)] * 2,
        compiler_params=_cparams(("arbitrary",)),
    )(r, dya, dyb, g)


CONV_ROWS = 64
CONV_UNROLL = 8
FFN_UNROLL = 4


def _unrolled(n, unroll, fn, init):
    def body(i, carry):
        for u in range(unroll):
            carry = fn(i * unroll + u, carry)
        return carry

    return lax.fori_loop(0, n // unroll, body, init)


def _for_shifted(ref, r0, tm, shifts, fn):
    for s in shifts:
        fn(s, ref[pl.ds(r0 + s, tm), :])


def _col_spec(t, cb, off=0):
    return pl.BlockSpec((t, cb), lambda j: (0, j + off))


def _ffn_act_fwd(name, hf, w, b, cb=128):
    t = hf.shape[0]
    f = hf.shape[1] // 2
    nb = f // cb
    tm = CONV_ROWS

    def body(g_ref, v_ref, w_ref, b_ref, act_ref, pad_ref):
        pad_ref[pl.ds(0, 8), :] = jnp.zeros((8, cb), F32)
        pad_ref[pl.ds(8, t), :] = g_ref[...].astype(F32)
        wv = [w_ref[pl.ds(k, 1), :] for k in range(FFN_KERNEL)]
        bias = b_ref[...]

        def tile(i, carry):
            r0 = pl.multiple_of(i * tm, tm)
            acc = [jnp.broadcast_to(bias, (tm, cb))]

            def tap(s, rows):
                acc[0] = acc[0] + wv[s - 6] * rows

            _for_shifted(pad_ref, r0, tm, (6, 7, 8), tap)
            gc = acc[0]
            act_ref[pl.ds(r0, tm), :] = (gc * _sigmoid(gc) * v_ref[pl.ds(r0, tm), :].astype(F32)).astype(BF16)
            return carry

        _unrolled(t // tm, FFN_UNROLL, tile, 0)

    return pl.pallas_call(
        body, name=name, grid=(nb,),
        in_specs=[_col_spec(t, cb), _col_spec(t, cb, nb),
                  pl.BlockSpec((FFN_KERNEL, cb), lambda j: (0, j)), pl.BlockSpec((1, cb), lambda j: (0, j))],
        out_specs=_col_spec(t, cb), out_shape=jax.ShapeDtypeStruct((t, f), BF16),
        scratch_shapes=[pltpu.VMEM((t + 8, cb), F32)],
        compiler_params=_cparams(("parallel",)),
    )(hf, hf, w, b)


def _ffn_act_bwd(name, dact, hf, w, b, cb=128):
    t = hf.shape[0]
    f = hf.shape[1] // 2
    nb = f // cb
    tm = CONV_ROWS

    def body(da_ref, g_ref, v_ref, w_ref, b_ref, dhf_ref, dw_ref, db_ref, pad_ref, dgc_ref):
        pad_ref[pl.ds(0, 8), :] = jnp.zeros((8, cb), F32)
        pad_ref[pl.ds(8, t), :] = g_ref[...].astype(F32)
        dgc_ref[pl.ds(t, 8), :] = jnp.zeros((8, cb), F32)
        wv = [w_ref[pl.ds(k, 1), :] for k in range(FFN_KERNEL)]
        bias = b_ref[...]

        def tile_a(i, carry):
            r0 = pl.multiple_of(i * tm, tm)
            taps = {}
            _for_shifted(pad_ref, r0, tm, (6, 7, 8), lambda s, rows: taps.__setitem__(s, rows))
            gc = bias + wv[0] * taps[6] + wv[1] * taps[7] + wv[2] * taps[8]
            sg = _sigmoid(gc)
            da = da_ref[pl.ds(r0, tm), :].astype(F32)
            dhf_ref[1, pl.ds(r0, tm), :] = (da * gc * sg).astype(BF16)
            dgc = da * v_ref[pl.ds(r0, tm), :].astype(F32) * sg * (1.0 + gc * (1.0 - sg))
            dgc_ref[pl.ds(r0, tm), :] = dgc
            sums = [jnp.sum(dgc * taps[6 + k], axis=0, keepdims=True) for k in range(3)]
            sums.append(jnp.sum(dgc, axis=0, keepdims=True))
            return tuple(c + s for c, s in zip(carry, sums))

        zero = jnp.zeros((1, cb), F32)
        dw0, dw1, dw2, dbias = _unrolled(t // tm, FFN_UNROLL, tile_a, (zero, zero, zero, zero))
        row = lax.broadcasted_iota(jnp.int32, (8, cb), 0)
        dw_ref[...] = jnp.where(row == 0, dw0, jnp.where(row == 1, dw1, jnp.where(row == 2, dw2, 0.0)))
        db_ref[...] = dbias

        def tile_b(i, carry):
            r0 = pl.multiple_of(i * tm, tm)
            acc = [jnp.zeros((tm, cb), F32)]

            def tap(s, rows):
                acc[0] = acc[0] + wv[2 - s] * rows

            _for_shifted(dgc_ref, r0, tm, (0, 1, 2), tap)
            dhf_ref[0, pl.ds(r0, tm), :] = acc[0].astype(BF16)
            return carry

        lax.fori_loop(0, t // tm, tile_b, 0)

    return pl.pallas_call(
        body, name=name, grid=(nb,),
        in_specs=[_col_spec(t, cb), _col_spec(t, cb), _col_spec(t, cb, nb),
                  pl.BlockSpec((FFN_KERNEL, cb), lambda j: (0, j)), pl.BlockSpec((1, cb), lambda j: (0, j))],
        out_specs=[pl.BlockSpec((2, t, cb), lambda j: (0, 0, j)),
                   pl.BlockSpec((8, cb), lambda j: (0, j)), pl.BlockSpec((1, cb), lambda j: (0, j))],
        out_shape=[jax.ShapeDtypeStruct((2, t, f), BF16), jax.ShapeDtypeStruct((8, f), F32),
                   jax.ShapeDtypeStruct((1, f), F32)],
        scratch_shapes=[pltpu.VMEM((t + 8, cb), F32), pltpu.VMEM((t + 8, cb), F32)],
        compiler_params=_cparams(("parallel",)),
    )(dact, hf, hf, w, b)


def _silu_grad(z, sg):
    return sg * (1.0 + z * (1.0 - sg))


def _conv_fwd(name, hin, w, b, ng, nb_, cat):
    t = hin.shape[0]
    c = GROUP
    tm = CONV_ROWS
    pad = 32
    shifts = tuple(2 + k for k in range(CONV_KERNEL))

    def body(a_ref, gt_ref, w_ref, b_ref, ng_ref, nb_ref, cat_ref, u1_ref, u3_ref, pad_ref):
        pad_ref[pl.ds(0, pad), :] = jnp.zeros((pad, c), F32)
        pad_ref[pl.ds(pad, t), :] = a_ref[...] * _sigmoid(gt_ref[...])
        bias, gam, bet = b_ref[...], ng_ref[...], nb_ref[...]

        def tile(i, carry):
            r0 = pl.multiple_of(i * tm, tm)
            acc = [jnp.broadcast_to(bias, (tm, c))]

            def tap(s, rows):
                acc[0] = acc[0] + w_ref[pl.ds(s - 2, 1), :] * rows

            _for_shifted(pad_ref, r0, tm, shifts, tap)
            u1 = acc[0]
            u1_ref[pl.ds(r0, tm), :] = u1
            xhat, _ = _ln_stats(u1)
            u2 = xhat * gam + bet
            u3_ref[pl.ds(r0, tm), :] = (u2 * _sigmoid(u2)).astype(BF16)
            return carry

        _unrolled(t // tm, CONV_UNROLL, tile, 0)

    vec = pl.BlockSpec((1, c), lambda j: (0, j))
    return pl.pallas_call(
        body, name=name, grid=(N_GROUPS,),
        in_specs=[_col_spec(t, c), _col_spec(t, c, N_GROUPS),
                  pl.BlockSpec((CONV_KERNEL, c), lambda j: (0, j)), vec, vec, vec, ANY],
        out_specs=[_col_spec(t, c), _col_spec(t, c)],
        out_shape=[jax.ShapeDtypeStruct((t, CONV_WIDTH), F32), jax.ShapeDtypeStruct(cat.shape, BF16)],
        input_output_aliases={6: 1},
        scratch_shapes=[pltpu.VMEM((t + pad, c), F32)],
        compiler_params=_cparams(("parallel",)),
    )(hin, hin, w, b, ng, nb_, cat)


def _conv_bwd(name, dcat, u1, hin, w, ng, nb_):
    t = hin.shape[0]
    c = GROUP
    tm = CONV_ROWS
    pad = 32
    nk = CONV_KERNEL

    def body(du3_ref, u1_ref, a_ref, gt_ref, w_ref, ng_ref, nb_ref,
             da_ref, dgt_ref, dw_ref, db_ref, dng_ref, dnb_ref, u0_ref, du1_ref, dwp_ref):
        u0_ref[pl.ds(0, pad), :] = jnp.zeros((pad, c), F32)
        u0_ref[pl.ds(pad, t), :] = a_ref[...] * _sigmoid(gt_ref[...])
        du1_ref[pl.ds(t, pad), :] = jnp.zeros((pad, c), F32)
        dwp_ref[...] = jnp.zeros_like(dwp_ref)
        gam, bet = ng_ref[...], nb_ref[...]

        def tile_a(i, carry):
            r0 = pl.multiple_of(i * tm, tm)
            u1 = u1_ref[pl.ds(r0, tm), :]
            xhat, rstd = _ln_stats(u1)
            u2 = xhat * gam + bet
            sg = _sigmoid(u2)
            du2 = du3_ref[pl.ds(r0, tm), :] * _silu_grad(u2, sg)
            dxhat = du2 * gam
            m1 = jnp.mean(dxhat, axis=-1, keepdims=True)
            m2 = jnp.mean(dxhat * xhat, axis=-1, keepdims=True)
            du1 = rstd * (dxhat - m1 - xhat * m2)
            du1_ref[pl.ds(r0, tm), :] = du1
            sums = (jnp.sum(du1, axis=0, keepdims=True), jnp.sum(du2 * xhat, axis=0, keepdims=True),
                    jnp.sum(du2, axis=0, keepdims=True))
            return tuple(x + s for x, s in zip(carry, sums))

        zero = jnp.zeros((1, c), F32)
        dbias, dgam, dbet = _unrolled(t // tm, CONV_UNROLL, tile_a, (zero, zero, zero))
        db_ref[...] = dbias
        dng_ref[...] = dgam
        dnb_ref[...] = dbet

        def tile_b(i, carry):
            r0 = pl.multiple_of(i * tm, tm)
            du1 = du1_ref[pl.ds(r0, tm), :]
            acc = [jnp.zeros((tm, c), F32)]

            def tap_dx(s, rows):
                acc[0] = acc[0] + w_ref[pl.ds(nk - 1 - s, 1), :] * rows

            _for_shifted(du1_ref, r0, tm, tuple(range(nk)), tap_dx)

            def tap_dw(s, rows):
                part = (du1 * rows).reshape(tm // 8, 8, c).sum(axis=0)
                dwp_ref[s - 2] = dwp_ref[s - 2] + part

            _for_shifted(u0_ref, r0, tm, tuple(2 + k for k in range(nk)), tap_dw)
            du0 = acc[0]
            a = a_ref[pl.ds(r0, tm), :]
            sg = _sigmoid(gt_ref[pl.ds(r0, tm), :])
            da_ref[pl.ds(r0, tm), :] = (du0 * sg).astype(BF16)
            dgt_ref[pl.ds(r0, tm), :] = (du0 * a * sg * (1.0 - sg)).astype(BF16)
            return carry

        lax.fori_loop(0, t // tm, tile_b, 0)
        dw_ref[...] = jnp.sum(dwp_ref[...], axis=1)

    vec = pl.BlockSpec((1, c), lambda j: (0, j))
    vshape = jax.ShapeDtypeStruct((1, CONV_WIDTH), F32)
    return pl.pallas_call(
        body, name=name, grid=(N_GROUPS,),
        in_specs=[_col_spec(t, c), _col_spec(t, c), _col_spec(t, c), _col_spec(t, c, N_GROUPS),
                  pl.BlockSpec((nk, c), lambda j: (0, j)), vec, vec],
        out_specs=[_col_spec(t, c), _col_spec(t, c), pl.BlockSpec((32, c), lambda j: (0, j)), vec, vec, vec],
        out_shape=[jax.ShapeDtypeStruct((t, CONV_WIDTH), BF16), jax.ShapeDtypeStruct((t, CONV_WIDTH), BF16),
                   jax.ShapeDtypeStruct((32, CONV_WIDTH), F32), vshape, vshape, vshape],
        scratch_shapes=[pltpu.VMEM((t + pad, c), F32), pltpu.VMEM((t + pad, c), F32),
                        pltpu.VMEM((32, 8, c), F32)],
        compiler_params=_cparams(("parallel",)),
    )(dcat, u1, hin, hin, w, ng, nb_)


LEVELS = (64, 32, 16)
HGRN_UNROLL = 8
HGRN_UNROLL_FWD = 16
NT_DIMS = (((1,), (1,)), ((), ()))
NN_DIMS = (((1,), (0,)), ((), ()))
TN_DIMS = (((0,), (0,)), ((), ()))


def _bdot(a, b, dims):
    return lax.dot_general(a.astype(BF16), b.astype(BF16), dims, preferred_element_type=F32)


def _hdot(a, b):
    return jnp.dot(a, b, precision=lax.Precision.HIGHEST, preferred_element_type=F32)


def _chunk_consts():
    rid = lax.broadcasted_iota(jnp.int32, (CHUNK, GROUP), 0)
    ti = lax.broadcasted_iota(jnp.int32, (CHUNK, CHUNK), 0)
    si = lax.broadcasted_iota(jnp.int32, (CHUNK, CHUNK), 1)
    tri = (si <= ti).astype(F32)
    second = [(rid & (b // 2)) != 0 for b in LEVELS]
    same = [None] + [(ti // b) == (si // b) for b in LEVELS[1:]]
    sub = lax.broadcasted_iota(jnp.int32, (SUB, GROUP), 0)
    return rid, tri, second, same, sub


def _level_refs(cum_ref, rid, base):
    row = lambda i: cum_ref[pl.ds(base + i, 1), :]
    l1 = jnp.broadcast_to(row(31), (CHUNK, GROUP))
    l2 = jnp.where(rid < 32, row(15), row(47))
    l3 = jnp.where(rid < 16, row(7), jnp.where(rid < 32, row(23), jnp.where(rid < 48, row(39), row(55))))
    return l1, l2, l3


def _level_factors(cum, brefs, second):
    out = []
    for bref, sec in zip(brefs, second):
        eq = jnp.where(sec, jnp.exp(jnp.minimum(cum - bref, 0.0)), 0.0)
        ek = jnp.where(sec, 0.0, jnp.exp(jnp.minimum(bref - cum, 0.0)))
        out.append((eq, ek))
    return out


def _gates(q, f, lb):
    sq = _sigmoid(q)
    sf = _sigmoid(f)
    fg = lb + (1.0 - lb) * sf
    return q * sq, sq, sf, fg


def _hgrn_specs(t, nc):
    c = GROUP
    col = lambda off: pl.BlockSpec((t, c), lambda h: (0, h + off))
    hin_specs = [col(16), col(24), col(32), col(40)]
    vec = pl.BlockSpec((1, c), lambda h: (0, h))
    lbs = pl.BlockSpec((2, c), lambda h: (0, h))
    st = pl.BlockSpec((1, nc, c, c), lambda h: (h, 0, 0, 0))
    return col, hin_specs, vec, lbs, st


def _hgrn_fwd(name, hin, lb_logits, hg):
    t = hin.shape[0]
    nc = t // CHUNK
    c = GROUP
    col, hin_specs, vec, lbs, st = _hgrn_specs(t, nc)

    def body(q_ref, f_ref, v_ref, og_ref, lb_ref, hg_ref, o_ref, ob_ref, st_ref,
             s_ref, cum_ref, kk_ref, vc_ref):
        rid, tri, second, same, sub = _chunk_consts()
        lb = _sigmoid(lb_ref[pl.ds(0, 1), :] - lb_ref[pl.ds(1, 1), :])
        gain = hg_ref[...]
        s_ref[...] = jnp.zeros_like(s_ref)

        def chunk(ci, u):
            base = u * CHUNK
            r0 = pl.multiple_of(ci * CHUNK, CHUNK)
            rows = pl.ds(r0, CHUNK)
            qh, _, _, fg = _gates(q_ref[rows, :], f_ref[rows, :], lb)
            v = v_ref[rows, :]
            kk = 1.0 - fg
            cum = _hdot(tri, jnp.log(fg))
            cum_ref[pl.ds(base, CHUNK), :] = cum
            kk_ref[pl.ds(base, CHUNK), :] = kk
            vc_ref[pl.ds(base, CHUNK), :] = v
            sprev = s_ref[...]
            st_ref[0, ci] = sprev
            blast = cum_ref[pl.ds(base + CHUNK - 1, 1), :]
            o = _bdot(qh * jnp.exp(cum), sprev, NT_DIMS)
            s_ref[...] = sprev * jnp.exp(blast) + _bdot(v, kk * jnp.exp(blast - cum), TN_DIMS)
            a = None
            for (eq, ek), msk in zip(_level_factors(cum, _level_refs(cum_ref, rid, base), second), same):
                al = _bdot(qh * eq, kk * ek, NT_DIMS)
                al = al if msk is None else jnp.where(msk, al, 0.0)
                a = al if a is None else a + al
            o = o + _bdot(a, v, NN_DIMS)
            diag = []
            for sb in range(CHUNK // SUB):
                lo = sb * SUB
                qb = qh[lo:lo + SUB]
                cb = cum[lo:lo + SUB]
                od = jnp.zeros((SUB, c), F32)
                for s in range(SUB):
                    e = jnp.where(sub >= s, jnp.exp(jnp.minimum(cb - cum_ref[pl.ds(base + lo + s, 1), :], 0.0)), 0.0)
                    acol = jnp.sum(qb * e * kk_ref[pl.ds(base + lo + s, 1), :], axis=-1, keepdims=True)
                    od = od + acol * vc_ref[pl.ds(base + lo + s, 1), :]
                diag.append(od)
            o = o + jnp.concatenate(diag, axis=0)
            o_ref[rows, :] = o
            y = o * lax.rsqrt(jnp.mean(o * o, axis=-1, keepdims=True) + RMS_EPS) * gain
            og = og_ref[rows, :]
            ob_ref[rows, :] = (y * og * _sigmoid(og)).astype(BF16)

        def chunks(i, carry):
            for u in range(HGRN_UNROLL_FWD):
                chunk(i * HGRN_UNROLL_FWD + u, u)
            return carry

        lax.fori_loop(0, nc // HGRN_UNROLL_FWD, chunks, 0)

    return pl.pallas_call(
        body, name=name, grid=(N_GROUPS,),
        in_specs=hin_specs + [lbs, vec],
        out_specs=[col(0), col(N_GROUPS), st],
        out_shape=[jax.ShapeDtypeStruct((t, HGRN_WIDTH), F32), jax.ShapeDtypeStruct((t, CONV_WIDTH + HGRN_WIDTH), BF16),
                   jax.ShapeDtypeStruct((N_GROUPS, nc, c, c), F32)],
        scratch_shapes=[pltpu.VMEM((c, c), F32)] + [pltpu.VMEM((HGRN_UNROLL_FWD * CHUNK, c), F32)] * 3,
        compiler_params=_cparams(("parallel",)),
    )(hin, hin, hin, hin, lb_logits, hg)


def _hgrn_bwd(name, dcat, hin, o_raw, states, lb_logits, hg):
    t = hin.shape[0]
    nc = t // CHUNK
    c = GROUP
    col, hin_specs, vec, lbs, st = _hgrn_specs(t, nc)

    def body(do_ref, q_ref, f_ref, v_ref, og_ref, o_ref, st_ref, lb_ref, hg_ref,
             dq_ref, df_ref, dv_ref, dog_ref, dhg_ref, dlb_ref,
             ds_ref, cum_ref, kk_ref, vc_ref):
        rid, tri, second, same, sub = _chunk_consts()
        trit = tri.T
        lb = _sigmoid(lb_ref[pl.ds(0, 1), :] - lb_ref[pl.ds(1, 1), :])
        gain = hg_ref[...]
        ds_ref[...] = jnp.zeros_like(ds_ref)

        def chunk(i, carry, u):
            base = u * CHUNK
            dhg, dlb = carry
            ci = nc - 1 - i
            r0 = pl.multiple_of(ci * CHUNK, CHUNK)
            rows = pl.ds(r0, CHUNK)
            q = q_ref[rows, :]
            qh, sq, sf, fg = _gates(q, f_ref[rows, :], lb)
            v = v_ref[rows, :]
            kk = 1.0 - fg
            cum = _hdot(tri, jnp.log(fg))
            cum_ref[pl.ds(base, CHUNK), :] = cum
            kk_ref[pl.ds(base, CHUNK), :] = kk
            vc_ref[pl.ds(base, CHUNK), :] = v
            o = o_ref[rows, :]
            og = og_ref[rows, :]
            sg = _sigmoid(og)
            rinv = lax.rsqrt(jnp.mean(o * o, axis=-1, keepdims=True) + RMS_EPS)
            yn = o * rinv
            dof = do_ref[rows, :]
            dog_ref[rows, :] = (dof * yn * gain * _silu_grad(og, sg)).astype(BF16)
            dz = dof * og * sg
            dhg = dhg + jnp.sum(dz * yn, axis=0, keepdims=True)
            dy = dz * gain
            do = rinv * (dy - yn * jnp.mean(dy * yn, axis=-1, keepdims=True))
            sprev = st_ref[0, ci]
            dsn = ds_ref[...]
            blast = cum_ref[pl.ds(base + CHUNK - 1, 1), :]
            eq0 = jnp.exp(cum)
            ek0 = jnp.exp(blast - cum)
            dqh = _bdot(do, sprev, NN_DIMS) * eq0
            dkk = _bdot(v, dsn, NN_DIMS) * ek0
            dlast = (jnp.sum(kk * dkk, axis=0, keepdims=True)
                     + jnp.exp(blast) * jnp.sum(dsn * sprev, axis=0, keepdims=True))
            dv = _bdot(kk * ek0, dsn, NT_DIMS)
            ds_ref[...] = dsn * jnp.exp(blast) + _bdot(do, qh * eq0, TN_DIMS)
            dg = qh * dqh - kk * dkk
            da = _bdot(do, v, NT_DIMS)
            a = None
            for (eq, ek), msk in zip(_level_factors(cum, _level_refs(cum_ref, rid, base), second), same):
                ql, kl = (qh * eq).astype(BF16), (kk * ek).astype(BF16)
                al = _bdot(ql, kl, NT_DIMS)
                dal = da
                if msk is not None:
                    al = jnp.where(msk, al, 0.0)
                    dal = jnp.where(msk, da, 0.0)
                a = al if a is None else a + al
                dql = _bdot(dal, kl, NN_DIMS)
                dkl = _bdot(dal, ql, TN_DIMS)
                dqh = dqh + dql * eq
                dkk = dkk + dkl * ek
                dg = dg + (ql.astype(F32) * dql - kl.astype(F32) * dkl)
            dv = dv + _bdot(a, do, TN_DIMS)
            dq_d, dk_d, dv_d = [], [], []
            for sb in range(CHUNK // SUB):
                lo = sb * SUB
                qb = qh[lo:lo + SUB]
                cb = cum[lo:lo + SUB]
                dob = do[lo:lo + SUB]
                dqb = jnp.zeros((SUB, c), F32)
                dkb = jnp.zeros((SUB, c), F32)
                dvb = jnp.zeros((SUB, c), F32)
                for s in range(SUB):
                    e = jnp.where(sub >= s, jnp.exp(jnp.minimum(cb - cum_ref[pl.ds(base + lo + s, 1), :], 0.0)), 0.0)
                    ks = kk_ref[pl.ds(base + lo + s, 1), :]
                    qe = qb * e
                    dacol = jnp.sum(dob * vc_ref[pl.ds(base + lo + s, 1), :], axis=-1, keepdims=True)
                    acol = jnp.sum(qe * ks, axis=-1, keepdims=True)
                    dqb = dqb + dacol * (ks * e)
                    dkb = jnp.where(sub == s, jnp.sum(dacol * qe, axis=0, keepdims=True), dkb)
                    dvb = jnp.where(sub == s, jnp.sum(acol * dob, axis=0, keepdims=True), dvb)
                dq_d.append(dqb)
                dk_d.append(dkb)
                dv_d.append(dvb)
            dq_d = jnp.concatenate(dq_d, axis=0)
            dk_d = jnp.concatenate(dk_d, axis=0)
            dqh = dqh + dq_d
            dkk = dkk + dk_d
            dg = dg + (qh * dq_d - kk * dk_d)
            dv = dv + jnp.concatenate(dv_d, axis=0)
            dlf = _hdot(trit, dg) + dlast
            dfg = dlf / fg - dkk
            df_ref[rows, :] = (dfg * (1.0 - lb) * sf * (1.0 - sf)).astype(BF16)
            dlb = dlb + jnp.sum(dfg * (1.0 - sf), axis=0, keepdims=True)
            dq_ref[rows, :] = (dqh * _silu_grad(q, sq)).astype(BF16)
            dv_ref[rows, :] = dv.astype(BF16)
            return dhg, dlb

        def chunks(i, carry):
            for u in range(HGRN_UNROLL):
                carry = chunk(i * HGRN_UNROLL + u, carry, u)
            return carry

        zero = jnp.zeros((1, c), F32)
        dhg, dlb = lax.fori_loop(0, nc // HGRN_UNROLL, chunks, (zero, zero))
        dhg_ref[...] = dhg
        dl0 = dlb * lb * (1.0 - lb)
        dlb_ref[...] = jnp.where(lax.broadcasted_iota(jnp.int32, (2, c), 0) == 0, dl0, -dl0)

    big = jax.ShapeDtypeStruct((t, HGRN_WIDTH), BF16)
    return pl.pallas_call(
        body, name=name, grid=(N_GROUPS,),
        in_specs=[col(8)] + hin_specs + [col(0), st, lbs, vec],
        out_specs=[col(0)] * 4 + [vec, lbs],
        out_shape=[big] * 4 + [jax.ShapeDtypeStruct((1, HGRN_WIDTH), F32), jax.ShapeDtypeStruct((2, HGRN_WIDTH), F32)],
        scratch_shapes=[pltpu.VMEM((c, c), F32)] + [pltpu.VMEM((HGRN_UNROLL * CHUNK, c), F32)] * 3,
        compiler_params=_cparams(("parallel",)),
    )(dcat, hin, hin, hin, hin, o_raw, states, lb_logits, hg)


ANY = pl.BlockSpec(memory_space=pl.ANY)


def _my_place():
    return lax.axis_index("x"), lax.axis_index("y"), lax.axis_index("c")


HBM = pl.BlockSpec(memory_space=pltpu.HBM)
SEM = pl.BlockSpec(memory_space=pltpu.SEMAPHORE)
EFFECT = pltpu.SideEffectType.DATAFLOW_SIDE_EFFECTING


def _peer(k):
    x, y, c = _my_place()
    px = 1 - x if k & 4 else x
    py = 1 - y if k & 2 else y
    pc = 1 - c if k & 1 else c
    return (px, py, pc), 4 * px + 2 * py + pc


def _slot(land_ref, idx):
    if len(land_ref.shape) == 2:
        ns = land_ref.shape[1] // N_DEV
        return land_ref.at[:, pl.ds(pl.multiple_of(idx * ns, 128), ns)]
    return land_ref.at[idx]


def _exchange_copy(k, src_ref, land_ref, send_sems, recv_sems, scatter, landing):
    x, y, c = _my_place()
    me = 4 * x + 2 * y + c
    to, idx = _peer(k)
    return pltpu.make_async_remote_copy(
        src_ref=_slot(src_ref, idx) if scatter else src_ref,
        dst_ref=_slot(land_ref, idx) if landing else _slot(land_ref, me),
        send_sem=send_sems.at[k - 1], recv_sem=recv_sems.at[k - 1], device_id=to, device_id_type=MESH)


ALL_PEERS = tuple(range(1, N_DEV))
NEAR_PEERS = (1, 2, 4, 6)
SAME_CORE_PEERS = (2, 4, 6)


def _exchange_start(name, src, land, scatter, ks=ALL_PEERS):
    def body(src_ref, land_ref, send_sems, recv_sems, src_thru, land_thru, token):
        for k in ks:
            _exchange_copy(k, src_ref, land_ref, send_sems, recv_sems, scatter, landing=False).start()
        token[...] = jnp.zeros_like(token)

    send_sems, recv_sems, src_thru, land_thru, token = pl.pallas_call(
        body, name=name,
        out_shape=(pltpu.SemaphoreType.DMA((N_DEV - 1,)), pltpu.SemaphoreType.DMA((N_DEV - 1,)),
                   pltpu.HBM(src.shape, src.dtype), pltpu.HBM(land.shape, land.dtype),
                   jax.ShapeDtypeStruct((8, 128), F32)),
        in_specs=(HBM, HBM), out_specs=(SEM, SEM, HBM, HBM, pl.BlockSpec(memory_space=pltpu.VMEM)),
        input_output_aliases={0: 2, 1: 3},
        compiler_params=pltpu.CompilerParams(has_side_effects=EFFECT),
    )(pltpu.with_memory_space_constraint(src, pltpu.HBM), pltpu.with_memory_space_constraint(land, pltpu.HBM))
    return (send_sems, recv_sems, src_thru, land_thru, scatter, ks), token


def _exchange_wait(name, handle, after):
    send_sems, recv_sems, src_thru, land_thru, scatter, ks = handle

    def body(src_ref, land_ref, send_sems, recv_sems, after_ref, src_dead, got_ref):
        for k in ks:
            cp = _exchange_copy(k, src_ref, land_ref, send_sems, recv_sems, scatter, landing=True)
            cp.wait_send()
            cp.wait_recv()

    return pl.pallas_call(
        body, name=name,
        out_shape=(pltpu.HBM(src_thru.shape, src_thru.dtype), pltpu.HBM(land_thru.shape, land_thru.dtype)),
        in_specs=(HBM, HBM, SEM, SEM, ANY), out_specs=(HBM, HBM), input_output_aliases={0: 0, 1: 1},
        compiler_params=pltpu.CompilerParams(has_side_effects=EFFECT),
    )(src_thru, land_thru, send_sems, recv_sems, after)[1]


def _relay_copy(j, land_ref, send_sems, recv_sems, landing):
    x, y, c = _my_place()
    k = SAME_CORE_PEERS[j]
    _, sent = _peer(k)
    _, got = _peer(k + 1)
    return pltpu.make_async_remote_copy(
        src_ref=_slot(land_ref, sent), dst_ref=_slot(land_ref, got) if landing else _slot(land_ref, sent),
        send_sem=send_sems.at[j], recv_sem=recv_sems.at[j], device_id=(x, y, 1 - c), device_id_type=MESH)


def _relay_start(name, land):
    n = len(SAME_CORE_PEERS)

    def body(land_ref, send_sems, recv_sems, land_thru, token):
        for j in range(n):
            _relay_copy(j, land_ref, send_sems, recv_sems, landing=False).start()
        token[...] = jnp.zeros_like(token)

    send_sems, recv_sems, land_thru, token = pl.pallas_call(
        body, name=name,
        out_shape=(pltpu.SemaphoreType.DMA((n,)), pltpu.SemaphoreType.DMA((n,)),
                   pltpu.HBM(land.shape, land.dtype), jax.ShapeDtypeStruct((8, 128), F32)),
        in_specs=(HBM,), out_specs=(SEM, SEM, HBM, pl.BlockSpec(memory_space=pltpu.VMEM)),
        input_output_aliases={0: 2},
        compiler_params=pltpu.CompilerParams(has_side_effects=EFFECT),
    )(pltpu.with_memory_space_constraint(land, pltpu.HBM))
    return (send_sems, recv_sems, land_thru), token


def _relay_wait(name, handle, after):
    send_sems, recv_sems, land_thru = handle

    def body(land_ref, send_sems, recv_sems, after_ref, got_ref):
        for j in range(len(SAME_CORE_PEERS)):
            cp = _relay_copy(j, land_ref, send_sems, recv_sems, landing=True)
            cp.wait_send()
            cp.wait_recv()

    return pl.pallas_call(
        body, name=name, out_shape=pltpu.HBM(land_thru.shape, land_thru.dtype),
        in_specs=(HBM, SEM, SEM, ANY), out_specs=HBM, input_output_aliases={0: 0},
        compiler_params=pltpu.CompilerParams(has_side_effects=EFFECT),
    )(land_thru, send_sems, recv_sems, after)


def _own_cols(name, own, me):
    r, ns = own.shape
    tr = 256

    def body(me_ref, own_ref, land_ref):
        land_ref[...] = own_ref[...]

    return pl.pallas_call(
        body, name=name,
        grid_spec=pltpu.PrefetchScalarGridSpec(
            num_scalar_prefetch=1, grid=(r // tr,),
            in_specs=[pl.BlockSpec((tr, ns), lambda i, me_ref: (i, 0))],
            out_specs=pl.BlockSpec((tr, ns), lambda i, me_ref: (i, me_ref[0]))),
        out_shape=jax.ShapeDtypeStruct((r, N_DEV * ns), own.dtype),
    )(jnp.reshape(me, (1,)).astype(jnp.int32), own)


def _own_slot(own, me):
    land = lax.empty((N_DEV,) + own.shape, own.dtype)
    return lax.dynamic_update_slice_in_dim(land, own[None], me, axis=0)


def _adamw_math(w, g, m, v):
    m = ADAM_B1 * m + (1.0 - ADAM_B1) * g
    v = ADAM_B2 * v + (1.0 - ADAM_B2) * (g * g)
    m_hat = m / (1.0 - ADAM_B1 ** ADAM_STEP)
    v_hat = v / (1.0 - ADAM_B2 ** ADAM_STEP)
    delta = -ADAM_LR * (m_hat / (jnp.sqrt(v_hat) + ADAM_EPS) + ADAM_WD * w)
    return delta, m, v


def _adamw_sum(name, recv, w, m, v, tr, row0=0, partial=None):
    r, c = w.shape
    rr = recv.shape[1]
    off = row0 // tr

    def body(recv_ref, w_ref, m_ref, v_ref, *refs):
        g_ref, d_ref, mo_ref, vo_ref = refs[-4:]
        g = recv_ref[0].astype(F32)
        for j in range(1, N_DEV):
            g = g + recv_ref[j].astype(F32)
        g_ref[...] = g
        d_ref[...], mo_ref[...], vo_ref[...] = _adamw_math(w_ref[...], g, m_ref[...], v_ref[...])

    tile = pl.BlockSpec((tr, c), lambda i: (i + off, 0))
    out = jax.ShapeDtypeStruct((r, c), F32)
    prev = list(partial) if partial is not None else []
    return pl.pallas_call(
        body, name=name, grid=(rr // tr,),
        in_specs=[pl.BlockSpec((N_DEV, tr, c), lambda i: (0, i, 0)), tile, tile, tile] + [ANY] * len(prev),
        out_specs=[tile] * 4, out_shape=[out] * 4,
        input_output_aliases={4 + i: i for i in range(len(prev))},
        compiler_params=_cparams(("parallel",)),
    )(recv, w, m, v, *prev)


def _sum_parts(name, parts):
    _, r, c = parts.shape

    def body(p_ref, o_ref):
        acc = p_ref[0]
        for j in range(1, N_DEV):
            acc = acc + p_ref[j]
        o_ref[...] = acc

    return pl.pallas_call(body, name=name, out_shape=jax.ShapeDtypeStruct((r, c), F32),
                          compiler_params=_cparams())(parts)


def _adamw_small(name, w, g, m, v):
    def body(w_ref, g_ref, m_ref, v_ref, d_ref, mo_ref, vo_ref):
        d_ref[...], mo_ref[...], vo_ref[...] = _adamw_math(w_ref[...], g_ref[...], m_ref[...], v_ref[...])

    out = jax.ShapeDtypeStruct(w.shape, F32)
    return pl.pallas_call(body, name=name, out_shape=[out] * 3, compiler_params=_cparams())(w, g, m, v)


def _pack(pieces, rows):
    flat = jnp.concatenate([p.reshape(-1).astype(F32) for p in pieces])
    return jnp.pad(flat, (0, rows * 128 - flat.shape[0])).reshape(rows, 128)


def _unpack(packed, shapes):
    flat = packed.reshape(-1)
    out, off = [], 0
    for s in shapes:
        n = 1
        for d in s:
            n *= d
        out.append(flat[off:off + n].reshape(s))
        off += n
    return out


def kernel(x, emb_ln_g, emb_ln_b, w_in, conv_w, conv_b, conv_norm_g, conv_norm_b, lb_logits, hgrn_norm_g, w_out, ln1_g, ln1_b, w_ffn_up, ffn_conv_w, ffn_conv_b, w_ffn_down, ln2_g, ln2_b, loss_target, m_emb_ln_g, m_emb_ln_b, m_w_in, m_conv_w, m_conv_b, m_conv_norm_g, m_conv_norm_b, m_lb_logits, m_hgrn_norm_g, m_w_out, m_ln1_g, m_ln1_b, m_w_ffn_up, m_ffn_conv_w, m_ffn_conv_b, m_w_ffn_down, m_ln2_g, m_ln2_b, v_emb_ln_g, v_emb_ln_b, v_w_in, v_conv_w, v_conv_b, v_conv_norm_g, v_conv_norm_b, v_lb_logits, v_hgrn_norm_g, v_w_out, v_ln1_g, v_ln1_b, v_w_ffn_up, v_ffn_conv_w, v_ffn_conv_b, v_w_ffn_down, v_ln2_g, v_ln2_b):
    t = x.shape[1]
    me = 4 * lax.axis_index("x") + 2 * lax.axis_index("y") + lax.axis_index("c")
    x2, tgt = x[0], loss_target[0]
    ns_in, ns_up = w_in.shape[2], w_ffn_up.shape[2]
    rs_out, rs_down = w_out.shape[1], w_ffn_down.shape[1]
    cs, fs = conv_w.shape[2], ffn_conv_w.shape[2]

    def gather_start(name, w, prev, ks=ALL_PEERS, cols=False):
        shard = (w[0] + prev).astype(BF16)
        land = _own_cols(name.replace("ag_", "own_"), shard, me) if cols else _own_slot(shard, me)
        return _exchange_start(name, shard, land, scatter=False, ks=ks)

    h_in, tok = gather_start("ag_w_in_start", w_in, 0.0, NEAR_PEERS, cols=True)
    taps = _pack([conv_w[0], ffn_conv_w[0]], 48) + tok[0, 0]
    h_taps, tok = _exchange_start("ag_taps_start", taps, _own_slot(taps, me), scatter=False)
    h_out, tok = gather_start("ag_w_out_start", w_out, tok[0, 0])
    h_up, tok = gather_start("ag_w_up_start", w_ffn_up, tok[0, 0], NEAR_PEERS, cols=True)
    h_down, tok = gather_start("ag_w_down_start", w_ffn_down, tok[0, 0])

    row = lambda a: a.reshape(1, -1)

    _, h0b, h0bt = _ln_fwd("ln_in", x2, None, row(emb_ln_g) + tok[0, 0], row(emb_ln_b), 1.0)
    h_relay, tok_relay = _relay_start("ag_w_in_relay_start", _exchange_wait("ag_w_in_wait", h_in, h0b))
    win_n = _relay_wait("ag_w_in_relay_wait", h_relay, tok_relay)
    hin = _mm_nn("mm_in", h0b, win_n, F32, tm=2048, tn=ns_in, tk=D_MODEL)
    n_cw, n_fw = CONV_KERNEL * cs, FFN_KERNEL * fs
    taps_g = _exchange_wait("ag_taps_wait", h_taps, hin).reshape(N_DEV, -1)
    cw_full = taps_g[:, :n_cw].reshape(N_DEV, CONV_KERNEL, cs).transpose(1, 0, 2).reshape(CONV_KERNEL, CONV_WIDTH)
    fw_full = taps_g[:, n_cw:n_cw + n_fw].reshape(N_DEV, FFN_KERNEL, fs).transpose(1, 0, 2).reshape(FFN_KERNEL, D_FF)

    o_raw, cat_right, states = _hgrn_fwd("hgrn_fwd", hin, lb_logits, hgrn_norm_g)
    u1, catb = _conv_fwd("conv_fwd", hin, cw_full, conv_b, conv_norm_g, conv_norm_b, cat_right)
    wout_g = _exchange_wait("ag_w_out_wait", h_out, catb).reshape(D_MODEL, D_MODEL)
    h_up_relay, tok = _relay_start("ag_w_up_relay_start", _exchange_wait("ag_w_up_wait", h_up, wout_g))
    mix = _mm_nn("mm_out", catb, wout_g, F32, tm=2048, tn=1024, tk=D_MODEL, after=tok)
    r1, h1b, h1bt = _ln_fwd("ln1", x2, mix, ln1_g, ln1_b, ALPHA, pre=(row(emb_ln_g), row(emb_ln_b)))
    wup_n = _relay_wait("ag_w_up_relay_wait", h_up_relay, h1b)
    hf = _mm_nn("mm_up", h1b, wup_n, BF16, tm=1024, tn=1024, tk=D_MODEL)
    actb = _ffn_act_fwd("ffn_act", hf, fw_full, ffn_conv_b)
    wdown_g = _exchange_wait("ag_w_down_wait", h_down, actb).reshape(D_FF, D_MODEL)
    ffn = _mm_nn("mm_down", actb, wdown_g, F32, tm=1024, tn=512, tk=D_FF)
    dr2, dr2b, g_ln2g, g_ln2b, loss = _ln2_loss_bwd("ln2_loss", r1, ln1_g, ln1_b, ffn, ln2_g, ln2_b, tgt)

    def scatter_start(name, parts):
        if parts.ndim == 2:
            ns = parts.shape[1] // N_DEV
            own = lax.dynamic_slice_in_dim(parts, me * ns, ns, axis=1)
        else:
            own = lax.dynamic_index_in_dim(parts, me, axis=0, keepdims=False)
        return _exchange_start(name, parts, _own_slot(own, me), scatter=True)

    dact = _mm_nt("mm_dact", dr2b, wdown_g, BF16, tm=1024, tn=D_FF // 2, tk=D_MODEL)
    gw_down = _matmul(
        "mm_dw_down", actb, dr2b, (D_FF, D_MODEL), BF16, (N_DEV // 2, D_MODEL // 1024, 2),
        pl.BlockSpec((t // 2, 2 * rs_down), lambda i, j, kk: (kk, i)),
        pl.BlockSpec((t // 2, 1024), lambda i, j, kk: (kk, j)),
        pl.BlockSpec((2 * rs_down, 1024), lambda i, j, kk: (i, j)), nt="tn")
    s_down, tok = scatter_start("a2a_w_down_start", gw_down.reshape(N_DEV, rs_down, D_MODEL))
    dhf, g_fw, g_fb = _ffn_act_bwd("ffn_act_bwd", dact, hf, fw_full, ffn_conv_b + tok[0, 0])
    tm = min(1024, t)
    gw_up = _matmul(
        "mm_dw_up", h1bt, dhf, (D_MODEL, 2 * D_FF), BF16, (D_MODEL // 1024, 2 * D_FF // 512, 1),
        pl.BlockSpec((1024, t), lambda i, j, kk: (i, 0)),
        pl.BlockSpec((1, t, 512), lambda i, j, kk: (j // 11, 0, j % 11)),
        pl.BlockSpec((1024, 512), lambda i, j, kk: (i, j)), nt=False)
    s_up, tok = scatter_start("a2a_w_up_start", gw_up)
    tkf = D_FF // 2
    dh1 = _matmul(
        "mm_dh1", dhf, wup_n, (t, D_MODEL), F32, (t // tm, D_MODEL // 1024, 4),
        pl.BlockSpec((1, tm, tkf), lambda i, j, kk: (kk // 2, i, kk % 2)),
        pl.BlockSpec((1024, tkf), lambda i, j, kk: (j, kk)),
        pl.BlockSpec((tm, 1024), lambda i, j, kk: (i, j)), nt=True, after=tok)
    dr1, dr1b, g_ln1g, g_ln1b = _ln_bwd("ln1_bwd", r1, dr2, dh1, ln1_g + tok[0, 0], ALPHA, True)
    gw_out = _matmul(
        "mm_dw_out", catb, dr1b, (D_MODEL, D_MODEL), BF16, (2, 2, 2),
        pl.BlockSpec((t // 2, 1024), lambda i, j, kk: (kk, i)),
        pl.BlockSpec((t // 2, 1024), lambda i, j, kk: (kk, j)),
        pl.BlockSpec((1024, 1024), lambda i, j, kk: (i, j)), nt="tn")
    s_out, tok = scatter_start("a2a_w_out_start", gw_out.reshape(N_DEV, rs_out, D_MODEL))
    dcat = _mm_nt("mm_dcat", dr1b, wout_g, F32, tm=2048, tn=1024, tk=D_MODEL, after=tok)
    da, dgate, g_cw, g_cb, g_cng, g_cnb = _conv_bwd("conv_bwd", dcat, u1, hin, cw_full, conv_norm_g + tok[0, 0],
                                                    conv_norm_b)
    dq, df, di, dog, g_hg, g_lb = _hgrn_bwd("hgrn_bwd", dcat, hin, o_raw, states, lb_logits, hgrn_norm_g)
    dhin = jnp.concatenate([da, dgate, dq, df, di, dog], axis=1)
    half = D_MODEL // 2
    gw_in_a = _mm_grad_cols("mm_dw_in_a", h0bt, dhin, ns_in, 0, half, after=tok)
    s_in_a, tok = scatter_start("a2a_w_in_a_start", gw_in_a)
    gw_in_b = _mm_grad_cols("mm_dw_in_b", h0bt, dhin, ns_in, half, half, after=tok)
    s_in_b, tok = scatter_start("a2a_w_in_b_start", gw_in_b)
    dh0 = _mm_nt("mm_dh0", dhin, win_n, F32, tm=1024, tn=512, tk=IN_PROJ, after=tok)
    grad_x, g_eg, g_eb = _ln_bwd("ln_in_bwd", x2, dr1, dh0, row(emb_ln_g), ALPHA, False)

    small_shapes = [(D_MODEL,), (D_MODEL,), (CONV_KERNEL, CONV_WIDTH), (1, CONV_WIDTH), (1, CONV_WIDTH),
                    (1, CONV_WIDTH), (2, HGRN_WIDTH), (1, HGRN_WIDTH), (1, D_MODEL), (1, D_MODEL),
                    (FFN_KERNEL, D_FF), (1, D_FF), (1, D_MODEL), (1, D_MODEL), (128,)]
    rows_small = 569
    packed = _pack([g_eg, g_eb, g_cw[:CONV_KERNEL], g_cb, g_cng, g_cnb, g_lb, g_hg, g_ln1g, g_ln1b,
                    g_fw[:FFN_KERNEL], g_fb, g_ln2g, g_ln2b, loss], rows_small)
    h_small, tok = _exchange_start("ag_small_start", packed, _own_slot(packed, me), scatter=False)

    def big(name, handle, after, w, m, v, tr):
        recv = _exchange_wait("a2a_" + name + "_wait", handle, after)
        return [o[None] for o in _adamw_sum("adamw_" + name, recv, w[0], m[0], v[0], tr)]

    u_down = big("w_down", s_down, tok, w_ffn_down, m_w_ffn_down, v_w_ffn_down, 64)
    u_up = big("w_up", s_up, u_down[1], w_ffn_up, m_w_ffn_up, v_w_ffn_up, 64)
    u_out = big("w_out", s_out, u_up[1], w_out, m_w_out, v_w_out, 64)
    summed = _sum_parts("sum_small", _exchange_wait("ag_small_wait", h_small, u_out[1]))
    (s_eg, s_eb, s_cw, s_cb, s_cng, s_cnb, s_lb, s_hg, s_l1g, s_l1b, s_fw, s_fb, s_l2g, s_l2b,
     s_loss) = _unpack(summed, small_shapes)
    s_cw = lax.dynamic_slice_in_dim(s_cw, me * cs, cs, axis=1)[None]
    s_fw = lax.dynamic_slice_in_dim(s_fw, me * fs, fs, axis=1)[None]
    g_small = [s_eg, s_eb, s_cw, s_cb, s_cng, s_cnb, s_lb, s_hg, s_l1g, s_l1b, s_fw, s_fb, s_l2g, s_l2b]
    w_small = [emb_ln_g, emb_ln_b, conv_w, conv_b, conv_norm_g, conv_norm_b, lb_logits, hgrn_norm_g,
               ln1_g, ln1_b, ffn_conv_w, ffn_conv_b, ln2_g, ln2_b]
    m_small = [m_emb_ln_g, m_emb_ln_b, m_conv_w, m_conv_b, m_conv_norm_g, m_conv_norm_b, m_lb_logits,
               m_hgrn_norm_g, m_ln1_g, m_ln1_b, m_ffn_conv_w, m_ffn_conv_b, m_ln2_g, m_ln2_b]
    v_small = [v_emb_ln_g, v_emb_ln_b, v_conv_w, v_conv_b, v_conv_norm_g, v_conv_norm_b, v_lb_logits,
               v_hgrn_norm_g, v_ln1_g, v_ln1_b, v_ffn_conv_w, v_ffn_conv_b, v_ln2_g, v_ln2_b]
    rows_own = 236
    shapes_own = [w.shape for w in w_small]
    upd = _adamw_small("adamw_small", _pack(w_small, rows_own), _pack(g_small, rows_own),
                       _pack(m_small, rows_own), _pack(v_small, rows_own))
    d_small, nm_small, nv_small = (_unpack(u, shapes_own) for u in upd)
    g_small = [g.reshape(s) for g, s in zip(g_small, shapes_own)]

    recv_a = _exchange_wait("a2a_w_in_a_wait", s_in_a, upd[0])
    part = _adamw_sum("adamw_w_in_a", recv_a, w_in[0], m_w_in[0], v_w_in[0], 128)
    recv_b = _exchange_wait("a2a_w_in_b_wait", s_in_b, part[1])
    u_in = [o[None] for o in _adamw_sum("adamw_w_in_b", recv_b, w_in[0], m_w_in[0], v_w_in[0], 128,
                                        row0=half, partial=part)]

    def ordered(small, i_in, i_out, i_up, i_down):
        (eg, eb, cw, cb, cng, cnb, lb, hg, l1g, l1b, fw, fb, l2g, l2b) = small
        return [eg, eb, i_in, cw, cb, cng, cnb, lb, hg, i_out, l1g, l1b, i_up, fw, fb, i_down, l2g, l2b]

    outs = [s_loss[0], grad_x[None]]
    for k, small in enumerate([g_small, d_small, nm_small, nv_small]):
        outs += ordered(small, u_in[k], u_out[k], u_up[k], u_down[k])
    return tuple(outs)
```

```python
import functools

import jax
import jax.numpy as jnp
from jax import lax
from jax.experimental import pallas as pl
from jax.experimental.pallas import tpu as pltpu

F32 = jnp.float32
BF16 = jnp.bfloat16

N_DEV = 8
D_MODEL = 2048
CONV_WIDTH = 1024
CONV_KERNEL = 31
HGRN_WIDTH = 1024
GROUP = 128
N_GROUPS = 8
IN_PROJ = 2 * CONV_WIDTH + 4 * HGRN_WIDTH
D_FF = 5632
FFN_KERNEL = 3
CHUNK = 64
SUB = 8
LN_EPS = 1e-5
RMS_EPS = 1e-6
ALPHA = 2.0 ** 0.25
ADAM_LR, ADAM_B1, ADAM_B2, ADAM_EPS, ADAM_WD, ADAM_STEP = 0.001, 0.9, 0.999, 1e-08, 0.01, 10

VMEM_LIMIT = 56 * 1024 * 1024
MESH = pl.DeviceIdType.MESH


def _cparams(sem=None):
    return pltpu.CompilerParams(dimension_semantics=sem, vmem_limit_bytes=VMEM_LIMIT)


def _sigmoid(x):
    return 0.5 * jnp.tanh(0.5 * x) + 0.5


def _matmul(name, a, b, out_shape, out_dtype, grid, a_spec, b_spec, o_spec, nt, after=None):
    nk = grid[2]
    dims = {True: (((1,), (1,)), ((), ())), False: (((1,), (0,)), ((), ())), "tn": (((0,), (0,)), ((), ()))}[nt]
    extra = [] if after is None else [after]

    def body(a_ref, b_ref, *rest):
        o_ref, *scratch = rest[len(extra):]
        if len(a_ref.shape) == 3 and a_ref.shape[0] > 1:
            kp = a_ref.shape[2]
            part = None
            for p in range(a_ref.shape[0]):
                d = lax.dot_general(a_ref[p], b_ref[:, p * kp:(p + 1) * kp], dims, preferred_element_type=F32)
                part = d if part is None else part + d
        else:
            av = a_ref[0] if len(a_ref.shape) == 3 else a_ref[...]
            bv = b_ref[0] if len(b_ref.shape) == 3 else b_ref[...]
            part = lax.dot_general(av, bv, dims, preferred_element_type=F32)

        def write(res):
            if len(o_ref.shape) == 3:
                o_ref[0] = res.astype(out_dtype)
            else:
                o_ref[...] = res.astype(out_dtype)

        if nk == 1:
            write(part)
            return
        acc_ref, = scratch
        k = pl.program_id(2)

        @pl.when(k == 0)
        def _():
            acc_ref[...] = part

        @pl.when(jnp.logical_and(k > 0, k < nk - 1))
        def _():
            acc_ref[...] += part

        @pl.when(k == nk - 1)
        def _():
            write(acc_ref[...] + part)

    acc_shape = o_spec.block_shape[-2:]
    assert all(g >= 1 for g in grid), (name, grid)
    return pl.pallas_call(
        body, name=name, grid=grid, in_specs=[a_spec, b_spec] + [pl.BlockSpec(memory_space=pl.ANY)] * len(extra),
        out_specs=o_spec, out_shape=jax.ShapeDtypeStruct(out_shape, out_dtype),
        scratch_shapes=[pltpu.VMEM(acc_shape, F32)] if nk > 1 else [],
        compiler_params=_cparams(("parallel", "parallel", "arbitrary")),
    )(a, b, *extra)


def _mm_nn(name, a, w, out_dtype, tm, tn, tk, after=None):
    m, k = a.shape
    tm, tk = min(tm, m), min(tk, k)
    n = w.shape[1]
    return _matmul(
        name, a, w, (m, n), out_dtype, (m // tm, n // tn, k // tk),
        pl.BlockSpec((tm, tk), lambda i, j, kk: (i, kk)),
        pl.BlockSpec((tk, tn), lambda i, j, kk: (kk, j)),
        pl.BlockSpec((tm, tn), lambda i, j, kk: (i, j)), nt=False, after=after)


def _mm_nt(name, a, w, out_dtype, tm, tn, tk, after=None):
    m, k = a.shape
    tm = min(tm, m)
    n = w.shape[0]
    return _matmul(
        name, a, w, (m, n), out_dtype, (m // tm, n // tn, k // tk),
        pl.BlockSpec((tm, tk), lambda i, j, kk: (i, kk)),
        pl.BlockSpec((tn, tk), lambda i, j, kk: (j, kk)),
        pl.BlockSpec((tm, tn), lambda i, j, kk: (i, j)), nt=True, after=after)


def _mm_grad_cols(name, at, b, ns, row0, rows, after, tm=1024, tk=4096):
    t = at.shape[1]
    tk = min(tk, t)
    off = row0 // tm
    return _matmul(
        name, at, b, (N_DEV, rows, ns), BF16, (rows // tm, N_DEV, t // tk),
        pl.BlockSpec((tm, tk), lambda i, j, kk: (i + off, kk)),
        pl.BlockSpec((tk, ns), lambda i, j, kk: (kk, j)),
        pl.BlockSpec((1, tm, ns), lambda i, j, kk: (j, i, 0)), nt=False, after=after)


LN_ROWS = 256


def _ln_stats(r):
    mu = jnp.mean(r, axis=-1, keepdims=True)
    xc = r - mu
    var = jnp.mean(xc * xc, axis=-1, keepdims=True)
    rstd = lax.rsqrt(var + LN_EPS)
    return xc * rstd, rstd


def _row_spec(d):
    return pl.BlockSpec((LN_ROWS, d), lambda i: (i, 0))


def _vec_spec(d):
    return pl.BlockSpec((1, d), lambda i: (0, 0))


def _ln_apply(r, g, b):
    xhat, _ = _ln_stats(r)
    return xhat * g + b


def _ln_fwd(name, a, m, g, b, alpha, pre=None):
    t, d = a.shape
    has_m = m is not None
    pre = list(pre) if pre is not None else []

    def body(*refs):
        a_ref, refs = refs[0], refs[1:]
        av = a_ref[...]
        if pre:
            av = _ln_apply(av, refs[0][...], refs[1][...])
            refs = refs[2:]
        if has_m:
            m_ref, g_ref, b_ref, r_ref, yb_ref, yt_ref = refs
            r = alpha * av + m_ref[...]
            r_ref[...] = r
        else:
            g_ref, b_ref, yb_ref, yt_ref = refs
            r = av
        y = _ln_apply(r, g_ref[...], b_ref[...])
        yb_ref[...] = y.astype(BF16)
        yt_ref[...] = y.T.astype(BF16)

    ins = [a] + pre + ([m] if has_m else []) + [g, b]
    in_specs = [_row_spec(d)] + [_vec_spec(d)] * len(pre) + [_row_spec(d)] * has_m + [_vec_spec(d)] * 2
    outs = ([jax.ShapeDtypeStruct((t, d), F32)] if has_m else []) + [
        jax.ShapeDtypeStruct((t, d), BF16), jax.ShapeDtypeStruct((d, t), BF16)]
    res = pl.pallas_call(
        body, name=name, grid=(t // LN_ROWS,), in_specs=in_specs,
        out_specs=[_row_spec(d)] * (len(outs) - 1) + [pl.BlockSpec((d, LN_ROWS), lambda i: (0, i))], out_shape=outs,
        compiler_params=_cparams(("parallel",)),
    )(*ins)
    return res if has_m else (None, *res)


def _ln_bwd_math(r, dy, g):
    xhat, rstd = _ln_stats(r)
    dxhat = dy * g
    m1 = jnp.mean(dxhat, axis=-1, keepdims=True)
    m2 = jnp.mean(dxhat * xhat, axis=-1, keepdims=True)
    dr = rstd * (dxhat - m1 - xhat * m2)
    return dr, jnp.sum(dy * xhat, axis=0, keepdims=True), jnp.sum(dy, axis=0, keepdims=True)


def _ln2_loss_bwd(name, r1, g1, b1, ffn, g, b, tgt):
    t, d = r1.shape

    def body(r1_ref, g1_ref, b1_ref, f_ref, g_ref, b_ref, t_ref, dr_ref, drb_ref, dg_ref, db_ref, loss_ref):
        @pl.when(pl.program_id(0) == 0)
        def _():
            dg_ref[...] = jnp.zeros_like(dg_ref)
            db_ref[...] = jnp.zeros_like(db_ref)
            loss_ref[...] = jnp.zeros_like(loss_ref)

        r = ALPHA * _ln_apply(r1_ref[...], g1_ref[...], b1_ref[...]) + f_ref[...]
        xhat, _ = _ln_stats(r)
        e = xhat * g_ref[...] + b_ref[...] - t_ref[...]
        loss_ref[...] += 0.5 / d * jnp.sum(e * e)
        dr, dg, db = _ln_bwd_math(r, e * (1.0 / d), g_ref[...])
        dr_ref[...] = dr
        drb_ref[...] = dr.astype(BF16)
        dg_ref[...] += dg
        db_ref[...] += db

    return pl.pallas_call(
        body, name=name, grid=(t // LN_ROWS,),
        in_specs=[_row_spec(d), _vec_spec(d), _vec_spec(d), _row_spec(d), _vec_spec(d), _vec_spec(d), _row_spec(d)],
        out_specs=[_row_spec(d), _row_spec(d), _vec_spec(d), _vec_spec(d), _vec_spec(128)],
        out_shape=[jax.ShapeDtypeStruct((t, d), F32), jax.ShapeDtypeStruct((t, d), BF16),
                   jax.ShapeDtypeStruct((1, d), F32), jax.ShapeDtypeStruct((1, d), F32),
                   jax.ShapeDtypeStruct((1, 128), F32)],
        compiler_params=_cparams(("arbitrary",)),
    )(r1, g1, b1, ffn, g, b, tgt)


def _ln_bwd(name, r, dya, dyb, g, alpha, want_bf16):
    t, d = r.shape

    def body(r_ref, dya_ref, dyb_ref, g_ref, *outs):
        dr_ref = outs[0]
        dg_ref, db_ref = outs[-2:]

        @pl.when(pl.program_id(0) == 0)
        def _():
            dg_ref[...] = jnp.zeros_like(dg_ref)
            db_ref[...] = jnp.zeros_like(db_ref)

        dy = alpha * dya_ref[...] + dyb_ref[...]
        dr, dg, db = _ln_bwd_math(r_ref[...], dy, g_ref[...])
        dr_ref[...] = dr
        if want_bf16:
            outs[1][...] = dr.astype(BF16)
        dg_ref[...] += dg
        db_ref[...] += db

    big = [jax.ShapeDtypeStruct((t, d), F32)] + ([jax.ShapeDtypeStruct((t, d), BF16)] if want_bf16 else [])
    return pl.pallas_call(
        body, name=name, grid=(t // LN_ROWS,),
        in_specs=[_row_spec(d)] * 3 + [_vec_spec(d)],
        out_specs=[_row_spec(d)] * len(big) + [_vec_spec(d)] * 2,
        out_shape=big + [jax.ShapeDtypeStruct((1, d), F32)] * 2,
        compiler_params=_cparams(("arbitrary",)),
    )(r, dya, dyb, g)


CONV_ROWS = 64
CONV_UNROLL = 8
FFN_UNROLL = 4


def _unrolled(n, unroll, fn, init):
    def body(i, carry):
        for u in range(unroll):
            carry = fn(i * unroll + u, carry)
        return carry

    return lax.fori_loop(0, n // unroll, body, init)


def _for_shifted(ref, r0, tm, shifts, fn):
    for s in shifts:
        fn(s, ref[pl.ds(r0 + s, tm), :])


def _col_spec(t, cb, off=0):
    return pl.BlockSpec((t, cb), lambda j: (0, j + off))


def _ffn_act_fwd(name, hf, w, b, cb=128):
    t = hf.shape[0]
    f = hf.shape[1] // 2
    nb = f // cb
    tm = CONV_ROWS

    def body(g_ref, v_ref, w_ref, b_ref, act_ref, pad_ref):
        pad_ref[pl.ds(0, 8), :] = jnp.zeros((8, cb), F32)
        pad_ref[pl.ds(8, t), :] = g_ref[...].astype(F32)
        wv = [w_ref[pl.ds(k, 1), :] for k in range(FFN_KERNEL)]
        bias = b_ref[...]

        def tile(i, carry):
            r0 = pl.multiple_of(i * tm, tm)
            acc = [jnp.broadcast_to(bias, (tm, cb))]

            def tap(s, rows):
                acc[0] = acc[0] + wv[s - 6] * rows

            _for_shifted(pad_ref, r0, tm, (6, 7, 8), tap)
            gc = acc[0]
            act_ref[pl.ds(r0, tm), :] = (gc * _sigmoid(gc) * v_ref[pl.ds(r0, tm), :].astype(F32)).astype(BF16)
            return carry

        _unrolled(t // tm, FFN_UNROLL, tile, 0)

    return pl.pallas_call(
        body, name=name, grid=(nb,),
        in_specs=[_col_spec(t, cb), _col_spec(t, cb, nb),
                  pl.BlockSpec((FFN_KERNEL, cb), lambda j: (0, j)), pl.BlockSpec((1, cb), lambda j: (0, j))],
        out_specs=_col_spec(t, cb), out_shape=jax.ShapeDtypeStruct((t, f), BF16),
        scratch_shapes=[pltpu.VMEM((t + 8, cb), F32)],
        compiler_params=_cparams(("parallel",)),
    )(hf, hf, w, b)


def _ffn_act_bwd(name, dact, hf, w, b, cb=128):
    t = hf.shape[0]
    f = hf.shape[1] // 2
    nb = f // cb
    tm = CONV_ROWS

    def body(da_ref, g_ref, v_ref, w_ref, b_ref, dhf_ref, dw_ref, db_ref, pad_ref, dgc_ref):
        pad_ref[pl.ds(0, 8), :] = jnp.zeros((8, cb), F32)
        pad_ref[pl.ds(8, t), :] = g_ref[...].astype(F32)
        dgc_ref[pl.ds(t, 8), :] = jnp.zeros((8, cb), F32)
        wv = [w_ref[pl.ds(k, 1), :] for k in range(FFN_KERNEL)]
        bias = b_ref[...]

        def tile_a(i, carry):
            r0 = pl.multiple_of(i * tm, tm)
            taps = {}
            _for_shifted(pad_ref, r0, tm, (6, 7, 8), lambda s, rows: taps.__setitem__(s, rows))
            gc = bias + wv[0] * taps[6] + wv[1] * taps[7] + wv[2] * taps[8]
            sg = _sigmoid(gc)
            da = da_ref[pl.ds(r0, tm), :].astype(F32)
            dhf_ref[1, pl.ds(r0, tm), :] = (da * gc * sg).astype(BF16)
            dgc = da * v_ref[pl.ds(r0, tm), :].astype(F32) * sg * (1.0 + gc * (1.0 - sg))
            dgc_ref[pl.ds(r0, tm), :] = dgc
            sums = [jnp.sum(dgc * taps[6 + k], axis=0, keepdims=True) for k in range(3)]
            sums.append(jnp.sum(dgc, axis=0, keepdims=True))
            return tuple(c + s for c, s in zip(carry, sums))

        zero = jnp.zeros((1, cb), F32)
        dw0, dw1, dw2, dbias = _unrolled(t // tm, FFN_UNROLL, tile_a, (zero, zero, zero, zero))
        row = lax.broadcasted_iota(jnp.int32, (8, cb), 0)
        dw_ref[...] = jnp.where(row == 0, dw0, jnp.where(row == 1, dw1, jnp.where(row == 2, dw2, 0.0)))
        db_ref[...] = dbias

        def tile_b(i, carry):
            r0 = pl.multiple_of(i * tm, tm)
            acc = [jnp.zeros((tm, cb), F32)]

            def tap(s, rows):
                acc[0] = acc[0] + wv[2 - s] * rows

            _for_shifted(dgc_ref, r0, tm, (0, 1, 2), tap)
            dhf_ref[0, pl.ds(r0, tm), :] = acc[0].astype(BF16)
            return carry

        lax.fori_loop(0, t // tm, tile_b, 0)

    return pl.pallas_call(
        body, name=name, grid=(nb,),
        in_specs=[_col_spec(t, cb), _col_spec(t, cb), _col_spec(t, cb, nb),
                  pl.BlockSpec((FFN_KERNEL, cb), lambda j: (0, j)), pl.BlockSpec((1, cb), lambda j: (0, j))],
        out_specs=[pl.BlockSpec((2, t, cb), lambda j: (0, 0, j)),
                   pl.BlockSpec((8, cb), lambda j: (0, j)), pl.BlockSpec((1, cb), lambda j: (0, j))],
        out_shape=[jax.ShapeDtypeStruct((2, t, f), BF16), jax.ShapeDtypeStruct((8, f), F32),
                   jax.ShapeDtypeStruct((1, f), F32)],
        scratch_shapes=[pltpu.VMEM((t + 8, cb), F32), pltpu.VMEM((t + 8, cb), F32)],
        compiler_params=_cparams(("parallel",)),
    )(dact, hf, hf, w, b)


def _silu_grad(z, sg):
    return sg * (1.0 + z * (1.0 - sg))


def _conv_fwd(name, hin, w, b, ng, nb_, cat):
    t = hin.shape[0]
    c = GROUP
    tm = CONV_ROWS
    pad = 32
    shifts = tuple(2 + k for k in range(CONV_KERNEL))

    def body(a_ref, gt_ref, w_ref, b_ref, ng_ref, nb_ref, cat_ref, u1_ref, u3_ref, pad_ref):
        pad_ref[pl.ds(0, pad), :] = jnp.zeros((pad, c), F32)
        pad_ref[pl.ds(pad, t), :] = a_ref[...] * _sigmoid(gt_ref[...])
        bias, gam, bet = b_ref[...], ng_ref[...], nb_ref[...]

        def tile(i, carry):
            r0 = pl.multiple_of(i * tm, tm)
            acc = [jnp.broadcast_to(bias, (tm, c))]

            def tap(s, rows):
                acc[0] = acc[0] + w_ref[pl.ds(s - 2, 1), :] * rows

            _for_shifted(pad_ref, r0, tm, shifts, tap)
            u1 = acc[0]
            u1_ref[pl.ds(r0, tm), :] = u1
            xhat, _ = _ln_stats(u1)
            u2 = xhat * gam + bet
            u3_ref[pl.ds(r0, tm), :] = (u2 * _sigmoid(u2)).astype(BF16)
            return carry

        _unrolled(t // tm, CONV_UNROLL, tile, 0)

    vec = pl.BlockSpec((1, c), lambda j: (0, j))
    return pl.pallas_call(
        body, name=name, grid=(N_GROUPS,),
        in_specs=[_col_spec(t, c), _col_spec(t, c, N_GROUPS),
                  pl.BlockSpec((CONV_KERNEL, c), lambda j: (0, j)), vec, vec, vec, ANY],
        out_specs=[_col_spec(t, c), _col_spec(t, c)],
        out_shape=[jax.ShapeDtypeStruct((t, CONV_WIDTH), F32), jax.ShapeDtypeStruct(cat.shape, BF16)],
        input_output_aliases={6: 1},
        scratch_shapes=[pltpu.VMEM((t + pad, c), F32)],
        compiler_params=_cparams(("parallel",)),
    )(hin, hin, w, b, ng, nb_, cat)


def _conv_bwd(name, dcat, u1, hin, w, ng, nb_):
    t = hin.shape[0]
    c = GROUP
    tm = CONV_ROWS
    pad = 32
    nk = CONV_KERNEL

    def body(du3_ref, u1_ref, a_ref, gt_ref, w_ref, ng_ref, nb_ref,
             da_ref, dgt_ref, dw_ref, db_ref, dng_ref, dnb_ref, u0_ref, du1_ref, dwp_ref):
        u0_ref[pl.ds(0, pad), :] = jnp.zeros((pad, c), F32)
        u0_ref[pl.ds(pad, t), :] = a_ref[...] * _sigmoid(gt_ref[...])
        du1_ref[pl.ds(t, pad), :] = jnp.zeros((pad, c), F32)
        dwp_ref[...] = jnp.zeros_like(dwp_ref)
        gam, bet = ng_ref[...], nb_ref[...]

        def tile_a(i, carry):
            r0 = pl.multiple_of(i * tm, tm)
            u1 = u1_ref[pl.ds(r0, tm), :]
            xhat, rstd = _ln_stats(u1)
            u2 = xhat * gam + bet
            sg = _sigmoid(u2)
            du2 = du3_ref[pl.ds(r0, tm), :] * _silu_grad(u2, sg)
            dxhat = du2 * gam
            m1 = jnp.mean(dxhat, axis=-1, keepdims=True)
            m2 = jnp.mean(dxhat * xhat, axis=-1, keepdims=True)
            du1 = rstd * (dxhat - m1 - xhat * m2)
            du1_ref[pl.ds(r0, tm), :] = du1
            sums = (jnp.sum(du1, axis=0, keepdims=True), jnp.sum(du2 * xhat, axis=0, keepdims=True),
                    jnp.sum(du2, axis=0, keepdims=True))
            return tuple(x + s for x, s in zip(carry, sums))

        zero = jnp.zeros((1, c), F32)
        dbias, dgam, dbet = _unrolled(t // tm, CONV_UNROLL, tile_a, (zero, zero, zero))
        db_ref[...] = dbias
        dng_ref[...] = dgam
        dnb_ref[...] = dbet

        def tile_b(i, carry):
            r0 = pl.multiple_of(i * tm, tm)
            du1 = du1_ref[pl.ds(r0, tm), :]
            acc = [jnp.zeros((tm, c), F32)]

            def tap_dx(s, rows):
                acc[0] = acc[0] + w_ref[pl.ds(nk - 1 - s, 1), :] * rows

            _for_shifted(du1_ref, r0, tm, tuple(range(nk)), tap_dx)

            def tap_dw(s, rows):
                part = (du1 * rows).reshape(tm // 8, 8, c).sum(axis=0)
                dwp_ref[s - 2] = dwp_ref[s - 2] + part

            _for_shifted(u0_ref, r0, tm, tuple(2 + k for k in range(nk)), tap_dw)
            du0 = acc[0]
            a = a_ref[pl.ds(r0, tm), :]
            sg = _sigmoid(gt_ref[pl.ds(r0, tm), :])
            da_ref[pl.ds(r0, tm), :] = (du0 * sg).astype(BF16)
            dgt_ref[pl.ds(r0, tm), :] = (du0 * a * sg * (1.0 - sg)).astype(BF16)
            return carry

        lax.fori_loop(0, t // tm, tile_b, 0)
        dw_ref[...] = jnp.sum(dwp_ref[...], axis=1)

    vec = pl.BlockSpec((1, c), lambda j: (0, j))
    vshape = jax.ShapeDtypeStruct((1, CONV_WIDTH), F32)
    return pl.pallas_call(
        body, name=name, grid=(N_GROUPS,),
        in_specs=[_col_spec(t, c), _col_spec(t, c), _col_spec(t, c), _col_spec(t, c, N_GROUPS),
                  pl.BlockSpec((nk, c), lambda j: (0, j)), vec, vec],
        out_specs=[_col_spec(t, c), _col_spec(t, c), pl.BlockSpec((32, c), lambda j: (0, j)), vec, vec, vec],
        out_shape=[jax.ShapeDtypeStruct((t, CONV_WIDTH), BF16), jax.ShapeDtypeStruct((t, CONV_WIDTH), BF16),
                   jax.ShapeDtypeStruct((32, CONV_WIDTH), F32), vshape, vshape, vshape],
        scratch_shapes=[pltpu.VMEM((t + pad, c), F32), pltpu.VMEM((t + pad, c), F32),
                        pltpu.VMEM((32, 8, c), F32)],
        compiler_params=_cparams(("parallel",)),
    )(dcat, u1, hin, hin, w, ng, nb_)


LEVELS = (64, 32, 16)
HGRN_UNROLL = 8
HGRN_UNROLL_FWD = 16
NT_DIMS = (((1,), (1,)), ((), ()))
NN_DIMS = (((1,), (0,)), ((), ()))
TN_DIMS = (((0,), (0,)), ((), ()))


def _bdot(a, b, dims):
    return lax.dot_general(a.astype(BF16), b.astype(BF16), dims, preferred_element_type=F32)


def _hdot(a, b):
    return jnp.dot(a, b, precision=lax.Precision.HIGHEST, preferred_element_type=F32)


def _chunk_consts():
    rid = lax.broadcasted_iota(jnp.int32, (CHUNK, GROUP), 0)
    ti = lax.broadcasted_iota(jnp.int32, (CHUNK, CHUNK), 0)
    si = lax.broadcasted_iota(jnp.int32, (CHUNK, CHUNK), 1)
    tri = (si <= ti).astype(F32)
    second = [(rid & (b // 2)) != 0 for b in LEVELS]
    same = [None] + [(ti // b) == (si // b) for b in LEVELS[1:]]
    sub = lax.broadcasted_iota(jnp.int32, (SUB, GROUP), 0)
    return rid, tri, second, same, sub


def _level_refs(cum_ref, rid, base):
    row = lambda i: cum_ref[pl.ds(base + i, 1), :]
    l1 = jnp.broadcast_to(row(31), (CHUNK, GROUP))
    l2 = jnp.where(rid < 32, row(15), row(47))
    l3 = jnp.where(rid < 16, row(7), jnp.where(rid < 32, row(23), jnp.where(rid < 48, row(39), row(55))))
    return l1, l2, l3


def _level_factors(cum, brefs, second):
    out = []
    for bref, sec in zip(brefs, second):
        eq = jnp.where(sec, jnp.exp(jnp.minimum(cum - bref, 0.0)), 0.0)
        ek = jnp.where(sec, 0.0, jnp.exp(jnp.minimum(bref - cum, 0.0)))
        out.append((eq, ek))
    return out


def _gates(q, f, lb):
    sq = _sigmoid(q)
    sf = _sigmoid(f)
    fg = lb + (1.0 - lb) * sf
    return q * sq, sq, sf, fg


def _hgrn_specs(t, nc):
    c = GROUP
    col = lambda off: pl.BlockSpec((t, c), lambda h: (0, h + off))
    hin_specs = [col(16), col(24), col(32), col(40)]
    vec = pl.BlockSpec((1, c), lambda h: (0, h))
    lbs = pl.BlockSpec((2, c), lambda h: (0, h))
    st = pl.BlockSpec((1, nc, c, c), lambda h: (h, 0, 0, 0))
    return col, hin_specs, vec, lbs, st


def _hgrn_fwd(name, hin, lb_logits, hg):
    t = hin.shape[0]
    nc = t // CHUNK
    c = GROUP
    col, hin_specs, vec, lbs, st = _hgrn_specs(t, nc)

    def body(q_ref, f_ref, v_ref, og_ref, lb_ref, hg_ref, o_ref, ob_ref, st_ref,
             s_ref, cum_ref, kk_ref, vc_ref):
        rid, tri, second, same, sub = _chunk_consts()
        lb = _sigmoid(lb_ref[pl.ds(0, 1), :] - lb_ref[pl.ds(1, 1), :])
        gain = hg_ref[...]
        s_ref[...] = jnp.zeros_like(s_ref)

        def chunk(ci, u):
            base = u * CHUNK
            r0 = pl.multiple_of(ci * CHUNK, CHUNK)
            rows = pl.ds(r0, CHUNK)
            qh, _, _, fg = _gates(q_ref[rows, :], f_ref[rows, :], lb)
            v = v_ref[rows, :]
            kk = 1.0 - fg
            cum = _hdot(tri, jnp.log(fg))
            cum_ref[pl.ds(base, CHUNK), :] = cum
            kk_ref[pl.ds(base, CHUNK), :] = kk
            vc_ref[pl.ds(base, CHUNK), :] = v
            sprev = s_ref[...]
            st_ref[0, ci] = sprev
            blast = cum_ref[pl.ds(base + CHUNK - 1, 1), :]
            o = _bdot(qh * jnp.exp(cum), sprev, NT_DIMS)
            s_ref[...] = sprev * jnp.exp(blast) + _bdot(v, kk * jnp.exp(blast - cum), TN_DIMS)
            a = None
            for (eq, ek), msk in zip(_level_factors(cum, _level_refs(cum_ref, rid, base), second), same):
                al = _bdot(qh * eq, kk * ek, NT_DIMS)
                al = al if msk is None else jnp.where(msk, al, 0.0)
                a = al if a is None else a + al
            o = o + _bdot(a, v, NN_DIMS)
            diag = []
            for sb in range(CHUNK // SUB):
                lo = sb * SUB
                qb = qh[lo:lo + SUB]
                cb = cum[lo:lo + SUB]
                od = jnp.zeros((SUB, c), F32)
                for s in range(SUB):
                    e = jnp.where(sub >= s, jnp.exp(jnp.minimum(cb - cum_ref[pl.ds(base + lo + s, 1), :], 0.0)), 0.0)
                    acol = jnp.sum(qb * e * kk_ref[pl.ds(base + lo + s, 1), :], axis=-1, keepdims=True)
                    od = od + acol * vc_ref[pl.ds(base + lo + s, 1), :]
                diag.append(od)
            o = o + jnp.concatenate(diag, axis=0)
            o_ref[rows, :] = o
            y = o * lax.rsqrt(jnp.mean(o * o, axis=-1, keepdims=True) + RMS_EPS) * gain
            og = og_ref[rows, :]
            ob_ref[rows, :] = (y * og * _sigmoid(og)).astype(BF16)

        def chunks(i, carry):
            for u in range(HGRN_UNROLL_FWD):
                chunk(i * HGRN_UNROLL_FWD + u, u)
            return carry

        lax.fori_loop(0, nc // HGRN_UNROLL_FWD, chunks, 0)

    return pl.pallas_call(
        body, name=name, grid=(N_GROUPS,),
        in_specs=hin_specs + [lbs, vec],
        out_specs=[col(0), col(N_GROUPS), st],
        out_shape=[jax.ShapeDtypeStruct((t, HGRN_WIDTH), F32), jax.ShapeDtypeStruct((t, CONV_WIDTH + HGRN_WIDTH), BF16),
                   jax.ShapeDtypeStruct((N_GROUPS, nc, c, c), F32)],
        scratch_shapes=[pltpu.VMEM((c, c), F32)] + [pltpu.VMEM((HGRN_UNROLL_FWD * CHUNK, c), F32)] * 3,
        compiler_params=_cparams(("parallel",)),
    )(hin, hin, hin, hin, lb_logits, hg)


def _hgrn_bwd(name, dcat, hin, o_raw, states, lb_logits, hg):
    t = hin.shape[0]
    nc = t // CHUNK
    c = GROUP
    col, hin_specs, vec, lbs, st = _hgrn_specs(t, nc)

    def body(do_ref, q_ref, f_ref, v_ref, og_ref, o_ref, st_ref, lb_ref, hg_ref,
             dq_ref, df_ref, dv_ref, dog_ref, dhg_ref, dlb_ref,
             ds_ref, cum_ref, kk_ref, vc_ref, qh_ref_s, do_ref_s):
        rid, tri, second, same, sub = _chunk_consts()
        trit = tri.T
        lb = _sigmoid(lb_ref[pl.ds(0, 1), :] - lb_ref[pl.ds(1, 1), :])
        gain = hg_ref[...]
        ds_ref[...] = jnp.zeros_like(ds_ref)

        def chunk(i, carry, u):
            base = u * CHUNK
            dhg, dlb = carry
            ci = nc - 1 - i
            r0 = pl.multiple_of(ci * CHUNK, CHUNK)
            rows = pl.ds(r0, CHUNK)
            q = q_ref[rows, :]
            qh, sq, sf, fg = _gates(q, f_ref[rows, :], lb)
            v = v_ref[rows, :]
            kk = 1.0 - fg
            cum = _hdot(tri, jnp.log(fg))
            cum_ref[pl.ds(base, CHUNK), :] = cum
            kk_ref[pl.ds(base, CHUNK), :] = kk
            vc_ref[pl.ds(base, CHUNK), :] = v
            qh_ref_s[pl.ds(base, CHUNK), :] = qh
            o = o_ref[rows, :]
            og = og_ref[rows, :]
            sg = _sigmoid(og)
            rinv = lax.rsqrt(jnp.mean(o * o, axis=-1, keepdims=True) + RMS_EPS)
            yn = o * rinv
            dof = do_ref[rows, :]
            dog_ref[rows, :] = (dof * yn * gain * _silu_grad(og, sg)).astype(BF16)
            dz = dof * og * sg
            dhg = dhg + jnp.sum(dz * yn, axis=0, keepdims=True)
            dy = dz * gain
            do = rinv * (dy - yn * jnp.mean(dy * yn, axis=-1, keepdims=True))
            do_ref_s[pl.ds(base, CHUNK), :] = do
            sprev = st_ref[0, ci]
            dsn = ds_ref[...]
            blast = cum_ref[pl.ds(base + CHUNK - 1, 1), :]
            eq0 = jnp.exp(cum)
            ek0 = jnp.exp(blast - cum)
            dqh = _bdot(do, sprev, NN_DIMS) * eq0
            dkk = _bdot(v, dsn, NN_DIMS) * ek0
            dlast = (jnp.sum(kk * dkk, axis=0, keepdims=True)
                     + jnp.exp(blast) * jnp.sum(dsn * sprev, axis=0, keepdims=True))
            dv = _bdot(kk * ek0, dsn, NT_DIMS)
            ds_ref[...] = dsn * jnp.exp(blast) + _bdot(do, qh * eq0, TN_DIMS)
            dg = qh * dqh - kk * dkk
            da = _bdot(do, v, NT_DIMS)
            a = None
            for (eq, ek), msk in zip(_level_factors(cum, _level_refs(cum_ref, rid, base), second), same):
                ql, kl = (qh * eq).astype(BF16), (kk * ek).astype(BF16)
                al = _bdot(ql, kl, NT_DIMS)
                dal = da
                if msk is not None:
                    al = jnp.where(msk, al, 0.0)
                    dal = jnp.where(msk, da, 0.0)
                a = al if a is None else a + al
                dql = _bdot(dal, kl, NN_DIMS)
                dkl = _bdot(dal, ql, TN_DIMS)
                dqh = dqh + dql * eq
                dkk = dkk + dkl * ek
                dg = dg + (ql.astype(F32) * dql - kl.astype(F32) * dkl)
            dv = dv + _bdot(a, do, TN_DIMS)
            dq_d, dk_d, dv_d = [], [], []
            for sb in range(CHUNK // SUB):
                lo = sb * SUB
                cb = cum[lo:lo + SUB]
                kb = kk[lo:lo + SUB]
                vb = v[lo:lo + SUB]
                dob = do[lo:lo + SUB]
                dqb = jnp.zeros((SUB, c), F32)
                dkb = jnp.zeros((SUB, c), F32)
                dvb = jnp.zeros((SUB, c), F32)
                for s in range(SUB):
                    crow = cum_ref[pl.ds(base + lo + s, 1), :]
                    e = jnp.where(sub >= s, jnp.exp(jnp.minimum(cb - crow, 0.0)), 0.0)
                    dacol = jnp.sum(dob * vc_ref[pl.ds(base + lo + s, 1), :], axis=-1, keepdims=True)
                    dqb = dqb + dacol * (kk_ref[pl.ds(base + lo + s, 1), :] * e)
                    et = jnp.where(sub <= s, jnp.exp(jnp.minimum(crow - cb, 0.0)), 0.0)
                    dorow = do_ref_s[pl.ds(base + lo + s, 1), :]
                    qe = qh_ref_s[pl.ds(base + lo + s, 1), :] * et
                    dkb = dkb + jnp.sum(vb * dorow, axis=-1, keepdims=True) * qe
                    dvb = dvb + jnp.sum(kb * qe, axis=-1, keepdims=True) * dorow
                dq_d.append(dqb)
                dk_d.append(dkb)
                dv_d.append(dvb)
            dq_d = jnp.concatenate(dq_d, axis=0)
            dk_d = jnp.concatenate(dk_d, axis=0)
            dqh = dqh + dq_d
            dkk = dkk + dk_d
            dg = dg + (qh * dq_d - kk * dk_d)
            dv = dv + jnp.concatenate(dv_d, axis=0)
            dlf = _hdot(trit, dg) + dlast
            dfg = dlf / fg - dkk
            df_ref[rows, :] = (dfg * (1.0 - lb) * sf * (1.0 - sf)).astype(BF16)
            dlb = dlb + jnp.sum(dfg * (1.0 - sf), axis=0, keepdims=True)
            dq_ref[rows, :] = (dqh * _silu_grad(q, sq)).astype(BF16)
            dv_ref[rows, :] = dv.astype(BF16)
            return dhg, dlb

        def chunks(i, carry):
            for u in range(HGRN_UNROLL):
                carry = chunk(i * HGRN_UNROLL + u, carry, u)
            return carry

        zero = jnp.zeros((1, c), F32)
        dhg, dlb = lax.fori_loop(0, nc // HGRN_UNROLL, chunks, (zero, zero))
        dhg_ref[...] = dhg
        dl0 = dlb * lb * (1.0 - lb)
        dlb_ref[...] = jnp.where(lax.broadcasted_iota(jnp.int32, (2, c), 0) == 0, dl0, -dl0)

    big = jax.ShapeDtypeStruct((t, HGRN_WIDTH), BF16)
    return pl.pallas_call(
        body, name=name, grid=(N_GROUPS,),
        in_specs=[col(8)] + hin_specs + [col(0), st, lbs, vec],
        out_specs=[col(0)] * 4 + [vec, lbs],
        out_shape=[big] * 4 + [jax.ShapeDtypeStruct((1, HGRN_WIDTH), F32), jax.ShapeDtypeStruct((2, HGRN_WIDTH), F32)],
        scratch_shapes=[pltpu.VMEM((c, c), F32)] + [pltpu.VMEM((HGRN_UNROLL * CHUNK, c), F32)] * 5,
        compiler_params=_cparams(("parallel",)),
    )(dcat, hin, hin, hin, hin, o_raw, states, lb_logits, hg)


ANY = pl.BlockSpec(memory_space=pl.ANY)


def _my_place():
    return lax.axis_index("x"), lax.axis_index("y"), lax.axis_index("c")


HBM = pl.BlockSpec(memory_space=pltpu.HBM)
SEM = pl.BlockSpec(memory_space=pltpu.SEMAPHORE)
EFFECT = pltpu.SideEffectType.DATAFLOW_SIDE_EFFECTING


def _peer(k):
    x, y, c = _my_place()
    px = 1 - x if k & 4 else x
    py = 1 - y if k & 2 else y
    pc = 1 - c if k & 1 else c
    return (px, py, pc), 4 * px + 2 * py + pc


def _slot(land_ref, idx):
    if len(land_ref.shape) == 2:
        ns = land_ref.shape[1] // N_DEV
        return land_ref.at[:, pl.ds(pl.multiple_of(idx * ns, 128), ns)]
    return land_ref.at[idx]


def _exchange_copy(k, src_ref, land_ref, send_sems, recv_sems, scatter, landing):
    x, y, c = _my_place()
    me = 4 * x + 2 * y + c
    to, idx = _peer(k)
    return pltpu.make_async_remote_copy(
        src_ref=_slot(src_ref, idx) if scatter else src_ref,
        dst_ref=_slot(land_ref, idx) if landing else _slot(land_ref, me),
        send_sem=send_sems.at[k - 1], recv_sem=recv_sems.at[k - 1], device_id=to, device_id_type=MESH)


ALL_PEERS = tuple(range(1, N_DEV))
NEAR_PEERS = (1, 2, 4, 6)
SAME_CORE_PEERS = (2, 4, 6)


def _exchange_start(name, src, land, scatter, ks=ALL_PEERS):
    def body(src_ref, land_ref, send_sems, recv_sems, src_thru, land_thru, token):
        for k in ks:
            _exchange_copy(k, src_ref, land_ref, send_sems, recv_sems, scatter, landing=False).start()
        token[...] = jnp.zeros_like(token)

    send_sems, recv_sems, src_thru, land_thru, token = pl.pallas_call(
        body, name=name,
        out_shape=(pltpu.SemaphoreType.DMA((N_DEV - 1,)), pltpu.SemaphoreType.DMA((N_DEV - 1,)),
                   pltpu.HBM(src.shape, src.dtype), pltpu.HBM(land.shape, land.dtype),
                   jax.ShapeDtypeStruct((8, 128), F32)),
        in_specs=(HBM, HBM), out_specs=(SEM, SEM, HBM, HBM, pl.BlockSpec(memory_space=pltpu.VMEM)),
        input_output_aliases={0: 2, 1: 3},
        compiler_params=pltpu.CompilerParams(has_side_effects=EFFECT),
    )(pltpu.with_memory_space_constraint(src, pltpu.HBM), pltpu.with_memory_space_constraint(land, pltpu.HBM))
    return (send_sems, recv_sems, src_thru, land_thru, scatter, ks), token


def _exchange_wait(name, handle, after):
    send_sems, recv_sems, src_thru, land_thru, scatter, ks = handle

    def body(src_ref, land_ref, send_sems, recv_sems, after_ref, src_dead, got_ref):
        for k in ks:
            cp = _exchange_copy(k, src_ref, land_ref, send_sems, recv_sems, scatter, landing=True)
            cp.wait_send()
            cp.wait_recv()

    return pl.pallas_call(
        body, name=name,
        out_shape=(pltpu.HBM(src_thru.shape, src_thru.dtype), pltpu.HBM(land_thru.shape, land_thru.dtype)),
        in_specs=(HBM, HBM, SEM, SEM, ANY), out_specs=(HBM, HBM), input_output_aliases={0: 0, 1: 1},
        compiler_params=pltpu.CompilerParams(has_side_effects=EFFECT),
    )(src_thru, land_thru, send_sems, recv_sems, after)[1]


def _relay_copy(j, land_ref, send_sems, recv_sems, landing):
    x, y, c = _my_place()
    k = SAME_CORE_PEERS[j]
    _, sent = _peer(k)
    _, got = _peer(k + 1)
    return pltpu.make_async_remote_copy(
        src_ref=_slot(land_ref, sent), dst_ref=_slot(land_ref, got) if landing else _slot(land_ref, sent),
        send_sem=send_sems.at[j], recv_sem=recv_sems.at[j], device_id=(x, y, 1 - c), device_id_type=MESH)


def _relay_start(name, land):
    n = len(SAME_CORE_PEERS)

    def body(land_ref, send_sems, recv_sems, land_thru, token):
        for j in range(n):
            _relay_copy(j, land_ref, send_sems, recv_sems, landing=False).start()
        token[...] = jnp.zeros_like(token)

    send_sems, recv_sems, land_thru, token = pl.pallas_call(
        body, name=name,
        out_shape=(pltpu.SemaphoreType.DMA((n,)), pltpu.SemaphoreType.DMA((n,)),
                   pltpu.HBM(land.shape, land.dtype), jax.ShapeDtypeStruct((8, 128), F32)),
        in_specs=(HBM,), out_specs=(SEM, SEM, HBM, pl.BlockSpec(memory_space=pltpu.VMEM)),
        input_output_aliases={0: 2},
        compiler_params=pltpu.CompilerParams(has_side_effects=EFFECT),
    )(pltpu.with_memory_space_constraint(land, pltpu.HBM))
    return (send_sems, recv_sems, land_thru), token


def _relay_wait(name, handle, after):
    send_sems, recv_sems, land_thru = handle

    def body(land_ref, send_sems, recv_sems, after_ref, got_ref):
        for j in range(len(SAME_CORE_PEERS)):
            cp = _relay_copy(j, land_ref, send_sems, recv_sems, landing=True)
            cp.wait_send()
            cp.wait_recv()

    return pl.pallas_call(
        body, name=name, out_shape=pltpu.HBM(land_thru.shape, land_thru.dtype),
        in_specs=(HBM, SEM, SEM, ANY), out_specs=HBM, input_output_aliases={0: 0},
        compiler_params=pltpu.CompilerParams(has_side_effects=EFFECT),
    )(land_thru, send_sems, recv_sems, after)


def _own_cols(name, own, me):
    r, ns = own.shape
    tr = 256

    def body(me_ref, own_ref, land_ref):
        land_ref[...] = own_ref[...]

    return pl.pallas_call(
        body, name=name,
        grid_spec=pltpu.PrefetchScalarGridSpec(
            num_scalar_prefetch=1, grid=(r // tr,),
            in_specs=[pl.BlockSpec((tr, ns), lambda i, me_ref: (i, 0))],
            out_specs=pl.BlockSpec((tr, ns), lambda i, me_ref: (i, me_ref[0]))),
        out_shape=jax.ShapeDtypeStruct((r, N_DEV * ns), own.dtype),
    )(jnp.reshape(me, (1,)).astype(jnp.int32), own)


def _own_slot(own, me):
    land = lax.empty((N_DEV,) + own.shape, own.dtype)
    return lax.dynamic_update_slice_in_dim(land, own[None], me, axis=0)


def _adamw_math(w, g, m, v):
    m = ADAM_B1 * m + (1.0 - ADAM_B1) * g
    v = ADAM_B2 * v + (1.0 - ADAM_B2) * (g * g)
    m_hat = m / (1.0 - ADAM_B1 ** ADAM_STEP)
    v_hat = v / (1.0 - ADAM_B2 ** ADAM_STEP)
    delta = -ADAM_LR * (m_hat / (jnp.sqrt(v_hat) + ADAM_EPS) + ADAM_WD * w)
    return delta, m, v


def _adamw_sum(name, recv, w, m, v, tr, row0=0, partial=None):
    r, c = w.shape
    rr = recv.shape[1]
    off = row0 // tr

    def body(recv_ref, w_ref, m_ref, v_ref, *refs):
        g_ref, d_ref, mo_ref, vo_ref = refs[-4:]
        g = recv_ref[0].astype(F32)
        for j in range(1, N_DEV):
            g = g + recv_ref[j].astype(F32)
        g_ref[...] = g
        d_ref[...], mo_ref[...], vo_ref[...] = _adamw_math(w_ref[...], g, m_ref[...], v_ref[...])

    tile = pl.BlockSpec((tr, c), lambda i: (i + off, 0))
    out = jax.ShapeDtypeStruct((r, c), F32)
    prev = list(partial) if partial is not None else []
    return pl.pallas_call(
        body, name=name, grid=(rr // tr,),
        in_specs=[pl.BlockSpec((N_DEV, tr, c), lambda i: (0, i, 0)), tile, tile, tile] + [ANY] * len(prev),
        out_specs=[tile] * 4, out_shape=[out] * 4,
        input_output_aliases={4 + i: i for i in range(len(prev))},
        compiler_params=_cparams(("parallel",)),
    )(recv, w, m, v, *prev)


def _sum_parts(name, parts):
    _, r, c = parts.shape

    def body(p_ref, o_ref):
        acc = p_ref[0]
        for j in range(1, N_DEV):
            acc = acc + p_ref[j]
        o_ref[...] = acc

    return pl.pallas_call(body, name=name, out_shape=jax.ShapeDtypeStruct((r, c), F32),
                          compiler_params=_cparams())(parts)


def _adamw_small(name, w, g, m, v):
    def body(w_ref, g_ref, m_ref, v_ref, d_ref, mo_ref, vo_ref):
        d_ref[...], mo_ref[...], vo_ref[...] = _adamw_math(w_ref[...], g_ref[...], m_ref[...], v_ref[...])

    out = jax.ShapeDtypeStruct(w.shape, F32)
    return pl.pallas_call(body, name=name, out_shape=[out] * 3, compiler_params=_cparams())(w, g, m, v)


def _pack(pieces, rows):
    flat = jnp.concatenate([p.reshape(-1).astype(F32) for p in pieces])
    return jnp.pad(flat, (0, rows * 128 - flat.shape[0])).reshape(rows, 128)


def _unpack(packed, shapes):
    flat = packed.reshape(-1)
    out, off = [], 0
    for s in shapes:
        n = 1
        for d in s:
            n *= d
        out.append(flat[off:off + n].reshape(s))
        off += n
    return out


def kernel(x, emb_ln_g, emb_ln_b, w_in, conv_w, conv_b, conv_norm_g, conv_norm_b, lb_logits, hgrn_norm_g, w_out, ln1_g, ln1_b, w_ffn_up, ffn_conv_w, ffn_conv_b, w_ffn_down, ln2_g, ln2_b, loss_target, m_emb_ln_g, m_emb_ln_b, m_w_in, m_conv_w, m_conv_b, m_conv_norm_g, m_conv_norm_b, m_lb_logits, m_hgrn_norm_g, m_w_out, m_ln1_g, m_ln1_b, m_w_ffn_up, m_ffn_conv_w, m_ffn_conv_b, m_w_ffn_down, m_ln2_g, m_ln2_b, v_emb_ln_g, v_emb_ln_b, v_w_in, v_conv_w, v_conv_b, v_conv_norm_g, v_conv_norm_b, v_lb_logits, v_hgrn_norm_g, v_w_out, v_ln1_g, v_ln1_b, v_w_ffn_up, v_ffn_conv_w, v_ffn_conv_b, v_w_ffn_down, v_ln2_g, v_ln2_b):
    t = x.shape[1]
    me = 4 * lax.axis_index("x") + 2 * lax.axis_index("y") + lax.axis_index("c")
    x2, tgt = x[0], loss_target[0]
    ns_in, ns_up = w_in.shape[2], w_ffn_up.shape[2]
    rs_out, rs_down = w_out.shape[1], w_ffn_down.shape[1]
    cs, fs = conv_w.shape[2], ffn_conv_w.shape[2]

    def gather_start(name, w, prev, ks=ALL_PEERS, cols=False):
        shard = (w[0] + prev).astype(BF16)
        land = _own_cols(name.replace("ag_", "own_"), shard, me) if cols else _own_slot(shard, me)
        return _exchange_start(name, shard, land, scatter=False, ks=ks)

    h_in, tok = gather_start("ag_w_in_start", w_in, 0.0, NEAR_PEERS, cols=True)
    taps = _pack([conv_w[0], ffn_conv_w[0]], 48) + tok[0, 0]
    h_taps, tok = _exchange_start("ag_taps_start", taps, _own_slot(taps, me), scatter=False)
    h_out, tok = gather_start("ag_w_out_start", w_out, tok[0, 0])
    h_up, tok = gather_start("ag_w_up_start", w_ffn_up, tok[0, 0], NEAR_PEERS, cols=True)
    h_down, tok = gather_start("ag_w_down_start", w_ffn_down, tok[0, 0])

    row = lambda a: a.reshape(1, -1)

    _, h0b, h0bt = _ln_fwd("ln_in", x2, None, row(emb_ln_g) + tok[0, 0], row(emb_ln_b), 1.0)
    h_relay, tok_relay = _relay_start("ag_w_in_relay_start", _exchange_wait("ag_w_in_wait", h_in, h0b))
    win_n = _relay_wait("ag_w_in_relay_wait", h_relay, tok_relay)
    hin = _mm_nn("mm_in", h0b, win_n, F32, tm=2048, tn=ns_in, tk=D_MODEL)
    n_cw, n_fw = CONV_KERNEL * cs, FFN_KERNEL * fs
    taps_g = _exchange_wait("ag_taps_wait", h_taps, hin).reshape(N_DEV, -1)
    cw_full = taps_g[:, :n_cw].reshape(N_DEV, CONV_KERNEL, cs).transpose(1, 0, 2).reshape(CONV_KERNEL, CONV_WIDTH)
    fw_full = taps_g[:, n_cw:n_cw + n_fw].reshape(N_DEV, FFN_KERNEL, fs).transpose(1, 0, 2).reshape(FFN_KERNEL, D_FF)

    o_raw, cat_right, states = _hgrn_fwd("hgrn_fwd", hin, lb_logits, hgrn_norm_g)
    u1, catb = _conv_fwd("conv_fwd", hin, cw_full, conv_b, conv_norm_g, conv_norm_b, cat_right)
    wout_g = _exchange_wait("ag_w_out_wait", h_out, catb).reshape(D_MODEL, D_MODEL)
    h_up_relay, tok = _relay_start("ag_w_up_relay_start", _exchange_wait("ag_w_up_wait", h_up, wout_g))
    mix = _mm_nn("mm_out", catb, wout_g, F32, tm=2048, tn=1024, tk=D_MODEL, after=tok)
    r1, h1b, h1bt = _ln_fwd("ln1", x2, mix, ln1_g, ln1_b, ALPHA, pre=(row(emb_ln_g), row(emb_ln_b)))
    wup_n = _relay_wait("ag_w_up_relay_wait", h_up_relay, h1b)
    hf = _mm_nn("mm_up", h1b, wup_n, BF16, tm=1024, tn=1024, tk=D_MODEL)
    actb = _ffn_act_fwd("ffn_act", hf, fw_full, ffn_conv_b)
    wdown_g = _exchange_wait("ag_w_down_wait", h_down, actb).reshape(D_FF, D_MODEL)
    ffn = _mm_nn("mm_down", actb, wdown_g, F32, tm=1024, tn=512, tk=D_FF)
    dr2, dr2b, g_ln2g, g_ln2b, loss = _ln2_loss_bwd("ln2_loss", r1, ln1_g, ln1_b, ffn, ln2_g, ln2_b, tgt)

    def scatter_start(name, parts):
        if parts.ndim == 2:
            ns = parts.shape[1] // N_DEV
            own = lax.dynamic_slice_in_dim(parts, me * ns, ns, axis=1)
        else:
            own = lax.dynamic_index_in_dim(parts, me, axis=0, keepdims=False)
        return _exchange_start(name, parts, _own_slot(own, me), scatter=True)

    dact = _mm_nt("mm_dact", dr2b, wdown_g, BF16, tm=1024, tn=D_FF // 2, tk=D_MODEL)
    gw_down = _matmul(
        "mm_dw_down", actb, dr2b, (D_FF, D_MODEL), BF16, (N_DEV // 2, D_MODEL // 1024, 2),
        pl.BlockSpec((t // 2, 2 * rs_down), lambda i, j, kk: (kk, i)),
        pl.BlockSpec((t // 2, 1024), lambda i, j, kk: (kk, j)),
        pl.BlockSpec((2 * rs_down, 1024), lambda i, j, kk: (i, j)), nt="tn")
    s_down, tok = scatter_start("a2a_w_down_start", gw_down.reshape(N_DEV, rs_down, D_MODEL))
    dhf, g_fw, g_fb = _ffn_act_bwd("ffn_act_bwd", dact, hf, fw_full, ffn_conv_b + tok[0, 0])
    tm = min(1024, t)
    gw_up = _matmul(
        "mm_dw_up", h1bt, dhf, (D_MODEL, 2 * D_FF), BF16, (D_MODEL // 1024, 2 * D_FF // 512, 1),
        pl.BlockSpec((1024, t), lambda i, j, kk: (i, 0)),
        pl.BlockSpec((1, t, 512), lambda i, j, kk: (j // 11, 0, j % 11)),
        pl.BlockSpec((1024, 512), lambda i, j, kk: (i, j)), nt=False)
    s_up, tok = scatter_start("a2a_w_up_start", gw_up)
    tkf = D_FF // 2
    dh1 = _matmul(
        "mm_dh1", dhf, wup_n, (t, D_MODEL), F32, (t // tm, D_MODEL // 1024, 4),
        pl.BlockSpec((1, tm, tkf), lambda i, j, kk: (kk // 2, i, kk % 2)),
        pl.BlockSpec((1024, tkf), lambda i, j, kk: (j, kk)),
        pl.BlockSpec((tm, 1024), lambda i, j, kk: (i, j)), nt=True, after=tok)
    dr1, dr1b, g_ln1g, g_ln1b = _ln_bwd("ln1_bwd", r1, dr2, dh1, ln1_g + tok[0, 0], ALPHA, True)
    gw_out = _matmul(
        "mm_dw_out", catb, dr1b, (D_MODEL, D_MODEL), BF16, (2, 2, 2),
        pl.BlockSpec((t // 2, 1024), lambda i, j, kk: (kk, i)),
        pl.BlockSpec((t // 2, 1024), lambda i, j, kk: (kk, j)),
        pl.BlockSpec((1024, 1024), lambda i, j, kk: (i, j)), nt="tn")
    s_out, tok = scatter_start("a2a_w_out_start", gw_out.reshape(N_DEV, rs_out, D_MODEL))
    dcat = _mm_nt("mm_dcat", dr1b, wout_g, F32, tm=2048, tn=1024, tk=D_MODEL, after=tok)
    da, dgate, g_cw, g_cb, g_cng, g_cnb = _conv_bwd("conv_bwd", dcat, u1, hin, cw_full, conv_norm_g + tok[0, 0],
                                                    conv_norm_b)
    dq, df, di, dog, g_hg, g_lb = _hgrn_bwd("hgrn_bwd", dcat, hin, o_raw, states, lb_logits, hgrn_norm_g)
    dhin = jnp.concatenate([da, dgate, dq, df, di, dog], axis=1)
    half = D_MODEL // 2
    gw_in_a = _mm_grad_cols("mm_dw_in_a", h0bt, dhin, ns_in, 0, half, after=tok)
    s_in_a, tok = scatter_start("a2a_w_in_a_start", gw_in_a)
    gw_in_b = _mm_grad_cols("mm_dw_in_b", h0bt, dhin, ns_in, half, half, after=tok)
    s_in_b, tok = scatter_start("a2a_w_in_b_start", gw_in_b)
    dh0 = _mm_nt("mm_dh0", dhin, win_n, F32, tm=1024, tn=512, tk=IN_PROJ, after=tok)
    grad_x, g_eg, g_eb = _ln_bwd("ln_in_bwd", x2, dr1, dh0, row(emb_ln_g), ALPHA, False)

    small_shapes = [(D_MODEL,), (D_MODEL,), (CONV_KERNEL, CONV_WIDTH), (1, CONV_WIDTH), (1, CONV_WIDTH),
                    (1, CONV_WIDTH), (2, HGRN_WIDTH), (1, HGRN_WIDTH), (1, D_MODEL), (1, D_MODEL),
                    (FFN_KERNEL, D_FF), (1, D_FF), (1, D_MODEL), (1, D_MODEL), (128,)]
    rows_small = 569
    packed = _pack([g_eg, g_eb, g_cw[:CONV_KERNEL], g_cb, g_cng, g_cnb, g_lb, g_hg, g_ln1g, g_ln1b,
                    g_fw[:FFN_KERNEL], g_fb, g_ln2g, g_ln2b, loss], rows_small)
    h_small, tok = _exchange_start("ag_small_start", packed, _own_slot(packed, me), scatter=False)

    def big(name, handle, after, w, m, v, tr):
        recv = _exchange_wait("a2a_" + name + "_wait", handle, after)
        return [o[None] for o in _adamw_sum("adamw_" + name, recv, w[0], m[0], v[0], tr)]

    u_down = big("w_down", s_down, tok, w_ffn_down, m_w_ffn_down, v_w_ffn_down, 64)
    u_up = big("w_up", s_up, u_down[1], w_ffn_up, m_w_ffn_up, v_w_ffn_up, 64)
    u_out = big("w_out", s_out, u_up[1], w_out, m_w_out, v_w_out, 64)
    summed = _sum_parts("sum_small", _exchange_wait("ag_small_wait", h_small, u_out[1]))
    (s_eg, s_eb, s_cw, s_cb, s_cng, s_cnb, s_lb, s_hg, s_l1g, s_l1b, s_fw, s_fb, s_l2g, s_l2b,
     s_loss) = _unpack(summed, small_shapes)
    s_cw = lax.dynamic_slice_in_dim(s_cw, me * cs, cs, axis=1)[None]
    s_fw = lax.dynamic_slice_in_dim(s_fw, me * fs, fs, axis=1)[None]
    g_small = [s_eg, s_eb, s_cw, s_cb, s_cng, s_cnb, s_lb, s_hg, s_l1g, s_l1b, s_fw, s_fb, s_l2g, s_l2b]
    w_small = [emb_ln_g, emb_ln_b, conv_w, conv_b, conv_norm_g, conv_norm_b, lb_logits, hgrn_norm_g,
               ln1_g, ln1_b, ffn_conv_w, ffn_conv_b, ln2_g, ln2_b]
    m_small = [m_emb_ln_g, m_emb_ln_b, m_conv_w, m_conv_b, m_conv_norm_g, m_conv_norm_b, m_lb_logits,
               m_hgrn_norm_g, m_ln1_g, m_ln1_b, m_ffn_conv_w, m_ffn_conv_b, m_ln2_g, m_ln2_b]
    v_small = [v_emb_ln_g, v_emb_ln_b, v_conv_w, v_conv_b, v_conv_norm_g, v_conv_norm_b, v_lb_logits,
               v_hgrn_norm_g, v_ln1_g, v_ln1_b, v_ffn_conv_w, v_ffn_conv_b, v_ln2_g, v_ln2_b]
    rows_own = 236
    shapes_own = [w.shape for w in w_small]
    upd = _adamw_small("adamw_small", _pack(w_small, rows_own), _pack(g_small, rows_own),
                       _pack(m_small, rows_own), _pack(v_small, rows_own))
    d_small, nm_small, nv_small = (_unpack(u, shapes_own) for u in upd)
    g_small = [g.reshape(s) for g, s in zip(g_small, shapes_own)]

    recv_a = _exchange_wait("a2a_w_in_a_wait", s_in_a, upd[0])
    part = _adamw_sum("adamw_w_in_a", recv_a, w_in[0], m_w_in[0], v_w_in[0], 128)
    recv_b = _exchange_wait("a2a_w_in_b_wait", s_in_b, part[1])
    u_in = [o[None] for o in _adamw_sum("adamw_w_in_b", recv_b, w_in[0], m_w_in[0], v_w_in[0], 128,
                                        row0=half, partial=part)]

    def ordered(small, i_in, i_out, i_up, i_down):
        (eg, eb, cw, cb, cng, cnb, lb, hg, l1g, l1b, fw, fb, l2g, l2b) = small
        return [eg, eb, i_in, cw, cb, cng, cnb, lb, hg, i_out, l1g, l1b, i_up, fw, fb, i_down, l2g, l2b]

    outs = [s_loss[0], grad_x[None]]
    for k, small in enumerate([g_small, d_small, nm_small, nv_small]):
        outs += ordered(small, u_in[k], u_out[k], u_up[k], u_down[k])
    return tuple(outs)
```

```python
import functools

import jax
import jax.numpy as jnp
from jax import lax
from jax.experimental import pallas as pl
from jax.experimental.pallas import tpu as pltpu

F32 = jnp.float32
BF16 = jnp.bfloat16

N_DEV = 8
D_MODEL = 2048
CONV_WIDTH = 1024
CONV_KERNEL = 31
HGRN_WIDTH = 1024
GROUP = 128
N_GROUPS = 8
IN_PROJ = 2 * CONV_WIDTH + 4 * HGRN_WIDTH
D_FF = 5632
FFN_KERNEL = 3
CHUNK = 64
SUB = 8
LN_EPS = 1e-5
RMS_EPS = 1e-6
ALPHA = 2.0 ** 0.25
ADAM_LR, ADAM_B1, ADAM_B2, ADAM_EPS, ADAM_WD, ADAM_STEP = 0.001, 0.9, 0.999, 1e-08, 0.01, 10

VMEM_LIMIT = 56 * 1024 * 1024
MESH = pl.DeviceIdType.MESH


def _cparams(sem=None):
    return pltpu.CompilerParams(dimension_semantics=sem, vmem_limit_bytes=VMEM_LIMIT)


def _sigmoid(x):
    return 0.5 * jnp.tanh(0.5 * x) + 0.5


def _matmul(name, a, b, out_shape, out_dtype, grid, a_spec, b_spec, o_spec, nt, after=None):
    nk = grid[2]
    dims = {True: (((1,), (1,)), ((), ())), False: (((1,), (0,)), ((), ())), "tn": (((0,), (0,)), ((), ()))}[nt]
    extra = [] if after is None else [after]

    def body(a_ref, b_ref, *rest):
        o_ref, *scratch = rest[len(extra):]
        if len(a_ref.shape) == 3 and a_ref.shape[0] > 1:
            kp = a_ref.shape[2]
            part = None
            for p in range(a_ref.shape[0]):
                d = lax.dot_general(a_ref[p], b_ref[:, p * kp:(p + 1) * kp], dims, preferred_element_type=F32)
                part = d if part is None else part + d
        else:
            av = a_ref[0] if len(a_ref.shape) == 3 else a_ref[...]
            bv = b_ref[0] if len(b_ref.shape) == 3 else b_ref[...]
            part = lax.dot_general(av, bv, dims, preferred_element_type=F32)

        def write(res):
            if len(o_ref.shape) == 3:
                o_ref[0] = res.astype(out_dtype)
            else:
                o_ref[...] = res.astype(out_dtype)

        if nk == 1:
            write(part)
            return
        acc_ref, = scratch
        k = pl.program_id(2)

        @pl.when(k == 0)
        def _():
            acc_ref[...] = part

        @pl.when(jnp.logical_and(k > 0, k < nk - 1))
        def _():
            acc_ref[...] += part

        @pl.when(k == nk - 1)
        def _():
            write(acc_ref[...] + part)

    acc_shape = o_spec.block_shape[-2:]
    assert all(g >= 1 for g in grid), (name, grid)
    return pl.pallas_call(
        body, name=name, grid=grid, in_specs=[a_spec, b_spec] + [pl.BlockSpec(memory_space=pl.ANY)] * len(extra),
        out_specs=o_spec, out_shape=jax.ShapeDtypeStruct(out_shape, out_dtype),
        scratch_shapes=[pltpu.VMEM(acc_shape, F32)] if nk > 1 else [],
        compiler_params=_cparams(("parallel", "parallel", "arbitrary")),
    )(a, b, *extra)


def _mm_nn(name, a, w, out_dtype, tm, tn, tk, after=None):
    m, k = a.shape
    tm, tk = min(tm, m), min(tk, k)
    n = w.shape[1]
    return _matmul(
        name, a, w, (m, n), out_dtype, (m // tm, n // tn, k // tk),
        pl.BlockSpec((tm, tk), lambda i, j, kk: (i, kk)),
        pl.BlockSpec((tk, tn), lambda i, j, kk: (kk, j)),
        pl.BlockSpec((tm, tn), lambda i, j, kk: (i, j)), nt=False, after=after)


def _mm_nt(name, a, w, out_dtype, tm, tn, tk, after=None):
    m, k = a.shape
    tm = min(tm, m)
    n = w.shape[0]
    return _matmul(
        name, a, w, (m, n), out_dtype, (m // tm, n // tn, k // tk),
        pl.BlockSpec((tm, tk), lambda i, j, kk: (i, kk)),
        pl.BlockSpec((tn, tk), lambda i, j, kk: (j, kk)),
        pl.BlockSpec((tm, tn), lambda i, j, kk: (i, j)), nt=True, after=after)


def _mm_grad_cols(name, at, b, ns, row0, rows, after, tm=1024, tk=4096):
    t = at.shape[1]
    tk = min(tk, t)
    off = row0 // tm
    return _matmul(
        name, at, b, (N_DEV, rows, ns), BF16, (rows // tm, N_DEV, t // tk),
        pl.BlockSpec((tm, tk), lambda i, j, kk: (i + off, kk)),
        pl.BlockSpec((tk, ns), lambda i, j, kk: (kk, j)),
        pl.BlockSpec((1, tm, ns), lambda i, j, kk: (j, i, 0)), nt=False, after=after)


LN_ROWS = 512


def _ln_stats(r):
    mu = jnp.mean(r, axis=-1, keepdims=True)
    xc = r - mu
    var = jnp.mean(xc * xc, axis=-1, keepdims=True)
    rstd = lax.rsqrt(var + LN_EPS)
    return xc * rstd, rstd


def _row_spec(d):
    return pl.BlockSpec((LN_ROWS, d), lambda i: (i, 0))


def _vec_spec(d):
    return pl.BlockSpec((1, d), lambda i: (0, 0))


def _ln_apply(r, g, b):
    xhat, _ = _ln_stats(r)
    return xhat * g + b


def _ln_fwd(name, a, m, g, b, alpha, pre=None):
    t, d = a.shape
    has_m = m is not None
    pre = list(pre) if pre is not None else []

    def body(*refs):
        a_ref, refs = refs[0], refs[1:]
        av = a_ref[...]
        if pre:
            av = _ln_apply(av, refs[0][...], refs[1][...])
            refs = refs[2:]
        if has_m:
            m_ref, g_ref, b_ref, r_ref, yb_ref, yt_ref = refs
            r = alpha * av + m_ref[...]
            r_ref[...] = r
        else:
            g_ref, b_ref, yb_ref, yt_ref = refs
            r = av
        y = _ln_apply(r, g_ref[...], b_ref[...])
        yb_ref[...] = y.astype(BF16)
        yt_ref[...] = y.T.astype(BF16)

    ins = [a] + pre + ([m] if has_m else []) + [g, b]
    in_specs = [_row_spec(d)] + [_vec_spec(d)] * len(pre) + [_row_spec(d)] * has_m + [_vec_spec(d)] * 2
    outs = ([jax.ShapeDtypeStruct((t, d), F32)] if has_m else []) + [
        jax.ShapeDtypeStruct((t, d), BF16), jax.ShapeDtypeStruct((d, t), BF16)]
    res = pl.pallas_call(
        body, name=name, grid=(t // LN_ROWS,), in_specs=in_specs,
        out_specs=[_row_spec(d)] * (len(outs) - 1) + [pl.BlockSpec((d, LN_ROWS), lambda i: (0, i))], out_shape=outs,
        compiler_params=_cparams(("parallel",)),
    )(*ins)
    return res if has_m else (None, *res)


def _ln_bwd_math(r, dy, g):
    xhat, rstd = _ln_stats(r)
    dxhat = dy * g
    m1 = jnp.mean(dxhat, axis=-1, keepdims=True)
    m2 = jnp.mean(dxhat * xhat, axis=-1, keepdims=True)
    dr = rstd * (dxhat - m1 - xhat * m2)
    return dr, jnp.sum(dy * xhat, axis=0, keepdims=True), jnp.sum(dy, axis=0, keepdims=True)


def _ln2_loss_bwd(name, r1, g1, b1, ffn, g, b, tgt):
    t, d = r1.shape

    def body(r1_ref, g1_ref, b1_ref, f_ref, g_ref, b_ref, t_ref, dr_ref, drb_ref, dg_ref, db_ref, loss_ref):
        @pl.when(pl.program_id(0) == 0)
        def _():
            dg_ref[...] = jnp.zeros_like(dg_ref)
            db_ref[...] = jnp.zeros_like(db_ref)
            loss_ref[...] = jnp.zeros_like(loss_ref)

        r = ALPHA * _ln_apply(r1_ref[...], g1_ref[...], b1_ref[...]) + f_ref[...]
        xhat, _ = _ln_stats(r)
        e = xhat * g_ref[...] + b_ref[...] - t_ref[...]
        loss_ref[...] += 0.5 / d * jnp.sum(e * e)
        dr, dg, db = _ln_bwd_math(r, e * (1.0 / d), g_ref[...])
        dr_ref[...] = dr
        drb_ref[...] = dr.astype(BF16)
        dg_ref[...] += dg
        db_ref[...] += db

    return pl.pallas_call(
        body, name=name, grid=(t // LN_ROWS,),
        in_specs=[_row_spec(d), _vec_spec(d), _vec_spec(d), _row_spec(d), _vec_spec(d), _vec_spec(d), _row_spec(d)],
        out_specs=[_row_spec(d), _row_spec(d), _vec_spec(d), _vec_spec(d), _vec_spec(128)],
        out_shape=[jax.ShapeDtypeStruct((t, d), F32), jax.ShapeDtypeStruct((t, d), BF16),
                   jax.ShapeDtypeStruct((1, d), F32), jax.ShapeDtypeStruct((1, d), F32),
                   jax.ShapeDtypeStruct((1, 128), F32)],
        compiler_params=_cparams(("arbitrary",)),
    )(r1, g1, b1, ffn, g, b, tgt)


def _ln_bwd(name, r, dya, dyb, g, alpha, want_bf16):
    t, d = r.shape

    def body(r_ref, dya_ref, dyb_ref, g_ref, *outs):
        dr_ref = outs[0]
        dg_ref, db_ref = outs[-2:]

        @pl.when(pl.program_id(0) == 0)
        def _():
            dg_ref[...] = jnp.zeros_like(dg_ref)
            db_ref[...] = jnp.zeros_like(db_ref)

        dy = alpha * dya_ref[...] + dyb_ref[...]
        dr, dg, db = _ln_bwd_math(r_ref[...], dy, g_ref[...])
        dr_ref[...] = dr
        if want_bf16:
            outs[1][...] = dr.astype(BF16)
        dg_ref[...] += dg
        db_ref[...] += db

    big = [jax.ShapeDtypeStruct((t, d), F32)] + ([jax.ShapeDtypeStruct((t, d), BF16)] if want_bf16 else [])
    return pl.pallas_call(
        body, name=name, grid=(t // LN_ROWS,),
        in_specs=[_row_spec(d)] * 3 + [_vec_spec(d)],
        out_specs=[_row_spec(d)] * len(big) + [_vec_spec(d)] * 2,
        out_shape=big + [jax.ShapeDtypeStruct((1, d), F32)] * 2,
        compiler_params=_cparams(("arbitrary",)),
    )(r, dya, dyb, g)


CONV_ROWS = 64
CONV_UNROLL = 16
FFN_UNROLL = 4


def _unrolled(n, unroll, fn, init):
    def body(i, carry):
        for u in range(unroll):
            carry = fn(i * unroll + u, carry)
        return carry

    return lax.fori_loop(0, n // unroll, body, init)


def _for_shifted(ref, r0, tm, shifts, fn):
    for s in shifts:
        fn(s, ref[pl.ds(r0 + s, tm), :])


def _col_spec(t, cb, off=0):
    return pl.BlockSpec((t, cb), lambda j: (0, j + off))


def _ffn_act_fwd(name, hf, w, b, cb=128):
    t = hf.shape[0]
    f = hf.shape[1] // 2
    nb = f // cb
    tm = CONV_ROWS

    def body(g_ref, v_ref, w_ref, b_ref, act_ref, pad_ref):
        pad_ref[pl.ds(0, 8), :] = jnp.zeros((8, cb), F32)
        pad_ref[pl.ds(8, t), :] = g_ref[...].astype(F32)
        wv = [w_ref[pl.ds(k, 1), :] for k in range(FFN_KERNEL)]
        bias = b_ref[...]

        def tile(i, carry):
            r0 = pl.multiple_of(i * tm, tm)
            acc = [jnp.broadcast_to(bias, (tm, cb))]

            def tap(s, rows):
                acc[0] = acc[0] + wv[s - 6] * rows

            _for_shifted(pad_ref, r0, tm, (6, 7, 8), tap)
            gc = acc[0]
            act_ref[pl.ds(r0, tm), :] = (gc * _sigmoid(gc) * v_ref[pl.ds(r0, tm), :].astype(F32)).astype(BF16)
            return carry

        _unrolled(t // tm, FFN_UNROLL, tile, 0)

    return pl.pallas_call(
        body, name=name, grid=(nb,),
        in_specs=[_col_spec(t, cb), _col_spec(t, cb, nb),
                  pl.BlockSpec((FFN_KERNEL, cb), lambda j: (0, j)), pl.BlockSpec((1, cb), lambda j: (0, j))],
        out_specs=_col_spec(t, cb), out_shape=jax.ShapeDtypeStruct((t, f), BF16),
        scratch_shapes=[pltpu.VMEM((t + 8, cb), F32)],
        compiler_params=_cparams(("parallel",)),
    )(hf, hf, w, b)


def _ffn_act_bwd(name, dact, hf, w, b, cb=128):
    t = hf.shape[0]
    f = hf.shape[1] // 2
    nb = f // cb
    tm = CONV_ROWS

    def body(da_ref, g_ref, v_ref, w_ref, b_ref, dhf_ref, dw_ref, db_ref, pad_ref, dgc_ref):
        pad_ref[pl.ds(0, 8), :] = jnp.zeros((8, cb), F32)
        pad_ref[pl.ds(8, t), :] = g_ref[...].astype(F32)
        dgc_ref[pl.ds(t, 8), :] = jnp.zeros((8, cb), F32)
        wv = [w_ref[pl.ds(k, 1), :] for k in range(FFN_KERNEL)]
        bias = b_ref[...]

        def tile_a(i, carry):
            r0 = pl.multiple_of(i * tm, tm)
            taps = {}
            _for_shifted(pad_ref, r0, tm, (6, 7, 8), lambda s, rows: taps.__setitem__(s, rows))
            gc = bias + wv[0] * taps[6] + wv[1] * taps[7] + wv[2] * taps[8]
            sg = _sigmoid(gc)
            da = da_ref[pl.ds(r0, tm), :].astype(F32)
            dhf_ref[1, pl.ds(r0, tm), :] = (da * gc * sg).astype(BF16)
            dgc = da * v_ref[pl.ds(r0, tm), :].astype(F32) * sg * (1.0 + gc * (1.0 - sg))
            dgc_ref[pl.ds(r0, tm), :] = dgc
            sums = [jnp.sum(dgc * taps[6 + k], axis=0, keepdims=True) for k in range(3)]
            sums.append(jnp.sum(dgc, axis=0, keepdims=True))
            return tuple(c + s for c, s in zip(carry, sums))

        zero = jnp.zeros((1, cb), F32)
        dw0, dw1, dw2, dbias = _unrolled(t // tm, FFN_UNROLL, tile_a, (zero, zero, zero, zero))
        row = lax.broadcasted_iota(jnp.int32, (8, cb), 0)
        dw_ref[...] = jnp.where(row == 0, dw0, jnp.where(row == 1, dw1, jnp.where(row == 2, dw2, 0.0)))
        db_ref[...] = dbias

        def tile_b(i, carry):
            r0 = pl.multiple_of(i * tm, tm)
            acc = [jnp.zeros((tm, cb), F32)]

            def tap(s, rows):
                acc[0] = acc[0] + wv[2 - s] * rows

            _for_shifted(dgc_ref, r0, tm, (0, 1, 2), tap)
            dhf_ref[0, pl.ds(r0, tm), :] = acc[0].astype(BF16)
            return carry

        lax.fori_loop(0, t // tm, tile_b, 0)

    return pl.pallas_call(
        body, name=name, grid=(nb,),
        in_specs=[_col_spec(t, cb), _col_spec(t, cb), _col_spec(t, cb, nb),
                  pl.BlockSpec((FFN_KERNEL, cb), lambda j: (0, j)), pl.BlockSpec((1, cb), lambda j: (0, j))],
        out_specs=[pl.BlockSpec((2, t, cb), lambda j: (0, 0, j)),
                   pl.BlockSpec((8, cb), lambda j: (0, j)), pl.BlockSpec((1, cb), lambda j: (0, j))],
        out_shape=[jax.ShapeDtypeStruct((2, t, f), BF16), jax.ShapeDtypeStruct((8, f), F32),
                   jax.ShapeDtypeStruct((1, f), F32)],
        scratch_shapes=[pltpu.VMEM((t + 8, cb), F32), pltpu.VMEM((t + 8, cb), F32)],
        compiler_params=_cparams(("parallel",)),
    )(dact, hf, hf, w, b)


def _silu_grad(z, sg):
    return sg * (1.0 + z * (1.0 - sg))


def _conv_fwd(name, hin, w, b, ng, nb_, cat):
    t = hin.shape[0]
    c = GROUP
    tm = CONV_ROWS
    pad = 32
    shifts = tuple(2 + k for k in range(CONV_KERNEL))

    def body(a_ref, gt_ref, w_ref, b_ref, ng_ref, nb_ref, cat_ref, u1_ref, u3_ref, pad_ref):
        pad_ref[pl.ds(0, pad), :] = jnp.zeros((pad, c), F32)
        pad_ref[pl.ds(pad, t), :] = a_ref[...] * _sigmoid(gt_ref[...])
        bias, gam, bet = b_ref[...], ng_ref[...], nb_ref[...]

        def tile(i, carry):
            r0 = pl.multiple_of(i * tm, tm)
            acc = [jnp.broadcast_to(bias, (tm, c))]

            def tap(s, rows):
                acc[0] = acc[0] + w_ref[pl.ds(s - 2, 1), :] * rows

            _for_shifted(pad_ref, r0, tm, shifts, tap)
            u1 = acc[0]
            u1_ref[pl.ds(r0, tm), :] = u1
            xhat, _ = _ln_stats(u1)
            u2 = xhat * gam + bet
            u3_ref[pl.ds(r0, tm), :] = (u2 * _sigmoid(u2)).astype(BF16)
            return carry

        _unrolled(t // tm, CONV_UNROLL, tile, 0)

    vec = pl.BlockSpec((1, c), lambda j: (0, j))
    return pl.pallas_call(
        body, name=name, grid=(N_GROUPS,),
        in_specs=[_col_spec(t, c), _col_spec(t, c, N_GROUPS),
                  pl.BlockSpec((CONV_KERNEL, c), lambda j: (0, j)), vec, vec, vec, ANY],
        out_specs=[_col_spec(t, c), _col_spec(t, c)],
        out_shape=[jax.ShapeDtypeStruct((t, CONV_WIDTH), F32), jax.ShapeDtypeStruct(cat.shape, BF16)],
        input_output_aliases={6: 1},
        scratch_shapes=[pltpu.VMEM((t + pad, c), F32)],
        compiler_params=_cparams(("parallel",)),
    )(hin, hin, w, b, ng, nb_, cat)


def _conv_bwd(name, dcat, u1, hin, w, ng, nb_):
    t = hin.shape[0]
    c = GROUP
    tm = CONV_ROWS
    pad = 32
    nk = CONV_KERNEL

    def body(du3_ref, u1_ref, a_ref, gt_ref, w_ref, ng_ref, nb_ref,
             da_ref, dgt_ref, dw_ref, db_ref, dng_ref, dnb_ref, u0_ref, du1_ref, dwp_ref):
        u0_ref[pl.ds(0, pad), :] = jnp.zeros((pad, c), F32)
        u0_ref[pl.ds(pad, t), :] = a_ref[...] * _sigmoid(gt_ref[...])
        du1_ref[pl.ds(t, pad), :] = jnp.zeros((pad, c), F32)
        dwp_ref[...] = jnp.zeros_like(dwp_ref)
        gam, bet = ng_ref[...], nb_ref[...]

        def tile_a(i, carry):
            r0 = pl.multiple_of(i * tm, tm)
            u1 = u1_ref[pl.ds(r0, tm), :]
            xhat, rstd = _ln_stats(u1)
            u2 = xhat * gam + bet
            sg = _sigmoid(u2)
            du2 = du3_ref[pl.ds(r0, tm), :] * _silu_grad(u2, sg)
            dxhat = du2 * gam
            m1 = jnp.mean(dxhat, axis=-1, keepdims=True)
            m2 = jnp.mean(dxhat * xhat, axis=-1, keepdims=True)
            du1 = rstd * (dxhat - m1 - xhat * m2)
            du1_ref[pl.ds(r0, tm), :] = du1
            sums = (jnp.sum(du1, axis=0, keepdims=True), jnp.sum(du2 * xhat, axis=0, keepdims=True),
                    jnp.sum(du2, axis=0, keepdims=True))
            return tuple(x + s for x, s in zip(carry, sums))

        zero = jnp.zeros((1, c), F32)
        dbias, dgam, dbet = _unrolled(t // tm, CONV_UNROLL, tile_a, (zero, zero, zero))
        db_ref[...] = dbias
        dng_ref[...] = dgam
        dnb_ref[...] = dbet

        def tile_b(i, carry):
            r0 = pl.multiple_of(i * tm, tm)
            du1 = du1_ref[pl.ds(r0, tm), :]
            acc = [jnp.zeros((tm, c), F32)]

            def tap_dx(s, rows):
                acc[0] = acc[0] + w_ref[pl.ds(nk - 1 - s, 1), :] * rows

            _for_shifted(du1_ref, r0, tm, tuple(range(nk)), tap_dx)

            def tap_dw(s, rows):
                part = (du1 * rows).reshape(tm // 8, 8, c).sum(axis=0)
                dwp_ref[s - 2] = dwp_ref[s - 2] + part

            _for_shifted(u0_ref, r0, tm, tuple(2 + k for k in range(nk)), tap_dw)
            du0 = acc[0]
            a = a_ref[pl.ds(r0, tm), :]
            sg = _sigmoid(gt_ref[pl.ds(r0, tm), :])
            da_ref[pl.ds(r0, tm), :] = (du0 * sg).astype(BF16)
            dgt_ref[pl.ds(r0, tm), :] = (du0 * a * sg * (1.0 - sg)).astype(BF16)
            return carry

        lax.fori_loop(0, t // tm, tile_b, 0)
        dw_ref[...] = jnp.sum(dwp_ref[...], axis=1)

    vec = pl.BlockSpec((1, c), lambda j: (0, j))
    vshape = jax.ShapeDtypeStruct((1, CONV_WIDTH), F32)
    return pl.pallas_call(
        body, name=name, grid=(N_GROUPS,),
        in_specs=[_col_spec(t, c), _col_spec(t, c), _col_spec(t, c), _col_spec(t, c, N_GROUPS),
                  pl.BlockSpec((nk, c), lambda j: (0, j)), vec, vec],
        out_specs=[_col_spec(t, c), _col_spec(t, c), pl.BlockSpec((32, c), lambda j: (0, j)), vec, vec, vec],
        out_shape=[jax.ShapeDtypeStruct((t, CONV_WIDTH), BF16), jax.ShapeDtypeStruct((t, CONV_WIDTH), BF16),
                   jax.ShapeDtypeStruct((32, CONV_WIDTH), F32), vshape, vshape, vshape],
        scratch_shapes=[pltpu.VMEM((t + pad, c), F32), pltpu.VMEM((t + pad, c), F32),
                        pltpu.VMEM((32, 8, c), F32)],
        compiler_params=_cparams(("parallel",)),
    )(dcat, u1, hin, hin, w, ng, nb_)


LEVELS = (64, 32, 16)
HGRN_UNROLL = 8
HGRN_UNROLL_FWD = 16
NT_DIMS = (((1,), (1,)), ((), ()))
NN_DIMS = (((1,), (0,)), ((), ()))
TN_DIMS = (((0,), (0,)), ((), ()))


def _bdot(a, b, dims):
    return lax.dot_general(a.astype(BF16), b.astype(BF16), dims, preferred_element_type=F32)


def _hdot(a, b):
    return jnp.dot(a, b, precision=lax.Precision.HIGHEST, preferred_element_type=F32)


def _chunk_consts():
    rid = lax.broadcasted_iota(jnp.int32, (CHUNK, GROUP), 0)
    ti = lax.broadcasted_iota(jnp.int32, (CHUNK, CHUNK), 0)
    si = lax.broadcasted_iota(jnp.int32, (CHUNK, CHUNK), 1)
    tri = (si <= ti).astype(F32)
    second = [(rid & (b // 2)) != 0 for b in LEVELS]
    same = [None] + [(ti // b) == (si // b) for b in LEVELS[1:]]
    sub = lax.broadcasted_iota(jnp.int32, (SUB, GROUP), 0)
    return rid, tri, second, same, sub


def _level_refs(cum_ref, rid, base):
    row = lambda i: cum_ref[pl.ds(base + i, 1), :]
    l1 = jnp.broadcast_to(row(31), (CHUNK, GROUP))
    l2 = jnp.where(rid < 32, row(15), row(47))
    l3 = jnp.where(rid < 16, row(7), jnp.where(rid < 32, row(23), jnp.where(rid < 48, row(39), row(55))))
    return l1, l2, l3


def _level_factors(cum, brefs, second):
    out = []
    for bref, sec in zip(brefs, second):
        eq = jnp.where(sec, jnp.exp(jnp.minimum(cum - bref, 0.0)), 0.0)
        ek = jnp.where(sec, 0.0, jnp.exp(jnp.minimum(bref - cum, 0.0)))
        out.append((eq, ek))
    return out


def _gates(q, f, lb):
    sq = _sigmoid(q)
    sf = _sigmoid(f)
    fg = lb + (1.0 - lb) * sf
    return q * sq, sq, sf, fg


def _hgrn_specs(t, nc):
    c = GROUP
    col = lambda off: pl.BlockSpec((t, c), lambda h: (0, h + off))
    hin_specs = [col(16), col(24), col(32), col(40)]
    vec = pl.BlockSpec((1, c), lambda h: (0, h))
    lbs = pl.BlockSpec((2, c), lambda h: (0, h))
    st = pl.BlockSpec((1, nc, c, c), lambda h: (h, 0, 0, 0))
    return col, hin_specs, vec, lbs, st


def _hgrn_fwd(name, hin, lb_logits, hg):
    t = hin.shape[0]
    nc = t // CHUNK
    c = GROUP
    col, hin_specs, vec, lbs, st = _hgrn_specs(t, nc)

    def body(q_ref, f_ref, v_ref, og_ref, lb_ref, hg_ref, o_ref, ob_ref, st_ref,
             s_ref, cum_ref, kk_ref, vc_ref):
        rid, tri, second, same, sub = _chunk_consts()
        lb = _sigmoid(lb_ref[pl.ds(0, 1), :] - lb_ref[pl.ds(1, 1), :])
        gain = hg_ref[...]
        s_ref[...] = jnp.zeros_like(s_ref)

        def chunk(ci, u):
            base = u * CHUNK
            r0 = pl.multiple_of(ci * CHUNK, CHUNK)
            rows = pl.ds(r0, CHUNK)
            qh, _, _, fg = _gates(q_ref[rows, :], f_ref[rows, :], lb)
            v = v_ref[rows, :]
            kk = 1.0 - fg
            cum = _hdot(tri, jnp.log(fg))
            cum_ref[pl.ds(base, CHUNK), :] = cum
            kk_ref[pl.ds(base, CHUNK), :] = kk
            vc_ref[pl.ds(base, CHUNK), :] = v
            sprev = s_ref[...]
            st_ref[0, ci] = sprev
            blast = cum_ref[pl.ds(base + CHUNK - 1, 1), :]
            o = _bdot(qh * jnp.exp(cum), sprev, NT_DIMS)
            s_ref[...] = sprev * jnp.exp(blast) + _bdot(v, kk * jnp.exp(blast - cum), TN_DIMS)
            a = None
            for (eq, ek), msk in zip(_level_factors(cum, _level_refs(cum_ref, rid, base), second), same):
                al = _bdot(qh * eq, kk * ek, NT_DIMS)
                al = al if msk is None else jnp.where(msk, al, 0.0)
                a = al if a is None else a + al
            o = o + _bdot(a, v, NN_DIMS)
            diag = []
            for sb in range(CHUNK // SUB):
                lo = sb * SUB
                qb = qh[lo:lo + SUB]
                cb = cum[lo:lo + SUB]
                od = jnp.zeros((SUB, c), F32)
                for s in range(SUB):
                    e = jnp.where(sub >= s, jnp.exp(jnp.minimum(cb - cum_ref[pl.ds(base + lo + s, 1), :], 0.0)), 0.0)
                    acol = jnp.sum(qb * e * kk_ref[pl.ds(base + lo + s, 1), :], axis=-1, keepdims=True)
                    od = od + acol * vc_ref[pl.ds(base + lo + s, 1), :]
                diag.append(od)
            o = o + jnp.concatenate(diag, axis=0)
            o_ref[rows, :] = o
            y = o * lax.rsqrt(jnp.mean(o * o, axis=-1, keepdims=True) + RMS_EPS) * gain
            og = og_ref[rows, :]
            ob_ref[rows, :] = (y * og * _sigmoid(og)).astype(BF16)

        def chunks(i, carry):
            for u in range(HGRN_UNROLL_FWD):
                chunk(i * HGRN_UNROLL_FWD + u, u)
            return carry

        lax.fori_loop(0, nc // HGRN_UNROLL_FWD, chunks, 0)

    return pl.pallas_call(
        body, name=name, grid=(N_GROUPS,),
        in_specs=hin_specs + [lbs, vec],
        out_specs=[col(0), col(N_GROUPS), st],
        out_shape=[jax.ShapeDtypeStruct((t, HGRN_WIDTH), F32), jax.ShapeDtypeStruct((t, CONV_WIDTH + HGRN_WIDTH), BF16),
                   jax.ShapeDtypeStruct((N_GROUPS, nc, c, c), F32)],
        scratch_shapes=[pltpu.VMEM((c, c), F32)] + [pltpu.VMEM((HGRN_UNROLL_FWD * CHUNK, c), F32)] * 3,
        compiler_params=_cparams(("parallel",)),
    )(hin, hin, hin, hin, lb_logits, hg)


def _hgrn_bwd(name, dcat, hin, o_raw, states, lb_logits, hg):
    t = hin.shape[0]
    nc = t // CHUNK
    c = GROUP
    col, hin_specs, vec, lbs, st = _hgrn_specs(t, nc)

    def body(do_ref, q_ref, f_ref, v_ref, og_ref, o_ref, st_ref, lb_ref, hg_ref,
             dq_ref, df_ref, dv_ref, dog_ref, dhg_ref, dlb_ref,
             ds_ref, cum_ref, kk_ref, vc_ref, qh_ref_s, do_ref_s):
        rid, tri, second, same, sub = _chunk_consts()
        trit = tri.T
        lb = _sigmoid(lb_ref[pl.ds(0, 1), :] - lb_ref[pl.ds(1, 1), :])
        gain = hg_ref[...]
        ds_ref[...] = jnp.zeros_like(ds_ref)

        def chunk(i, carry, u):
            base = u * CHUNK
            dhg, dlb = carry
            ci = nc - 1 - i
            r0 = pl.multiple_of(ci * CHUNK, CHUNK)
            rows = pl.ds(r0, CHUNK)
            q = q_ref[rows, :]
            qh, sq, sf, fg = _gates(q, f_ref[rows, :], lb)
            v = v_ref[rows, :]
            kk = 1.0 - fg
            cum = _hdot(tri, jnp.log(fg))
            cum_ref[pl.ds(base, CHUNK), :] = cum
            kk_ref[pl.ds(base, CHUNK), :] = kk
            vc_ref[pl.ds(base, CHUNK), :] = v
            qh_ref_s[pl.ds(base, CHUNK), :] = qh
            o = o_ref[rows, :]
            og = og_ref[rows, :]
            sg = _sigmoid(og)
            rinv = lax.rsqrt(jnp.mean(o * o, axis=-1, keepdims=True) + RMS_EPS)
            yn = o * rinv
            dof = do_ref[rows, :]
            dog_ref[rows, :] = (dof * yn * gain * _silu_grad(og, sg)).astype(BF16)
            dz = dof * og * sg
            dhg = dhg + jnp.sum(dz * yn, axis=0, keepdims=True)
            dy = dz * gain
            do = rinv * (dy - yn * jnp.mean(dy * yn, axis=-1, keepdims=True))
            do_ref_s[pl.ds(base, CHUNK), :] = do
            sprev = st_ref[0, ci]
            dsn = ds_ref[...]
            blast = cum_ref[pl.ds(base + CHUNK - 1, 1), :]
            eq0 = jnp.exp(cum)
            ek0 = jnp.exp(blast - cum)
            dqh = _bdot(do, sprev, NN_DIMS) * eq0
            dkk = _bdot(v, dsn, NN_DIMS) * ek0
            dlast = (jnp.sum(kk * dkk, axis=0, keepdims=True)
                     + jnp.exp(blast) * jnp.sum(dsn * sprev, axis=0, keepdims=True))
            dv = _bdot(kk * ek0, dsn, NT_DIMS)
            ds_ref[...] = dsn * jnp.exp(blast) + _bdot(do, qh * eq0, TN_DIMS)
            dg = qh * dqh - kk * dkk
            da = _bdot(do, v, NT_DIMS)
            a = None
            for (eq, ek), msk in zip(_level_factors(cum, _level_refs(cum_ref, rid, base), second), same):
                ql, kl = (qh * eq).astype(BF16), (kk * ek).astype(BF16)
                al = _bdot(ql, kl, NT_DIMS)
                dal = da
                if msk is not None:
                    al = jnp.where(msk, al, 0.0)
                    dal = jnp.where(msk, da, 0.0)
                a = al if a is None else a + al
                dql = _bdot(dal, kl, NN_DIMS)
                dkl = _bdot(dal, ql, TN_DIMS)
                dqh = dqh + dql * eq
                dkk = dkk + dkl * ek
                dg = dg + (ql.astype(F32) * dql - kl.astype(F32) * dkl)
            dv = dv + _bdot(a, do, TN_DIMS)
            dq_d, dk_d, dv_d = [], [], []
            for sb in range(CHUNK // SUB):
                lo = sb * SUB
                cb = cum[lo:lo + SUB]
                kb = kk[lo:lo + SUB]
                vb = v[lo:lo + SUB]
                dob = do[lo:lo + SUB]
                dqb = jnp.zeros((SUB, c), F32)
                dkb = jnp.zeros((SUB, c), F32)
                dvb = jnp.zeros((SUB, c), F32)
                for s in range(SUB):
                    crow = cum_ref[pl.ds(base + lo + s, 1), :]
                    e = jnp.where(sub >= s, jnp.exp(jnp.minimum(cb - crow, 0.0)), 0.0)
                    dacol = jnp.sum(dob * vc_ref[pl.ds(base + lo + s, 1), :], axis=-1, keepdims=True)
                    dqb = dqb + dacol * (kk_ref[pl.ds(base + lo + s, 1), :] * e)
                    et = jnp.where(sub <= s, jnp.exp(jnp.minimum(crow - cb, 0.0)), 0.0)
                    dorow = do_ref_s[pl.ds(base + lo + s, 1), :]
                    qe = qh_ref_s[pl.ds(base + lo + s, 1), :] * et
                    dkb = dkb + jnp.sum(vb * dorow, axis=-1, keepdims=True) * qe
                    dvb = dvb + jnp.sum(kb * qe, axis=-1, keepdims=True) * dorow
                dq_d.append(dqb)
                dk_d.append(dkb)
                dv_d.append(dvb)
            dq_d = jnp.concatenate(dq_d, axis=0)
            dk_d = jnp.concatenate(dk_d, axis=0)
            dqh = dqh + dq_d
            dkk = dkk + dk_d
            dg = dg + (qh * dq_d - kk * dk_d)
            dv = dv + jnp.concatenate(dv_d, axis=0)
            dlf = _hdot(trit, dg) + dlast
            dfg = dlf / fg - dkk
            df_ref[rows, :] = (dfg * (1.0 - lb) * sf * (1.0 - sf)).astype(BF16)
            dlb = dlb + jnp.sum(dfg * (1.0 - sf), axis=0, keepdims=True)
            dq_ref[rows, :] = (dqh * _silu_grad(q, sq)).astype(BF16)
            dv_ref[rows, :] = dv.astype(BF16)
            return dhg, dlb

        def chunks(i, carry):
            for u in range(HGRN_UNROLL):
                carry = chunk(i * HGRN_UNROLL + u, carry, u)
            return carry

        zero = jnp.zeros((1, c), F32)
        dhg, dlb = lax.fori_loop(0, nc // HGRN_UNROLL, chunks, (zero, zero))
        dhg_ref[...] = dhg
        dl0 = dlb * lb * (1.0 - lb)
        dlb_ref[...] = jnp.where(lax.broadcasted_iota(jnp.int32, (2, c), 0) == 0, dl0, -dl0)

    big = jax.ShapeDtypeStruct((t, HGRN_WIDTH), BF16)
    return pl.pallas_call(
        body, name=name, grid=(N_GROUPS,),
        in_specs=[col(8)] + hin_specs + [col(0), st, lbs, vec],
        out_specs=[col(0)] * 4 + [vec, lbs],
        out_shape=[big] * 4 + [jax.ShapeDtypeStruct((1, HGRN_WIDTH), F32), jax.ShapeDtypeStruct((2, HGRN_WIDTH), F32)],
        scratch_shapes=[pltpu.VMEM((c, c), F32)] + [pltpu.VMEM((HGRN_UNROLL * CHUNK, c), F32)] * 5,
        compiler_params=_cparams(("parallel",)),
    )(dcat, hin, hin, hin, hin, o_raw, states, lb_logits, hg)


ANY = pl.BlockSpec(memory_space=pl.ANY)


def _my_place():
    return lax.axis_index("x"), lax.axis_index("y"), lax.axis_index("c")


HBM = pl.BlockSpec(memory_space=pltpu.HBM)
SEM = pl.BlockSpec(memory_space=pltpu.SEMAPHORE)
EFFECT = pltpu.SideEffectType.DATAFLOW_SIDE_EFFECTING


def _peer(k):
    x, y, c = _my_place()
    px = 1 - x if k & 4 else x
    py = 1 - y if k & 2 else y
    pc = 1 - c if k & 1 else c
    return (px, py, pc), 4 * px + 2 * py + pc


def _slot(land_ref, idx):
    if len(land_ref.shape) == 2:
        ns = land_ref.shape[1] // N_DEV
        return land_ref.at[:, pl.ds(pl.multiple_of(idx * ns, 128), ns)]
    return land_ref.at[idx]


def _exchange_copy(k, src_ref, land_ref, send_sems, recv_sems, scatter, landing):
    x, y, c = _my_place()
    me = 4 * x + 2 * y + c
    to, idx = _peer(k)
    return pltpu.make_async_remote_copy(
        src_ref=_slot(src_ref, idx) if scatter else src_ref,
        dst_ref=_slot(land_ref, idx) if landing else _slot(land_ref, me),
        send_sem=send_sems.at[k - 1], recv_sem=recv_sems.at[k - 1], device_id=to, device_id_type=MESH)


ALL_PEERS = tuple(range(1, N_DEV))
NEAR_PEERS = (1, 2, 4, 6)
SAME_CORE_PEERS = (2, 4, 6)


def _exchange_start(name, src, land, scatter, ks=ALL_PEERS):
    def body(src_ref, land_ref, send_sems, recv_sems, src_thru, land_thru, token):
        for k in ks:
            _exchange_copy(k, src_ref, land_ref, send_sems, recv_sems, scatter, landing=False).start()
        token[...] = jnp.zeros_like(token)

    send_sems, recv_sems, src_thru, land_thru, token = pl.pallas_call(
        body, name=name,
        out_shape=(pltpu.SemaphoreType.DMA((N_DEV - 1,)), pltpu.SemaphoreType.DMA((N_DEV - 1,)),
                   pltpu.HBM(src.shape, src.dtype), pltpu.HBM(land.shape, land.dtype),
                   jax.ShapeDtypeStruct((8, 128), F32)),
        in_specs=(HBM, HBM), out_specs=(SEM, SEM, HBM, HBM, pl.BlockSpec(memory_space=pltpu.VMEM)),
        input_output_aliases={0: 2, 1: 3},
        compiler_params=pltpu.CompilerParams(has_side_effects=EFFECT),
    )(pltpu.with_memory_space_constraint(src, pltpu.HBM), pltpu.with_memory_space_constraint(land, pltpu.HBM))
    return (send_sems, recv_sems, src_thru, land_thru, scatter, ks), token


def _exchange_wait(name, handle, after):
    send_sems, recv_sems, src_thru, land_thru, scatter, ks = handle

    def body(src_ref, land_ref, send_sems, recv_sems, after_ref, src_dead, got_ref):
        for k in ks:
            cp = _exchange_copy(k, src_ref, land_ref, send_sems, recv_sems, scatter, landing=True)
            cp.wait_send()
            cp.wait_recv()

    return pl.pallas_call(
        body, name=name,
        out_shape=(pltpu.HBM(src_thru.shape, src_thru.dtype), pltpu.HBM(land_thru.shape, land_thru.dtype)),
        in_specs=(HBM, HBM, SEM, SEM, ANY), out_specs=(HBM, HBM), input_output_aliases={0: 0, 1: 1},
        compiler_params=pltpu.CompilerParams(has_side_effects=EFFECT),
    )(src_thru, land_thru, send_sems, recv_sems, after)[1]


def _relay_copy(j, land_ref, send_sems, recv_sems, landing):
    x, y, c = _my_place()
    k = SAME_CORE_PEERS[j]
    _, sent = _peer(k)
    _, got = _peer(k + 1)
    return pltpu.make_async_remote_copy(
        src_ref=_slot(land_ref, sent), dst_ref=_slot(land_ref, got) if landing else _slot(land_ref, sent),
        send_sem=send_sems.at[j], recv_sem=recv_sems.at[j], device_id=(x, y, 1 - c), device_id_type=MESH)


def _relay_start(name, land):
    n = len(SAME_CORE_PEERS)

    def body(land_ref, send_sems, recv_sems, land_thru, token):
        for j in range(n):
            _relay_copy(j, land_ref, send_sems, recv_sems, landing=False).start()
        token[...] = jnp.zeros_like(token)

    send_sems, recv_sems, land_thru, token = pl.pallas_call(
        body, name=name,
        out_shape=(pltpu.SemaphoreType.DMA((n,)), pltpu.SemaphoreType.DMA((n,)),
                   pltpu.HBM(land.shape, land.dtype), jax.ShapeDtypeStruct((8, 128), F32)),
        in_specs=(HBM,), out_specs=(SEM, SEM, HBM, pl.BlockSpec(memory_space=pltpu.VMEM)),
        input_output_aliases={0: 2},
        compiler_params=pltpu.CompilerParams(has_side_effects=EFFECT),
    )(pltpu.with_memory_space_constraint(land, pltpu.HBM))
    return (send_sems, recv_sems, land_thru), token


def _relay_wait(name, handle, after):
    send_sems, recv_sems, land_thru = handle

    def body(land_ref, send_sems, recv_sems, after_ref, got_ref):
        for j in range(len(SAME_CORE_PEERS)):
            cp = _relay_copy(j, land_ref, send_sems, recv_sems, landing=True)
            cp.wait_send()
            cp.wait_recv()

    return pl.pallas_call(
        body, name=name, out_shape=pltpu.HBM(land_thru.shape, land_thru.dtype),
        in_specs=(HBM, SEM, SEM, ANY), out_specs=HBM, input_output_aliases={0: 0},
        compiler_params=pltpu.CompilerParams(has_side_effects=EFFECT),
    )(land_thru, send_sems, recv_sems, after)


def _own_cols(name, own, me):
    r, ns = own.shape
    tr = 256

    def body(me_ref, own_ref, land_ref):
        land_ref[...] = own_ref[...]

    return pl.pallas_call(
        body, name=name,
        grid_spec=pltpu.PrefetchScalarGridSpec(
            num_scalar_prefetch=1, grid=(r // tr,),
            in_specs=[pl.BlockSpec((tr, ns), lambda i, me_ref: (i, 0))],
            out_specs=pl.BlockSpec((tr, ns), lambda i, me_ref: (i, me_ref[0]))),
        out_shape=jax.ShapeDtypeStruct((r, N_DEV * ns), own.dtype),
    )(jnp.reshape(me, (1,)).astype(jnp.int32), own)


def _own_slot(own, me):
    land = lax.empty((N_DEV,) + own.shape, own.dtype)
    return lax.dynamic_update_slice_in_dim(land, own[None], me, axis=0)


def _adamw_math(w, g, m, v):
    m = ADAM_B1 * m + (1.0 - ADAM_B1) * g
    v = ADAM_B2 * v + (1.0 - ADAM_B2) * (g * g)
    m_hat = m / (1.0 - ADAM_B1 ** ADAM_STEP)
    v_hat = v / (1.0 - ADAM_B2 ** ADAM_STEP)
    delta = -ADAM_LR * (m_hat / (jnp.sqrt(v_hat) + ADAM_EPS) + ADAM_WD * w)
    return delta, m, v


def _adamw_sum(name, recv, w, m, v, tr, row0=0, partial=None):
    r, c = w.shape
    rr = recv.shape[1]
    off = row0 // tr

    def body(recv_ref, w_ref, m_ref, v_ref, *refs):
        g_ref, d_ref, mo_ref, vo_ref = refs[-4:]
        g = recv_ref[0].astype(F32)
        for j in range(1, N_DEV):
            g = g + recv_ref[j].astype(F32)
        g_ref[...] = g
        d_ref[...], mo_ref[...], vo_ref[...] = _adamw_math(w_ref[...], g, m_ref[...], v_ref[...])

    tile = pl.BlockSpec((tr, c), lambda i: (i + off, 0))
    out = jax.ShapeDtypeStruct((r, c), F32)
    prev = list(partial) if partial is not None else []
    return pl.pallas_call(
        body, name=name, grid=(rr // tr,),
        in_specs=[pl.BlockSpec((N_DEV, tr, c), lambda i: (0, i, 0)), tile, tile, tile] + [ANY] * len(prev),
        out_specs=[tile] * 4, out_shape=[out] * 4,
        input_output_aliases={4 + i: i for i in range(len(prev))},
        compiler_params=_cparams(("parallel",)),
    )(recv, w, m, v, *prev)


def _sum_parts(name, parts):
    _, r, c = parts.shape

    def body(p_ref, o_ref):
        acc = p_ref[0]
        for j in range(1, N_DEV):
            acc = acc + p_ref[j]
        o_ref[...] = acc

    return pl.pallas_call(body, name=name, out_shape=jax.ShapeDtypeStruct((r, c), F32),
                          compiler_params=_cparams())(parts)


def _adamw_small(name, w, g, m, v):
    def body(w_ref, g_ref, m_ref, v_ref, d_ref, mo_ref, vo_ref):
        d_ref[...], mo_ref[...], vo_ref[...] = _adamw_math(w_ref[...], g_ref[...], m_ref[...], v_ref[...])

    out = jax.ShapeDtypeStruct(w.shape, F32)
    return pl.pallas_call(body, name=name, out_shape=[out] * 3, compiler_params=_cparams())(w, g, m, v)


def _pack(pieces, rows):
    flat = jnp.concatenate([p.reshape(-1).astype(F32) for p in pieces])
    return jnp.pad(flat, (0, rows * 128 - flat.shape[0])).reshape(rows, 128)


def _unpack(packed, shapes):
    flat = packed.reshape(-1)
    out, off = [], 0
    for s in shapes:
        n = 1
        for d in s:
            n *= d
        out.append(flat[off:off + n].reshape(s))
        off += n
    return out


def kernel(x, emb_ln_g, emb_ln_b, w_in, conv_w, conv_b, conv_norm_g, conv_norm_b, lb_logits, hgrn_norm_g, w_out, ln1_g, ln1_b, w_ffn_up, ffn_conv_w, ffn_conv_b, w_ffn_down, ln2_g, ln2_b, loss_target, m_emb_ln_g, m_emb_ln_b, m_w_in, m_conv_w, m_conv_b, m_conv_norm_g, m_conv_norm_b, m_lb_logits, m_hgrn_norm_g, m_w_out, m_ln1_g, m_ln1_b, m_w_ffn_up, m_ffn_conv_w, m_ffn_conv_b, m_w_ffn_down, m_ln2_g, m_ln2_b, v_emb_ln_g, v_emb_ln_b, v_w_in, v_conv_w, v_conv_b, v_conv_norm_g, v_conv_norm_b, v_lb_logits, v_hgrn_norm_g, v_w_out, v_ln1_g, v_ln1_b, v_w_ffn_up, v_ffn_conv_w, v_ffn_conv_b, v_w_ffn_down, v_ln2_g, v_ln2_b):
    t = x.shape[1]
    me = 4 * lax.axis_index("x") + 2 * lax.axis_index("y") + lax.axis_index("c")
    x2, tgt = x[0], loss_target[0]
    ns_in, ns_up = w_in.shape[2], w_ffn_up.shape[2]
    rs_out, rs_down = w_out.shape[1], w_ffn_down.shape[1]
    cs, fs = conv_w.shape[2], ffn_conv_w.shape[2]

    def gather_start(name, w, prev, ks=ALL_PEERS, cols=False):
        shard = (w[0] + prev).astype(BF16)
        land = _own_cols(name.replace("ag_", "own_"), shard, me) if cols else _own_slot(shard, me)
        return _exchange_start(name, shard, land, scatter=False, ks=ks)

    h_in, tok = gather_start("ag_w_in_start", w_in, 0.0, NEAR_PEERS, cols=True)
    taps = _pack([conv_w[0], ffn_conv_w[0]], 48) + tok[0, 0]
    h_taps, tok = _exchange_start("ag_taps_start", taps, _own_slot(taps, me), scatter=False)
    h_out, tok = gather_start("ag_w_out_start", w_out, tok[0, 0])
    h_up, tok = gather_start("ag_w_up_start", w_ffn_up, tok[0, 0], NEAR_PEERS, cols=True)
    h_down, tok = gather_start("ag_w_down_start", w_ffn_down, tok[0, 0])

    row = lambda a: a.reshape(1, -1)

    _, h0b, h0bt = _ln_fwd("ln_in", x2, None, row(emb_ln_g) + tok[0, 0], row(emb_ln_b), 1.0)
    h_relay, tok_relay = _relay_start("ag_w_in_relay_start", _exchange_wait("ag_w_in_wait", h_in, h0b))
    win_n = _relay_wait("ag_w_in_relay_wait", h_relay, tok_relay)
    hin = _mm_nn("mm_in", h0b, win_n, F32, tm=2048, tn=ns_in, tk=D_MODEL)
    n_cw, n_fw = CONV_KERNEL * cs, FFN_KERNEL * fs
    taps_g = _exchange_wait("ag_taps_wait", h_taps, hin).reshape(N_DEV, -1)
    cw_full = taps_g[:, :n_cw].reshape(N_DEV, CONV_KERNEL, cs).transpose(1, 0, 2).reshape(CONV_KERNEL, CONV_WIDTH)
    fw_full = taps_g[:, n_cw:n_cw + n_fw].reshape(N_DEV, FFN_KERNEL, fs).transpose(1, 0, 2).reshape(FFN_KERNEL, D_FF)

    o_raw, cat_right, states = _hgrn_fwd("hgrn_fwd", hin, lb_logits, hgrn_norm_g)
    u1, catb = _conv_fwd("conv_fwd", hin, cw_full, conv_b, conv_norm_g, conv_norm_b, cat_right)
    wout_g = _exchange_wait("ag_w_out_wait", h_out, catb).reshape(D_MODEL, D_MODEL)
    h_up_relay, tok = _relay_start("ag_w_up_relay_start", _exchange_wait("ag_w_up_wait", h_up, wout_g))
    mix = _mm_nn("mm_out", catb, wout_g, F32, tm=2048, tn=1024, tk=D_MODEL, after=tok)
    r1, h1b, h1bt = _ln_fwd("ln1", x2, mix, ln1_g, ln1_b, ALPHA, pre=(row(emb_ln_g), row(emb_ln_b)))
    wup_n = _relay_wait("ag_w_up_relay_wait", h_up_relay, h1b)
    hf = _mm_nn("mm_up", h1b, wup_n, BF16, tm=1024, tn=1024, tk=D_MODEL)
    actb = _ffn_act_fwd("ffn_act", hf, fw_full, ffn_conv_b)
    wdown_g = _exchange_wait("ag_w_down_wait", h_down, actb).reshape(D_FF, D_MODEL)
    ffn = _mm_nn("mm_down", actb, wdown_g, F32, tm=1024, tn=512, tk=D_FF)
    dr2, dr2b, g_ln2g, g_ln2b, loss = _ln2_loss_bwd("ln2_loss", r1, ln1_g, ln1_b, ffn, ln2_g, ln2_b, tgt)

    def scatter_start(name, parts):
        if parts.ndim == 2:
            ns = parts.shape[1] // N_DEV
            own = lax.dynamic_slice_in_dim(parts, me * ns, ns, axis=1)
        else:
            own = lax.dynamic_index_in_dim(parts, me, axis=0, keepdims=False)
        return _exchange_start(name, parts, _own_slot(own, me), scatter=True)

    dact = _mm_nt("mm_dact", dr2b, wdown_g, BF16, tm=1024, tn=D_FF // 2, tk=D_MODEL)
    gw_down = _matmul(
        "mm_dw_down", actb, dr2b, (D_FF, D_MODEL), BF16, (N_DEV // 2, D_MODEL // 1024, 2),
        pl.BlockSpec((t // 2, 2 * rs_down), lambda i, j, kk: (kk, i)),
        pl.BlockSpec((t // 2, 1024), lambda i, j, kk: (kk, j)),
        pl.BlockSpec((2 * rs_down, 1024), lambda i, j, kk: (i, j)), nt="tn")
    s_down, tok = scatter_start("a2a_w_down_start", gw_down.reshape(N_DEV, rs_down, D_MODEL))
    dhf, g_fw, g_fb = _ffn_act_bwd("ffn_act_bwd", dact, hf, fw_full, ffn_conv_b + tok[0, 0])
    tm = min(1024, t)
    gw_up = _matmul(
        "mm_dw_up", h1bt, dhf, (D_MODEL, 2 * D_FF), BF16, (D_MODEL // 1024, 2 * D_FF // 512, 1),
        pl.BlockSpec((1024, t), lambda i, j, kk: (i, 0)),
        pl.BlockSpec((1, t, 512), lambda i, j, kk: (j // 11, 0, j % 11)),
        pl.BlockSpec((1024, 512), lambda i, j, kk: (i, j)), nt=False)
    s_up, tok = scatter_start("a2a_w_up_start", gw_up)
    tkf = D_FF // 2
    dh1 = _matmul(
        "mm_dh1", dhf, wup_n, (t, D_MODEL), F32, (t // tm, D_MODEL // 1024, 4),
        pl.BlockSpec((1, tm, tkf), lambda i, j, kk: (kk // 2, i, kk % 2)),
        pl.BlockSpec((1024, tkf), lambda i, j, kk: (j, kk)),
        pl.BlockSpec((tm, 1024), lambda i, j, kk: (i, j)), nt=True, after=tok)
    dr1, dr1b, g_ln1g, g_ln1b = _ln_bwd("ln1_bwd", r1, dr2, dh1, ln1_g + tok[0, 0], ALPHA, True)
    gw_out = _matmul(
        "mm_dw_out", catb, dr1b, (D_MODEL, D_MODEL), BF16, (2, 2, 2),
        pl.BlockSpec((t // 2, 1024), lambda i, j, kk: (kk, i)),
        pl.BlockSpec((t // 2, 1024), lambda i, j, kk: (kk, j)),
        pl.BlockSpec((1024, 1024), lambda i, j, kk: (i, j)), nt="tn")
    s_out, tok = scatter_start("a2a_w_out_start", gw_out.reshape(N_DEV, rs_out, D_MODEL))
    dcat = _mm_nt("mm_dcat", dr1b, wout_g, F32, tm=2048, tn=1024, tk=D_MODEL, after=tok)
    da, dgate, g_cw, g_cb, g_cng, g_cnb = _conv_bwd("conv_bwd", dcat, u1, hin, cw_full, conv_norm_g + tok[0, 0],
                                                    conv_norm_b)
    dq, df, di, dog, g_hg, g_lb = _hgrn_bwd("hgrn_bwd", dcat, hin, o_raw, states, lb_logits, hgrn_norm_g)
    dhin = jnp.concatenate([da, dgate, dq, df, di, dog], axis=1)
    half = D_MODEL // 2
    gw_in_a = _mm_grad_cols("mm_dw_in_a", h0bt, dhin, ns_in, 0, half, after=tok)
    s_in_a, tok = scatter_start("a2a_w_in_a_start", gw_in_a)
    gw_in_b = _mm_grad_cols("mm_dw_in_b", h0bt, dhin, ns_in, half, half, after=tok)
    s_in_b, tok = scatter_start("a2a_w_in_b_start", gw_in_b)
    dh0 = _mm_nt("mm_dh0", dhin, win_n, F32, tm=1024, tn=512, tk=IN_PROJ, after=tok)
    grad_x, g_eg, g_eb = _ln_bwd("ln_in_bwd", x2, dr1, dh0, row(emb_ln_g), ALPHA, False)

    small_shapes = [(D_MODEL,), (D_MODEL,), (CONV_KERNEL, CONV_WIDTH), (1, CONV_WIDTH), (1, CONV_WIDTH),
                    (1, CONV_WIDTH), (2, HGRN_WIDTH), (1, HGRN_WIDTH), (1, D_MODEL), (1, D_MODEL),
                    (FFN_KERNEL, D_FF), (1, D_FF), (1, D_MODEL), (1, D_MODEL), (128,)]
    rows_small = 569
    packed = _pack([g_eg, g_eb, g_cw[:CONV_KERNEL], g_cb, g_cng, g_cnb, g_lb, g_hg, g_ln1g, g_ln1b,
                    g_fw[:FFN_KERNEL], g_fb, g_ln2g, g_ln2b, loss], rows_small)
    h_small, tok = _exchange_start("ag_small_start", packed, _own_slot(packed, me), scatter=False)

    def big(name, handle, after, w, m, v, tr):
        recv = _exchange_wait("a2a_" + name + "_wait", handle, after)
        return [o[None] for o in _adamw_sum("adamw_" + name, recv, w[0], m[0], v[0], tr)]

    u_down = big("w_down", s_down, tok, w_ffn_down, m_w_ffn_down, v_w_ffn_down, 64)
    u_up = big("w_up", s_up, u_down[1], w_ffn_up, m_w_ffn_up, v_w_ffn_up, 64)
    u_out = big("w_out", s_out, u_up[1], w_out, m_w_out, v_w_out, 64)
    summed = _sum_parts("sum_small", _exchange_wait("ag_small_wait", h_small, u_out[1]))
    (s_eg, s_eb, s_cw, s_cb, s_cng, s_cnb, s_lb, s_hg, s_l1g, s_l1b, s_fw, s_fb, s_l2g, s_l2b,
     s_loss) = _unpack(summed, small_shapes)
    s_cw = lax.dynamic_slice_in_dim(s_cw, me * cs, cs, axis=1)[None]
    s_fw = lax.dynamic_slice_in_dim(s_fw, me * fs, fs, axis=1)[None]
    g_small = [s_eg, s_eb, s_cw, s_cb, s_cng, s_cnb, s_lb, s_hg, s_l1g, s_l1b, s_fw, s_fb, s_l2g, s_l2b]
    w_small = [emb_ln_g, emb_ln_b, conv_w, conv_b, conv_norm_g, conv_norm_b, lb_logits, hgrn_norm_g,
               ln1_g, ln1_b, ffn_conv_w, ffn_conv_b, ln2_g, ln2_b]
    m_small = [m_emb_ln_g, m_emb_ln_b, m_conv_w, m_conv_b, m_conv_norm_g, m_conv_norm_b, m_lb_logits,
               m_hgrn_norm_g, m_ln1_g, m_ln1_b, m_ffn_conv_w, m_ffn_conv_b, m_ln2_g, m_ln2_b]
    v_small = [v_emb_ln_g, v_emb_ln_b, v_conv_w, v_conv_b, v_conv_norm_g, v_conv_norm_b, v_lb_logits,
               v_hgrn_norm_g, v_ln1_g, v_ln1_b, v_ffn_conv_w, v_ffn_conv_b, v_ln2_g, v_ln2_b]
    rows_own = 236
    shapes_own = [w.shape for w in w_small]
    upd = _adamw_small("adamw_small", _pack(w_small, rows_own), _pack(g_small, rows_own),
                       _pack(m_small, rows_own), _pack(v_small, rows_own))
    d_small, nm_small, nv_small = (_unpack(u, shapes_own) for u in upd)
    g_small = [g.reshape(s) for g, s in zip(g_small, shapes_own)]

    recv_a = _exchange_wait("a2a_w_in_a_wait", s_in_a, upd[0])
    part = _adamw_sum("adamw_w_in_a", recv_a, w_in[0], m_w_in[0], v_w_in[0], 128)
    recv_b = _exchange_wait("a2a_w_in_b_wait", s_in_b, part[1])
    u_in = [o[None] for o in _adamw_sum("adamw_w_in_b", recv_b, w_in[0], m_w_in[0], v_w_in[0], 128,
                                        row0=half, partial=part)]

    def ordered(small, i_in, i_out, i_up, i_down):
        (eg, eb, cw, cb, cng, cnb, lb, hg, l1g, l1b, fw, fb, l2g, l2b) = small
        return [eg, eb, i_in, cw, cb, cng, cnb, lb, hg, i_out, l1g, l1b, i_up, fw, fb, i_down, l2g, l2b]

    outs = [s_loss[0], grad_x[None]]
    for k, small in enumerate([g_small, d_small, nm_small, nv_small]):
        outs += ordered(small, u_in[k], u_out[k], u_up[k], u_down[k])
    return tuple(outs)
```

```python
import functools

import jax
import jax.numpy as jnp
from jax import lax
from jax.experimental import pallas as pl
from jax.experimental.pallas import tpu as pltpu

F32 = jnp.float32
BF16 = jnp.bfloat16

N_DEV = 8
D_MODEL = 2048
CONV_WIDTH = 1024
CONV_KERNEL = 31
HGRN_WIDTH = 1024
GROUP = 128
N_GROUPS = 8
IN_PROJ = 2 * CONV_WIDTH + 4 * HGRN_WIDTH
D_FF = 5632
FFN_KERNEL = 3
CHUNK = 64
SUB = 8
LN_EPS = 1e-5
RMS_EPS = 1e-6
ALPHA = 2.0 ** 0.25
ADAM_LR, ADAM_B1, ADAM_B2, ADAM_EPS, ADAM_WD, ADAM_STEP = 0.001, 0.9, 0.999, 1e-08, 0.01, 10

VMEM_LIMIT = 56 * 1024 * 1024
MESH = pl.DeviceIdType.MESH


def _cparams(sem=None):
    return pltpu.CompilerParams(dimension_semantics=sem, vmem_limit_bytes=VMEM_LIMIT)


def _sigmoid(x):
    return 0.5 * jnp.tanh(0.5 * x) + 0.5


def _matmul(name, a, b, out_shape, out_dtype, grid, a_spec, b_spec, o_spec, nt, after=None):
    nk = grid[2]
    dims = {True: (((1,), (1,)), ((), ())), False: (((1,), (0,)), ((), ())), "tn": (((0,), (0,)), ((), ()))}[nt]
    extra = [] if after is None else [after]

    def body(a_ref, b_ref, *rest):
        o_ref, *scratch = rest[len(extra):]
        if len(a_ref.shape) == 3 and a_ref.shape[0] > 1:
            kp = a_ref.shape[2]
            part = None
            for p in range(a_ref.shape[0]):
                d = lax.dot_general(a_ref[p], b_ref[:, p * kp:(p + 1) * kp], dims, preferred_element_type=F32)
                part = d if part is None else part + d
        else:
            av = a_ref[0] if len(a_ref.shape) == 3 else a_ref[...]
            bv = b_ref[0] if len(b_ref.shape) == 3 else b_ref[...]
            part = lax.dot_general(av, bv, dims, preferred_element_type=F32)

        def write(res):
            if len(o_ref.shape) == 3:
                o_ref[0] = res.astype(out_dtype)
            else:
                o_ref[...] = res.astype(out_dtype)

        if nk == 1:
            write(part)
            return
        acc_ref, = scratch
        k = pl.program_id(2)

        @pl.when(k == 0)
        def _():
            acc_ref[...] = part

        @pl.when(jnp.logical_and(k > 0, k < nk - 1))
        def _():
            acc_ref[...] += part

        @pl.when(k == nk - 1)
        def _():
            write(acc_ref[...] + part)

    acc_shape = o_spec.block_shape[-2:]
    assert all(g >= 1 for g in grid), (name, grid)
    return pl.pallas_call(
        body, name=name, grid=grid, in_specs=[a_spec, b_spec] + [pl.BlockSpec(memory_space=pl.ANY)] * len(extra),
        out_specs=o_spec, out_shape=jax.ShapeDtypeStruct(out_shape, out_dtype),
        scratch_shapes=[pltpu.VMEM(acc_shape, F32)] if nk > 1 else [],
        compiler_params=_cparams(("parallel", "parallel", "arbitrary")),
    )(a, b, *extra)


def _mm_nn(name, a, w, out_dtype, tm, tn, tk, after=None):
    m, k = a.shape
    tm, tk = min(tm, m), min(tk, k)
    n = w.shape[1]
    return _matmul(
        name, a, w, (m, n), out_dtype, (m // tm, n // tn, k // tk),
        pl.BlockSpec((tm, tk), lambda i, j, kk: (i, kk)),
        pl.BlockSpec((tk, tn), lambda i, j, kk: (kk, j)),
        pl.BlockSpec((tm, tn), lambda i, j, kk: (i, j)), nt=False, after=after)


def _mm_nt(name, a, w, out_dtype, tm, tn, tk, after=None):
    m, k = a.shape
    tm = min(tm, m)
    n = w.shape[0]
    return _matmul(
        name, a, w, (m, n), out_dtype, (m // tm, n // tn, k // tk),
        pl.BlockSpec((tm, tk), lambda i, j, kk: (i, kk)),
        pl.BlockSpec((tn, tk), lambda i, j, kk: (j, kk)),
        pl.BlockSpec((tm, tn), lambda i, j, kk: (i, j)), nt=True, after=after)


def _mm_grad_cols(name, at, b, ns, row0, rows, after, tm=1024, tk=4096):
    t = at.shape[1]
    tk = min(tk, t)
    off = row0 // tm
    return _matmul(
        name, at, b, (N_DEV, rows, ns), BF16, (rows // tm, N_DEV, t // tk),
        pl.BlockSpec((tm, tk), lambda i, j, kk: (i + off, kk)),
        pl.BlockSpec((tk, ns), lambda i, j, kk: (kk, j)),
        pl.BlockSpec((1, tm, ns), lambda i, j, kk: (j, i, 0)), nt=False, after=after)


LN_ROWS = 512


def _ln_stats(r):
    mu = jnp.mean(r, axis=-1, keepdims=True)
    xc = r - mu
    var = jnp.mean(xc * xc, axis=-1, keepdims=True)
    rstd = lax.rsqrt(var + LN_EPS)
    return xc * rstd, rstd


def _row_spec(d):
    return pl.BlockSpec((LN_ROWS, d), lambda i: (i, 0))


def _vec_spec(d):
    return pl.BlockSpec((1, d), lambda i: (0, 0))


def _ln_apply(r, g, b):
    xhat, _ = _ln_stats(r)
    return xhat * g + b


def _ln_fwd(name, a, m, g, b, alpha, pre=None):
    t, d = a.shape
    has_m = m is not None
    pre = list(pre) if pre is not None else []

    def body(*refs):
        a_ref, refs = refs[0], refs[1:]
        av = a_ref[...]
        if pre:
            av = _ln_apply(av, refs[0][...], refs[1][...])
            refs = refs[2:]
        if has_m:
            m_ref, g_ref, b_ref, r_ref, yb_ref, yt_ref = refs
            r = alpha * av + m_ref[...]
            r_ref[...] = r
        else:
            g_ref, b_ref, yb_ref, yt_ref = refs
            r = av
        y = _ln_apply(r, g_ref[...], b_ref[...])
        yb_ref[...] = y.astype(BF16)
        yt_ref[...] = y.T.astype(BF16)

    ins = [a] + pre + ([m] if has_m else []) + [g, b]
    in_specs = [_row_spec(d)] + [_vec_spec(d)] * len(pre) + [_row_spec(d)] * has_m + [_vec_spec(d)] * 2
    outs = ([jax.ShapeDtypeStruct((t, d), F32)] if has_m else []) + [
        jax.ShapeDtypeStruct((t, d), BF16), jax.ShapeDtypeStruct((d, t), BF16)]
    res = pl.pallas_call(
        body, name=name, grid=(t // LN_ROWS,), in_specs=in_specs,
        out_specs=[_row_spec(d)] * (len(outs) - 1) + [pl.BlockSpec((d, LN_ROWS), lambda i: (0, i))], out_shape=outs,
        compiler_params=_cparams(("parallel",)),
    )(*ins)
    return res if has_m else (None, *res)


def _ln_bwd_math(r, dy, g):
    xhat, rstd = _ln_stats(r)
    dxhat = dy * g
    m1 = jnp.mean(dxhat, axis=-1, keepdims=True)
    m2 = jnp.mean(dxhat * xhat, axis=-1, keepdims=True)
    dr = rstd * (dxhat - m1 - xhat * m2)
    return dr, jnp.sum(dy * xhat, axis=0, keepdims=True), jnp.sum(dy, axis=0, keepdims=True)


def _ln2_loss_bwd(name, r1, g1, b1, ffn, g, b, tgt):
    t, d = r1.shape

    def body(r1_ref, g1_ref, b1_ref, f_ref, g_ref, b_ref, t_ref, dr_ref, drb_ref, dg_ref, db_ref, loss_ref):
        @pl.when(pl.program_id(0) == 0)
        def _():
            dg_ref[...] = jnp.zeros_like(dg_ref)
            db_ref[...] = jnp.zeros_like(db_ref)
            loss_ref[...] = jnp.zeros_like(loss_ref)

        r = ALPHA * _ln_apply(r1_ref[...], g1_ref[...], b1_ref[...]) + f_ref[...]
        xhat, _ = _ln_stats(r)
        e = xhat * g_ref[...] + b_ref[...] - t_ref[...]
        loss_ref[...] += 0.5 / d * jnp.sum(e * e)
        dr, dg, db = _ln_bwd_math(r, e * (1.0 / d), g_ref[...])
        dr_ref[...] = dr
        drb_ref[...] = dr.astype(BF16)
        dg_ref[...] += dg
        db_ref[...] += db

    return pl.pallas_call(
        body, name=name, grid=(t // LN_ROWS,),
        in_specs=[_row_spec(d), _vec_spec(d), _vec_spec(d), _row_spec(d), _vec_spec(d), _vec_spec(d), _row_spec(d)],
        out_specs=[_row_spec(d), _row_spec(d), _vec_spec(d), _vec_spec(d), _vec_spec(128)],
        out_shape=[jax.ShapeDtypeStruct((t, d), F32), jax.ShapeDtypeStruct((t, d), BF16),
                   jax.ShapeDtypeStruct((1, d), F32), jax.ShapeDtypeStruct((1, d), F32),
                   jax.ShapeDtypeStruct((1, 128), F32)],
        compiler_params=_cparams(("arbitrary",)),
    )(r1, g1, b1, ffn, g, b, tgt)


def _ln_bwd(name, r, dya, dyb, g, alpha, want_bf16):
    t, d = r.shape

    def body(r_ref, dya_ref, dyb_ref, g_ref, *outs):
        dr_ref = outs[0]
        dg_ref, db_ref = outs[-2:]

        @pl.when(pl.program_id(0) == 0)
        def _():
            dg_ref[...] = jnp.zeros_like(dg_ref)
            db_ref[...] = jnp.zeros_like(db_ref)

        dy = alpha * dya_ref[...] + dyb_ref[...]
        dr, dg, db = _ln_bwd_math(r_ref[...], dy, g_ref[...])
        dr_ref[...] = dr
        if want_bf16:
            outs[1][...] = dr.astype(BF16)
        dg_ref[...] += dg
        db_ref[...] += db

    big = [jax.ShapeDtypeStruct((t, d), F32)] + ([jax.ShapeDtypeStruct((t, d), BF16)] if want_bf16 else [])
    return pl.pallas_call(
        body, name=name, grid=(t // LN_ROWS,),
        in_specs=[_row_spec(d)] * 3 + [_vec_spec(d)],
        out_specs=[_row_spec(d)] * len(big) + [_vec_spec(d)] * 2,
        out_shape=big + [jax.ShapeDtypeStruct((1, d), F32)] * 2,
        compiler_params=_cparams(("arbitrary",)),
    )(r, dya, dyb, g)


CONV_ROWS = 64
CONV_UNROLL = 8
FFN_UNROLL = 4


def _unrolled(n, unroll, fn, init):
    def body(i, carry):
        for u in range(unroll):
            carry = fn(i * unroll + u, carry)
        return carry

    return lax.fori_loop(0, n // unroll, body, init)


def _for_shifted(ref, r0, tm, shifts, fn):
    for s in shifts:
        fn(s, ref[pl.ds(r0 + s, tm), :])


def _col_spec(t, cb, off=0):
    return pl.BlockSpec((t, cb), lambda j: (0, j + off))


def _ffn_act_fwd(name, hf, w, b, cb=128):
    t = hf.shape[0]
    f = hf.shape[1] // 2
    nb = f // cb
    tm = CONV_ROWS

    def body(g_ref, v_ref, w_ref, b_ref, act_ref, pad_ref):
        pad_ref[pl.ds(0, 8), :] = jnp.zeros((8, cb), F32)
        pad_ref[pl.ds(8, t), :] = g_ref[...].astype(F32)
        wv = [w_ref[pl.ds(k, 1), :] for k in range(FFN_KERNEL)]
        bias = b_ref[...]

        def tile(i, carry):
            r0 = pl.multiple_of(i * tm, tm)
            acc = [jnp.broadcast_to(bias, (tm, cb))]

            def tap(s, rows):
                acc[0] = acc[0] + wv[s - 6] * rows

            _for_shifted(pad_ref, r0, tm, (6, 7, 8), tap)
            gc = acc[0]
            act_ref[pl.ds(r0, tm), :] = (gc * _sigmoid(gc) * v_ref[pl.ds(r0, tm), :].astype(F32)).astype(BF16)
            return carry

        _unrolled(t // tm, FFN_UNROLL, tile, 0)

    return pl.pallas_call(
        body, name=name, grid=(nb,),
        in_specs=[_col_spec(t, cb), _col_spec(t, cb, nb),
                  pl.BlockSpec((FFN_KERNEL, cb), lambda j: (0, j)), pl.BlockSpec((1, cb), lambda j: (0, j))],
        out_specs=_col_spec(t, cb), out_shape=jax.ShapeDtypeStruct((t, f), BF16),
        scratch_shapes=[pltpu.VMEM((t + 8, cb), F32)],
        compiler_params=_cparams(("parallel",)),
    )(hf, hf, w, b)


def _ffn_act_bwd(name, dact, hf, w, b, cb=128):
    t = hf.shape[0]
    f = hf.shape[1] // 2
    nb = f // cb
    tm = CONV_ROWS

    def body(da_ref, g_ref, v_ref, w_ref, b_ref, dhf_ref, dw_ref, db_ref, pad_ref, dgc_ref):
        pad_ref[pl.ds(0, 8), :] = jnp.zeros((8, cb), F32)
        pad_ref[pl.ds(8, t), :] = g_ref[...].astype(F32)
        dgc_ref[pl.ds(t, 8), :] = jnp.zeros((8, cb), F32)
        wv = [w_ref[pl.ds(k, 1), :] for k in range(FFN_KERNEL)]
        bias = b_ref[...]

        def tile_a(i, carry):
            r0 = pl.multiple_of(i * tm, tm)
            taps = {}
            _for_shifted(pad_ref, r0, tm, (6, 7, 8), lambda s, rows: taps.__setitem__(s, rows))
            gc = bias + wv[0] * taps[6] + wv[1] * taps[7] + wv[2] * taps[8]
            sg = _sigmoid(gc)
            da = da_ref[pl.ds(r0, tm), :].astype(F32)
            dhf_ref[1, pl.ds(r0, tm), :] = (da * gc * sg).astype(BF16)
            dgc = da * v_ref[pl.ds(r0, tm), :].astype(F32) * sg * (1.0 + gc * (1.0 - sg))
            dgc_ref[pl.ds(r0, tm), :] = dgc
            sums = [jnp.sum(dgc * taps[6 + k], axis=0, keepdims=True) for k in range(3)]
            sums.append(jnp.sum(dgc, axis=0, keepdims=True))
            return tuple(c + s for c, s in zip(carry, sums))

        zero = jnp.zeros((1, cb), F32)
        dw0, dw1, dw2, dbias = _unrolled(t // tm, FFN_UNROLL, tile_a, (zero, zero, zero, zero))
        row = lax.broadcasted_iota(jnp.int32, (8, cb), 0)
        dw_ref[...] = jnp.where(row == 0, dw0, jnp.where(row == 1, dw1, jnp.where(row == 2, dw2, 0.0)))
        db_ref[...] = dbias

        def tile_b(i, carry):
            r0 = pl.multiple_of(i * tm, tm)
            acc = [jnp.zeros((tm, cb), F32)]

            def tap(s, rows):
                acc[0] = acc[0] + wv[2 - s] * rows

            _for_shifted(dgc_ref, r0, tm, (0, 1, 2), tap)
            dhf_ref[0, pl.ds(r0, tm), :] = acc[0].astype(BF16)
            return carry

        lax.fori_loop(0, t // tm, tile_b, 0)

    return pl.pallas_call(
        body, name=name, grid=(nb,),
        in_specs=[_col_spec(t, cb), _col_spec(t, cb), _col_spec(t, cb, nb),
                  pl.BlockSpec((FFN_KERNEL, cb), lambda j: (0, j)), pl.BlockSpec((1, cb), lambda j: (0, j))],
        out_specs=[pl.BlockSpec((2, t, cb), lambda j: (0, 0, j)),
                   pl.BlockSpec((8, cb), lambda j: (0, j)), pl.BlockSpec((1, cb), lambda j: (0, j))],
        out_shape=[jax.ShapeDtypeStruct((2, t, f), BF16), jax.ShapeDtypeStruct((8, f), F32),
                   jax.ShapeDtypeStruct((1, f), F32)],
        scratch_shapes=[pltpu.VMEM((t + 8, cb), F32), pltpu.VMEM((t + 8, cb), F32)],
        compiler_params=_cparams(("parallel",)),
    )(dact, hf, hf, w, b)


def _silu_grad(z, sg):
    return sg * (1.0 + z * (1.0 - sg))


def _conv_fwd(name, hin, w, b, ng, nb_, cat):
    t = hin.shape[0]
    c = GROUP
    tm = CONV_ROWS
    pad = 32
    shifts = tuple(2 + k for k in range(CONV_KERNEL))

    def body(a_ref, gt_ref, w_ref, b_ref, ng_ref, nb_ref, cat_ref, u1_ref, u3_ref, pad_ref):
        pad_ref[pl.ds(0, pad), :] = jnp.zeros((pad, c), F32)
        pad_ref[pl.ds(pad, t), :] = a_ref[...] * _sigmoid(gt_ref[...])
        bias, gam, bet = b_ref[...], ng_ref[...], nb_ref[...]

        def tile(i, carry):
            r0 = pl.multiple_of(i * tm, tm)
            acc = [jnp.broadcast_to(bias, (tm, c))]

            def tap(s, rows):
                acc[0] = acc[0] + w_ref[pl.ds(s - 2, 1), :] * rows

            _for_shifted(pad_ref, r0, tm, shifts, tap)
            u1 = acc[0]
            u1_ref[pl.ds(r0, tm), :] = u1
            xhat, _ = _ln_stats(u1)
            u2 = xhat * gam + bet
            u3_ref[pl.ds(r0, tm), :] = (u2 * _sigmoid(u2)).astype(BF16)
            return carry

        _unrolled(t // tm, CONV_UNROLL, tile, 0)

    vec = pl.BlockSpec((1, c), lambda j: (0, j))
    return pl.pallas_call(
        body, name=name, grid=(N_GROUPS,),
        in_specs=[_col_spec(t, c), _col_spec(t, c, N_GROUPS),
                  pl.BlockSpec((CONV_KERNEL, c), lambda j: (0, j)), vec, vec, vec, ANY],
        out_specs=[_col_spec(t, c), _col_spec(t, c)],
        out_shape=[jax.ShapeDtypeStruct((t, CONV_WIDTH), F32), jax.ShapeDtypeStruct(cat.shape, BF16)],
        input_output_aliases={6: 1},
        scratch_shapes=[pltpu.VMEM((t + pad, c), F32)],
        compiler_params=_cparams(("parallel",)),
    )(hin, hin, w, b, ng, nb_, cat)


def _conv_bwd(name, dcat, u1, hin, w, ng, nb_):
    t = hin.shape[0]
    c = GROUP
    tm = CONV_ROWS
    pad = 32
    nk = CONV_KERNEL

    def body(du3_ref, u1_ref, a_ref, gt_ref, w_ref, ng_ref, nb_ref,
             da_ref, dgt_ref, dw_ref, db_ref, dng_ref, dnb_ref, u0_ref, du1_ref, dwp_ref):
        u0_ref[pl.ds(0, pad), :] = jnp.zeros((pad, c), F32)
        u0_ref[pl.ds(pad, t), :] = a_ref[...] * _sigmoid(gt_ref[...])
        du1_ref[pl.ds(t, pad), :] = jnp.zeros((pad, c), F32)
        dwp_ref[...] = jnp.zeros_like(dwp_ref)
        gam, bet = ng_ref[...], nb_ref[...]

        def tile_a(i, carry):
            r0 = pl.multiple_of(i * tm, tm)
            u1 = u1_ref[pl.ds(r0, tm), :]
            xhat, rstd = _ln_stats(u1)
            u2 = xhat * gam + bet
            sg = _sigmoid(u2)
            du2 = du3_ref[pl.ds(r0, tm), :] * _silu_grad(u2, sg)
            dxhat = du2 * gam
            m1 = jnp.mean(dxhat, axis=-1, keepdims=True)
            m2 = jnp.mean(dxhat * xhat, axis=-1, keepdims=True)
            du1 = rstd * (dxhat - m1 - xhat * m2)
            du1_ref[pl.ds(r0, tm), :] = du1
            sums = (jnp.sum(du1, axis=0, keepdims=True), jnp.sum(du2 * xhat, axis=0, keepdims=True),
                    jnp.sum(du2, axis=0, keepdims=True))
            return tuple(x + s for x, s in zip(carry, sums))

        zero = jnp.zeros((1, c), F32)
        dbias, dgam, dbet = _unrolled(t // tm, CONV_UNROLL, tile_a, (zero, zero, zero))
        db_ref[...] = dbias
        dng_ref[...] = dgam
        dnb_ref[...] = dbet

        def tile_b(i, carry):
            r0 = pl.multiple_of(i * tm, tm)
            du1 = du1_ref[pl.ds(r0, tm), :]
            acc = [jnp.zeros((tm, c), F32)]

            def tap_dx(s, rows):
                acc[0] = acc[0] + w_ref[pl.ds(nk - 1 - s, 1), :] * rows

            _for_shifted(du1_ref, r0, tm, tuple(range(nk)), tap_dx)

            def tap_dw(s, rows):
                part = (du1 * rows).reshape(tm // 8, 8, c).sum(axis=0)
                dwp_ref[s - 2] = dwp_ref[s - 2] + part

            _for_shifted(u0_ref, r0, tm, tuple(2 + k for k in range(nk)), tap_dw)
            du0 = acc[0]
            a = a_ref[pl.ds(r0, tm), :]
            sg = _sigmoid(gt_ref[pl.ds(r0, tm), :])
            da_ref[pl.ds(r0, tm), :] = (du0 * sg).astype(BF16)
            dgt_ref[pl.ds(r0, tm), :] = (du0 * a * sg * (1.0 - sg)).astype(BF16)
            return carry

        lax.fori_loop(0, t // tm, tile_b, 0)
        dw_ref[...] = jnp.sum(dwp_ref[...], axis=1)

    vec = pl.BlockSpec((1, c), lambda j: (0, j))
    vshape = jax.ShapeDtypeStruct((1, CONV_WIDTH), F32)
    return pl.pallas_call(
        body, name=name, grid=(N_GROUPS,),
        in_specs=[_col_spec(t, c), _col_spec(t, c), _col_spec(t, c), _col_spec(t, c, N_GROUPS),
                  pl.BlockSpec((nk, c), lambda j: (0, j)), vec, vec],
        out_specs=[_col_spec(t, c), _col_spec(t, c), pl.BlockSpec((32, c), lambda j: (0, j)), vec, vec, vec],
        out_shape=[jax.ShapeDtypeStruct((t, CONV_WIDTH), BF16), jax.ShapeDtypeStruct((t, CONV_WIDTH), BF16),
                   jax.ShapeDtypeStruct((32, CONV_WIDTH), F32), vshape, vshape, vshape],
        scratch_shapes=[pltpu.VMEM((t + pad, c), F32), pltpu.VMEM((t + pad, c), F32),
                        pltpu.VMEM((32, 8, c), F32)],
        compiler_params=_cparams(("parallel",)),
    )(dcat, u1, hin, hin, w, ng, nb_)


LEVELS = (64, 32, 16)
HGRN_UNROLL = 8
HGRN_UNROLL_FWD = 16
NT_DIMS = (((1,), (1,)), ((), ()))
NN_DIMS = (((1,), (0,)), ((), ()))
TN_DIMS = (((0,), (0,)), ((), ()))


def _bdot(a, b, dims):
    return lax.dot_general(a.astype(BF16), b.astype(BF16), dims, preferred_element_type=F32)


def _hdot(a, b):
    return jnp.dot(a, b, precision=lax.Precision.HIGHEST, preferred_element_type=F32)


def _chunk_consts():
    rid = lax.broadcasted_iota(jnp.int32, (CHUNK, GROUP), 0)
    ti = lax.broadcasted_iota(jnp.int32, (CHUNK, CHUNK), 0)
    si = lax.broadcasted_iota(jnp.int32, (CHUNK, CHUNK), 1)
    tri = (si <= ti).astype(F32)
    second = [(rid & (b // 2)) != 0 for b in LEVELS]
    same = [None] + [(ti // b) == (si // b) for b in LEVELS[1:]]
    sub = lax.broadcasted_iota(jnp.int32, (SUB, GROUP), 0)
    return rid, tri, second, same, sub


def _level_refs(cum_ref, rid, base):
    row = lambda i: cum_ref[pl.ds(base + i, 1), :]
    l1 = jnp.broadcast_to(row(31), (CHUNK, GROUP))
    l2 = jnp.where(rid < 32, row(15), row(47))
    l3 = jnp.where(rid < 16, row(7), jnp.where(rid < 32, row(23), jnp.where(rid < 48, row(39), row(55))))
    return l1, l2, l3


def _level_factors(cum, brefs, second):
    out = []
    for bref, sec in zip(brefs, second):
        eq = jnp.where(sec, jnp.exp(jnp.minimum(cum - bref, 0.0)), 0.0)
        ek = jnp.where(sec, 0.0, jnp.exp(jnp.minimum(bref - cum, 0.0)))
        out.append((eq, ek))
    return out


def _gates(q, f, lb):
    sq = _sigmoid(q)
    sf = _sigmoid(f)
    fg = lb + (1.0 - lb) * sf
    return q * sq, sq, sf, fg


def _hgrn_specs(t, nc):
    c = GROUP
    col = lambda off: pl.BlockSpec((t, c), lambda h: (0, h + off))
    hin_specs = [col(16), col(24), col(32), col(40)]
    vec = pl.BlockSpec((1, c), lambda h: (0, h))
    lbs = pl.BlockSpec((2, c), lambda h: (0, h))
    st = pl.BlockSpec((1, nc, c, c), lambda h: (h, 0, 0, 0))
    return col, hin_specs, vec, lbs, st


def _hgrn_fwd(name, hin, lb_logits, hg):
    t = hin.shape[0]
    nc = t // CHUNK
    c = GROUP
    col, hin_specs, vec, lbs, st = _hgrn_specs(t, nc)

    def body(q_ref, f_ref, v_ref, og_ref, lb_ref, hg_ref, o_ref, ob_ref, st_ref,
             s_ref, cum_ref, kk_ref, vc_ref):
        rid, tri, second, same, sub = _chunk_consts()
        lb = _sigmoid(lb_ref[pl.ds(0, 1), :] - lb_ref[pl.ds(1, 1), :])
        gain = hg_ref[...]
        s_ref[...] = jnp.zeros_like(s_ref)

        def chunk(ci, u):
            base = u * CHUNK
            r0 = pl.multiple_of(ci * CHUNK, CHUNK)
            rows = pl.ds(r0, CHUNK)
            qh, _, _, fg = _gates(q_ref[rows, :], f_ref[rows, :], lb)
            v = v_ref[rows, :]
            kk = 1.0 - fg
            cum = _hdot(tri, jnp.log(fg))
            cum_ref[pl.ds(base, CHUNK), :] = cum
            kk_ref[pl.ds(base, CHUNK), :] = kk
            vc_ref[pl.ds(base, CHUNK), :] = v
            sprev = s_ref[...]
            st_ref[0, ci] = sprev
            blast = cum_ref[pl.ds(base + CHUNK - 1, 1), :]
            o = _bdot(qh * jnp.exp(cum), sprev, NT_DIMS)
            s_ref[...] = sprev * jnp.exp(blast) + _bdot(v, kk * jnp.exp(blast - cum), TN_DIMS)
            a = None
            for (eq, ek), msk in zip(_level_factors(cum, _level_refs(cum_ref, rid, base), second), same):
                al = _bdot(qh * eq, kk * ek, NT_DIMS)
                al = al if msk is None else jnp.where(msk, al, 0.0)
                a = al if a is None else a + al
            o = o + _bdot(a, v, NN_DIMS)
            diag = []
            for sb in range(CHUNK // SUB):
                lo = sb * SUB
                qb = qh[lo:lo + SUB]
                cb = cum[lo:lo + SUB]
                od = jnp.zeros((SUB, c), F32)
                for s in range(SUB):
                    e = jnp.where(sub >= s, jnp.exp(jnp.minimum(cb - cum_ref[pl.ds(base + lo + s, 1), :], 0.0)), 0.0)
                    acol = jnp.sum(qb * e * kk_ref[pl.ds(base + lo + s, 1), :], axis=-1, keepdims=True)
                    od = od + acol * vc_ref[pl.ds(base + lo + s, 1), :]
                diag.append(od)
            o = o + jnp.concatenate(diag, axis=0)
            o_ref[rows, :] = o
            y = o * lax.rsqrt(jnp.mean(o * o, axis=-1, keepdims=True) + RMS_EPS) * gain
            og = og_ref[rows, :]
            ob_ref[rows, :] = (y * og * _sigmoid(og)).astype(BF16)

        def chunks(i, carry):
            for u in range(HGRN_UNROLL_FWD):
                chunk(i * HGRN_UNROLL_FWD + u, u)
            return carry

        lax.fori_loop(0, nc // HGRN_UNROLL_FWD, chunks, 0)

    return pl.pallas_call(
        body, name=name, grid=(N_GROUPS,),
        in_specs=hin_specs + [lbs, vec],
        out_specs=[col(0), col(N_GROUPS), st],
        out_shape=[jax.ShapeDtypeStruct((t, HGRN_WIDTH), F32), jax.ShapeDtypeStruct((t, CONV_WIDTH + HGRN_WIDTH), BF16),
                   jax.ShapeDtypeStruct((N_GROUPS, nc, c, c), F32)],
        scratch_shapes=[pltpu.VMEM((c, c), F32)] + [pltpu.VMEM((HGRN_UNROLL_FWD * CHUNK, c), F32)] * 3,
        compiler_params=_cparams(("parallel",)),
    )(hin, hin, hin, hin, lb_logits, hg)


def _hgrn_bwd(name, dcat, hin, o_raw, states, lb_logits, hg):
    t = hin.shape[0]
    nc = t // CHUNK
    c = GROUP
    col, hin_specs, vec, lbs, st = _hgrn_specs(t, nc)

    def body(do_ref, q_ref, f_ref, v_ref, og_ref, o_ref, st_ref, lb_ref, hg_ref,
             dq_ref, df_ref, dv_ref, dog_ref, dhg_ref, dlb_ref,
             ds_ref, cum_ref, kk_ref, vc_ref, qh_ref_s, do_ref_s):
        rid, tri, second, same, sub = _chunk_consts()
        trit = tri.T
        lb = _sigmoid(lb_ref[pl.ds(0, 1), :] - lb_ref[pl.ds(1, 1), :])
        gain = hg_ref[...]
        ds_ref[...] = jnp.zeros_like(ds_ref)

        def chunk(i, carry, u):
            base = u * CHUNK
            dhg, dlb = carry
            ci = nc - 1 - i
            r0 = pl.multiple_of(ci * CHUNK, CHUNK)
            rows = pl.ds(r0, CHUNK)
            q = q_ref[rows, :]
            qh, sq, sf, fg = _gates(q, f_ref[rows, :], lb)
            v = v_ref[rows, :]
            kk = 1.0 - fg
            cum = _hdot(tri, jnp.log(fg))
            cum_ref[pl.ds(base, CHUNK), :] = cum
            kk_ref[pl.ds(base, CHUNK), :] = kk
            vc_ref[pl.ds(base, CHUNK), :] = v
            qh_ref_s[pl.ds(base, CHUNK), :] = qh
            o = o_ref[rows, :]
            og = og_ref[rows, :]
            sg = _sigmoid(og)
            rinv = lax.rsqrt(jnp.mean(o * o, axis=-1, keepdims=True) + RMS_EPS)
            yn = o * rinv
            dof = do_ref[rows, :]
            dog_ref[rows, :] = (dof * yn * gain * _silu_grad(og, sg)).astype(BF16)
            dz = dof * og * sg
            dhg = dhg + jnp.sum(dz * yn, axis=0, keepdims=True)
            dy = dz * gain
            do = rinv * (dy - yn * jnp.mean(dy * yn, axis=-1, keepdims=True))
            do_ref_s[pl.ds(base, CHUNK), :] = do
            sprev = st_ref[0, ci]
            dsn = ds_ref[...]
            blast = cum_ref[pl.ds(base + CHUNK - 1, 1), :]
            eq0 = jnp.exp(cum)
            ek0 = jnp.exp(blast - cum)
            dqh = _bdot(do, sprev, NN_DIMS) * eq0
            dkk = _bdot(v, dsn, NN_DIMS) * ek0
            dlast = (jnp.sum(kk * dkk, axis=0, keepdims=True)
                     + jnp.exp(blast) * jnp.sum(dsn * sprev, axis=0, keepdims=True))
            dv = _bdot(kk * ek0, dsn, NT_DIMS)
            ds_ref[...] = dsn * jnp.exp(blast) + _bdot(do, qh * eq0, TN_DIMS)
            dg = qh * dqh - kk * dkk
            da = _bdot(do, v, NT_DIMS)
            a = None
            for (eq, ek), msk in zip(_level_factors(cum, _level_refs(cum_ref, rid, base), second), same):
                ql, kl = (qh * eq).astype(BF16), (kk * ek).astype(BF16)
                al = _bdot(ql, kl, NT_DIMS)
                dal = da
                if msk is not None:
                    al = jnp.where(msk, al, 0.0)
                    dal = jnp.where(msk, da, 0.0)
                a = al if a is None else a + al
                dql = _bdot(dal, kl, NN_DIMS)
                dkl = _bdot(dal, ql, TN_DIMS)
                dqh = dqh + dql * eq
                dkk = dkk + dkl * ek
                dg = dg + (ql.astype(F32) * dql - kl.astype(F32) * dkl)
            dv = dv + _bdot(a, do, TN_DIMS)
            dq_d, dk_d, dv_d = [], [], []
            for sb in range(CHUNK // SUB):
                lo = sb * SUB
                cb = cum[lo:lo + SUB]
                kb = kk[lo:lo + SUB]
                vb = v[lo:lo + SUB]
                dob = do[lo:lo + SUB]
                dqb = jnp.zeros((SUB, c), F32)
                dkb = jnp.zeros((SUB, c), F32)
                dvb = jnp.zeros((SUB, c), F32)
                for s in range(SUB):
                    crow = cum_ref[pl.ds(base + lo + s, 1), :]
                    e = jnp.where(sub >= s, jnp.exp(jnp.minimum(cb - crow, 0.0)), 0.0)
                    dacol = jnp.sum(dob * vc_ref[pl.ds(base + lo + s, 1), :], axis=-1, keepdims=True)
                    dqb = dqb + dacol * (kk_ref[pl.ds(base + lo + s, 1), :] * e)
                    et = jnp.where(sub <= s, jnp.exp(jnp.minimum(crow - cb, 0.0)), 0.0)
                    dorow = do_ref_s[pl.ds(base + lo + s, 1), :]
                    qe = qh_ref_s[pl.ds(base + lo + s, 1), :] * et
                    dkb = dkb + jnp.sum(vb * dorow, axis=-1, keepdims=True) * qe
                    dvb = dvb + jnp.sum(kb * qe, axis=-1, keepdims=True) * dorow
                dq_d.append(dqb)
                dk_d.append(dkb)
                dv_d.append(dvb)
            dq_d = jnp.concatenate(dq_d, axis=0)
            dk_d = jnp.concatenate(dk_d, axis=0)
            dqh = dqh + dq_d
            dkk = dkk + dk_d
            dg = dg + (qh * dq_d - kk * dk_d)
            dv = dv + jnp.concatenate(dv_d, axis=0)
            dlf = _hdot(trit, dg) + dlast
            dfg = dlf / fg - dkk
            df_ref[rows, :] = (dfg * (1.0 - lb) * sf * (1.0 - sf)).astype(BF16)
            dlb = dlb + jnp.sum(dfg * (1.0 - sf), axis=0, keepdims=True)
            dq_ref[rows, :] = (dqh * _silu_grad(q, sq)).astype(BF16)
            dv_ref[rows, :] = dv.astype(BF16)
            return dhg, dlb

        def chunks(i, carry):
            for u in range(HGRN_UNROLL):
                carry = chunk(i * HGRN_UNROLL + u, carry, u)
            return carry

        zero = jnp.zeros((1, c), F32)
        dhg, dlb = lax.fori_loop(0, nc // HGRN_UNROLL, chunks, (zero, zero))
        dhg_ref[...] = dhg
        dl0 = dlb * lb * (1.0 - lb)
        dlb_ref[...] = jnp.where(lax.broadcasted_iota(jnp.int32, (2, c), 0) == 0, dl0, -dl0)

    big = jax.ShapeDtypeStruct((t, HGRN_WIDTH), BF16)
    return pl.pallas_call(
        body, name=name, grid=(N_GROUPS,),
        in_specs=[col(8)] + hin_specs + [col(0), st, lbs, vec],
        out_specs=[col(0)] * 4 + [vec, lbs],
        out_shape=[big] * 4 + [jax.ShapeDtypeStruct((1, HGRN_WIDTH), F32), jax.ShapeDtypeStruct((2, HGRN_WIDTH), F32)],
        scratch_shapes=[pltpu.VMEM((c, c), F32)] + [pltpu.VMEM((HGRN_UNROLL * CHUNK, c), F32)] * 5,
        compiler_params=_cparams(("parallel",)),
    )(dcat, hin, hin, hin, hin, o_raw, states, lb_logits, hg)


ANY = pl.BlockSpec(memory_space=pl.ANY)


def _my_place():
    return lax.axis_index("x"), lax.axis_index("y"), lax.axis_index("c")


HBM = pl.BlockSpec(memory_space=pltpu.HBM)
SEM = pl.BlockSpec(memory_space=pltpu.SEMAPHORE)
EFFECT = pltpu.SideEffectType.DATAFLOW_SIDE_EFFECTING


def _peer(k):
    x, y, c = _my_place()
    px = 1 - x if k & 4 else x
    py = 1 - y if k & 2 else y
    pc = 1 - c if k & 1 else c
    return (px, py, pc), 4 * px + 2 * py + pc


def _slot(land_ref, idx):
    if len(land_ref.shape) == 2:
        ns = land_ref.shape[1] // N_DEV
        return land_ref.at[:, pl.ds(pl.multiple_of(idx * ns, 128), ns)]
    return land_ref.at[idx]


def _exchange_copy(k, src_ref, land_ref, send_sems, recv_sems, scatter, landing):
    x, y, c = _my_place()
    me = 4 * x + 2 * y + c
    to, idx = _peer(k)
    return pltpu.make_async_remote_copy(
        src_ref=_slot(src_ref, idx) if scatter else src_ref,
        dst_ref=_slot(land_ref, idx) if landing else _slot(land_ref, me),
        send_sem=send_sems.at[k - 1], recv_sem=recv_sems.at[k - 1], device_id=to, device_id_type=MESH)


ALL_PEERS = tuple(range(1, N_DEV))
NEAR_PEERS = (1, 2, 4, 6)
SAME_CORE_PEERS = (2, 4, 6)


def _exchange_start(name, src, land, scatter, ks=ALL_PEERS):
    def body(src_ref, land_ref, send_sems, recv_sems, src_thru, land_thru, token):
        for k in ks:
            _exchange_copy(k, src_ref, land_ref, send_sems, recv_sems, scatter, landing=False).start()
        token[...] = jnp.zeros_like(token)

    send_sems, recv_sems, src_thru, land_thru, token = pl.pallas_call(
        body, name=name,
        out_shape=(pltpu.SemaphoreType.DMA((N_DEV - 1,)), pltpu.SemaphoreType.DMA((N_DEV - 1,)),
                   pltpu.HBM(src.shape, src.dtype), pltpu.HBM(land.shape, land.dtype),
                   jax.ShapeDtypeStruct((8, 128), F32)),
        in_specs=(HBM, HBM), out_specs=(SEM, SEM, HBM, HBM, pl.BlockSpec(memory_space=pltpu.VMEM)),
        input_output_aliases={0: 2, 1: 3},
        compiler_params=pltpu.CompilerParams(has_side_effects=EFFECT),
    )(pltpu.with_memory_space_constraint(src, pltpu.HBM), pltpu.with_memory_space_constraint(land, pltpu.HBM))
    return (send_sems, recv_sems, src_thru, land_thru, scatter, ks), token


def _exchange_wait(name, handle, after):
    send_sems, recv_sems, src_thru, land_thru, scatter, ks = handle

    def body(src_ref, land_ref, send_sems, recv_sems, after_ref, src_dead, got_ref):
        for k in ks:
            cp = _exchange_copy(k, src_ref, land_ref, send_sems, recv_sems, scatter, landing=True)
            cp.wait_send()
            cp.wait_recv()

    return pl.pallas_call(
        body, name=name,
        out_shape=(pltpu.HBM(src_thru.shape, src_thru.dtype), pltpu.HBM(land_thru.shape, land_thru.dtype)),
        in_specs=(HBM, HBM, SEM, SEM, ANY), out_specs=(HBM, HBM), input_output_aliases={0: 0, 1: 1},
        compiler_params=pltpu.CompilerParams(has_side_effects=EFFECT),
    )(src_thru, land_thru, send_sems, recv_sems, after)[1]


def _relay_copy(j, land_ref, send_sems, recv_sems, landing):
    x, y, c = _my_place()
    k = SAME_CORE_PEERS[j]
    _, sent = _peer(k)
    _, got = _peer(k + 1)
    return pltpu.make_async_remote_copy(
        src_ref=_slot(land_ref, sent), dst_ref=_slot(land_ref, got) if landing else _slot(land_ref, sent),
        send_sem=send_sems.at[j], recv_sem=recv_sems.at[j], device_id=(x, y, 1 - c), device_id_type=MESH)


def _relay_start(name, land):
    n = len(SAME_CORE_PEERS)

    def body(land_ref, send_sems, recv_sems, land_thru, token):
        for j in range(n):
            _relay_copy(j, land_ref, send_sems, recv_sems, landing=False).start()
        token[...] = jnp.zeros_like(token)

    send_sems, recv_sems, land_thru, token = pl.pallas_call(
        body, name=name,
        out_shape=(pltpu.SemaphoreType.DMA((n,)), pltpu.SemaphoreType.DMA((n,)),
                   pltpu.HBM(land.shape, land.dtype), jax.ShapeDtypeStruct((8, 128), F32)),
        in_specs=(HBM,), out_specs=(SEM, SEM, HBM, pl.BlockSpec(memory_space=pltpu.VMEM)),
        input_output_aliases={0: 2},
        compiler_params=pltpu.CompilerParams(has_side_effects=EFFECT),
    )(pltpu.with_memory_space_constraint(land, pltpu.HBM))
    return (send_sems, recv_sems, land_thru), token


def _relay_wait(name, handle, after):
    send_sems, recv_sems, land_thru = handle

    def body(land_ref, send_sems, recv_sems, after_ref, got_ref):
        for j in range(len(SAME_CORE_PEERS)):
            cp = _relay_copy(j, land_ref, send_sems, recv_sems, landing=True)
            cp.wait_send()
            cp.wait_recv()

    return pl.pallas_call(
        body, name=name, out_shape=pltpu.HBM(land_thru.shape, land_thru.dtype),
        in_specs=(HBM, SEM, SEM, ANY), out_specs=HBM, input_output_aliases={0: 0},
        compiler_params=pltpu.CompilerParams(has_side_effects=EFFECT),
    )(land_thru, send_sems, recv_sems, after)


def _own_cols(name, own, me):
    r, ns = own.shape
    tr = 256

    def body(me_ref, own_ref, land_ref):
        land_ref[...] = own_ref[...]

    return pl.pallas_call(
        body, name=name,
        grid_spec=pltpu.PrefetchScalarGridSpec(
            num_scalar_prefetch=1, grid=(r // tr,),
            in_specs=[pl.BlockSpec((tr, ns), lambda i, me_ref: (i, 0))],
            out_specs=pl.BlockSpec((tr, ns), lambda i, me_ref: (i, me_ref[0]))),
        out_shape=jax.ShapeDtypeStruct((r, N_DEV * ns), own.dtype),
    )(jnp.reshape(me, (1,)).astype(jnp.int32), own)


def _own_slot(own, me):
    land = lax.empty((N_DEV,) + own.shape, own.dtype)
    return lax.dynamic_update_slice_in_dim(land, own[None], me, axis=0)


def _adamw_math(w, g, m, v):
    m = ADAM_B1 * m + (1.0 - ADAM_B1) * g
    v = ADAM_B2 * v + (1.0 - ADAM_B2) * (g * g)
    m_hat = m / (1.0 - ADAM_B1 ** ADAM_STEP)
    v_hat = v / (1.0 - ADAM_B2 ** ADAM_STEP)
    delta = -ADAM_LR * (m_hat / (jnp.sqrt(v_hat) + ADAM_EPS) + ADAM_WD * w)
    return delta, m, v


def _adamw_sum(name, recv, w, m, v, tr, row0=0, partial=None):
    r, c = w.shape
    rr = recv.shape[1]
    off = row0 // tr

    def body(recv_ref, w_ref, m_ref, v_ref, *refs):
        g_ref, d_ref, mo_ref, vo_ref = refs[-4:]
        g = recv_ref[0].astype(F32)
        for j in range(1, N_DEV):
            g = g + recv_ref[j].astype(F32)
        g_ref[...] = g
        d_ref[...], mo_ref[...], vo_ref[...] = _adamw_math(w_ref[...], g, m_ref[...], v_ref[...])

    tile = pl.BlockSpec((tr, c), lambda i: (i + off, 0))
    out = jax.ShapeDtypeStruct((r, c), F32)
    prev = list(partial) if partial is not None else []
    return pl.pallas_call(
        body, name=name, grid=(rr // tr,),
        in_specs=[pl.BlockSpec((N_DEV, tr, c), lambda i: (0, i, 0)), tile, tile, tile] + [ANY] * len(prev),
        out_specs=[tile] * 4, out_shape=[out] * 4,
        input_output_aliases={4 + i: i for i in range(len(prev))},
        compiler_params=_cparams(("parallel",)),
    )(recv, w, m, v, *prev)


def _sum_parts(name, parts):
    _, r, c = parts.shape

    def body(p_ref, o_ref):
        acc = p_ref[0]
        for j in range(1, N_DEV):
            acc = acc + p_ref[j]
        o_ref[...] = acc

    return pl.pallas_call(body, name=name, out_shape=jax.ShapeDtypeStruct((r, c), F32),
                          compiler_params=_cparams())(parts)


def _adamw_small(name, w, g, m, v):
    def body(w_ref, g_ref, m_ref, v_ref, d_ref, mo_ref, vo_ref):
        d_ref[...], mo_ref[...], vo_ref[...] = _adamw_math(w_ref[...], g_ref[...], m_ref[...], v_ref[...])

    out = jax.ShapeDtypeStruct(w.shape, F32)
    return pl.pallas_call(body, name=name, out_shape=[out] * 3, compiler_params=_cparams())(w, g, m, v)


def _pack(pieces, rows):
    flat = jnp.concatenate([p.reshape(-1).astype(F32) for p in pieces])
    return jnp.pad(flat, (0, rows * 128 - flat.shape[0])).reshape(rows, 128)


def _unpack(packed, shapes):
    flat = packed.reshape(-1)
    out, off = [], 0
    for s in shapes:
        n = 1
        for d in s:
            n *= d
        out.append(flat[off:off + n].reshape(s))
        off += n
    return out


def kernel(x, emb_ln_g, emb_ln_b, w_in, conv_w, conv_b, conv_norm_g, conv_norm_b, lb_logits, hgrn_norm_g, w_out, ln1_g, ln1_b, w_ffn_up, ffn_conv_w, ffn_conv_b, w_ffn_down, ln2_g, ln2_b, loss_target, m_emb_ln_g, m_emb_ln_b, m_w_in, m_conv_w, m_conv_b, m_conv_norm_g, m_conv_norm_b, m_lb_logits, m_hgrn_norm_g, m_w_out, m_ln1_g, m_ln1_b, m_w_ffn_up, m_ffn_conv_w, m_ffn_conv_b, m_w_ffn_down, m_ln2_g, m_ln2_b, v_emb_ln_g, v_emb_ln_b, v_w_in, v_conv_w, v_conv_b, v_conv_norm_g, v_conv_norm_b, v_lb_logits, v_hgrn_norm_g, v_w_out, v_ln1_g, v_ln1_b, v_w_ffn_up, v_ffn_conv_w, v_ffn_conv_b, v_w_ffn_down, v_ln2_g, v_ln2_b):
    t = x.shape[1]
    me = 4 * lax.axis_index("x") + 2 * lax.axis_index("y") + lax.axis_index("c")
    x2, tgt = x[0], loss_target[0]
    ns_in, ns_up = w_in.shape[2], w_ffn_up.shape[2]
    rs_out, rs_down = w_out.shape[1], w_ffn_down.shape[1]
    cs, fs = conv_w.shape[2], ffn_conv_w.shape[2]

    def gather_start(name, w, prev, ks=ALL_PEERS, cols=False):
        shard = (w[0] + prev).astype(BF16)
        land = _own_cols(name.replace("ag_", "own_"), shard, me) if cols else _own_slot(shard, me)
        return _exchange_start(name, shard, land, scatter=False, ks=ks)

    h_in, tok = gather_start("ag_w_in_start", w_in, 0.0, NEAR_PEERS, cols=True)
    taps = _pack([conv_w[0], ffn_conv_w[0]], 48) + tok[0, 0]
    h_taps, tok = _exchange_start("ag_taps_start", taps, _own_slot(taps, me), scatter=False)
    h_out, tok = gather_start("ag_w_out_start", w_out, tok[0, 0])
    h_up, tok = gather_start("ag_w_up_start", w_ffn_up, tok[0, 0], NEAR_PEERS, cols=True)
    h_down, tok = gather_start("ag_w_down_start", w_ffn_down, tok[0, 0])

    row = lambda a: a.reshape(1, -1)

    _, h0b, h0bt = _ln_fwd("ln_in", x2, None, row(emb_ln_g) + tok[0, 0], row(emb_ln_b), 1.0)
    h_relay, tok_relay = _relay_start("ag_w_in_relay_start", _exchange_wait("ag_w_in_wait", h_in, h0b))
    win_n = _relay_wait("ag_w_in_relay_wait", h_relay, tok_relay)
    hin = _mm_nn("mm_in", h0b, win_n, F32, tm=2048, tn=ns_in, tk=D_MODEL)
    n_cw, n_fw = CONV_KERNEL * cs, FFN_KERNEL * fs
    taps_g = _exchange_wait("ag_taps_wait", h_taps, hin).reshape(N_DEV, -1)
    cw_full = taps_g[:, :n_cw].reshape(N_DEV, CONV_KERNEL, cs).transpose(1, 0, 2).reshape(CONV_KERNEL, CONV_WIDTH)
    fw_full = taps_g[:, n_cw:n_cw + n_fw].reshape(N_DEV, FFN_KERNEL, fs).transpose(1, 0, 2).reshape(FFN_KERNEL, D_FF)

    o_raw, cat_right, states = _hgrn_fwd("hgrn_fwd", hin, lb_logits, hgrn_norm_g)
    u1, catb = _conv_fwd("conv_fwd", hin, cw_full, conv_b, conv_norm_g, conv_norm_b, cat_right)
    wout_g = _exchange_wait("ag_w_out_wait", h_out, catb).reshape(D_MODEL, D_MODEL)
    h_up_relay, tok = _relay_start("ag_w_up_relay_start", _exchange_wait("ag_w_up_wait", h_up, wout_g))
    mix = _mm_nn("mm_out", catb, wout_g, F32, tm=2048, tn=1024, tk=D_MODEL, after=tok)
    r1, h1b, h1bt = _ln_fwd("ln1", x2, mix, ln1_g, ln1_b, ALPHA, pre=(row(emb_ln_g), row(emb_ln_b)))
    wup_n = _relay_wait("ag_w_up_relay_wait", h_up_relay, h1b)
    hf = _mm_nn("mm_up", h1b, wup_n, BF16, tm=1024, tn=1024, tk=D_MODEL)
    actb = _ffn_act_fwd("ffn_act", hf, fw_full, ffn_conv_b)
    wdown_g = _exchange_wait("ag_w_down_wait", h_down, actb).reshape(D_FF, D_MODEL)
    ffn = _mm_nn("mm_down", actb, wdown_g, F32, tm=1024, tn=512, tk=D_FF)
    dr2, dr2b, g_ln2g, g_ln2b, loss = _ln2_loss_bwd("ln2_loss", r1, ln1_g, ln1_b, ffn, ln2_g, ln2_b, tgt)

    def scatter_start(name, parts):
        if parts.ndim == 2:
            ns = parts.shape[1] // N_DEV
            own = lax.dynamic_slice_in_dim(parts, me * ns, ns, axis=1)
        else:
            own = lax.dynamic_index_in_dim(parts, me, axis=0, keepdims=False)
        return _exchange_start(name, parts, _own_slot(own, me), scatter=True)

    dact = _mm_nt("mm_dact", dr2b, wdown_g, BF16, tm=1024, tn=D_FF // 2, tk=D_MODEL)
    gw_down = _matmul(
        "mm_dw_down", actb, dr2b, (D_FF, D_MODEL), BF16, (N_DEV // 2, D_MODEL // 1024, 2),
        pl.BlockSpec((t // 2, 2 * rs_down), lambda i, j, kk: (kk, i)),
        pl.BlockSpec((t // 2, 1024), lambda i, j, kk: (kk, j)),
        pl.BlockSpec((2 * rs_down, 1024), lambda i, j, kk: (i, j)), nt="tn")
    s_down, tok = scatter_start("a2a_w_down_start", gw_down.reshape(N_DEV, rs_down, D_MODEL))
    dhf, g_fw, g_fb = _ffn_act_bwd("ffn_act_bwd", dact, hf, fw_full, ffn_conv_b + tok[0, 0])
    tm = min(1024, t)
    gw_up = _matmul(
        "mm_dw_up", h1bt, dhf, (D_MODEL, 2 * D_FF), BF16, (D_MODEL // 1024, 2 * D_FF // 512, 1),
        pl.BlockSpec((1024, t), lambda i, j, kk: (i, 0)),
        pl.BlockSpec((1, t, 512), lambda i, j, kk: (j // 11, 0, j % 11)),
        pl.BlockSpec((1024, 512), lambda i, j, kk: (i, j)), nt=False)
    s_up, tok = scatter_start("a2a_w_up_start", gw_up)
    tkf = D_FF // 2
    dh1 = _matmul(
        "mm_dh1", dhf, wup_n, (t, D_MODEL), F32, (t // tm, D_MODEL // 1024, 4),
        pl.BlockSpec((1, tm, tkf), lambda i, j, kk: (kk // 2, i, kk % 2)),
        pl.BlockSpec((1024, tkf), lambda i, j, kk: (j, kk)),
        pl.BlockSpec((tm, 1024), lambda i, j, kk: (i, j)), nt=True, after=tok)
    dr1, dr1b, g_ln1g, g_ln1b = _ln_bwd("ln1_bwd", r1, dr2, dh1, ln1_g + tok[0, 0], ALPHA, True)
    gw_out = _matmul(
        "mm_dw_out", catb, dr1b, (D_MODEL, D_MODEL), BF16, (2, 2, 2),
        pl.BlockSpec((t // 2, 1024), lambda i, j, kk: (kk, i)),
        pl.BlockSpec((t // 2, 1024), lambda i, j, kk: (kk, j)),
        pl.BlockSpec((1024, 1024), lambda i, j, kk: (i, j)), nt="tn")
    s_out, tok = scatter_start("a2a_w_out_start", gw_out.reshape(N_DEV, rs_out, D_MODEL))
    dcat = _mm_nt("mm_dcat", dr1b, wout_g, F32, tm=2048, tn=1024, tk=D_MODEL, after=tok)
    da, dgate, g_cw, g_cb, g_cng, g_cnb = _conv_bwd("conv_bwd", dcat, u1, hin, cw_full, conv_norm_g + tok[0, 0],
                                                    conv_norm_b)
    dq, df, di, dog, g_hg, g_lb = _hgrn_bwd("hgrn_bwd", dcat, hin, o_raw, states, lb_logits, hgrn_norm_g)
    dhin = jnp.concatenate([da, dgate, dq, df, di, dog], axis=1)
    half = D_MODEL // 2
    gw_in_a = _mm_grad_cols("mm_dw_in_a", h0bt, dhin, ns_in, 0, half, after=tok)
    s_in_a, tok = scatter_start("a2a_w_in_a_start", gw_in_a)
    gw_in_b = _mm_grad_cols("mm_dw_in_b", h0bt, dhin, ns_in, half, half, after=tok)
    s_in_b, tok = scatter_start("a2a_w_in_b_start", gw_in_b)
    dh0 = _mm_nt("mm_dh0", dhin, win_n, F32, tm=1024, tn=512, tk=IN_PROJ, after=tok)
    grad_x, g_eg, g_eb = _ln_bwd("ln_in_bwd", x2, dr1, dh0, row(emb_ln_g), ALPHA, False)

    small_shapes = [(D_MODEL,), (D_MODEL,), (CONV_KERNEL, CONV_WIDTH), (1, CONV_WIDTH), (1, CONV_WIDTH),
                    (1, CONV_WIDTH), (2, HGRN_WIDTH), (1, HGRN_WIDTH), (1, D_MODEL), (1, D_MODEL),
                    (FFN_KERNEL, D_FF), (1, D_FF), (1, D_MODEL), (1, D_MODEL), (128,)]
    rows_small = 569
    packed = _pack([g_eg, g_eb, g_cw[:CONV_KERNEL], g_cb, g_cng, g_cnb, g_lb, g_hg, g_ln1g, g_ln1b,
                    g_fw[:FFN_KERNEL], g_fb, g_ln2g, g_ln2b, loss], rows_small)
    h_small, tok = _exchange_start("ag_small_start", packed, _own_slot(packed, me), scatter=False)

    def big(name, handle, after, w, m, v, tr):
        recv = _exchange_wait("a2a_" + name + "_wait", handle, after)
        return [o[None] for o in _adamw_sum("adamw_" + name, recv, w[0], m[0], v[0], tr)]

    u_down = big("w_down", s_down, tok, w_ffn_down, m_w_ffn_down, v_w_ffn_down, 64)
    u_up = big("w_up", s_up, u_down[1], w_ffn_up, m_w_ffn_up, v_w_ffn_up, 64)
    u_out = big("w_out", s_out, u_up[1], w_out, m_w_out, v_w_out, 64)
    summed = _sum_parts("sum_small", _exchange_wait("ag_small_wait", h_small, u_out[1]))
    (s_eg, s_eb, s_cw, s_cb, s_cng, s_cnb, s_lb, s_hg, s_l1g, s_l1b, s_fw, s_fb, s_l2g, s_l2b,
     s_loss) = _unpack(summed, small_shapes)
    s_cw = lax.dynamic_slice_in_dim(s_cw, me * cs, cs, axis=1)[None]
    s_fw = lax.dynamic_slice_in_dim(s_fw, me * fs, fs, axis=1)[None]
    g_small = [s_eg, s_eb, s_cw, s_cb, s_cng, s_cnb, s_lb, s_hg, s_l1g, s_l1b, s_fw, s_fb, s_l2g, s_l2b]
    w_small = [emb_ln_g, emb_ln_b, conv_w, conv_b, conv_norm_g, conv_norm_b, lb_logits, hgrn_norm_g,
               ln1_g, ln1_b, ffn_conv_w, ffn_conv_b, ln2_g, ln2_b]
    m_small = [m_emb_ln_g, m_emb_ln_b, m_conv_w, m_conv_b, m_conv_norm_g, m_conv_norm_b, m_lb_logits,
               m_hgrn_norm_g, m_ln1_g, m_ln1_b, m_ffn_conv_w, m_ffn_conv_b, m_ln2_g, m_ln2_b]
    v_small = [v_emb_ln_g, v_emb_ln_b, v_conv_w, v_conv_b, v_conv_norm_g, v_conv_norm_b, v_lb_logits,
               v_hgrn_norm_g, v_ln1_g, v_ln1_b, v_ffn_conv_w, v_ffn_conv_b, v_ln2_g, v_ln2_b]
    rows_own = 236
    shapes_own = [w.shape for w in w_small]
    upd = _adamw_small("adamw_small", _pack(w_small, rows_own), _pack(g_small, rows_own),
                       _pack(m_small, rows_own), _pack(v_small, rows_own))
    d_small, nm_small, nv_small = (_unpack(u, shapes_own) for u in upd)
    g_small = [g.reshape(s) for g, s in zip(g_small, shapes_own)]

    recv_a = _exchange_wait("a2a_w_in_a_wait", s_in_a, upd[0])
    part = _adamw_sum("adamw_w_in_a", recv_a, w_in[0], m_w_in[0], v_w_in[0], 128)
    recv_b = _exchange_wait("a2a_w_in_b_wait", s_in_b, part[1])
    u_in = [o[None] for o in _adamw_sum("adamw_w_in_b", recv_b, w_in[0], m_w_in[0], v_w_in[0], 128,
                                        row0=half, partial=part)]

    def ordered(small, i_in, i_out, i_up, i_down):
        (eg, eb, cw, cb, cng, cnb, lb, hg, l1g, l1b, fw, fb, l2g, l2b) = small
        return [eg, eb, i_in, cw, cb, cng, cnb, lb, hg, i_out, l1g, l1b, i_up, fw, fb, i_down, l2g, l2b]

    outs = [s_loss[0], grad_x[None]]
    for k, small in enumerate([g_small, d_small, nm_small, nv_small]):
        outs += ordered(small, u_in[k], u_out[k], u_up[k], u_down[k])
    return tuple(outs)
```
